```python
import jax
import jax.numpy as jnp
from jax import lax
import numpy as np

D_MODEL = 1024
BATCH = 8
SEQ = 4096
DEPTH = 2

GRID_W = 64
CTX_LEN = 256
N_MOD = 9
D_FF = 2816
RMS_EPS = 1e-6
A_HEADS = 8
A_KV_HEADS = 2
A_HEAD_DIM = 64
WINDOW = 128
BLOCK = WINDOW
ROPE_BASE = 10000.0
B_HEADS = 4
B_DK = 64
B_DV = 128
B_GATE_RANK = 16
B_GATE_NORM = 16.0
B_CHUNK = 64
POOL_WINDOWS = (2, 4, 8, 16)
POOL_GROUP = D_MODEL // len(POOL_WINDOWS)
A_Q = A_HEADS * A_HEAD_DIM
A_KV = A_KV_HEADS * A_HEAD_DIM
B_QK = B_HEADS * B_DK
B_V = B_HEADS * B_DV
PROJ_SIZES = (A_Q, A_KV, A_KV, B_QK, B_QK, B_V, B_V, 2 * B_GATE_RANK)
PROJ_DIM = A_Q + 2 * A_KV + 2 * B_QK + 2 * B_V + 2 * B_GATE_RANK
MIX_OUT = A_Q + B_V

kernel_name = "hybrid_swa_gla_pool_prefix_dit"


def rmsnorm(x, g):
    x32 = x.astype(jnp.float32)
    y = x32 * lax.rsqrt(jnp.mean(x32 * x32, axis=-1, keepdims=True) + RMS_EPS)
    return y.astype(x.dtype) * g


def adaln(cond, w, b):
    mm = jax.nn.silu(cond) @ w + b
    mm = mm.reshape(mm.shape[:-1] + (N_MOD, mm.shape[-1] // N_MOD))
    return [mm[..., i, None, :] for i in range(N_MOD)]


def modulate(z, g, shift, scale):
    return rmsnorm(z, g) * (1.0 + scale) + shift


def swiglu(h, wi, wo):
    a, u = jnp.split(h @ wi, 2, axis=-1)
    return (jax.nn.silu(a) * u) @ wo


def _rotate(x, pos):
    n = x.shape[-1] // 2
    freqs = ROPE_BASE ** (-jnp.arange(n, dtype=jnp.float32) / n)
    ang = pos[:, None] * freqs
    cos = jnp.cos(ang)[:, None, :].astype(x.dtype)
    sin = jnp.sin(ang)[:, None, :].astype(x.dtype)
    x1, x2 = x[..., :n], x[..., n:]
    return jnp.concatenate([x1 * cos - x2 * sin, x2 * cos + x1 * sin], axis=-1)


def axial_rope(x, rows, cols):
    half = x.shape[-1] // 2
    return jnp.concatenate([_rotate(x[..., :half], rows), _rotate(x[..., half:], cols)], axis=-1)


def window_attention(q, k, v, kc, vc, sink):
    B, T, Hq, d = q.shape
    G = k.shape[2]
    R = Hq // G
    nb = T // BLOCK
    L = kc.shape[1]
    scale = d ** -0.5
    qb = q.reshape(B, nb, BLOCK, G, R, d)

    def band(a):
        ap = jnp.pad(a, ((0, 0), (BLOCK, BLOCK), (0, 0), (0, 0))).reshape(B, nb + 2, BLOCK, G, d)
        return jnp.concatenate([ap[:, :nb], ap[:, 1:nb + 1], ap[:, 2:]], axis=2)

    kb, vb = band(k), band(v)
    s_band = jnp.einsum('bnigrd,bnjgd->bgrnij', qb, kb).astype(jnp.float32) * scale
    s_ctx = jnp.einsum('bnigrd,blgd->bgrnil', qb, kc).astype(jnp.float32) * scale
    qpos = jnp.arange(nb)[:, None, None] * BLOCK + jnp.arange(BLOCK)[None, :, None]
    kpos = jnp.arange(nb)[:, None, None] * BLOCK - BLOCK + jnp.arange(3 * BLOCK)[None, None, :]
    valid = (kpos >= 0) & (kpos < T) & (jnp.abs(kpos - qpos) <= WINDOW)
    s_band = jnp.where(valid, s_band, -jnp.inf)
    sink_l = jnp.broadcast_to(sink.astype(jnp.float32).reshape(1, G, R, 1, 1, 1), s_band.shape[:-1] + (1,))
    p = jax.nn.softmax(jnp.concatenate([s_band, s_ctx, sink_l], axis=-1), axis=-1)
    nk = 3 * BLOCK
    o = (jnp.einsum('bgrnij,bnjgd->bnigrd', p[..., :nk].astype(vb.dtype), vb)
         + jnp.einsum('bgrnil,blgd->bnigrd', p[..., nk:nk + L].astype(vc.dtype), vc))
    return o.reshape(B, T, Hq * d)


def context_attention(qc, kc, vc, sink):
    B, L, Hq, d = qc.shape
    G = kc.shape[2]
    R = Hq // G
    s = jnp.einsum('blgrd,bmgd->bgrlm', qc.reshape(B, L, G, R, d), kc).astype(jnp.float32) * d ** -0.5
    sink_l = jnp.broadcast_to(sink.astype(jnp.float32).reshape(1, G, R, 1, 1), s.shape[:-1] + (1,))
    p = jax.nn.softmax(jnp.concatenate([s, sink_l], axis=-1), axis=-1)
    o = jnp.einsum('bgrlm,bmgd->blgrd', p[..., :L].astype(vc.dtype), vc)
    return o.reshape(B, L, Hq * d)


def gla_chunked(q, k, v, log_a, s0):
    B, T, H, dk = q.shape
    C = B_CHUNK
    n = T // C
    f32 = jnp.float32

    def chunks(a):
        return a.astype(f32).reshape(B, n, C, H, a.shape[-1])

    qc_ = chunks(q) * dk ** -0.5
    kc_ = chunks(k)
    vc_ = chunks(v)
    g = jnp.cumsum(chunks(log_a), axis=2)
    g_last = g[:, :, -1:]
    q_t = qc_ * jnp.exp(g)
    k_t = kc_ * jnp.exp(-g)
    k_end = kc_ * jnp.exp(g_last - g)
    lower = jnp.tril(jnp.ones((C, C), dtype=bool))
    att = jnp.where(lower, jnp.einsum('bnihd,bnjhd->bnhij', q_t, k_t), 0.0)
    o = jnp.einsum('bnhij,bnjhv->bnihv', att, vc_)
    d_state = jnp.einsum('bnjhd,bnjhv->bnhdv', k_end, vc_)
    decay = jnp.exp(g_last[:, :, 0])

    def step(S, inp):
        dec, ds = inp
        return dec[..., None] * S + ds, S

    s_final, s_prev = lax.scan(step, s0, (jnp.moveaxis(decay, 1, 0), jnp.moveaxis(d_state, 1, 0)))
    s_prev = jnp.moveaxis(s_prev, 0, 1)
    o = o + jnp.einsum('bnihd,bnhdv->bnihv', q_t, s_prev)
    return o.reshape(B, T, H, v.shape[-1]).astype(q.dtype), s_final


def bidir_gla(q, k, v, la_f, la_b, qc, kc, vc, lac_f, lac_b):
    flip = lambda a: a[:, ::-1]
    B = q.shape[0]
    zeros = jnp.zeros((B, B_HEADS, B_DK, B_DV), jnp.float32)
    oc_f, sc_f = gla_chunked(qc, kc, vc, lac_f, zeros)
    oc_b, sc_b = gla_chunked(flip(qc), flip(kc), flip(vc), flip(lac_b), zeros)
    o_f, _ = gla_chunked(q, k, v, la_f, sc_f)
    o_b, _ = gla_chunked(flip(q), flip(k), flip(v), flip(la_b), sc_b)
    return o_f + flip(o_b), oc_f + flip(oc_b)


def gla_output(o, r, gla_g):
    B, T = o.shape[:2]
    return rmsnorm(o, gla_g).reshape(B, T, B_V) * jax.nn.silu(r)


def mixer_ab(h, hc, rows, cols, need_ctx_out, w_in, w_a2_f, b_a_f, w_a2_b, b_a_b, sink, gla_g, w_out):
    split_points = [int(s) for s in np.cumsum(PROJ_SIZES)[:-1]]

    def project(z):
        Bz, Tz = z.shape[:2]
        qa, ka, va, qb, kb, vb, rb, zg = jnp.split(z @ w_in, split_points, axis=-1)
        la_f = jax.nn.log_sigmoid((zg[..., :B_GATE_RANK] @ w_a2_f + b_a_f).astype(jnp.float32)) / B_GATE_NORM
        la_b = jax.nn.log_sigmoid((zg[..., B_GATE_RANK:] @ w_a2_b + b_a_b).astype(jnp.float32)) / B_GATE_NORM
        return (qa.reshape(Bz, Tz, A_HEADS, A_HEAD_DIM),
                ka.reshape(Bz, Tz, A_KV_HEADS, A_HEAD_DIM),
                va.reshape(Bz, Tz, A_KV_HEADS, A_HEAD_DIM),
                qb.reshape(Bz, Tz, B_HEADS, B_DK),
                kb.reshape(Bz, Tz, B_HEADS, B_DK),
                vb.reshape(Bz, Tz, B_HEADS, B_DV),
                rb,
                la_f.reshape(Bz, Tz, B_HEADS, B_DK),
                la_b.reshape(Bz, Tz, B_HEADS, B_DK))

    qa, ka, va, qb, kb, vb, rb, la_f, la_b = project(h)
    cqa, cka, cva, cqb, ckb, cvb, crb, cla_f, cla_b = project(hc)
    o_a = window_attention(axial_rope(qa, rows, cols), axial_rope(ka, rows, cols), va, cka, cva, sink)
    o_b, oc_b = bidir_gla(qb, kb, vb, la_f, la_b, cqb, ckb, cvb, cla_f, cla_b)
    y = jnp.concatenate([o_a, gla_output(o_b, rb, gla_g)], axis=-1) @ w_out
    yc = None
    if need_ctx_out:
        oc_a = context_attention(cqa, cka, cva, sink)
        yc = jnp.concatenate([oc_a, gla_output(oc_b, crb, gla_g)], axis=-1) @ w_out
    return y, yc


def pool_mixer(h, w_pool, pool_scale):
    B, T, D = h.shape
    ng = len(POOL_WINDOWS)
    hg = h.astype(jnp.float32).reshape(B, T, ng, POOL_GROUP)
    prefix = jnp.pad(jnp.cumsum(hg, axis=1), ((0, 0), (1, 0), (0, 0), (0, 0)))
    t = jnp.arange(T)
    means = []
    for gi, w in enumerate(POOL_WINDOWS):
        lo = jnp.maximum(t - w // 2, 0)
        hi = jnp.minimum(t + (w - w // 2), T)
        total = prefix[:, hi, gi] - prefix[:, lo, gi]
        means.append(total / (hi - lo).astype(jnp.float32)[:, None])
    pooled = (jnp.stack(means, axis=2) - hg).astype(h.dtype)
    y = jnp.einsum('btgc,gce->btge', pooled, w_pool)
    return y.reshape(B, T, D) * pool_scale


def _fwd_setup_inputs(seed: int = 0) -> dict:
    key = jax.random.key(seed)
    ks = jax.random.split(key, 24)
    D = D_MODEL
    ne = (DEPTH + 1) // 2
    no = DEPTH // 2

    def nrm(k, shape, scale=1.0):
        return jax.random.normal(k, shape, jnp.float32) * scale

    return {
        "x": nrm(ks[0], (BATCH, SEQ, D)),
        "c": nrm(ks[1], (BATCH, D)),
        "ctx": nrm(ks[2], (BATCH, CTX_LEN, D)),
        "c_ctx": nrm(ks[3], (D,)),
        "w_mod": nrm(ks[4], (DEPTH, D, N_MOD * D), 0.5 * D ** -0.5),
        "b_mod": nrm(ks[5], (DEPTH, N_MOD * D), 0.01),
        "norm_g": 1.0 + nrm(ks[6], (DEPTH, 3, D), 0.05),
        "ffn1_wi": nrm(ks[7], (DEPTH, D, 2 * D_FF), D ** -0.5),
        "ffn1_wo": nrm(ks[8], (DEPTH, D_FF, D), D_FF ** -0.5),
        "ffn2_wi": nrm(ks[9], (DEPTH, D, 2 * D_FF), D ** -0.5),
        "ffn2_wo": nrm(ks[10], (DEPTH, D_FF, D), D_FF ** -0.5),
        "w_in": nrm(ks[11], (ne, D, PROJ_DIM), D ** -0.5),
        "w_a2_f": nrm(ks[12], (ne, B_GATE_RANK, B_QK), B_GATE_RANK ** -0.5),
        "b_a_f": nrm(ks[13], (ne, B_QK), 0.1),
        "w_a2_b": nrm(ks[14], (ne, B_GATE_RANK, B_QK), B_GATE_RANK ** -0.5),
        "b_a_b": nrm(ks[15], (ne, B_QK), 0.1),
        "sink": nrm(ks[16], (ne, A_HEADS), 1.0),
        "gla_g": 1.0 + nrm(ks[17], (ne, B_DV), 0.05),
        "w_out": nrm(ks[18], (ne, MIX_OUT, D), MIX_OUT ** -0.5),
        "w_pool": nrm(ks[19], (no, len(POOL_WINDOWS), POOL_GROUP, POOL_GROUP), POOL_GROUP ** -0.5),
        "pool_scale": 1.0 + nrm(ks[20], (no, D), 0.1),
        "final_g": 1.0 + nrm(ks[21], (D,), 0.05),
    }


def _fwd_reference(x, c, ctx, c_ctx, w_mod, b_mod, norm_g, ffn1_wi, ffn1_wo, ffn2_wi, ffn2_wo,
              w_in, w_a2_f, b_a_f, w_a2_b, b_a_b, sink, gla_g, w_out, w_pool, pool_scale, final_g):
    T = x.shape[1]
    ROWS = T // GRID_W
    rows = jnp.repeat(jnp.arange(ROWS, dtype=jnp.float32), GRID_W)
    cols = jnp.tile(jnp.arange(GRID_W, dtype=jnp.float32), ROWS)

    for l in range(DEPTH):
        even = l % 2 == 0
        ctx_out = any(j % 2 == 0 for j in range(l + 1, DEPTH))
        ctx_in = even or ctx_out
        m = adaln(c, w_mod[l], b_mod[l])
        mc = adaln(c_ctx, w_mod[l], b_mod[l]) if ctx_in else None

        x = x + 0.5 * m[2] * swiglu(modulate(x, norm_g[l, 0], m[0], m[1]), ffn1_wi[l], ffn1_wo[l])
        if ctx_in:
            ctx = ctx + 0.5 * mc[2] * swiglu(modulate(ctx, norm_g[l, 0], mc[0], mc[1]), ffn1_wi[l], ffn1_wo[l])

        h = modulate(x, norm_g[l, 1], m[3], m[4])
        if even:
            e = l // 2
            hc = modulate(ctx, norm_g[l, 1], mc[3], mc[4])
            y, yc = mixer_ab(h, hc, rows, cols, ctx_out, w_in[e], w_a2_f[e], b_a_f[e], w_a2_b[e], b_a_b[e],
                             sink[e], gla_g[e], w_out[e])
        else:
            o = l // 2
            y = pool_mixer(h, w_pool[o], pool_scale[o])
            yc = pool_mixer(modulate(ctx, norm_g[l, 1], mc[3], mc[4]), w_pool[o], pool_scale[o]) if ctx_out else None
        x = x + m[5] * y
        if ctx_out:
            ctx = ctx + mc[5] * yc

        x = x + 0.5 * m[8] * swiglu(modulate(x, norm_g[l, 2], m[6], m[7]), ffn2_wi[l], ffn2_wo[l])
        if ctx_out:
            ctx = ctx + 0.5 * mc[8] * swiglu(modulate(ctx, norm_g[l, 2], mc[6], mc[7]), ffn2_wi[l], ffn2_wo[l])

    return rmsnorm(x, final_g)


import jax as _jax
import jax.numpy as _jnp

TWIN_FORMAT = 'train_step'
FWD_PARAMS = ['x', 'c', 'ctx', 'c_ctx', 'w_mod', 'b_mod', 'norm_g', 'ffn1_wi', 'ffn1_wo', 'ffn2_wi', 'ffn2_wo', 'w_in', 'w_a2_f', 'b_a_f', 'w_a2_b', 'b_a_b', 'sink', 'gla_g', 'w_out', 'w_pool', 'pool_scale', 'final_g']
TWIN_WEIGHTS = ['c_ctx', 'w_mod', 'b_mod', 'norm_g', 'ffn1_wi', 'ffn1_wo', 'ffn2_wi', 'ffn2_wo', 'w_in', 'w_a2_f', 'b_a_f', 'w_a2_b', 'b_a_b', 'sink', 'gla_g', 'w_out', 'w_pool', 'pool_scale', 'final_g']
TWIN_DIFF_INPUT = 'x'
TWIN_INPUTS = ['x', 'c', 'ctx', 'c_ctx', 'w_mod', 'b_mod', 'norm_g', 'ffn1_wi', 'ffn1_wo', 'ffn2_wi', 'ffn2_wo', 'w_in', 'w_a2_f', 'b_a_f', 'w_a2_b', 'b_a_b', 'sink', 'gla_g', 'w_out', 'w_pool', 'pool_scale', 'final_g', 'loss_target', 'm_c_ctx', 'm_w_mod', 'm_b_mod', 'm_norm_g', 'm_ffn1_wi', 'm_ffn1_wo', 'm_ffn2_wi', 'm_ffn2_wo', 'm_w_in', 'm_w_a2_f', 'm_b_a_f', 'm_w_a2_b', 'm_b_a_b', 'm_sink', 'm_gla_g', 'm_w_out', 'm_w_pool', 'm_pool_scale', 'm_final_g', 'v_c_ctx', 'v_w_mod', 'v_b_mod', 'v_norm_g', 'v_ffn1_wi', 'v_ffn1_wo', 'v_ffn2_wi', 'v_ffn2_wo', 'v_w_in', 'v_w_a2_f', 'v_b_a_f', 'v_w_a2_b', 'v_b_a_b', 'v_sink', 'v_gla_g', 'v_w_out', 'v_w_pool', 'v_pool_scale', 'v_final_g']
TWIN_OUTPUTS = ['loss', 'grad_x', 'grad_c_ctx', 'grad_w_mod', 'grad_b_mod', 'grad_norm_g', 'grad_ffn1_wi', 'grad_ffn1_wo', 'grad_ffn2_wi', 'grad_ffn2_wo', 'grad_w_in', 'grad_w_a2_f', 'grad_b_a_f', 'grad_w_a2_b', 'grad_b_a_b', 'grad_sink', 'grad_gla_g', 'grad_w_out', 'grad_w_pool', 'grad_pool_scale', 'grad_final_g', 'delta_c_ctx', 'delta_w_mod', 'delta_b_mod', 'delta_norm_g', 'delta_ffn1_wi', 'delta_ffn1_wo', 'delta_ffn2_wi', 'delta_ffn2_wo', 'delta_w_in', 'delta_w_a2_f', 'delta_b_a_f', 'delta_w_a2_b', 'delta_b_a_b', 'delta_sink', 'delta_gla_g', 'delta_w_out', 'delta_w_pool', 'delta_pool_scale', 'delta_final_g', 'new_m_c_ctx', 'new_m_w_mod', 'new_m_b_mod', 'new_m_norm_g', 'new_m_ffn1_wi', 'new_m_ffn1_wo', 'new_m_ffn2_wi', 'new_m_ffn2_wo', 'new_m_w_in', 'new_m_w_a2_f', 'new_m_b_a_f', 'new_m_w_a2_b', 'new_m_b_a_b', 'new_m_sink', 'new_m_gla_g', 'new_m_w_out', 'new_m_w_pool', 'new_m_pool_scale', 'new_m_final_g', 'new_v_c_ctx', 'new_v_w_mod', 'new_v_b_mod', 'new_v_norm_g', 'new_v_ffn1_wi', 'new_v_ffn1_wo', 'new_v_ffn2_wi', 'new_v_ffn2_wo', 'new_v_w_in', 'new_v_w_a2_f', 'new_v_b_a_f', 'new_v_w_a2_b', 'new_v_b_a_b', 'new_v_sink', 'new_v_gla_g', 'new_v_w_out', 'new_v_w_pool', 'new_v_pool_scale', 'new_v_final_g']
TWIN_LEAF_KINDS = {'loss': 'loss', 'grad_x': 'grad_x', 'grad_c_ctx': 'grad_w', 'grad_w_mod': 'grad_w', 'grad_b_mod': 'grad_w', 'grad_norm_g': 'grad_w', 'grad_ffn1_wi': 'grad_w', 'grad_ffn1_wo': 'grad_w', 'grad_ffn2_wi': 'grad_w', 'grad_ffn2_wo': 'grad_w', 'grad_w_in': 'grad_w', 'grad_w_a2_f': 'grad_w', 'grad_b_a_f': 'grad_w', 'grad_w_a2_b': 'grad_w', 'grad_b_a_b': 'grad_w', 'grad_sink': 'grad_w', 'grad_gla_g': 'grad_w', 'grad_w_out': 'grad_w', 'grad_w_pool': 'grad_w', 'grad_pool_scale': 'grad_w', 'grad_final_g': 'grad_w', 'delta_c_ctx': 'delta_w', 'delta_w_mod': 'delta_w', 'delta_b_mod': 'delta_w', 'delta_norm_g': 'delta_w', 'delta_ffn1_wi': 'delta_w', 'delta_ffn1_wo': 'delta_w', 'delta_ffn2_wi': 'delta_w', 'delta_ffn2_wo': 'delta_w', 'delta_w_in': 'delta_w', 'delta_w_a2_f': 'delta_w', 'delta_b_a_f': 'delta_w', 'delta_w_a2_b': 'delta_w', 'delta_b_a_b': 'delta_w', 'delta_sink': 'delta_w', 'delta_gla_g': 'delta_w', 'delta_w_out': 'delta_w', 'delta_w_pool': 'delta_w', 'delta_pool_scale': 'delta_w', 'delta_final_g': 'delta_w', 'new_m_c_ctx': 'new_m', 'new_m_w_mod': 'new_m', 'new_m_b_mod': 'new_m', 'new_m_norm_g': 'new_m', 'new_m_ffn1_wi': 'new_m', 'new_m_ffn1_wo': 'new_m', 'new_m_ffn2_wi': 'new_m', 'new_m_ffn2_wo': 'new_m', 'new_m_w_in': 'new_m', 'new_m_w_a2_f': 'new_m', 'new_m_b_a_f': 'new_m', 'new_m_w_a2_b': 'new_m', 'new_m_b_a_b': 'new_m', 'new_m_sink': 'new_m', 'new_m_gla_g': 'new_m', 'new_m_w_out': 'new_m', 'new_m_w_pool': 'new_m', 'new_m_pool_scale': 'new_m', 'new_m_final_g': 'new_m', 'new_v_c_ctx': 'new_v', 'new_v_w_mod': 'new_v', 'new_v_b_mod': 'new_v', 'new_v_norm_g': 'new_v', 'new_v_ffn1_wi': 'new_v', 'new_v_ffn1_wo': 'new_v', 'new_v_ffn2_wi': 'new_v', 'new_v_ffn2_wo': 'new_v', 'new_v_w_in': 'new_v', 'new_v_w_a2_f': 'new_v', 'new_v_b_a_f': 'new_v', 'new_v_w_a2_b': 'new_v', 'new_v_b_a_b': 'new_v', 'new_v_sink': 'new_v', 'new_v_gla_g': 'new_v', 'new_v_w_out': 'new_v', 'new_v_w_pool': 'new_v', 'new_v_pool_scale': 'new_v', 'new_v_final_g': 'new_v'}


def _forward(args):
    return _fwd_reference(*[args[k] for k in FWD_PARAMS])


def _output_shape():
    def fwd():
        inp = _fwd_setup_inputs(0)
        return _fwd_reference(*[inp[k] for k in FWD_PARAMS])
    out = _jax.eval_shape(fwd)
    return out.shape, out.dtype

N_MICROBATCH = 1
ADAM_LR = 0.001
ADAM_B1 = 0.9
ADAM_B2 = 0.999
ADAM_EPS = 1e-08
ADAM_WD = 0.01
ADAM_STEP = 10
PER_EXAMPLE_BATCH_AXIS = {'x': 0, 'c': 0, 'ctx': 0, 'loss_target': 0}
SHARED_INPUTS = []
_WEIGHT_DTYPES = {'c_ctx': _jnp.float32, 'w_mod': _jnp.float32, 'b_mod': _jnp.float32, 'norm_g': _jnp.float32, 'ffn1_wi': _jnp.float32, 'ffn1_wo': _jnp.float32, 'ffn2_wi': _jnp.float32, 'ffn2_wo': _jnp.float32, 'w_in': _jnp.float32, 'w_a2_f': _jnp.float32, 'b_a_f': _jnp.float32, 'w_a2_b': _jnp.float32, 'b_a_b': _jnp.float32, 'sink': _jnp.float32, 'gla_g': _jnp.float32, 'w_out': _jnp.float32, 'w_pool': _jnp.float32, 'pool_scale': _jnp.float32, 'final_g': _jnp.float32}
MOMENT_SCALE = {'c_ctx': 7.749033e-03, 'w_mod': 4.863642e-02, 'b_mod': 9.975536e-02, 'norm_g': 3.637726e-02, 'ffn1_wi': 1.200454e-02, 'ffn1_wo': 1.961623e-02, 'ffn2_wi': 1.135358e-02, 'ffn2_wo': 1.849467e-02, 'w_in': 3.540389e-02, 'w_a2_f': 5.439169e-03, 'b_a_f': 1.650284e-02, 'w_a2_b': 5.974633e-03, 'b_a_b': 1.670049e-02, 'sink': 2.054436e-04, 'gla_g': 7.277918e-02, 'w_out': 2.681082e-02, 'w_pool': 5.070413e-02, 'pool_scale': 2.436110e-01, 'final_g': 3.214608e+01}


def _to_microbatches(a, axis):
    t = _jnp.moveaxis(a, axis, 0)
    t = t.reshape((N_MICROBATCH, t.shape[0] // N_MICROBATCH) + t.shape[1:])
    return _jnp.moveaxis(t, 1, axis + 1)


def setup_inputs(seed: int = 0) -> dict:
    inp = _fwd_setup_inputs(seed)
    key = _jax.random.fold_in(_jax.random.key(seed), 7919)
    shape, _ = _output_shape()
    out = dict(inp)
    out["loss_target"] = _jax.random.normal(_jax.random.fold_in(key, 0), shape, _jnp.float32)
    for i, name in enumerate(TWIN_WEIGHTS):
        w = inp[name].astype(_jnp.float32)
        if MOMENT_SCALE is None:
            s = _jnp.sqrt(_jnp.mean(_jnp.square(w)) + 1e-30)
        else:
            s = MOMENT_SCALE[name]
        km, kv = _jax.random.split(_jax.random.fold_in(key, i + 1))
        out[name] = w
        out["m_" + name] = s * _jax.random.normal(km, w.shape, _jnp.float32)
        out["v_" + name] = (s * s) * _jax.random.uniform(kv, w.shape, _jnp.float32, 0.5, 1.5)
    if N_MICROBATCH > 1:
        for name, axis in PER_EXAMPLE_BATCH_AXIS.items():
            out[name] = _to_microbatches(out[name], axis)
    return {'x': out['x'], 'c': out['c'], 'ctx': out['ctx'], 'c_ctx': out['c_ctx'], 'w_mod': out['w_mod'], 'b_mod': out['b_mod'], 'norm_g': out['norm_g'], 'ffn1_wi': out['ffn1_wi'], 'ffn1_wo': out['ffn1_wo'], 'ffn2_wi': out['ffn2_wi'], 'ffn2_wo': out['ffn2_wo'], 'w_in': out['w_in'], 'w_a2_f': out['w_a2_f'], 'b_a_f': out['b_a_f'], 'w_a2_b': out['w_a2_b'], 'b_a_b': out['b_a_b'], 'sink': out['sink'], 'gla_g': out['gla_g'], 'w_out': out['w_out'], 'w_pool': out['w_pool'], 'pool_scale': out['pool_scale'], 'final_g': out['final_g'], 'loss_target': out['loss_target'], 'm_c_ctx': out['m_c_ctx'], 'm_w_mod': out['m_w_mod'], 'm_b_mod': out['m_b_mod'], 'm_norm_g': out['m_norm_g'], 'm_ffn1_wi': out['m_ffn1_wi'], 'm_ffn1_wo': out['m_ffn1_wo'], 'm_ffn2_wi': out['m_ffn2_wi'], 'm_ffn2_wo': out['m_ffn2_wo'], 'm_w_in': out['m_w_in'], 'm_w_a2_f': out['m_w_a2_f'], 'm_b_a_f': out['m_b_a_f'], 'm_w_a2_b': out['m_w_a2_b'], 'm_b_a_b': out['m_b_a_b'], 'm_sink': out['m_sink'], 'm_gla_g': out['m_gla_g'], 'm_w_out': out['m_w_out'], 'm_w_pool': out['m_w_pool'], 'm_pool_scale': out['m_pool_scale'], 'm_final_g': out['m_final_g'], 'v_c_ctx': out['v_c_ctx'], 'v_w_mod': out['v_w_mod'], 'v_b_mod': out['v_b_mod'], 'v_norm_g': out['v_norm_g'], 'v_ffn1_wi': out['v_ffn1_wi'], 'v_ffn1_wo': out['v_ffn1_wo'], 'v_ffn2_wi': out['v_ffn2_wi'], 'v_ffn2_wo': out['v_ffn2_wo'], 'v_w_in': out['v_w_in'], 'v_w_a2_f': out['v_w_a2_f'], 'v_b_a_f': out['v_b_a_f'], 'v_w_a2_b': out['v_w_a2_b'], 'v_b_a_b': out['v_b_a_b'], 'v_sink': out['v_sink'], 'v_gla_g': out['v_gla_g'], 'v_w_out': out['v_w_out'], 'v_w_pool': out['v_w_pool'], 'v_pool_scale': out['v_pool_scale'], 'v_final_g': out['v_final_g']}


def _loss(weights, diff, rest, loss_target):
    with _jax.named_scope("forward"):
        args = {**rest, TWIN_DIFF_INPUT: diff, **{k: w.astype(_WEIGHT_DTYPES[k]) for k, w in weights.items()}}
        y = _forward(args)
    with _jax.named_scope("loss_head"):
        err = _jnp.square(y.astype(_jnp.float32) - loss_target)
        return 0.5 * _jnp.sum(_jnp.mean(err, axis=-1)) if err.ndim else 0.5 * err


def _adamw(w, g, m, v):
    m = ADAM_B1 * m + (1.0 - ADAM_B1) * g
    v = ADAM_B2 * v + (1.0 - ADAM_B2) * _jnp.square(g)
    m_hat = m / (1.0 - ADAM_B1 ** ADAM_STEP)
    v_hat = v / (1.0 - ADAM_B2 ** ADAM_STEP)
    delta = -ADAM_LR * (m_hat / (_jnp.sqrt(v_hat) + ADAM_EPS) + ADAM_WD * w)
    return delta, m, v


def reference(x, c, ctx, c_ctx, w_mod, b_mod, norm_g, ffn1_wi, ffn1_wo, ffn2_wi, ffn2_wo, w_in, w_a2_f, b_a_f, w_a2_b, b_a_b, sink, gla_g, w_out, w_pool, pool_scale, final_g, loss_target, m_c_ctx, m_w_mod, m_b_mod, m_norm_g, m_ffn1_wi, m_ffn1_wo, m_ffn2_wi, m_ffn2_wo, m_w_in, m_w_a2_f, m_b_a_f, m_w_a2_b, m_b_a_b, m_sink, m_gla_g, m_w_out, m_w_pool, m_pool_scale, m_final_g, v_c_ctx, v_w_mod, v_b_mod, v_norm_g, v_ffn1_wi, v_ffn1_wo, v_ffn2_wi, v_ffn2_wo, v_w_in, v_w_a2_f, v_b_a_f, v_w_a2_b, v_b_a_b, v_sink, v_gla_g, v_w_out, v_w_pool, v_pool_scale, v_final_g):
    given = dict(x=x, c=c, ctx=ctx, c_ctx=c_ctx, w_mod=w_mod, b_mod=b_mod, norm_g=norm_g, ffn1_wi=ffn1_wi, ffn1_wo=ffn1_wo, ffn2_wi=ffn2_wi, ffn2_wo=ffn2_wo, w_in=w_in, w_a2_f=w_a2_f, b_a_f=b_a_f, w_a2_b=w_a2_b, b_a_b=b_a_b, sink=sink, gla_g=gla_g, w_out=w_out, w_pool=w_pool, pool_scale=pool_scale, final_g=final_g, loss_target=loss_target, m_c_ctx=m_c_ctx, m_w_mod=m_w_mod, m_b_mod=m_b_mod, m_norm_g=m_norm_g, m_ffn1_wi=m_ffn1_wi, m_ffn1_wo=m_ffn1_wo, m_ffn2_wi=m_ffn2_wi, m_ffn2_wo=m_ffn2_wo, m_w_in=m_w_in, m_w_a2_f=m_w_a2_f, m_b_a_f=m_b_a_f, m_w_a2_b=m_w_a2_b, m_b_a_b=m_b_a_b, m_sink=m_sink, m_gla_g=m_gla_g, m_w_out=m_w_out, m_w_pool=m_w_pool, m_pool_scale=m_pool_scale, m_final_g=m_final_g, v_c_ctx=v_c_ctx, v_w_mod=v_w_mod, v_b_mod=v_b_mod, v_norm_g=v_norm_g, v_ffn1_wi=v_ffn1_wi, v_ffn1_wo=v_ffn1_wo, v_ffn2_wi=v_ffn2_wi, v_ffn2_wo=v_ffn2_wo, v_w_in=v_w_in, v_w_a2_f=v_w_a2_f, v_b_a_f=v_b_a_f, v_w_a2_b=v_w_a2_b, v_b_a_b=v_b_a_b, v_sink=v_sink, v_gla_g=v_gla_g, v_w_out=v_w_out, v_w_pool=v_w_pool, v_pool_scale=v_pool_scale, v_final_g=v_final_g)
    weights = {n: given[n] for n in TWIN_WEIGHTS}
    shared = {n: given[n] for n in SHARED_INPUTS}
    per_example = {n: given[n] for n in ['x', 'c', 'ctx']}
    grad_fn = _jax.value_and_grad(_loss, argnums=(0, 1))

    def one_microbatch(ex, loss_target):
        ex = dict(ex)
        diff = ex.pop(TWIN_DIFF_INPUT)
        return grad_fn(weights, diff, {**shared, **ex}, loss_target)

    if N_MICROBATCH == 1:
        loss, (grad_w, grad_x) = one_microbatch(per_example, given["loss_target"])
    else:
        def body(carry, xs):
            loss_sum, grad_sum = carry
            l_k, (gw_k, gx_k) = one_microbatch(xs[0], xs[1])
            with _jax.named_scope("update"):
                return (loss_sum + l_k, _jax.tree.map(_jnp.add, grad_sum, gw_k)), gx_k

        init = (_jnp.zeros((), _jnp.float32), _jax.tree.map(_jnp.zeros_like, weights))
        (loss, grad_w), grad_x = _jax.lax.scan(body, init, (per_example, given["loss_target"]))
    with _jax.named_scope("update"):
        delta_w, new_m, new_v = {}, {}, {}
        for n in TWIN_WEIGHTS:
            delta_w[n], new_m[n], new_v[n] = _adamw(weights[n], grad_w[n], given["m_" + n], given["v_" + n])
    return (loss, grad_x, *[grad_w[n] for n in TWIN_WEIGHTS], *[delta_w[n] for n in TWIN_WEIGHTS],
            *[new_m[n] for n in TWIN_WEIGHTS], *[new_v[n] for n in TWIN_WEIGHTS])
```

```python
import functools

import jax
import jax.numpy as jnp
import numpy as np
from jax import lax
from jax.experimental import pallas as pl
from jax.experimental.pallas import tpu as pltpu

F32 = jnp.float32
BF16 = jnp.bfloat16

D_MODEL = 1024
N_MOD = 9
D_FF = 2816
RMS_EPS = 1e-6
A_HEADS = 8
A_KV_HEADS = 2
A_HEAD_DIM = 64
WINDOW = 128
ROPE_BASE = 10000.0
GRID_W = 64
B_HEADS = 4
B_DK = 64
B_DV = 128
B_GATE_RANK = 16
B_GATE_NORM = 16.0
B_CHUNK = 64
POOL_WINDOWS = (2, 4, 8, 16)
POOL_GROUP = D_MODEL // len(POOL_WINDOWS)
PROJ_DIM = 2336

ADAM_LR = 0.001
ADAM_B1 = 0.9
ADAM_B2 = 0.999
ADAM_EPS = 1e-08
ADAM_WD = 0.01
ADAM_STEP = 10

N_CHIPS = 4
N_DEV = 8
ROW_TILE = 512
VMEM_LIMIT_BYTES = 56 * 1024 * 1024
MESH = pl.DeviceIdType.MESH

ZC_Q, ZC_QK, ZC_V, ZC_R, ZC_KV, ZC_G, ZC_W = 0, 512, 1024, 1536, 2048, 2304, 2432


def _cp(*sem):
    return pltpu.CompilerParams(dimension_semantics=sem if sem else None, vmem_limit_bytes=VMEM_LIMIT_BYTES)


def _dot(a, b):
    return jnp.dot(a, b, preferred_element_type=F32)


def _dot_nt(a, b):
    return lax.dot_general(a, b, (((1,), (1,)), ((), ())), preferred_element_type=F32)


def _dot_tn(a, b):
    return lax.dot_general(a, b, (((0,), (0,)), ((), ())), preferred_element_type=F32)


def _dot_hi(a, b):
    return jnp.dot(a, b, preferred_element_type=F32, precision=lax.Precision.HIGHEST)


def _dot_tn_hi(a, b):
    return lax.dot_general(a, b, (((0,), (0,)), ((), ())), preferred_element_type=F32,
                           precision=lax.Precision.HIGHEST)


def _sigmoid(x):
    return 1.0 / (1.0 + jnp.exp(-x))


def _stream_of(i, n_x):
    return jnp.where(i >= n_x, 1, 0)


def _rms_mod_fwd(x, g3, mods, j, n_x, out_dtype, name):
    rows = x.shape[0]
    tm = ROW_TILE
    n_i = rows // tm

    def body(x_ref, g_ref, m_ref, o_ref):
        xv = x_ref[...]
        r = lax.rsqrt(jnp.mean(xv * xv, axis=-1, keepdims=True) + RMS_EPS)
        g = g_ref[j:j + 1, :]
        shift = m_ref[0, 3 * j:3 * j + 1, :]
        scale = m_ref[0, 3 * j + 1:3 * j + 2, :]
        o_ref[...] = (((xv * r) * g) * (1.0 + scale) + shift).astype(out_dtype)

    return pl.pallas_call(
        body, name=name, grid=(n_i,),
        in_specs=[pl.BlockSpec((tm, D_MODEL), lambda i: (i, 0)),
                  pl.BlockSpec((3, D_MODEL), lambda i: (0, 0)),
                  pl.BlockSpec((1, N_MOD, D_MODEL), lambda i: (_stream_of(i, n_x), 0, 0))],
        out_specs=pl.BlockSpec((tm, D_MODEL), lambda i: (i, 0)),
        out_shape=jax.ShapeDtypeStruct((rows, D_MODEL), out_dtype),
        compiler_params=_cp("parallel"),
    )(x, g3, mods)


def _rms_mod_bwd_tail(dh, xv, g, scale, stream, acc_ref, first):
    r = lax.rsqrt(jnp.mean(xv * xv, axis=-1, keepdims=True) + RMS_EPS)
    xhat = xv * r
    t1 = jnp.sum(dh, axis=0, keepdims=True)
    t2 = jnp.sum(dh * xhat, axis=0, keepdims=True)
    stats = jnp.concatenate([t1, t2 * g, t2 * (1.0 + scale)], axis=0)

    @pl.when(first)
    def _():
        acc_ref[...] = jnp.zeros_like(acc_ref)

    acc_ref[pl.ds(stream, 1)] += stats[None]
    dxh = dh * (g * (1.0 + scale))
    return r * (dxh - xhat * jnp.mean(dxh * xhat, axis=-1, keepdims=True))


def _ffn_up(hn, w4, name):
    rows = hn.shape[0]
    h = w4.shape[2]
    tm = ROW_TILE
    n_i = rows // tm

    def body(h_ref, wa_ref, wu_ref, au_ref, s_ref):
        hv = h_ref[...]
        a = _dot(hv, wa_ref[0])
        u = _dot(hv, wu_ref[0])
        au_ref[0] = a.astype(BF16)
        au_ref[1] = u.astype(BF16)
        s_ref[...] = (a * _sigmoid(a) * u).astype(BF16)

    return pl.pallas_call(
        body, name=name, grid=(2, n_i),
        in_specs=[pl.BlockSpec((tm, D_MODEL), lambda j, i: (i, 0)),
                  pl.BlockSpec((1, D_MODEL, h), lambda j, i: (j, 0, 0)),
                  pl.BlockSpec((1, D_MODEL, h), lambda j, i: (j + 2, 0, 0))],
        out_specs=[pl.BlockSpec((2, tm, h), lambda j, i: (0, i, j)),
                   pl.BlockSpec((tm, h), lambda j, i: (i, j))],
        out_shape=[jax.ShapeDtypeStruct((2, rows, 2 * h), BF16),
                   jax.ShapeDtypeStruct((rows, 2 * h), BF16)],
        compiler_params=_cp("arbitrary", "arbitrary"),
    )(hn, w4, w4)


def _matmul_resid(a, w, xres, mods, gate_idx, coef, n_x, rows, name):
    k = a.shape[1]
    tm = ROW_TILE
    n_i = rows // tm

    def body(a_ref, w_ref, x_ref, m_ref, o_ref, f_ref):
        f = _dot(a_ref[...], w_ref[...])
        gate = m_ref[0, gate_idx:gate_idx + 1, :]
        f_ref[...] = f
        o_ref[...] = x_ref[...] + (coef * gate) * f

    return pl.pallas_call(
        body, name=name, grid=(n_i,),
        in_specs=[pl.BlockSpec((tm, k), lambda i: (i, 0)),
                  pl.BlockSpec((k, D_MODEL), lambda i: (0, 0)),
                  pl.BlockSpec((tm, D_MODEL), lambda i: (i, 0)),
                  pl.BlockSpec((1, N_MOD, D_MODEL), lambda i: (_stream_of(i, n_x), 0, 0))],
        out_specs=[pl.BlockSpec((tm, D_MODEL), lambda i: (i, 0)),
                   pl.BlockSpec((tm, D_MODEL), lambda i: (i, 0))],
        out_shape=[jax.ShapeDtypeStruct((rows, D_MODEL), F32),
                   jax.ShapeDtypeStruct((rows, D_MODEL), F32)],
        compiler_params=_cp("parallel"),
    )(a, w, xres, mods)


def _gate_dy(dout, f, mods, gate_idx, coef, n_x, rows, name):
    tm = ROW_TILE
    n_i = rows // tm

    def body(d_ref, f_ref, m_ref, dy_ref, acc_ref):
        i = pl.program_id(0)
        dv = d_ref[...]
        gate = m_ref[0, gate_idx:gate_idx + 1, :]
        dy_ref[...] = (dv * (coef * gate)).astype(BF16)

        @pl.when(i == 0)
        def _():
            acc_ref[...] = jnp.zeros_like(acc_ref)

        part = coef * jnp.sum(dv * f_ref[...], axis=0, keepdims=True)
        acc_ref[pl.ds(_stream_of(i, n_x), 1)] += part[None]

    return pl.pallas_call(
        body, name=name, grid=(n_i,),
        in_specs=[pl.BlockSpec((tm, D_MODEL), lambda i: (i, 0)),
                  pl.BlockSpec((tm, D_MODEL), lambda i: (i, 0)),
                  pl.BlockSpec((1, N_MOD, D_MODEL), lambda i: (_stream_of(i, n_x), 0, 0))],
        out_specs=[pl.BlockSpec((tm, D_MODEL), lambda i: (i, 0)),
                   pl.BlockSpec((2, 1, D_MODEL), lambda i: (0, 0, 0))],
        out_shape=[jax.ShapeDtypeStruct((rows, D_MODEL), BF16),
                   jax.ShapeDtypeStruct((2, 1, D_MODEL), F32)],
        compiler_params=_cp("arbitrary"),
    )(dout, f, mods)


def _ffn_bwd_dz(dy, wo2, au, name):
    rows = dy.shape[0]
    h = wo2.shape[1]
    tm = ROW_TILE
    n_i = rows // tm

    def body(dy_ref, wo_ref, au_ref, dz_ref):
        ds = _dot_nt(dy_ref[...], wo_ref[0])
        a = au_ref[0].astype(F32)
        u = au_ref[1].astype(F32)
        sg = _sigmoid(a)
        dz_ref[0] = (ds * u * (sg * (1.0 + a * (1.0 - sg)))).astype(BF16)
        dz_ref[1] = (ds * (a * sg)).astype(BF16)

    return pl.pallas_call(
        body, name=name, grid=(2, n_i),
        in_specs=[pl.BlockSpec((tm, D_MODEL), lambda j, i: (i, 0)),
                  pl.BlockSpec((1, h, D_MODEL), lambda j, i: (j, 0, 0)),
                  pl.BlockSpec((2, tm, h), lambda j, i: (0, i, j))],
        out_specs=pl.BlockSpec((2, tm, h), lambda j, i: (0, i, j)),
        out_shape=jax.ShapeDtypeStruct((2, rows, 2 * h), BF16),
        compiler_params=_cp("arbitrary", "arbitrary"),
    )(dy, wo2, au)


def _matmul_tn(a, b, a_spec, b_spec, out_shape, out_spec, grid, name):
    nd_a = len(a_spec.block_shape)
    nd_b = len(b_spec.block_shape)
    nd_o = len(out_spec.block_shape)
    k_axis = len(grid) - 1

    def body(a_ref, b_ref, o_ref):
        av = a_ref[(0,) * (nd_a - 2)]
        bv = b_ref[(0,) * (nd_b - 2)]
        part = _dot_tn(av, bv)

        @pl.when(pl.program_id(k_axis) == 0)
        def _():
            o_ref[...] = jnp.zeros_like(o_ref)

        o_ref[(0,) * (nd_o - 2)] += part

    return pl.pallas_call(
        body, name=name, grid=grid, in_specs=[a_spec, b_spec], out_specs=out_spec,
        out_shape=jax.ShapeDtypeStruct(out_shape, F32),
        compiler_params=_cp(*(("arbitrary",) * len(grid))),
    )(a, b)


def _bwd_dx(pairs, x, dres, dres_tiles, g3, mods, j, n_x, name):
    rows = x.shape[0]
    tm = ROW_TILE
    n_i = rows // tm
    n_p = len(pairs)
    nds = [(len(p[1].block_shape), len(p[3].block_shape)) for p in pairs]

    def body(*refs):
        dz_refs = refs[0:2 * n_p:2]
        w_refs = refs[1:2 * n_p:2]
        x_ref, dres_ref, g_ref, m_ref, dx_ref, acc_ref = refs[2 * n_p:]
        i = pl.program_id(0)
        dh = None
        for p in range(n_p):
            dzv = dz_refs[p][(0,) * (nds[p][0] - 2)]
            wv = w_refs[p][(0,) * (nds[p][1] - 2)]
            part = _dot_nt(dzv, wv)
            dh = part if dh is None else dh + part
        g = g_ref[j:j + 1, :]
        scale = m_ref[0, 3 * j + 1:3 * j + 2, :]
        dx = _rms_mod_bwd_tail(dh, x_ref[...], g, scale, _stream_of(i, n_x), acc_ref, i == 0)
        dres_v = jnp.where(i < dres_tiles, dres_ref[...], 0.0)
        dx_ref[...] = dres_v + dx

    in_specs, args = [], []
    for dz, dz_spec, w, w_spec in pairs:
        in_specs += [dz_spec, w_spec]
        args += [dz, w]
    in_specs += [pl.BlockSpec((tm, D_MODEL), lambda i: (i, 0)),
                 pl.BlockSpec((tm, D_MODEL), lambda i: (jnp.minimum(i, dres_tiles - 1), 0)),
                 pl.BlockSpec((3, D_MODEL), lambda i: (0, 0)),
                 pl.BlockSpec((1, N_MOD, D_MODEL), lambda i: (_stream_of(i, n_x), 0, 0))]
    args += [x, dres, g3, mods]
    return pl.pallas_call(
        body, name=name, grid=(n_i,), in_specs=in_specs,
        out_specs=[pl.BlockSpec((tm, D_MODEL), lambda i: (i, 0)),
                   pl.BlockSpec((2, 3, D_MODEL), lambda i: (0, 0, 0))],
        out_shape=[jax.ShapeDtypeStruct((rows, D_MODEL), F32),
                   jax.ShapeDtypeStruct((2, 3, D_MODEL), F32)],
        compiler_params=_cp("arbitrary"),
    )(*args)


def _ffn_forward(x, g3, mods, j, w4_in, w4_out, n_x, name):
    rows = x.shape[0]
    hn = _rms_mod_fwd(x, g3, mods, j, n_x, BF16, name + "_mod")
    au, s = _ffn_up(hn, w4_in, name + "_up")
    wo = w4_out.reshape(D_FF, D_MODEL)
    out, f = _matmul_resid(s, wo, x, mods, 3 * j + 2, 0.5, n_x, rows, name + "_down")
    return out, (x, hn, au, s, f)


def _ffn_backward(dout, saved, g3, mods, j, w4_in, w4_out, n_x, name):
    x, hn, au, s, f = saved
    rows = x.shape[0]
    tm = ROW_TILE
    n_i = rows // tm
    h = w4_in.shape[2]
    dy, dgate = _gate_dy(dout, f, mods, 3 * j + 2, 0.5, n_x, rows, name + "_dy")
    wo2 = w4_out.reshape(2, h, D_MODEL)
    dz = _ffn_bwd_dz(dy, wo2, au, name + "_dz")
    d_wo = _matmul_tn(
        s, dy, pl.BlockSpec((tm, h), lambda n, k: (k, n)), pl.BlockSpec((tm, D_MODEL), lambda n, k: (k, 0)),
        (D_FF, D_MODEL), pl.BlockSpec((h, D_MODEL), lambda n, k: (n, 0)), (2, n_i), name + "_dwo")
    d_wi = _matmul_tn(
        hn, dz, pl.BlockSpec((tm, D_MODEL), lambda q, k: (k, 0)),
        pl.BlockSpec((1, tm, h), lambda q, k: (q // 2, k, q % 2)),
        (4, D_MODEL, h), pl.BlockSpec((1, D_MODEL, h), lambda q, k: (q, 0, 0)), (4, n_i), name + "_dwi")
    pairs = [(dz, pl.BlockSpec((1, tm, h), functools.partial(lambda q, i: (q // 2, i, q % 2), q)),
              w4_in, pl.BlockSpec((1, D_MODEL, h), functools.partial(lambda q, i: (q, 0, 0), q)))
             for q in range(4)]
    dx, stats = _bwd_dx(pairs, x, dout, n_i, g3, mods, j, n_x, name + "_dx")
    return dx, d_wi, d_wo.reshape(w4_out.shape), stats, dgate


def _matmul_nt(a, w, name):
    rows, k = a.shape
    n = w.shape[0]
    tm = ROW_TILE

    def body(a_ref, w_ref, o_ref):
        o_ref[...] = _dot_nt(a_ref[...], w_ref[...])

    return pl.pallas_call(
        body, name=name, grid=(rows // tm,),
        in_specs=[pl.BlockSpec((tm, k), lambda i: (i, 0)), pl.BlockSpec((n, k), lambda i: (0, 0))],
        out_specs=pl.BlockSpec((tm, n), lambda i: (i, 0)),
        out_shape=jax.ShapeDtypeStruct((rows, n), F32),
        compiler_params=_cp("parallel"),
    )(a, w)


def _rope_tables(t_len, rows):
    n = A_HEAD_DIM // 4
    freqs = ROPE_BASE ** (-jnp.arange(n, dtype=F32) / n)
    t = jnp.arange(t_len)
    ang_r = (t // GRID_W).astype(F32)[:, None] * freqs
    ang_c = (t % GRID_W).astype(F32)[:, None] * freqs
    cos = jnp.concatenate([jnp.cos(ang_r), jnp.cos(ang_r), jnp.cos(ang_c), jnp.cos(ang_c)], axis=1)
    sin = jnp.concatenate([-jnp.sin(ang_r), jnp.sin(ang_r), -jnp.sin(ang_c), jnp.sin(ang_c)], axis=1)
    cos = jnp.concatenate([cos, jnp.ones((rows - t_len, A_HEAD_DIM), F32)], axis=0)
    sin = jnp.concatenate([sin, jnp.zeros((rows - t_len, A_HEAD_DIM), F32)], axis=0)
    return jnp.concatenate([cos, cos, sin, sin], axis=1)


def _swap16(x):
    n = x.shape[1]
    lane = lax.broadcasted_iota(jnp.int32, x.shape, 1)
    first = jnp.bitwise_and(lane, 16) == 0
    return jnp.where(first, pltpu.roll(x, n - 16, 1), pltpu.roll(x, 16, 1))


def _log_sigmoid(x):
    return jnp.minimum(x, 0.0) - jnp.log(1.0 + jnp.exp(-jnp.abs(x)))


def _proj_fwd(h, wcat, wg2, bias2, cs, name):
    rows = h.shape[0]
    tm = ROW_TILE

    def body(h_ref, w_ref, wg_ref, b_ref, cs_ref, zc_ref, la_ref):
        z = _dot(h_ref[...], w_ref[...])
        cos = cs_ref[:, 0:128]
        sin = cs_ref[:, 128:256]
        cosq = jnp.concatenate([cos] * 4, axis=1)
        sinq = jnp.concatenate([sin] * 4, axis=1)
        q = z[:, ZC_Q:ZC_QK]
        zc_ref[:, ZC_Q:ZC_QK] = q * cosq + _swap16(q) * sinq
        zc_ref[:, ZC_QK:ZC_KV] = z[:, ZC_QK:ZC_KV]
        kk = z[:, ZC_KV:ZC_KV + 128]
        zc_ref[:, ZC_KV:ZC_KV + 128] = kk * cos + _swap16(kk) * sin
        zc_ref[:, ZC_KV + 128:ZC_W] = z[:, ZC_KV + 128:ZC_W]
        zg = z[:, ZC_G:ZC_W]
        pre = _dot(zg.astype(BF16), wg_ref[...]) + b_ref[...]
        la_ref[...] = _log_sigmoid(pre) / B_GATE_NORM

    return pl.pallas_call(
        body, name=name, grid=(rows // tm,),
        in_specs=[pl.BlockSpec((tm, D_MODEL), lambda i: (i, 0)),
                  pl.BlockSpec((D_MODEL, ZC_W), lambda i: (0, 0)),
                  pl.BlockSpec((128, 512), lambda i: (0, 0)),
                  pl.BlockSpec((1, 512), lambda i: (0, 0)),
                  pl.BlockSpec((tm, 256), lambda i: (i, 0))],
        out_specs=[pl.BlockSpec((tm, ZC_W), lambda i: (i, 0)),
                   pl.BlockSpec((tm, 512), lambda i: (i, 0))],
        out_shape=[jax.ShapeDtypeStruct((rows, ZC_W), F32),
                   jax.ShapeDtypeStruct((rows, 512), F32)],
        compiler_params=_cp("parallel"),
    )(h, wcat, wg2, bias2, cs)


_QB = WINDOW


def _attn_specs(t_len, l_ctx):
    nb = t_len // _QB
    kvb = ZC_KV // 256
    return [pl.BlockSpec(memory_space=pltpu.SMEM),
            pl.BlockSpec((_QB, 512), lambda n: (n, 0)),
            pl.BlockSpec((_QB, 256), lambda n: (jnp.maximum(n - 1, 0), kvb)),
            pl.BlockSpec((_QB, 256), lambda n: (n, kvb)),
            pl.BlockSpec((_QB, 256), lambda n: (n + 1, kvb)),
            pl.BlockSpec((l_ctx, 256), lambda n: (t_len // l_ctx, kvb))], nb


def _attn_probs(n, t_len, sink_ref, qv, kp, kc, kn, kx, g):
    hd = A_HEAD_DIM
    ks = slice(g * hd, (g + 1) * hd)
    vs = slice(128 + g * hd, 128 + (g + 1) * hd)
    kb = jnp.concatenate([kp[:, ks], kc[:, ks], kn[:, ks]], axis=0).astype(BF16)
    vb = jnp.concatenate([kp[:, vs], kc[:, vs], kn[:, vs]], axis=0).astype(BF16)
    kxb = kx[:, ks].astype(BF16)
    vxb = kx[:, vs].astype(BF16)
    qg = jnp.concatenate([qv[:, (4 * g + r) * hd:(4 * g + r + 1) * hd] for r in range(4)], axis=0).astype(BF16)
    qi = lax.broadcasted_iota(jnp.int32, (_QB, 3 * _QB), 0)
    kj = lax.broadcasted_iota(jnp.int32, (_QB, 3 * _QB), 1)
    kpos = n * _QB - _QB + kj
    valid = (kpos >= 0) & (kpos < t_len) & (jnp.abs(kj - _QB - qi) <= WINDOW)
    valid4 = jnp.concatenate([valid] * 4, axis=0)
    scale = hd ** -0.5
    s = jnp.where(valid4, _dot_nt(qg, kb) * scale, -jnp.inf)
    sc = _dot_nt(qg, kxb) * scale
    sk = jnp.concatenate([jnp.full((_QB, 1), sink_ref[4 * g + r], F32) for r in range(4)], axis=0)
    m = jnp.maximum(jnp.maximum(jnp.max(s, axis=-1, keepdims=True), jnp.max(sc, axis=-1, keepdims=True)), sk)
    p = jnp.exp(s - m)
    pc = jnp.exp(sc - m)
    ps = jnp.exp(sk - m)
    inv = 1.0 / (jnp.sum(p, axis=-1, keepdims=True) + jnp.sum(pc, axis=-1, keepdims=True) + ps)
    return p * inv, pc * inv, ps * inv, qg, kb, vb, kxb, vxb


def _attn_fwd(zc, sink, t_len, l_ctx, name):
    in_specs, nb = _attn_specs(t_len, l_ctx)

    def body(sink_ref, q_ref, kp_ref, kc_ref, kn_ref, kx_ref, o_ref):
        n = pl.program_id(0)
        outs = []
        for g in range(A_KV_HEADS):
            p, pc, _, _, _, vb, _, vxb = _attn_probs(
                n, t_len, sink_ref, q_ref[...], kp_ref[...], kc_ref[...], kn_ref[...], kx_ref[...], g)
            o = _dot(p.astype(BF16), vb) + _dot(pc.astype(BF16), vxb)
            outs += [o[r * _QB:(r + 1) * _QB] for r in range(4)]
        o_ref[...] = jnp.concatenate(outs, axis=1)

    return pl.pallas_call(
        body, name=name, grid=(nb,), in_specs=in_specs,
        out_specs=pl.BlockSpec((_QB, 512), lambda n: (n, 0)),
        out_shape=jax.ShapeDtypeStruct((t_len, 512), F32),
        compiler_params=_cp("parallel"),
    )(sink, zc, zc, zc, zc, zc)


def _attn_bwd(zc, sink, o, dcat, t_len, l_ctx, name):
    rows = zc.shape[0]
    in_specs, nb = _attn_specs(t_len, l_ctx)
    in_specs = in_specs + [pl.BlockSpec((_QB, 512), lambda n: (n, 0)), pl.BlockSpec((_QB, 512), lambda n: (n, 0))]
    hd = A_HEAD_DIM
    scale = hd ** -0.5

    def body(sink_ref, q_ref, kp_ref, kc_ref, kn_ref, kx_ref, o_ref, do_ref, dq_ref, dkv_ref, dsink_ref):
        n = pl.program_id(0)

        @pl.when(n == 0)
        def _():
            dkv_ref[...] = jnp.zeros_like(dkv_ref)
            dsink_ref[...] = jnp.zeros_like(dsink_ref)

        ov = o_ref[...]
        dov = do_ref[...]
        dqs, dkbs, dvbs, dkxs, dvxs = [], [], [], [], []
        for g in range(A_KV_HEADS):
            p, pc, ps, qg, kb, vb, kxb, vxb = _attn_probs(
                n, t_len, sink_ref, q_ref[...], kp_ref[...], kc_ref[...], kn_ref[...], kx_ref[...], g)
            og = jnp.concatenate([ov[:, (4 * g + r) * hd:(4 * g + r + 1) * hd] for r in range(4)], axis=0)
            dog = jnp.concatenate([dov[:, (4 * g + r) * hd:(4 * g + r + 1) * hd] for r in range(4)], axis=0)
            delta = jnp.sum(og * dog, axis=-1, keepdims=True)
            dogb = dog.astype(BF16)
            ds = (p * (_dot_nt(dogb, vb) - delta) * scale).astype(BF16)
            dsc = (pc * (_dot_nt(dogb, vxb) - delta) * scale).astype(BF16)
            dsk = ps * (0.0 - delta)
            dqg = _dot(ds, kb) + _dot(dsc, kxb)
            dqs += [dqg[r * _QB:(r + 1) * _QB] for r in range(4)]
            dkbs.append(_dot_tn(ds, qg))
            dvbs.append(_dot_tn(p.astype(BF16), dogb))
            dkxs.append(_dot_tn(dsc, qg))
            dvxs.append(_dot_tn(pc.astype(BF16), dogb))
            for r in range(4):
                hrow = 4 * g + r
                tot = jnp.sum(dsk[r * _QB:(r + 1) * _QB], axis=0, keepdims=True)
                dsink_ref[hrow:hrow + 1, :] += jnp.broadcast_to(tot, (1, 128))
        dq_ref[...] = jnp.concatenate(dqs, axis=1)
        band = jnp.concatenate(dkbs + dvbs, axis=1)
        ctxc = jnp.concatenate(dkxs + dvxs, axis=1)
        r_prev = pl.multiple_of(jnp.maximum(n - 1, 0) * _QB, _QB)
        r_cur = pl.multiple_of(n * _QB, _QB)
        r_next = pl.multiple_of((n + 1) * _QB, _QB)
        dkv_ref[pl.ds(r_prev, _QB), :] += band[0:_QB]
        dkv_ref[pl.ds(r_cur, _QB), :] += band[_QB:2 * _QB]
        dkv_ref[pl.ds(r_next, _QB), :] += band[2 * _QB:3 * _QB]
        dkv_ref[t_len:t_len + l_ctx, :] += ctxc

    return pl.pallas_call(
        body, name=name, grid=(nb,), in_specs=in_specs,
        out_specs=[pl.BlockSpec((_QB, 512), lambda n: (n, 0)),
                   pl.BlockSpec((rows, 256), lambda n: (0, 0)),
                   pl.BlockSpec((8, 128), lambda n: (0, 0))],
        out_shape=[jax.ShapeDtypeStruct((t_len, 512), F32),
                   jax.ShapeDtypeStruct((rows, 256), F32),
                   jax.ShapeDtypeStruct((8, 128), F32)],
        compiler_params=_cp("arbitrary"),
    )(sink, zc, zc, zc, zc, zc, o, dcat)


_GC = B_CHUNK


def _gla_chunk_terms(qk, v, la, head, reverse):
    q = qk[:, head * B_DK:(head + 1) * B_DK]
    k = qk[:, 256 + head * B_DK:256 + (head + 1) * B_DK]
    vh = v[:, head * B_DV:(head + 1) * B_DV]
    off = 256 if reverse else 0
    lah = la[:, off + head * B_DK:off + (head + 1) * B_DK]
    ii = lax.broadcasted_iota(jnp.int32, (_GC, _GC), 0)
    jj = lax.broadcasted_iota(jnp.int32, (_GC, _GC), 1)
    mask = (jj >= ii) if reverse else (jj <= ii)
    tri = jnp.where(mask, 1.0, 0.0).astype(F32)
    g = _dot_hi(tri, lah)
    gl = jnp.sum(lah, axis=0, keepdims=True)
    eg = jnp.exp(g)
    eng = jnp.exp(-g)
    eend = jnp.exp(gl - g)
    sc = B_DK ** -0.5
    qt = q * (sc * eg)
    kt = k * eng
    ke = k * eend
    return q, k, vh, lah, mask, tri, g, gl, eg, eng, eend, qt, kt, ke


def _gla_fwd(zc, la, t_len, l_ctx, name):
    rows = zc.shape[0]
    n_x = t_len // _GC
    n_c = n_x + l_ctx // _GC
    qkb, vb = ZC_QK // 512, ZC_V // 512

    def ch_f(c):
        return lax.rem(c + n_x, n_c)

    def ch_r(c):
        return n_c - 1 - c

    def body(qkf_ref, vf_ref, laf_ref, qkr_ref, vr_ref, lar_ref, of_ref, or_ref, spf_ref, spr_ref, stf, strv):
        c = pl.program_id(0)

        @pl.when(c == 0)
        def _():
            stf[...] = jnp.zeros_like(stf)
            strv[...] = jnp.zeros_like(strv)

        for qk_ref, v_ref, la_ref, o_ref, sp_ref, st, reverse in (
                (qkf_ref, vf_ref, laf_ref, of_ref, spf_ref, stf, False),
                (qkr_ref, vr_ref, lar_ref, or_ref, spr_ref, strv, True)):
            qk = qk_ref[...]
            v = v_ref[...]
            la = la_ref[...]
            for hh in range(B_HEADS):
                _, _, vh, _, mask, _, _, gl, _, _, _, qt, kt, ke = _gla_chunk_terms(qk, v, la, hh, reverse)
                s_prev = st[hh]
                att = jnp.where(mask, _dot_nt(qt.astype(BF16), kt.astype(BF16)), 0.0)
                o = _dot(att.astype(BF16), vh.astype(BF16)) + _dot_nt(qt.astype(BF16), s_prev.astype(BF16))
                o_ref[:, hh * B_DV:(hh + 1) * B_DV] = o
                sp_ref[0, hh] = s_prev
                st[hh] = s_prev * jnp.exp(gl) + _dot_tn(vh.astype(BF16), ke.astype(BF16))

    st_shape = (B_HEADS, B_DV, B_DK)
    return pl.pallas_call(
        body, name=name, grid=(n_c,),
        in_specs=[pl.BlockSpec((_GC, 512), lambda c: (ch_f(c), qkb)),
                  pl.BlockSpec((_GC, 512), lambda c: (ch_f(c), vb)),
                  pl.BlockSpec((_GC, 512), lambda c: (ch_f(c), 0)),
                  pl.BlockSpec((_GC, 512), lambda c: (ch_r(c), qkb)),
                  pl.BlockSpec((_GC, 512), lambda c: (ch_r(c), vb)),
                  pl.BlockSpec((_GC, 512), lambda c: (ch_r(c), 0))],
        out_specs=[pl.BlockSpec((_GC, 512), lambda c: (ch_f(c), 0)),
                   pl.BlockSpec((_GC, 512), lambda c: (ch_r(c), 0)),
                   pl.BlockSpec((1,) + st_shape, lambda c: (c, 0, 0, 0)),
                   pl.BlockSpec((1,) + st_shape, lambda c: (c, 0, 0, 0))],
        out_shape=[jax.ShapeDtypeStruct((rows, 512), F32), jax.ShapeDtypeStruct((rows, 512), F32),
                   jax.ShapeDtypeStruct((n_c,) + st_shape, F32), jax.ShapeDtypeStruct((n_c,) + st_shape, F32)],
        scratch_shapes=[pltpu.VMEM(st_shape, F32), pltpu.VMEM(st_shape, F32)],
        compiler_params=_cp("arbitrary"),
    )(zc, zc, la, zc, zc, la)


def _gla_bwd(zc, la, spf, spr, dosum, t_len, l_ctx, name):
    rows = zc.shape[0]
    n_x = t_len // _GC
    n_c = n_x + l_ctx // _GC
    n_all = rows // _GC
    qkb, vb = ZC_QK // 512, ZC_V // 512

    def scan_of(c):
        return jnp.maximum(n_c - 1 - c, 0)

    def ch_f(c):
        return jnp.where(c < n_c, lax.rem(scan_of(c) + n_x, n_c), c)

    def ch_r(c):
        return c

    def do_of(ch):
        return jnp.minimum(ch, n_x - 1)

    def body(qkf_ref, vf_ref, laf_ref, spf_ref, dof_ref, qkr_ref, vr_ref, lar_ref, spr_ref, dor_ref,
             dqkf_ref, dvf_ref, dlaf_ref, dqkr_ref, dvr_ref, dlar_ref, dsf, dsr):
        c = pl.program_id(0)

        @pl.when(c == 0)
        def _():
            dsf[...] = jnp.zeros_like(dsf)
            dsr[...] = jnp.zeros_like(dsr)

        @pl.when(c >= n_c)
        def _():
            for r in (dqkf_ref, dvf_ref, dlaf_ref, dqkr_ref, dvr_ref, dlar_ref):
                r[...] = jnp.zeros_like(r)

        @pl.when(c < n_c)
        def _():
            for qk_ref, v_ref, la_ref, sp_ref, do_ref, dqk_ref, dv_ref, dla_ref, dst, reverse, ch in (
                    (qkf_ref, vf_ref, laf_ref, spf_ref, dof_ref, dqkf_ref, dvf_ref, dlaf_ref, dsf, False, ch_f(c)),
                    (qkr_ref, vr_ref, lar_ref, spr_ref, dor_ref, dqkr_ref, dvr_ref, dlar_ref, dsr, True, ch_r(c))):
                qk = qk_ref[...]
                v = v_ref[...]
                la = la_ref[...]
                dov = jnp.where(ch < n_x, do_ref[...], 0.0)
                sc = B_DK ** -0.5
                for hh in range(B_HEADS):
                    _, _, vh, _, mask, tri, _, gl, eg, eng, eend, qt, kt, ke = _gla_chunk_terms(qk, v, la, hh, reverse)
                    s_prev = sp_ref[0, hh]
                    ds_new = dst[hh]
                    doh = dov[:, hh * B_DV:(hh + 1) * B_DV]
                    dob = doh.astype(BF16)
                    vbh = vh.astype(BF16)
                    qtb, ktb, keb = qt.astype(BF16), kt.astype(BF16), ke.astype(BF16)
                    att = jnp.where(mask, _dot_nt(qtb, ktb), 0.0).astype(BF16)
                    datt = jnp.where(mask, _dot_nt(dob, vbh), 0.0).astype(BF16)
                    dqt = _dot(datt, ktb) + _dot(dob, s_prev.astype(BF16))
                    dkt = _dot_tn(datt, qtb)
                    dvh = _dot_tn(att, dob) + _dot_nt(keb, ds_new.astype(BF16))
                    dke = _dot(vbh, ds_new.astype(BF16))
                    egl = jnp.exp(gl)
                    dst[hh] = ds_new * egl + _dot_tn(dob, qtb)
                    dgl = (jnp.sum(dke * ke, axis=0, keepdims=True)
                           + jnp.sum(ds_new * s_prev, axis=0, keepdims=True) * egl)
                    dq = dqt * (sc * eg)
                    dk = dkt * eng + dke * eend
                    dg = dqt * qt - dkt * kt - dke * ke
                    dlah = _dot_tn_hi(tri, dg) + dgl
                    dqk_ref[:, hh * B_DK:(hh + 1) * B_DK] = dq
                    dqk_ref[:, 256 + hh * B_DK:256 + (hh + 1) * B_DK] = dk
                    dv_ref[:, hh * B_DV:(hh + 1) * B_DV] = dvh
                    dla_ref[:, hh * B_DK:(hh + 1) * B_DK] = dlah

    st_shape = (B_HEADS, B_DV, B_DK)

    def side(chf):
        return [pl.BlockSpec((_GC, 512), lambda c: (chf(c), qkb)),
                pl.BlockSpec((_GC, 512), lambda c: (chf(c), vb)),
                pl.BlockSpec((_GC, 512), lambda c: (chf(c), 0)),
                pl.BlockSpec((1,) + st_shape, lambda c: (scan_of(c), 0, 0, 0)),
                pl.BlockSpec((_GC, 512), lambda c: (do_of(chf(c)), 0))]

    def out_side(chf):
        return [pl.BlockSpec((_GC, 512), lambda c: (chf(c), 0)),
                pl.BlockSpec((_GC, 512), lambda c: (chf(c), 0)),
                pl.BlockSpec((_GC, 256), lambda c: (chf(c), 0))]

    shp = [jax.ShapeDtypeStruct((rows, 512), F32), jax.ShapeDtypeStruct((rows, 512), F32),
           jax.ShapeDtypeStruct((rows, 256), F32)]
    return pl.pallas_call(
        body, name=name, grid=(n_all,),
        in_specs=side(ch_f) + side(ch_r),
        out_specs=out_side(ch_f) + out_side(ch_r),
        out_shape=shp + shp,
        scratch_shapes=[pltpu.VMEM(st_shape, F32), pltpu.VMEM(st_shape, F32)],
        compiler_params=_cp("arbitrary"),
    )(zc, zc, la, spf, dosum, zc, zc, la, spr, dosum)


def _gla_out_fwd(o_a, o_f, o_r, zc, gla_g, t_len, name):
    tm = ROW_TILE
    rb = ZC_R // 512

    def body(oa_ref, of_ref, or_ref, r_ref, g_ref, cat_ref):
        osum = of_ref[...] + or_ref[...]
        g = g_ref[...]
        pieces = []
        for hh in range(B_HEADS):
            oh = osum[:, hh * B_DV:(hh + 1) * B_DV]
            rs = lax.rsqrt(jnp.mean(oh * oh, axis=-1, keepdims=True) + RMS_EPS)
            pieces.append((oh * rs) * g)
        r = r_ref[...]
        cat_ref[:, 0:512] = oa_ref[...].astype(BF16)
        cat_ref[:, 512:1024] = (jnp.concatenate(pieces, axis=1) * (r * _sigmoid(r))).astype(BF16)

    return pl.pallas_call(
        body, name=name, grid=(t_len // tm,),
        in_specs=[pl.BlockSpec((tm, 512), lambda i: (i, 0)),
                  pl.BlockSpec((tm, 512), lambda i: (i, 0)),
                  pl.BlockSpec((tm, 512), lambda i: (i, 0)),
                  pl.BlockSpec((tm, 512), lambda i: (i, rb)),
                  pl.BlockSpec((1, B_DV), lambda i: (0, 0))],
        out_specs=pl.BlockSpec((tm, D_MODEL), lambda i: (i, 0)),
        out_shape=jax.ShapeDtypeStruct((t_len, D_MODEL), BF16),
        compiler_params=_cp("parallel"),
    )(o_a, o_f, o_r, zc, gla_g)


def _gla_out_bwd(dcat, o_f, o_r, zc, gla_g, t_len, name):
    tm = ROW_TILE
    rb = ZC_R // 512

    def body(d_ref, of_ref, or_ref, r_ref, g_ref, dos_ref, dr_ref, dg_ref):
        i = pl.program_id(0)
        osum = of_ref[...] + or_ref[...]
        g = g_ref[...]
        r = r_ref[...]
        dgo = d_ref[...]
        sg = _sigmoid(r)
        dnrmg = dgo * (r * sg)
        nrms, dos = [], []
        dg_acc = jnp.zeros((1, B_DV), F32)
        for hh in range(B_HEADS):
            oh = osum[:, hh * B_DV:(hh + 1) * B_DV]
            rs = lax.rsqrt(jnp.mean(oh * oh, axis=-1, keepdims=True) + RMS_EPS)
            nrm = oh * rs
            dn = dnrmg[:, hh * B_DV:(hh + 1) * B_DV]
            dg_acc = dg_acc + jnp.sum(dn * nrm, axis=0, keepdims=True)
            dnn = dn * g
            dos.append(rs * (dnn - nrm * jnp.mean(dnn * nrm, axis=-1, keepdims=True)))
            nrms.append(nrm * g)
        dos_ref[...] = jnp.concatenate(dos, axis=1)
        dr_ref[...] = dgo * jnp.concatenate(nrms, axis=1) * (sg * (1.0 + r * (1.0 - sg)))

        @pl.when(i == 0)
        def _():
            dg_ref[...] = jnp.zeros_like(dg_ref)

        dg_ref[...] += dg_acc

    return pl.pallas_call(
        body, name=name, grid=(t_len // tm,),
        in_specs=[pl.BlockSpec((tm, 512), lambda i: (i, 1)),
                  pl.BlockSpec((tm, 512), lambda i: (i, 0)),
                  pl.BlockSpec((tm, 512), lambda i: (i, 0)),
                  pl.BlockSpec((tm, 512), lambda i: (i, rb)),
                  pl.BlockSpec((1, B_DV), lambda i: (0, 0))],
        out_specs=[pl.BlockSpec((tm, 512), lambda i: (i, 0)),
                   pl.BlockSpec((tm, 512), lambda i: (i, 0)),
                   pl.BlockSpec((1, B_DV), lambda i: (0, 0))],
        out_shape=[jax.ShapeDtypeStruct((t_len, 512), F32), jax.ShapeDtypeStruct((t_len, 512), F32),
                   jax.ShapeDtypeStruct((1, B_DV), F32)],
        compiler_params=_cp("arbitrary"),
    )(dcat, o_f, o_r, zc, gla_g)


def _mix_prep(dq, dkv, dqk_f, dqk_r, dv_f, dv_r, d_r, dla_f, dla_r, zc, wg2, bias2, cs, t_len, name):
    rows = zc.shape[0]
    tm = ROW_TILE
    n_x = t_len // tm
    gb = ZC_G // 128

    def xrow(i):
        return jnp.minimum(i, n_x - 1)

    def body(dq_ref, dkv_ref, dqkf_ref, dqkr_ref, dvf_ref, dvr_ref, dr_ref, dlaf_ref, dlar_ref, zg_ref, wg_ref,
             b_ref, cs_ref, dz_ref, dwg_ref, db_ref):
        i = pl.program_id(0)
        is_x = i < n_x
        cos = cs_ref[:, 0:128]
        sin = cs_ref[:, 128:256]
        cosq = jnp.concatenate([cos] * 4, axis=1)
        sinq = jnp.concatenate([sin] * 4, axis=1)
        dqv = jnp.where(is_x, dq_ref[...], 0.0)
        dz_ref[:, ZC_Q:ZC_QK] = (dqv * cosq + _swap16(dqv * sinq)).astype(BF16)
        dz_ref[:, ZC_QK:ZC_V] = (dqkf_ref[...] + dqkr_ref[...]).astype(BF16)
        dz_ref[:, ZC_V:ZC_R] = (dvf_ref[...] + dvr_ref[...]).astype(BF16)
        dz_ref[:, ZC_R:ZC_KV] = jnp.where(is_x, dr_ref[...], 0.0).astype(BF16)
        dk = dkv_ref[:, 0:128]
        dz_ref[:, ZC_KV:ZC_KV + 128] = (dk * cos + _swap16(dk * sin)).astype(BF16)
        dz_ref[:, ZC_KV + 128:ZC_G] = dkv_ref[:, 128:256].astype(BF16)
        zgb = zg_ref[...].astype(BF16)
        wg = wg_ref[...]
        pre = _dot(zgb, wg) + b_ref[...]
        dla = jnp.concatenate([dlaf_ref[...], dlar_ref[...]], axis=1)
        dpre = dla * (_sigmoid(-pre) / B_GATE_NORM)
        dpb = dpre.astype(BF16)
        dz_ref[:, ZC_G:ZC_W] = _dot_nt(dpb, wg).astype(BF16)

        @pl.when(i == 0)
        def _():
            dwg_ref[...] = jnp.zeros_like(dwg_ref)
            db_ref[...] = jnp.zeros_like(db_ref)

        dwg_ref[...] += _dot_tn(zgb, dpb)
        db_ref[...] += jnp.sum(dpre, axis=0, keepdims=True)

    return pl.pallas_call(
        body, name=name, grid=(rows // tm,),
        in_specs=[pl.BlockSpec((tm, 512), lambda i: (xrow(i), 0)),
                  pl.BlockSpec((tm, 256), lambda i: (i, 0)),
                  pl.BlockSpec((tm, 512), lambda i: (i, 0)),
                  pl.BlockSpec((tm, 512), lambda i: (i, 0)),
                  pl.BlockSpec((tm, 512), lambda i: (i, 0)),
                  pl.BlockSpec((tm, 512), lambda i: (i, 0)),
                  pl.BlockSpec((tm, 512), lambda i: (xrow(i), 0)),
                  pl.BlockSpec((tm, 256), lambda i: (i, 0)),
                  pl.BlockSpec((tm, 256), lambda i: (i, 0)),
                  pl.BlockSpec((tm, 128), lambda i: (i, gb)),
                  pl.BlockSpec((128, 512), lambda i: (0, 0)),
                  pl.BlockSpec((1, 512), lambda i: (0, 0)),
                  pl.BlockSpec((tm, 256), lambda i: (i, 0))],
        out_specs=[pl.BlockSpec((tm, ZC_W), lambda i: (i, 0)),
                   pl.BlockSpec((128, 512), lambda i: (0, 0)),
                   pl.BlockSpec((1, 512), lambda i: (0, 0))],
        out_shape=[jax.ShapeDtypeStruct((rows, ZC_W), BF16),
                   jax.ShapeDtypeStruct((128, 512), F32),
                   jax.ShapeDtypeStruct((1, 512), F32)],
        compiler_params=_cp("arbitrary"),
    )(dq, dkv, dqk_f, dqk_r, dv_f, dv_r, d_r, dla_f, dla_r, zc, wg2, bias2, cs)


def _gate_weights(w_a2_f, b_a_f, w_a2_b, b_a_b):
    wg2 = jnp.zeros((128, 512), F32)
    wg2 = wg2.at[0:B_GATE_RANK, 0:256].set(w_a2_f).at[B_GATE_RANK:2 * B_GATE_RANK, 256:512].set(w_a2_b)
    bias2 = jnp.concatenate([b_a_f, b_a_b]).reshape(1, 512)
    return wg2.astype(BF16), bias2


_WIN_PERM = ((0, 512), (768, 1280), (1280, 1792), (1792, 2304), (512, 768), (2304, 2336))


def _w_in_to_cat(w_in_full):
    parts = [w_in_full[:, a:b] for a, b in _WIN_PERM]
    parts.append(jnp.zeros((w_in_full.shape[0], ZC_W - PROJ_DIM), w_in_full.dtype))
    return jnp.concatenate(parts, axis=1)


def _cat_to_w_in(d_wcat):
    return jnp.concatenate([d_wcat[:, ZC_Q:ZC_QK], d_wcat[:, ZC_KV:ZC_G], d_wcat[:, ZC_QK:ZC_KV],
                            d_wcat[:, ZC_G:ZC_G + 2 * B_GATE_RANK]], axis=1)


def _mixer_ab_forward(x1, g3, mods, wcat, wg2, bias2, sink, gla_g, w_out, cs, t_len, l_ctx, n_x):
    h = _rms_mod_fwd(x1, g3, mods, 1, n_x, BF16, "mix0_mod")
    zc, la = _proj_fwd(h, wcat, wg2, bias2, cs, "mix0_proj")
    o_a = _attn_fwd(zc, sink, t_len, l_ctx, "mix0_attn")
    o_f, o_r, spf, spr = _gla_fwd(zc, la, t_len, l_ctx, "mix0_gla")
    cat = _gla_out_fwd(o_a, o_f, o_r, zc, gla_g, t_len, "mix0_glaout")
    x2, y = _matmul_resid(cat, w_out, x1, mods, 5, 1.0, n_x, t_len, "mix0_out")
    return x2, (x1, h, zc, la, o_a, o_f, o_r, spf, spr, cat, y)


def _mixer_ab_backward(dx2, saved, g3, mods, wcat, wg2, bias2, sink, gla_g, w_out, cs, t_len, l_ctx, n_x):
    x1, h, zc, la, o_a, o_f, o_r, spf, spr, cat, y = saved
    rows = x1.shape[0]
    tm = ROW_TILE
    dy, dgate = _gate_dy(dx2, y, mods, 5, 1.0, n_x, t_len, "mix0_dy")
    dcat = _matmul_nt(dy, w_out, "mix0_dcat")
    d_wout = _matmul_tn(
        cat, dy, pl.BlockSpec((tm, D_MODEL), lambda n, k: (k, 0)), pl.BlockSpec((tm, D_MODEL), lambda n, k: (k, 0)),
        (D_MODEL, D_MODEL), pl.BlockSpec((D_MODEL, D_MODEL), lambda n, k: (0, 0)), (1, t_len // tm), "mix0_dwout")
    dos, d_r, d_glag = _gla_out_bwd(dcat, o_f, o_r, zc, gla_g, t_len, "mix0_dglaout")
    dqk_f, dv_f, dla_f, dqk_r, dv_r, dla_r = _gla_bwd(zc, la, spf, spr, dos, t_len, l_ctx, "mix0_dgla")
    dq, dkv, dsink = _attn_bwd(zc, sink, o_a, dcat, t_len, l_ctx, "mix0_dattn")
    dzc, dwg2, dbias2 = _mix_prep(dq, dkv, dqk_f, dqk_r, dv_f, dv_r, d_r, dla_f, dla_r, zc, wg2, bias2, cs, t_len,
                                  "mix0_prep")
    d_wcat = _matmul_tn(
        h, dzc, pl.BlockSpec((tm, D_MODEL), lambda n, k: (k, 0)), pl.BlockSpec((tm, ZC_W), lambda n, k: (k, 0)),
        (D_MODEL, ZC_W), pl.BlockSpec((D_MODEL, ZC_W), lambda n, k: (0, 0)), (1, rows // tm), "mix0_dwin")
    pairs = [(dzc, pl.BlockSpec((tm, ZC_W), lambda i: (i, 0)), wcat, pl.BlockSpec((D_MODEL, ZC_W), lambda i: (0, 0)))]
    dx1, stats = _bwd_dx(pairs, x1, dx2, t_len // tm, g3, mods, 1, n_x, "mix0_dx")
    return dx1, stats, dgate, d_wcat, dwg2, dbias2, dsink, d_glag, d_wout


_PT = 256
_PH = 16


def _pool_band(n, t_len, w, transpose):
    shape = (_PT, _PT + 2 * _PH)
    a = n * _PT + lax.broadcasted_iota(jnp.int32, shape, 0)
    b = n * _PT - _PH + lax.broadcasted_iota(jnp.int32, shape, 1)
    t, s = (b, a) if transpose else (a, b)
    lo = jnp.maximum(t - w // 2, 0)
    hi = jnp.minimum(t + (w - w // 2), t_len)
    inside = (s >= lo) & (s < hi) & (t >= 0) & (t < t_len)
    mean = jnp.where(inside, 1.0 / (hi - lo).astype(F32), 0.0)
    return mean - jnp.where(s == t, 1.0, 0.0)


def _pool_halo(p_ref, c_ref, n_ref):
    return jnp.concatenate([p_ref[_PT - _PH:_PT, :], c_ref[...], n_ref[0:_PH, :]], axis=0)


def _pool_specs(t_len):
    nb = t_len // _PT
    return [pl.BlockSpec((_PT, D_MODEL), lambda n: (jnp.maximum(n - 1, 0), 0)),
            pl.BlockSpec((_PT, D_MODEL), lambda n: (n, 0)),
            pl.BlockSpec((_PT, D_MODEL), lambda n: (jnp.minimum(n + 1, nb - 1), 0))], nb


def _pool_fwd(h, wp, pscale, x1, mods, t_len, name):
    halo_specs, nb = _pool_specs(t_len)

    def body(hp_ref, hc_ref, hn_ref, w_ref, ps_ref, x_ref, m_ref, x2_ref, pooled_ref, ypre_ref):
        n = pl.program_id(0)
        hcat = _pool_halo(hp_ref, hc_ref, hn_ref)
        ys = []
        for gi, w in enumerate(POOL_WINDOWS):
            cols = slice(gi * POOL_GROUP, (gi + 1) * POOL_GROUP)
            pooled = _dot_hi(_pool_band(n, t_len, w, False), hcat[:, cols]).astype(BF16)
            pooled_ref[:, cols] = pooled
            ys.append(_dot(pooled, w_ref[gi]))
        ypre = jnp.concatenate(ys, axis=1)
        ypre_ref[...] = ypre
        x2_ref[...] = x_ref[...] + m_ref[0, 5:6, :] * (ypre * ps_ref[...])

    return pl.pallas_call(
        body, name=name, grid=(nb,),
        in_specs=halo_specs + [pl.BlockSpec((4, POOL_GROUP, POOL_GROUP), lambda n: (0, 0, 0)),
                               pl.BlockSpec((1, D_MODEL), lambda n: (0, 0)),
                               pl.BlockSpec((_PT, D_MODEL), lambda n: (n, 0)),
                               pl.BlockSpec((1, N_MOD, D_MODEL), lambda n: (0, 0, 0))],
        out_specs=[pl.BlockSpec((_PT, D_MODEL), lambda n: (n, 0))] * 3,
        out_shape=[jax.ShapeDtypeStruct((t_len, D_MODEL), F32), jax.ShapeDtypeStruct((t_len, D_MODEL), BF16),
                   jax.ShapeDtypeStruct((t_len, D_MODEL), F32)],
        compiler_params=_cp("parallel"),
    )(h, h, h, wp, pscale, x1, mods)


def _pool_bwd_a(dx2, ypre, wp, pscale, mods, t_len, name):
    nb = t_len // _PT

    def body(d_ref, y_ref, w_ref, ps_ref, m_ref, dyp_ref, dpl_ref, dgate_ref, dps_ref):
        n = pl.program_id(0)
        dv = d_ref[...]
        ypre = y_ref[...]
        ps = ps_ref[...]
        dy = dv * m_ref[0, 5:6, :]
        dyp = (dy * ps).astype(BF16)
        dyp_ref[...] = dyp
        for gi in range(len(POOL_WINDOWS)):
            cols = slice(gi * POOL_GROUP, (gi + 1) * POOL_GROUP)
            dpl_ref[:, cols] = _dot_nt(dyp[:, cols], w_ref[gi])

        @pl.when(n == 0)
        def _():
            dgate_ref[...] = jnp.zeros_like(dgate_ref)
            dps_ref[...] = jnp.zeros_like(dps_ref)

        dgate_ref[...] += jnp.sum(dv * (ypre * ps), axis=0, keepdims=True)
        dps_ref[...] += jnp.sum(dy * ypre, axis=0, keepdims=True)

    return pl.pallas_call(
        body, name=name, grid=(nb,),
        in_specs=[pl.BlockSpec((_PT, D_MODEL), lambda n: (n, 0)),
                  pl.BlockSpec((_PT, D_MODEL), lambda n: (n, 0)),
                  pl.BlockSpec((4, POOL_GROUP, POOL_GROUP), lambda n: (0, 0, 0)),
                  pl.BlockSpec((1, D_MODEL), lambda n: (0, 0)),
                  pl.BlockSpec((1, N_MOD, D_MODEL), lambda n: (0, 0, 0))],
        out_specs=[pl.BlockSpec((_PT, D_MODEL), lambda n: (n, 0)),
                   pl.BlockSpec((_PT, D_MODEL), lambda n: (n, 0)),
                   pl.BlockSpec((1, D_MODEL), lambda n: (0, 0)),
                   pl.BlockSpec((1, D_MODEL), lambda n: (0, 0))],
        out_shape=[jax.ShapeDtypeStruct((t_len, D_MODEL), BF16), jax.ShapeDtypeStruct((t_len, D_MODEL), F32),
                   jax.ShapeDtypeStruct((1, D_MODEL), F32), jax.ShapeDtypeStruct((1, D_MODEL), F32)],
        compiler_params=_cp("arbitrary"),
    )(dx2, ypre, wp, pscale, mods)


def _pool_bwd_dx(dpl, x1, dx2, g3, mods, t_len, name):
    halo_specs, nb = _pool_specs(t_len)

    def body(dp_ref, dc_ref, dn_ref, x_ref, d_ref, g_ref, m_ref, dx_ref, acc_ref):
        n = pl.program_id(0)
        dcat = _pool_halo(dp_ref, dc_ref, dn_ref)
        dhs = []
        for gi, w in enumerate(POOL_WINDOWS):
            cols = slice(gi * POOL_GROUP, (gi + 1) * POOL_GROUP)
            dhs.append(_dot_hi(_pool_band(n, t_len, w, True), dcat[:, cols]))
        dh = jnp.concatenate(dhs, axis=1)
        g = g_ref[1:2, :]
        scale = m_ref[0, 4:5, :]
        dx = _rms_mod_bwd_tail(dh, x_ref[...], g, scale, 0, acc_ref, n == 0)
        dx_ref[...] = d_ref[...] + dx

    return pl.pallas_call(
        body, name=name, grid=(nb,),
        in_specs=halo_specs + [pl.BlockSpec((_PT, D_MODEL), lambda n: (n, 0)),
                               pl.BlockSpec((_PT, D_MODEL), lambda n: (n, 0)),
                               pl.BlockSpec((3, D_MODEL), lambda n: (0, 0)),
                               pl.BlockSpec((1, N_MOD, D_MODEL), lambda n: (0, 0, 0))],
        out_specs=[pl.BlockSpec((_PT, D_MODEL), lambda n: (n, 0)),
                   pl.BlockSpec((2, 3, D_MODEL), lambda n: (0, 0, 0))],
        out_shape=[jax.ShapeDtypeStruct((t_len, D_MODEL), F32), jax.ShapeDtypeStruct((2, 3, D_MODEL), F32)],
        compiler_params=_cp("arbitrary"),
    )(dpl, dpl, dpl, x1, dx2, g3, mods)


def _mixer_pool_forward(x1, g3, mods, wp, pscale, t_len):
    h = _rms_mod_fwd(x1, g3, mods, 1, t_len // ROW_TILE, F32, "mix1_mod")
    x2, pooled, ypre = _pool_fwd(h, wp, pscale, x1, mods, t_len, "mix1_pool")
    return x2, (x1, pooled, ypre)


def _mixer_pool_backward(dx2, saved, g3, mods, wp, pscale, t_len):
    x1, pooled, ypre = saved
    tm = ROW_TILE
    dyp, dpl, dgate, dps = _pool_bwd_a(dx2, ypre, wp, pscale, mods, t_len, "mix1_da")
    d_wp = _matmul_tn(
        pooled, dyp, pl.BlockSpec((tm, POOL_GROUP), lambda g, k: (k, g)),
        pl.BlockSpec((tm, POOL_GROUP), lambda g, k: (k, g)),
        (4, POOL_GROUP, POOL_GROUP), pl.BlockSpec((1, POOL_GROUP, POOL_GROUP), lambda g, k: (g, 0, 0)),
        (4, t_len // tm), "mix1_dwp")
    dx1, stats = _pool_bwd_dx(dpl, x1, dx2, g3, mods, t_len, "mix1_dx")
    return dx1, stats, dgate, dps, d_wp


def _final_loss(x3, final_g, target, name):
    t_len = x3.shape[0]
    tm = ROW_TILE

    def body(x_ref, g_ref, t_ref, dx_ref, loss_ref, dg_ref):
        i = pl.program_id(0)
        xv = x_ref[...]
        g = g_ref[...]
        r = lax.rsqrt(jnp.mean(xv * xv, axis=-1, keepdims=True) + RMS_EPS)
        xhat = xv * r
        err = xhat * g - t_ref[...]
        part = 0.5 * jnp.sum(jnp.mean(err * err, axis=-1, keepdims=True), axis=0, keepdims=True)
        dy = err * (1.0 / D_MODEL)

        @pl.when(i == 0)
        def _():
            loss_ref[...] = jnp.zeros_like(loss_ref)
            dg_ref[...] = jnp.zeros_like(dg_ref)

        loss_ref[...] += jnp.broadcast_to(part, (1, 128))
        dg_ref[...] += jnp.sum(dy * xhat, axis=0, keepdims=True)
        dxh = dy * g
        dx_ref[...] = r * (dxh - xhat * jnp.mean(dxh * xhat, axis=-1, keepdims=True))

    return pl.pallas_call(
        body, name=name, grid=(t_len // tm,),
        in_specs=[pl.BlockSpec((tm, D_MODEL), lambda i: (i, 0)),
                  pl.BlockSpec((1, D_MODEL), lambda i: (0, 0)),
                  pl.BlockSpec((tm, D_MODEL), lambda i: (i, 0))],
        out_specs=[pl.BlockSpec((tm, D_MODEL), lambda i: (i, 0)),
                   pl.BlockSpec((1, 128), lambda i: (0, 0)),
                   pl.BlockSpec((1, D_MODEL), lambda i: (0, 0))],
        out_shape=[jax.ShapeDtypeStruct((t_len, D_MODEL), F32), jax.ShapeDtypeStruct((1, 128), F32),
                   jax.ShapeDtypeStruct((1, D_MODEL), F32)],
        compiler_params=_cp("arbitrary"),
    )(x3, final_g, target)


_CROWS = 16


def _adaln_fwd(c16, w_mod, bias_k, name):
    n_l, _, cols = w_mod.shape

    def body(c_ref, w_ref, b_ref, o_ref):
        cv = c_ref[...]
        sc = (cv * _sigmoid(cv)).astype(BF16)
        o_ref[0] = _dot(sc, w_ref[0].astype(BF16)) + b_ref[0]

    return pl.pallas_call(
        body, name=name, grid=(n_l,),
        in_specs=[pl.BlockSpec((_CROWS, D_MODEL), lambda l: (0, 0)),
                  pl.BlockSpec((1, D_MODEL, cols), lambda l: (l, 0, 0)),
                  pl.BlockSpec((1, 1, cols), lambda l: (l, 0, 0))],
        out_specs=pl.BlockSpec((1, _CROWS, cols), lambda l: (l, 0, 0)),
        out_shape=jax.ShapeDtypeStruct((n_l, _CROWS, cols), F32),
        compiler_params=_cp("parallel"),
    )(c16, w_mod, bias_k)


def _adaln_bwd(c16, d16, w_mod, dmmc_k, name):
    n_l, _, cols = w_mod.shape

    def body(c_ref, d_ref, w_ref, dm_ref, gw_ref, cp_ref):
        layer = pl.program_id(0)
        cv = c_ref[...]
        gw_ref[0] = _dot_tn_hi(cv * _sigmoid(cv), d_ref[0])

        @pl.when(layer == 0)
        def _():
            cp_ref[...] = jnp.sum(w_ref[0] * dm_ref[...], axis=1, keepdims=True)

    return pl.pallas_call(
        body, name=name, grid=(n_l,),
        in_specs=[pl.BlockSpec((_CROWS, D_MODEL), lambda l: (0, 0)),
                  pl.BlockSpec((1, _CROWS, cols), lambda l: (l, 0, 0)),
                  pl.BlockSpec((1, D_MODEL, cols), lambda l: (0, 0, 0)),
                  pl.BlockSpec((1, cols), lambda l: (0, 0))],
        out_specs=[pl.BlockSpec((1, D_MODEL, cols), lambda l: (l, 0, 0)),
                   pl.BlockSpec((D_MODEL, 1), lambda l: (0, 0))],
        out_shape=[jax.ShapeDtypeStruct((n_l, D_MODEL, cols), F32), jax.ShapeDtypeStruct((D_MODEL, 1), F32)],
        compiler_params=_cp("arbitrary"),
    )(c16, d16, w_mod, dmmc_k)


def _cctx_grad(cparts, c_ctx2, name):
    def body(p_ref, c_ref, o_ref):
        tot = ((p_ref[0] + p_ref[2]) + p_ref[4]) + p_ref[6]
        cv = c_ref[...]
        sg = _sigmoid(cv)
        o_ref[...] = tot * (sg * (1.0 + cv * (1.0 - sg)))

    return pl.pallas_call(
        body, name=name, out_shape=jax.ShapeDtypeStruct((8, 128), F32),
        in_specs=[pl.BlockSpec(memory_space=pltpu.VMEM), pl.BlockSpec(memory_space=pltpu.VMEM)],
        out_specs=pl.BlockSpec(memory_space=pltpu.VMEM),
    )(cparts, c_ctx2)


def _sum_devices(ga, name):
    def body(g_ref, o_ref):
        acc = g_ref[0]
        for d in range(1, N_DEV):
            acc = acc + g_ref[d]
        o_ref[...] = acc

    return pl.pallas_call(
        body, name=name, out_shape=jax.ShapeDtypeStruct(ga.shape[1:], F32),
        in_specs=[pl.BlockSpec(memory_space=pltpu.VMEM)], out_specs=pl.BlockSpec(memory_space=pltpu.VMEM),
    )(ga)


def _place():
    return lax.axis_index("x"), lax.axis_index("y"), lax.axis_index("c")


def _flip(a, d):
    return 1 - a if d else a


_CHIP_FLIPS = ((1, 0), (0, 1), (1, 1))


def _allgather_small(v, name):
    r, cc = v.shape

    def body(v_ref, out_ref, send_sems, recv_sems, local_sem):
        x, y, c = _place()
        me = 4 * x + 2 * y + c
        mine = pltpu.make_async_copy(v_ref, out_ref.at[me], local_sem)
        mine.start()
        sends = []
        for k in range(1, N_DEV):
            peer = (_flip(x, (k >> 2) & 1), _flip(y, (k >> 1) & 1), _flip(c, k & 1))
            cp = pltpu.make_async_remote_copy(src_ref=v_ref, dst_ref=out_ref.at[me], send_sem=send_sems.at[k - 1],
                                              recv_sem=recv_sems.at[k - 1], device_id=peer, device_id_type=MESH)
            cp.start()
            sends.append(cp)
        for k in range(1, N_DEV):
            px, py, pc = _flip(x, (k >> 2) & 1), _flip(y, (k >> 1) & 1), _flip(c, k & 1)
            pltpu.make_async_remote_copy(src_ref=v_ref, dst_ref=out_ref.at[4 * px + 2 * py + pc],
                                         send_sem=send_sems.at[k - 1], recv_sem=recv_sems.at[k - 1],
                                         device_id=(px, py, pc), device_id_type=MESH).wait_recv()
        for cp in sends:
            cp.wait_send()
        mine.wait()

    return pl.pallas_call(
        body, name=name, out_shape=jax.ShapeDtypeStruct((N_DEV, r, cc), F32),
        in_specs=[pl.BlockSpec(memory_space=pltpu.VMEM)], out_specs=pl.BlockSpec(memory_space=pltpu.VMEM),
        scratch_shapes=[pltpu.SemaphoreType.DMA((N_DEV - 1,)), pltpu.SemaphoreType.DMA((N_DEV - 1,)),
                        pltpu.SemaphoreType.DMA],
        compiler_params=pltpu.CompilerParams(vmem_limit_bytes=VMEM_LIMIT_BYTES),
    )(v)


def _gather_shards(arrs, name):
    n = len(arrs)

    def body(*refs):
        ins, outs = refs[:n], refs[n:2 * n]
        send_sems, recv_sems, local_sems = refs[2 * n:]
        x, y, c = _place()
        k_me = 2 * x + y
        local_copies, sends = [], []
        for a in range(n):
            lc = pltpu.make_async_copy(ins[a], outs[a].at[k_me], local_sems.at[a])
            lc.start()
            local_copies.append(lc)
            for j, (dx, dy) in enumerate(_CHIP_FLIPS):
                cp = pltpu.make_async_remote_copy(
                    src_ref=ins[a], dst_ref=outs[a].at[k_me], send_sem=send_sems.at[3 * a + j],
                    recv_sem=recv_sems.at[3 * a + j], device_id=(_flip(x, dx), _flip(y, dy), c), device_id_type=MESH)
                cp.start()
                sends.append(cp)
        for a in range(n):
            for j, (dx, dy) in enumerate(_CHIP_FLIPS):
                px, py = _flip(x, dx), _flip(y, dy)
                pltpu.make_async_remote_copy(
                    src_ref=ins[a], dst_ref=outs[a].at[2 * px + py], send_sem=send_sems.at[3 * a + j],
                    recv_sem=recv_sems.at[3 * a + j], device_id=(px, py, c), device_id_type=MESH).wait_recv()
        for cp in sends:
            cp.wait_send()
        for lc in local_copies:
            lc.wait()

    any_spec = pl.BlockSpec(memory_space=pl.ANY)
    return pl.pallas_call(
        body, name=name,
        out_shape=[jax.ShapeDtypeStruct((N_CHIPS,) + a.shape, a.dtype) for a in arrs],
        in_specs=[any_spec] * n, out_specs=[any_spec] * n,
        scratch_shapes=[pltpu.SemaphoreType.DMA((3 * n,)), pltpu.SemaphoreType.DMA((3 * n,)),
                        pltpu.SemaphoreType.DMA((n,))],
    )(*arrs)


def _scatter_shards(arrs, name):
    n = len(arrs)

    def body(*refs):
        ins, outs = refs[:n], refs[n:2 * n]
        send_sems, recv_sems = refs[2 * n:]
        x, y, c = _place()
        sends = []
        for a in range(n):
            for j, (dx, dy) in enumerate(_CHIP_FLIPS):
                px, py = _flip(x, dx), _flip(y, dy)
                cp = pltpu.make_async_remote_copy(
                    src_ref=ins[a].at[2 * px + py], dst_ref=outs[a].at[j], send_sem=send_sems.at[3 * a + j],
                    recv_sem=recv_sems.at[3 * a + j], device_id=(px, py, c), device_id_type=MESH)
                cp.start()
                sends.append(cp)
        for cp in sends:
            cp.wait_recv()
        for cp in sends:
            cp.wait_send()

    any_spec = pl.BlockSpec(memory_space=pl.ANY)
    return pl.pallas_call(
        body, name=name,
        out_shape=[jax.ShapeDtypeStruct((3,) + a.shape[1:], a.dtype) for a in arrs],
        in_specs=[any_spec] * n, out_specs=[any_spec] * n,
        scratch_shapes=[pltpu.SemaphoreType.DMA((3 * n,)), pltpu.SemaphoreType.DMA((3 * n,))],
    )(*arrs)


def _swap_sibling(arrs, name):
    n = len(arrs)

    def body(*refs):
        ins, outs = refs[:n], refs[n:2 * n]
        send_sems, recv_sems = refs[2 * n:]
        x, y, c = _place()
        sends = []
        for a in range(n):
            cp = pltpu.make_async_remote_copy(src_ref=ins[a], dst_ref=outs[a], send_sem=send_sems.at[a],
                                              recv_sem=recv_sems.at[a], device_id=(x, y, 1 - c), device_id_type=MESH)
            cp.start()
            sends.append(cp)
        for cp in sends:
            cp.wait()

    any_spec = pl.BlockSpec(memory_space=pl.ANY)
    return pl.pallas_call(
        body, name=name,
        out_shape=[jax.ShapeDtypeStruct(a.shape, a.dtype) for a in arrs],
        in_specs=[any_spec] * n, out_specs=[any_spec] * n,
        scratch_shapes=[pltpu.SemaphoreType.DMA((n,)), pltpu.SemaphoreType.DMA((n,))],
    )(*arrs)


def _row_tile(rows, cols):
    for tr in (1024, 512, 256, 128, 64, 32, 16, 8):
        if rows % tr == 0 and tr * cols * 4 <= (1 << 20):
            return tr
    return rows


def _partial_sum(g_full, recv, k_idx, name):
    _, r, c = g_full.shape
    tr = _row_tile(r, c)

    def body(k_ref, g_ref, r_ref, o_ref):
        del k_ref
        acc = g_ref[0]
        for j in range(3):
            acc = acc + r_ref[j].astype(F32)
        o_ref[...] = acc

    return pl.pallas_call(
        body, name=name,
        grid_spec=pltpu.PrefetchScalarGridSpec(
            num_scalar_prefetch=1, grid=(r // tr,),
            in_specs=[pl.BlockSpec((1, tr, c), lambda i, k: (k[0], i, 0)),
                      pl.BlockSpec((3, tr, c), lambda i, k: (0, i, 0))],
            out_specs=pl.BlockSpec((tr, c), lambda i, k: (i, 0))),
        out_shape=jax.ShapeDtypeStruct((r, c), F32),
        compiler_params=_cp("parallel"),
    )(k_idx, g_full, recv)


def _adamw(w3, parts, m3, v3, name):
    n_l, r, c = w3.shape
    tr = _row_tile(r, c)
    n_i = r // tr
    n_p = len(parts)
    c1 = 1.0 - ADAM_B1 ** ADAM_STEP
    c2 = 1.0 - ADAM_B2 ** ADAM_STEP

    def body(*refs):
        w_ref, m_ref, v_ref = refs[0:3]
        g_refs = refs[3:3 + n_p * n_l]
        go_ref, d_ref, mo_ref, vo_ref = refs[3 + n_p * n_l:]
        layer = pl.program_id(0)
        g = None
        for p in range(n_p):
            term = g_refs[p * n_l][...]
            for l in range(1, n_l):
                term = jnp.where(layer == l, g_refs[p * n_l + l][...], term)
            g = term if g is None else g + term
        w = w_ref[0]
        m = ADAM_B1 * m_ref[0] + (1.0 - ADAM_B1) * g
        v = ADAM_B2 * v_ref[0] + (1.0 - ADAM_B2) * (g * g)
        m_hat = m / c1
        v_hat = v / c2
        go_ref[0] = g
        d_ref[0] = -ADAM_LR * (m_hat / (jnp.sqrt(v_hat) + ADAM_EPS) + ADAM_WD * w)
        mo_ref[0] = m
        vo_ref[0] = v

    blk = pl.BlockSpec((1, tr, c), lambda l, i: (l, i, 0))
    in_specs = [blk, blk, blk]
    args = [w3, m3, v3]
    for part in parts:
        for l in range(n_l):
            in_specs.append(pl.BlockSpec(
                (tr, c), functools.partial(lambda lay, l, i: (jnp.where(l == lay, i, jnp.where(l < lay, 0, n_i - 1)), 0), l)))
            args.append(part[l])
    shp = jax.ShapeDtypeStruct((n_l, r, c), F32)
    return pl.pallas_call(
        body, name=name, grid=(n_l, n_i), in_specs=in_specs, out_specs=[blk] * 4, out_shape=[shp] * 4,
        compiler_params=_cp("arbitrary", "arbitrary"),
    )(*args)


_SMALL_W = 4096
_PACK_ROWS = 352
_N9 = N_MOD * D_MODEL


def _flat_pad(parts, total):
    flat = jnp.concatenate([p.reshape(-1) for p in parts])
    return jnp.concatenate([flat, jnp.zeros((total - flat.shape[0],), F32)])


def kernel(x, c, ctx, c_ctx, w_mod, b_mod, norm_g, ffn1_wi, ffn1_wo, ffn2_wi, ffn2_wo, w_in, w_a2_f, b_a_f, w_a2_b, b_a_b, sink, gla_g, w_out, w_pool, pool_scale, final_g, loss_target, m_c_ctx, m_w_mod, m_b_mod, m_norm_g, m_ffn1_wi, m_ffn1_wo, m_ffn2_wi, m_ffn2_wo, m_w_in, m_w_a2_f, m_b_a_f, m_w_a2_b, m_b_a_b, m_sink, m_gla_g, m_w_out, m_w_pool, m_pool_scale, m_final_g, v_c_ctx, v_w_mod, v_b_mod, v_norm_g, v_ffn1_wi, v_ffn1_wo, v_ffn2_wi, v_ffn2_wo, v_w_in, v_w_a2_f, v_b_a_f, v_w_a2_b, v_b_a_b, v_sink, v_gla_g, v_w_out, v_w_pool, v_pool_scale, v_final_g):
    t_len, l_ctx = x.shape[1], ctx.shape[1]
    tm = ROW_TILE
    pad = (-(t_len + l_ctx)) % tm
    rows0 = t_len + l_ctx + pad
    n_x = t_len // tm
    xi, yi, ci = _place()
    k_me = 2 * xi + yi
    me = 4 * xi + 2 * yi + ci
    mod_cols = w_mod.shape[2]
    n_grp = len(POOL_WINDOWS)

    small_w = _flat_pad([norm_g, w_a2_f, w_a2_b, pool_scale], _SMALL_W).reshape(_SMALL_W // 128, 128)
    shards = [ffn1_wi[0], ffn1_wi[1], ffn1_wo[0], ffn1_wo[1], ffn2_wi[0], ffn2_wi[1], ffn2_wo[0], ffn2_wo[1],
              w_in[0], w_out[0], w_pool[0].reshape(n_grp * w_pool.shape[2], POOL_GROUP)]
    gathered = _gather_shards([s.astype(BF16) for s in shards] + [small_w], "gather_weights")
    w1i, w1o, w2i, w2o = gathered[0:2], gathered[2:4], gathered[4:6], gathered[6:8]
    w_in_full = jnp.concatenate([gathered[8][k] for k in range(N_CHIPS)], axis=1)
    wcat = _w_in_to_cat(w_in_full)
    w_out_full = gathered[9].reshape(D_MODEL, D_MODEL)
    wp_full = gathered[10].reshape(N_CHIPS, n_grp, -1, POOL_GROUP).transpose(1, 0, 2, 3).reshape(
        n_grp, POOL_GROUP, POOL_GROUP)
    sw = gathered[11].reshape(N_CHIPS, _SMALL_W)
    ng_n = norm_g.size
    a2_n = w_a2_f.size
    norm_g_full = jnp.concatenate([sw[k, :ng_n].reshape(norm_g.shape) for k in range(N_CHIPS)], axis=-1)
    w_a2_f_full = jnp.concatenate([sw[k, ng_n:ng_n + a2_n].reshape(w_a2_f.shape[1:]) for k in range(N_CHIPS)], axis=-1)
    w_a2_b_full = jnp.concatenate(
        [sw[k, ng_n + a2_n:ng_n + 2 * a2_n].reshape(w_a2_b.shape[1:]) for k in range(N_CHIPS)], axis=-1)
    pscale_full = jnp.concatenate(
        [sw[k, ng_n + 2 * a2_n:ng_n + 2 * a2_n + pool_scale.size] for k in range(N_CHIPS)]).reshape(1, D_MODEL)
    wg2, bias2 = _gate_weights(w_a2_f_full, b_a_f[0], w_a2_b_full, b_a_b[0])
    gla_g2 = gla_g.reshape(1, B_DV)
    final_g2 = final_g.reshape(1, D_MODEL)
    cs = _rope_tables(t_len, rows0)

    c_all = _allgather_small(c.reshape(8, 128), "gather_cond").reshape(N_DEV, D_MODEL)
    c16 = jnp.concatenate([c_all, c_ctx[None], jnp.zeros((_CROWS - N_DEV - 1, D_MODEL), F32)], axis=0)
    bias_k = lax.dynamic_slice(b_mod, (0, k_me * mod_cols), (2, mod_cols)).reshape(2, 1, mod_cols)
    mm_k = _adaln_fwd(c16, w_mod, bias_k, "adaln_fwd")
    mm_all = _allgather_small(mm_k.reshape(-1, 128), "gather_mod").reshape(N_DEV, 2, _CROWS, mod_cols)
    mm_full = jnp.concatenate([mm_all[2 * k] for k in range(N_CHIPS)], axis=-1)
    mm_x = lax.dynamic_index_in_dim(mm_full, me, axis=1, keepdims=False)
    mm_c = mm_full[:, N_DEV]
    mods = [jnp.stack([mm_x[l].reshape(N_MOD, D_MODEL), mm_c[l].reshape(N_MOD, D_MODEL)]) for l in range(2)]
    g3 = [norm_g_full[0], norm_g_full[1]]

    xcat = jnp.concatenate([x[0], ctx[0], jnp.zeros((pad, D_MODEL), F32)], axis=0)
    x1, sv_a1 = _ffn_forward(xcat, g3[0], mods[0], 0, w1i[0], w1o[0], n_x, "l0_ffn1")
    x2, sv_am = _mixer_ab_forward(x1, g3[0], mods[0], wcat, wg2, bias2, sink[0], gla_g2, w_out_full, cs,
                                  t_len, l_ctx, n_x)
    x3, sv_a2 = _ffn_forward(x2, g3[0], mods[0], 2, w2i[0], w2o[0], n_x, "l0_ffn2")
    x4, sv_b1 = _ffn_forward(x3, g3[1], mods[1], 0, w1i[1], w1o[1], n_x, "l1_ffn1")
    x5, sv_bm = _mixer_pool_forward(x4, g3[1], mods[1], wp_full, pscale_full, t_len)
    x6, sv_b2 = _ffn_forward(x5, g3[1], mods[1], 2, w2i[1], w2o[1], n_x, "l1_ffn2")
    dx6, loss_part, d_final_g = _final_loss(x6, final_g2, loss_target[0], "final_loss")
    loss = lax.psum(loss_part[0, 0], ("x", "y", "c"))

    dx5, d_w2i_b, d_w2o_b, st_b2, dg_b2 = _ffn_backward(dx6, sv_b2, g3[1], mods[1], 2, w2i[1], w2o[1], n_x, "l1_ffn2_b")
    dx4, st_bm, dg_bm, d_pscale, d_wp = _mixer_pool_backward(dx5, sv_bm, g3[1], mods[1], wp_full, pscale_full, t_len)
    dx3, d_w1i_b, d_w1o_b, st_b1, dg_b1 = _ffn_backward(dx4, sv_b1, g3[1], mods[1], 0, w1i[1], w1o[1], n_x, "l1_ffn1_b")
    dx2, d_w2i_a, d_w2o_a, st_a2, dg_a2 = _ffn_backward(dx3, sv_a2, g3[0], mods[0], 2, w2i[0], w2o[0], n_x, "l0_ffn2_b")
    dx1, st_am, dg_am, d_wcat, d_wg2, d_bias2, d_sink, d_glag, d_wout = _mixer_ab_backward(
        dx2, sv_am, g3[0], mods[0], wcat, wg2, bias2, sink[0], gla_g2, w_out_full, cs, t_len, l_ctx, n_x)
    dx0, d_w1i_a, d_w1o_a, st_a1, dg_a1 = _ffn_backward(dx1, sv_a1, g3[0], mods[0], 0, w1i[0], w1o[0], n_x, "l0_ffn1_b")
    grad_x = dx0[:t_len][None]

    def mod_row(st1, dg1, stm, dgm, st2, dg2, s):
        return jnp.concatenate([st1[s, 0], st1[s, 1], dg1[s, 0], stm[s, 0], stm[s, 1], dgm[s, 0],
                                st2[s, 0], st2[s, 1], dg2[s, 0]])

    dg_bm2 = jnp.concatenate([dg_bm, jnp.zeros_like(dg_bm)], axis=0)[:, None, :]
    d_mm_x0 = mod_row(st_a1, dg_a1, st_am, dg_am, st_a2, dg_a2, 0)
    d_mm_x1 = mod_row(st_b1, dg_b1, st_bm, dg_bm2, st_b2, dg_b2, 0)
    d_mm_c0 = mod_row(st_a1, dg_a1, st_am, dg_am, st_a2, dg_a2, 1)
    d_norm_g = jnp.stack([jnp.stack([st[0, 2] + st[1, 2] for st in (st_a1, st_am, st_a2)]),
                          jnp.stack([st[0, 2] + st[1, 2] for st in (st_b1, st_bm, st_b2)])])
    rk = B_GATE_RANK
    pack = _flat_pad([d_mm_x0, d_mm_x1, d_mm_c0, d_norm_g, d_bias2, d_wg2[0:rk, 0:256], d_wg2[rk:2 * rk, 256:512],
                      d_sink[:, 0], jnp.zeros((120,), F32), d_glag, d_pscale, d_final_g],
                     _PACK_ROWS * 128).reshape(_PACK_ROWS, 128)
    pack_all = _allgather_small(pack, "gather_small_grads")
    tot = _sum_devices(pack_all, "sum_small_grads").reshape(-1)
    rows_all = pack_all.reshape(N_DEV, -1)
    o = 3 * _N9
    g_norm_g_full = tot[o:o + 6 * D_MODEL].reshape(2, 3, D_MODEL)
    o += 6 * D_MODEL
    g_bias2 = tot[o:o + 512]
    o += 512
    g_w_a2_f_full = tot[o:o + rk * 256].reshape(rk, 256)
    o += rk * 256
    g_w_a2_b_full = tot[o:o + rk * 256].reshape(rk, 256)
    o += rk * 256
    g_sink = tot[o:o + A_HEADS]
    o += 128
    g_gla_g = tot[o:o + B_DV]
    o += B_DV
    g_pscale_full = tot[o:o + D_MODEL]
    o += D_MODEL
    g_final_g = tot[o:o + D_MODEL]
    d_mmc_tot = tot[2 * _N9:3 * _N9]
    g_b_mod = jnp.stack([tot[0:_N9] + d_mmc_tot, tot[_N9:2 * _N9]])

    zrows = jnp.zeros((_CROWS - N_DEV - 1, _N9), F32)
    d16 = jnp.stack([jnp.concatenate([rows_all[:, 0:_N9], d_mmc_tot[None], zrows], axis=0),
                     jnp.concatenate([rows_all[:, _N9:2 * _N9], jnp.zeros((1, _N9), F32), zrows], axis=0)])
    d16_k = lax.dynamic_slice(d16, (0, 0, k_me * mod_cols), (2, _CROWS, mod_cols))
    dmmc_k = lax.dynamic_slice(d_mmc_tot, (k_me * mod_cols,), (mod_cols,)).reshape(1, mod_cols)
    g_w_mod, c_part = _adaln_bwd(c16, d16_k, w_mod, dmmc_k, "adaln_bwd")
    c_parts = _allgather_small(c_part.reshape(8, 128), "gather_cctx")
    g_c_ctx = _cctx_grad(c_parts, c_ctx.reshape(8, 128), "cctx_grad").reshape(D_MODEL)

    d_w_in4 = _cat_to_w_in(d_wcat).reshape(D_MODEL, N_CHIPS, -1).transpose(1, 0, 2)
    d_wout4 = d_wout.reshape(N_CHIPS, -1, D_MODEL)
    d_wp4 = d_wp.reshape(n_grp, N_CHIPS, -1, POOL_GROUP).transpose(1, 0, 2, 3).reshape(N_CHIPS, -1, POOL_GROUP)
    full = [d_w1i_a, d_w1i_b, d_w1o_a, d_w1o_b, d_w2i_a, d_w2i_b, d_w2o_a, d_w2o_b, d_w_in4, d_wout4, d_wp4]
    recv = _scatter_shards([g.astype(BF16) for g in full], "scatter_grads")
    k_idx = k_me.reshape(1).astype(jnp.int32)
    partial = [_partial_sum(g, r, k_idx, "partial_sum_%d" % i) for i, (g, r) in enumerate(zip(full, recv))]
    other = _swap_sibling(partial, "swap_partials")

    def as3(a, n_l):
        return a.reshape(n_l, -1, a.shape[-1])

    def big(w, m, v, idx, n_l, nm):
        parts = [tuple(partial[idx:idx + n_l]), tuple(other[idx:idx + n_l])]
        return [o_.reshape(w.shape) for o_ in _adamw(as3(w, n_l), parts, as3(m, n_l), as3(v, n_l), "adamw_" + nm)]

    def small(w, g, m, v, shape3, nm):
        return [o_.reshape(w.shape) for o_ in _adamw(w.reshape(shape3), [(g.reshape(shape3[1:]),)],
                                                    m.reshape(shape3), v.reshape(shape3), "adamw_" + nm)]

    def own(a, axis, size):
        return lax.dynamic_slice_in_dim(a, k_me * size, size, axis=axis)

    res = {}
    res["c_ctx"] = small(c_ctx, g_c_ctx, m_c_ctx, v_c_ctx, (1, 8, 128), "c_ctx")
    res["w_mod"] = [o_ for o_ in _adamw(w_mod, [(g_w_mod[0], g_w_mod[1])], m_w_mod, v_w_mod, "adamw_w_mod")]
    res["b_mod"] = small(b_mod, g_b_mod, m_b_mod, v_b_mod, (1, 2, _N9), "b_mod")
    res["norm_g"] = small(norm_g, own(g_norm_g_full, 2, norm_g.shape[2]), m_norm_g, v_norm_g,
                          (1, 6, norm_g.shape[2]), "norm_g")
    res["ffn1_wi"] = big(ffn1_wi, m_ffn1_wi, v_ffn1_wi, 0, 2, "ffn1_wi")
    res["ffn1_wo"] = big(ffn1_wo, m_ffn1_wo, v_ffn1_wo, 2, 2, "ffn1_wo")
    res["ffn2_wi"] = big(ffn2_wi, m_ffn2_wi, v_ffn2_wi, 4, 2, "ffn2_wi")
    res["ffn2_wo"] = big(ffn2_wo, m_ffn2_wo, v_ffn2_wo, 6, 2, "ffn2_wo")
    res["w_in"] = big(w_in, m_w_in, v_w_in, 8, 1, "w_in")
    res["w_a2_f"] = small(w_a2_f, own(g_w_a2_f_full, 1, w_a2_f.shape[2]), m_w_a2_f, v_w_a2_f,
                          (1, rk, w_a2_f.shape[2]), "w_a2_f")
    res["b_a_f"] = small(b_a_f, g_bias2[0:256], m_b_a_f, v_b_a_f, (1, 1, 256), "b_a_f")
    res["w_a2_b"] = small(w_a2_b, own(g_w_a2_b_full, 1, w_a2_b.shape[2]), m_w_a2_b, v_w_a2_b,
                          (1, rk, w_a2_b.shape[2]), "w_a2_b")
    res["b_a_b"] = small(b_a_b, g_bias2[256:512], m_b_a_b, v_b_a_b, (1, 1, 256), "b_a_b")
    res["sink"] = small(sink, g_sink, m_sink, v_sink, (1, 1, A_HEADS), "sink")
    res["gla_g"] = small(gla_g, g_gla_g, m_gla_g, v_gla_g, (1, 1, B_DV), "gla_g")
    res["w_out"] = big(w_out, m_w_out, v_w_out, 9, 1, "w_out")
    res["w_pool"] = big(w_pool, m_w_pool, v_w_pool, 10, 1, "w_pool")
    res["pool_scale"] = small(pool_scale, own(g_pscale_full, 0, pool_scale.shape[1]), m_pool_scale, v_pool_scale,
                              (1, 1, pool_scale.shape[1]), "pool_scale")
    res["final_g"] = small(final_g, g_final_g, m_final_g, v_final_g, (1, 8, 128), "final_g")

    names = ["c_ctx", "w_mod", "b_mod", "norm_g", "ffn1_wi", "ffn1_wo", "ffn2_wi", "ffn2_wo", "w_in", "w_a2_f",
             "b_a_f", "w_a2_b", "b_a_b", "sink", "gla_g", "w_out", "w_pool", "pool_scale", "final_g"]
    outs = [loss, grad_x]
    for field in range(4):
        outs += [res[nm][field] for nm in names]
    return tuple(outs)
```

```python
import functools

import jax
import jax.numpy as jnp
import numpy as np
from jax import lax
from jax.experimental import pallas as pl
from jax.experimental.pallas import tpu as pltpu

F32 = jnp.float32
BF16 = jnp.bfloat16

D_MODEL = 1024
N_MOD = 9
D_FF = 2816
RMS_EPS = 1e-6
A_HEADS = 8
A_KV_HEADS = 2
A_HEAD_DIM = 64
WINDOW = 128
ROPE_BASE = 10000.0
GRID_W = 64
B_HEADS = 4
B_DK = 64
B_DV = 128
B_GATE_RANK = 16
B_GATE_NORM = 16.0
B_CHUNK = 64
POOL_WINDOWS = (2, 4, 8, 16)
POOL_GROUP = D_MODEL // len(POOL_WINDOWS)
PROJ_DIM = 2336

ADAM_LR = 0.001
ADAM_B1 = 0.9
ADAM_B2 = 0.999
ADAM_EPS = 1e-08
ADAM_WD = 0.01
ADAM_STEP = 10

N_CHIPS = 4
N_DEV = 8
ROW_TILE = 512
VMEM_LIMIT_BYTES = 56 * 1024 * 1024
MESH = pl.DeviceIdType.MESH

ZC_Q, ZC_QK, ZC_V, ZC_R, ZC_KV, ZC_G, ZC_W = 0, 512, 1024, 1536, 2048, 2304, 2432


def _cp(*sem):
    return pltpu.CompilerParams(dimension_semantics=sem if sem else None, vmem_limit_bytes=VMEM_LIMIT_BYTES)


def _dot(a, b):
    return jnp.dot(a, b, preferred_element_type=F32)


def _dot_nt(a, b):
    return lax.dot_general(a, b, (((1,), (1,)), ((), ())), preferred_element_type=F32)


def _dot_tn(a, b):
    return lax.dot_general(a, b, (((0,), (0,)), ((), ())), preferred_element_type=F32)


def _dot_hi(a, b):
    return jnp.dot(a, b, preferred_element_type=F32, precision=lax.Precision.HIGHEST)


def _dot_tn_hi(a, b):
    return lax.dot_general(a, b, (((0,), (0,)), ((), ())), preferred_element_type=F32,
                           precision=lax.Precision.HIGHEST)


def _sigmoid(x):
    return 1.0 / (1.0 + jnp.exp(-x))


def _stream_of(i, n_x):
    return jnp.where(i >= n_x, 1, 0)


def _rms_mod_fwd(x, g3, mods, j, n_x, out_dtype, name):
    rows = x.shape[0]
    tm = ROW_TILE
    n_i = rows // tm

    def body(x_ref, g_ref, m_ref, o_ref):
        xv = x_ref[...]
        r = lax.rsqrt(jnp.mean(xv * xv, axis=-1, keepdims=True) + RMS_EPS)
        g = g_ref[j:j + 1, :]
        shift = m_ref[0, 3 * j:3 * j + 1, :]
        scale = m_ref[0, 3 * j + 1:3 * j + 2, :]
        o_ref[...] = (((xv * r) * g) * (1.0 + scale) + shift).astype(out_dtype)

    return pl.pallas_call(
        body, name=name, grid=(n_i,),
        in_specs=[pl.BlockSpec((tm, D_MODEL), lambda i: (i, 0)),
                  pl.BlockSpec((3, D_MODEL), lambda i: (0, 0)),
                  pl.BlockSpec((1, N_MOD, D_MODEL), lambda i: (_stream_of(i, n_x), 0, 0))],
        out_specs=pl.BlockSpec((tm, D_MODEL), lambda i: (i, 0)),
        out_shape=jax.ShapeDtypeStruct((rows, D_MODEL), out_dtype),
        compiler_params=_cp("parallel"),
    )(x, g3, mods)


def _rms_mod_bwd_tail(dh, xv, g, scale, stream, acc_ref, first):
    r = lax.rsqrt(jnp.mean(xv * xv, axis=-1, keepdims=True) + RMS_EPS)
    xhat = xv * r
    t1 = jnp.sum(dh, axis=0, keepdims=True)
    t2 = jnp.sum(dh * xhat, axis=0, keepdims=True)
    stats = jnp.concatenate([t1, t2 * g, t2 * (1.0 + scale)], axis=0)

    @pl.when(first)
    def _():
        acc_ref[...] = jnp.zeros_like(acc_ref)

    acc_ref[pl.ds(stream, 1)] += stats[None]
    dxh = dh * (g * (1.0 + scale))
    return r * (dxh - xhat * jnp.mean(dxh * xhat, axis=-1, keepdims=True))


def _ffn_up(hn, w4, name):
    rows = hn.shape[0]
    h = w4.shape[2]
    tm = ROW_TILE
    n_i = rows // tm

    def body(h_ref, wa_ref, wu_ref, au_ref, s_ref):
        hv = h_ref[...]
        a = _dot(hv, wa_ref[0])
        u = _dot(hv, wu_ref[0])
        au_ref[0] = a.astype(BF16)
        au_ref[1] = u.astype(BF16)
        s_ref[...] = (a * _sigmoid(a) * u).astype(BF16)

    return pl.pallas_call(
        body, name=name, grid=(2, n_i),
        in_specs=[pl.BlockSpec((tm, D_MODEL), lambda j, i: (i, 0)),
                  pl.BlockSpec((1, D_MODEL, h), lambda j, i: (j, 0, 0)),
                  pl.BlockSpec((1, D_MODEL, h), lambda j, i: (j + 2, 0, 0))],
        out_specs=[pl.BlockSpec((2, tm, h), lambda j, i: (0, i, j)),
                   pl.BlockSpec((tm, h), lambda j, i: (i, j))],
        out_shape=[jax.ShapeDtypeStruct((2, rows, 2 * h), BF16),
                   jax.ShapeDtypeStruct((rows, 2 * h), BF16)],
        compiler_params=_cp("arbitrary", "arbitrary"),
    )(hn, w4, w4)


def _matmul_resid(a, w, xres, mods, gate_idx, coef, n_x, rows, name):
    k = a.shape[1]
    tm = ROW_TILE
    n_i = rows // tm

    def body(a_ref, w_ref, x_ref, m_ref, o_ref, f_ref):
        f = _dot(a_ref[...], w_ref[...])
        gate = m_ref[0, gate_idx:gate_idx + 1, :]
        f_ref[...] = f
        o_ref[...] = x_ref[...] + (coef * gate) * f

    return pl.pallas_call(
        body, name=name, grid=(n_i,),
        in_specs=[pl.BlockSpec((tm, k), lambda i: (i, 0)),
                  pl.BlockSpec((k, D_MODEL), lambda i: (0, 0)),
                  pl.BlockSpec((tm, D_MODEL), lambda i: (i, 0)),
                  pl.BlockSpec((1, N_MOD, D_MODEL), lambda i: (_stream_of(i, n_x), 0, 0))],
        out_specs=[pl.BlockSpec((tm, D_MODEL), lambda i: (i, 0)),
                   pl.BlockSpec((tm, D_MODEL), lambda i: (i, 0))],
        out_shape=[jax.ShapeDtypeStruct((rows, D_MODEL), F32),
                   jax.ShapeDtypeStruct((rows, D_MODEL), F32)],
        compiler_params=_cp("parallel"),
    )(a, w, xres, mods)


def _gate_dy(dout, f, mods, gate_idx, coef, n_x, rows, name):
    tm = ROW_TILE
    n_i = rows // tm

    def body(d_ref, f_ref, m_ref, dy_ref, acc_ref):
        i = pl.program_id(0)
        dv = d_ref[...]
        gate = m_ref[0, gate_idx:gate_idx + 1, :]
        dy_ref[...] = (dv * (coef * gate)).astype(BF16)

        @pl.when(i == 0)
        def _():
            acc_ref[...] = jnp.zeros_like(acc_ref)

        part = coef * jnp.sum(dv * f_ref[...], axis=0, keepdims=True)
        acc_ref[pl.ds(_stream_of(i, n_x), 1)] += part[None]

    return pl.pallas_call(
        body, name=name, grid=(n_i,),
        in_specs=[pl.BlockSpec((tm, D_MODEL), lambda i: (i, 0)),
                  pl.BlockSpec((tm, D_MODEL), lambda i: (i, 0)),
                  pl.BlockSpec((1, N_MOD, D_MODEL), lambda i: (_stream_of(i, n_x), 0, 0))],
        out_specs=[pl.BlockSpec((tm, D_MODEL), lambda i: (i, 0)),
                   pl.BlockSpec((2, 1, D_MODEL), lambda i: (0, 0, 0))],
        out_shape=[jax.ShapeDtypeStruct((rows, D_MODEL), BF16),
                   jax.ShapeDtypeStruct((2, 1, D_MODEL), F32)],
        compiler_params=_cp("arbitrary"),
    )(dout, f, mods)


def _ffn_bwd_dz(dy, wo2, au, name):
    rows = dy.shape[0]
    h = wo2.shape[1]
    tm = ROW_TILE
    n_i = rows // tm

    def body(dy_ref, wo_ref, au_ref, dz_ref):
        ds = _dot_nt(dy_ref[...], wo_ref[0])
        a = au_ref[0].astype(F32)
        u = au_ref[1].astype(F32)
        sg = _sigmoid(a)
        dz_ref[0] = (ds * u * (sg * (1.0 + a * (1.0 - sg)))).astype(BF16)
        dz_ref[1] = (ds * (a * sg)).astype(BF16)

    return pl.pallas_call(
        body, name=name, grid=(2, n_i),
        in_specs=[pl.BlockSpec((tm, D_MODEL), lambda j, i: (i, 0)),
                  pl.BlockSpec((1, h, D_MODEL), lambda j, i: (j, 0, 0)),
                  pl.BlockSpec((2, tm, h), lambda j, i: (0, i, j))],
        out_specs=pl.BlockSpec((2, tm, h), lambda j, i: (0, i, j)),
        out_shape=jax.ShapeDtypeStruct((2, rows, 2 * h), BF16),
        compiler_params=_cp("arbitrary", "arbitrary"),
    )(dy, wo2, au)


def _matmul_tn(a, b, a_spec, b_spec, out_shape, out_spec, grid, name):
    nd_a = len(a_spec.block_shape)
    nd_b = len(b_spec.block_shape)
    nd_o = len(out_spec.block_shape)
    k_axis = len(grid) - 1

    def body(a_ref, b_ref, o_ref):
        av = a_ref[(0,) * (nd_a - 2)]
        bv = b_ref[(0,) * (nd_b - 2)]
        part = _dot_tn(av, bv)

        @pl.when(pl.program_id(k_axis) == 0)
        def _():
            o_ref[...] = jnp.zeros_like(o_ref)

        o_ref[(0,) * (nd_o - 2)] += part

    return pl.pallas_call(
        body, name=name, grid=grid, in_specs=[a_spec, b_spec], out_specs=out_spec,
        out_shape=jax.ShapeDtypeStruct(out_shape, F32),
        compiler_params=_cp(*(("arbitrary",) * len(grid))),
    )(a, b)


def _bwd_dx(pairs, x, dres, dres_tiles, g3, mods, j, n_x, name):
    rows = x.shape[0]
    tm = ROW_TILE
    n_i = rows // tm
    n_p = len(pairs)
    nds = [(len(p[1].block_shape), len(p[3].block_shape)) for p in pairs]

    def body(*refs):
        dz_refs = refs[0:2 * n_p:2]
        w_refs = refs[1:2 * n_p:2]
        x_ref, dres_ref, g_ref, m_ref, dx_ref, acc_ref = refs[2 * n_p:]
        i = pl.program_id(0)
        dh = None
        for p in range(n_p):
            dzv = dz_refs[p][(0,) * (nds[p][0] - 2)]
            wv = w_refs[p][(0,) * (nds[p][1] - 2)]
            part = _dot_nt(dzv, wv)
            dh = part if dh is None else dh + part
        g = g_ref[j:j + 1, :]
        scale = m_ref[0, 3 * j + 1:3 * j + 2, :]
        dx = _rms_mod_bwd_tail(dh, x_ref[...], g, scale, _stream_of(i, n_x), acc_ref, i == 0)
        dres_v = jnp.where(i < dres_tiles, dres_ref[...], 0.0)
        dx_ref[...] = dres_v + dx

    in_specs, args = [], []
    for dz, dz_spec, w, w_spec in pairs:
        in_specs += [dz_spec, w_spec]
        args += [dz, w]
    in_specs += [pl.BlockSpec((tm, D_MODEL), lambda i: (i, 0)),
                 pl.BlockSpec((tm, D_MODEL), lambda i: (jnp.minimum(i, dres_tiles - 1), 0)),
                 pl.BlockSpec((3, D_MODEL), lambda i: (0, 0)),
                 pl.BlockSpec((1, N_MOD, D_MODEL), lambda i: (_stream_of(i, n_x), 0, 0))]
    args += [x, dres, g3, mods]
    return pl.pallas_call(
        body, name=name, grid=(n_i,), in_specs=in_specs,
        out_specs=[pl.BlockSpec((tm, D_MODEL), lambda i: (i, 0)),
                   pl.BlockSpec((2, 3, D_MODEL), lambda i: (0, 0, 0))],
        out_shape=[jax.ShapeDtypeStruct((rows, D_MODEL), F32),
                   jax.ShapeDtypeStruct((2, 3, D_MODEL), F32)],
        compiler_params=_cp("arbitrary"),
    )(*args)


def _ffn_forward(x, g3, mods, j, w4_in, w4_out, n_x, name):
    rows = x.shape[0]
    hn = _rms_mod_fwd(x, g3, mods, j, n_x, BF16, name + "_mod")
    au, s = _ffn_up(hn, w4_in, name + "_up")
    wo = w4_out.reshape(D_FF, D_MODEL)
    out, f = _matmul_resid(s, wo, x, mods, 3 * j + 2, 0.5, n_x, rows, name + "_down")
    return out, (x, hn, au, s, f)


def _ffn_backward(dout, saved, g3, mods, j, w4_in, w4_out, n_x, name):
    x, hn, au, s, f = saved
    rows = x.shape[0]
    tm = ROW_TILE
    n_i = rows // tm
    h = w4_in.shape[2]
    dy, dgate = _gate_dy(dout, f, mods, 3 * j + 2, 0.5, n_x, rows, name + "_dy")
    wo2 = w4_out.reshape(2, h, D_MODEL)
    dz = _ffn_bwd_dz(dy, wo2, au, name + "_dz")
    d_wo = _matmul_tn(
        s, dy, pl.BlockSpec((tm, h), lambda n, k: (k, n)), pl.BlockSpec((tm, D_MODEL), lambda n, k: (k, 0)),
        (D_FF, D_MODEL), pl.BlockSpec((h, D_MODEL), lambda n, k: (n, 0)), (2, n_i), name + "_dwo")
    d_wi = _matmul_tn(
        hn, dz, pl.BlockSpec((tm, D_MODEL), lambda q, k: (k, 0)),
        pl.BlockSpec((1, tm, h), lambda q, k: (q // 2, k, q % 2)),
        (4, D_MODEL, h), pl.BlockSpec((1, D_MODEL, h), lambda q, k: (q, 0, 0)), (4, n_i), name + "_dwi")
    pairs = [(dz, pl.BlockSpec((1, tm, h), functools.partial(lambda q, i: (q // 2, i, q % 2), q)),
              w4_in, pl.BlockSpec((1, D_MODEL, h), functools.partial(lambda q, i: (q, 0, 0), q)))
             for q in range(4)]
    dx, stats = _bwd_dx(pairs, x, dout, n_i, g3, mods, j, n_x, name + "_dx")
    return dx, d_wi, d_wo.reshape(w4_out.shape), stats, dgate


def _matmul_nt(a, w, name):
    rows, k = a.shape
    n = w.shape[0]
    tm = ROW_TILE

    def body(a_ref, w_ref, o_ref):
        o_ref[...] = _dot_nt(a_ref[...], w_ref[...])

    return pl.pallas_call(
        body, name=name, grid=(rows // tm,),
        in_specs=[pl.BlockSpec((tm, k), lambda i: (i, 0)), pl.BlockSpec((n, k), lambda i: (0, 0))],
        out_specs=pl.BlockSpec((tm, n), lambda i: (i, 0)),
        out_shape=jax.ShapeDtypeStruct((rows, n), F32),
        compiler_params=_cp("parallel"),
    )(a, w)


def _rope_tables(t_len, rows):
    n = A_HEAD_DIM // 4
    freqs = ROPE_BASE ** (-jnp.arange(n, dtype=F32) / n)
    t = jnp.arange(t_len)
    ang_r = (t // GRID_W).astype(F32)[:, None] * freqs
    ang_c = (t % GRID_W).astype(F32)[:, None] * freqs
    cos = jnp.concatenate([jnp.cos(ang_r), jnp.cos(ang_r), jnp.cos(ang_c), jnp.cos(ang_c)], axis=1)
    sin = jnp.concatenate([-jnp.sin(ang_r), jnp.sin(ang_r), -jnp.sin(ang_c), jnp.sin(ang_c)], axis=1)
    cos = jnp.concatenate([cos, jnp.ones((rows - t_len, A_HEAD_DIM), F32)], axis=0)
    sin = jnp.concatenate([sin, jnp.zeros((rows - t_len, A_HEAD_DIM), F32)], axis=0)
    return jnp.concatenate([cos, cos, sin, sin], axis=1)


def _swap16(x):
    n = x.shape[1]
    lane = lax.broadcasted_iota(jnp.int32, x.shape, 1)
    first = jnp.bitwise_and(lane, 16) == 0
    return jnp.where(first, pltpu.roll(x, n - 16, 1), pltpu.roll(x, 16, 1))


def _log_sigmoid(x):
    return jnp.minimum(x, 0.0) - jnp.log(1.0 + jnp.exp(-jnp.abs(x)))


def _proj_fwd(h, wcat, wg2, bias2, cs, name):
    rows = h.shape[0]
    tm = ROW_TILE

    def body(h_ref, w_ref, wg_ref, b_ref, cs_ref, zc_ref, la_ref):
        z = _dot(h_ref[...], w_ref[...])
        cos = cs_ref[:, 0:128]
        sin = cs_ref[:, 128:256]
        cosq = jnp.concatenate([cos] * 4, axis=1)
        sinq = jnp.concatenate([sin] * 4, axis=1)
        q = z[:, ZC_Q:ZC_QK]
        zc_ref[:, ZC_Q:ZC_QK] = q * cosq + _swap16(q) * sinq
        zc_ref[:, ZC_QK:ZC_KV] = z[:, ZC_QK:ZC_KV]
        kk = z[:, ZC_KV:ZC_KV + 128]
        zc_ref[:, ZC_KV:ZC_KV + 128] = kk * cos + _swap16(kk) * sin
        zc_ref[:, ZC_KV + 128:ZC_W] = z[:, ZC_KV + 128:ZC_W]
        zg = z[:, ZC_G:ZC_W]
        pre = _dot(zg.astype(BF16), wg_ref[...]) + b_ref[...]
        la_ref[...] = _log_sigmoid(pre) / B_GATE_NORM

    return pl.pallas_call(
        body, name=name, grid=(rows // tm,),
        in_specs=[pl.BlockSpec((tm, D_MODEL), lambda i: (i, 0)),
                  pl.BlockSpec((D_MODEL, ZC_W), lambda i: (0, 0)),
                  pl.BlockSpec((128, 512), lambda i: (0, 0)),
                  pl.BlockSpec((1, 512), lambda i: (0, 0)),
                  pl.BlockSpec((tm, 256), lambda i: (i, 0))],
        out_specs=[pl.BlockSpec((tm, ZC_W), lambda i: (i, 0)),
                   pl.BlockSpec((tm, 512), lambda i: (i, 0))],
        out_shape=[jax.ShapeDtypeStruct((rows, ZC_W), F32),
                   jax.ShapeDtypeStruct((rows, 512), F32)],
        compiler_params=_cp("parallel"),
    )(h, wcat, wg2, bias2, cs)


_QB = WINDOW


def _attn_specs(t_len, l_ctx):
    nb = t_len // _QB
    kvb = ZC_KV // 256
    return [pl.BlockSpec(memory_space=pltpu.SMEM),
            pl.BlockSpec((_QB, 512), lambda n: (n, 0)),
            pl.BlockSpec((_QB, 256), lambda n: (jnp.maximum(n - 1, 0), kvb)),
            pl.BlockSpec((_QB, 256), lambda n: (n, kvb)),
            pl.BlockSpec((_QB, 256), lambda n: (n + 1, kvb)),
            pl.BlockSpec((l_ctx, 256), lambda n: (t_len // l_ctx, kvb))], nb


def _attn_probs(n, t_len, sink_ref, qv, kp, kc, kn, kx, g):
    hd = A_HEAD_DIM
    ks = slice(g * hd, (g + 1) * hd)
    vs = slice(128 + g * hd, 128 + (g + 1) * hd)
    kb = jnp.concatenate([kp[:, ks], kc[:, ks], kn[:, ks]], axis=0).astype(BF16)
    vb = jnp.concatenate([kp[:, vs], kc[:, vs], kn[:, vs]], axis=0).astype(BF16)
    kxb = kx[:, ks].astype(BF16)
    vxb = kx[:, vs].astype(BF16)
    qg = jnp.concatenate([qv[:, (4 * g + r) * hd:(4 * g + r + 1) * hd] for r in range(4)], axis=0).astype(BF16)
    qi = lax.broadcasted_iota(jnp.int32, (_QB, 3 * _QB), 0)
    kj = lax.broadcasted_iota(jnp.int32, (_QB, 3 * _QB), 1)
    kpos = n * _QB - _QB + kj
    valid = (kpos >= 0) & (kpos < t_len) & (jnp.abs(kj - _QB - qi) <= WINDOW)
    valid4 = jnp.concatenate([valid] * 4, axis=0)
    scale = hd ** -0.5
    s = jnp.where(valid4, _dot_nt(qg, kb) * scale, -jnp.inf)
    sc = _dot_nt(qg, kxb) * scale
    sk = jnp.concatenate([jnp.full((_QB, 1), sink_ref[4 * g + r], F32) for r in range(4)], axis=0)
    m = jnp.maximum(jnp.maximum(jnp.max(s, axis=-1, keepdims=True), jnp.max(sc, axis=-1, keepdims=True)), sk)
    p = jnp.exp(s - m)
    pc = jnp.exp(sc - m)
    ps = jnp.exp(sk - m)
    inv = 1.0 / (jnp.sum(p, axis=-1, keepdims=True) + jnp.sum(pc, axis=-1, keepdims=True) + ps)
    return p * inv, pc * inv, ps * inv, qg, kb, vb, kxb, vxb


def _attn_fwd(zc, sink, t_len, l_ctx, name):
    in_specs, nb = _attn_specs(t_len, l_ctx)

    def body(sink_ref, q_ref, kp_ref, kc_ref, kn_ref, kx_ref, o_ref):
        n = pl.program_id(0)
        outs = []
        for g in range(A_KV_HEADS):
            p, pc, _, _, _, vb, _, vxb = _attn_probs(
                n, t_len, sink_ref, q_ref[...], kp_ref[...], kc_ref[...], kn_ref[...], kx_ref[...], g)
            o = _dot(p.astype(BF16), vb) + _dot(pc.astype(BF16), vxb)
            outs += [o[r * _QB:(r + 1) * _QB] for r in range(4)]
        o_ref[...] = jnp.concatenate(outs, axis=1)

    return pl.pallas_call(
        body, name=name, grid=(nb,), in_specs=in_specs,
        out_specs=pl.BlockSpec((_QB, 512), lambda n: (n, 0)),
        out_shape=jax.ShapeDtypeStruct((t_len, 512), F32),
        compiler_params=_cp("parallel"),
    )(sink, zc, zc, zc, zc, zc)


def _attn_bwd(zc, sink, o, dcat, t_len, l_ctx, name):
    rows = zc.shape[0]
    in_specs, nb = _attn_specs(t_len, l_ctx)
    in_specs = in_specs + [pl.BlockSpec((_QB, 512), lambda n: (n, 0)), pl.BlockSpec((_QB, 512), lambda n: (n, 0))]
    hd = A_HEAD_DIM
    scale = hd ** -0.5

    def body(sink_ref, q_ref, kp_ref, kc_ref, kn_ref, kx_ref, o_ref, do_ref, dq_ref, dkv_ref, dsink_ref):
        n = pl.program_id(0)

        @pl.when(n == 0)
        def _():
            dkv_ref[...] = jnp.zeros_like(dkv_ref)
            dsink_ref[...] = jnp.zeros_like(dsink_ref)

        ov = o_ref[...]
        dov = do_ref[...]
        dqs, dkbs, dvbs, dkxs, dvxs = [], [], [], [], []
        for g in range(A_KV_HEADS):
            p, pc, ps, qg, kb, vb, kxb, vxb = _attn_probs(
                n, t_len, sink_ref, q_ref[...], kp_ref[...], kc_ref[...], kn_ref[...], kx_ref[...], g)
            og = jnp.concatenate([ov[:, (4 * g + r) * hd:(4 * g + r + 1) * hd] for r in range(4)], axis=0)
            dog = jnp.concatenate([dov[:, (4 * g + r) * hd:(4 * g + r + 1) * hd] for r in range(4)], axis=0)
            delta = jnp.sum(og * dog, axis=-1, keepdims=True)
            dogb = dog.astype(BF16)
            ds = (p * (_dot_nt(dogb, vb) - delta) * scale).astype(BF16)
            dsc = (pc * (_dot_nt(dogb, vxb) - delta) * scale).astype(BF16)
            dsk = ps * (0.0 - delta)
            dqg = _dot(ds, kb) + _dot(dsc, kxb)
            dqs += [dqg[r * _QB:(r + 1) * _QB] for r in range(4)]
            dkbs.append(_dot_tn(ds, qg))
            dvbs.append(_dot_tn(p.astype(BF16), dogb))
            dkxs.append(_dot_tn(dsc, qg))
            dvxs.append(_dot_tn(pc.astype(BF16), dogb))
            for r in range(4):
                hrow = 4 * g + r
                tot = jnp.sum(dsk[r * _QB:(r + 1) * _QB], axis=0, keepdims=True)
                dsink_ref[hrow:hrow + 1, :] += jnp.broadcast_to(tot, (1, 128))
        dq_ref[...] = jnp.concatenate(dqs, axis=1)
        band = jnp.concatenate(dkbs + dvbs, axis=1)
        ctxc = jnp.concatenate(dkxs + dvxs, axis=1)
        r_prev = pl.multiple_of(jnp.maximum(n - 1, 0) * _QB, _QB)
        r_cur = pl.multiple_of(n * _QB, _QB)
        r_next = pl.multiple_of((n + 1) * _QB, _QB)
        dkv_ref[pl.ds(r_prev, _QB), :] += band[0:_QB]
        dkv_ref[pl.ds(r_cur, _QB), :] += band[_QB:2 * _QB]
        dkv_ref[pl.ds(r_next, _QB), :] += band[2 * _QB:3 * _QB]
        dkv_ref[t_len:t_len + l_ctx, :] += ctxc

    return pl.pallas_call(
        body, name=name, grid=(nb,), in_specs=in_specs,
        out_specs=[pl.BlockSpec((_QB, 512), lambda n: (n, 0)),
                   pl.BlockSpec((rows, 256), lambda n: (0, 0)),
                   pl.BlockSpec((8, 128), lambda n: (0, 0))],
        out_shape=[jax.ShapeDtypeStruct((t_len, 512), F32),
                   jax.ShapeDtypeStruct((rows, 256), F32),
                   jax.ShapeDtypeStruct((8, 128), F32)],
        compiler_params=_cp("arbitrary"),
    )(sink, zc, zc, zc, zc, zc, o, dcat)


_GC = B_CHUNK


def _gla_chunk_terms(qk, v, la, head, reverse):
    q = qk[:, head * B_DK:(head + 1) * B_DK]
    k = qk[:, 256 + head * B_DK:256 + (head + 1) * B_DK]
    vh = v[:, head * B_DV:(head + 1) * B_DV]
    off = 256 if reverse else 0
    lah = la[:, off + head * B_DK:off + (head + 1) * B_DK]
    ii = lax.broadcasted_iota(jnp.int32, (_GC, _GC), 0)
    jj = lax.broadcasted_iota(jnp.int32, (_GC, _GC), 1)
    mask = (jj >= ii) if reverse else (jj <= ii)
    tri = jnp.where(mask, 1.0, 0.0).astype(F32)
    g = _dot_hi(tri, lah)
    gl = jnp.sum(lah, axis=0, keepdims=True)
    eg = jnp.exp(g)
    eng = jnp.exp(-g)
    eend = jnp.exp(gl - g)
    sc = B_DK ** -0.5
    qt = q * (sc * eg)
    kt = k * eng
    ke = k * eend
    return q, k, vh, lah, mask, tri, g, gl, eg, eng, eend, qt, kt, ke


def _gla_fwd(zc, la, t_len, l_ctx, name):
    rows = zc.shape[0]
    n_x = t_len // _GC
    n_c = n_x + l_ctx // _GC
    qkb, vb = ZC_QK // 512, ZC_V // 512

    def ch_f(c):
        return lax.rem(c + n_x, n_c)

    def ch_r(c):
        return n_c - 1 - c

    def body(qkf_ref, vf_ref, laf_ref, qkr_ref, vr_ref, lar_ref, of_ref, or_ref, spf_ref, spr_ref, stf, strv):
        c = pl.program_id(0)

        @pl.when(c == 0)
        def _():
            stf[...] = jnp.zeros_like(stf)
            strv[...] = jnp.zeros_like(strv)

        for qk_ref, v_ref, la_ref, o_ref, sp_ref, st, reverse in (
                (qkf_ref, vf_ref, laf_ref, of_ref, spf_ref, stf, False),
                (qkr_ref, vr_ref, lar_ref, or_ref, spr_ref, strv, True)):
            qk = qk_ref[...]
            v = v_ref[...]
            la = la_ref[...]
            for hh in range(B_HEADS):
                _, _, vh, _, mask, _, _, gl, _, _, _, qt, kt, ke = _gla_chunk_terms(qk, v, la, hh, reverse)
                s_prev = st[hh]
                att = jnp.where(mask, _dot_nt(qt.astype(BF16), kt.astype(BF16)), 0.0)
                o = _dot(att.astype(BF16), vh.astype(BF16)) + _dot_nt(qt.astype(BF16), s_prev.astype(BF16))
                o_ref[:, hh * B_DV:(hh + 1) * B_DV] = o
                sp_ref[0, hh] = s_prev
                st[hh] = s_prev * jnp.exp(gl) + _dot_tn(vh.astype(BF16), ke.astype(BF16))

    st_shape = (B_HEADS, B_DV, B_DK)
    return pl.pallas_call(
        body, name=name, grid=(n_c,),
        in_specs=[pl.BlockSpec((_GC, 512), lambda c: (ch_f(c), qkb)),
                  pl.BlockSpec((_GC, 512), lambda c: (ch_f(c), vb)),
                  pl.BlockSpec((_GC, 512), lambda c: (ch_f(c), 0)),
                  pl.BlockSpec((_GC, 512), lambda c: (ch_r(c), qkb)),
                  pl.BlockSpec((_GC, 512), lambda c: (ch_r(c), vb)),
                  pl.BlockSpec((_GC, 512), lambda c: (ch_r(c), 0))],
        out_specs=[pl.BlockSpec((_GC, 512), lambda c: (ch_f(c), 0)),
                   pl.BlockSpec((_GC, 512), lambda c: (ch_r(c), 0)),
                   pl.BlockSpec((1,) + st_shape, lambda c: (c, 0, 0, 0)),
                   pl.BlockSpec((1,) + st_shape, lambda c: (c, 0, 0, 0))],
        out_shape=[jax.ShapeDtypeStruct((rows, 512), F32), jax.ShapeDtypeStruct((rows, 512), F32),
                   jax.ShapeDtypeStruct((n_c,) + st_shape, F32), jax.ShapeDtypeStruct((n_c,) + st_shape, F32)],
        scratch_shapes=[pltpu.VMEM(st_shape, F32), pltpu.VMEM(st_shape, F32)],
        compiler_params=_cp("arbitrary"),
    )(zc, zc, la, zc, zc, la)


def _gla_bwd(zc, la, spf, spr, dosum, t_len, l_ctx, name):
    rows = zc.shape[0]
    n_x = t_len // _GC
    n_c = n_x + l_ctx // _GC
    n_all = rows // _GC
    qkb, vb = ZC_QK // 512, ZC_V // 512

    def scan_of(c):
        return jnp.maximum(n_c - 1 - c, 0)

    def ch_f(c):
        return jnp.where(c < n_c, lax.rem(scan_of(c) + n_x, n_c), c)

    def ch_r(c):
        return c

    def do_of(ch):
        return jnp.minimum(ch, n_x - 1)

    def body(qkf_ref, vf_ref, laf_ref, spf_ref, dof_ref, qkr_ref, vr_ref, lar_ref, spr_ref, dor_ref,
             dqkf_ref, dvf_ref, dlaf_ref, dqkr_ref, dvr_ref, dlar_ref, dsf, dsr):
        c = pl.program_id(0)

        @pl.when(c == 0)
        def _():
            dsf[...] = jnp.zeros_like(dsf)
            dsr[...] = jnp.zeros_like(dsr)

        @pl.when(c >= n_c)
        def _():
            for r in (dqkf_ref, dvf_ref, dlaf_ref, dqkr_ref, dvr_ref, dlar_ref):
                r[...] = jnp.zeros_like(r)

        @pl.when(c < n_c)
        def _():
            for qk_ref, v_ref, la_ref, sp_ref, do_ref, dqk_ref, dv_ref, dla_ref, dst, reverse, ch in (
                    (qkf_ref, vf_ref, laf_ref, spf_ref, dof_ref, dqkf_ref, dvf_ref, dlaf_ref, dsf, False, ch_f(c)),
                    (qkr_ref, vr_ref, lar_ref, spr_ref, dor_ref, dqkr_ref, dvr_ref, dlar_ref, dsr, True, ch_r(c))):
                qk = qk_ref[...]
                v = v_ref[...]
                la = la_ref[...]
                dov = jnp.where(ch < n_x, do_ref[...], 0.0)
                sc = B_DK ** -0.5
                for hh in range(B_HEADS):
                    _, _, vh, _, mask, tri, _, gl, eg, eng, eend, qt, kt, ke = _gla_chunk_terms(qk, v, la, hh, reverse)
                    s_prev = sp_ref[0, hh]
                    ds_new = dst[hh]
                    doh = dov[:, hh * B_DV:(hh + 1) * B_DV]
                    dob = doh.astype(BF16)
                    vbh = vh.astype(BF16)
                    qtb, ktb, keb = qt.astype(BF16), kt.astype(BF16), ke.astype(BF16)
                    att = jnp.where(mask, _dot_nt(qtb, ktb), 0.0).astype(BF16)
                    datt = jnp.where(mask, _dot_nt(dob, vbh), 0.0).astype(BF16)
                    dqt = _dot(datt, ktb) + _dot(dob, s_prev.astype(BF16))
                    dkt = _dot_tn(datt, qtb)
                    dvh = _dot_tn(att, dob) + _dot_nt(keb, ds_new.astype(BF16))
                    dke = _dot(vbh, ds_new.astype(BF16))
                    egl = jnp.exp(gl)
                    dst[hh] = ds_new * egl + _dot_tn(dob, qtb)
                    dgl = (jnp.sum(dke * ke, axis=0, keepdims=True)
                           + jnp.sum(ds_new * s_prev, axis=0, keepdims=True) * egl)
                    dq = dqt * (sc * eg)
                    dk = dkt * eng + dke * eend
                    dg = dqt * qt - dkt * kt - dke * ke
                    dlah = _dot_tn_hi(tri, dg) + dgl
                    dqk_ref[:, hh * B_DK:(hh + 1) * B_DK] = dq
                    dqk_ref[:, 256 + hh * B_DK:256 + (hh + 1) * B_DK] = dk
                    dv_ref[:, hh * B_DV:(hh + 1) * B_DV] = dvh
                    dla_ref[:, hh * B_DK:(hh + 1) * B_DK] = dlah

    st_shape = (B_HEADS, B_DV, B_DK)

    def side(chf):
        return [pl.BlockSpec((_GC, 512), lambda c: (chf(c), qkb)),
                pl.BlockSpec((_GC, 512), lambda c: (chf(c), vb)),
                pl.BlockSpec((_GC, 512), lambda c: (chf(c), 0)),
                pl.BlockSpec((1,) + st_shape, lambda c: (scan_of(c), 0, 0, 0)),
                pl.BlockSpec((_GC, 512), lambda c: (do_of(chf(c)), 0))]

    def out_side(chf):
        return [pl.BlockSpec((_GC, 512), lambda c: (chf(c), 0)),
                pl.BlockSpec((_GC, 512), lambda c: (chf(c), 0)),
                pl.BlockSpec((_GC, 256), lambda c: (chf(c), 0))]

    shp = [jax.ShapeDtypeStruct((rows, 512), F32), jax.ShapeDtypeStruct((rows, 512), F32),
           jax.ShapeDtypeStruct((rows, 256), F32)]
    return pl.pallas_call(
        body, name=name, grid=(n_all,),
        in_specs=side(ch_f) + side(ch_r),
        out_specs=out_side(ch_f) + out_side(ch_r),
        out_shape=shp + shp,
        scratch_shapes=[pltpu.VMEM(st_shape, F32), pltpu.VMEM(st_shape, F32)],
        compiler_params=_cp("arbitrary"),
    )(zc, zc, la, spf, dosum, zc, zc, la, spr, dosum)


def _gla_out_fwd(o_a, o_f, o_r, zc, gla_g, t_len, name):
    tm = ROW_TILE
    rb = ZC_R // 512

    def body(oa_ref, of_ref, or_ref, r_ref, g_ref, cat_ref):
        osum = of_ref[...] + or_ref[...]
        g = g_ref[...]
        pieces = []
        for hh in range(B_HEADS):
            oh = osum[:, hh * B_DV:(hh + 1) * B_DV]
            rs = lax.rsqrt(jnp.mean(oh * oh, axis=-1, keepdims=True) + RMS_EPS)
            pieces.append((oh * rs) * g)
        r = r_ref[...]
        cat_ref[:, 0:512] = oa_ref[...].astype(BF16)
        cat_ref[:, 512:1024] = (jnp.concatenate(pieces, axis=1) * (r * _sigmoid(r))).astype(BF16)

    return pl.pallas_call(
        body, name=name, grid=(t_len // tm,),
        in_specs=[pl.BlockSpec((tm, 512), lambda i: (i, 0)),
                  pl.BlockSpec((tm, 512), lambda i: (i, 0)),
                  pl.BlockSpec((tm, 512), lambda i: (i, 0)),
                  pl.BlockSpec((tm, 512), lambda i: (i, rb)),
                  pl.BlockSpec((1, B_DV), lambda i: (0, 0))],
        out_specs=pl.BlockSpec((tm, D_MODEL), lambda i: (i, 0)),
        out_shape=jax.ShapeDtypeStruct((t_len, D_MODEL), BF16),
        compiler_params=_cp("parallel"),
    )(o_a, o_f, o_r, zc, gla_g)


def _gla_out_bwd(dcat, o_f, o_r, zc, gla_g, t_len, name):
    tm = ROW_TILE
    rb = ZC_R // 512

    def body(d_ref, of_ref, or_ref, r_ref, g_ref, dos_ref, dr_ref, dg_ref):
        i = pl.program_id(0)
        osum = of_ref[...] + or_ref[...]
        g = g_ref[...]
        r = r_ref[...]
        dgo = d_ref[...]
        sg = _sigmoid(r)
        dnrmg = dgo * (r * sg)
        nrms, dos = [], []
        dg_acc = jnp.zeros((1, B_DV), F32)
        for hh in range(B_HEADS):
            oh = osum[:, hh * B_DV:(hh + 1) * B_DV]
            rs = lax.rsqrt(jnp.mean(oh * oh, axis=-1, keepdims=True) + RMS_EPS)
            nrm = oh * rs
            dn = dnrmg[:, hh * B_DV:(hh + 1) * B_DV]
            dg_acc = dg_acc + jnp.sum(dn * nrm, axis=0, keepdims=True)
            dnn = dn * g
            dos.append(rs * (dnn - nrm * jnp.mean(dnn * nrm, axis=-1, keepdims=True)))
            nrms.append(nrm * g)
        dos_ref[...] = jnp.concatenate(dos, axis=1)
        dr_ref[...] = dgo * jnp.concatenate(nrms, axis=1) * (sg * (1.0 + r * (1.0 - sg)))

        @pl.when(i == 0)
        def _():
            dg_ref[...] = jnp.zeros_like(dg_ref)

        dg_ref[...] += dg_acc

    return pl.pallas_call(
        body, name=name, grid=(t_len // tm,),
        in_specs=[pl.BlockSpec((tm, 512), lambda i: (i, 1)),
                  pl.BlockSpec((tm, 512), lambda i: (i, 0)),
                  pl.BlockSpec((tm, 512), lambda i: (i, 0)),
                  pl.BlockSpec((tm, 512), lambda i: (i, rb)),
                  pl.BlockSpec((1, B_DV), lambda i: (0, 0))],
        out_specs=[pl.BlockSpec((tm, 512), lambda i: (i, 0)),
                   pl.BlockSpec((tm, 512), lambda i: (i, 0)),
                   pl.BlockSpec((1, B_DV), lambda i: (0, 0))],
        out_shape=[jax.ShapeDtypeStruct((t_len, 512), F32), jax.ShapeDtypeStruct((t_len, 512), F32),
                   jax.ShapeDtypeStruct((1, B_DV), F32)],
        compiler_params=_cp("arbitrary"),
    )(dcat, o_f, o_r, zc, gla_g)


def _mix_prep(dq, dkv, dqk_f, dqk_r, dv_f, dv_r, d_r, dla_f, dla_r, zc, wg2, bias2, cs, t_len, name):
    rows = zc.shape[0]
    tm = ROW_TILE
    n_x = t_len // tm
    gb = ZC_G // 128

    def xrow(i):
        return jnp.minimum(i, n_x - 1)

    def body(dq_ref, dkv_ref, dqkf_ref, dqkr_ref, dvf_ref, dvr_ref, dr_ref, dlaf_ref, dlar_ref, zg_ref, wg_ref,
             b_ref, cs_ref, dz_ref, dwg_ref, db_ref):
        i = pl.program_id(0)
        is_x = i < n_x
        cos = cs_ref[:, 0:128]
        sin = cs_ref[:, 128:256]
        cosq = jnp.concatenate([cos] * 4, axis=1)
        sinq = jnp.concatenate([sin] * 4, axis=1)
        dqv = jnp.where(is_x, dq_ref[...], 0.0)
        dz_ref[:, ZC_Q:ZC_QK] = (dqv * cosq + _swap16(dqv * sinq)).astype(BF16)
        dz_ref[:, ZC_QK:ZC_V] = (dqkf_ref[...] + dqkr_ref[...]).astype(BF16)
        dz_ref[:, ZC_V:ZC_R] = (dvf_ref[...] + dvr_ref[...]).astype(BF16)
        dz_ref[:, ZC_R:ZC_KV] = jnp.where(is_x, dr_ref[...], 0.0).astype(BF16)
        dk = dkv_ref[:, 0:128]
        dz_ref[:, ZC_KV:ZC_KV + 128] = (dk * cos + _swap16(dk * sin)).astype(BF16)
        dz_ref[:, ZC_KV + 128:ZC_G] = dkv_ref[:, 128:256].astype(BF16)
        zgb = zg_ref[...].astype(BF16)
        wg = wg_ref[...]
        pre = _dot(zgb, wg) + b_ref[...]
        dla = jnp.concatenate([dlaf_ref[...], dlar_ref[...]], axis=1)
        dpre = dla * (_sigmoid(-pre) / B_GATE_NORM)
        dpb = dpre.astype(BF16)
        dz_ref[:, ZC_G:ZC_W] = _dot_nt(dpb, wg).astype(BF16)

        @pl.when(i == 0)
        def _():
            dwg_ref[...] = jnp.zeros_like(dwg_ref)
            db_ref[...] = jnp.zeros_like(db_ref)

        dwg_ref[...] += _dot_tn(zgb, dpb)
        db_ref[...] += jnp.sum(dpre, axis=0, keepdims=True)

    return pl.pallas_call(
        body, name=name, grid=(rows // tm,),
        in_specs=[pl.BlockSpec((tm, 512), lambda i: (xrow(i), 0)),
                  pl.BlockSpec((tm, 256), lambda i: (i, 0)),
                  pl.BlockSpec((tm, 512), lambda i: (i, 0)),
                  pl.BlockSpec((tm, 512), lambda i: (i, 0)),
                  pl.BlockSpec((tm, 512), lambda i: (i, 0)),
                  pl.BlockSpec((tm, 512), lambda i: (i, 0)),
                  pl.BlockSpec((tm, 512), lambda i: (xrow(i), 0)),
                  pl.BlockSpec((tm, 256), lambda i: (i, 0)),
                  pl.BlockSpec((tm, 256), lambda i: (i, 0)),
                  pl.BlockSpec((tm, 128), lambda i: (i, gb)),
                  pl.BlockSpec((128, 512), lambda i: (0, 0)),
                  pl.BlockSpec((1, 512), lambda i: (0, 0)),
                  pl.BlockSpec((tm, 256), lambda i: (i, 0))],
        out_specs=[pl.BlockSpec((tm, ZC_W), lambda i: (i, 0)),
                   pl.BlockSpec((128, 512), lambda i: (0, 0)),
                   pl.BlockSpec((1, 512), lambda i: (0, 0))],
        out_shape=[jax.ShapeDtypeStruct((rows, ZC_W), BF16),
                   jax.ShapeDtypeStruct((128, 512), F32),
                   jax.ShapeDtypeStruct((1, 512), F32)],
        compiler_params=_cp("arbitrary"),
    )(dq, dkv, dqk_f, dqk_r, dv_f, dv_r, d_r, dla_f, dla_r, zc, wg2, bias2, cs)


def _gate_weights(w_a2_f, b_a_f, w_a2_b, b_a_b):
    wg2 = jnp.zeros((128, 512), F32)
    wg2 = wg2.at[0:B_GATE_RANK, 0:256].set(w_a2_f).at[B_GATE_RANK:2 * B_GATE_RANK, 256:512].set(w_a2_b)
    bias2 = jnp.concatenate([b_a_f, b_a_b]).reshape(1, 512)
    return wg2.astype(BF16), bias2


_WIN_PERM = ((0, 512), (768, 1280), (1280, 1792), (1792, 2304), (512, 768), (2304, 2336))


def _w_in_to_cat(w_in_full):
    parts = [w_in_full[:, a:b] for a, b in _WIN_PERM]
    parts.append(jnp.zeros((w_in_full.shape[0], ZC_W - PROJ_DIM), w_in_full.dtype))
    return jnp.concatenate(parts, axis=1)


def _cat_to_w_in(d_wcat):
    return jnp.concatenate([d_wcat[:, ZC_Q:ZC_QK], d_wcat[:, ZC_KV:ZC_G], d_wcat[:, ZC_QK:ZC_KV],
                            d_wcat[:, ZC_G:ZC_G + 2 * B_GATE_RANK]], axis=1)


def _mixer_ab_forward(x1, g3, mods, wcat, wg2, bias2, sink, gla_g, w_out, cs, t_len, l_ctx, n_x):
    h = _rms_mod_fwd(x1, g3, mods, 1, n_x, BF16, "mix0_mod")
    zc, la = _proj_fwd(h, wcat, wg2, bias2, cs, "mix0_proj")
    o_a = _attn_fwd(zc, sink, t_len, l_ctx, "mix0_attn")
    o_f, o_r, spf, spr = _gla_fwd(zc, la, t_len, l_ctx, "mix0_gla")
    cat = _gla_out_fwd(o_a, o_f, o_r, zc, gla_g, t_len, "mix0_glaout")
    x2, y = _matmul_resid(cat, w_out, x1, mods, 5, 1.0, n_x, t_len, "mix0_out")
    return x2, (x1, h, zc, la, o_a, o_f, o_r, spf, spr, cat, y)


def _mixer_ab_backward(dx2, saved, g3, mods, wcat, wg2, bias2, sink, gla_g, w_out, cs, t_len, l_ctx, n_x):
    x1, h, zc, la, o_a, o_f, o_r, spf, spr, cat, y = saved
    rows = x1.shape[0]
    tm = ROW_TILE
    dy, dgate = _gate_dy(dx2, y, mods, 5, 1.0, n_x, t_len, "mix0_dy")
    dcat = _matmul_nt(dy, w_out, "mix0_dcat")
    d_wout = _matmul_tn(
        cat, dy, pl.BlockSpec((tm, D_MODEL), lambda n, k: (k, 0)), pl.BlockSpec((tm, D_MODEL), lambda n, k: (k, 0)),
        (D_MODEL, D_MODEL), pl.BlockSpec((D_MODEL, D_MODEL), lambda n, k: (0, 0)), (1, t_len // tm), "mix0_dwout")
    dos, d_r, d_glag = _gla_out_bwd(dcat, o_f, o_r, zc, gla_g, t_len, "mix0_dglaout")
    dqk_f, dv_f, dla_f, dqk_r, dv_r, dla_r = _gla_bwd(zc, la, spf, spr, dos, t_len, l_ctx, "mix0_dgla")
    dq, dkv, dsink = _attn_bwd(zc, sink, o_a, dcat, t_len, l_ctx, "mix0_dattn")
    dzc, dwg2, dbias2 = _mix_prep(dq, dkv, dqk_f, dqk_r, dv_f, dv_r, d_r, dla_f, dla_r, zc, wg2, bias2, cs, t_len,
                                  "mix0_prep")
    d_wcat = _matmul_tn(
        h, dzc, pl.BlockSpec((tm, D_MODEL), lambda n, k: (k, 0)), pl.BlockSpec((tm, ZC_W), lambda n, k: (k, 0)),
        (D_MODEL, ZC_W), pl.BlockSpec((D_MODEL, ZC_W), lambda n, k: (0, 0)), (1, rows // tm), "mix0_dwin")
    pairs = [(dzc, pl.BlockSpec((tm, ZC_W), lambda i: (i, 0)), wcat, pl.BlockSpec((D_MODEL, ZC_W), lambda i: (0, 0)))]
    dx1, stats = _bwd_dx(pairs, x1, dx2, t_len // tm, g3, mods, 1, n_x, "mix0_dx")
    return dx1, stats, dgate, d_wcat, dwg2, dbias2, dsink, d_glag, d_wout


_PT = 256
_PH = 16


def _pool_band(n, t_len, w, transpose):
    shape = (_PT, _PT + 2 * _PH)
    a = n * _PT + lax.broadcasted_iota(jnp.int32, shape, 0)
    b = n * _PT - _PH + lax.broadcasted_iota(jnp.int32, shape, 1)
    t, s = (b, a) if transpose else (a, b)
    lo = jnp.maximum(t - w // 2, 0)
    hi = jnp.minimum(t + (w - w // 2), t_len)
    inside = (s >= lo) & (s < hi) & (t >= 0) & (t < t_len)
    mean = jnp.where(inside, 1.0 / (hi - lo).astype(F32), 0.0)
    return mean - jnp.where(s == t, 1.0, 0.0)


def _pool_halo(p_ref, c_ref, n_ref):
    return jnp.concatenate([p_ref[_PT - _PH:_PT, :], c_ref[...], n_ref[0:_PH, :]], axis=0)


def _pool_specs(t_len):
    nb = t_len // _PT
    return [pl.BlockSpec((_PT, D_MODEL), lambda n: (jnp.maximum(n - 1, 0), 0)),
            pl.BlockSpec((_PT, D_MODEL), lambda n: (n, 0)),
            pl.BlockSpec((_PT, D_MODEL), lambda n: (jnp.minimum(n + 1, nb - 1), 0))], nb


def _pool_fwd(h, wp, pscale, x1, mods, t_len, name):
    halo_specs, nb = _pool_specs(t_len)

    def body(hp_ref, hc_ref, hn_ref, w_ref, ps_ref, x_ref, m_ref, x2_ref, pooled_ref, ypre_ref):
        n = pl.program_id(0)
        hcat = _pool_halo(hp_ref, hc_ref, hn_ref)
        ys = []
        for gi, w in enumerate(POOL_WINDOWS):
            cols = slice(gi * POOL_GROUP, (gi + 1) * POOL_GROUP)
            pooled = _dot_hi(_pool_band(n, t_len, w, False), hcat[:, cols]).astype(BF16)
            pooled_ref[:, cols] = pooled
            ys.append(_dot(pooled, w_ref[gi]))
        ypre = jnp.concatenate(ys, axis=1)
        ypre_ref[...] = ypre
        x2_ref[...] = x_ref[...] + m_ref[0, 5:6, :] * (ypre * ps_ref[...])

    return pl.pallas_call(
        body, name=name, grid=(nb,),
        in_specs=halo_specs + [pl.BlockSpec((4, POOL_GROUP, POOL_GROUP), lambda n: (0, 0, 0)),
                               pl.BlockSpec((1, D_MODEL), lambda n: (0, 0)),
                               pl.BlockSpec((_PT, D_MODEL), lambda n: (n, 0)),
                               pl.BlockSpec((1, N_MOD, D_MODEL), lambda n: (0, 0, 0))],
        out_specs=[pl.BlockSpec((_PT, D_MODEL), lambda n: (n, 0))] * 3,
        out_shape=[jax.ShapeDtypeStruct((t_len, D_MODEL), F32), jax.ShapeDtypeStruct((t_len, D_MODEL), BF16),
                   jax.ShapeDtypeStruct((t_len, D_MODEL), F32)],
        compiler_params=_cp("parallel"),
    )(h, h, h, wp, pscale, x1, mods)


def _pool_bwd_a(dx2, ypre, wp, pscale, mods, t_len, name):
    nb = t_len // _PT

    def body(d_ref, y_ref, w_ref, ps_ref, m_ref, dyp_ref, dpl_ref, dgate_ref, dps_ref):
        n = pl.program_id(0)
        dv = d_ref[...]
        ypre = y_ref[...]
        ps = ps_ref[...]
        dy = dv * m_ref[0, 5:6, :]
        dyp = (dy * ps).astype(BF16)
        dyp_ref[...] = dyp
        for gi in range(len(POOL_WINDOWS)):
            cols = slice(gi * POOL_GROUP, (gi + 1) * POOL_GROUP)
            dpl_ref[:, cols] = _dot_nt(dyp[:, cols], w_ref[gi])

        @pl.when(n == 0)
        def _():
            dgate_ref[...] = jnp.zeros_like(dgate_ref)
            dps_ref[...] = jnp.zeros_like(dps_ref)

        dgate_ref[...] += jnp.sum(dv * (ypre * ps), axis=0, keepdims=True)
        dps_ref[...] += jnp.sum(dy * ypre, axis=0, keepdims=True)

    return pl.pallas_call(
        body, name=name, grid=(nb,),
        in_specs=[pl.BlockSpec((_PT, D_MODEL), lambda n: (n, 0)),
                  pl.BlockSpec((_PT, D_MODEL), lambda n: (n, 0)),
                  pl.BlockSpec((4, POOL_GROUP, POOL_GROUP), lambda n: (0, 0, 0)),
                  pl.BlockSpec((1, D_MODEL), lambda n: (0, 0)),
                  pl.BlockSpec((1, N_MOD, D_MODEL), lambda n: (0, 0, 0))],
        out_specs=[pl.BlockSpec((_PT, D_MODEL), lambda n: (n, 0)),
                   pl.BlockSpec((_PT, D_MODEL), lambda n: (n, 0)),
                   pl.BlockSpec((1, D_MODEL), lambda n: (0, 0)),
                   pl.BlockSpec((1, D_MODEL), lambda n: (0, 0))],
        out_shape=[jax.ShapeDtypeStruct((t_len, D_MODEL), BF16), jax.ShapeDtypeStruct((t_len, D_MODEL), F32),
                   jax.ShapeDtypeStruct((1, D_MODEL), F32), jax.ShapeDtypeStruct((1, D_MODEL), F32)],
        compiler_params=_cp("arbitrary"),
    )(dx2, ypre, wp, pscale, mods)


def _pool_bwd_dx(dpl, x1, dx2, g3, mods, t_len, name):
    halo_specs, nb = _pool_specs(t_len)

    def body(dp_ref, dc_ref, dn_ref, x_ref, d_ref, g_ref, m_ref, dx_ref, acc_ref):
        n = pl.program_id(0)
        dcat = _pool_halo(dp_ref, dc_ref, dn_ref)
        dhs = []
        for gi, w in enumerate(POOL_WINDOWS):
            cols = slice(gi * POOL_GROUP, (gi + 1) * POOL_GROUP)
            dhs.append(_dot_hi(_pool_band(n, t_len, w, True), dcat[:, cols]))
        dh = jnp.concatenate(dhs, axis=1)
        g = g_ref[1:2, :]
        scale = m_ref[0, 4:5, :]
        dx = _rms_mod_bwd_tail(dh, x_ref[...], g, scale, 0, acc_ref, n == 0)
        dx_ref[...] = d_ref[...] + dx

    return pl.pallas_call(
        body, name=name, grid=(nb,),
        in_specs=halo_specs + [pl.BlockSpec((_PT, D_MODEL), lambda n: (n, 0)),
                               pl.BlockSpec((_PT, D_MODEL), lambda n: (n, 0)),
                               pl.BlockSpec((3, D_MODEL), lambda n: (0, 0)),
                               pl.BlockSpec((1, N_MOD, D_MODEL), lambda n: (0, 0, 0))],
        out_specs=[pl.BlockSpec((_PT, D_MODEL), lambda n: (n, 0)),
                   pl.BlockSpec((2, 3, D_MODEL), lambda n: (0, 0, 0))],
        out_shape=[jax.ShapeDtypeStruct((t_len, D_MODEL), F32), jax.ShapeDtypeStruct((2, 3, D_MODEL), F32)],
        compiler_params=_cp("arbitrary"),
    )(dpl, dpl, dpl, x1, dx2, g3, mods)


def _mixer_pool_forward(x1, g3, mods, wp, pscale, t_len):
    h = _rms_mod_fwd(x1, g3, mods, 1, t_len // ROW_TILE, F32, "mix1_mod")
    x2, pooled, ypre = _pool_fwd(h, wp, pscale, x1, mods, t_len, "mix1_pool")
    return x2, (x1, pooled, ypre)


def _mixer_pool_backward(dx2, saved, g3, mods, wp, pscale, t_len):
    x1, pooled, ypre = saved
    tm = ROW_TILE
    dyp, dpl, dgate, dps = _pool_bwd_a(dx2, ypre, wp, pscale, mods, t_len, "mix1_da")
    d_wp = _matmul_tn(
        pooled, dyp, pl.BlockSpec((tm, POOL_GROUP), lambda g, k: (k, g)),
        pl.BlockSpec((tm, POOL_GROUP), lambda g, k: (k, g)),
        (4, POOL_GROUP, POOL_GROUP), pl.BlockSpec((1, POOL_GROUP, POOL_GROUP), lambda g, k: (g, 0, 0)),
        (4, t_len // tm), "mix1_dwp")
    dx1, stats = _pool_bwd_dx(dpl, x1, dx2, g3, mods, t_len, "mix1_dx")
    return dx1, stats, dgate, dps, d_wp


def _final_loss(x3, final_g, target, name):
    t_len = x3.shape[0]
    tm = ROW_TILE

    def body(x_ref, g_ref, t_ref, dx_ref, loss_ref, dg_ref):
        i = pl.program_id(0)
        xv = x_ref[...]
        g = g_ref[...]
        r = lax.rsqrt(jnp.mean(xv * xv, axis=-1, keepdims=True) + RMS_EPS)
        xhat = xv * r
        err = xhat * g - t_ref[...]
        part = 0.5 * jnp.sum(jnp.mean(err * err, axis=-1, keepdims=True), axis=0, keepdims=True)
        dy = err * (1.0 / D_MODEL)

        @pl.when(i == 0)
        def _():
            loss_ref[...] = jnp.zeros_like(loss_ref)
            dg_ref[...] = jnp.zeros_like(dg_ref)

        loss_ref[...] += jnp.broadcast_to(part, (1, 128))
        dg_ref[...] += jnp.sum(dy * xhat, axis=0, keepdims=True)
        dxh = dy * g
        dx_ref[...] = r * (dxh - xhat * jnp.mean(dxh * xhat, axis=-1, keepdims=True))

    return pl.pallas_call(
        body, name=name, grid=(t_len // tm,),
        in_specs=[pl.BlockSpec((tm, D_MODEL), lambda i: (i, 0)),
                  pl.BlockSpec((1, D_MODEL), lambda i: (0, 0)),
                  pl.BlockSpec((tm, D_MODEL), lambda i: (i, 0))],
        out_specs=[pl.BlockSpec((tm, D_MODEL), lambda i: (i, 0)),
                   pl.BlockSpec((1, 128), lambda i: (0, 0)),
                   pl.BlockSpec((1, D_MODEL), lambda i: (0, 0))],
        out_shape=[jax.ShapeDtypeStruct((t_len, D_MODEL), F32), jax.ShapeDtypeStruct((1, 128), F32),
                   jax.ShapeDtypeStruct((1, D_MODEL), F32)],
        compiler_params=_cp("arbitrary"),
    )(x3, final_g, target)


_CROWS = 16


def _adaln_fwd(c16, w_mod, bias_k, name):
    n_l, _, cols = w_mod.shape

    def body(c_ref, w_ref, b_ref, o_ref):
        cv = c_ref[...]
        sc = (cv * _sigmoid(cv)).astype(BF16)
        o_ref[0] = _dot(sc, w_ref[0].astype(BF16)) + b_ref[0]

    return pl.pallas_call(
        body, name=name, grid=(n_l,),
        in_specs=[pl.BlockSpec((_CROWS, D_MODEL), lambda l: (0, 0)),
                  pl.BlockSpec((1, D_MODEL, cols), lambda l: (l, 0, 0)),
                  pl.BlockSpec((1, 1, cols), lambda l: (l, 0, 0))],
        out_specs=pl.BlockSpec((1, _CROWS, cols), lambda l: (l, 0, 0)),
        out_shape=jax.ShapeDtypeStruct((n_l, _CROWS, cols), F32),
        compiler_params=_cp("parallel"),
    )(c16, w_mod, bias_k)


def _adaln_bwd(c16, d16, w_mod, dmmc_k, name):
    n_l, _, cols = w_mod.shape

    def body(c_ref, d_ref, w_ref, dm_ref, gw_ref, cp_ref):
        layer = pl.program_id(0)
        cv = c_ref[...]
        gw_ref[0] = _dot_tn_hi(cv * _sigmoid(cv), d_ref[0])

        @pl.when(layer == 0)
        def _():
            cp_ref[...] = jnp.sum(w_ref[0] * dm_ref[...], axis=1, keepdims=True)

    return pl.pallas_call(
        body, name=name, grid=(n_l,),
        in_specs=[pl.BlockSpec((_CROWS, D_MODEL), lambda l: (0, 0)),
                  pl.BlockSpec((1, _CROWS, cols), lambda l: (l, 0, 0)),
                  pl.BlockSpec((1, D_MODEL, cols), lambda l: (0, 0, 0)),
                  pl.BlockSpec((1, cols), lambda l: (0, 0))],
        out_specs=[pl.BlockSpec((1, D_MODEL, cols), lambda l: (l, 0, 0)),
                   pl.BlockSpec((D_MODEL, 1), lambda l: (0, 0))],
        out_shape=[jax.ShapeDtypeStruct((n_l, D_MODEL, cols), F32), jax.ShapeDtypeStruct((D_MODEL, 1), F32)],
        compiler_params=_cp("arbitrary"),
    )(c16, d16, w_mod, dmmc_k)


def _cctx_grad(cparts, c_ctx2, name):
    def body(p_ref, c_ref, o_ref):
        tot = ((p_ref[0] + p_ref[2]) + p_ref[4]) + p_ref[6]
        cv = c_ref[...]
        sg = _sigmoid(cv)
        o_ref[...] = tot * (sg * (1.0 + cv * (1.0 - sg)))

    return pl.pallas_call(
        body, name=name, out_shape=jax.ShapeDtypeStruct((8, 128), F32),
        in_specs=[pl.BlockSpec(memory_space=pltpu.VMEM), pl.BlockSpec(memory_space=pltpu.VMEM)],
        out_specs=pl.BlockSpec(memory_space=pltpu.VMEM),
    )(cparts, c_ctx2)


def _sum_devices(ga, name):
    def body(g_ref, o_ref):
        acc = g_ref[0]
        for d in range(1, N_DEV):
            acc = acc + g_ref[d]
        o_ref[...] = acc

    return pl.pallas_call(
        body, name=name, out_shape=jax.ShapeDtypeStruct(ga.shape[1:], F32),
        in_specs=[pl.BlockSpec(memory_space=pltpu.VMEM)], out_specs=pl.BlockSpec(memory_space=pltpu.VMEM),
    )(ga)


def _place():
    return lax.axis_index("x"), lax.axis_index("y"), lax.axis_index("c")


def _flip(a, d):
    return 1 - a if d else a


_CHIP_FLIPS = ((1, 0), (0, 1), (1, 1))


def _allgather_small(v, name):
    r, cc = v.shape

    def body(v_ref, out_ref, send_sems, recv_sems, local_sem):
        x, y, c = _place()
        me = 4 * x + 2 * y + c
        mine = pltpu.make_async_copy(v_ref, out_ref.at[me], local_sem)
        mine.start()
        sends = []
        for k in range(1, N_DEV):
            peer = (_flip(x, (k >> 2) & 1), _flip(y, (k >> 1) & 1), _flip(c, k & 1))
            cp = pltpu.make_async_remote_copy(src_ref=v_ref, dst_ref=out_ref.at[me], send_sem=send_sems.at[k - 1],
                                              recv_sem=recv_sems.at[k - 1], device_id=peer, device_id_type=MESH)
            cp.start()
            sends.append(cp)
        for k in range(1, N_DEV):
            px, py, pc = _flip(x, (k >> 2) & 1), _flip(y, (k >> 1) & 1), _flip(c, k & 1)
            pltpu.make_async_remote_copy(src_ref=v_ref, dst_ref=out_ref.at[4 * px + 2 * py + pc],
                                         send_sem=send_sems.at[k - 1], recv_sem=recv_sems.at[k - 1],
                                         device_id=(px, py, pc), device_id_type=MESH).wait_recv()
        for cp in sends:
            cp.wait_send()
        mine.wait()

    return pl.pallas_call(
        body, name=name, out_shape=jax.ShapeDtypeStruct((N_DEV, r, cc), F32),
        in_specs=[pl.BlockSpec(memory_space=pltpu.VMEM)], out_specs=pl.BlockSpec(memory_space=pltpu.VMEM),
        scratch_shapes=[pltpu.SemaphoreType.DMA((N_DEV - 1,)), pltpu.SemaphoreType.DMA((N_DEV - 1,)),
                        pltpu.SemaphoreType.DMA],
        compiler_params=pltpu.CompilerParams(vmem_limit_bytes=VMEM_LIMIT_BYTES),
    )(v)


_HBM_SPEC = pl.BlockSpec(memory_space=pltpu.HBM)
_SEM_SPEC = pl.BlockSpec(memory_space=pltpu.SEMAPHORE)
_EFFECT = pltpu.SideEffectType.DATAFLOW_SIDE_EFFECTING


def _in_hbm(a):
    return pltpu.with_memory_space_constraint(a, pltpu.HBM)


def _gather_start(arrs, groups, name):
    n, n_g = len(arrs), len(groups)

    def body(*refs):
        ins, zones = refs[:n], refs[n:2 * n]
        sems = refs[2 * n:2 * n + 2 * n_g]
        token = refs[2 * n + 2 * n_g + 2 * n]
        local_sems = refs[-1]
        x, y, c = _place()
        k_me = 2 * x + y
        local_copies = []
        for a in range(n):
            lc = pltpu.make_async_copy(ins[a], zones[a].at[k_me], local_sems.at[a])
            lc.start()
            local_copies.append(lc)
        for g, members in enumerate(groups):
            for t, a in enumerate(members):
                for j, (dx, dy) in enumerate(_CHIP_FLIPS):
                    pltpu.make_async_remote_copy(
                        src_ref=ins[a], dst_ref=zones[a].at[k_me], send_sem=sems[2 * g].at[3 * t + j],
                        recv_sem=sems[2 * g + 1].at[3 * t + j], device_id=(_flip(x, dx), _flip(y, dy), c),
                        device_id_type=MESH).start()
        for lc in local_copies:
            lc.wait()
        token[...] = jnp.zeros_like(token)

    zones = [lax.empty((N_CHIPS,) + a.shape, a.dtype) for a in arrs]
    sem_shapes = []
    for members in groups:
        sem_shapes += [pltpu.SemaphoreType.DMA((3 * len(members),))] * 2
    outs = pl.pallas_call(
        body, name=name,
        out_shape=sem_shapes + [pltpu.HBM(a.shape, a.dtype) for a in arrs]
        + [pltpu.HBM(z.shape, z.dtype) for z in zones] + [jax.ShapeDtypeStruct((8, 128), F32)],
        in_specs=[_HBM_SPEC] * (2 * n),
        out_specs=[_SEM_SPEC] * (2 * n_g) + [_HBM_SPEC] * (2 * n) + [pl.BlockSpec(memory_space=pltpu.VMEM)],
        input_output_aliases={i: 2 * n_g + i for i in range(2 * n)},
        scratch_shapes=[pltpu.SemaphoreType.DMA((n,))],
        compiler_params=pltpu.CompilerParams(has_side_effects=_EFFECT),
    )(*[_in_hbm(a) for a in arrs], *[_in_hbm(z) for z in zones])
    sems = outs[:2 * n_g]
    thru = outs[2 * n_g:2 * n_g + n]
    zones = outs[2 * n_g + n:2 * n_g + 2 * n]
    return [(sems[2 * g], sems[2 * g + 1]) for g in range(n_g)], thru, zones, outs[-1]


def _gather_wait(shards, zones, send_sems, recv_sems, after, name):
    m = len(shards)

    def body(*refs):
        ins, zs = refs[:m], refs[m:2 * m]
        ssem, rsem = refs[2 * m], refs[2 * m + 1]
        x, y, c = _place()
        for t in range(m):
            for j, (dx, dy) in enumerate(_CHIP_FLIPS):
                px, py = _flip(x, dx), _flip(y, dy)
                cp = pltpu.make_async_remote_copy(
                    src_ref=ins[t], dst_ref=zs[t].at[2 * px + py], send_sem=ssem.at[3 * t + j],
                    recv_sem=rsem.at[3 * t + j], device_id=(px, py, c), device_id_type=MESH)
                cp.wait_send()
                cp.wait_recv()

    outs = pl.pallas_call(
        body, name=name,
        out_shape=[pltpu.HBM(a.shape, a.dtype) for a in list(shards) + list(zones)],
        in_specs=[_HBM_SPEC] * (2 * m) + [_SEM_SPEC, _SEM_SPEC, pl.BlockSpec(memory_space=pl.ANY)],
        out_specs=[_HBM_SPEC] * (2 * m),
        input_output_aliases={i: i for i in range(2 * m)},
        compiler_params=pltpu.CompilerParams(has_side_effects=_EFFECT),
    )(*shards, *zones, send_sems, recv_sems, after)
    return outs[m:]


def _scatter_start(arrs, name):
    n = len(arrs)

    def body(*refs):
        ins, lands = refs[:n], refs[n:2 * n]
        ssem, rsem = refs[2 * n], refs[2 * n + 1]
        token = refs[2 * n + 2 + 2 * n]
        x, y, c = _place()
        for a in range(n):
            for j, (dx, dy) in enumerate(_CHIP_FLIPS):
                px, py = _flip(x, dx), _flip(y, dy)
                pltpu.make_async_remote_copy(
                    src_ref=ins[a].at[2 * px + py], dst_ref=lands[a].at[j], send_sem=ssem.at[3 * a + j],
                    recv_sem=rsem.at[3 * a + j], device_id=(px, py, c), device_id_type=MESH).start()
        token[...] = jnp.zeros_like(token)

    lands = [lax.empty((3,) + a.shape[1:], a.dtype) for a in arrs]
    outs = pl.pallas_call(
        body, name=name,
        out_shape=[pltpu.SemaphoreType.DMA((3 * n,))] * 2 + [pltpu.HBM(a.shape, a.dtype) for a in arrs]
        + [pltpu.HBM(z.shape, z.dtype) for z in lands] + [jax.ShapeDtypeStruct((8, 128), F32)],
        in_specs=[_HBM_SPEC] * (2 * n),
        out_specs=[_SEM_SPEC] * 2 + [_HBM_SPEC] * (2 * n) + [pl.BlockSpec(memory_space=pltpu.VMEM)],
        input_output_aliases={i: 2 + i for i in range(2 * n)},
        compiler_params=pltpu.CompilerParams(has_side_effects=_EFFECT),
    )(*[_in_hbm(a) for a in arrs], *[_in_hbm(z) for z in lands])
    return outs[0], outs[1], outs[2:2 + n], outs[2 + n:2 + 2 * n], outs[-1]


def _scatter_wait(arrs, lands, send_sems, recv_sems, after, name):
    n = len(arrs)

    def body(*refs):
        ins, lz = refs[:n], refs[n:2 * n]
        ssem, rsem = refs[2 * n], refs[2 * n + 1]
        x, y, c = _place()
        for a in range(n):
            for j, (dx, dy) in enumerate(_CHIP_FLIPS):
                px, py = _flip(x, dx), _flip(y, dy)
                cp = pltpu.make_async_remote_copy(
                    src_ref=ins[a].at[2 * px + py], dst_ref=lz[a].at[j], send_sem=ssem.at[3 * a + j],
                    recv_sem=rsem.at[3 * a + j], device_id=(px, py, c), device_id_type=MESH)
                cp.wait_send()
                cp.wait_recv()

    outs = pl.pallas_call(
        body, name=name,
        out_shape=[pltpu.HBM(a.shape, a.dtype) for a in list(arrs) + list(lands)],
        in_specs=[_HBM_SPEC] * (2 * n) + [_SEM_SPEC, _SEM_SPEC, pl.BlockSpec(memory_space=pl.ANY)],
        out_specs=[_HBM_SPEC] * (2 * n),
        input_output_aliases={i: i for i in range(2 * n)},
        compiler_params=pltpu.CompilerParams(has_side_effects=_EFFECT),
    )(*arrs, *lands, send_sems, recv_sems, after)
    return outs[n:]


def _swap_sibling(arrs, name):
    n = len(arrs)

    def body(*refs):
        ins, outs = refs[:n], refs[n:2 * n]
        send_sems, recv_sems = refs[2 * n:]
        x, y, c = _place()
        sends = []
        for a in range(n):
            cp = pltpu.make_async_remote_copy(src_ref=ins[a], dst_ref=outs[a], send_sem=send_sems.at[a],
                                              recv_sem=recv_sems.at[a], device_id=(x, y, 1 - c), device_id_type=MESH)
            cp.start()
            sends.append(cp)
        for cp in sends:
            cp.wait()

    any_spec = pl.BlockSpec(memory_space=pl.ANY)
    return pl.pallas_call(
        body, name=name,
        out_shape=[jax.ShapeDtypeStruct(a.shape, a.dtype) for a in arrs],
        in_specs=[any_spec] * n, out_specs=[any_spec] * n,
        scratch_shapes=[pltpu.SemaphoreType.DMA((n,)), pltpu.SemaphoreType.DMA((n,))],
    )(*arrs)


def _row_tile(rows, cols):
    for tr in (1024, 512, 256, 128, 64, 32, 16, 8):
        if rows % tr == 0 and tr * cols * 4 <= (1 << 20):
            return tr
    return rows


def _partial_sum(g_full, recv, k_idx, name):
    _, r, c = g_full.shape
    tr = _row_tile(r, c)

    def body(k_ref, g_ref, r_ref, o_ref):
        del k_ref
        acc = g_ref[0]
        for j in range(3):
            acc = acc + r_ref[j].astype(F32)
        o_ref[...] = acc

    return pl.pallas_call(
        body, name=name,
        grid_spec=pltpu.PrefetchScalarGridSpec(
            num_scalar_prefetch=1, grid=(r // tr,),
            in_specs=[pl.BlockSpec((1, tr, c), lambda i, k: (k[0], i, 0)),
                      pl.BlockSpec((3, tr, c), lambda i, k: (0, i, 0))],
            out_specs=pl.BlockSpec((tr, c), lambda i, k: (i, 0))),
        out_shape=jax.ShapeDtypeStruct((r, c), F32),
        compiler_params=_cp("parallel"),
    )(k_idx, g_full, recv)


def _adamw(w3, parts, m3, v3, layer, prev, name):
    n_l, r, c = w3.shape
    tr = _row_tile(r, c)
    n_i = r // tr
    n_p = len(parts)
    c1 = 1.0 - ADAM_B1 ** ADAM_STEP
    c2 = 1.0 - ADAM_B2 ** ADAM_STEP
    stacked = [isinstance(p, tuple) for p in parts]

    def body(*refs):
        w_ref, m_ref, v_ref = refs[0:3]
        g_refs = refs[3:3 + n_p]
        go_ref, d_ref, mo_ref, vo_ref = refs[-4:]
        g = None
        for p in range(n_p):
            term = g_refs[p][0] if stacked[p] else g_refs[p][...]
            g = term if g is None else g + term
        w = w_ref[0]
        m = ADAM_B1 * m_ref[0] + (1.0 - ADAM_B1) * g
        v = ADAM_B2 * v_ref[0] + (1.0 - ADAM_B2) * (g * g)
        m_hat = m / c1
        v_hat = v / c2
        go_ref[0] = g
        d_ref[0] = -ADAM_LR * (m_hat / (jnp.sqrt(v_hat) + ADAM_EPS) + ADAM_WD * w)
        mo_ref[0] = m
        vo_ref[0] = v

    blk = pl.BlockSpec((1, tr, c), lambda i: (layer, i, 0))
    in_specs = [blk, blk, blk]
    args = [w3, m3, v3]
    for part in parts:
        if isinstance(part, tuple):
            in_specs.append(pl.BlockSpec((1, tr, c), functools.partial(lambda idx, i: (idx, i, 0), part[1])))
            args.append(part[0])
        else:
            in_specs.append(pl.BlockSpec((tr, c), lambda i: (i, 0)))
            args.append(part)
    aliases = {}
    if prev is not None:
        in_specs += [pl.BlockSpec(memory_space=pl.ANY)] * 4
        aliases = {len(args) + q: q for q in range(4)}
        args += list(prev)
    shp = jax.ShapeDtypeStruct((n_l, r, c), F32)
    return pl.pallas_call(
        body, name=name, grid=(n_i,), in_specs=in_specs, out_specs=[blk] * 4, out_shape=[shp] * 4,
        input_output_aliases=aliases, compiler_params=_cp("parallel"),
    )(*args)


_SMALL_W = 4096
_PACK_ROWS = 352
_N9 = N_MOD * D_MODEL


def _flat_pad(parts, total):
    flat = jnp.concatenate([p.reshape(-1) for p in parts])
    return jnp.concatenate([flat, jnp.zeros((total - flat.shape[0],), F32)])


def kernel(x, c, ctx, c_ctx, w_mod, b_mod, norm_g, ffn1_wi, ffn1_wo, ffn2_wi, ffn2_wo, w_in, w_a2_f, b_a_f, w_a2_b, b_a_b, sink, gla_g, w_out, w_pool, pool_scale, final_g, loss_target, m_c_ctx, m_w_mod, m_b_mod, m_norm_g, m_ffn1_wi, m_ffn1_wo, m_ffn2_wi, m_ffn2_wo, m_w_in, m_w_a2_f, m_b_a_f, m_w_a2_b, m_b_a_b, m_sink, m_gla_g, m_w_out, m_w_pool, m_pool_scale, m_final_g, v_c_ctx, v_w_mod, v_b_mod, v_norm_g, v_ffn1_wi, v_ffn1_wo, v_ffn2_wi, v_ffn2_wo, v_w_in, v_w_a2_f, v_b_a_f, v_w_a2_b, v_b_a_b, v_sink, v_gla_g, v_w_out, v_w_pool, v_pool_scale, v_final_g):
    t_len, l_ctx = x.shape[1], ctx.shape[1]
    tm = ROW_TILE
    pad = (-(t_len + l_ctx)) % tm
    rows0 = t_len + l_ctx + pad
    n_x = t_len // tm
    xi, yi, ci = _place()
    k_me = 2 * xi + yi
    me = 4 * xi + 2 * yi + ci
    mod_cols = w_mod.shape[2]
    n_grp = len(POOL_WINDOWS)

    small_w = _flat_pad([norm_g, w_a2_f, w_a2_b, pool_scale], _SMALL_W).reshape(_SMALL_W // 128, 128)
    shards = [ffn1_wi[0], ffn1_wi[1], ffn1_wo[0], ffn1_wo[1], ffn2_wi[0], ffn2_wi[1], ffn2_wo[0], ffn2_wo[1],
              w_in[0], w_out[0], w_pool[0].reshape(n_grp * w_pool.shape[2], POOL_GROUP)]
    groups = ([11, 0, 2], [8, 9], [4, 6], [1, 3], [10, 5, 7])
    g_sems, g_thru, g_zones, g_token = _gather_start([s.astype(BF16) for s in shards] + [small_w], groups,
                                                     "gather_start")

    def gather_wait(g, after):
        members = groups[g]
        got = _gather_wait([g_thru[a] for a in members], [g_zones[a] for a in members], g_sems[g][0], g_sems[g][1],
                           after, "gather_wait_%d" % g)
        return dict(zip(members, got))

    gathered = gather_wait(0, g_token)
    c = c + g_token[0:1, 0:1]
    sw = gathered[11].reshape(N_CHIPS, _SMALL_W)
    ng_n = norm_g.size
    a2_n = w_a2_f.size
    norm_g_full = jnp.concatenate([sw[k, :ng_n].reshape(norm_g.shape) for k in range(N_CHIPS)], axis=-1)
    w_a2_f_full = jnp.concatenate([sw[k, ng_n:ng_n + a2_n].reshape(w_a2_f.shape[1:]) for k in range(N_CHIPS)], axis=-1)
    w_a2_b_full = jnp.concatenate(
        [sw[k, ng_n + a2_n:ng_n + 2 * a2_n].reshape(w_a2_b.shape[1:]) for k in range(N_CHIPS)], axis=-1)
    pscale_full = jnp.concatenate(
        [sw[k, ng_n + 2 * a2_n:ng_n + 2 * a2_n + pool_scale.size] for k in range(N_CHIPS)]).reshape(1, D_MODEL)
    wg2, bias2 = _gate_weights(w_a2_f_full, b_a_f[0], w_a2_b_full, b_a_b[0])
    gla_g2 = gla_g.reshape(1, B_DV)
    final_g2 = final_g.reshape(1, D_MODEL)
    cs = _rope_tables(t_len, rows0)

    c_all = _allgather_small(c.reshape(8, 128), "gather_cond").reshape(N_DEV, D_MODEL)
    c16 = jnp.concatenate([c_all, c_ctx[None], jnp.zeros((_CROWS - N_DEV - 1, D_MODEL), F32)], axis=0)
    bias_k = lax.dynamic_slice(b_mod, (0, k_me * mod_cols), (2, mod_cols)).reshape(2, 1, mod_cols)
    mm_k = _adaln_fwd(c16, w_mod, bias_k, "adaln_fwd")
    mm_all = _allgather_small(mm_k.reshape(-1, 128), "gather_mod").reshape(N_DEV, 2, _CROWS, mod_cols)
    mm_full = jnp.concatenate([mm_all[2 * k] for k in range(N_CHIPS)], axis=-1)
    mm_x = lax.dynamic_index_in_dim(mm_full, me, axis=1, keepdims=False)
    mm_c = mm_full[:, N_DEV]
    mods = [jnp.stack([mm_x[l].reshape(N_MOD, D_MODEL), mm_c[l].reshape(N_MOD, D_MODEL)]) for l in range(2)]
    g3 = [norm_g_full[0], norm_g_full[1]]

    xcat = jnp.concatenate([x[0], ctx[0], jnp.zeros((pad, D_MODEL), F32)], axis=0)
    w1i, w1o, w2i, w2o = [None, None], [None, None], [None, None], [None, None]
    w1i[0], w1o[0] = gathered[0], gathered[2]
    x1, sv_a1 = _ffn_forward(xcat, g3[0], mods[0], 0, w1i[0], w1o[0], n_x, "l0_ffn1")
    gathered = gather_wait(1, x1)
    w_in_full = jnp.concatenate([gathered[8][k] for k in range(N_CHIPS)], axis=1)
    wcat = _w_in_to_cat(w_in_full)
    w_out_full = gathered[9].reshape(D_MODEL, D_MODEL)
    x2, sv_am = _mixer_ab_forward(x1, g3[0], mods[0], wcat, wg2, bias2, sink[0], gla_g2, w_out_full, cs,
                                  t_len, l_ctx, n_x)
    gathered = gather_wait(2, x2)
    w2i[0], w2o[0] = gathered[4], gathered[6]
    x3, sv_a2 = _ffn_forward(x2, g3[0], mods[0], 2, w2i[0], w2o[0], n_x, "l0_ffn2")
    gathered = gather_wait(3, x3)
    w1i[1], w1o[1] = gathered[1], gathered[3]
    x4, sv_b1 = _ffn_forward(x3, g3[1], mods[1], 0, w1i[1], w1o[1], n_x, "l1_ffn1")
    gathered = gather_wait(4, x4)
    w2i[1], w2o[1] = gathered[5], gathered[7]
    wp_full = gathered[10].reshape(N_CHIPS, n_grp, -1, POOL_GROUP).transpose(1, 0, 2, 3).reshape(
        n_grp, POOL_GROUP, POOL_GROUP)
    x5, sv_bm = _mixer_pool_forward(x4, g3[1], mods[1], wp_full, pscale_full, t_len)
    x6, sv_b2 = _ffn_forward(x5, g3[1], mods[1], 2, w2i[1], w2o[1], n_x, "l1_ffn2")
    dx6, loss_part, d_final_g = _final_loss(x6, final_g2, loss_target[0], "final_loss")
    loss = lax.psum(loss_part[0, 0], ("x", "y", "c"))

    stages = []

    def send(grads, mods_next, nm):
        ssem, rsem, thru, lands, token = _scatter_start([g.astype(BF16) for g in grads], "scatter_start_" + nm)
        stages.append((nm, grads, thru, lands, ssem, rsem))
        return mods_next + token[0:1, 0:1]

    dx5, d_w2i_b, d_w2o_b, st_b2, dg_b2 = _ffn_backward(dx6, sv_b2, g3[1], mods[1], 2, w2i[1], w2o[1], n_x, "l1_ffn2_b")
    mods1 = send([d_w2i_b, d_w2o_b], mods[1], "l1f2")
    dx4, st_bm, dg_bm, d_pscale, d_wp = _mixer_pool_backward(dx5, sv_bm, g3[1], mods1, wp_full, pscale_full, t_len)
    dx3, d_w1i_b, d_w1o_b, st_b1, dg_b1 = _ffn_backward(dx4, sv_b1, g3[1], mods1, 0, w1i[1], w1o[1], n_x, "l1_ffn1_b")
    d_wp4 = d_wp.reshape(n_grp, N_CHIPS, -1, POOL_GROUP).transpose(1, 0, 2, 3).reshape(N_CHIPS, -1, POOL_GROUP)
    mods0 = send([d_wp4, d_w1i_b, d_w1o_b], mods[0], "l1f1")
    dx2, d_w2i_a, d_w2o_a, st_a2, dg_a2 = _ffn_backward(dx3, sv_a2, g3[0], mods0, 2, w2i[0], w2o[0], n_x, "l0_ffn2_b")
    mods0 = send([d_w2i_a, d_w2o_a], mods0, "l0f2")
    dx1, st_am, dg_am, d_wcat, d_wg2, d_bias2, d_sink, d_glag, d_wout = _mixer_ab_backward(
        dx2, sv_am, g3[0], mods0, wcat, wg2, bias2, sink[0], gla_g2, w_out_full, cs, t_len, l_ctx, n_x)
    d_w_in4 = _cat_to_w_in(d_wcat).reshape(D_MODEL, N_CHIPS, -1).transpose(1, 0, 2)
    d_wout4 = d_wout.reshape(N_CHIPS, -1, D_MODEL)
    mods0 = send([d_w_in4, d_wout4], mods0, "mix0")
    dx0, d_w1i_a, d_w1o_a, st_a1, dg_a1 = _ffn_backward(dx1, sv_a1, g3[0], mods0, 0, w1i[0], w1o[0], n_x, "l0_ffn1_b")
    send([d_w1i_a, d_w1o_a], mods0, "l0f1")
    grad_x = dx0[:t_len][None]

    def mod_row(st1, dg1, stm, dgm, st2, dg2, s):
        return jnp.concatenate([st1[s, 0], st1[s, 1], dg1[s, 0], stm[s, 0], stm[s, 1], dgm[s, 0],
                                st2[s, 0], st2[s, 1], dg2[s, 0]])

    dg_bm2 = jnp.concatenate([dg_bm, jnp.zeros_like(dg_bm)], axis=0)[:, None, :]
    d_mm_x0 = mod_row(st_a1, dg_a1, st_am, dg_am, st_a2, dg_a2, 0)
    d_mm_x1 = mod_row(st_b1, dg_b1, st_bm, dg_bm2, st_b2, dg_b2, 0)
    d_mm_c0 = mod_row(st_a1, dg_a1, st_am, dg_am, st_a2, dg_a2, 1)
    d_norm_g = jnp.stack([jnp.stack([st[0, 2] + st[1, 2] for st in (st_a1, st_am, st_a2)]),
                          jnp.stack([st[0, 2] + st[1, 2] for st in (st_b1, st_bm, st_b2)])])
    rk = B_GATE_RANK
    pack = _flat_pad([d_mm_x0, d_mm_x1, d_mm_c0, d_norm_g, d_bias2, d_wg2[0:rk, 0:256], d_wg2[rk:2 * rk, 256:512],
                      d_sink[:, 0], jnp.zeros((120,), F32), d_glag, d_pscale, d_final_g],
                     _PACK_ROWS * 128).reshape(_PACK_ROWS, 128)
    pack_all = _allgather_small(pack, "gather_small_grads")
    tot = _sum_devices(pack_all, "sum_small_grads").reshape(-1)
    rows_all = pack_all.reshape(N_DEV, -1)
    o = 3 * _N9
    g_norm_g_full = tot[o:o + 6 * D_MODEL].reshape(2, 3, D_MODEL)
    o += 6 * D_MODEL
    g_bias2 = tot[o:o + 512]
    o += 512
    g_w_a2_f_full = tot[o:o + rk * 256].reshape(rk, 256)
    o += rk * 256
    g_w_a2_b_full = tot[o:o + rk * 256].reshape(rk, 256)
    o += rk * 256
    g_sink = tot[o:o + A_HEADS]
    o += 128
    g_gla_g = tot[o:o + B_DV]
    o += B_DV
    g_pscale_full = tot[o:o + D_MODEL]
    o += D_MODEL
    g_final_g = tot[o:o + D_MODEL]
    d_mmc_tot = tot[2 * _N9:3 * _N9]
    g_b_mod = jnp.stack([tot[0:_N9] + d_mmc_tot, tot[_N9:2 * _N9]])

    zrows = jnp.zeros((_CROWS - N_DEV - 1, _N9), F32)
    d16 = jnp.stack([jnp.concatenate([rows_all[:, 0:_N9], d_mmc_tot[None], zrows], axis=0),
                     jnp.concatenate([rows_all[:, _N9:2 * _N9], jnp.zeros((1, _N9), F32), zrows], axis=0)])
    d16_k = lax.dynamic_slice(d16, (0, 0, k_me * mod_cols), (2, _CROWS, mod_cols))
    dmmc_k = lax.dynamic_slice(d_mmc_tot, (k_me * mod_cols,), (mod_cols,)).reshape(1, mod_cols)
    g_w_mod, c_part = _adaln_bwd(c16, d16_k, w_mod, dmmc_k, "adaln_bwd")
    c_parts = _allgather_small(c_part.reshape(8, 128), "gather_cctx")
    g_c_ctx = _cctx_grad(c_parts, c_ctx.reshape(8, 128), "cctx_grad").reshape(D_MODEL)

    def small(w, g, m, v, shape3, nm):
        return [o_.reshape(w.shape) for o_ in _adamw(w.reshape(shape3), [g.reshape(shape3[1:])],
                                                    m.reshape(shape3), v.reshape(shape3), 0, None, "adamw_" + nm)]

    def own(a, axis, size):
        return lax.dynamic_slice_in_dim(a, k_me * size, size, axis=axis)

    res = {}
    res["c_ctx"] = small(c_ctx, g_c_ctx, m_c_ctx, v_c_ctx, (1, 8, 128), "c_ctx")
    upd = _adamw(w_mod, [(g_w_mod, 1)], m_w_mod, v_w_mod, 1, None, "adamw_w_mod_1")
    res["w_mod"] = _adamw(w_mod, [(g_w_mod, 0)], m_w_mod, v_w_mod, 0, upd, "adamw_w_mod_0")
    res["b_mod"] = small(b_mod, g_b_mod, m_b_mod, v_b_mod, (1, 2, _N9), "b_mod")
    res["norm_g"] = small(norm_g, own(g_norm_g_full, 2, norm_g.shape[2]), m_norm_g, v_norm_g,
                          (1, 6, norm_g.shape[2]), "norm_g")
    res["w_a2_f"] = small(w_a2_f, own(g_w_a2_f_full, 1, w_a2_f.shape[2]), m_w_a2_f, v_w_a2_f,
                          (1, rk, w_a2_f.shape[2]), "w_a2_f")
    res["b_a_f"] = small(b_a_f, g_bias2[0:256], m_b_a_f, v_b_a_f, (1, 1, 256), "b_a_f")
    res["w_a2_b"] = small(w_a2_b, own(g_w_a2_b_full, 1, w_a2_b.shape[2]), m_w_a2_b, v_w_a2_b,
                          (1, rk, w_a2_b.shape[2]), "w_a2_b")
    res["b_a_b"] = small(b_a_b, g_bias2[256:512], m_b_a_b, v_b_a_b, (1, 1, 256), "b_a_b")
    res["sink"] = small(sink, g_sink, m_sink, v_sink, (1, 1, A_HEADS), "sink")
    res["gla_g"] = small(gla_g, g_gla_g, m_gla_g, v_gla_g, (1, 1, B_DV), "gla_g")
    res["pool_scale"] = small(pool_scale, own(g_pscale_full, 0, pool_scale.shape[1]), m_pool_scale, v_pool_scale,
                              (1, 1, pool_scale.shape[1]), "pool_scale")
    res["final_g"] = small(final_g, g_final_g, m_final_g, v_final_g, (1, 8, 128), "final_g")

    def as3(a):
        n_l = a.shape[0] if a.ndim == 3 else 1
        return a.reshape(n_l, -1, a.shape[-1])

    big_w = {"ffn1_wi": (ffn1_wi, m_ffn1_wi, v_ffn1_wi), "ffn1_wo": (ffn1_wo, m_ffn1_wo, v_ffn1_wo),
             "ffn2_wi": (ffn2_wi, m_ffn2_wi, v_ffn2_wi), "ffn2_wo": (ffn2_wo, m_ffn2_wo, v_ffn2_wo),
             "w_in": (w_in, m_w_in, v_w_in), "w_out": (w_out, m_w_out, v_w_out), "w_pool": (w_pool, m_w_pool, v_w_pool)}
    target = {"l1f2": (("ffn2_wi", 1), ("ffn2_wo", 1)), "l1f1": (("w_pool", 0), ("ffn1_wi", 1), ("ffn1_wo", 1)),
              "l0f2": (("ffn2_wi", 0), ("ffn2_wo", 0)), "mix0": (("w_in", 0), ("w_out", 0)),
              "l0f1": (("ffn1_wi", 0), ("ffn1_wo", 0))}
    k_idx = k_me.reshape(1).astype(jnp.int32)
    chain = res["final_g"][0]
    for nm, grads, thru, lands, ssem, rsem in stages:
        recv = _scatter_wait(thru, lands, ssem, rsem, chain, "scatter_wait_" + nm)
        partial = [_partial_sum(g, r, k_idx, "partial_sum_%s_%d" % (nm, i))
                   for i, (g, r) in enumerate(zip(grads, recv))]
        other = _swap_sibling(partial, "swap_partials_" + nm)
        for (wname, layer), p, q in zip(target[nm], partial, other):
            w, m, v = big_w[wname]
            res[wname] = _adamw(as3(w), [p, q], as3(m), as3(v), layer, res.get(wname),
                                "adamw_%s_%d" % (wname, layer))
            chain = res[wname][3]
    for wname, (w, _, _) in big_w.items():
        res[wname] = [o_.reshape(w.shape) for o_ in res[wname]]

    names = ["c_ctx", "w_mod", "b_mod", "norm_g", "ffn1_wi", "ffn1_wo", "ffn2_wi", "ffn2_wo", "w_in", "w_a2_f",
             "b_a_f", "w_a2_b", "b_a_b", "sink", "gla_g", "w_out", "w_pool", "pool_scale", "final_g"]
    outs = [loss, grad_x]
    for field in range(4):
        outs += [res[nm][field] for nm in names]
    return tuple(outs)
```

```python
import functools

import jax
import jax.numpy as jnp
import numpy as np
from jax import lax
from jax.experimental import pallas as pl
from jax.experimental.pallas import tpu as pltpu

F32 = jnp.float32
BF16 = jnp.bfloat16

D_MODEL = 1024
N_MOD = 9
D_FF = 2816
RMS_EPS = 1e-6
A_HEADS = 8
A_KV_HEADS = 2
A_HEAD_DIM = 64
WINDOW = 128
ROPE_BASE = 10000.0
GRID_W = 64
B_HEADS = 4
B_DK = 64
B_DV = 128
B_GATE_RANK = 16
B_GATE_NORM = 16.0
B_CHUNK = 64
POOL_WINDOWS = (2, 4, 8, 16)
POOL_GROUP = D_MODEL // len(POOL_WINDOWS)
PROJ_DIM = 2336

ADAM_LR = 0.001
ADAM_B1 = 0.9
ADAM_B2 = 0.999
ADAM_EPS = 1e-08
ADAM_WD = 0.01
ADAM_STEP = 10

N_CHIPS = 4
N_DEV = 8
ROW_TILE = 512
VMEM_LIMIT_BYTES = 56 * 1024 * 1024
MESH = pl.DeviceIdType.MESH

ZC_Q, ZC_QK, ZC_V, ZC_R, ZC_KV, ZC_G, ZC_W = 0, 512, 1024, 1536, 2048, 2304, 2432


def _cp(*sem):
    return pltpu.CompilerParams(dimension_semantics=sem if sem else None, vmem_limit_bytes=VMEM_LIMIT_BYTES)


def _dot(a, b):
    return jnp.dot(a, b, preferred_element_type=F32)


def _dot_nt(a, b):
    return lax.dot_general(a, b, (((1,), (1,)), ((), ())), preferred_element_type=F32)


def _dot_tn(a, b):
    return lax.dot_general(a, b, (((0,), (0,)), ((), ())), preferred_element_type=F32)


def _dot_hi(a, b):
    return jnp.dot(a, b, preferred_element_type=F32, precision=lax.Precision.HIGHEST)


def _dot_tn_hi(a, b):
    return lax.dot_general(a, b, (((0,), (0,)), ((), ())), preferred_element_type=F32,
                           precision=lax.Precision.HIGHEST)


def _sigmoid(x):
    return 1.0 / (1.0 + jnp.exp(-x))


def _stream_of(i, n_x):
    return jnp.where(i >= n_x, 1, 0)


def _rms_mod_fwd(x, g3, mods, j, n_x, out_dtype, name):
    rows = x.shape[0]
    tm = ROW_TILE
    n_i = rows // tm

    def body(x_ref, g_ref, m_ref, o_ref):
        xv = x_ref[...]
        r = lax.rsqrt(jnp.mean(xv * xv, axis=-1, keepdims=True) + RMS_EPS)
        g = g_ref[j:j + 1, :]
        shift = m_ref[0, 3 * j:3 * j + 1, :]
        scale = m_ref[0, 3 * j + 1:3 * j + 2, :]
        o_ref[...] = (((xv * r) * g) * (1.0 + scale) + shift).astype(out_dtype)

    return pl.pallas_call(
        body, name=name, grid=(n_i,),
        in_specs=[pl.BlockSpec((tm, D_MODEL), lambda i: (i, 0)),
                  pl.BlockSpec((3, D_MODEL), lambda i: (0, 0)),
                  pl.BlockSpec((1, N_MOD, D_MODEL), lambda i: (_stream_of(i, n_x), 0, 0))],
        out_specs=pl.BlockSpec((tm, D_MODEL), lambda i: (i, 0)),
        out_shape=jax.ShapeDtypeStruct((rows, D_MODEL), out_dtype),
        compiler_params=_cp("parallel"),
    )(x, g3, mods)


def _rms_mod_bwd_tail(dh, xv, g, scale, stream, acc_ref, first):
    r = lax.rsqrt(jnp.mean(xv * xv, axis=-1, keepdims=True) + RMS_EPS)
    xhat = xv * r
    t1 = jnp.sum(dh, axis=0, keepdims=True)
    t2 = jnp.sum(dh * xhat, axis=0, keepdims=True)
    stats = jnp.concatenate([t1, t2 * g, t2 * (1.0 + scale)], axis=0)

    @pl.when(first)
    def _():
        acc_ref[...] = jnp.zeros_like(acc_ref)

    acc_ref[pl.ds(stream, 1)] += stats[None]
    dxh = dh * (g * (1.0 + scale))
    return r * (dxh - xhat * jnp.mean(dxh * xhat, axis=-1, keepdims=True))


def _ffn_up(hn, w4, name):
    rows = hn.shape[0]
    h = w4.shape[2]
    tm = ROW_TILE
    n_i = rows // tm

    def body(h_ref, wa_ref, wu_ref, au_ref, s_ref):
        hv = h_ref[...]
        a = _dot(hv, wa_ref[0])
        u = _dot(hv, wu_ref[0])
        au_ref[0] = a.astype(BF16)
        au_ref[1] = u.astype(BF16)
        s_ref[...] = (a * _sigmoid(a) * u).astype(BF16)

    return pl.pallas_call(
        body, name=name, grid=(2, n_i),
        in_specs=[pl.BlockSpec((tm, D_MODEL), lambda j, i: (i, 0)),
                  pl.BlockSpec((1, D_MODEL, h), lambda j, i: (j, 0, 0)),
                  pl.BlockSpec((1, D_MODEL, h), lambda j, i: (j + 2, 0, 0))],
        out_specs=[pl.BlockSpec((2, tm, h), lambda j, i: (0, i, j)),
                   pl.BlockSpec((tm, h), lambda j, i: (i, j))],
        out_shape=[jax.ShapeDtypeStruct((2, rows, 2 * h), BF16),
                   jax.ShapeDtypeStruct((rows, 2 * h), BF16)],
        compiler_params=_cp("arbitrary", "arbitrary"),
    )(hn, w4, w4)


def _matmul_resid(a, w, xres, mods, gate_idx, coef, n_x, rows, name):
    k = a.shape[1]
    tm = ROW_TILE
    n_i = rows // tm

    def body(a_ref, w_ref, x_ref, m_ref, o_ref, f_ref):
        f = _dot(a_ref[...], w_ref[...])
        gate = m_ref[0, gate_idx:gate_idx + 1, :]
        f_ref[...] = f
        o_ref[...] = x_ref[...] + (coef * gate) * f

    return pl.pallas_call(
        body, name=name, grid=(n_i,),
        in_specs=[pl.BlockSpec((tm, k), lambda i: (i, 0)),
                  pl.BlockSpec((k, D_MODEL), lambda i: (0, 0)),
                  pl.BlockSpec((tm, D_MODEL), lambda i: (i, 0)),
                  pl.BlockSpec((1, N_MOD, D_MODEL), lambda i: (_stream_of(i, n_x), 0, 0))],
        out_specs=[pl.BlockSpec((tm, D_MODEL), lambda i: (i, 0)),
                   pl.BlockSpec((tm, D_MODEL), lambda i: (i, 0))],
        out_shape=[jax.ShapeDtypeStruct((rows, D_MODEL), F32),
                   jax.ShapeDtypeStruct((rows, D_MODEL), F32)],
        compiler_params=_cp("parallel"),
    )(a, w, xres, mods)


def _gate_dy(dout, f, mods, gate_idx, coef, n_x, rows, name):
    tm = ROW_TILE
    n_i = rows // tm

    def body(d_ref, f_ref, m_ref, dy_ref, acc_ref):
        i = pl.program_id(0)
        dv = d_ref[...]
        gate = m_ref[0, gate_idx:gate_idx + 1, :]
        dy_ref[...] = (dv * (coef * gate)).astype(BF16)

        @pl.when(i == 0)
        def _():
            acc_ref[...] = jnp.zeros_like(acc_ref)

        part = coef * jnp.sum(dv * f_ref[...], axis=0, keepdims=True)
        acc_ref[pl.ds(_stream_of(i, n_x), 1)] += part[None]

    return pl.pallas_call(
        body, name=name, grid=(n_i,),
        in_specs=[pl.BlockSpec((tm, D_MODEL), lambda i: (i, 0)),
                  pl.BlockSpec((tm, D_MODEL), lambda i: (i, 0)),
                  pl.BlockSpec((1, N_MOD, D_MODEL), lambda i: (_stream_of(i, n_x), 0, 0))],
        out_specs=[pl.BlockSpec((tm, D_MODEL), lambda i: (i, 0)),
                   pl.BlockSpec((2, 1, D_MODEL), lambda i: (0, 0, 0))],
        out_shape=[jax.ShapeDtypeStruct((rows, D_MODEL), BF16),
                   jax.ShapeDtypeStruct((2, 1, D_MODEL), F32)],
        compiler_params=_cp("arbitrary"),
    )(dout, f, mods)


def _ffn_bwd_dz(dy, wo2, au, name):
    rows = dy.shape[0]
    h = wo2.shape[1]
    tm = ROW_TILE
    n_i = rows // tm

    def body(dy_ref, wo_ref, au_ref, dz_ref):
        ds = _dot_nt(dy_ref[...], wo_ref[0])
        a = au_ref[0].astype(F32)
        u = au_ref[1].astype(F32)
        sg = _sigmoid(a)
        dz_ref[0] = (ds * u * (sg * (1.0 + a * (1.0 - sg)))).astype(BF16)
        dz_ref[1] = (ds * (a * sg)).astype(BF16)

    return pl.pallas_call(
        body, name=name, grid=(2, n_i),
        in_specs=[pl.BlockSpec((tm, D_MODEL), lambda j, i: (i, 0)),
                  pl.BlockSpec((1, h, D_MODEL), lambda j, i: (j, 0, 0)),
                  pl.BlockSpec((2, tm, h), lambda j, i: (0, i, j))],
        out_specs=pl.BlockSpec((2, tm, h), lambda j, i: (0, i, j)),
        out_shape=jax.ShapeDtypeStruct((2, rows, 2 * h), BF16),
        compiler_params=_cp("arbitrary", "arbitrary"),
    )(dy, wo2, au)


def _matmul_tn(a, b, a_spec, b_spec, out_shape, out_spec, grid, name):
    nd_a = len(a_spec.block_shape)
    nd_b = len(b_spec.block_shape)
    nd_o = len(out_spec.block_shape)
    k_axis = len(grid) - 1
    n_k = grid[k_axis]

    def body(a_ref, b_ref, o_ref, acc_ref):
        av = a_ref[(0,) * (nd_a - 2)]
        bv = b_ref[(0,) * (nd_b - 2)]
        part = _dot_tn(av, bv)
        k = pl.program_id(k_axis)

        @pl.when(k == 0)
        def _():
            acc_ref[...] = part

        @pl.when(k > 0)
        def _():
            acc_ref[...] += part

        @pl.when(k == n_k - 1)
        def _():
            o_ref[(0,) * (nd_o - 2)] = acc_ref[...].astype(BF16)

    return pl.pallas_call(
        body, name=name, grid=grid, in_specs=[a_spec, b_spec], out_specs=out_spec,
        out_shape=jax.ShapeDtypeStruct(out_shape, BF16),
        scratch_shapes=[pltpu.VMEM(tuple(out_spec.block_shape[-2:]), F32)],
        compiler_params=_cp(*(("arbitrary",) * len(grid))),
    )(a, b)


def _bwd_dx(pairs, x, dres, dres_tiles, g3, mods, j, n_x, name):
    rows = x.shape[0]
    tm = ROW_TILE
    n_i = rows // tm
    n_p = len(pairs)
    nds = [(len(p[1].block_shape), len(p[3].block_shape)) for p in pairs]

    def body(*refs):
        dz_refs = refs[0:2 * n_p:2]
        w_refs = refs[1:2 * n_p:2]
        x_ref, dres_ref, g_ref, m_ref, dx_ref, acc_ref = refs[2 * n_p:]
        i = pl.program_id(0)
        dh = None
        for p in range(n_p):
            dzv = dz_refs[p][(0,) * (nds[p][0] - 2)]
            wv = w_refs[p][(0,) * (nds[p][1] - 2)]
            part = _dot_nt(dzv, wv)
            dh = part if dh is None else dh + part
        g = g_ref[j:j + 1, :]
        scale = m_ref[0, 3 * j + 1:3 * j + 2, :]
        dx = _rms_mod_bwd_tail(dh, x_ref[...], g, scale, _stream_of(i, n_x), acc_ref, i == 0)
        dres_v = jnp.where(i < dres_tiles, dres_ref[...], 0.0)
        dx_ref[...] = dres_v + dx

    in_specs, args = [], []
    for dz, dz_spec, w, w_spec in pairs:
        in_specs += [dz_spec, w_spec]
        args += [dz, w]
    in_specs += [pl.BlockSpec((tm, D_MODEL), lambda i: (i, 0)),
                 pl.BlockSpec((tm, D_MODEL), lambda i: (jnp.minimum(i, dres_tiles - 1), 0)),
                 pl.BlockSpec((3, D_MODEL), lambda i: (0, 0)),
                 pl.BlockSpec((1, N_MOD, D_MODEL), lambda i: (_stream_of(i, n_x), 0, 0))]
    args += [x, dres, g3, mods]
    return pl.pallas_call(
        body, name=name, grid=(n_i,), in_specs=in_specs,
        out_specs=[pl.BlockSpec((tm, D_MODEL), lambda i: (i, 0)),
                   pl.BlockSpec((2, 3, D_MODEL), lambda i: (0, 0, 0))],
        out_shape=[jax.ShapeDtypeStruct((rows, D_MODEL), F32),
                   jax.ShapeDtypeStruct((2, 3, D_MODEL), F32)],
        compiler_params=_cp("arbitrary"),
    )(*args)


def _ffn_forward(x, g3, mods, j, w4_in, w4_out_of, n_x, name):
    rows = x.shape[0]
    hn = _rms_mod_fwd(x, g3, mods, j, n_x, BF16, name + "_mod")
    au, s = _ffn_up(hn, w4_in, name + "_up")
    w4_out = w4_out_of(s)
    wo = w4_out.reshape(D_FF, D_MODEL)
    out, f = _matmul_resid(s, wo, x, mods, 3 * j + 2, 0.5, n_x, rows, name + "_down")
    return out, (x, hn, au, s, f), w4_out


def _ffn_backward(dout, saved, g3, mods, j, w4_in, w4_out, n_x, send, name):
    x, hn, au, s, f = saved
    rows = x.shape[0]
    tm = ROW_TILE
    n_i = rows // tm
    h = w4_in.shape[2]
    dy, dgate = _gate_dy(dout, f, mods, 3 * j + 2, 0.5, n_x, rows, name + "_dy")
    wo2 = w4_out.reshape(2, h, D_MODEL)
    dz = _ffn_bwd_dz(dy, wo2, au, name + "_dz")
    d_wi = _matmul_tn(
        hn, dz, pl.BlockSpec((tm, D_MODEL), lambda q, k: (k, 0)),
        pl.BlockSpec((1, tm, h), lambda q, k: (q // 2, k, q % 2)),
        (4, D_MODEL, h), pl.BlockSpec((1, D_MODEL, h), lambda q, k: (q, 0, 0)), (4, n_i), name + "_dwi")
    mods = mods + send(d_wi, "wi")
    d_wo = _matmul_tn(
        s, dy, pl.BlockSpec((tm, h), lambda n, k: (k, n)), pl.BlockSpec((tm, D_MODEL), lambda n, k: (k, 0)),
        (D_FF, D_MODEL), pl.BlockSpec((h, D_MODEL), lambda n, k: (n, 0)), (2, n_i), name + "_dwo")
    mods = mods + send(d_wo.reshape(w4_out.shape), "wo")
    pairs = [(dz, pl.BlockSpec((1, tm, h), functools.partial(lambda q, i: (q // 2, i, q % 2), q)),
              w4_in, pl.BlockSpec((1, D_MODEL, h), functools.partial(lambda q, i: (q, 0, 0), q)))
             for q in range(4)]
    dx, stats = _bwd_dx(pairs, x, dout, n_i, g3, mods, j, n_x, name + "_dx")
    return dx, stats, dgate


def _matmul_nt(a, w, name):
    rows, k = a.shape
    n = w.shape[0]
    tm = ROW_TILE

    def body(a_ref, w_ref, o_ref):
        o_ref[...] = _dot_nt(a_ref[...], w_ref[...])

    return pl.pallas_call(
        body, name=name, grid=(rows // tm,),
        in_specs=[pl.BlockSpec((tm, k), lambda i: (i, 0)), pl.BlockSpec((n, k), lambda i: (0, 0))],
        out_specs=pl.BlockSpec((tm, n), lambda i: (i, 0)),
        out_shape=jax.ShapeDtypeStruct((rows, n), F32),
        compiler_params=_cp("parallel"),
    )(a, w)


def _rope_tables(t_len, rows):
    n = A_HEAD_DIM // 4
    freqs = ROPE_BASE ** (-jnp.arange(n, dtype=F32) / n)
    t = jnp.arange(t_len)
    ang_r = (t // GRID_W).astype(F32)[:, None] * freqs
    ang_c = (t % GRID_W).astype(F32)[:, None] * freqs
    cos = jnp.concatenate([jnp.cos(ang_r), jnp.cos(ang_r), jnp.cos(ang_c), jnp.cos(ang_c)], axis=1)
    sin = jnp.concatenate([-jnp.sin(ang_r), jnp.sin(ang_r), -jnp.sin(ang_c), jnp.sin(ang_c)], axis=1)
    cos = jnp.concatenate([cos, jnp.ones((rows - t_len, A_HEAD_DIM), F32)], axis=0)
    sin = jnp.concatenate([sin, jnp.zeros((rows - t_len, A_HEAD_DIM), F32)], axis=0)
    return jnp.concatenate([cos, cos, sin, sin], axis=1)


def _swap16(x):
    n = x.shape[1]
    lane = lax.broadcasted_iota(jnp.int32, x.shape, 1)
    first = jnp.bitwise_and(lane, 16) == 0
    return jnp.where(first, pltpu.roll(x, n - 16, 1), pltpu.roll(x, 16, 1))


def _log_sigmoid(x):
    return jnp.minimum(x, 0.0) - jnp.log(1.0 + jnp.exp(-jnp.abs(x)))


def _proj_fwd(h, wcat, wg2, bias2, cs, name):
    rows = h.shape[0]
    tm = ROW_TILE

    def body(h_ref, w_ref, wg_ref, b_ref, cs_ref, zc_ref, la_ref):
        z = _dot(h_ref[...], w_ref[...])
        cos = cs_ref[:, 0:128]
        sin = cs_ref[:, 128:256]
        cosq = jnp.concatenate([cos] * 4, axis=1)
        sinq = jnp.concatenate([sin] * 4, axis=1)
        q = z[:, ZC_Q:ZC_QK]
        zc_ref[:, ZC_Q:ZC_QK] = q * cosq + _swap16(q) * sinq
        zc_ref[:, ZC_QK:ZC_KV] = z[:, ZC_QK:ZC_KV]
        kk = z[:, ZC_KV:ZC_KV + 128]
        zc_ref[:, ZC_KV:ZC_KV + 128] = kk * cos + _swap16(kk) * sin
        zc_ref[:, ZC_KV + 128:ZC_W] = z[:, ZC_KV + 128:ZC_W]
        zg = z[:, ZC_G:ZC_W]
        pre = _dot(zg.astype(BF16), wg_ref[...]) + b_ref[...]
        la_ref[...] = _log_sigmoid(pre) / B_GATE_NORM

    return pl.pallas_call(
        body, name=name, grid=(rows // tm,),
        in_specs=[pl.BlockSpec((tm, D_MODEL), lambda i: (i, 0)),
                  pl.BlockSpec((D_MODEL, ZC_W), lambda i: (0, 0)),
                  pl.BlockSpec((128, 512), lambda i: (0, 0)),
                  pl.BlockSpec((1, 512), lambda i: (0, 0)),
                  pl.BlockSpec((tm, 256), lambda i: (i, 0))],
        out_specs=[pl.BlockSpec((tm, ZC_W), lambda i: (i, 0)),
                   pl.BlockSpec((tm, 512), lambda i: (i, 0))],
        out_shape=[jax.ShapeDtypeStruct((rows, ZC_W), F32),
                   jax.ShapeDtypeStruct((rows, 512), F32)],
        compiler_params=_cp("parallel"),
    )(h, wcat, wg2, bias2, cs)


_QB = WINDOW


def _attn_specs(t_len, l_ctx):
    nb = t_len // _QB
    kvb = ZC_KV // 256
    return [pl.BlockSpec(memory_space=pltpu.SMEM),
            pl.BlockSpec((_QB, 512), lambda n: (n, 0)),
            pl.BlockSpec((_QB, 256), lambda n: (jnp.maximum(n - 1, 0), kvb)),
            pl.BlockSpec((_QB, 256), lambda n: (n, kvb)),
            pl.BlockSpec((_QB, 256), lambda n: (n + 1, kvb)),
            pl.BlockSpec((l_ctx, 256), lambda n: (t_len // l_ctx, kvb))], nb


def _attn_probs(n, t_len, sink_ref, qv, kp, kc, kn, kx, g):
    hd = A_HEAD_DIM
    ks = slice(g * hd, (g + 1) * hd)
    vs = slice(128 + g * hd, 128 + (g + 1) * hd)
    kb = jnp.concatenate([kp[:, ks], kc[:, ks], kn[:, ks]], axis=0).astype(BF16)
    vb = jnp.concatenate([kp[:, vs], kc[:, vs], kn[:, vs]], axis=0).astype(BF16)
    kxb = kx[:, ks].astype(BF16)
    vxb = kx[:, vs].astype(BF16)
    qg = jnp.concatenate([qv[:, (4 * g + r) * hd:(4 * g + r + 1) * hd] for r in range(4)], axis=0).astype(BF16)
    qi = lax.broadcasted_iota(jnp.int32, (_QB, 3 * _QB), 0)
    kj = lax.broadcasted_iota(jnp.int32, (_QB, 3 * _QB), 1)
    kpos = n * _QB - _QB + kj
    valid = (kpos >= 0) & (kpos < t_len) & (jnp.abs(kj - _QB - qi) <= WINDOW)
    valid4 = jnp.concatenate([valid] * 4, axis=0)
    scale = hd ** -0.5
    s = jnp.where(valid4, _dot_nt(qg, kb) * scale, -jnp.inf)
    sc = _dot_nt(qg, kxb) * scale
    sk = jnp.concatenate([jnp.full((_QB, 1), sink_ref[4 * g + r], F32) for r in range(4)], axis=0)
    m = jnp.maximum(jnp.maximum(jnp.max(s, axis=-1, keepdims=True), jnp.max(sc, axis=-1, keepdims=True)), sk)
    p = jnp.exp(s - m)
    pc = jnp.exp(sc - m)
    ps = jnp.exp(sk - m)
    inv = 1.0 / (jnp.sum(p, axis=-1, keepdims=True) + jnp.sum(pc, axis=-1, keepdims=True) + ps)
    return p * inv, pc * inv, ps * inv, qg, kb, vb, kxb, vxb


def _attn_fwd(zc, sink, t_len, l_ctx, name):
    in_specs, nb = _attn_specs(t_len, l_ctx)

    def body(sink_ref, q_ref, kp_ref, kc_ref, kn_ref, kx_ref, o_ref):
        n = pl.program_id(0)
        outs = []
        for g in range(A_KV_HEADS):
            p, pc, _, _, _, vb, _, vxb = _attn_probs(
                n, t_len, sink_ref, q_ref[...], kp_ref[...], kc_ref[...], kn_ref[...], kx_ref[...], g)
            o = _dot(p.astype(BF16), vb) + _dot(pc.astype(BF16), vxb)
            outs += [o[r * _QB:(r + 1) * _QB] for r in range(4)]
        o_ref[...] = jnp.concatenate(outs, axis=1)

    return pl.pallas_call(
        body, name=name, grid=(nb,), in_specs=in_specs,
        out_specs=pl.BlockSpec((_QB, 512), lambda n: (n, 0)),
        out_shape=jax.ShapeDtypeStruct((t_len, 512), F32),
        compiler_params=_cp("parallel"),
    )(sink, zc, zc, zc, zc, zc)


def _attn_bwd(zc, sink, o, dcat, t_len, l_ctx, name):
    rows = zc.shape[0]
    in_specs, nb = _attn_specs(t_len, l_ctx)
    in_specs = in_specs + [pl.BlockSpec((_QB, 512), lambda n: (n, 0)), pl.BlockSpec((_QB, 512), lambda n: (n, 0))]
    hd = A_HEAD_DIM
    scale = hd ** -0.5

    def body(sink_ref, q_ref, kp_ref, kc_ref, kn_ref, kx_ref, o_ref, do_ref, dq_ref, dkv_ref, dsink_ref):
        n = pl.program_id(0)

        @pl.when(n == 0)
        def _():
            dkv_ref[...] = jnp.zeros_like(dkv_ref)
            dsink_ref[...] = jnp.zeros_like(dsink_ref)

        ov = o_ref[...]
        dov = do_ref[...]
        dqs, dkbs, dvbs, dkxs, dvxs = [], [], [], [], []
        for g in range(A_KV_HEADS):
            p, pc, ps, qg, kb, vb, kxb, vxb = _attn_probs(
                n, t_len, sink_ref, q_ref[...], kp_ref[...], kc_ref[...], kn_ref[...], kx_ref[...], g)
            og = jnp.concatenate([ov[:, (4 * g + r) * hd:(4 * g + r + 1) * hd] for r in range(4)], axis=0)
            dog = jnp.concatenate([dov[:, (4 * g + r) * hd:(4 * g + r + 1) * hd] for r in range(4)], axis=0)
            delta = jnp.sum(og * dog, axis=-1, keepdims=True)
            dogb = dog.astype(BF16)
            ds = (p * (_dot_nt(dogb, vb) - delta) * scale).astype(BF16)
            dsc = (pc * (_dot_nt(dogb, vxb) - delta) * scale).astype(BF16)
            dsk = ps * (0.0 - delta)
            dqg = _dot(ds, kb) + _dot(dsc, kxb)
            dqs += [dqg[r * _QB:(r + 1) * _QB] for r in range(4)]
            dkbs.append(_dot_tn(ds, qg))
            dvbs.append(_dot_tn(p.astype(BF16), dogb))
            dkxs.append(_dot_tn(dsc, qg))
            dvxs.append(_dot_tn(pc.astype(BF16), dogb))
            for r in range(4):
                hrow = 4 * g + r
                tot = jnp.sum(dsk[r * _QB:(r + 1) * _QB], axis=0, keepdims=True)
                dsink_ref[hrow:hrow + 1, :] += jnp.broadcast_to(tot, (1, 128))
        dq_ref[...] = jnp.concatenate(dqs, axis=1)
        band = jnp.concatenate(dkbs + dvbs, axis=1)
        ctxc = jnp.concatenate(dkxs + dvxs, axis=1)
        r_prev = pl.multiple_of(jnp.maximum(n - 1, 0) * _QB, _QB)
        r_cur = pl.multiple_of(n * _QB, _QB)
        r_next = pl.multiple_of((n + 1) * _QB, _QB)
        dkv_ref[pl.ds(r_prev, _QB), :] += band[0:_QB]
        dkv_ref[pl.ds(r_cur, _QB), :] += band[_QB:2 * _QB]
        dkv_ref[pl.ds(r_next, _QB), :] += band[2 * _QB:3 * _QB]
        dkv_ref[t_len:t_len + l_ctx, :] += ctxc

    return pl.pallas_call(
        body, name=name, grid=(nb,), in_specs=in_specs,
        out_specs=[pl.BlockSpec((_QB, 512), lambda n: (n, 0)),
                   pl.BlockSpec((rows, 256), lambda n: (0, 0)),
                   pl.BlockSpec((8, 128), lambda n: (0, 0))],
        out_shape=[jax.ShapeDtypeStruct((t_len, 512), F32),
                   jax.ShapeDtypeStruct((rows, 256), F32),
                   jax.ShapeDtypeStruct((8, 128), F32)],
        compiler_params=_cp("arbitrary"),
    )(sink, zc, zc, zc, zc, zc, o, dcat)


_GC = B_CHUNK


def _gla_chunk_terms(qk, v, la, head, reverse):
    q = qk[:, head * B_DK:(head + 1) * B_DK]
    k = qk[:, 256 + head * B_DK:256 + (head + 1) * B_DK]
    vh = v[:, head * B_DV:(head + 1) * B_DV]
    off = 256 if reverse else 0
    lah = la[:, off + head * B_DK:off + (head + 1) * B_DK]
    ii = lax.broadcasted_iota(jnp.int32, (_GC, _GC), 0)
    jj = lax.broadcasted_iota(jnp.int32, (_GC, _GC), 1)
    mask = (jj >= ii) if reverse else (jj <= ii)
    tri = jnp.where(mask, 1.0, 0.0).astype(F32)
    g = _dot_hi(tri, lah)
    gl = jnp.sum(lah, axis=0, keepdims=True)
    eg = jnp.exp(g)
    eng = jnp.exp(-g)
    eend = jnp.exp(gl - g)
    sc = B_DK ** -0.5
    qt = q * (sc * eg)
    kt = k * eng
    ke = k * eend
    return q, k, vh, lah, mask, tri, g, gl, eg, eng, eend, qt, kt, ke


def _gla_fwd(zc, la, t_len, l_ctx, name):
    rows = zc.shape[0]
    n_x = t_len // _GC
    n_c = n_x + l_ctx // _GC
    qkb, vb = ZC_QK // 512, ZC_V // 512

    def ch_f(c):
        return lax.rem(c + n_x, n_c)

    def ch_r(c):
        return n_c - 1 - c

    def body(qkf_ref, vf_ref, laf_ref, qkr_ref, vr_ref, lar_ref, of_ref, or_ref, spf_ref, spr_ref, stf, strv):
        c = pl.program_id(0)

        @pl.when(c == 0)
        def _():
            stf[...] = jnp.zeros_like(stf)
            strv[...] = jnp.zeros_like(strv)

        for qk_ref, v_ref, la_ref, o_ref, sp_ref, st, reverse in (
                (qkf_ref, vf_ref, laf_ref, of_ref, spf_ref, stf, False),
                (qkr_ref, vr_ref, lar_ref, or_ref, spr_ref, strv, True)):
            qk = qk_ref[...]
            v = v_ref[...]
            la = la_ref[...]
            for hh in range(B_HEADS):
                _, _, vh, _, mask, _, _, gl, _, _, _, qt, kt, ke = _gla_chunk_terms(qk, v, la, hh, reverse)
                s_prev = st[hh]
                att = jnp.where(mask, _dot_nt(qt.astype(BF16), kt.astype(BF16)), 0.0)
                o = _dot(att.astype(BF16), vh.astype(BF16)) + _dot_nt(qt.astype(BF16), s_prev.astype(BF16))
                o_ref[:, hh * B_DV:(hh + 1) * B_DV] = o
                sp_ref[0, hh] = s_prev
                st[hh] = s_prev * jnp.exp(gl) + _dot_tn(vh.astype(BF16), ke.astype(BF16))

    st_shape = (B_HEADS, B_DV, B_DK)
    return pl.pallas_call(
        body, name=name, grid=(n_c,),
        in_specs=[pl.BlockSpec((_GC, 512), lambda c: (ch_f(c), qkb)),
                  pl.BlockSpec((_GC, 512), lambda c: (ch_f(c), vb)),
                  pl.BlockSpec((_GC, 512), lambda c: (ch_f(c), 0)),
                  pl.BlockSpec((_GC, 512), lambda c: (ch_r(c), qkb)),
                  pl.BlockSpec((_GC, 512), lambda c: (ch_r(c), vb)),
                  pl.BlockSpec((_GC, 512), lambda c: (ch_r(c), 0))],
        out_specs=[pl.BlockSpec((_GC, 512), lambda c: (ch_f(c), 0)),
                   pl.BlockSpec((_GC, 512), lambda c: (ch_r(c), 0)),
                   pl.BlockSpec((1,) + st_shape, lambda c: (c, 0, 0, 0)),
                   pl.BlockSpec((1,) + st_shape, lambda c: (c, 0, 0, 0))],
        out_shape=[jax.ShapeDtypeStruct((rows, 512), F32), jax.ShapeDtypeStruct((rows, 512), F32),
                   jax.ShapeDtypeStruct((n_c,) + st_shape, F32), jax.ShapeDtypeStruct((n_c,) + st_shape, F32)],
        scratch_shapes=[pltpu.VMEM(st_shape, F32), pltpu.VMEM(st_shape, F32)],
        compiler_params=_cp("arbitrary"),
    )(zc, zc, la, zc, zc, la)


def _gla_bwd(zc, la, spf, spr, dosum, t_len, l_ctx, name):
    rows = zc.shape[0]
    n_x = t_len // _GC
    n_c = n_x + l_ctx // _GC
    n_all = rows // _GC
    qkb, vb = ZC_QK // 512, ZC_V // 512

    def scan_of(c):
        return jnp.maximum(n_c - 1 - c, 0)

    def ch_f(c):
        return jnp.where(c < n_c, lax.rem(scan_of(c) + n_x, n_c), c)

    def ch_r(c):
        return c

    def do_of(ch):
        return jnp.minimum(ch, n_x - 1)

    def body(qkf_ref, vf_ref, laf_ref, spf_ref, dof_ref, qkr_ref, vr_ref, lar_ref, spr_ref, dor_ref,
             dqkf_ref, dvf_ref, dlaf_ref, dqkr_ref, dvr_ref, dlar_ref, dsf, dsr):
        c = pl.program_id(0)

        @pl.when(c == 0)
        def _():
            dsf[...] = jnp.zeros_like(dsf)
            dsr[...] = jnp.zeros_like(dsr)

        @pl.when(c >= n_c)
        def _():
            for r in (dqkf_ref, dvf_ref, dlaf_ref, dqkr_ref, dvr_ref, dlar_ref):
                r[...] = jnp.zeros_like(r)

        @pl.when(c < n_c)
        def _():
            for qk_ref, v_ref, la_ref, sp_ref, do_ref, dqk_ref, dv_ref, dla_ref, dst, reverse, ch in (
                    (qkf_ref, vf_ref, laf_ref, spf_ref, dof_ref, dqkf_ref, dvf_ref, dlaf_ref, dsf, False, ch_f(c)),
                    (qkr_ref, vr_ref, lar_ref, spr_ref, dor_ref, dqkr_ref, dvr_ref, dlar_ref, dsr, True, ch_r(c))):
                qk = qk_ref[...]
                v = v_ref[...]
                la = la_ref[...]
                dov = jnp.where(ch < n_x, do_ref[...], 0.0)
                sc = B_DK ** -0.5
                for hh in range(B_HEADS):
                    _, _, vh, _, mask, tri, _, gl, eg, eng, eend, qt, kt, ke = _gla_chunk_terms(qk, v, la, hh, reverse)
                    s_prev = sp_ref[0, hh]
                    ds_new = dst[hh]
                    doh = dov[:, hh * B_DV:(hh + 1) * B_DV]
                    dob = doh.astype(BF16)
                    vbh = vh.astype(BF16)
                    qtb, ktb, keb = qt.astype(BF16), kt.astype(BF16), ke.astype(BF16)
                    att = jnp.where(mask, _dot_nt(qtb, ktb), 0.0).astype(BF16)
                    datt = jnp.where(mask, _dot_nt(dob, vbh), 0.0).astype(BF16)
                    dqt = _dot(datt, ktb) + _dot(dob, s_prev.astype(BF16))
                    dkt = _dot_tn(datt, qtb)
                    dvh = _dot_tn(att, dob) + _dot_nt(keb, ds_new.astype(BF16))
                    dke = _dot(vbh, ds_new.astype(BF16))
                    egl = jnp.exp(gl)
                    dst[hh] = ds_new * egl + _dot_tn(dob, qtb)
                    dgl = (jnp.sum(dke * ke, axis=0, keepdims=True)
                           + jnp.sum(ds_new * s_prev, axis=0, keepdims=True) * egl)
                    dq = dqt * (sc * eg)
                    dk = dkt * eng + dke * eend
                    dg = dqt * qt - dkt * kt - dke * ke
                    dlah = _dot_tn_hi(tri, dg) + dgl
                    dqk_ref[:, hh * B_DK:(hh + 1) * B_DK] = dq
                    dqk_ref[:, 256 + hh * B_DK:256 + (hh + 1) * B_DK] = dk
                    dv_ref[:, hh * B_DV:(hh + 1) * B_DV] = dvh
                    dla_ref[:, hh * B_DK:(hh + 1) * B_DK] = dlah

    st_shape = (B_HEADS, B_DV, B_DK)

    def side(chf):
        return [pl.BlockSpec((_GC, 512), lambda c: (chf(c), qkb)),
                pl.BlockSpec((_GC, 512), lambda c: (chf(c), vb)),
                pl.BlockSpec((_GC, 512), lambda c: (chf(c), 0)),
                pl.BlockSpec((1,) + st_shape, lambda c: (scan_of(c), 0, 0, 0)),
                pl.BlockSpec((_GC, 512), lambda c: (do_of(chf(c)), 0))]

    def out_side(chf):
        return [pl.BlockSpec((_GC, 512), lambda c: (chf(c), 0)),
                pl.BlockSpec((_GC, 512), lambda c: (chf(c), 0)),
                pl.BlockSpec((_GC, 256), lambda c: (chf(c), 0))]

    shp = [jax.ShapeDtypeStruct((rows, 512), F32), jax.ShapeDtypeStruct((rows, 512), F32),
           jax.ShapeDtypeStruct((rows, 256), F32)]
    return pl.pallas_call(
        body, name=name, grid=(n_all,),
        in_specs=side(ch_f) + side(ch_r),
        out_specs=out_side(ch_f) + out_side(ch_r),
        out_shape=shp + shp,
        scratch_shapes=[pltpu.VMEM(st_shape, F32), pltpu.VMEM(st_shape, F32)],
        compiler_params=_cp("arbitrary"),
    )(zc, zc, la, spf, dosum, zc, zc, la, spr, dosum)


def _gla_out_fwd(o_a, o_f, o_r, zc, gla_g, t_len, name):
    tm = ROW_TILE
    rb = ZC_R // 512

    def body(oa_ref, of_ref, or_ref, r_ref, g_ref, cat_ref):
        osum = of_ref[...] + or_ref[...]
        g = g_ref[...]
        pieces = []
        for hh in range(B_HEADS):
            oh = osum[:, hh * B_DV:(hh + 1) * B_DV]
            rs = lax.rsqrt(jnp.mean(oh * oh, axis=-1, keepdims=True) + RMS_EPS)
            pieces.append((oh * rs) * g)
        r = r_ref[...]
        cat_ref[:, 0:512] = oa_ref[...].astype(BF16)
        cat_ref[:, 512:1024] = (jnp.concatenate(pieces, axis=1) * (r * _sigmoid(r))).astype(BF16)

    return pl.pallas_call(
        body, name=name, grid=(t_len // tm,),
        in_specs=[pl.BlockSpec((tm, 512), lambda i: (i, 0)),
                  pl.BlockSpec((tm, 512), lambda i: (i, 0)),
                  pl.BlockSpec((tm, 512), lambda i: (i, 0)),
                  pl.BlockSpec((tm, 512), lambda i: (i, rb)),
                  pl.BlockSpec((1, B_DV), lambda i: (0, 0))],
        out_specs=pl.BlockSpec((tm, D_MODEL), lambda i: (i, 0)),
        out_shape=jax.ShapeDtypeStruct((t_len, D_MODEL), BF16),
        compiler_params=_cp("parallel"),
    )(o_a, o_f, o_r, zc, gla_g)


def _gla_out_bwd(dcat, o_f, o_r, zc, gla_g, t_len, name):
    tm = ROW_TILE
    rb = ZC_R // 512

    def body(d_ref, of_ref, or_ref, r_ref, g_ref, dos_ref, dr_ref, dg_ref):
        i = pl.program_id(0)
        osum = of_ref[...] + or_ref[...]
        g = g_ref[...]
        r = r_ref[...]
        dgo = d_ref[...]
        sg = _sigmoid(r)
        dnrmg = dgo * (r * sg)
        nrms, dos = [], []
        dg_acc = jnp.zeros((1, B_DV), F32)
        for hh in range(B_HEADS):
            oh = osum[:, hh * B_DV:(hh + 1) * B_DV]
            rs = lax.rsqrt(jnp.mean(oh * oh, axis=-1, keepdims=True) + RMS_EPS)
            nrm = oh * rs
            dn = dnrmg[:, hh * B_DV:(hh + 1) * B_DV]
            dg_acc = dg_acc + jnp.sum(dn * nrm, axis=0, keepdims=True)
            dnn = dn * g
            dos.append(rs * (dnn - nrm * jnp.mean(dnn * nrm, axis=-1, keepdims=True)))
            nrms.append(nrm * g)
        dos_ref[...] = jnp.concatenate(dos, axis=1)
        dr_ref[...] = dgo * jnp.concatenate(nrms, axis=1) * (sg * (1.0 + r * (1.0 - sg)))

        @pl.when(i == 0)
        def _():
            dg_ref[...] = jnp.zeros_like(dg_ref)

        dg_ref[...] += dg_acc

    return pl.pallas_call(
        body, name=name, grid=(t_len // tm,),
        in_specs=[pl.BlockSpec((tm, 512), lambda i: (i, 1)),
                  pl.BlockSpec((tm, 512), lambda i: (i, 0)),
                  pl.BlockSpec((tm, 512), lambda i: (i, 0)),
                  pl.BlockSpec((tm, 512), lambda i: (i, rb)),
                  pl.BlockSpec((1, B_DV), lambda i: (0, 0))],
        out_specs=[pl.BlockSpec((tm, 512), lambda i: (i, 0)),
                   pl.BlockSpec((tm, 512), lambda i: (i, 0)),
                   pl.BlockSpec((1, B_DV), lambda i: (0, 0))],
        out_shape=[jax.ShapeDtypeStruct((t_len, 512), F32), jax.ShapeDtypeStruct((t_len, 512), F32),
                   jax.ShapeDtypeStruct((1, B_DV), F32)],
        compiler_params=_cp("arbitrary"),
    )(dcat, o_f, o_r, zc, gla_g)


def _mix_prep(dq, dkv, dqk_f, dqk_r, dv_f, dv_r, d_r, dla_f, dla_r, zc, wg2, bias2, cs, t_len, name):
    rows = zc.shape[0]
    tm = ROW_TILE
    n_x = t_len // tm
    gb = ZC_G // 128

    def xrow(i):
        return jnp.minimum(i, n_x - 1)

    def body(dq_ref, dkv_ref, dqkf_ref, dqkr_ref, dvf_ref, dvr_ref, dr_ref, dlaf_ref, dlar_ref, zg_ref, wg_ref,
             b_ref, cs_ref, dz_ref, dwg_ref, db_ref):
        i = pl.program_id(0)
        is_x = i < n_x
        cos = cs_ref[:, 0:128]
        sin = cs_ref[:, 128:256]
        cosq = jnp.concatenate([cos] * 4, axis=1)
        sinq = jnp.concatenate([sin] * 4, axis=1)
        dqv = jnp.where(is_x, dq_ref[...], 0.0)
        dz_ref[:, ZC_Q:ZC_QK] = (dqv * cosq + _swap16(dqv * sinq)).astype(BF16)
        dz_ref[:, ZC_QK:ZC_V] = (dqkf_ref[...] + dqkr_ref[...]).astype(BF16)
        dz_ref[:, ZC_V:ZC_R] = (dvf_ref[...] + dvr_ref[...]).astype(BF16)
        dz_ref[:, ZC_R:ZC_KV] = jnp.where(is_x, dr_ref[...], 0.0).astype(BF16)
        dk = dkv_ref[:, 0:128]
        dz_ref[:, ZC_KV:ZC_KV + 128] = (dk * cos + _swap16(dk * sin)).astype(BF16)
        dz_ref[:, ZC_KV + 128:ZC_G] = dkv_ref[:, 128:256].astype(BF16)
        zgb = zg_ref[...].astype(BF16)
        wg = wg_ref[...]
        pre = _dot(zgb, wg) + b_ref[...]
        dla = jnp.concatenate([dlaf_ref[...], dlar_ref[...]], axis=1)
        dpre = dla * (_sigmoid(-pre) / B_GATE_NORM)
        dpb = dpre.astype(BF16)
        dz_ref[:, ZC_G:ZC_W] = _dot_nt(dpb, wg).astype(BF16)

        @pl.when(i == 0)
        def _():
            dwg_ref[...] = jnp.zeros_like(dwg_ref)
            db_ref[...] = jnp.zeros_like(db_ref)

        dwg_ref[...] += _dot_tn(zgb, dpb)
        db_ref[...] += jnp.sum(dpre, axis=0, keepdims=True)

    return pl.pallas_call(
        body, name=name, grid=(rows // tm,),
        in_specs=[pl.BlockSpec((tm, 512), lambda i: (xrow(i), 0)),
                  pl.BlockSpec((tm, 256), lambda i: (i, 0)),
                  pl.BlockSpec((tm, 512), lambda i: (i, 0)),
                  pl.BlockSpec((tm, 512), lambda i: (i, 0)),
                  pl.BlockSpec((tm, 512), lambda i: (i, 0)),
                  pl.BlockSpec((tm, 512), lambda i: (i, 0)),
                  pl.BlockSpec((tm, 512), lambda i: (xrow(i), 0)),
                  pl.BlockSpec((tm, 256), lambda i: (i, 0)),
                  pl.BlockSpec((tm, 256), lambda i: (i, 0)),
                  pl.BlockSpec((tm, 128), lambda i: (i, gb)),
                  pl.BlockSpec((128, 512), lambda i: (0, 0)),
                  pl.BlockSpec((1, 512), lambda i: (0, 0)),
                  pl.BlockSpec((tm, 256), lambda i: (i, 0))],
        out_specs=[pl.BlockSpec((tm, ZC_W), lambda i: (i, 0)),
                   pl.BlockSpec((128, 512), lambda i: (0, 0)),
                   pl.BlockSpec((1, 512), lambda i: (0, 0))],
        out_shape=[jax.ShapeDtypeStruct((rows, ZC_W), BF16),
                   jax.ShapeDtypeStruct((128, 512), F32),
                   jax.ShapeDtypeStruct((1, 512), F32)],
        compiler_params=_cp("arbitrary"),
    )(dq, dkv, dqk_f, dqk_r, dv_f, dv_r, d_r, dla_f, dla_r, zc, wg2, bias2, cs)


def _gate_weights(w_a2_f, b_a_f, w_a2_b, b_a_b):
    wg2 = jnp.zeros((128, 512), F32)
    wg2 = wg2.at[0:B_GATE_RANK, 0:256].set(w_a2_f).at[B_GATE_RANK:2 * B_GATE_RANK, 256:512].set(w_a2_b)
    bias2 = jnp.concatenate([b_a_f, b_a_b]).reshape(1, 512)
    return wg2.astype(BF16), bias2


_WIN_PERM = ((0, 512), (768, 1280), (1280, 1792), (1792, 2304), (512, 768), (2304, 2336))


def _w_in_to_cat(w_in_full):
    parts = [w_in_full[:, a:b] for a, b in _WIN_PERM]
    parts.append(jnp.zeros((w_in_full.shape[0], ZC_W - PROJ_DIM), w_in_full.dtype))
    return jnp.concatenate(parts, axis=1)


def _cat_to_w_in(d_wcat):
    return jnp.concatenate([d_wcat[:, ZC_Q:ZC_QK], d_wcat[:, ZC_KV:ZC_G], d_wcat[:, ZC_QK:ZC_KV],
                            d_wcat[:, ZC_G:ZC_G + 2 * B_GATE_RANK]], axis=1)


def _mixer_ab_forward(x1, g3, mods, wcat, wg2, bias2, sink, gla_g, w_out, cs, t_len, l_ctx, n_x):
    h = _rms_mod_fwd(x1, g3, mods, 1, n_x, BF16, "mix0_mod")
    zc, la = _proj_fwd(h, wcat, wg2, bias2, cs, "mix0_proj")
    o_a = _attn_fwd(zc, sink, t_len, l_ctx, "mix0_attn")
    o_f, o_r, spf, spr = _gla_fwd(zc, la, t_len, l_ctx, "mix0_gla")
    cat = _gla_out_fwd(o_a, o_f, o_r, zc, gla_g, t_len, "mix0_glaout")
    x2, y = _matmul_resid(cat, w_out, x1, mods, 5, 1.0, n_x, t_len, "mix0_out")
    return x2, (x1, h, zc, la, o_a, o_f, o_r, spf, spr, cat, y)


def _mixer_ab_backward(dx2, saved, g3, mods, wcat, wg2, bias2, sink, gla_g, w_out, cs, t_len, l_ctx, n_x):
    x1, h, zc, la, o_a, o_f, o_r, spf, spr, cat, y = saved
    rows = x1.shape[0]
    tm = ROW_TILE
    dy, dgate = _gate_dy(dx2, y, mods, 5, 1.0, n_x, t_len, "mix0_dy")
    dcat = _matmul_nt(dy, w_out, "mix0_dcat")
    d_wout = _matmul_tn(
        cat, dy, pl.BlockSpec((tm, D_MODEL), lambda n, k: (k, 0)), pl.BlockSpec((tm, D_MODEL), lambda n, k: (k, 0)),
        (D_MODEL, D_MODEL), pl.BlockSpec((D_MODEL, D_MODEL), lambda n, k: (0, 0)), (1, t_len // tm), "mix0_dwout")
    dos, d_r, d_glag = _gla_out_bwd(dcat, o_f, o_r, zc, gla_g, t_len, "mix0_dglaout")
    dqk_f, dv_f, dla_f, dqk_r, dv_r, dla_r = _gla_bwd(zc, la, spf, spr, dos, t_len, l_ctx, "mix0_dgla")
    dq, dkv, dsink = _attn_bwd(zc, sink, o_a, dcat, t_len, l_ctx, "mix0_dattn")
    dzc, dwg2, dbias2 = _mix_prep(dq, dkv, dqk_f, dqk_r, dv_f, dv_r, d_r, dla_f, dla_r, zc, wg2, bias2, cs, t_len,
                                  "mix0_prep")
    d_wcat = _matmul_tn(
        h, dzc, pl.BlockSpec((tm, D_MODEL), lambda n, k: (k, 0)), pl.BlockSpec((tm, ZC_W), lambda n, k: (k, 0)),
        (D_MODEL, ZC_W), pl.BlockSpec((D_MODEL, ZC_W), lambda n, k: (0, 0)), (1, rows // tm), "mix0_dwin")
    pairs = [(dzc, pl.BlockSpec((tm, ZC_W), lambda i: (i, 0)), wcat, pl.BlockSpec((D_MODEL, ZC_W), lambda i: (0, 0)))]
    dx1, stats = _bwd_dx(pairs, x1, dx2, t_len // tm, g3, mods, 1, n_x, "mix0_dx")
    return dx1, stats, dgate, d_wcat, dwg2, dbias2, dsink, d_glag, d_wout


_PT = 256
_PH = 16


def _pool_band(n, t_len, w, transpose):
    shape = (_PT, _PT + 2 * _PH)
    a = n * _PT + lax.broadcasted_iota(jnp.int32, shape, 0)
    b = n * _PT - _PH + lax.broadcasted_iota(jnp.int32, shape, 1)
    t, s = (b, a) if transpose else (a, b)
    lo = jnp.maximum(t - w // 2, 0)
    hi = jnp.minimum(t + (w - w // 2), t_len)
    inside = (s >= lo) & (s < hi) & (t >= 0) & (t < t_len)
    mean = jnp.where(inside, 1.0 / (hi - lo).astype(F32), 0.0)
    return mean - jnp.where(s == t, 1.0, 0.0)


def _pool_halo(p_ref, c_ref, n_ref):
    return jnp.concatenate([p_ref[_PT - _PH:_PT, :], c_ref[...], n_ref[0:_PH, :]], axis=0)


def _pool_specs(t_len):
    nb = t_len // _PT
    return [pl.BlockSpec((_PT, D_MODEL), lambda n: (jnp.maximum(n - 1, 0), 0)),
            pl.BlockSpec((_PT, D_MODEL), lambda n: (n, 0)),
            pl.BlockSpec((_PT, D_MODEL), lambda n: (jnp.minimum(n + 1, nb - 1), 0))], nb


def _pool_fwd(h, wp, pscale, x1, mods, t_len, name):
    halo_specs, nb = _pool_specs(t_len)

    def body(hp_ref, hc_ref, hn_ref, w_ref, ps_ref, x_ref, m_ref, x2_ref, pooled_ref, ypre_ref):
        n = pl.program_id(0)
        hcat = _pool_halo(hp_ref, hc_ref, hn_ref)
        ys = []
        for gi, w in enumerate(POOL_WINDOWS):
            cols = slice(gi * POOL_GROUP, (gi + 1) * POOL_GROUP)
            pooled = _dot_hi(_pool_band(n, t_len, w, False), hcat[:, cols]).astype(BF16)
            pooled_ref[:, cols] = pooled
            ys.append(_dot(pooled, w_ref[gi]))
        ypre = jnp.concatenate(ys, axis=1)
        ypre_ref[...] = ypre
        x2_ref[...] = x_ref[...] + m_ref[0, 5:6, :] * (ypre * ps_ref[...])

    return pl.pallas_call(
        body, name=name, grid=(nb,),
        in_specs=halo_specs + [pl.BlockSpec((4, POOL_GROUP, POOL_GROUP), lambda n: (0, 0, 0)),
                               pl.BlockSpec((1, D_MODEL), lambda n: (0, 0)),
                               pl.BlockSpec((_PT, D_MODEL), lambda n: (n, 0)),
                               pl.BlockSpec((1, N_MOD, D_MODEL), lambda n: (0, 0, 0))],
        out_specs=[pl.BlockSpec((_PT, D_MODEL), lambda n: (n, 0))] * 3,
        out_shape=[jax.ShapeDtypeStruct((t_len, D_MODEL), F32), jax.ShapeDtypeStruct((t_len, D_MODEL), BF16),
                   jax.ShapeDtypeStruct((t_len, D_MODEL), F32)],
        compiler_params=_cp("parallel"),
    )(h, h, h, wp, pscale, x1, mods)


def _pool_bwd_a(dx2, ypre, wp, pscale, mods, t_len, name):
    nb = t_len // _PT

    def body(d_ref, y_ref, w_ref, ps_ref, m_ref, dyp_ref, dpl_ref, dgate_ref, dps_ref):
        n = pl.program_id(0)
        dv = d_ref[...]
        ypre = y_ref[...]
        ps = ps_ref[...]
        dy = dv * m_ref[0, 5:6, :]
        dyp = (dy * ps).astype(BF16)
        dyp_ref[...] = dyp
        for gi in range(len(POOL_WINDOWS)):
            cols = slice(gi * POOL_GROUP, (gi + 1) * POOL_GROUP)
            dpl_ref[:, cols] = _dot_nt(dyp[:, cols], w_ref[gi])

        @pl.when(n == 0)
        def _():
            dgate_ref[...] = jnp.zeros_like(dgate_ref)
            dps_ref[...] = jnp.zeros_like(dps_ref)

        dgate_ref[...] += jnp.sum(dv * (ypre * ps), axis=0, keepdims=True)
        dps_ref[...] += jnp.sum(dy * ypre, axis=0, keepdims=True)

    return pl.pallas_call(
        body, name=name, grid=(nb,),
        in_specs=[pl.BlockSpec((_PT, D_MODEL), lambda n: (n, 0)),
                  pl.BlockSpec((_PT, D_MODEL), lambda n: (n, 0)),
                  pl.BlockSpec((4, POOL_GROUP, POOL_GROUP), lambda n: (0, 0, 0)),
                  pl.BlockSpec((1, D_MODEL), lambda n: (0, 0)),
                  pl.BlockSpec((1, N_MOD, D_MODEL), lambda n: (0, 0, 0))],
        out_specs=[pl.BlockSpec((_PT, D_MODEL), lambda n: (n, 0)),
                   pl.BlockSpec((_PT, D_MODEL), lambda n: (n, 0)),
                   pl.BlockSpec((1, D_MODEL), lambda n: (0, 0)),
                   pl.BlockSpec((1, D_MODEL), lambda n: (0, 0))],
        out_shape=[jax.ShapeDtypeStruct((t_len, D_MODEL), BF16), jax.ShapeDtypeStruct((t_len, D_MODEL), F32),
                   jax.ShapeDtypeStruct((1, D_MODEL), F32), jax.ShapeDtypeStruct((1, D_MODEL), F32)],
        compiler_params=_cp("arbitrary"),
    )(dx2, ypre, wp, pscale, mods)


def _pool_bwd_dx(dpl, x1, dx2, g3, mods, t_len, name):
    halo_specs, nb = _pool_specs(t_len)

    def body(dp_ref, dc_ref, dn_ref, x_ref, d_ref, g_ref, m_ref, dx_ref, acc_ref):
        n = pl.program_id(0)
        dcat = _pool_halo(dp_ref, dc_ref, dn_ref)
        dhs = []
        for gi, w in enumerate(POOL_WINDOWS):
            cols = slice(gi * POOL_GROUP, (gi + 1) * POOL_GROUP)
            dhs.append(_dot_hi(_pool_band(n, t_len, w, True), dcat[:, cols]))
        dh = jnp.concatenate(dhs, axis=1)
        g = g_ref[1:2, :]
        scale = m_ref[0, 4:5, :]
        dx = _rms_mod_bwd_tail(dh, x_ref[...], g, scale, 0, acc_ref, n == 0)
        dx_ref[...] = d_ref[...] + dx

    return pl.pallas_call(
        body, name=name, grid=(nb,),
        in_specs=halo_specs + [pl.BlockSpec((_PT, D_MODEL), lambda n: (n, 0)),
                               pl.BlockSpec((_PT, D_MODEL), lambda n: (n, 0)),
                               pl.BlockSpec((3, D_MODEL), lambda n: (0, 0)),
                               pl.BlockSpec((1, N_MOD, D_MODEL), lambda n: (0, 0, 0))],
        out_specs=[pl.BlockSpec((_PT, D_MODEL), lambda n: (n, 0)),
                   pl.BlockSpec((2, 3, D_MODEL), lambda n: (0, 0, 0))],
        out_shape=[jax.ShapeDtypeStruct((t_len, D_MODEL), F32), jax.ShapeDtypeStruct((2, 3, D_MODEL), F32)],
        compiler_params=_cp("arbitrary"),
    )(dpl, dpl, dpl, x1, dx2, g3, mods)


def _mixer_pool_forward(x1, g3, mods, wp, pscale, t_len):
    h = _rms_mod_fwd(x1, g3, mods, 1, t_len // ROW_TILE, F32, "mix1_mod")
    x2, pooled, ypre = _pool_fwd(h, wp, pscale, x1, mods, t_len, "mix1_pool")
    return x2, (x1, pooled, ypre)


def _mixer_pool_backward(dx2, saved, g3, mods, wp, pscale, t_len):
    x1, pooled, ypre = saved
    tm = ROW_TILE
    dyp, dpl, dgate, dps = _pool_bwd_a(dx2, ypre, wp, pscale, mods, t_len, "mix1_da")
    d_wp = _matmul_tn(
        pooled, dyp, pl.BlockSpec((tm, POOL_GROUP), lambda g, k: (k, g)),
        pl.BlockSpec((tm, POOL_GROUP), lambda g, k: (k, g)),
        (4, POOL_GROUP, POOL_GROUP), pl.BlockSpec((1, POOL_GROUP, POOL_GROUP), lambda g, k: (g, 0, 0)),
        (4, t_len // tm), "mix1_dwp")
    dx1, stats = _pool_bwd_dx(dpl, x1, dx2, g3, mods, t_len, "mix1_dx")
    return dx1, stats, dgate, dps, d_wp


def _final_loss(x3, final_g, target, name):
    t_len = x3.shape[0]
    tm = ROW_TILE

    def body(x_ref, g_ref, t_ref, dx_ref, loss_ref, dg_ref):
        i = pl.program_id(0)
        xv = x_ref[...]
        g = g_ref[...]
        r = lax.rsqrt(jnp.mean(xv * xv, axis=-1, keepdims=True) + RMS_EPS)
        xhat = xv * r
        err = xhat * g - t_ref[...]
        part = 0.5 * jnp.sum(jnp.mean(err * err, axis=-1, keepdims=True), axis=0, keepdims=True)
        dy = err * (1.0 / D_MODEL)

        @pl.when(i == 0)
        def _():
            loss_ref[...] = jnp.zeros_like(loss_ref)
            dg_ref[...] = jnp.zeros_like(dg_ref)

        loss_ref[...] += jnp.broadcast_to(part, (1, 128))
        dg_ref[...] += jnp.sum(dy * xhat, axis=0, keepdims=True)
        dxh = dy * g
        dx_ref[...] = r * (dxh - xhat * jnp.mean(dxh * xhat, axis=-1, keepdims=True))

    return pl.pallas_call(
        body, name=name, grid=(t_len // tm,),
        in_specs=[pl.BlockSpec((tm, D_MODEL), lambda i: (i, 0)),
                  pl.BlockSpec((1, D_MODEL), lambda i: (0, 0)),
                  pl.BlockSpec((tm, D_MODEL), lambda i: (i, 0))],
        out_specs=[pl.BlockSpec((tm, D_MODEL), lambda i: (i, 0)),
                   pl.BlockSpec((1, 128), lambda i: (0, 0)),
                   pl.BlockSpec((1, D_MODEL), lambda i: (0, 0))],
        out_shape=[jax.ShapeDtypeStruct((t_len, D_MODEL), F32), jax.ShapeDtypeStruct((1, 128), F32),
                   jax.ShapeDtypeStruct((1, D_MODEL), F32)],
        compiler_params=_cp("arbitrary"),
    )(x3, final_g, target)


_CROWS = 16


def _adaln_fwd(c16, w_mod, bias_k, name):
    n_l, _, cols = w_mod.shape

    def body(c_ref, w_ref, b_ref, o_ref):
        cv = c_ref[...]
        sc = (cv * _sigmoid(cv)).astype(BF16)
        o_ref[0] = _dot(sc, w_ref[0].astype(BF16)) + b_ref[0]

    return pl.pallas_call(
        body, name=name, grid=(n_l,),
        in_specs=[pl.BlockSpec((_CROWS, D_MODEL), lambda l: (0, 0)),
                  pl.BlockSpec((1, D_MODEL, cols), lambda l: (l, 0, 0)),
                  pl.BlockSpec((1, 1, cols), lambda l: (l, 0, 0))],
        out_specs=pl.BlockSpec((1, _CROWS, cols), lambda l: (l, 0, 0)),
        out_shape=jax.ShapeDtypeStruct((n_l, _CROWS, cols), F32),
        compiler_params=_cp("parallel"),
    )(c16, w_mod, bias_k)


def _adaln_bwd(c16, d16, w_mod, dmmc_k, name):
    n_l, _, cols = w_mod.shape

    def body(c_ref, d_ref, w_ref, dm_ref, gw_ref, cp_ref):
        layer = pl.program_id(0)
        cv = c_ref[...]
        gw_ref[0] = _dot_tn_hi(cv * _sigmoid(cv), d_ref[0])

        @pl.when(layer == 0)
        def _():
            cp_ref[...] = jnp.sum(w_ref[0] * dm_ref[...], axis=1, keepdims=True)

    return pl.pallas_call(
        body, name=name, grid=(n_l,),
        in_specs=[pl.BlockSpec((_CROWS, D_MODEL), lambda l: (0, 0)),
                  pl.BlockSpec((1, _CROWS, cols), lambda l: (l, 0, 0)),
                  pl.BlockSpec((1, D_MODEL, cols), lambda l: (0, 0, 0)),
                  pl.BlockSpec((1, cols), lambda l: (0, 0))],
        out_specs=[pl.BlockSpec((1, D_MODEL, cols), lambda l: (l, 0, 0)),
                   pl.BlockSpec((D_MODEL, 1), lambda l: (0, 0))],
        out_shape=[jax.ShapeDtypeStruct((n_l, D_MODEL, cols), F32), jax.ShapeDtypeStruct((D_MODEL, 1), F32)],
        compiler_params=_cp("arbitrary"),
    )(c16, d16, w_mod, dmmc_k)


def _cctx_grad(cparts, c_ctx2, name):
    def body(p_ref, c_ref, o_ref):
        tot = ((p_ref[0] + p_ref[2]) + p_ref[4]) + p_ref[6]
        cv = c_ref[...]
        sg = _sigmoid(cv)
        o_ref[...] = tot * (sg * (1.0 + cv * (1.0 - sg)))

    return pl.pallas_call(
        body, name=name, out_shape=jax.ShapeDtypeStruct((8, 128), F32),
        in_specs=[pl.BlockSpec(memory_space=pltpu.VMEM), pl.BlockSpec(memory_space=pltpu.VMEM)],
        out_specs=pl.BlockSpec(memory_space=pltpu.VMEM),
    )(cparts, c_ctx2)


def _sum_devices(ga, name):
    def body(g_ref, o_ref):
        acc = g_ref[0]
        for d in range(1, N_DEV):
            acc = acc + g_ref[d]
        o_ref[...] = acc

    return pl.pallas_call(
        body, name=name, out_shape=jax.ShapeDtypeStruct(ga.shape[1:], F32),
        in_specs=[pl.BlockSpec(memory_space=pltpu.VMEM)], out_specs=pl.BlockSpec(memory_space=pltpu.VMEM),
    )(ga)


def _place():
    return lax.axis_index("x"), lax.axis_index("y"), lax.axis_index("c")


def _flip(a, d):
    return 1 - a if d else a


_CHIP_FLIPS = ((1, 0), (0, 1), (1, 1))


def _allgather_small(v, name):
    r, cc = v.shape

    def body(v_ref, out_ref, send_sems, recv_sems, local_sem):
        x, y, c = _place()
        me = 4 * x + 2 * y + c
        mine = pltpu.make_async_copy(v_ref, out_ref.at[me], local_sem)
        mine.start()
        sends = []
        for k in range(1, N_DEV):
            peer = (_flip(x, (k >> 2) & 1), _flip(y, (k >> 1) & 1), _flip(c, k & 1))
            cp = pltpu.make_async_remote_copy(src_ref=v_ref, dst_ref=out_ref.at[me], send_sem=send_sems.at[k - 1],
                                              recv_sem=recv_sems.at[k - 1], device_id=peer, device_id_type=MESH)
            cp.start()
            sends.append(cp)
        for k in range(1, N_DEV):
            px, py, pc = _flip(x, (k >> 2) & 1), _flip(y, (k >> 1) & 1), _flip(c, k & 1)
            pltpu.make_async_remote_copy(src_ref=v_ref, dst_ref=out_ref.at[4 * px + 2 * py + pc],
                                         send_sem=send_sems.at[k - 1], recv_sem=recv_sems.at[k - 1],
                                         device_id=(px, py, pc), device_id_type=MESH).wait_recv()
        for cp in sends:
            cp.wait_send()
        mine.wait()

    return pl.pallas_call(
        body, name=name, out_shape=jax.ShapeDtypeStruct((N_DEV, r, cc), F32),
        in_specs=[pl.BlockSpec(memory_space=pltpu.VMEM)], out_specs=pl.BlockSpec(memory_space=pltpu.VMEM),
        scratch_shapes=[pltpu.SemaphoreType.DMA((N_DEV - 1,)), pltpu.SemaphoreType.DMA((N_DEV - 1,)),
                        pltpu.SemaphoreType.DMA],
        compiler_params=pltpu.CompilerParams(vmem_limit_bytes=VMEM_LIMIT_BYTES),
    )(v)


_HBM_SPEC = pl.BlockSpec(memory_space=pltpu.HBM)
_SEM_SPEC = pl.BlockSpec(memory_space=pltpu.SEMAPHORE)
_EFFECT = pltpu.SideEffectType.DATAFLOW_SIDE_EFFECTING


def _in_hbm(a):
    return pltpu.with_memory_space_constraint(a, pltpu.HBM)


def _gather_start(arrs, groups, after, name):
    n, n_g = len(arrs), len(groups)

    def body(*refs):
        ins, zones = refs[:n], refs[n:2 * n]
        sems = refs[2 * n + 1:2 * n + 1 + 2 * n_g]
        token = refs[2 * n + 1 + 2 * n_g + 2 * n]
        local_sems = refs[-1]
        x, y, c = _place()
        k_me = 2 * x + y
        local_copies = []
        for a in range(n):
            lc = pltpu.make_async_copy(ins[a], zones[a].at[k_me], local_sems.at[a])
            lc.start()
            local_copies.append(lc)
        for lc in local_copies:
            lc.wait()
        for g, members in enumerate(groups):
            for t, a in enumerate(members):
                for j, (dx, dy) in enumerate(_CHIP_FLIPS):
                    pltpu.make_async_remote_copy(
                        src_ref=ins[a], dst_ref=zones[a].at[k_me], send_sem=sems[2 * g].at[3 * t + j],
                        recv_sem=sems[2 * g + 1].at[3 * t + j], device_id=(_flip(x, dx), _flip(y, dy), c),
                        device_id_type=MESH).start()
        token[...] = jnp.zeros_like(token)

    zones = [lax.empty((N_CHIPS,) + a.shape, a.dtype) for a in arrs]
    sem_shapes = []
    for members in groups:
        sem_shapes += [pltpu.SemaphoreType.DMA((3 * len(members),))] * 2
    outs = pl.pallas_call(
        body, name=name,
        out_shape=sem_shapes + [pltpu.HBM(a.shape, a.dtype) for a in arrs]
        + [pltpu.HBM(z.shape, z.dtype) for z in zones] + [jax.ShapeDtypeStruct((8, 128), F32)],
        in_specs=[_HBM_SPEC] * (2 * n) + [pl.BlockSpec(memory_space=pl.ANY)],
        out_specs=[_SEM_SPEC] * (2 * n_g) + [_HBM_SPEC] * (2 * n) + [pl.BlockSpec(memory_space=pltpu.VMEM)],
        input_output_aliases={i: 2 * n_g + i for i in range(2 * n)},
        scratch_shapes=[pltpu.SemaphoreType.DMA((n,))],
        compiler_params=pltpu.CompilerParams(has_side_effects=_EFFECT),
    )(*[_in_hbm(a) for a in arrs], *[_in_hbm(z) for z in zones], after)
    sems = outs[:2 * n_g]
    thru = outs[2 * n_g:2 * n_g + n]
    zones = outs[2 * n_g + n:2 * n_g + 2 * n]
    return [(sems[2 * g], sems[2 * g + 1]) for g in range(n_g)], thru, zones, outs[-1]


def _gather_wait(shards, zones, send_sems, recv_sems, after, name):
    m = len(shards)

    def body(*refs):
        ins, zs = refs[:m], refs[m:2 * m]
        ssem, rsem = refs[2 * m], refs[2 * m + 1]
        x, y, c = _place()
        for t in range(m):
            for j, (dx, dy) in enumerate(_CHIP_FLIPS):
                px, py = _flip(x, dx), _flip(y, dy)
                cp = pltpu.make_async_remote_copy(
                    src_ref=ins[t], dst_ref=zs[t].at[2 * px + py], send_sem=ssem.at[3 * t + j],
                    recv_sem=rsem.at[3 * t + j], device_id=(px, py, c), device_id_type=MESH)
                cp.wait_send()
                cp.wait_recv()

    outs = pl.pallas_call(
        body, name=name,
        out_shape=[pltpu.HBM(a.shape, a.dtype) for a in list(shards) + list(zones)],
        in_specs=[_HBM_SPEC] * (2 * m) + [_SEM_SPEC, _SEM_SPEC, pl.BlockSpec(memory_space=pl.ANY)],
        out_specs=[_HBM_SPEC] * (2 * m),
        input_output_aliases={i: i for i in range(2 * m)},
        compiler_params=pltpu.CompilerParams(has_side_effects=_EFFECT),
    )(*shards, *zones, send_sems, recv_sems, after)
    return outs[m:]


def _scatter_start(arrs, name):
    n = len(arrs)

    def body(*refs):
        ins, lands = refs[:n], refs[n:2 * n]
        ssem, rsem = refs[2 * n], refs[2 * n + 1]
        token = refs[2 * n + 2 + 2 * n]
        x, y, c = _place()
        for a in range(n):
            for j, (dx, dy) in enumerate(_CHIP_FLIPS):
                px, py = _flip(x, dx), _flip(y, dy)
                pltpu.make_async_remote_copy(
                    src_ref=ins[a].at[2 * px + py], dst_ref=lands[a].at[j], send_sem=ssem.at[3 * a + j],
                    recv_sem=rsem.at[3 * a + j], device_id=(px, py, c), device_id_type=MESH).start()
        token[...] = jnp.zeros_like(token)

    lands = [lax.empty((3,) + a.shape[1:], a.dtype) for a in arrs]
    outs = pl.pallas_call(
        body, name=name,
        out_shape=[pltpu.SemaphoreType.DMA((3 * n,))] * 2 + [pltpu.HBM(a.shape, a.dtype) for a in arrs]
        + [pltpu.HBM(z.shape, z.dtype) for z in lands] + [jax.ShapeDtypeStruct((8, 128), F32)],
        in_specs=[_HBM_SPEC] * (2 * n),
        out_specs=[_SEM_SPEC] * 2 + [_HBM_SPEC] * (2 * n) + [pl.BlockSpec(memory_space=pltpu.VMEM)],
        input_output_aliases={i: 2 + i for i in range(2 * n)},
        compiler_params=pltpu.CompilerParams(has_side_effects=_EFFECT),
    )(*[_in_hbm(a) for a in arrs], *[_in_hbm(z) for z in lands])
    return outs[0], outs[1], outs[2:2 + n], outs[2 + n:2 + 2 * n], outs[-1]


def _scatter_wait(arrs, lands, send_sems, recv_sems, after, name):
    n = len(arrs)

    def body(*refs):
        ins, lz = refs[:n], refs[n:2 * n]
        ssem, rsem = refs[2 * n], refs[2 * n + 1]
        x, y, c = _place()
        for a in range(n):
            for j, (dx, dy) in enumerate(_CHIP_FLIPS):
                px, py = _flip(x, dx), _flip(y, dy)
                cp = pltpu.make_async_remote_copy(
                    src_ref=ins[a].at[2 * px + py], dst_ref=lz[a].at[j], send_sem=ssem.at[3 * a + j],
                    recv_sem=rsem.at[3 * a + j], device_id=(px, py, c), device_id_type=MESH)
                cp.wait_send()
                cp.wait_recv()

    outs = pl.pallas_call(
        body, name=name,
        out_shape=[pltpu.HBM(a.shape, a.dtype) for a in list(arrs) + list(lands)],
        in_specs=[_HBM_SPEC] * (2 * n) + [_SEM_SPEC, _SEM_SPEC, pl.BlockSpec(memory_space=pl.ANY)],
        out_specs=[_HBM_SPEC] * (2 * n),
        input_output_aliases={i: i for i in range(2 * n)},
        compiler_params=pltpu.CompilerParams(has_side_effects=_EFFECT),
    )(*arrs, *lands, send_sems, recv_sems, after)
    return outs[:n], outs[n:]


def _swap_sibling(arrs, name):
    n = len(arrs)

    def body(*refs):
        ins, outs = refs[:n], refs[n:2 * n]
        send_sems, recv_sems = refs[2 * n:]
        x, y, c = _place()
        sends = []
        for a in range(n):
            cp = pltpu.make_async_remote_copy(src_ref=ins[a], dst_ref=outs[a], send_sem=send_sems.at[a],
                                              recv_sem=recv_sems.at[a], device_id=(x, y, 1 - c), device_id_type=MESH)
            cp.start()
            sends.append(cp)
        for cp in sends:
            cp.wait()

    any_spec = pl.BlockSpec(memory_space=pl.ANY)
    return pl.pallas_call(
        body, name=name,
        out_shape=[jax.ShapeDtypeStruct(a.shape, a.dtype) for a in arrs],
        in_specs=[any_spec] * n, out_specs=[any_spec] * n,
        scratch_shapes=[pltpu.SemaphoreType.DMA((n,)), pltpu.SemaphoreType.DMA((n,))],
    )(*arrs)


def _row_tile(rows, cols):
    for tr in (1024, 512, 256, 128, 64, 32, 16, 8):
        if rows % tr == 0 and tr * cols * 4 <= (1 << 20):
            return tr
    return rows


def _partial_sum(g_full, recv, k_idx, name):
    _, r, c = g_full.shape
    tr = _row_tile(r, c)

    def body(k_ref, g_ref, r_ref, o_ref):
        del k_ref
        acc = g_ref[0].astype(F32)
        for j in range(3):
            acc = acc + r_ref[j].astype(F32)
        o_ref[...] = acc

    return pl.pallas_call(
        body, name=name,
        grid_spec=pltpu.PrefetchScalarGridSpec(
            num_scalar_prefetch=1, grid=(r // tr,),
            in_specs=[pl.BlockSpec((1, tr, c), lambda i, k: (k[0], i, 0)),
                      pl.BlockSpec((3, tr, c), lambda i, k: (0, i, 0))],
            out_specs=pl.BlockSpec((tr, c), lambda i, k: (i, 0))),
        out_shape=jax.ShapeDtypeStruct((r, c), F32),
        compiler_params=_cp("parallel"),
    )(k_idx, g_full, recv)


def _adamw(w3, parts, m3, v3, layer, prev, name):
    n_l, r, c = w3.shape
    tr = _row_tile(r, c)
    n_i = r // tr
    n_p = len(parts)
    c1 = 1.0 - ADAM_B1 ** ADAM_STEP
    c2 = 1.0 - ADAM_B2 ** ADAM_STEP
    stacked = [isinstance(p, tuple) for p in parts]

    def body(*refs):
        w_ref, m_ref, v_ref = refs[0:3]
        g_refs = refs[3:3 + n_p]
        go_ref, d_ref, mo_ref, vo_ref = refs[-4:]
        g = None
        for p in range(n_p):
            term = g_refs[p][0] if stacked[p] else g_refs[p][...]
            g = term if g is None else g + term
        w = w_ref[0]
        m = ADAM_B1 * m_ref[0] + (1.0 - ADAM_B1) * g
        v = ADAM_B2 * v_ref[0] + (1.0 - ADAM_B2) * (g * g)
        m_hat = m / c1
        v_hat = v / c2
        go_ref[0] = g
        d_ref[0] = -ADAM_LR * (m_hat / (jnp.sqrt(v_hat) + ADAM_EPS) + ADAM_WD * w)
        mo_ref[0] = m
        vo_ref[0] = v

    blk = pl.BlockSpec((1, tr, c), lambda i: (layer, i, 0))
    in_specs = [blk, blk, blk]
    args = [w3, m3, v3]
    for part in parts:
        if isinstance(part, tuple):
            in_specs.append(pl.BlockSpec((1, tr, c), functools.partial(lambda idx, i: (idx, i, 0), part[1])))
            args.append(part[0])
        else:
            in_specs.append(pl.BlockSpec((tr, c), lambda i: (i, 0)))
            args.append(part)
    aliases = {}
    if prev is not None:
        in_specs += [pl.BlockSpec(memory_space=pl.ANY)] * 4
        aliases = {len(args) + q: q for q in range(4)}
        args += list(prev)
    shp = jax.ShapeDtypeStruct((n_l, r, c), F32)
    return pl.pallas_call(
        body, name=name, grid=(n_i,), in_specs=in_specs, out_specs=[blk] * 4, out_shape=[shp] * 4,
        input_output_aliases=aliases, compiler_params=_cp("parallel"),
    )(*args)


_SMALL_W = 4096
_PACK_ROWS = 352
_N9 = N_MOD * D_MODEL


def _flat_pad(parts, total):
    flat = jnp.concatenate([p.reshape(-1) for p in parts])
    return jnp.concatenate([flat, jnp.zeros((total - flat.shape[0],), F32)])


def kernel(x, c, ctx, c_ctx, w_mod, b_mod, norm_g, ffn1_wi, ffn1_wo, ffn2_wi, ffn2_wo, w_in, w_a2_f, b_a_f, w_a2_b, b_a_b, sink, gla_g, w_out, w_pool, pool_scale, final_g, loss_target, m_c_ctx, m_w_mod, m_b_mod, m_norm_g, m_ffn1_wi, m_ffn1_wo, m_ffn2_wi, m_ffn2_wo, m_w_in, m_w_a2_f, m_b_a_f, m_w_a2_b, m_b_a_b, m_sink, m_gla_g, m_w_out, m_w_pool, m_pool_scale, m_final_g, v_c_ctx, v_w_mod, v_b_mod, v_norm_g, v_ffn1_wi, v_ffn1_wo, v_ffn2_wi, v_ffn2_wo, v_w_in, v_w_a2_f, v_b_a_f, v_w_a2_b, v_b_a_b, v_sink, v_gla_g, v_w_out, v_w_pool, v_pool_scale, v_final_g):
    t_len, l_ctx = x.shape[1], ctx.shape[1]
    tm = ROW_TILE
    pad = (-(t_len + l_ctx)) % tm
    rows0 = t_len + l_ctx + pad
    n_x = t_len // tm
    xi, yi, ci = _place()
    k_me = 2 * xi + yi
    me = 4 * xi + 2 * yi + ci
    mod_cols = w_mod.shape[2]
    n_grp = len(POOL_WINDOWS)

    small_w = _flat_pad([norm_g, w_a2_f, w_a2_b, pool_scale], _SMALL_W).reshape(_SMALL_W // 128, 128)
    shards = [ffn1_wi[0], ffn1_wi[1], ffn1_wo[0], ffn1_wo[1], ffn2_wi[0], ffn2_wi[1], ffn2_wo[0], ffn2_wo[1],
              w_in[0], w_out[0], w_pool[0].reshape(n_grp * w_pool.shape[2], POOL_GROUP)]

    c_all = _allgather_small(c.reshape(8, 128), "gather_cond").reshape(N_DEV, D_MODEL)
    c16 = jnp.concatenate([c_all, c_ctx[None], jnp.zeros((_CROWS - N_DEV - 1, D_MODEL), F32)], axis=0)
    bias_k = lax.dynamic_slice(b_mod, (0, k_me * mod_cols), (2, mod_cols)).reshape(2, 1, mod_cols)
    mm_k = _adaln_fwd(c16, w_mod, bias_k, "adaln_fwd")
    mm_all = _allgather_small(mm_k.reshape(-1, 128), "gather_mod")

    groups = ([11, 0], [2], [8, 9], [4, 6], [1, 3], [10, 5, 7])
    g_sems, g_thru, g_zones, _ = _gather_start([s.astype(BF16) for s in shards] + [small_w], groups, mm_all,
                                               "gather_start")

    def gather_wait(g, after):
        members = groups[g]
        got = _gather_wait([g_thru[a] for a in members], [g_zones[a] for a in members], g_sems[g][0], g_sems[g][1],
                           after, "gather_wait_%d" % g)
        return dict(zip(members, got))

    mm_all = mm_all.reshape(N_DEV, 2, _CROWS, mod_cols)
    mm_full = jnp.concatenate([mm_all[2 * k] for k in range(N_CHIPS)], axis=-1)
    mm_x = lax.dynamic_index_in_dim(mm_full, me, axis=1, keepdims=False)
    mm_c = mm_full[:, N_DEV]
    mods = [jnp.stack([mm_x[l].reshape(N_MOD, D_MODEL), mm_c[l].reshape(N_MOD, D_MODEL)]) for l in range(2)]
    gathered = gather_wait(0, mods[0])
    sw = gathered[11].reshape(N_CHIPS, _SMALL_W)
    ng_n = norm_g.size
    a2_n = w_a2_f.size
    norm_g_full = jnp.concatenate([sw[k, :ng_n].reshape(norm_g.shape) for k in range(N_CHIPS)], axis=-1)
    w_a2_f_full = jnp.concatenate([sw[k, ng_n:ng_n + a2_n].reshape(w_a2_f.shape[1:]) for k in range(N_CHIPS)], axis=-1)
    w_a2_b_full = jnp.concatenate(
        [sw[k, ng_n + a2_n:ng_n + 2 * a2_n].reshape(w_a2_b.shape[1:]) for k in range(N_CHIPS)], axis=-1)
    pscale_full = jnp.concatenate(
        [sw[k, ng_n + 2 * a2_n:ng_n + 2 * a2_n + pool_scale.size] for k in range(N_CHIPS)]).reshape(1, D_MODEL)
    wg2, bias2 = _gate_weights(w_a2_f_full, b_a_f[0], w_a2_b_full, b_a_b[0])
    gla_g2 = gla_g.reshape(1, B_DV)
    final_g2 = final_g.reshape(1, D_MODEL)
    cs = _rope_tables(t_len, rows0)

    g3 = [norm_g_full[0], norm_g_full[1]]

    xcat = jnp.concatenate([x[0], ctx[0], jnp.zeros((pad, D_MODEL), F32)], axis=0)
    w1i, w1o, w2i, w2o = [None, None], [None, None], [None, None], [None, None]
    w1i[0] = gathered[0]
    x1, sv_a1, w1o[0] = _ffn_forward(xcat, g3[0], mods[0], 0, w1i[0], lambda s: gather_wait(1, s)[2], n_x, "l0_ffn1")
    gathered = gather_wait(2, x1)
    w_in_full = jnp.concatenate([gathered[8][k] for k in range(N_CHIPS)], axis=1)
    wcat = _w_in_to_cat(w_in_full)
    w_out_full = gathered[9].reshape(D_MODEL, D_MODEL)
    x2, sv_am = _mixer_ab_forward(x1, g3[0], mods[0], wcat, wg2, bias2, sink[0], gla_g2, w_out_full, cs,
                                  t_len, l_ctx, n_x)
    gathered = gather_wait(3, x2)
    w2i[0], w2o[0] = gathered[4], gathered[6]
    x3, sv_a2, _ = _ffn_forward(x2, g3[0], mods[0], 2, w2i[0], lambda s: w2o[0], n_x, "l0_ffn2")
    gathered = gather_wait(4, x3)
    w1i[1], w1o[1] = gathered[1], gathered[3]
    x4, sv_b1, _ = _ffn_forward(x3, g3[1], mods[1], 0, w1i[1], lambda s: w1o[1], n_x, "l1_ffn1")
    gathered = gather_wait(5, x4)
    w2i[1], w2o[1] = gathered[5], gathered[7]
    wp_full = gathered[10].reshape(N_CHIPS, n_grp, -1, POOL_GROUP).transpose(1, 0, 2, 3).reshape(
        n_grp, POOL_GROUP, POOL_GROUP)
    x5, sv_bm = _mixer_pool_forward(x4, g3[1], mods[1], wp_full, pscale_full, t_len)
    x6, sv_b2, _ = _ffn_forward(x5, g3[1], mods[1], 2, w2i[1], lambda s: w2o[1], n_x, "l1_ffn2")
    dx6, loss_part, d_final_g = _final_loss(x6, final_g2, loss_target[0], "final_loss")
    loss = lax.psum(loss_part[0, 0], ("x", "y", "c"))

    sent = []

    def sender(weight, layer):
        def send(grad, tag):
            nm = "%s_%s_%d" % (weight, tag, layer)
            ssem, rsem, thru, lands, token = _scatter_start([grad], "scatter_start_" + nm)
            sent.append((nm, weight + "_" + tag if tag else weight, layer, thru, lands, ssem, rsem))
            return token[0:1, 0:1]
        return send

    dx5, st_b2, dg_b2 = _ffn_backward(dx6, sv_b2, g3[1], mods[1], 2, w2i[1], w2o[1], n_x, sender("ffn2", 1),
                                      "l1_ffn2_b")
    dx4, st_bm, dg_bm, d_pscale, d_wp = _mixer_pool_backward(dx5, sv_bm, g3[1], mods[1], wp_full, pscale_full, t_len)
    d_wp4 = d_wp.reshape(n_grp, N_CHIPS, -1, POOL_GROUP).transpose(1, 0, 2, 3).reshape(N_CHIPS, -1, POOL_GROUP)
    mods1 = mods[1] + sender("w_pool", 0)(d_wp4, "")
    dx3, st_b1, dg_b1 = _ffn_backward(dx4, sv_b1, g3[1], mods1, 0, w1i[1], w1o[1], n_x, sender("ffn1", 1),
                                      "l1_ffn1_b")
    dx2, st_a2, dg_a2 = _ffn_backward(dx3, sv_a2, g3[0], mods[0], 2, w2i[0], w2o[0], n_x, sender("ffn2", 0),
                                      "l0_ffn2_b")
    dx1, st_am, dg_am, d_wcat, d_wg2, d_bias2, d_sink, d_glag, d_wout = _mixer_ab_backward(
        dx2, sv_am, g3[0], mods[0], wcat, wg2, bias2, sink[0], gla_g2, w_out_full, cs, t_len, l_ctx, n_x)
    d_w_in4 = _cat_to_w_in(d_wcat).reshape(D_MODEL, N_CHIPS, -1).transpose(1, 0, 2)
    mods0 = mods[0] + sender("w_in", 0)(d_w_in4, "") + sender("w_out", 0)(d_wout.reshape(N_CHIPS, -1, D_MODEL), "")
    dx0, st_a1, dg_a1 = _ffn_backward(dx1, sv_a1, g3[0], mods0, 0, w1i[0], w1o[0], n_x, sender("ffn1", 0),
                                      "l0_ffn1_b")
    grad_x = dx0[:t_len][None]

    def mod_row(st1, dg1, stm, dgm, st2, dg2, s):
        return jnp.concatenate([st1[s, 0], st1[s, 1], dg1[s, 0], stm[s, 0], stm[s, 1], dgm[s, 0],
                                st2[s, 0], st2[s, 1], dg2[s, 0]])

    dg_bm2 = jnp.concatenate([dg_bm, jnp.zeros_like(dg_bm)], axis=0)[:, None, :]
    d_mm_x0 = mod_row(st_a1, dg_a1, st_am, dg_am, st_a2, dg_a2, 0)
    d_mm_x1 = mod_row(st_b1, dg_b1, st_bm, dg_bm2, st_b2, dg_b2, 0)
    d_mm_c0 = mod_row(st_a1, dg_a1, st_am, dg_am, st_a2, dg_a2, 1)
    d_norm_g = jnp.stack([jnp.stack([st[0, 2] + st[1, 2] for st in (st_a1, st_am, st_a2)]),
                          jnp.stack([st[0, 2] + st[1, 2] for st in (st_b1, st_bm, st_b2)])])
    rk = B_GATE_RANK
    pack = _flat_pad([d_mm_x0, d_mm_x1, d_mm_c0, d_norm_g, d_bias2, d_wg2[0:rk, 0:256], d_wg2[rk:2 * rk, 256:512],
                      d_sink[:, 0], jnp.zeros((120,), F32), d_glag, d_pscale, d_final_g],
                     _PACK_ROWS * 128).reshape(_PACK_ROWS, 128)
    pack_all = _allgather_small(pack, "gather_small_grads")
    tot = _sum_devices(pack_all, "sum_small_grads").reshape(-1)
    rows_all = pack_all.reshape(N_DEV, -1)
    o = 3 * _N9
    g_norm_g_full = tot[o:o + 6 * D_MODEL].reshape(2, 3, D_MODEL)
    o += 6 * D_MODEL
    g_bias2 = tot[o:o + 512]
    o += 512
    g_w_a2_f_full = tot[o:o + rk * 256].reshape(rk, 256)
    o += rk * 256
    g_w_a2_b_full = tot[o:o + rk * 256].reshape(rk, 256)
    o += rk * 256
    g_sink = tot[o:o + A_HEADS]
    o += 128
    g_gla_g = tot[o:o + B_DV]
    o += B_DV
    g_pscale_full = tot[o:o + D_MODEL]
    o += D_MODEL
    g_final_g = tot[o:o + D_MODEL]
    d_mmc_tot = tot[2 * _N9:3 * _N9]
    g_b_mod = jnp.stack([tot[0:_N9] + d_mmc_tot, tot[_N9:2 * _N9]])

    zrows = jnp.zeros((_CROWS - N_DEV - 1, _N9), F32)
    d16 = jnp.stack([jnp.concatenate([rows_all[:, 0:_N9], d_mmc_tot[None], zrows], axis=0),
                     jnp.concatenate([rows_all[:, _N9:2 * _N9], jnp.zeros((1, _N9), F32), zrows], axis=0)])
    d16_k = lax.dynamic_slice(d16, (0, 0, k_me * mod_cols), (2, _CROWS, mod_cols))
    dmmc_k = lax.dynamic_slice(d_mmc_tot, (k_me * mod_cols,), (mod_cols,)).reshape(1, mod_cols)
    g_w_mod, c_part = _adaln_bwd(c16, d16_k, w_mod, dmmc_k, "adaln_bwd")
    c_parts = _allgather_small(c_part.reshape(8, 128), "gather_cctx")
    g_c_ctx = _cctx_grad(c_parts, c_ctx.reshape(8, 128), "cctx_grad").reshape(D_MODEL)

    def small(w, g, m, v, shape3, nm):
        return [o_.reshape(w.shape) for o_ in _adamw(w.reshape(shape3), [g.reshape(shape3[1:])],
                                                    m.reshape(shape3), v.reshape(shape3), 0, None, "adamw_" + nm)]

    def own(a, axis, size):
        return lax.dynamic_slice_in_dim(a, k_me * size, size, axis=axis)

    res = {}
    res["c_ctx"] = small(c_ctx, g_c_ctx, m_c_ctx, v_c_ctx, (1, 8, 128), "c_ctx")
    upd = _adamw(w_mod, [(g_w_mod, 1)], m_w_mod, v_w_mod, 1, None, "adamw_w_mod_1")
    res["w_mod"] = _adamw(w_mod, [(g_w_mod, 0)], m_w_mod, v_w_mod, 0, upd, "adamw_w_mod_0")
    res["b_mod"] = small(b_mod, g_b_mod, m_b_mod, v_b_mod, (1, 2, _N9), "b_mod")
    res["norm_g"] = small(norm_g, own(g_norm_g_full, 2, norm_g.shape[2]), m_norm_g, v_norm_g,
                          (1, 6, norm_g.shape[2]), "norm_g")
    res["w_a2_f"] = small(w_a2_f, own(g_w_a2_f_full, 1, w_a2_f.shape[2]), m_w_a2_f, v_w_a2_f,
                          (1, rk, w_a2_f.shape[2]), "w_a2_f")
    res["b_a_f"] = small(b_a_f, g_bias2[0:256], m_b_a_f, v_b_a_f, (1, 1, 256), "b_a_f")
    res["w_a2_b"] = small(w_a2_b, own(g_w_a2_b_full, 1, w_a2_b.shape[2]), m_w_a2_b, v_w_a2_b,
                          (1, rk, w_a2_b.shape[2]), "w_a2_b")
    res["b_a_b"] = small(b_a_b, g_bias2[256:512], m_b_a_b, v_b_a_b, (1, 1, 256), "b_a_b")
    res["sink"] = small(sink, g_sink, m_sink, v_sink, (1, 1, A_HEADS), "sink")
    res["gla_g"] = small(gla_g, g_gla_g, m_gla_g, v_gla_g, (1, 1, B_DV), "gla_g")
    res["pool_scale"] = small(pool_scale, own(g_pscale_full, 0, pool_scale.shape[1]), m_pool_scale, v_pool_scale,
                              (1, 1, pool_scale.shape[1]), "pool_scale")
    res["final_g"] = small(final_g, g_final_g, m_final_g, v_final_g, (1, 8, 128), "final_g")

    def as3(a):
        n_l = a.shape[0] if a.ndim == 3 else 1
        return a.reshape(n_l, -1, a.shape[-1])

    big_w = {"ffn1_wi": (ffn1_wi, m_ffn1_wi, v_ffn1_wi), "ffn1_wo": (ffn1_wo, m_ffn1_wo, v_ffn1_wo),
             "ffn2_wi": (ffn2_wi, m_ffn2_wi, v_ffn2_wi), "ffn2_wo": (ffn2_wo, m_ffn2_wo, v_ffn2_wo),
             "w_in": (w_in, m_w_in, v_w_in), "w_out": (w_out, m_w_out, v_w_out), "w_pool": (w_pool, m_w_pool, v_w_pool)}
    k_idx = k_me.reshape(1).astype(jnp.int32)
    chain = res["final_g"][0]
    for lo, hi in ((0, 2), (2, 5), (5, 7), (7, 9), (9, 11)):
        partial = []
        for nm, wname, layer, thru, lands, ssem, rsem in sent[lo:hi]:
            mine, recv = _scatter_wait(thru, lands, ssem, rsem, chain, "scatter_wait_" + nm)
            partial.append(_partial_sum(mine[0], recv[0], k_idx, "partial_sum_" + nm))
        other = _swap_sibling(partial, "swap_partials_%d" % lo)
        for (nm, wname, layer, _, _, _, _), p, q in zip(sent[lo:hi], partial, other):
            w, m, v = big_w[wname]
            res[wname] = _adamw(as3(w), [p, q], as3(m), as3(v), layer, res.get(wname),
                                "adamw_%s_%d" % (wname, layer))
            chain = res[wname][3]
    for wname, (w, _, _) in big_w.items():
        res[wname] = [o_.reshape(w.shape) for o_ in res[wname]]

    names = ["c_ctx", "w_mod", "b_mod", "norm_g", "ffn1_wi", "ffn1_wo", "ffn2_wi", "ffn2_wo", "w_in", "w_a2_f",
             "b_a_f", "w_a2_b", "b_a_b", "sink", "gla_g", "w_out", "w_pool", "pool_scale", "final_g"]
    outs = [loss, grad_x]
    for field in range(4):
        outs += [res[nm][field] for nm in names]
    return tuple(outs)
```

```python
import functools

import jax
import jax.numpy as jnp
import numpy as np
from jax import lax
from jax.experimental import pallas as pl
from jax.experimental.pallas import tpu as pltpu

F32 = jnp.float32
BF16 = jnp.bfloat16

D_MODEL = 1024
N_MOD = 9
D_FF = 2816
RMS_EPS = 1e-6
A_HEADS = 8
A_KV_HEADS = 2
A_HEAD_DIM = 64
WINDOW = 128
ROPE_BASE = 10000.0
GRID_W = 64
B_HEADS = 4
B_DK = 64
B_DV = 128
B_GATE_RANK = 16
B_GATE_NORM = 16.0
B_CHUNK = 64
POOL_WINDOWS = (2, 4, 8, 16)
POOL_GROUP = D_MODEL // len(POOL_WINDOWS)
PROJ_DIM = 2336

ADAM_LR = 0.001
ADAM_B1 = 0.9
ADAM_B2 = 0.999
ADAM_EPS = 1e-08
ADAM_WD = 0.01
ADAM_STEP = 10

N_CHIPS = 4
N_DEV = 8
ROW_TILE = 512
VMEM_LIMIT_BYTES = 56 * 1024 * 1024
MESH = pl.DeviceIdType.MESH

ZC_Q, ZC_QK, ZC_V, ZC_R, ZC_KV, ZC_G, ZC_W = 0, 512, 1024, 1536, 2048, 2304, 2432


def _cp(*sem):
    return pltpu.CompilerParams(dimension_semantics=sem if sem else None, vmem_limit_bytes=VMEM_LIMIT_BYTES)


def _dot(a, b):
    return jnp.dot(a, b, preferred_element_type=F32)


def _dot_nt(a, b):
    return lax.dot_general(a, b, (((1,), (1,)), ((), ())), preferred_element_type=F32)


def _dot_tn(a, b):
    return lax.dot_general(a, b, (((0,), (0,)), ((), ())), preferred_element_type=F32)


def _dot_hi(a, b):
    return jnp.dot(a, b, preferred_element_type=F32, precision=lax.Precision.HIGHEST)


def _dot_tn_hi(a, b):
    return lax.dot_general(a, b, (((0,), (0,)), ((), ())), preferred_element_type=F32,
                           precision=lax.Precision.HIGHEST)


def _sigmoid(x):
    return 1.0 / (1.0 + jnp.exp(-x))


def _stream_of(i, n_x):
    return jnp.where(i >= n_x, 1, 0)


def _rms_mod_fwd(x, g3, mods, j, n_x, out_dtype, name):
    rows = x.shape[0]
    tm = ROW_TILE
    n_i = rows // tm

    def body(x_ref, g_ref, m_ref, o_ref):
        xv = x_ref[...]
        r = lax.rsqrt(jnp.mean(xv * xv, axis=-1, keepdims=True) + RMS_EPS)
        g = g_ref[j:j + 1, :]
        shift = m_ref[0, 3 * j:3 * j + 1, :]
        scale = m_ref[0, 3 * j + 1:3 * j + 2, :]
        o_ref[...] = (((xv * r) * g) * (1.0 + scale) + shift).astype(out_dtype)

    return pl.pallas_call(
        body, name=name, grid=(n_i,),
        in_specs=[pl.BlockSpec((tm, D_MODEL), lambda i: (i, 0)),
                  pl.BlockSpec((3, D_MODEL), lambda i: (0, 0)),
                  pl.BlockSpec((1, N_MOD, D_MODEL), lambda i: (_stream_of(i, n_x), 0, 0))],
        out_specs=pl.BlockSpec((tm, D_MODEL), lambda i: (i, 0)),
        out_shape=jax.ShapeDtypeStruct((rows, D_MODEL), out_dtype),
        compiler_params=_cp("parallel"),
    )(x, g3, mods)


def _rms_mod_bwd_tail(dh, xv, g, scale, stream, acc_ref, first):
    r = lax.rsqrt(jnp.mean(xv * xv, axis=-1, keepdims=True) + RMS_EPS)
    xhat = xv * r
    t1 = jnp.sum(dh, axis=0, keepdims=True)
    t2 = jnp.sum(dh * xhat, axis=0, keepdims=True)
    stats = jnp.concatenate([t1, t2 * g, t2 * (1.0 + scale)], axis=0)

    @pl.when(first)
    def _():
        acc_ref[...] = jnp.zeros_like(acc_ref)

    acc_ref[pl.ds(stream, 1)] += stats[None]
    dxh = dh * (g * (1.0 + scale))
    return r * (dxh - xhat * jnp.mean(dxh * xhat, axis=-1, keepdims=True))


def _ffn_up(hn, w4, name):
    rows = hn.shape[0]
    h = w4.shape[2]
    tm = ROW_TILE
    n_i = rows // tm

    def body(h_ref, wa_ref, wu_ref, au_ref, s_ref):
        hv = h_ref[...]
        a = _dot(hv, wa_ref[0])
        u = _dot(hv, wu_ref[0])
        au_ref[0] = a.astype(BF16)
        au_ref[1] = u.astype(BF16)
        s_ref[...] = (a * _sigmoid(a) * u).astype(BF16)

    return pl.pallas_call(
        body, name=name, grid=(2, n_i),
        in_specs=[pl.BlockSpec((tm, D_MODEL), lambda j, i: (i, 0)),
                  pl.BlockSpec((1, D_MODEL, h), lambda j, i: (j, 0, 0)),
                  pl.BlockSpec((1, D_MODEL, h), lambda j, i: (j + 2, 0, 0))],
        out_specs=[pl.BlockSpec((2, tm, h), lambda j, i: (0, i, j)),
                   pl.BlockSpec((tm, h), lambda j, i: (i, j))],
        out_shape=[jax.ShapeDtypeStruct((2, rows, 2 * h), BF16),
                   jax.ShapeDtypeStruct((rows, 2 * h), BF16)],
        compiler_params=_cp("arbitrary", "arbitrary"),
    )(hn, w4, w4)


def _matmul_resid(a, w, xres, mods, gate_idx, coef, n_x, rows, name):
    k = a.shape[1]
    tm = ROW_TILE
    n_i = rows // tm

    def body(a_ref, w_ref, x_ref, m_ref, o_ref, f_ref):
        f = _dot(a_ref[...], w_ref[...])
        gate = m_ref[0, gate_idx:gate_idx + 1, :]
        f_ref[...] = f
        o_ref[...] = x_ref[...] + (coef * gate) * f

    return pl.pallas_call(
        body, name=name, grid=(n_i,),
        in_specs=[pl.BlockSpec((tm, k), lambda i: (i, 0)),
                  pl.BlockSpec((k, D_MODEL), lambda i: (0, 0)),
                  pl.BlockSpec((tm, D_MODEL), lambda i: (i, 0)),
                  pl.BlockSpec((1, N_MOD, D_MODEL), lambda i: (_stream_of(i, n_x), 0, 0))],
        out_specs=[pl.BlockSpec((tm, D_MODEL), lambda i: (i, 0)),
                   pl.BlockSpec((tm, D_MODEL), lambda i: (i, 0))],
        out_shape=[jax.ShapeDtypeStruct((rows, D_MODEL), F32),
                   jax.ShapeDtypeStruct((rows, D_MODEL), F32)],
        compiler_params=_cp("parallel"),
    )(a, w, xres, mods)


def _gate_dy(dout, f, mods, gate_idx, coef, n_x, rows, name):
    tm = ROW_TILE
    n_i = rows // tm

    def body(d_ref, f_ref, m_ref, dy_ref, acc_ref):
        i = pl.program_id(0)
        dv = d_ref[...]
        gate = m_ref[0, gate_idx:gate_idx + 1, :]
        dy_ref[...] = (dv * (coef * gate)).astype(BF16)

        @pl.when(i == 0)
        def _():
            acc_ref[...] = jnp.zeros_like(acc_ref)

        part = coef * jnp.sum(dv * f_ref[...], axis=0, keepdims=True)
        acc_ref[pl.ds(_stream_of(i, n_x), 1)] += part[None]

    return pl.pallas_call(
        body, name=name, grid=(n_i,),
        in_specs=[pl.BlockSpec((tm, D_MODEL), lambda i: (i, 0)),
                  pl.BlockSpec((tm, D_MODEL), lambda i: (i, 0)),
                  pl.BlockSpec((1, N_MOD, D_MODEL), lambda i: (_stream_of(i, n_x), 0, 0))],
        out_specs=[pl.BlockSpec((tm, D_MODEL), lambda i: (i, 0)),
                   pl.BlockSpec((2, 1, D_MODEL), lambda i: (0, 0, 0))],
        out_shape=[jax.ShapeDtypeStruct((rows, D_MODEL), BF16),
                   jax.ShapeDtypeStruct((2, 1, D_MODEL), F32)],
        compiler_params=_cp("arbitrary"),
    )(dout, f, mods)


def _ffn_bwd_dz(dy, wo2, au, name):
    rows = dy.shape[0]
    h = wo2.shape[1]
    tm = ROW_TILE
    n_i = rows // tm

    def body(dy_ref, wo_ref, au_ref, dz_ref):
        ds = _dot_nt(dy_ref[...], wo_ref[0])
        a = au_ref[0].astype(F32)
        u = au_ref[1].astype(F32)
        sg = _sigmoid(a)
        dz_ref[0] = (ds * u * (sg * (1.0 + a * (1.0 - sg)))).astype(BF16)
        dz_ref[1] = (ds * (a * sg)).astype(BF16)

    return pl.pallas_call(
        body, name=name, grid=(2, n_i),
        in_specs=[pl.BlockSpec((tm, D_MODEL), lambda j, i: (i, 0)),
                  pl.BlockSpec((1, h, D_MODEL), lambda j, i: (j, 0, 0)),
                  pl.BlockSpec((2, tm, h), lambda j, i: (0, i, j))],
        out_specs=pl.BlockSpec((2, tm, h), lambda j, i: (0, i, j)),
        out_shape=jax.ShapeDtypeStruct((2, rows, 2 * h), BF16),
        compiler_params=_cp("arbitrary", "arbitrary"),
    )(dy, wo2, au)


def _matmul_tn(a, b, a_spec, b_spec, out_shape, out_spec, grid, name):
    nd_a = len(a_spec.block_shape)
    nd_b = len(b_spec.block_shape)
    nd_o = len(out_spec.block_shape)
    k_axis = len(grid) - 1
    n_k = grid[k_axis]

    def body(a_ref, b_ref, o_ref, acc_ref):
        av = a_ref[(0,) * (nd_a - 2)]
        bv = b_ref[(0,) * (nd_b - 2)]
        part = _dot_tn(av, bv)
        k = pl.program_id(k_axis)

        @pl.when(k == 0)
        def _():
            acc_ref[...] = part

        @pl.when(k > 0)
        def _():
            acc_ref[...] += part

        @pl.when(k == n_k - 1)
        def _():
            o_ref[(0,) * (nd_o - 2)] = acc_ref[...].astype(BF16)

    return pl.pallas_call(
        body, name=name, grid=grid, in_specs=[a_spec, b_spec], out_specs=out_spec,
        out_shape=jax.ShapeDtypeStruct(out_shape, BF16),
        scratch_shapes=[pltpu.VMEM(tuple(out_spec.block_shape[-2:]), F32)],
        compiler_params=_cp(*(("arbitrary",) * len(grid))),
    )(a, b)


def _bwd_dx(pairs, x, dres, dres_tiles, g3, mods, j, n_x, name):
    rows = x.shape[0]
    tm = ROW_TILE
    n_i = rows // tm
    n_p = len(pairs)
    nds = [(len(p[1].block_shape), len(p[3].block_shape)) for p in pairs]

    def body(*refs):
        dz_refs = refs[0:2 * n_p:2]
        w_refs = refs[1:2 * n_p:2]
        x_ref, dres_ref, g_ref, m_ref, dx_ref, acc_ref = refs[2 * n_p:]
        i = pl.program_id(0)
        dh = None
        for p in range(n_p):
            dzv = dz_refs[p][(0,) * (nds[p][0] - 2)]
            wv = w_refs[p][(0,) * (nds[p][1] - 2)]
            part = _dot_nt(dzv, wv)
            dh = part if dh is None else dh + part
        g = g_ref[j:j + 1, :]
        scale = m_ref[0, 3 * j + 1:3 * j + 2, :]
        dx = _rms_mod_bwd_tail(dh, x_ref[...], g, scale, _stream_of(i, n_x), acc_ref, i == 0)
        dres_v = jnp.where(i < dres_tiles, dres_ref[...], 0.0)
        dx_ref[...] = dres_v + dx

    in_specs, args = [], []
    for dz, dz_spec, w, w_spec in pairs:
        in_specs += [dz_spec, w_spec]
        args += [dz, w]
    in_specs += [pl.BlockSpec((tm, D_MODEL), lambda i: (i, 0)),
                 pl.BlockSpec((tm, D_MODEL), lambda i: (jnp.minimum(i, dres_tiles - 1), 0)),
                 pl.BlockSpec((3, D_MODEL), lambda i: (0, 0)),
                 pl.BlockSpec((1, N_MOD, D_MODEL), lambda i: (_stream_of(i, n_x), 0, 0))]
    args += [x, dres, g3, mods]
    return pl.pallas_call(
        body, name=name, grid=(n_i,), in_specs=in_specs,
        out_specs=[pl.BlockSpec((tm, D_MODEL), lambda i: (i, 0)),
                   pl.BlockSpec((2, 3, D_MODEL), lambda i: (0, 0, 0))],
        out_shape=[jax.ShapeDtypeStruct((rows, D_MODEL), F32),
                   jax.ShapeDtypeStruct((2, 3, D_MODEL), F32)],
        compiler_params=_cp("arbitrary"),
    )(*args)


def _ffn_forward(x, g3, mods, j, w4_in, w4_out_of, n_x, name):
    rows = x.shape[0]
    hn = _rms_mod_fwd(x, g3, mods, j, n_x, BF16, name + "_mod")
    au, s = _ffn_up(hn, w4_in, name + "_up")
    w4_out, dep = w4_out_of(s)
    if dep is not None:
        mods = mods + dep[0:1, 0:1]
    wo = w4_out.reshape(D_FF, D_MODEL)
    out, f = _matmul_resid(s, wo, x, mods, 3 * j + 2, 0.5, n_x, rows, name + "_down")
    return out, (x, hn, au, s, f), w4_out


def _ffn_backward(dout, saved, g3, mods, j, w4_in, w4_out, n_x, send, name):
    x, hn, au, s, f = saved
    rows = x.shape[0]
    tm = ROW_TILE
    n_i = rows // tm
    h = w4_in.shape[2]
    dy, dgate = _gate_dy(dout, f, mods, 3 * j + 2, 0.5, n_x, rows, name + "_dy")
    wo2 = w4_out.reshape(2, h, D_MODEL)
    dz = _ffn_bwd_dz(dy, wo2, au, name + "_dz")
    d_wi = _matmul_tn(
        hn, dz, pl.BlockSpec((tm, D_MODEL), lambda q, k: (k, 0)),
        pl.BlockSpec((1, tm, h), lambda q, k: (q // 2, k, q % 2)),
        (4, D_MODEL, h), pl.BlockSpec((1, D_MODEL, h), lambda q, k: (q, 0, 0)), (4, n_i), name + "_dwi")
    mods = mods + send(d_wi, "wi")
    d_wo = _matmul_tn(
        s, dy, pl.BlockSpec((tm, h), lambda n, k: (k, n)), pl.BlockSpec((tm, D_MODEL), lambda n, k: (k, 0)),
        (D_FF, D_MODEL), pl.BlockSpec((h, D_MODEL), lambda n, k: (n, 0)), (2, n_i), name + "_dwo")
    mods = mods + send(d_wo.reshape(w4_out.shape), "wo")
    pairs = [(dz, pl.BlockSpec((1, tm, h), functools.partial(lambda q, i: (q // 2, i, q % 2), q)),
              w4_in, pl.BlockSpec((1, D_MODEL, h), functools.partial(lambda q, i: (q, 0, 0), q)))
             for q in range(4)]
    dx, stats = _bwd_dx(pairs, x, dout, n_i, g3, mods, j, n_x, name + "_dx")
    return dx, stats, dgate


def _matmul_nt(a, w, name):
    rows, k = a.shape
    n = w.shape[0]
    tm = ROW_TILE

    def body(a_ref, w_ref, o_ref):
        o_ref[...] = _dot_nt(a_ref[...], w_ref[...])

    return pl.pallas_call(
        body, name=name, grid=(rows // tm,),
        in_specs=[pl.BlockSpec((tm, k), lambda i: (i, 0)), pl.BlockSpec((n, k), lambda i: (0, 0))],
        out_specs=pl.BlockSpec((tm, n), lambda i: (i, 0)),
        out_shape=jax.ShapeDtypeStruct((rows, n), F32),
        compiler_params=_cp("parallel"),
    )(a, w)


def _rope_tables(t_len, rows):
    n = A_HEAD_DIM // 4
    freqs = ROPE_BASE ** (-jnp.arange(n, dtype=F32) / n)
    t = jnp.arange(t_len)
    ang_r = (t // GRID_W).astype(F32)[:, None] * freqs
    ang_c = (t % GRID_W).astype(F32)[:, None] * freqs
    cos = jnp.concatenate([jnp.cos(ang_r), jnp.cos(ang_r), jnp.cos(ang_c), jnp.cos(ang_c)], axis=1)
    sin = jnp.concatenate([-jnp.sin(ang_r), jnp.sin(ang_r), -jnp.sin(ang_c), jnp.sin(ang_c)], axis=1)
    cos = jnp.concatenate([cos, jnp.ones((rows - t_len, A_HEAD_DIM), F32)], axis=0)
    sin = jnp.concatenate([sin, jnp.zeros((rows - t_len, A_HEAD_DIM), F32)], axis=0)
    return jnp.concatenate([cos, cos, sin, sin], axis=1)


def _swap16(x):
    n = x.shape[1]
    lane = lax.broadcasted_iota(jnp.int32, x.shape, 1)
    first = jnp.bitwise_and(lane, 16) == 0
    return jnp.where(first, pltpu.roll(x, n - 16, 1), pltpu.roll(x, 16, 1))


def _log_sigmoid(x):
    return jnp.minimum(x, 0.0) - jnp.log(1.0 + jnp.exp(-jnp.abs(x)))


def _proj_fwd(h, wcat, wg2, bias2, cs, name):
    rows = h.shape[0]
    tm = ROW_TILE

    def body(h_ref, w_ref, wg_ref, b_ref, cs_ref, zc_ref, la_ref):
        z = _dot(h_ref[...], w_ref[...])
        cos = cs_ref[:, 0:128]
        sin = cs_ref[:, 128:256]
        cosq = jnp.concatenate([cos] * 4, axis=1)
        sinq = jnp.concatenate([sin] * 4, axis=1)
        q = z[:, ZC_Q:ZC_QK]
        zc_ref[:, ZC_Q:ZC_QK] = q * cosq + _swap16(q) * sinq
        zc_ref[:, ZC_QK:ZC_KV] = z[:, ZC_QK:ZC_KV]
        kk = z[:, ZC_KV:ZC_KV + 128]
        zc_ref[:, ZC_KV:ZC_KV + 128] = kk * cos + _swap16(kk) * sin
        zc_ref[:, ZC_KV + 128:ZC_W] = z[:, ZC_KV + 128:ZC_W]
        zg = z[:, ZC_G:ZC_W]
        pre = _dot(zg.astype(BF16), wg_ref[...]) + b_ref[...]
        la_ref[...] = _log_sigmoid(pre) / B_GATE_NORM

    return pl.pallas_call(
        body, name=name, grid=(rows // tm,),
        in_specs=[pl.BlockSpec((tm, D_MODEL), lambda i: (i, 0)),
                  pl.BlockSpec((D_MODEL, ZC_W), lambda i: (0, 0)),
                  pl.BlockSpec((128, 512), lambda i: (0, 0)),
                  pl.BlockSpec((1, 512), lambda i: (0, 0)),
                  pl.BlockSpec((tm, 256), lambda i: (i, 0))],
        out_specs=[pl.BlockSpec((tm, ZC_W), lambda i: (i, 0)),
                   pl.BlockSpec((tm, 512), lambda i: (i, 0))],
        out_shape=[jax.ShapeDtypeStruct((rows, ZC_W), F32),
                   jax.ShapeDtypeStruct((rows, 512), F32)],
        compiler_params=_cp("parallel"),
    )(h, wcat, wg2, bias2, cs)


_QB = WINDOW


def _attn_specs(t_len, l_ctx):
    nb = t_len // _QB
    kvb = ZC_KV // 256
    return [pl.BlockSpec(memory_space=pltpu.SMEM),
            pl.BlockSpec((_QB, 512), lambda n: (n, 0)),
            pl.BlockSpec((_QB, 256), lambda n: (jnp.maximum(n - 1, 0), kvb)),
            pl.BlockSpec((_QB, 256), lambda n: (n, kvb)),
            pl.BlockSpec((_QB, 256), lambda n: (n + 1, kvb)),
            pl.BlockSpec((l_ctx, 256), lambda n: (t_len // l_ctx, kvb))], nb


def _attn_probs(n, t_len, sink_ref, qv, kp, kc, kn, kx, g):
    hd = A_HEAD_DIM
    ks = slice(g * hd, (g + 1) * hd)
    vs = slice(128 + g * hd, 128 + (g + 1) * hd)
    kb = jnp.concatenate([kp[:, ks], kc[:, ks], kn[:, ks]], axis=0).astype(BF16)
    vb = jnp.concatenate([kp[:, vs], kc[:, vs], kn[:, vs]], axis=0).astype(BF16)
    kxb = kx[:, ks].astype(BF16)
    vxb = kx[:, vs].astype(BF16)
    qg = jnp.concatenate([qv[:, (4 * g + r) * hd:(4 * g + r + 1) * hd] for r in range(4)], axis=0).astype(BF16)
    qi = lax.broadcasted_iota(jnp.int32, (_QB, 3 * _QB), 0)
    kj = lax.broadcasted_iota(jnp.int32, (_QB, 3 * _QB), 1)
    kpos = n * _QB - _QB + kj
    valid = (kpos >= 0) & (kpos < t_len) & (jnp.abs(kj - _QB - qi) <= WINDOW)
    valid4 = jnp.concatenate([valid] * 4, axis=0)
    scale = hd ** -0.5
    s = jnp.where(valid4, _dot_nt(qg, kb) * scale, -jnp.inf)
    sc = _dot_nt(qg, kxb) * scale
    sk = jnp.concatenate([jnp.full((_QB, 1), sink_ref[4 * g + r], F32) for r in range(4)], axis=0)
    m = jnp.maximum(jnp.maximum(jnp.max(s, axis=-1, keepdims=True), jnp.max(sc, axis=-1, keepdims=True)), sk)
    p = jnp.exp(s - m)
    pc = jnp.exp(sc - m)
    ps = jnp.exp(sk - m)
    inv = 1.0 / (jnp.sum(p, axis=-1, keepdims=True) + jnp.sum(pc, axis=-1, keepdims=True) + ps)
    return p * inv, pc * inv, ps * inv, qg, kb, vb, kxb, vxb


def _attn_fwd(zc, sink, t_len, l_ctx, name):
    in_specs, nb = _attn_specs(t_len, l_ctx)

    def body(sink_ref, q_ref, kp_ref, kc_ref, kn_ref, kx_ref, o_ref):
        n = pl.program_id(0)
        outs = []
        for g in range(A_KV_HEADS):
            p, pc, _, _, _, vb, _, vxb = _attn_probs(
                n, t_len, sink_ref, q_ref[...], kp_ref[...], kc_ref[...], kn_ref[...], kx_ref[...], g)
            o = _dot(p.astype(BF16), vb) + _dot(pc.astype(BF16), vxb)
            outs += [o[r * _QB:(r + 1) * _QB] for r in range(4)]
        o_ref[...] = jnp.concatenate(outs, axis=1)

    return pl.pallas_call(
        body, name=name, grid=(nb,), in_specs=in_specs,
        out_specs=pl.BlockSpec((_QB, 512), lambda n: (n, 0)),
        out_shape=jax.ShapeDtypeStruct((t_len, 512), F32),
        compiler_params=_cp("parallel"),
    )(sink, zc, zc, zc, zc, zc)


def _attn_bwd(zc, sink, o, dcat, t_len, l_ctx, name):
    rows = zc.shape[0]
    in_specs, nb = _attn_specs(t_len, l_ctx)
    in_specs = in_specs + [pl.BlockSpec((_QB, 512), lambda n: (n, 0)), pl.BlockSpec((_QB, 512), lambda n: (n, 0))]
    hd = A_HEAD_DIM
    scale = hd ** -0.5

    def body(sink_ref, q_ref, kp_ref, kc_ref, kn_ref, kx_ref, o_ref, do_ref, dq_ref, dkv_ref, dsink_ref):
        n = pl.program_id(0)

        @pl.when(n == 0)
        def _():
            dkv_ref[...] = jnp.zeros_like(dkv_ref)
            dsink_ref[...] = jnp.zeros_like(dsink_ref)

        ov = o_ref[...]
        dov = do_ref[...]
        dqs, dkbs, dvbs, dkxs, dvxs = [], [], [], [], []
        for g in range(A_KV_HEADS):
            p, pc, ps, qg, kb, vb, kxb, vxb = _attn_probs(
                n, t_len, sink_ref, q_ref[...], kp_ref[...], kc_ref[...], kn_ref[...], kx_ref[...], g)
            og = jnp.concatenate([ov[:, (4 * g + r) * hd:(4 * g + r + 1) * hd] for r in range(4)], axis=0)
            dog = jnp.concatenate([dov[:, (4 * g + r) * hd:(4 * g + r + 1) * hd] for r in range(4)], axis=0)
            delta = jnp.sum(og * dog, axis=-1, keepdims=True)
            dogb = dog.astype(BF16)
            ds = (p * (_dot_nt(dogb, vb) - delta) * scale).astype(BF16)
            dsc = (pc * (_dot_nt(dogb, vxb) - delta) * scale).astype(BF16)
            dsk = ps * (0.0 - delta)
            dqg = _dot(ds, kb) + _dot(dsc, kxb)
            dqs += [dqg[r * _QB:(r + 1) * _QB] for r in range(4)]
            dkbs.append(_dot_tn(ds, qg))
            dvbs.append(_dot_tn(p.astype(BF16), dogb))
            dkxs.append(_dot_tn(dsc, qg))
            dvxs.append(_dot_tn(pc.astype(BF16), dogb))
            for r in range(4):
                hrow = 4 * g + r
                tot = jnp.sum(dsk[r * _QB:(r + 1) * _QB], axis=0, keepdims=True)
                dsink_ref[hrow:hrow + 1, :] += jnp.broadcast_to(tot, (1, 128))
        dq_ref[...] = jnp.concatenate(dqs, axis=1)
        band = jnp.concatenate(dkbs + dvbs, axis=1)
        ctxc = jnp.concatenate(dkxs + dvxs, axis=1)
        r_prev = pl.multiple_of(jnp.maximum(n - 1, 0) * _QB, _QB)
        r_cur = pl.multiple_of(n * _QB, _QB)
        r_next = pl.multiple_of((n + 1) * _QB, _QB)
        dkv_ref[pl.ds(r_prev, _QB), :] += band[0:_QB]
        dkv_ref[pl.ds(r_cur, _QB), :] += band[_QB:2 * _QB]
        dkv_ref[pl.ds(r_next, _QB), :] += band[2 * _QB:3 * _QB]
        dkv_ref[t_len:t_len + l_ctx, :] += ctxc

    return pl.pallas_call(
        body, name=name, grid=(nb,), in_specs=in_specs,
        out_specs=[pl.BlockSpec((_QB, 512), lambda n: (n, 0)),
                   pl.BlockSpec((rows, 256), lambda n: (0, 0)),
                   pl.BlockSpec((8, 128), lambda n: (0, 0))],
        out_shape=[jax.ShapeDtypeStruct((t_len, 512), F32),
                   jax.ShapeDtypeStruct((rows, 256), F32),
                   jax.ShapeDtypeStruct((8, 128), F32)],
        compiler_params=_cp("arbitrary"),
    )(sink, zc, zc, zc, zc, zc, o, dcat)


_GC = B_CHUNK


def _gla_chunk_terms(qk, v, la, head, reverse):
    q = qk[:, head * B_DK:(head + 1) * B_DK]
    k = qk[:, 256 + head * B_DK:256 + (head + 1) * B_DK]
    vh = v[:, head * B_DV:(head + 1) * B_DV]
    off = 256 if reverse else 0
    lah = la[:, off + head * B_DK:off + (head + 1) * B_DK]
    ii = lax.broadcasted_iota(jnp.int32, (_GC, _GC), 0)
    jj = lax.broadcasted_iota(jnp.int32, (_GC, _GC), 1)
    mask = (jj >= ii) if reverse else (jj <= ii)
    tri = jnp.where(mask, 1.0, 0.0).astype(F32)
    g = _dot_hi(tri, lah)
    gl = jnp.sum(lah, axis=0, keepdims=True)
    eg = jnp.exp(g)
    eng = jnp.exp(-g)
    eend = jnp.exp(gl - g)
    sc = B_DK ** -0.5
    qt = q * (sc * eg)
    kt = k * eng
    ke = k * eend
    return q, k, vh, lah, mask, tri, g, gl, eg, eng, eend, qt, kt, ke


def _gla_fwd(zc, la, dep, t_len, l_ctx, name):
    rows = zc.shape[0]
    n_x = t_len // _GC
    n_c = n_x + l_ctx // _GC
    qkb, vb = ZC_QK // 512, ZC_V // 512

    def ch_f(c):
        return lax.rem(c + n_x, n_c)

    def ch_r(c):
        return n_c - 1 - c

    def body(qkf_ref, vf_ref, laf_ref, qkr_ref, vr_ref, lar_ref, dep_ref, of_ref, or_ref, spf_ref, spr_ref, stf, strv):
        del dep_ref
        c = pl.program_id(0)

        @pl.when(c == 0)
        def _():
            stf[...] = jnp.zeros_like(stf)
            strv[...] = jnp.zeros_like(strv)

        for qk_ref, v_ref, la_ref, o_ref, sp_ref, st, reverse in (
                (qkf_ref, vf_ref, laf_ref, of_ref, spf_ref, stf, False),
                (qkr_ref, vr_ref, lar_ref, or_ref, spr_ref, strv, True)):
            qk = qk_ref[...]
            v = v_ref[...]
            la = la_ref[...]
            for hh in range(B_HEADS):
                _, _, vh, _, mask, _, _, gl, _, _, _, qt, kt, ke = _gla_chunk_terms(qk, v, la, hh, reverse)
                s_prev = st[hh]
                att = jnp.where(mask, _dot_nt(qt.astype(BF16), kt.astype(BF16)), 0.0)
                o = _dot(att.astype(BF16), vh.astype(BF16)) + _dot_nt(qt.astype(BF16), s_prev.astype(BF16))
                o_ref[:, hh * B_DV:(hh + 1) * B_DV] = o
                sp_ref[0, hh] = s_prev
                st[hh] = s_prev * jnp.exp(gl) + _dot_tn(vh.astype(BF16), ke.astype(BF16))

    st_shape = (B_HEADS, B_DV, B_DK)
    return pl.pallas_call(
        body, name=name, grid=(n_c,),
        in_specs=[pl.BlockSpec((_GC, 512), lambda c: (ch_f(c), qkb)),
                  pl.BlockSpec((_GC, 512), lambda c: (ch_f(c), vb)),
                  pl.BlockSpec((_GC, 512), lambda c: (ch_f(c), 0)),
                  pl.BlockSpec((_GC, 512), lambda c: (ch_r(c), qkb)),
                  pl.BlockSpec((_GC, 512), lambda c: (ch_r(c), vb)),
                  pl.BlockSpec((_GC, 512), lambda c: (ch_r(c), 0)),
                  pl.BlockSpec((8, 128), lambda c: (0, 0))],
        out_specs=[pl.BlockSpec((_GC, 512), lambda c: (ch_f(c), 0)),
                   pl.BlockSpec((_GC, 512), lambda c: (ch_r(c), 0)),
                   pl.BlockSpec((1,) + st_shape, lambda c: (c, 0, 0, 0)),
                   pl.BlockSpec((1,) + st_shape, lambda c: (c, 0, 0, 0))],
        out_shape=[jax.ShapeDtypeStruct((rows, 512), F32), jax.ShapeDtypeStruct((rows, 512), F32),
                   jax.ShapeDtypeStruct((n_c,) + st_shape, F32), jax.ShapeDtypeStruct((n_c,) + st_shape, F32)],
        scratch_shapes=[pltpu.VMEM(st_shape, F32), pltpu.VMEM(st_shape, F32)],
        compiler_params=_cp("arbitrary"),
    )(zc, zc, la, zc, zc, la, dep)


def _gla_bwd(zc, la, spf, spr, dosum, t_len, l_ctx, name):
    rows = zc.shape[0]
    n_x = t_len // _GC
    n_c = n_x + l_ctx // _GC
    n_all = rows // _GC
    qkb, vb = ZC_QK // 512, ZC_V // 512

    def scan_of(c):
        return jnp.maximum(n_c - 1 - c, 0)

    def ch_f(c):
        return jnp.where(c < n_c, lax.rem(scan_of(c) + n_x, n_c), c)

    def ch_r(c):
        return c

    def do_of(ch):
        return jnp.minimum(ch, n_x - 1)

    def body(qkf_ref, vf_ref, laf_ref, spf_ref, dof_ref, qkr_ref, vr_ref, lar_ref, spr_ref, dor_ref,
             dqkf_ref, dvf_ref, dlaf_ref, dqkr_ref, dvr_ref, dlar_ref, dsf, dsr):
        c = pl.program_id(0)

        @pl.when(c == 0)
        def _():
            dsf[...] = jnp.zeros_like(dsf)
            dsr[...] = jnp.zeros_like(dsr)

        @pl.when(c >= n_c)
        def _():
            for r in (dqkf_ref, dvf_ref, dlaf_ref, dqkr_ref, dvr_ref, dlar_ref):
                r[...] = jnp.zeros_like(r)

        @pl.when(c < n_c)
        def _():
            for qk_ref, v_ref, la_ref, sp_ref, do_ref, dqk_ref, dv_ref, dla_ref, dst, reverse, ch in (
                    (qkf_ref, vf_ref, laf_ref, spf_ref, dof_ref, dqkf_ref, dvf_ref, dlaf_ref, dsf, False, ch_f(c)),
                    (qkr_ref, vr_ref, lar_ref, spr_ref, dor_ref, dqkr_ref, dvr_ref, dlar_ref, dsr, True, ch_r(c))):
                qk = qk_ref[...]
                v = v_ref[...]
                la = la_ref[...]
                dov = jnp.where(ch < n_x, do_ref[...], 0.0)
                sc = B_DK ** -0.5
                for hh in range(B_HEADS):
                    _, _, vh, _, mask, tri, _, gl, eg, eng, eend, qt, kt, ke = _gla_chunk_terms(qk, v, la, hh, reverse)
                    s_prev = sp_ref[0, hh]
                    ds_new = dst[hh]
                    doh = dov[:, hh * B_DV:(hh + 1) * B_DV]
                    dob = doh.astype(BF16)
                    vbh = vh.astype(BF16)
                    qtb, ktb, keb = qt.astype(BF16), kt.astype(BF16), ke.astype(BF16)
                    att = jnp.where(mask, _dot_nt(qtb, ktb), 0.0).astype(BF16)
                    datt = jnp.where(mask, _dot_nt(dob, vbh), 0.0).astype(BF16)
                    dqt = _dot(datt, ktb) + _dot(dob, s_prev.astype(BF16))
                    dkt = _dot_tn(datt, qtb)
                    dvh = _dot_tn(att, dob) + _dot_nt(keb, ds_new.astype(BF16))
                    dke = _dot(vbh, ds_new.astype(BF16))
                    egl = jnp.exp(gl)
                    dst[hh] = ds_new * egl + _dot_tn(dob, qtb)
                    dgl = (jnp.sum(dke * ke, axis=0, keepdims=True)
                           + jnp.sum(ds_new * s_prev, axis=0, keepdims=True) * egl)
                    dq = dqt * (sc * eg)
                    dk = dkt * eng + dke * eend
                    dg = dqt * qt - dkt * kt - dke * ke
                    dlah = _dot_tn_hi(tri, dg) + dgl
                    dqk_ref[:, hh * B_DK:(hh + 1) * B_DK] = dq
                    dqk_ref[:, 256 + hh * B_DK:256 + (hh + 1) * B_DK] = dk
                    dv_ref[:, hh * B_DV:(hh + 1) * B_DV] = dvh
                    dla_ref[:, hh * B_DK:(hh + 1) * B_DK] = dlah

    st_shape = (B_HEADS, B_DV, B_DK)

    def side(chf):
        return [pl.BlockSpec((_GC, 512), lambda c: (chf(c), qkb)),
                pl.BlockSpec((_GC, 512), lambda c: (chf(c), vb)),
                pl.BlockSpec((_GC, 512), lambda c: (chf(c), 0)),
                pl.BlockSpec((1,) + st_shape, lambda c: (scan_of(c), 0, 0, 0)),
                pl.BlockSpec((_GC, 512), lambda c: (do_of(chf(c)), 0))]

    def out_side(chf):
        return [pl.BlockSpec((_GC, 512), lambda c: (chf(c), 0)),
                pl.BlockSpec((_GC, 512), lambda c: (chf(c), 0)),
                pl.BlockSpec((_GC, 256), lambda c: (chf(c), 0))]

    shp = [jax.ShapeDtypeStruct((rows, 512), F32), jax.ShapeDtypeStruct((rows, 512), F32),
           jax.ShapeDtypeStruct((rows, 256), F32)]
    return pl.pallas_call(
        body, name=name, grid=(n_all,),
        in_specs=side(ch_f) + side(ch_r),
        out_specs=out_side(ch_f) + out_side(ch_r),
        out_shape=shp + shp,
        scratch_shapes=[pltpu.VMEM(st_shape, F32), pltpu.VMEM(st_shape, F32)],
        compiler_params=_cp("arbitrary"),
    )(zc, zc, la, spf, dosum, zc, zc, la, spr, dosum)


def _gla_out_fwd(o_a, o_f, o_r, zc, gla_g, t_len, name):
    tm = ROW_TILE
    rb = ZC_R // 512

    def body(oa_ref, of_ref, or_ref, r_ref, g_ref, cat_ref):
        osum = of_ref[...] + or_ref[...]
        g = g_ref[...]
        pieces = []
        for hh in range(B_HEADS):
            oh = osum[:, hh * B_DV:(hh + 1) * B_DV]
            rs = lax.rsqrt(jnp.mean(oh * oh, axis=-1, keepdims=True) + RMS_EPS)
            pieces.append((oh * rs) * g)
        r = r_ref[...]
        cat_ref[:, 0:512] = oa_ref[...].astype(BF16)
        cat_ref[:, 512:1024] = (jnp.concatenate(pieces, axis=1) * (r * _sigmoid(r))).astype(BF16)

    return pl.pallas_call(
        body, name=name, grid=(t_len // tm,),
        in_specs=[pl.BlockSpec((tm, 512), lambda i: (i, 0)),
                  pl.BlockSpec((tm, 512), lambda i: (i, 0)),
                  pl.BlockSpec((tm, 512), lambda i: (i, 0)),
                  pl.BlockSpec((tm, 512), lambda i: (i, rb)),
                  pl.BlockSpec((1, B_DV), lambda i: (0, 0))],
        out_specs=pl.BlockSpec((tm, D_MODEL), lambda i: (i, 0)),
        out_shape=jax.ShapeDtypeStruct((t_len, D_MODEL), BF16),
        compiler_params=_cp("parallel"),
    )(o_a, o_f, o_r, zc, gla_g)


def _gla_out_bwd(dcat, o_f, o_r, zc, gla_g, t_len, name):
    tm = ROW_TILE
    rb = ZC_R // 512

    def body(d_ref, of_ref, or_ref, r_ref, g_ref, dos_ref, dr_ref, dg_ref):
        i = pl.program_id(0)
        osum = of_ref[...] + or_ref[...]
        g = g_ref[...]
        r = r_ref[...]
        dgo = d_ref[...]
        sg = _sigmoid(r)
        dnrmg = dgo * (r * sg)
        nrms, dos = [], []
        dg_acc = jnp.zeros((1, B_DV), F32)
        for hh in range(B_HEADS):
            oh = osum[:, hh * B_DV:(hh + 1) * B_DV]
            rs = lax.rsqrt(jnp.mean(oh * oh, axis=-1, keepdims=True) + RMS_EPS)
            nrm = oh * rs
            dn = dnrmg[:, hh * B_DV:(hh + 1) * B_DV]
            dg_acc = dg_acc + jnp.sum(dn * nrm, axis=0, keepdims=True)
            dnn = dn * g
            dos.append(rs * (dnn - nrm * jnp.mean(dnn * nrm, axis=-1, keepdims=True)))
            nrms.append(nrm * g)
        dos_ref[...] = jnp.concatenate(dos, axis=1)
        dr_ref[...] = dgo * jnp.concatenate(nrms, axis=1) * (sg * (1.0 + r * (1.0 - sg)))

        @pl.when(i == 0)
        def _():
            dg_ref[...] = jnp.zeros_like(dg_ref)

        dg_ref[...] += dg_acc

    return pl.pallas_call(
        body, name=name, grid=(t_len // tm,),
        in_specs=[pl.BlockSpec((tm, 512), lambda i: (i, 1)),
                  pl.BlockSpec((tm, 512), lambda i: (i, 0)),
                  pl.BlockSpec((tm, 512), lambda i: (i, 0)),
                  pl.BlockSpec((tm, 512), lambda i: (i, rb)),
                  pl.BlockSpec((1, B_DV), lambda i: (0, 0))],
        out_specs=[pl.BlockSpec((tm, 512), lambda i: (i, 0)),
                   pl.BlockSpec((tm, 512), lambda i: (i, 0)),
                   pl.BlockSpec((1, B_DV), lambda i: (0, 0))],
        out_shape=[jax.ShapeDtypeStruct((t_len, 512), F32), jax.ShapeDtypeStruct((t_len, 512), F32),
                   jax.ShapeDtypeStruct((1, B_DV), F32)],
        compiler_params=_cp("arbitrary"),
    )(dcat, o_f, o_r, zc, gla_g)


def _mix_prep(dq, dkv, dqk_f, dqk_r, dv_f, dv_r, d_r, dla_f, dla_r, zc, wg2, bias2, cs, t_len, name):
    rows = zc.shape[0]
    tm = ROW_TILE
    n_x = t_len // tm
    gb = ZC_G // 128

    def xrow(i):
        return jnp.minimum(i, n_x - 1)

    def body(dq_ref, dkv_ref, dqkf_ref, dqkr_ref, dvf_ref, dvr_ref, dr_ref, dlaf_ref, dlar_ref, zg_ref, wg_ref,
             b_ref, cs_ref, dz_ref, dwg_ref, db_ref):
        i = pl.program_id(0)
        is_x = i < n_x
        cos = cs_ref[:, 0:128]
        sin = cs_ref[:, 128:256]
        cosq = jnp.concatenate([cos] * 4, axis=1)
        sinq = jnp.concatenate([sin] * 4, axis=1)
        dqv = jnp.where(is_x, dq_ref[...], 0.0)
        dz_ref[:, ZC_Q:ZC_QK] = (dqv * cosq + _swap16(dqv * sinq)).astype(BF16)
        dz_ref[:, ZC_QK:ZC_V] = (dqkf_ref[...] + dqkr_ref[...]).astype(BF16)
        dz_ref[:, ZC_V:ZC_R] = (dvf_ref[...] + dvr_ref[...]).astype(BF16)
        dz_ref[:, ZC_R:ZC_KV] = jnp.where(is_x, dr_ref[...], 0.0).astype(BF16)
        dk = dkv_ref[:, 0:128]
        dz_ref[:, ZC_KV:ZC_KV + 128] = (dk * cos + _swap16(dk * sin)).astype(BF16)
        dz_ref[:, ZC_KV + 128:ZC_G] = dkv_ref[:, 128:256].astype(BF16)
        zgb = zg_ref[...].astype(BF16)
        wg = wg_ref[...]
        pre = _dot(zgb, wg) + b_ref[...]
        dla = jnp.concatenate([dlaf_ref[...], dlar_ref[...]], axis=1)
        dpre = dla * (_sigmoid(-pre) / B_GATE_NORM)
        dpb = dpre.astype(BF16)
        dz_ref[:, ZC_G:ZC_W] = _dot_nt(dpb, wg).astype(BF16)

        @pl.when(i == 0)
        def _():
            dwg_ref[...] = jnp.zeros_like(dwg_ref)
            db_ref[...] = jnp.zeros_like(db_ref)

        dwg_ref[...] += _dot_tn(zgb, dpb)
        db_ref[...] += jnp.sum(dpre, axis=0, keepdims=True)

    return pl.pallas_call(
        body, name=name, grid=(rows // tm,),
        in_specs=[pl.BlockSpec((tm, 512), lambda i: (xrow(i), 0)),
                  pl.BlockSpec((tm, 256), lambda i: (i, 0)),
                  pl.BlockSpec((tm, 512), lambda i: (i, 0)),
                  pl.BlockSpec((tm, 512), lambda i: (i, 0)),
                  pl.BlockSpec((tm, 512), lambda i: (i, 0)),
                  pl.BlockSpec((tm, 512), lambda i: (i, 0)),
                  pl.BlockSpec((tm, 512), lambda i: (xrow(i), 0)),
                  pl.BlockSpec((tm, 256), lambda i: (i, 0)),
                  pl.BlockSpec((tm, 256), lambda i: (i, 0)),
                  pl.BlockSpec((tm, 128), lambda i: (i, gb)),
                  pl.BlockSpec((128, 512), lambda i: (0, 0)),
                  pl.BlockSpec((1, 512), lambda i: (0, 0)),
                  pl.BlockSpec((tm, 256), lambda i: (i, 0))],
        out_specs=[pl.BlockSpec((tm, ZC_W), lambda i: (i, 0)),
                   pl.BlockSpec((128, 512), lambda i: (0, 0)),
                   pl.BlockSpec((1, 512), lambda i: (0, 0))],
        out_shape=[jax.ShapeDtypeStruct((rows, ZC_W), BF16),
                   jax.ShapeDtypeStruct((128, 512), F32),
                   jax.ShapeDtypeStruct((1, 512), F32)],
        compiler_params=_cp("arbitrary"),
    )(dq, dkv, dqk_f, dqk_r, dv_f, dv_r, d_r, dla_f, dla_r, zc, wg2, bias2, cs)


def _gate_weights(w_a2_f, b_a_f, w_a2_b, b_a_b):
    wg2 = jnp.zeros((128, 512), F32)
    wg2 = wg2.at[0:B_GATE_RANK, 0:256].set(w_a2_f).at[B_GATE_RANK:2 * B_GATE_RANK, 256:512].set(w_a2_b)
    bias2 = jnp.concatenate([b_a_f, b_a_b]).reshape(1, 512)
    return wg2.astype(BF16), bias2


_WIN_PERM = ((0, 512), (768, 1280), (1280, 1792), (1792, 2304), (512, 768), (2304, 2336))


def _w_in_to_cat(w_in_full):
    parts = [w_in_full[:, a:b] for a, b in _WIN_PERM]
    parts.append(jnp.zeros((w_in_full.shape[0], ZC_W - PROJ_DIM), w_in_full.dtype))
    return jnp.concatenate(parts, axis=1)


def _cat_to_w_in(d_wcat):
    return jnp.concatenate([d_wcat[:, ZC_Q:ZC_QK], d_wcat[:, ZC_KV:ZC_G], d_wcat[:, ZC_QK:ZC_KV],
                            d_wcat[:, ZC_G:ZC_G + 2 * B_GATE_RANK]], axis=1)


def _mixer_ab_forward(x1, g3, mods, wcat, wg2, bias2, sink, gla_g, w_out, cs, t_len, l_ctx, n_x, pace):
    h = _rms_mod_fwd(x1, g3, mods, 1, n_x, BF16, "mix0_mod")
    zc, la = _proj_fwd(h, wcat, wg2, bias2, cs, "mix0_proj")
    dep = pace("proj", zc)
    o_a = _attn_fwd(zc, sink + dep[0, 0], t_len, l_ctx, "mix0_attn")
    dep = pace("attn", o_a)
    o_f, o_r, spf, spr = _gla_fwd(zc, la, dep, t_len, l_ctx, "mix0_gla")
    dep = pace("gla", o_f)
    cat = _gla_out_fwd(o_a, o_f, o_r, zc, gla_g + dep[0:1, 0:1], t_len, "mix0_glaout")
    x2, y = _matmul_resid(cat, w_out, x1, mods, 5, 1.0, n_x, t_len, "mix0_out")
    return x2, (x1, h, zc, la, o_a, o_f, o_r, spf, spr, cat, y)


def _mixer_ab_backward(dx2, saved, g3, mods, wcat, wg2, bias2, sink, gla_g, w_out, cs, t_len, l_ctx, n_x):
    x1, h, zc, la, o_a, o_f, o_r, spf, spr, cat, y = saved
    rows = x1.shape[0]
    tm = ROW_TILE
    dy, dgate = _gate_dy(dx2, y, mods, 5, 1.0, n_x, t_len, "mix0_dy")
    dcat = _matmul_nt(dy, w_out, "mix0_dcat")
    d_wout = _matmul_tn(
        cat, dy, pl.BlockSpec((tm, D_MODEL), lambda n, k: (k, 0)), pl.BlockSpec((tm, D_MODEL), lambda n, k: (k, 0)),
        (D_MODEL, D_MODEL), pl.BlockSpec((D_MODEL, D_MODEL), lambda n, k: (0, 0)), (1, t_len // tm), "mix0_dwout")
    dos, d_r, d_glag = _gla_out_bwd(dcat, o_f, o_r, zc, gla_g, t_len, "mix0_dglaout")
    dqk_f, dv_f, dla_f, dqk_r, dv_r, dla_r = _gla_bwd(zc, la, spf, spr, dos, t_len, l_ctx, "mix0_dgla")
    dq, dkv, dsink = _attn_bwd(zc, sink, o_a, dcat, t_len, l_ctx, "mix0_dattn")
    dzc, dwg2, dbias2 = _mix_prep(dq, dkv, dqk_f, dqk_r, dv_f, dv_r, d_r, dla_f, dla_r, zc, wg2, bias2, cs, t_len,
                                  "mix0_prep")
    d_wcat = _matmul_tn(
        h, dzc, pl.BlockSpec((tm, D_MODEL), lambda n, k: (k, 0)), pl.BlockSpec((tm, ZC_W), lambda n, k: (k, 0)),
        (D_MODEL, ZC_W), pl.BlockSpec((D_MODEL, ZC_W), lambda n, k: (0, 0)), (1, rows // tm), "mix0_dwin")
    pairs = [(dzc, pl.BlockSpec((tm, ZC_W), lambda i: (i, 0)), wcat, pl.BlockSpec((D_MODEL, ZC_W), lambda i: (0, 0)))]
    dx1, stats = _bwd_dx(pairs, x1, dx2, t_len // tm, g3, mods, 1, n_x, "mix0_dx")
    return dx1, stats, dgate, d_wcat, dwg2, dbias2, dsink, d_glag, d_wout


_PT = 256
_PH = 16


def _pool_band(n, t_len, w, transpose):
    shape = (_PT, _PT + 2 * _PH)
    a = n * _PT + lax.broadcasted_iota(jnp.int32, shape, 0)
    b = n * _PT - _PH + lax.broadcasted_iota(jnp.int32, shape, 1)
    t, s = (b, a) if transpose else (a, b)
    lo = jnp.maximum(t - w // 2, 0)
    hi = jnp.minimum(t + (w - w // 2), t_len)
    inside = (s >= lo) & (s < hi) & (t >= 0) & (t < t_len)
    mean = jnp.where(inside, 1.0 / (hi - lo).astype(F32), 0.0)
    return mean - jnp.where(s == t, 1.0, 0.0)


def _pool_halo(p_ref, c_ref, n_ref):
    return jnp.concatenate([p_ref[_PT - _PH:_PT, :], c_ref[...], n_ref[0:_PH, :]], axis=0)


def _pool_specs(t_len):
    nb = t_len // _PT
    return [pl.BlockSpec((_PT, D_MODEL), lambda n: (jnp.maximum(n - 1, 0), 0)),
            pl.BlockSpec((_PT, D_MODEL), lambda n: (n, 0)),
            pl.BlockSpec((_PT, D_MODEL), lambda n: (jnp.minimum(n + 1, nb - 1), 0))], nb


def _pool_fwd(h, wp, pscale, x1, mods, t_len, name):
    halo_specs, nb = _pool_specs(t_len)

    def body(hp_ref, hc_ref, hn_ref, w_ref, ps_ref, x_ref, m_ref, x2_ref, pooled_ref, ypre_ref):
        n = pl.program_id(0)
        hcat = _pool_halo(hp_ref, hc_ref, hn_ref)
        ys = []
        for gi, w in enumerate(POOL_WINDOWS):
            cols = slice(gi * POOL_GROUP, (gi + 1) * POOL_GROUP)
            pooled = _dot_hi(_pool_band(n, t_len, w, False), hcat[:, cols]).astype(BF16)
            pooled_ref[:, cols] = pooled
            ys.append(_dot(pooled, w_ref[gi]))
        ypre = jnp.concatenate(ys, axis=1)
        ypre_ref[...] = ypre
        x2_ref[...] = x_ref[...] + m_ref[0, 5:6, :] * (ypre * ps_ref[...])

    return pl.pallas_call(
        body, name=name, grid=(nb,),
        in_specs=halo_specs + [pl.BlockSpec((4, POOL_GROUP, POOL_GROUP), lambda n: (0, 0, 0)),
                               pl.BlockSpec((1, D_MODEL), lambda n: (0, 0)),
                               pl.BlockSpec((_PT, D_MODEL), lambda n: (n, 0)),
                               pl.BlockSpec((1, N_MOD, D_MODEL), lambda n: (0, 0, 0))],
        out_specs=[pl.BlockSpec((_PT, D_MODEL), lambda n: (n, 0))] * 3,
        out_shape=[jax.ShapeDtypeStruct((t_len, D_MODEL), F32), jax.ShapeDtypeStruct((t_len, D_MODEL), BF16),
                   jax.ShapeDtypeStruct((t_len, D_MODEL), F32)],
        compiler_params=_cp("parallel"),
    )(h, h, h, wp, pscale, x1, mods)


def _pool_bwd_a(dx2, ypre, wp, pscale, mods, t_len, name):
    nb = t_len // _PT

    def body(d_ref, y_ref, w_ref, ps_ref, m_ref, dyp_ref, dpl_ref, dgate_ref, dps_ref):
        n = pl.program_id(0)
        dv = d_ref[...]
        ypre = y_ref[...]
        ps = ps_ref[...]
        dy = dv * m_ref[0, 5:6, :]
        dyp = (dy * ps).astype(BF16)
        dyp_ref[...] = dyp
        for gi in range(len(POOL_WINDOWS)):
            cols = slice(gi * POOL_GROUP, (gi + 1) * POOL_GROUP)
            dpl_ref[:, cols] = _dot_nt(dyp[:, cols], w_ref[gi])

        @pl.when(n == 0)
        def _():
            dgate_ref[...] = jnp.zeros_like(dgate_ref)
            dps_ref[...] = jnp.zeros_like(dps_ref)

        dgate_ref[...] += jnp.sum(dv * (ypre * ps), axis=0, keepdims=True)
        dps_ref[...] += jnp.sum(dy * ypre, axis=0, keepdims=True)

    return pl.pallas_call(
        body, name=name, grid=(nb,),
        in_specs=[pl.BlockSpec((_PT, D_MODEL), lambda n: (n, 0)),
                  pl.BlockSpec((_PT, D_MODEL), lambda n: (n, 0)),
                  pl.BlockSpec((4, POOL_GROUP, POOL_GROUP), lambda n: (0, 0, 0)),
                  pl.BlockSpec((1, D_MODEL), lambda n: (0, 0)),
                  pl.BlockSpec((1, N_MOD, D_MODEL), lambda n: (0, 0, 0))],
        out_specs=[pl.BlockSpec((_PT, D_MODEL), lambda n: (n, 0)),
                   pl.BlockSpec((_PT, D_MODEL), lambda n: (n, 0)),
                   pl.BlockSpec((1, D_MODEL), lambda n: (0, 0)),
                   pl.BlockSpec((1, D_MODEL), lambda n: (0, 0))],
        out_shape=[jax.ShapeDtypeStruct((t_len, D_MODEL), BF16), jax.ShapeDtypeStruct((t_len, D_MODEL), F32),
                   jax.ShapeDtypeStruct((1, D_MODEL), F32), jax.ShapeDtypeStruct((1, D_MODEL), F32)],
        compiler_params=_cp("arbitrary"),
    )(dx2, ypre, wp, pscale, mods)


def _pool_bwd_dx(dpl, x1, dx2, g3, mods, t_len, name):
    halo_specs, nb = _pool_specs(t_len)

    def body(dp_ref, dc_ref, dn_ref, x_ref, d_ref, g_ref, m_ref, dx_ref, acc_ref):
        n = pl.program_id(0)
        dcat = _pool_halo(dp_ref, dc_ref, dn_ref)
        dhs = []
        for gi, w in enumerate(POOL_WINDOWS):
            cols = slice(gi * POOL_GROUP, (gi + 1) * POOL_GROUP)
            dhs.append(_dot_hi(_pool_band(n, t_len, w, True), dcat[:, cols]))
        dh = jnp.concatenate(dhs, axis=1)
        g = g_ref[1:2, :]
        scale = m_ref[0, 4:5, :]
        dx = _rms_mod_bwd_tail(dh, x_ref[...], g, scale, 0, acc_ref, n == 0)
        dx_ref[...] = d_ref[...] + dx

    return pl.pallas_call(
        body, name=name, grid=(nb,),
        in_specs=halo_specs + [pl.BlockSpec((_PT, D_MODEL), lambda n: (n, 0)),
                               pl.BlockSpec((_PT, D_MODEL), lambda n: (n, 0)),
                               pl.BlockSpec((3, D_MODEL), lambda n: (0, 0)),
                               pl.BlockSpec((1, N_MOD, D_MODEL), lambda n: (0, 0, 0))],
        out_specs=[pl.BlockSpec((_PT, D_MODEL), lambda n: (n, 0)),
                   pl.BlockSpec((2, 3, D_MODEL), lambda n: (0, 0, 0))],
        out_shape=[jax.ShapeDtypeStruct((t_len, D_MODEL), F32), jax.ShapeDtypeStruct((2, 3, D_MODEL), F32)],
        compiler_params=_cp("arbitrary"),
    )(dpl, dpl, dpl, x1, dx2, g3, mods)


def _mixer_pool_forward(x1, g3, mods, wp, pscale, t_len):
    h = _rms_mod_fwd(x1, g3, mods, 1, t_len // ROW_TILE, F32, "mix1_mod")
    x2, pooled, ypre = _pool_fwd(h, wp, pscale, x1, mods, t_len, "mix1_pool")
    return x2, (x1, pooled, ypre)


def _mixer_pool_backward(dx2, saved, g3, mods, wp, pscale, t_len):
    x1, pooled, ypre = saved
    tm = ROW_TILE
    dyp, dpl, dgate, dps = _pool_bwd_a(dx2, ypre, wp, pscale, mods, t_len, "mix1_da")
    d_wp = _matmul_tn(
        pooled, dyp, pl.BlockSpec((tm, POOL_GROUP), lambda g, k: (k, g)),
        pl.BlockSpec((tm, POOL_GROUP), lambda g, k: (k, g)),
        (4, POOL_GROUP, POOL_GROUP), pl.BlockSpec((1, POOL_GROUP, POOL_GROUP), lambda g, k: (g, 0, 0)),
        (4, t_len // tm), "mix1_dwp")
    dx1, stats = _pool_bwd_dx(dpl, x1, dx2, g3, mods, t_len, "mix1_dx")
    return dx1, stats, dgate, dps, d_wp


def _final_loss(x3, final_g, target, name):
    t_len = x3.shape[0]
    tm = ROW_TILE

    def body(x_ref, g_ref, t_ref, dx_ref, loss_ref, dg_ref):
        i = pl.program_id(0)
        xv = x_ref[...]
        g = g_ref[...]
        r = lax.rsqrt(jnp.mean(xv * xv, axis=-1, keepdims=True) + RMS_EPS)
        xhat = xv * r
        err = xhat * g - t_ref[...]
        part = 0.5 * jnp.sum(jnp.mean(err * err, axis=-1, keepdims=True), axis=0, keepdims=True)
        dy = err * (1.0 / D_MODEL)

        @pl.when(i == 0)
        def _():
            loss_ref[...] = jnp.zeros_like(loss_ref)
            dg_ref[...] = jnp.zeros_like(dg_ref)

        loss_ref[...] += jnp.broadcast_to(part, (1, 128))
        dg_ref[...] += jnp.sum(dy * xhat, axis=0, keepdims=True)
        dxh = dy * g
        dx_ref[...] = r * (dxh - xhat * jnp.mean(dxh * xhat, axis=-1, keepdims=True))

    return pl.pallas_call(
        body, name=name, grid=(t_len // tm,),
        in_specs=[pl.BlockSpec((tm, D_MODEL), lambda i: (i, 0)),
                  pl.BlockSpec((1, D_MODEL), lambda i: (0, 0)),
                  pl.BlockSpec((tm, D_MODEL), lambda i: (i, 0))],
        out_specs=[pl.BlockSpec((tm, D_MODEL), lambda i: (i, 0)),
                   pl.BlockSpec((1, 128), lambda i: (0, 0)),
                   pl.BlockSpec((1, D_MODEL), lambda i: (0, 0))],
        out_shape=[jax.ShapeDtypeStruct((t_len, D_MODEL), F32), jax.ShapeDtypeStruct((1, 128), F32),
                   jax.ShapeDtypeStruct((1, D_MODEL), F32)],
        compiler_params=_cp("arbitrary"),
    )(x3, final_g, target)


_CROWS = 16


def _adaln_fwd(c16, w_mod, bias_k, name):
    n_l, _, cols = w_mod.shape

    def body(c_ref, w_ref, b_ref, o_ref):
        cv = c_ref[...]
        sc = (cv * _sigmoid(cv)).astype(BF16)
        o_ref[0] = _dot(sc, w_ref[0].astype(BF16)) + b_ref[0]

    return pl.pallas_call(
        body, name=name, grid=(n_l,),
        in_specs=[pl.BlockSpec((_CROWS, D_MODEL), lambda l: (0, 0)),
                  pl.BlockSpec((1, D_MODEL, cols), lambda l: (l, 0, 0)),
                  pl.BlockSpec((1, 1, cols), lambda l: (l, 0, 0))],
        out_specs=pl.BlockSpec((1, _CROWS, cols), lambda l: (l, 0, 0)),
        out_shape=jax.ShapeDtypeStruct((n_l, _CROWS, cols), F32),
        compiler_params=_cp("parallel"),
    )(c16, w_mod, bias_k)


def _adaln_bwd(c16, d16, w_mod, dmmc_k, name):
    n_l, _, cols = w_mod.shape

    def body(c_ref, d_ref, w_ref, dm_ref, gw_ref, cp_ref):
        layer = pl.program_id(0)
        cv = c_ref[...]
        gw_ref[0] = _dot_tn_hi(cv * _sigmoid(cv), d_ref[0])

        @pl.when(layer == 0)
        def _():
            cp_ref[...] = jnp.sum(w_ref[0] * dm_ref[...], axis=1, keepdims=True)

    return pl.pallas_call(
        body, name=name, grid=(n_l,),
        in_specs=[pl.BlockSpec((_CROWS, D_MODEL), lambda l: (0, 0)),
                  pl.BlockSpec((1, _CROWS, cols), lambda l: (l, 0, 0)),
                  pl.BlockSpec((1, D_MODEL, cols), lambda l: (0, 0, 0)),
                  pl.BlockSpec((1, cols), lambda l: (0, 0))],
        out_specs=[pl.BlockSpec((1, D_MODEL, cols), lambda l: (l, 0, 0)),
                   pl.BlockSpec((D_MODEL, 1), lambda l: (0, 0))],
        out_shape=[jax.ShapeDtypeStruct((n_l, D_MODEL, cols), F32), jax.ShapeDtypeStruct((D_MODEL, 1), F32)],
        compiler_params=_cp("arbitrary"),
    )(c16, d16, w_mod, dmmc_k)


def _cctx_grad(cparts, c_ctx2, name):
    def body(p_ref, c_ref, o_ref):
        tot = ((p_ref[0] + p_ref[2]) + p_ref[4]) + p_ref[6]
        cv = c_ref[...]
        sg = _sigmoid(cv)
        o_ref[...] = tot * (sg * (1.0 + cv * (1.0 - sg)))

    return pl.pallas_call(
        body, name=name, out_shape=jax.ShapeDtypeStruct((8, 128), F32),
        in_specs=[pl.BlockSpec(memory_space=pltpu.VMEM), pl.BlockSpec(memory_space=pltpu.VMEM)],
        out_specs=pl.BlockSpec(memory_space=pltpu.VMEM),
    )(cparts, c_ctx2)


def _sum_devices(ga, name):
    def body(g_ref, o_ref):
        acc = g_ref[0]
        for d in range(1, N_DEV):
            acc = acc + g_ref[d]
        o_ref[...] = acc

    return pl.pallas_call(
        body, name=name, out_shape=jax.ShapeDtypeStruct(ga.shape[1:], F32),
        in_specs=[pl.BlockSpec(memory_space=pltpu.VMEM)], out_specs=pl.BlockSpec(memory_space=pltpu.VMEM),
    )(ga)


def _place():
    return lax.axis_index("x"), lax.axis_index("y"), lax.axis_index("c")


def _flip(a, d):
    return 1 - a if d else a


_CHIP_FLIPS = ((1, 0), (0, 1), (1, 1))


def _allgather_small(v, name):
    r, cc = v.shape

    def body(v_ref, out_ref, send_sems, recv_sems, local_sem):
        x, y, c = _place()
        me = 4 * x + 2 * y + c
        mine = pltpu.make_async_copy(v_ref, out_ref.at[me], local_sem)
        mine.start()
        sends = []
        for k in range(1, N_DEV):
            peer = (_flip(x, (k >> 2) & 1), _flip(y, (k >> 1) & 1), _flip(c, k & 1))
            cp = pltpu.make_async_remote_copy(src_ref=v_ref, dst_ref=out_ref.at[me], send_sem=send_sems.at[k - 1],
                                              recv_sem=recv_sems.at[k - 1], device_id=peer, device_id_type=MESH)
            cp.start()
            sends.append(cp)
        for k in range(1, N_DEV):
            px, py, pc = _flip(x, (k >> 2) & 1), _flip(y, (k >> 1) & 1), _flip(c, k & 1)
            pltpu.make_async_remote_copy(src_ref=v_ref, dst_ref=out_ref.at[4 * px + 2 * py + pc],
                                         send_sem=send_sems.at[k - 1], recv_sem=recv_sems.at[k - 1],
                                         device_id=(px, py, pc), device_id_type=MESH).wait_recv()
        for cp in sends:
            cp.wait_send()
        mine.wait()

    return pl.pallas_call(
        body, name=name, out_shape=jax.ShapeDtypeStruct((N_DEV, r, cc), F32),
        in_specs=[pl.BlockSpec(memory_space=pltpu.VMEM)], out_specs=pl.BlockSpec(memory_space=pltpu.VMEM),
        scratch_shapes=[pltpu.SemaphoreType.DMA((N_DEV - 1,)), pltpu.SemaphoreType.DMA((N_DEV - 1,)),
                        pltpu.SemaphoreType.DMA],
        compiler_params=pltpu.CompilerParams(vmem_limit_bytes=VMEM_LIMIT_BYTES),
    )(v)


_HBM_SPEC = pl.BlockSpec(memory_space=pltpu.HBM)
_SEM_SPEC = pl.BlockSpec(memory_space=pltpu.SEMAPHORE)
_EFFECT = pltpu.SideEffectType.DATAFLOW_SIDE_EFFECTING


def _in_hbm(a):
    return pltpu.with_memory_space_constraint(a, pltpu.HBM)


def _gather_start(arrs, groups, after, name):
    n, n_g = len(arrs), len(groups)

    def body(*refs):
        ins, zones = refs[:n], refs[n:2 * n]
        sems = refs[2 * n + 1:2 * n + 1 + 2 * n_g]
        token = refs[2 * n + 1 + 2 * n_g + 2 * n]
        local_sems = refs[-1]
        x, y, c = _place()
        k_me = 2 * x + y
        local_copies = []
        for a in range(n):
            lc = pltpu.make_async_copy(ins[a], zones[a].at[k_me], local_sems.at[a])
            lc.start()
            local_copies.append(lc)
        for lc in local_copies:
            lc.wait()
        for g, members in enumerate(groups):
            for t, a in enumerate(members):
                for j, (dx, dy) in enumerate(_CHIP_FLIPS):
                    pltpu.make_async_remote_copy(
                        src_ref=ins[a], dst_ref=zones[a].at[k_me], send_sem=sems[2 * g].at[3 * t + j],
                        recv_sem=sems[2 * g + 1].at[3 * t + j], device_id=(_flip(x, dx), _flip(y, dy), c),
                        device_id_type=MESH).start()
        token[...] = jnp.zeros_like(token)

    zones = [lax.empty((N_CHIPS,) + a.shape, a.dtype) for a in arrs]
    sem_shapes = []
    for members in groups:
        sem_shapes += [pltpu.SemaphoreType.DMA((3 * len(members),))] * 2
    outs = pl.pallas_call(
        body, name=name,
        out_shape=sem_shapes + [pltpu.HBM(a.shape, a.dtype) for a in arrs]
        + [pltpu.HBM(z.shape, z.dtype) for z in zones] + [jax.ShapeDtypeStruct((8, 128), F32)],
        in_specs=[_HBM_SPEC] * (2 * n) + [pl.BlockSpec(memory_space=pl.ANY)],
        out_specs=[_SEM_SPEC] * (2 * n_g) + [_HBM_SPEC] * (2 * n) + [pl.BlockSpec(memory_space=pltpu.VMEM)],
        input_output_aliases={i: 2 * n_g + i for i in range(2 * n)},
        scratch_shapes=[pltpu.SemaphoreType.DMA((n,))],
        compiler_params=pltpu.CompilerParams(has_side_effects=_EFFECT),
    )(*[_in_hbm(a) for a in arrs], *[_in_hbm(z) for z in zones], after)
    sems = outs[:2 * n_g]
    thru = outs[2 * n_g:2 * n_g + n]
    zones = outs[2 * n_g + n:2 * n_g + 2 * n]
    return [(sems[2 * g], sems[2 * g + 1]) for g in range(n_g)], thru, zones, outs[-1]


def _gather_wait(shards, zones, send_sems, recv_sems, after, name):
    m = len(shards)

    def body(*refs):
        ins, zs = refs[:m], refs[m:2 * m]
        ssem, rsem = refs[2 * m], refs[2 * m + 1]
        x, y, c = _place()
        for t in range(m):
            for j, (dx, dy) in enumerate(_CHIP_FLIPS):
                px, py = _flip(x, dx), _flip(y, dy)
                cp = pltpu.make_async_remote_copy(
                    src_ref=ins[t], dst_ref=zs[t].at[2 * px + py], send_sem=ssem.at[3 * t + j],
                    recv_sem=rsem.at[3 * t + j], device_id=(px, py, c), device_id_type=MESH)
                cp.wait_send()
                cp.wait_recv()

    outs = pl.pallas_call(
        body, name=name,
        out_shape=[pltpu.HBM(a.shape, a.dtype) for a in list(shards) + list(zones)],
        in_specs=[_HBM_SPEC] * (2 * m) + [_SEM_SPEC, _SEM_SPEC, pl.BlockSpec(memory_space=pl.ANY)],
        out_specs=[_HBM_SPEC] * (2 * m),
        input_output_aliases={i: i for i in range(2 * m)},
        compiler_params=pltpu.CompilerParams(has_side_effects=_EFFECT),
    )(*shards, *zones, send_sems, recv_sems, after)
    return outs[m:]


def _scatter_start(arrs, name):
    n = len(arrs)

    def body(*refs):
        ins, lands = refs[:n], refs[n:2 * n]
        ssem, rsem = refs[2 * n], refs[2 * n + 1]
        token = refs[2 * n + 2 + 2 * n]
        x, y, c = _place()
        for a in range(n):
            for j, (dx, dy) in enumerate(_CHIP_FLIPS):
                px, py = _flip(x, dx), _flip(y, dy)
                pltpu.make_async_remote_copy(
                    src_ref=ins[a].at[2 * px + py], dst_ref=lands[a].at[j], send_sem=ssem.at[3 * a + j],
                    recv_sem=rsem.at[3 * a + j], device_id=(px, py, c), device_id_type=MESH).start()
        token[...] = jnp.zeros_like(token)

    lands = [lax.empty((3,) + a.shape[1:], a.dtype) for a in arrs]
    outs = pl.pallas_call(
        body, name=name,
        out_shape=[pltpu.SemaphoreType.DMA((3 * n,))] * 2 + [pltpu.HBM(a.shape, a.dtype) for a in arrs]
        + [pltpu.HBM(z.shape, z.dtype) for z in lands] + [jax.ShapeDtypeStruct((8, 128), F32)],
        in_specs=[_HBM_SPEC] * (2 * n),
        out_specs=[_SEM_SPEC] * 2 + [_HBM_SPEC] * (2 * n) + [pl.BlockSpec(memory_space=pltpu.VMEM)],
        input_output_aliases={i: 2 + i for i in range(2 * n)},
        compiler_params=pltpu.CompilerParams(has_side_effects=_EFFECT),
    )(*[_in_hbm(a) for a in arrs], *[_in_hbm(z) for z in lands])
    return outs[0], outs[1], outs[2:2 + n], outs[2 + n:2 + 2 * n], outs[-1]


def _scatter_wait(arrs, lands, send_sems, recv_sems, after, name):
    n = len(arrs)

    def body(*refs):
        ins, lz = refs[:n], refs[n:2 * n]
        ssem, rsem = refs[2 * n], refs[2 * n + 1]
        x, y, c = _place()
        for a in range(n):
            for j, (dx, dy) in enumerate(_CHIP_FLIPS):
                px, py = _flip(x, dx), _flip(y, dy)
                cp = pltpu.make_async_remote_copy(
                    src_ref=ins[a].at[2 * px + py], dst_ref=lz[a].at[j], send_sem=ssem.at[3 * a + j],
                    recv_sem=rsem.at[3 * a + j], device_id=(px, py, c), device_id_type=MESH)
                cp.wait_send()
                cp.wait_recv()

    outs = pl.pallas_call(
        body, name=name,
        out_shape=[pltpu.HBM(a.shape, a.dtype) for a in list(arrs) + list(lands)],
        in_specs=[_HBM_SPEC] * (2 * n) + [_SEM_SPEC, _SEM_SPEC, pl.BlockSpec(memory_space=pl.ANY)],
        out_specs=[_HBM_SPEC] * (2 * n),
        input_output_aliases={i: i for i in range(2 * n)},
        compiler_params=pltpu.CompilerParams(has_side_effects=_EFFECT),
    )(*arrs, *lands, send_sems, recv_sems, after)
    return outs[:n], outs[n:]


def _swap_sibling(arrs, name):
    n = len(arrs)

    def body(*refs):
        ins, outs = refs[:n], refs[n:2 * n]
        send_sems, recv_sems = refs[2 * n:]
        x, y, c = _place()
        sends = []
        for a in range(n):
            cp = pltpu.make_async_remote_copy(src_ref=ins[a], dst_ref=outs[a], send_sem=send_sems.at[a],
                                              recv_sem=recv_sems.at[a], device_id=(x, y, 1 - c), device_id_type=MESH)
            cp.start()
            sends.append(cp)
        for cp in sends:
            cp.wait()

    any_spec = pl.BlockSpec(memory_space=pl.ANY)
    return pl.pallas_call(
        body, name=name,
        out_shape=[jax.ShapeDtypeStruct(a.shape, a.dtype) for a in arrs],
        in_specs=[any_spec] * n, out_specs=[any_spec] * n,
        scratch_shapes=[pltpu.SemaphoreType.DMA((n,)), pltpu.SemaphoreType.DMA((n,))],
    )(*arrs)


def _row_tile(rows, cols):
    for tr in (1024, 512, 256, 128, 64, 32, 16, 8):
        if rows % tr == 0 and tr * cols * 4 <= (1 << 20):
            return tr
    return rows


def _partial_sum(g_full, recv, k_idx, name):
    _, r, c = g_full.shape
    tr = _row_tile(r, c)

    def body(k_ref, g_ref, r_ref, o_ref):
        del k_ref
        acc = g_ref[0].astype(F32)
        for j in range(3):
            acc = acc + r_ref[j].astype(F32)
        o_ref[...] = acc

    return pl.pallas_call(
        body, name=name,
        grid_spec=pltpu.PrefetchScalarGridSpec(
            num_scalar_prefetch=1, grid=(r // tr,),
            in_specs=[pl.BlockSpec((1, tr, c), lambda i, k: (k[0], i, 0)),
                      pl.BlockSpec((3, tr, c), lambda i, k: (0, i, 0))],
            out_specs=pl.BlockSpec((tr, c), lambda i, k: (i, 0))),
        out_shape=jax.ShapeDtypeStruct((r, c), F32),
        compiler_params=_cp("parallel"),
    )(k_idx, g_full, recv)


def _adamw(w3, parts, m3, v3, layer, prev, name):
    n_l, r, c = w3.shape
    tr = _row_tile(r, c)
    n_i = r // tr
    n_p = len(parts)
    c1 = 1.0 - ADAM_B1 ** ADAM_STEP
    c2 = 1.0 - ADAM_B2 ** ADAM_STEP
    stacked = [isinstance(p, tuple) for p in parts]

    def body(*refs):
        w_ref, m_ref, v_ref = refs[0:3]
        g_refs = refs[3:3 + n_p]
        go_ref, d_ref, mo_ref, vo_ref = refs[-4:]
        g = None
        for p in range(n_p):
            term = g_refs[p][0] if stacked[p] else g_refs[p][...]
            g = term if g is None else g + term
        w = w_ref[0]
        m = ADAM_B1 * m_ref[0] + (1.0 - ADAM_B1) * g
        v = ADAM_B2 * v_ref[0] + (1.0 - ADAM_B2) * (g * g)
        m_hat = m / c1
        v_hat = v / c2
        go_ref[0] = g
        d_ref[0] = -ADAM_LR * (m_hat / (jnp.sqrt(v_hat) + ADAM_EPS) + ADAM_WD * w)
        mo_ref[0] = m
        vo_ref[0] = v

    blk = pl.BlockSpec((1, tr, c), lambda i: (layer, i, 0))
    in_specs = [blk, blk, blk]
    args = [w3, m3, v3]
    for part in parts:
        if isinstance(part, tuple):
            in_specs.append(pl.BlockSpec((1, tr, c), functools.partial(lambda idx, i: (idx, i, 0), part[1])))
            args.append(part[0])
        else:
            in_specs.append(pl.BlockSpec((tr, c), lambda i: (i, 0)))
            args.append(part)
    aliases = {}
    if prev is not None:
        in_specs += [pl.BlockSpec(memory_space=pl.ANY)] * 4
        aliases = {len(args) + q: q for q in range(4)}
        args += list(prev)
    shp = jax.ShapeDtypeStruct((n_l, r, c), F32)
    return pl.pallas_call(
        body, name=name, grid=(n_i,), in_specs=in_specs, out_specs=[blk] * 4, out_shape=[shp] * 4,
        input_output_aliases=aliases, compiler_params=_cp("parallel"),
    )(*args)


_SMALL_W = 4096
_PACK_ROWS = 352
_N9 = N_MOD * D_MODEL


def _flat_pad(parts, total):
    flat = jnp.concatenate([p.reshape(-1) for p in parts])
    return jnp.concatenate([flat, jnp.zeros((total - flat.shape[0],), F32)])


def kernel(x, c, ctx, c_ctx, w_mod, b_mod, norm_g, ffn1_wi, ffn1_wo, ffn2_wi, ffn2_wo, w_in, w_a2_f, b_a_f, w_a2_b, b_a_b, sink, gla_g, w_out, w_pool, pool_scale, final_g, loss_target, m_c_ctx, m_w_mod, m_b_mod, m_norm_g, m_ffn1_wi, m_ffn1_wo, m_ffn2_wi, m_ffn2_wo, m_w_in, m_w_a2_f, m_b_a_f, m_w_a2_b, m_b_a_b, m_sink, m_gla_g, m_w_out, m_w_pool, m_pool_scale, m_final_g, v_c_ctx, v_w_mod, v_b_mod, v_norm_g, v_ffn1_wi, v_ffn1_wo, v_ffn2_wi, v_ffn2_wo, v_w_in, v_w_a2_f, v_b_a_f, v_w_a2_b, v_b_a_b, v_sink, v_gla_g, v_w_out, v_w_pool, v_pool_scale, v_final_g):
    t_len, l_ctx = x.shape[1], ctx.shape[1]
    tm = ROW_TILE
    pad = (-(t_len + l_ctx)) % tm
    rows0 = t_len + l_ctx + pad
    n_x = t_len // tm
    xi, yi, ci = _place()
    k_me = 2 * xi + yi
    me = 4 * xi + 2 * yi + ci
    mod_cols = w_mod.shape[2]
    n_grp = len(POOL_WINDOWS)

    small_w = _flat_pad([norm_g, w_a2_f, w_a2_b, pool_scale], _SMALL_W).reshape(_SMALL_W // 128, 128)
    shards = [ffn1_wi[0], ffn1_wi[1], ffn1_wo[0], ffn1_wo[1], ffn2_wi[0], ffn2_wi[1], ffn2_wo[0], ffn2_wo[1],
              w_in[0], w_out[0], w_pool[0].reshape(n_grp * w_pool.shape[2], POOL_GROUP)]

    c_all = _allgather_small(c.reshape(8, 128), "gather_cond").reshape(N_DEV, D_MODEL)
    c16 = jnp.concatenate([c_all, c_ctx[None], jnp.zeros((_CROWS - N_DEV - 1, D_MODEL), F32)], axis=0)
    bias_k = lax.dynamic_slice(b_mod, (0, k_me * mod_cols), (2, mod_cols)).reshape(2, 1, mod_cols)
    mm_k = _adaln_fwd(c16, w_mod, bias_k, "adaln_fwd")
    mm_all = _allgather_small(mm_k.reshape(-1, 128), "gather_mod")

    send_src = [s.astype(BF16) for s in shards] + [small_w]
    groups = ([11, 0], [2], [8, 9], [4], [6], [1], [3], [10, 5], [7])
    started = {}

    def gather_start(g, after):
        members = groups[g]
        sems, thru, zones, token = _gather_start([send_src[a] for a in members], (tuple(range(len(members))),),
                                                 after, "gather_start_%d" % g)
        started[g] = (sems[0], thru, zones)
        return token

    def gather_wait(g, after):
        (ssem, rsem), thru, zones = started[g]
        return dict(zip(groups[g], _gather_wait(thru, zones, ssem, rsem, after, "gather_wait_%d" % g)))

    tok = gather_start(0, mm_all)
    mm_all = mm_all.reshape(N_DEV, 2, _CROWS, mod_cols)
    mm_full = jnp.concatenate([mm_all[2 * k] for k in range(N_CHIPS)], axis=-1)
    mm_x = lax.dynamic_index_in_dim(mm_full, me, axis=1, keepdims=False)
    mm_c = mm_full[:, N_DEV]
    mods = [jnp.stack([mm_x[l].reshape(N_MOD, D_MODEL), mm_c[l].reshape(N_MOD, D_MODEL)]) + tok[0:1, 0:1]
            for l in range(2)]
    gathered = gather_wait(0, mods[0])
    sw = gathered[11].reshape(N_CHIPS, _SMALL_W)
    ng_n = norm_g.size
    a2_n = w_a2_f.size
    norm_g_full = jnp.concatenate([sw[k, :ng_n].reshape(norm_g.shape) for k in range(N_CHIPS)], axis=-1)
    w_a2_f_full = jnp.concatenate([sw[k, ng_n:ng_n + a2_n].reshape(w_a2_f.shape[1:]) for k in range(N_CHIPS)], axis=-1)
    w_a2_b_full = jnp.concatenate(
        [sw[k, ng_n + a2_n:ng_n + 2 * a2_n].reshape(w_a2_b.shape[1:]) for k in range(N_CHIPS)], axis=-1)
    pscale_full = jnp.concatenate(
        [sw[k, ng_n + 2 * a2_n:ng_n + 2 * a2_n + pool_scale.size] for k in range(N_CHIPS)]).reshape(1, D_MODEL)
    wg2, bias2 = _gate_weights(w_a2_f_full, b_a_f[0], w_a2_b_full, b_a_b[0])
    gla_g2 = gla_g.reshape(1, B_DV)
    final_g2 = final_g.reshape(1, D_MODEL)
    cs = _rope_tables(t_len, rows0)

    g3 = [norm_g_full[0], norm_g_full[1]]

    xcat = jnp.concatenate([x[0], ctx[0], jnp.zeros((pad, D_MODEL), F32)], axis=0)
    w1i, w1o, w2i, w2o = [None, None], [None, None], [None, None], [None, None]
    w1i[0] = gathered[0]
    mods_a = mods[0] + gather_start(1, w1i[0])[0:1, 0:1] + gather_start(2, w1i[0])[0:1, 0:1]
    x1, sv_a1, w1o[0] = _ffn_forward(xcat, g3[0], mods_a, 0, w1i[0],
                                     lambda s: (gather_wait(1, s)[2], gather_start(3, s)), n_x, "l0_ffn1")
    gathered = gather_wait(2, x1)
    w_in_full = jnp.concatenate([gathered[8][k] for k in range(N_CHIPS)], axis=1)
    wcat = _w_in_to_cat(w_in_full)
    w_out_full = gathered[9].reshape(D_MODEL, D_MODEL)
    mods_a = mods[0] + gather_start(4, x1)[0:1, 0:1]
    pace_group = {"proj": 5, "attn": 6, "gla": 7}
    x2, sv_am = _mixer_ab_forward(x1, g3[0], mods_a, wcat, wg2, bias2, sink[0], gla_g2, w_out_full, cs,
                                  t_len, l_ctx, n_x, lambda tag, res_: gather_start(pace_group[tag], res_))
    mods_a = mods[0] + gather_start(8, x2)[0:1, 0:1]
    w2i[0], w2o[0] = gather_wait(3, x2)[4], gather_wait(4, x2)[6]
    x3, sv_a2, _ = _ffn_forward(x2, g3[0], mods_a, 2, w2i[0], lambda s: (w2o[0], None), n_x, "l0_ffn2")
    w1i[1], w1o[1] = gather_wait(5, x3)[1], gather_wait(6, x3)[3]
    x4, sv_b1, _ = _ffn_forward(x3, g3[1], mods[1], 0, w1i[1], lambda s: (w1o[1], None), n_x, "l1_ffn1")
    gathered = gather_wait(7, x4)
    w2i[1] = gathered[5]
    wp_full = gathered[10].reshape(N_CHIPS, n_grp, -1, POOL_GROUP).transpose(1, 0, 2, 3).reshape(
        n_grp, POOL_GROUP, POOL_GROUP)
    x5, sv_bm = _mixer_pool_forward(x4, g3[1], mods[1], wp_full, pscale_full, t_len)
    x6, sv_b2, w2o[1] = _ffn_forward(x5, g3[1], mods[1], 2, w2i[1], lambda s: (gather_wait(8, s)[7], None), n_x,
                                     "l1_ffn2")
    dx6, loss_part, d_final_g = _final_loss(x6, final_g2, loss_target[0], "final_loss")
    loss = lax.psum(loss_part[0, 0], ("x", "y", "c"))

    sent = []

    def sender(weight, layer):
        def send(grad, tag):
            nm = "%s_%s_%d" % (weight, tag, layer)
            ssem, rsem, thru, lands, token = _scatter_start([grad], "scatter_start_" + nm)
            sent.append((nm, weight + "_" + tag if tag else weight, layer, thru, lands, ssem, rsem))
            return token[0:1, 0:1]
        return send

    dx5, st_b2, dg_b2 = _ffn_backward(dx6, sv_b2, g3[1], mods[1], 2, w2i[1], w2o[1], n_x, sender("ffn2", 1),
                                      "l1_ffn2_b")
    dx4, st_bm, dg_bm, d_pscale, d_wp = _mixer_pool_backward(dx5, sv_bm, g3[1], mods[1], wp_full, pscale_full, t_len)
    d_wp4 = d_wp.reshape(n_grp, N_CHIPS, -1, POOL_GROUP).transpose(1, 0, 2, 3).reshape(N_CHIPS, -1, POOL_GROUP)
    mods1 = mods[1] + sender("w_pool", 0)(d_wp4, "")
    dx3, st_b1, dg_b1 = _ffn_backward(dx4, sv_b1, g3[1], mods1, 0, w1i[1], w1o[1], n_x, sender("ffn1", 1),
                                      "l1_ffn1_b")
    dx2, st_a2, dg_a2 = _ffn_backward(dx3, sv_a2, g3[0], mods[0], 2, w2i[0], w2o[0], n_x, sender("ffn2", 0),
                                      "l0_ffn2_b")
    dx1, st_am, dg_am, d_wcat, d_wg2, d_bias2, d_sink, d_glag, d_wout = _mixer_ab_backward(
        dx2, sv_am, g3[0], mods[0], wcat, wg2, bias2, sink[0], gla_g2, w_out_full, cs, t_len, l_ctx, n_x)
    d_w_in4 = _cat_to_w_in(d_wcat).reshape(D_MODEL, N_CHIPS, -1).transpose(1, 0, 2)
    mods0 = mods[0] + sender("w_in", 0)(d_w_in4, "") + sender("w_out", 0)(d_wout.reshape(N_CHIPS, -1, D_MODEL), "")
    dx0, st_a1, dg_a1 = _ffn_backward(dx1, sv_a1, g3[0], mods0, 0, w1i[0], w1o[0], n_x, sender("ffn1", 0),
                                      "l0_ffn1_b")
    grad_x = dx0[:t_len][None]

    def mod_row(st1, dg1, stm, dgm, st2, dg2, s):
        return jnp.concatenate([st1[s, 0], st1[s, 1], dg1[s, 0], stm[s, 0], stm[s, 1], dgm[s, 0],
                                st2[s, 0], st2[s, 1], dg2[s, 0]])

    dg_bm2 = jnp.concatenate([dg_bm, jnp.zeros_like(dg_bm)], axis=0)[:, None, :]
    d_mm_x0 = mod_row(st_a1, dg_a1, st_am, dg_am, st_a2, dg_a2, 0)
    d_mm_x1 = mod_row(st_b1, dg_b1, st_bm, dg_bm2, st_b2, dg_b2, 0)
    d_mm_c0 = mod_row(st_a1, dg_a1, st_am, dg_am, st_a2, dg_a2, 1)
    d_norm_g = jnp.stack([jnp.stack([st[0, 2] + st[1, 2] for st in (st_a1, st_am, st_a2)]),
                          jnp.stack([st[0, 2] + st[1, 2] for st in (st_b1, st_bm, st_b2)])])
    rk = B_GATE_RANK
    pack = _flat_pad([d_mm_x0, d_mm_x1, d_mm_c0, d_norm_g, d_bias2, d_wg2[0:rk, 0:256], d_wg2[rk:2 * rk, 256:512],
                      d_sink[:, 0], jnp.zeros((120,), F32), d_glag, d_pscale, d_final_g],
                     _PACK_ROWS * 128).reshape(_PACK_ROWS, 128)
    pack_all = _allgather_small(pack, "gather_small_grads")
    tot = _sum_devices(pack_all, "sum_small_grads").reshape(-1)
    rows_all = pack_all.reshape(N_DEV, -1)
    o = 3 * _N9
    g_norm_g_full = tot[o:o + 6 * D_MODEL].reshape(2, 3, D_MODEL)
    o += 6 * D_MODEL
    g_bias2 = tot[o:o + 512]
    o += 512
    g_w_a2_f_full = tot[o:o + rk * 256].reshape(rk, 256)
    o += rk * 256
    g_w_a2_b_full = tot[o:o + rk * 256].reshape(rk, 256)
    o += rk * 256
    g_sink = tot[o:o + A_HEADS]
    o += 128
    g_gla_g = tot[o:o + B_DV]
    o += B_DV
    g_pscale_full = tot[o:o + D_MODEL]
    o += D_MODEL
    g_final_g = tot[o:o + D_MODEL]
    d_mmc_tot = tot[2 * _N9:3 * _N9]
    g_b_mod = jnp.stack([tot[0:_N9] + d_mmc_tot, tot[_N9:2 * _N9]])

    zrows = jnp.zeros((_CROWS - N_DEV - 1, _N9), F32)
    d16 = jnp.stack([jnp.concatenate([rows_all[:, 0:_N9], d_mmc_tot[None], zrows], axis=0),
                     jnp.concatenate([rows_all[:, _N9:2 * _N9], jnp.zeros((1, _N9), F32), zrows], axis=0)])
    d16_k = lax.dynamic_slice(d16, (0, 0, k_me * mod_cols), (2, _CROWS, mod_cols))
    dmmc_k = lax.dynamic_slice(d_mmc_tot, (k_me * mod_cols,), (mod_cols,)).reshape(1, mod_cols)
    g_w_mod, c_part = _adaln_bwd(c16, d16_k, w_mod, dmmc_k, "adaln_bwd")
    c_parts = _allgather_small(c_part.reshape(8, 128), "gather_cctx")
    g_c_ctx = _cctx_grad(c_parts, c_ctx.reshape(8, 128), "cctx_grad").reshape(D_MODEL)

    def small(w, g, m, v, shape3, nm):
        return [o_.reshape(w.shape) for o_ in _adamw(w.reshape(shape3), [g.reshape(shape3[1:])],
                                                    m.reshape(shape3), v.reshape(shape3), 0, None, "adamw_" + nm)]

    def own(a, axis, size):
        return lax.dynamic_slice_in_dim(a, k_me * size, size, axis=axis)

    res = {}
    res["c_ctx"] = small(c_ctx, g_c_ctx, m_c_ctx, v_c_ctx, (1, 8, 128), "c_ctx")
    upd = _adamw(w_mod, [(g_w_mod, 1)], m_w_mod, v_w_mod, 1, None, "adamw_w_mod_1")
    res["w_mod"] = _adamw(w_mod, [(g_w_mod, 0)], m_w_mod, v_w_mod, 0, upd, "adamw_w_mod_0")
    res["b_mod"] = small(b_mod, g_b_mod, m_b_mod, v_b_mod, (1, 2, _N9), "b_mod")
    res["norm_g"] = small(norm_g, own(g_norm_g_full, 2, norm_g.shape[2]), m_norm_g, v_norm_g,
                          (1, 6, norm_g.shape[2]), "norm_g")
    res["w_a2_f"] = small(w_a2_f, own(g_w_a2_f_full, 1, w_a2_f.shape[2]), m_w_a2_f, v_w_a2_f,
                          (1, rk, w_a2_f.shape[2]), "w_a2_f")
    res["b_a_f"] = small(b_a_f, g_bias2[0:256], m_b_a_f, v_b_a_f, (1, 1, 256), "b_a_f")
    res["w_a2_b"] = small(w_a2_b, own(g_w_a2_b_full, 1, w_a2_b.shape[2]), m_w_a2_b, v_w_a2_b,
                          (1, rk, w_a2_b.shape[2]), "w_a2_b")
    res["b_a_b"] = small(b_a_b, g_bias2[256:512], m_b_a_b, v_b_a_b, (1, 1, 256), "b_a_b")
    res["sink"] = small(sink, g_sink, m_sink, v_sink, (1, 1, A_HEADS), "sink")
    res["gla_g"] = small(gla_g, g_gla_g, m_gla_g, v_gla_g, (1, 1, B_DV), "gla_g")
    res["pool_scale"] = small(pool_scale, own(g_pscale_full, 0, pool_scale.shape[1]), m_pool_scale, v_pool_scale,
                              (1, 1, pool_scale.shape[1]), "pool_scale")
    res["final_g"] = small(final_g, g_final_g, m_final_g, v_final_g, (1, 8, 128), "final_g")

    def as3(a):
        n_l = a.shape[0] if a.ndim == 3 else 1
        return a.reshape(n_l, -1, a.shape[-1])

    big_w = {"ffn1_wi": (ffn1_wi, m_ffn1_wi, v_ffn1_wi), "ffn1_wo": (ffn1_wo, m_ffn1_wo, v_ffn1_wo),
             "ffn2_wi": (ffn2_wi, m_ffn2_wi, v_ffn2_wi), "ffn2_wo": (ffn2_wo, m_ffn2_wo, v_ffn2_wo),
             "w_in": (w_in, m_w_in, v_w_in), "w_out": (w_out, m_w_out, v_w_out), "w_pool": (w_pool, m_w_pool, v_w_pool)}
    k_idx = k_me.reshape(1).astype(jnp.int32)
    chain = res["final_g"][0]
    for lo, hi in ((0, 2), (2, 5), (5, 7), (7, 9), (9, 11)):
        partial = []
        for nm, wname, layer, thru, lands, ssem, rsem in sent[lo:hi]:
            mine, recv = _scatter_wait(thru, lands, ssem, rsem, chain, "scatter_wait_" + nm)
            partial.append(_partial_sum(mine[0], recv[0], k_idx, "partial_sum_" + nm))
        other = _swap_sibling(partial, "swap_partials_%d" % lo)
        for (nm, wname, layer, _, _, _, _), p, q in zip(sent[lo:hi], partial, other):
            w, m, v = big_w[wname]
            res[wname] = _adamw(as3(w), [p, q], as3(m), as3(v), layer, res.get(wname),
                                "adamw_%s_%d" % (wname, layer))
            chain = res[wname][3]
    for wname, (w, _, _) in big_w.items():
        res[wname] = [o_.reshape(w.shape) for o_ in res[wname]]

    names = ["c_ctx", "w_mod", "b_mod", "norm_g", "ffn1_wi", "ffn1_wo", "ffn2_wi", "ffn2_wo", "w_in", "w_a2_f",
             "b_a_f", "w_a2_b", "b_a_b", "sink", "gla_g", "w_out", "w_pool", "pool_scale", "final_g"]
    outs = [loss, grad_x]
    for field in range(4):
        outs += [res[nm][field] for nm in names]
    return tuple(outs)
```

```python
import functools

import jax
import jax.numpy as jnp
import numpy as np
from jax import lax
from jax.experimental import pallas as pl
from jax.experimental.pallas import tpu as pltpu

F32 = jnp.float32
BF16 = jnp.bfloat16

D_MODEL = 1024
N_MOD = 9
D_FF = 2816
RMS_EPS = 1e-6
A_HEADS = 8
A_KV_HEADS = 2
A_HEAD_DIM = 64
WINDOW = 128
ROPE_BASE = 10000.0
GRID_W = 64
B_HEADS = 4
B_DK = 64
B_DV = 128
B_GATE_RANK = 16
B_GATE_NORM = 16.0
B_CHUNK = 64
POOL_WINDOWS = (2, 4, 8, 16)
POOL_GROUP = D_MODEL // len(POOL_WINDOWS)
PROJ_DIM = 2336

ADAM_LR = 0.001
ADAM_B1 = 0.9
ADAM_B2 = 0.999
ADAM_EPS = 1e-08
ADAM_WD = 0.01
ADAM_STEP = 10

N_CHIPS = 4
N_DEV = 8
ROW_TILE = 512
VMEM_LIMIT_BYTES = 56 * 1024 * 1024
MESH = pl.DeviceIdType.MESH

ZC_Q, ZC_QK, ZC_V, ZC_R, ZC_KV, ZC_G, ZC_W = 0, 512, 1024, 1536, 2048, 2304, 2432


def _cp(*sem):
    return pltpu.CompilerParams(dimension_semantics=sem if sem else None, vmem_limit_bytes=VMEM_LIMIT_BYTES)


def _dot(a, b):
    return jnp.dot(a, b, preferred_element_type=F32)


def _dot_nt(a, b):
    return lax.dot_general(a, b, (((1,), (1,)), ((), ())), preferred_element_type=F32)


def _dot_tn(a, b):
    return lax.dot_general(a, b, (((0,), (0,)), ((), ())), preferred_element_type=F32)


def _dot_hi(a, b):
    return jnp.dot(a, b, preferred_element_type=F32, precision=lax.Precision.HIGHEST)


def _dot_tn_hi(a, b):
    return lax.dot_general(a, b, (((0,), (0,)), ((), ())), preferred_element_type=F32,
                           precision=lax.Precision.HIGHEST)


def _sigmoid(x):
    return 1.0 / (1.0 + jnp.exp(-x))


def _stream_of(i, n_x):
    return jnp.where(i >= n_x, 1, 0)


def _rms_mod_fwd(x, g3, mods, j, n_x, out_dtype, name):
    rows = x.shape[0]
    tm = ROW_TILE
    n_i = rows // tm

    def body(x_ref, g_ref, m_ref, o_ref):
        xv = x_ref[...]
        r = lax.rsqrt(jnp.mean(xv * xv, axis=-1, keepdims=True) + RMS_EPS)
        g = g_ref[j:j + 1, :]
        shift = m_ref[0, 3 * j:3 * j + 1, :]
        scale = m_ref[0, 3 * j + 1:3 * j + 2, :]
        o_ref[...] = (((xv * r) * g) * (1.0 + scale) + shift).astype(out_dtype)

    return pl.pallas_call(
        body, name=name, grid=(n_i,),
        in_specs=[pl.BlockSpec((tm, D_MODEL), lambda i: (i, 0)),
                  pl.BlockSpec((3, D_MODEL), lambda i: (0, 0)),
                  pl.BlockSpec((1, N_MOD, D_MODEL), lambda i: (_stream_of(i, n_x), 0, 0))],
        out_specs=pl.BlockSpec((tm, D_MODEL), lambda i: (i, 0)),
        out_shape=jax.ShapeDtypeStruct((rows, D_MODEL), out_dtype),
        compiler_params=_cp("parallel"),
    )(x, g3, mods)


def _rms_mod_bwd_tail(dh, xv, g, scale, stream, acc_ref, first):
    r = lax.rsqrt(jnp.mean(xv * xv, axis=-1, keepdims=True) + RMS_EPS)
    xhat = xv * r
    t1 = jnp.sum(dh, axis=0, keepdims=True)
    t2 = jnp.sum(dh * xhat, axis=0, keepdims=True)
    stats = jnp.concatenate([t1, t2 * g, t2 * (1.0 + scale)], axis=0)

    @pl.when(first)
    def _():
        acc_ref[...] = jnp.zeros_like(acc_ref)

    acc_ref[pl.ds(stream, 1)] += stats[None]
    dxh = dh * (g * (1.0 + scale))
    return r * (dxh - xhat * jnp.mean(dxh * xhat, axis=-1, keepdims=True))


def _ffn_up(hn, w4, name):
    rows = hn.shape[0]
    h = w4.shape[2]
    tm = ROW_TILE
    n_i = rows // tm

    def body(h_ref, wa_ref, wu_ref, au_ref, s_ref):
        hv = h_ref[...]
        a = _dot(hv, wa_ref[0])
        u = _dot(hv, wu_ref[0])
        au_ref[0] = a.astype(BF16)
        au_ref[1] = u.astype(BF16)
        s_ref[...] = (a * _sigmoid(a) * u).astype(BF16)

    return pl.pallas_call(
        body, name=name, grid=(2, n_i),
        in_specs=[pl.BlockSpec((tm, D_MODEL), lambda j, i: (i, 0)),
                  pl.BlockSpec((1, D_MODEL, h), lambda j, i: (j, 0, 0)),
                  pl.BlockSpec((1, D_MODEL, h), lambda j, i: (j + 2, 0, 0))],
        out_specs=[pl.BlockSpec((2, tm, h), lambda j, i: (0, i, j)),
                   pl.BlockSpec((tm, h), lambda j, i: (i, j))],
        out_shape=[jax.ShapeDtypeStruct((2, rows, 2 * h), BF16),
                   jax.ShapeDtypeStruct((rows, 2 * h), BF16)],
        compiler_params=_cp("arbitrary", "arbitrary"),
    )(hn, w4, w4)


def _matmul_resid(a, w, xres, mods, gate_idx, coef, n_x, rows, name):
    k = a.shape[1]
    tm = ROW_TILE
    n_i = rows // tm

    def body(a_ref, w_ref, x_ref, m_ref, o_ref, f_ref):
        f = _dot(a_ref[...], w_ref[...])
        gate = m_ref[0, gate_idx:gate_idx + 1, :]
        f_ref[...] = f
        o_ref[...] = x_ref[...] + (coef * gate) * f

    return pl.pallas_call(
        body, name=name, grid=(n_i,),
        in_specs=[pl.BlockSpec((tm, k), lambda i: (i, 0)),
                  pl.BlockSpec((k, D_MODEL), lambda i: (0, 0)),
                  pl.BlockSpec((tm, D_MODEL), lambda i: (i, 0)),
                  pl.BlockSpec((1, N_MOD, D_MODEL), lambda i: (_stream_of(i, n_x), 0, 0))],
        out_specs=[pl.BlockSpec((tm, D_MODEL), lambda i: (i, 0)),
                   pl.BlockSpec((tm, D_MODEL), lambda i: (i, 0))],
        out_shape=[jax.ShapeDtypeStruct((rows, D_MODEL), F32),
                   jax.ShapeDtypeStruct((rows, D_MODEL), F32)],
        compiler_params=_cp("parallel"),
    )(a, w, xres, mods)


def _gate_dy(dout, f, mods, gate_idx, coef, n_x, rows, name):
    tm = ROW_TILE
    n_i = rows // tm

    def body(d_ref, f_ref, m_ref, dy_ref, acc_ref):
        i = pl.program_id(0)
        dv = d_ref[...]
        gate = m_ref[0, gate_idx:gate_idx + 1, :]
        dy_ref[...] = (dv * (coef * gate)).astype(BF16)

        @pl.when(i == 0)
        def _():
            acc_ref[...] = jnp.zeros_like(acc_ref)

        part = coef * jnp.sum(dv * f_ref[...], axis=0, keepdims=True)
        acc_ref[pl.ds(_stream_of(i, n_x), 1)] += part[None]

    return pl.pallas_call(
        body, name=name, grid=(n_i,),
        in_specs=[pl.BlockSpec((tm, D_MODEL), lambda i: (i, 0)),
                  pl.BlockSpec((tm, D_MODEL), lambda i: (i, 0)),
                  pl.BlockSpec((1, N_MOD, D_MODEL), lambda i: (_stream_of(i, n_x), 0, 0))],
        out_specs=[pl.BlockSpec((tm, D_MODEL), lambda i: (i, 0)),
                   pl.BlockSpec((2, 1, D_MODEL), lambda i: (0, 0, 0))],
        out_shape=[jax.ShapeDtypeStruct((rows, D_MODEL), BF16),
                   jax.ShapeDtypeStruct((2, 1, D_MODEL), F32)],
        compiler_params=_cp("arbitrary"),
    )(dout, f, mods)


def _ffn_bwd_dz(dy, wo2, au, name):
    rows = dy.shape[0]
    h = wo2.shape[1]
    tm = ROW_TILE
    n_i = rows // tm

    def body(dy_ref, wo_ref, au_ref, dz_ref):
        ds = _dot_nt(dy_ref[...], wo_ref[0])
        a = au_ref[0].astype(F32)
        u = au_ref[1].astype(F32)
        sg = _sigmoid(a)
        dz_ref[0] = (ds * u * (sg * (1.0 + a * (1.0 - sg)))).astype(BF16)
        dz_ref[1] = (ds * (a * sg)).astype(BF16)

    return pl.pallas_call(
        body, name=name, grid=(2, n_i),
        in_specs=[pl.BlockSpec((tm, D_MODEL), lambda j, i: (i, 0)),
                  pl.BlockSpec((1, h, D_MODEL), lambda j, i: (j, 0, 0)),
                  pl.BlockSpec((2, tm, h), lambda j, i: (0, i, j))],
        out_specs=pl.BlockSpec((2, tm, h), lambda j, i: (0, i, j)),
        out_shape=jax.ShapeDtypeStruct((2, rows, 2 * h), BF16),
        compiler_params=_cp("arbitrary", "arbitrary"),
    )(dy, wo2, au)


def _matmul_tn(a, b, a_spec, b_spec, out_shape, out_spec, grid, name):
    nd_a = len(a_spec.block_shape)
    nd_b = len(b_spec.block_shape)
    nd_o = len(out_spec.block_shape)
    k_axis = len(grid) - 1
    n_k = grid[k_axis]

    def body(a_ref, b_ref, o_ref, acc_ref):
        av = a_ref[(0,) * (nd_a - 2)]
        bv = b_ref[(0,) * (nd_b - 2)]
        part = _dot_tn(av, bv)
        k = pl.program_id(k_axis)

        @pl.when(k == 0)
        def _():
            acc_ref[...] = part

        @pl.when(k > 0)
        def _():
            acc_ref[...] += part

        @pl.when(k == n_k - 1)
        def _():
            o_ref[(0,) * (nd_o - 2)] = acc_ref[...].astype(BF16)

    return pl.pallas_call(
        body, name=name, grid=grid, in_specs=[a_spec, b_spec], out_specs=out_spec,
        out_shape=jax.ShapeDtypeStruct(out_shape, BF16),
        scratch_shapes=[pltpu.VMEM(tuple(out_spec.block_shape[-2:]), F32)],
        compiler_params=_cp(*(("arbitrary",) * len(grid))),
    )(a, b)


def _bwd_dx(pairs, x, dres, dres_tiles, g3, mods, j, n_x, name):
    rows = x.shape[0]
    tm = ROW_TILE
    n_i = rows // tm
    n_p = len(pairs)
    nds = [(len(p[1].block_shape), len(p[3].block_shape)) for p in pairs]

    def body(*refs):
        dz_refs = refs[0:2 * n_p:2]
        w_refs = refs[1:2 * n_p:2]
        x_ref, dres_ref, g_ref, m_ref, dx_ref, acc_ref = refs[2 * n_p:]
        i = pl.program_id(0)
        dh = None
        for p in range(n_p):
            dzv = dz_refs[p][(0,) * (nds[p][0] - 2)]
            wv = w_refs[p][(0,) * (nds[p][1] - 2)]
            part = _dot_nt(dzv, wv)
            dh = part if dh is None else dh + part
        g = g_ref[j:j + 1, :]
        scale = m_ref[0, 3 * j + 1:3 * j + 2, :]
        dx = _rms_mod_bwd_tail(dh, x_ref[...], g, scale, _stream_of(i, n_x), acc_ref, i == 0)
        dres_v = jnp.where(i < dres_tiles, dres_ref[...], 0.0)
        dx_ref[...] = dres_v + dx

    in_specs, args = [], []
    for dz, dz_spec, w, w_spec in pairs:
        in_specs += [dz_spec, w_spec]
        args += [dz, w]
    in_specs += [pl.BlockSpec((tm, D_MODEL), lambda i: (i, 0)),
                 pl.BlockSpec((tm, D_MODEL), lambda i: (jnp.minimum(i, dres_tiles - 1), 0)),
                 pl.BlockSpec((3, D_MODEL), lambda i: (0, 0)),
                 pl.BlockSpec((1, N_MOD, D_MODEL), lambda i: (_stream_of(i, n_x), 0, 0))]
    args += [x, dres, g3, mods]
    return pl.pallas_call(
        body, name=name, grid=(n_i,), in_specs=in_specs,
        out_specs=[pl.BlockSpec((tm, D_MODEL), lambda i: (i, 0)),
                   pl.BlockSpec((2, 3, D_MODEL), lambda i: (0, 0, 0))],
        out_shape=[jax.ShapeDtypeStruct((rows, D_MODEL), F32),
                   jax.ShapeDtypeStruct((2, 3, D_MODEL), F32)],
        compiler_params=_cp("arbitrary"),
    )(*args)


def _ffn_forward(x, g3, mods, j, w4_in, w4_out_of, n_x, name):
    rows = x.shape[0]
    hn = _rms_mod_fwd(x, g3, mods, j, n_x, BF16, name + "_mod")
    au, s = _ffn_up(hn, w4_in, name + "_up")
    w4_out, dep = w4_out_of(s)
    if dep is not None:
        mods = mods + dep[0:1, 0:1]
    wo = w4_out.reshape(D_FF, D_MODEL)
    out, f = _matmul_resid(s, wo, x, mods, 3 * j + 2, 0.5, n_x, rows, name + "_down")
    return out, (x, hn, au, s, f), w4_out


def _ffn_backward(dout, saved, g3, mods, j, w4_in, w4_out, n_x, send, name):
    x, hn, au, s, f = saved
    rows = x.shape[0]
    tm = ROW_TILE
    n_i = rows // tm
    h = w4_in.shape[2]
    dy, dgate = _gate_dy(dout, f, mods, 3 * j + 2, 0.5, n_x, rows, name + "_dy")
    wo2 = w4_out.reshape(2, h, D_MODEL)
    dz = _ffn_bwd_dz(dy, wo2, au, name + "_dz")
    d_wi = _matmul_tn(
        hn, dz, pl.BlockSpec((tm, D_MODEL), lambda q, k: (k, 0)),
        pl.BlockSpec((1, tm, h), lambda q, k: (q // 2, k, q % 2)),
        (4, D_MODEL, h), pl.BlockSpec((1, D_MODEL, h), lambda q, k: (q, 0, 0)), (4, n_i), name + "_dwi")
    mods = mods + send(d_wi, "wi")
    d_wo = _matmul_tn(
        s, dy, pl.BlockSpec((tm, h), lambda n, k: (k, n)), pl.BlockSpec((tm, D_MODEL), lambda n, k: (k, 0)),
        (D_FF, D_MODEL), pl.BlockSpec((h, D_MODEL), lambda n, k: (n, 0)), (2, n_i), name + "_dwo")
    mods = mods + send(d_wo.reshape(w4_out.shape), "wo")
    pairs = [(dz, pl.BlockSpec((1, tm, h), functools.partial(lambda q, i: (q // 2, i, q % 2), q)),
              w4_in, pl.BlockSpec((1, D_MODEL, h), functools.partial(lambda q, i: (q, 0, 0), q)))
             for q in range(4)]
    dx, stats = _bwd_dx(pairs, x, dout, n_i, g3, mods, j, n_x, name + "_dx")
    return dx, stats, dgate


def _matmul_nt(a, w, name):
    rows, k = a.shape
    n = w.shape[0]
    tm = ROW_TILE

    def body(a_ref, w_ref, o_ref):
        o_ref[...] = _dot_nt(a_ref[...], w_ref[...])

    return pl.pallas_call(
        body, name=name, grid=(rows // tm,),
        in_specs=[pl.BlockSpec((tm, k), lambda i: (i, 0)), pl.BlockSpec((n, k), lambda i: (0, 0))],
        out_specs=pl.BlockSpec((tm, n), lambda i: (i, 0)),
        out_shape=jax.ShapeDtypeStruct((rows, n), F32),
        compiler_params=_cp("parallel"),
    )(a, w)


def _rope_tables(t_len, rows):
    n = A_HEAD_DIM // 4
    freqs = ROPE_BASE ** (-jnp.arange(n, dtype=F32) / n)
    t = jnp.arange(t_len)
    ang_r = (t // GRID_W).astype(F32)[:, None] * freqs
    ang_c = (t % GRID_W).astype(F32)[:, None] * freqs
    cos = jnp.concatenate([jnp.cos(ang_r), jnp.cos(ang_r), jnp.cos(ang_c), jnp.cos(ang_c)], axis=1)
    sin = jnp.concatenate([-jnp.sin(ang_r), jnp.sin(ang_r), -jnp.sin(ang_c), jnp.sin(ang_c)], axis=1)
    cos = jnp.concatenate([cos, jnp.ones((rows - t_len, A_HEAD_DIM), F32)], axis=0)
    sin = jnp.concatenate([sin, jnp.zeros((rows - t_len, A_HEAD_DIM), F32)], axis=0)
    return jnp.concatenate([cos, cos, sin, sin], axis=1)


def _swap16(x):
    n = x.shape[1]
    lane = lax.broadcasted_iota(jnp.int32, x.shape, 1)
    first = jnp.bitwise_and(lane, 16) == 0
    return jnp.where(first, pltpu.roll(x, n - 16, 1), pltpu.roll(x, 16, 1))


def _log_sigmoid(x):
    return jnp.minimum(x, 0.0) - jnp.log(1.0 + jnp.exp(-jnp.abs(x)))


def _proj_fwd(h, wcat, wg2, bias2, cs, name):
    rows = h.shape[0]
    tm = ROW_TILE

    def body(h_ref, w_ref, wg_ref, b_ref, cs_ref, zc_ref, la_ref):
        z = _dot(h_ref[...], w_ref[...])
        cos = cs_ref[:, 0:128]
        sin = cs_ref[:, 128:256]
        cosq = jnp.concatenate([cos] * 4, axis=1)
        sinq = jnp.concatenate([sin] * 4, axis=1)
        q = z[:, ZC_Q:ZC_QK]
        zc_ref[:, ZC_Q:ZC_QK] = q * cosq + _swap16(q) * sinq
        zc_ref[:, ZC_QK:ZC_KV] = z[:, ZC_QK:ZC_KV]
        kk = z[:, ZC_KV:ZC_KV + 128]
        zc_ref[:, ZC_KV:ZC_KV + 128] = kk * cos + _swap16(kk) * sin
        zc_ref[:, ZC_KV + 128:ZC_W] = z[:, ZC_KV + 128:ZC_W]
        zg = z[:, ZC_G:ZC_W]
        pre = _dot(zg.astype(BF16), wg_ref[...]) + b_ref[...]
        la_ref[...] = _log_sigmoid(pre) / B_GATE_NORM

    return pl.pallas_call(
        body, name=name, grid=(rows // tm,),
        in_specs=[pl.BlockSpec((tm, D_MODEL), lambda i: (i, 0)),
                  pl.BlockSpec((D_MODEL, ZC_W), lambda i: (0, 0)),
                  pl.BlockSpec((128, 512), lambda i: (0, 0)),
                  pl.BlockSpec((1, 512), lambda i: (0, 0)),
                  pl.BlockSpec((tm, 256), lambda i: (i, 0))],
        out_specs=[pl.BlockSpec((tm, ZC_W), lambda i: (i, 0)),
                   pl.BlockSpec((tm, 512), lambda i: (i, 0))],
        out_shape=[jax.ShapeDtypeStruct((rows, ZC_W), F32),
                   jax.ShapeDtypeStruct((rows, 512), F32)],
        compiler_params=_cp("parallel"),
    )(h, wcat, wg2, bias2, cs)


_QB = WINDOW


def _attn_specs(t_len, l_ctx):
    nb = t_len // _QB
    kvb = ZC_KV // 256
    return [pl.BlockSpec(memory_space=pltpu.SMEM),
            pl.BlockSpec((_QB, 512), lambda n: (n, 0)),
            pl.BlockSpec((_QB, 256), lambda n: (jnp.maximum(n - 1, 0), kvb)),
            pl.BlockSpec((_QB, 256), lambda n: (n, kvb)),
            pl.BlockSpec((_QB, 256), lambda n: (n + 1, kvb)),
            pl.BlockSpec((l_ctx, 256), lambda n: (t_len // l_ctx, kvb))], nb


def _attn_probs(n, t_len, sink_ref, qv, kp, kc, kn, kx, g):
    hd = A_HEAD_DIM
    ks = slice(g * hd, (g + 1) * hd)
    vs = slice(128 + g * hd, 128 + (g + 1) * hd)
    kb = jnp.concatenate([kp[:, ks], kc[:, ks], kn[:, ks]], axis=0).astype(BF16)
    vb = jnp.concatenate([kp[:, vs], kc[:, vs], kn[:, vs]], axis=0).astype(BF16)
    kxb = kx[:, ks].astype(BF16)
    vxb = kx[:, vs].astype(BF16)
    qg = jnp.concatenate([qv[:, (4 * g + r) * hd:(4 * g + r + 1) * hd] for r in range(4)], axis=0).astype(BF16)
    qi = lax.broadcasted_iota(jnp.int32, (_QB, 3 * _QB), 0)
    kj = lax.broadcasted_iota(jnp.int32, (_QB, 3 * _QB), 1)
    kpos = n * _QB - _QB + kj
    valid = (kpos >= 0) & (kpos < t_len) & (jnp.abs(kj - _QB - qi) <= WINDOW)
    valid4 = jnp.concatenate([valid] * 4, axis=0)
    scale = hd ** -0.5
    s = jnp.where(valid4, _dot_nt(qg, kb) * scale, -jnp.inf)
    sc = _dot_nt(qg, kxb) * scale
    sk = jnp.concatenate([jnp.full((_QB, 1), sink_ref[4 * g + r], F32) for r in range(4)], axis=0)
    m = jnp.maximum(jnp.maximum(jnp.max(s, axis=-1, keepdims=True), jnp.max(sc, axis=-1, keepdims=True)), sk)
    p = jnp.exp(s - m)
    pc = jnp.exp(sc - m)
    ps = jnp.exp(sk - m)
    inv = 1.0 / (jnp.sum(p, axis=-1, keepdims=True) + jnp.sum(pc, axis=-1, keepdims=True) + ps)
    return p * inv, pc * inv, ps * inv, qg, kb, vb, kxb, vxb


def _attn_fwd(zc, sink, t_len, l_ctx, name):
    in_specs, nb = _attn_specs(t_len, l_ctx)

    def body(sink_ref, q_ref, kp_ref, kc_ref, kn_ref, kx_ref, o_ref):
        n = pl.program_id(0)
        outs = []
        for g in range(A_KV_HEADS):
            p, pc, _, _, _, vb, _, vxb = _attn_probs(
                n, t_len, sink_ref, q_ref[...], kp_ref[...], kc_ref[...], kn_ref[...], kx_ref[...], g)
            o = _dot(p.astype(BF16), vb) + _dot(pc.astype(BF16), vxb)
            outs += [o[r * _QB:(r + 1) * _QB] for r in range(4)]
        o_ref[...] = jnp.concatenate(outs, axis=1)

    return pl.pallas_call(
        body, name=name, grid=(nb,), in_specs=in_specs,
        out_specs=pl.BlockSpec((_QB, 512), lambda n: (n, 0)),
        out_shape=jax.ShapeDtypeStruct((t_len, 512), F32),
        compiler_params=_cp("parallel"),
    )(sink, zc, zc, zc, zc, zc)


def _attn_bwd(zc, sink, o, dcat, t_len, l_ctx, name):
    rows = zc.shape[0]
    in_specs, nb = _attn_specs(t_len, l_ctx)
    in_specs = in_specs + [pl.BlockSpec((_QB, 512), lambda n: (n, 0)), pl.BlockSpec((_QB, 512), lambda n: (n, 0))]
    hd = A_HEAD_DIM
    scale = hd ** -0.5

    def body(sink_ref, q_ref, kp_ref, kc_ref, kn_ref, kx_ref, o_ref, do_ref, dq_ref, dkv_ref, dsink_ref):
        n = pl.program_id(0)

        @pl.when(n == 0)
        def _():
            dkv_ref[...] = jnp.zeros_like(dkv_ref)
            dsink_ref[...] = jnp.zeros_like(dsink_ref)

        ov = o_ref[...]
        dov = do_ref[...]
        dqs, dkbs, dvbs, dkxs, dvxs, dsinks = [], [], [], [], [], []
        for g in range(A_KV_HEADS):
            p, pc, ps, qg, kb, vb, kxb, vxb = _attn_probs(
                n, t_len, sink_ref, q_ref[...], kp_ref[...], kc_ref[...], kn_ref[...], kx_ref[...], g)
            og = jnp.concatenate([ov[:, (4 * g + r) * hd:(4 * g + r + 1) * hd] for r in range(4)], axis=0)
            dog = jnp.concatenate([dov[:, (4 * g + r) * hd:(4 * g + r + 1) * hd] for r in range(4)], axis=0)
            delta = jnp.sum(og * dog, axis=-1, keepdims=True)
            dogb = dog.astype(BF16)
            ds = (p * (_dot_nt(dogb, vb) - delta) * scale).astype(BF16)
            dsc = (pc * (_dot_nt(dogb, vxb) - delta) * scale).astype(BF16)
            dsk = ps * (0.0 - delta)
            dqg = _dot(ds, kb) + _dot(dsc, kxb)
            dqs += [dqg[r * _QB:(r + 1) * _QB] for r in range(4)]
            dkbs.append(_dot_tn(ds, qg))
            dvbs.append(_dot_tn(p.astype(BF16), dogb))
            dkxs.append(_dot_tn(dsc, qg))
            dvxs.append(_dot_tn(pc.astype(BF16), dogb))
            for r in range(4):
                tot = jnp.sum(dsk[r * _QB:(r + 1) * _QB], axis=0, keepdims=True)
                dsinks.append(jnp.broadcast_to(tot, (1, 128)))
        dsink_ref[...] += jnp.concatenate(dsinks, axis=0)
        dq_ref[...] = jnp.concatenate(dqs, axis=1)
        band = jnp.concatenate(dkbs + dvbs, axis=1)
        ctxc = jnp.concatenate(dkxs + dvxs, axis=1)
        r_prev = pl.multiple_of(jnp.maximum(n - 1, 0) * _QB, _QB)
        r_cur = pl.multiple_of(n * _QB, _QB)
        r_next = pl.multiple_of((n + 1) * _QB, _QB)
        dkv_ref[pl.ds(r_prev, _QB), :] += band[0:_QB]
        dkv_ref[pl.ds(r_cur, _QB), :] += band[_QB:2 * _QB]
        dkv_ref[pl.ds(r_next, _QB), :] += band[2 * _QB:3 * _QB]
        dkv_ref[t_len:t_len + l_ctx, :] += ctxc

    return pl.pallas_call(
        body, name=name, grid=(nb,), in_specs=in_specs,
        out_specs=[pl.BlockSpec((_QB, 512), lambda n: (n, 0)),
                   pl.BlockSpec((rows, 256), lambda n: (0, 0)),
                   pl.BlockSpec((8, 128), lambda n: (0, 0))],
        out_shape=[jax.ShapeDtypeStruct((t_len, 512), F32),
                   jax.ShapeDtypeStruct((rows, 256), F32),
                   jax.ShapeDtypeStruct((8, 128), F32)],
        compiler_params=_cp("arbitrary"),
    )(sink, zc, zc, zc, zc, zc, o, dcat)


_GC = B_CHUNK


def _split_bf16(a):
    hi = a.astype(BF16)
    return hi, (a - hi.astype(F32)).astype(BF16)


def _gla_chunk_terms(qk, la, reverse):
    q = qk[:, 0:256]
    k = qk[:, 256:512]
    off = 256 if reverse else 0
    lad = la[:, off:off + 256]
    ii = lax.broadcasted_iota(jnp.int32, (_GC, _GC), 0)
    jj = lax.broadcasted_iota(jnp.int32, (_GC, _GC), 1)
    mask = (jj >= ii) if reverse else (jj <= ii)
    tri = jnp.where(mask, 1.0, 0.0).astype(BF16)
    la_hi, la_lo = _split_bf16(lad)
    g = _dot(tri, la_hi) + _dot(tri, la_lo)
    gl = jnp.sum(lad, axis=0, keepdims=True)
    eg = jnp.exp(g)
    eng = jnp.exp(-g)
    eend = jnp.exp(gl - g)
    sc = B_DK ** -0.5
    qt = q * (sc * eg)
    kt = k * eng
    ke = k * eend
    return mask, tri, gl, eg, eng, eend, qt, kt, ke


def _head(a, hh, width):
    return a[:, hh * width:(hh + 1) * width]


def _gla_fwd(zc, la, dep, t_len, l_ctx, name):
    rows = zc.shape[0]
    n_x = t_len // _GC
    n_c = n_x + l_ctx // _GC
    qkb, vb = ZC_QK // 512, ZC_V // 512

    def ch_f(c):
        return lax.rem(c + n_x, n_c)

    def ch_r(c):
        return n_c - 1 - c

    def body(qkf_ref, vf_ref, laf_ref, qkr_ref, vr_ref, lar_ref, dep_ref, of_ref, or_ref, spf_ref, spr_ref, stf, strv):
        del dep_ref
        c = pl.program_id(0)

        @pl.when(c == 0)
        def _():
            stf[...] = jnp.zeros_like(stf)
            strv[...] = jnp.zeros_like(strv)

        results = []
        for qk_ref, v_ref, la_ref, st, reverse in ((qkf_ref, vf_ref, laf_ref, stf, False),
                                                   (qkr_ref, vr_ref, lar_ref, strv, True)):
            mask, _, gl, _, _, _, qt, kt, ke = _gla_chunk_terms(qk_ref[...], la_ref[...], reverse)
            vbf = v_ref[...].astype(BF16)
            qtb, ktb, keb = qt.astype(BF16), kt.astype(BF16), ke.astype(BF16)
            egl = jnp.exp(gl)
            prevs = [st[hh] for hh in range(B_HEADS)]
            outs, news = [], []
            for hh in range(B_HEADS):
                qth, vh = _head(qtb, hh, B_DK), _head(vbf, hh, B_DV)
                att = jnp.where(mask, _dot_nt(qth, _head(ktb, hh, B_DK)), 0.0)
                outs.append(_dot(att.astype(BF16), vh) + _dot_nt(qth, prevs[hh].astype(BF16)))
                news.append(prevs[hh] * _head(egl, hh, B_DK) + _dot_tn(vh, _head(keb, hh, B_DK)))
            results.append((jnp.concatenate(outs, axis=1), prevs, news))
        for (o_all, prevs, news), o_ref, sp_ref, st in zip(results, (of_ref, or_ref), (spf_ref, spr_ref), (stf, strv)):
            o_ref[...] = o_all
            for hh in range(B_HEADS):
                sp_ref[0, hh] = prevs[hh]
                st[hh] = news[hh]

    st_shape = (B_HEADS, B_DV, B_DK)
    return pl.pallas_call(
        body, name=name, grid=(n_c,),
        in_specs=[pl.BlockSpec((_GC, 512), lambda c: (ch_f(c), qkb)),
                  pl.BlockSpec((_GC, 512), lambda c: (ch_f(c), vb)),
                  pl.BlockSpec((_GC, 512), lambda c: (ch_f(c), 0)),
                  pl.BlockSpec((_GC, 512), lambda c: (ch_r(c), qkb)),
                  pl.BlockSpec((_GC, 512), lambda c: (ch_r(c), vb)),
                  pl.BlockSpec((_GC, 512), lambda c: (ch_r(c), 0)),
                  pl.BlockSpec((8, 128), lambda c: (0, 0))],
        out_specs=[pl.BlockSpec((_GC, 512), lambda c: (ch_f(c), 0)),
                   pl.BlockSpec((_GC, 512), lambda c: (ch_r(c), 0)),
                   pl.BlockSpec((1,) + st_shape, lambda c: (c, 0, 0, 0)),
                   pl.BlockSpec((1,) + st_shape, lambda c: (c, 0, 0, 0))],
        out_shape=[jax.ShapeDtypeStruct((rows, 512), F32), jax.ShapeDtypeStruct((rows, 512), F32),
                   jax.ShapeDtypeStruct((n_c,) + st_shape, F32), jax.ShapeDtypeStruct((n_c,) + st_shape, F32)],
        scratch_shapes=[pltpu.VMEM(st_shape, F32), pltpu.VMEM(st_shape, F32)],
        compiler_params=_cp("arbitrary"),
    )(zc, zc, la, zc, zc, la, dep)


def _gla_bwd(zc, la, spf, spr, dosum, t_len, l_ctx, name):
    rows = zc.shape[0]
    n_x = t_len // _GC
    n_c = n_x + l_ctx // _GC
    n_all = rows // _GC
    qkb, vb = ZC_QK // 512, ZC_V // 512

    def scan_of(c):
        return jnp.maximum(n_c - 1 - c, 0)

    def ch_f(c):
        return jnp.where(c < n_c, lax.rem(scan_of(c) + n_x, n_c), c)

    def ch_r(c):
        return c

    def do_of(ch):
        return jnp.minimum(ch, n_x - 1)

    def body(qkf_ref, vf_ref, laf_ref, spf_ref, dof_ref, qkr_ref, vr_ref, lar_ref, spr_ref, dor_ref,
             dqkf_ref, dvf_ref, dlaf_ref, dqkr_ref, dvr_ref, dlar_ref, dsf, dsr):
        c = pl.program_id(0)

        @pl.when(c == 0)
        def _():
            dsf[...] = jnp.zeros_like(dsf)
            dsr[...] = jnp.zeros_like(dsr)

        @pl.when(c >= n_c)
        def _():
            for r in (dqkf_ref, dvf_ref, dlaf_ref, dqkr_ref, dvr_ref, dlar_ref):
                r[...] = jnp.zeros_like(r)

        @pl.when(c < n_c)
        def _():
            sc = B_DK ** -0.5
            results = []
            for qk_ref, v_ref, la_ref, sp_ref, do_ref, dst, reverse, ch in (
                    (qkf_ref, vf_ref, laf_ref, spf_ref, dof_ref, dsf, False, ch_f(c)),
                    (qkr_ref, vr_ref, lar_ref, spr_ref, dor_ref, dsr, True, ch_r(c))):
                mask, tri, gl, eg, eng, eend, qt, kt, ke = _gla_chunk_terms(qk_ref[...], la_ref[...], reverse)
                vbf = v_ref[...].astype(BF16)
                dob = jnp.where(ch < n_x, do_ref[...], 0.0).astype(BF16)
                qtb, ktb, keb = qt.astype(BF16), kt.astype(BF16), ke.astype(BF16)
                egl = jnp.exp(gl)
                prevs = [sp_ref[0, hh] for hh in range(B_HEADS)]
                dnews = [dst[hh] for hh in range(B_HEADS)]
                dqts, dkts, dkes, dvs, dprevs, dgls = [], [], [], [], [], []
                for hh in range(B_HEADS):
                    qth, kth, keh = _head(qtb, hh, B_DK), _head(ktb, hh, B_DK), _head(keb, hh, B_DK)
                    vh, doh = _head(vbf, hh, B_DV), _head(dob, hh, B_DV)
                    eglh = _head(egl, hh, B_DK)
                    dsb = dnews[hh].astype(BF16)
                    att = jnp.where(mask, _dot_nt(qth, kth), 0.0).astype(BF16)
                    datt = jnp.where(mask, _dot_nt(doh, vh), 0.0).astype(BF16)
                    dqts.append(_dot(datt, kth) + _dot(doh, prevs[hh].astype(BF16)))
                    dkts.append(_dot_tn(datt, qth))
                    dvs.append(_dot_tn(att, doh) + _dot_nt(keh, dsb))
                    dkes.append(_dot(vh, dsb))
                    dprevs.append(dnews[hh] * eglh + _dot_tn(doh, qth))
                    dgls.append(jnp.sum(dnews[hh] * prevs[hh], axis=0, keepdims=True) * eglh)
                dqt = jnp.concatenate(dqts, axis=1)
                dkt = jnp.concatenate(dkts, axis=1)
                dke = jnp.concatenate(dkes, axis=1)
                dgl = jnp.sum(dke * ke, axis=0, keepdims=True) + jnp.concatenate(dgls, axis=1)
                dg_hi, dg_lo = _split_bf16(dqt * qt - dkt * kt - dke * ke)
                dla = _dot_tn(tri, dg_hi) + _dot_tn(tri, dg_lo) + dgl
                dqk = jnp.concatenate([dqt * (sc * eg), dkt * eng + dke * eend], axis=1)
                results.append((dqk, jnp.concatenate(dvs, axis=1), dla, dprevs))
            for (dqk, dv, dla, dprevs), dqk_ref, dv_ref, dla_ref, dst in zip(
                    results, (dqkf_ref, dqkr_ref), (dvf_ref, dvr_ref), (dlaf_ref, dlar_ref), (dsf, dsr)):
                dqk_ref[...] = dqk
                dv_ref[...] = dv
                dla_ref[...] = dla
                for hh in range(B_HEADS):
                    dst[hh] = dprevs[hh]

    st_shape = (B_HEADS, B_DV, B_DK)

    def side(chf):
        return [pl.BlockSpec((_GC, 512), lambda c: (chf(c), qkb)),
                pl.BlockSpec((_GC, 512), lambda c: (chf(c), vb)),
                pl.BlockSpec((_GC, 512), lambda c: (chf(c), 0)),
                pl.BlockSpec((1,) + st_shape, lambda c: (scan_of(c), 0, 0, 0)),
                pl.BlockSpec((_GC, 512), lambda c: (do_of(chf(c)), 0))]

    def out_side(chf):
        return [pl.BlockSpec((_GC, 512), lambda c: (chf(c), 0)),
                pl.BlockSpec((_GC, 512), lambda c: (chf(c), 0)),
                pl.BlockSpec((_GC, 256), lambda c: (chf(c), 0))]

    shp = [jax.ShapeDtypeStruct((rows, 512), F32), jax.ShapeDtypeStruct((rows, 512), F32),
           jax.ShapeDtypeStruct((rows, 256), F32)]
    return pl.pallas_call(
        body, name=name, grid=(n_all,),
        in_specs=side(ch_f) + side(ch_r),
        out_specs=out_side(ch_f) + out_side(ch_r),
        out_shape=shp + shp,
        scratch_shapes=[pltpu.VMEM(st_shape, F32), pltpu.VMEM(st_shape, F32)],
        compiler_params=_cp("arbitrary"),
    )(zc, zc, la, spf, dosum, zc, zc, la, spr, dosum)


def _gla_out_fwd(o_a, o_f, o_r, zc, gla_g, t_len, name):
    tm = ROW_TILE
    rb = ZC_R // 512

    def body(oa_ref, of_ref, or_ref, r_ref, g_ref, cat_ref):
        osum = of_ref[...] + or_ref[...]
        g = g_ref[...]
        pieces = []
        for hh in range(B_HEADS):
            oh = osum[:, hh * B_DV:(hh + 1) * B_DV]
            rs = lax.rsqrt(jnp.mean(oh * oh, axis=-1, keepdims=True) + RMS_EPS)
            pieces.append((oh * rs) * g)
        r = r_ref[...]
        cat_ref[:, 0:512] = oa_ref[...].astype(BF16)
        cat_ref[:, 512:1024] = (jnp.concatenate(pieces, axis=1) * (r * _sigmoid(r))).astype(BF16)

    return pl.pallas_call(
        body, name=name, grid=(t_len // tm,),
        in_specs=[pl.BlockSpec((tm, 512), lambda i: (i, 0)),
                  pl.BlockSpec((tm, 512), lambda i: (i, 0)),
                  pl.BlockSpec((tm, 512), lambda i: (i, 0)),
                  pl.BlockSpec((tm, 512), lambda i: (i, rb)),
                  pl.BlockSpec((1, B_DV), lambda i: (0, 0))],
        out_specs=pl.BlockSpec((tm, D_MODEL), lambda i: (i, 0)),
        out_shape=jax.ShapeDtypeStruct((t_len, D_MODEL), BF16),
        compiler_params=_cp("parallel"),
    )(o_a, o_f, o_r, zc, gla_g)


def _gla_out_bwd(dcat, o_f, o_r, zc, gla_g, t_len, name):
    tm = ROW_TILE
    rb = ZC_R // 512

    def body(d_ref, of_ref, or_ref, r_ref, g_ref, dos_ref, dr_ref, dg_ref):
        i = pl.program_id(0)
        osum = of_ref[...] + or_ref[...]
        g = g_ref[...]
        r = r_ref[...]
        dgo = d_ref[...]
        sg = _sigmoid(r)
        dnrmg = dgo * (r * sg)
        nrms, dos = [], []
        dg_acc = jnp.zeros((1, B_DV), F32)
        for hh in range(B_HEADS):
            oh = osum[:, hh * B_DV:(hh + 1) * B_DV]
            rs = lax.rsqrt(jnp.mean(oh * oh, axis=-1, keepdims=True) + RMS_EPS)
            nrm = oh * rs
            dn = dnrmg[:, hh * B_DV:(hh + 1) * B_DV]
            dg_acc = dg_acc + jnp.sum(dn * nrm, axis=0, keepdims=True)
            dnn = dn * g
            dos.append(rs * (dnn - nrm * jnp.mean(dnn * nrm, axis=-1, keepdims=True)))
            nrms.append(nrm * g)
        dos_ref[...] = jnp.concatenate(dos, axis=1)
        dr_ref[...] = dgo * jnp.concatenate(nrms, axis=1) * (sg * (1.0 + r * (1.0 - sg)))

        @pl.when(i == 0)
        def _():
            dg_ref[...] = jnp.zeros_like(dg_ref)

        dg_ref[...] += dg_acc

    return pl.pallas_call(
        body, name=name, grid=(t_len // tm,),
        in_specs=[pl.BlockSpec((tm, 512), lambda i: (i, 1)),
                  pl.BlockSpec((tm, 512), lambda i: (i, 0)),
                  pl.BlockSpec((tm, 512), lambda i: (i, 0)),
                  pl.BlockSpec((tm, 512), lambda i: (i, rb)),
                  pl.BlockSpec((1, B_DV), lambda i: (0, 0))],
        out_specs=[pl.BlockSpec((tm, 512), lambda i: (i, 0)),
                   pl.BlockSpec((tm, 512), lambda i: (i, 0)),
                   pl.BlockSpec((1, B_DV), lambda i: (0, 0))],
        out_shape=[jax.ShapeDtypeStruct((t_len, 512), F32), jax.ShapeDtypeStruct((t_len, 512), F32),
                   jax.ShapeDtypeStruct((1, B_DV), F32)],
        compiler_params=_cp("arbitrary"),
    )(dcat, o_f, o_r, zc, gla_g)


def _mix_prep(dq, dkv, dqk_f, dqk_r, dv_f, dv_r, d_r, dla_f, dla_r, zc, wg2, bias2, cs, t_len, name):
    rows = zc.shape[0]
    tm = ROW_TILE
    n_x = t_len // tm
    gb = ZC_G // 128

    def xrow(i):
        return jnp.minimum(i, n_x - 1)

    def body(dq_ref, dkv_ref, dqkf_ref, dqkr_ref, dvf_ref, dvr_ref, dr_ref, dlaf_ref, dlar_ref, zg_ref, wg_ref,
             b_ref, cs_ref, dz_ref, dwg_ref, db_ref):
        i = pl.program_id(0)
        is_x = i < n_x
        cos = cs_ref[:, 0:128]
        sin = cs_ref[:, 128:256]
        cosq = jnp.concatenate([cos] * 4, axis=1)
        sinq = jnp.concatenate([sin] * 4, axis=1)
        dqv = jnp.where(is_x, dq_ref[...], 0.0)
        dz_ref[:, ZC_Q:ZC_QK] = (dqv * cosq + _swap16(dqv * sinq)).astype(BF16)
        dz_ref[:, ZC_QK:ZC_V] = (dqkf_ref[...] + dqkr_ref[...]).astype(BF16)
        dz_ref[:, ZC_V:ZC_R] = (dvf_ref[...] + dvr_ref[...]).astype(BF16)
        dz_ref[:, ZC_R:ZC_KV] = jnp.where(is_x, dr_ref[...], 0.0).astype(BF16)
        dk = dkv_ref[:, 0:128]
        dz_ref[:, ZC_KV:ZC_KV + 128] = (dk * cos + _swap16(dk * sin)).astype(BF16)
        dz_ref[:, ZC_KV + 128:ZC_G] = dkv_ref[:, 128:256].astype(BF16)
        zgb = zg_ref[...].astype(BF16)
        wg = wg_ref[...]
        pre = _dot(zgb, wg) + b_ref[...]
        dla = jnp.concatenate([dlaf_ref[...], dlar_ref[...]], axis=1)
        dpre = dla * (_sigmoid(-pre) / B_GATE_NORM)
        dpb = dpre.astype(BF16)
        dz_ref[:, ZC_G:ZC_W] = _dot_nt(dpb, wg).astype(BF16)

        @pl.when(i == 0)
        def _():
            dwg_ref[...] = jnp.zeros_like(dwg_ref)
            db_ref[...] = jnp.zeros_like(db_ref)

        dwg_ref[...] += _dot_tn(zgb, dpb)
        db_ref[...] += jnp.sum(dpre, axis=0, keepdims=True)

    return pl.pallas_call(
        body, name=name, grid=(rows // tm,),
        in_specs=[pl.BlockSpec((tm, 512), lambda i: (xrow(i), 0)),
                  pl.BlockSpec((tm, 256), lambda i: (i, 0)),
                  pl.BlockSpec((tm, 512), lambda i: (i, 0)),
                  pl.BlockSpec((tm, 512), lambda i: (i, 0)),
                  pl.BlockSpec((tm, 512), lambda i: (i, 0)),
                  pl.BlockSpec((tm, 512), lambda i: (i, 0)),
                  pl.BlockSpec((tm, 512), lambda i: (xrow(i), 0)),
                  pl.BlockSpec((tm, 256), lambda i: (i, 0)),
                  pl.BlockSpec((tm, 256), lambda i: (i, 0)),
                  pl.BlockSpec((tm, 128), lambda i: (i, gb)),
                  pl.BlockSpec((128, 512), lambda i: (0, 0)),
                  pl.BlockSpec((1, 512), lambda i: (0, 0)),
                  pl.BlockSpec((tm, 256), lambda i: (i, 0))],
        out_specs=[pl.BlockSpec((tm, ZC_W), lambda i: (i, 0)),
                   pl.BlockSpec((128, 512), lambda i: (0, 0)),
                   pl.BlockSpec((1, 512), lambda i: (0, 0))],
        out_shape=[jax.ShapeDtypeStruct((rows, ZC_W), BF16),
                   jax.ShapeDtypeStruct((128, 512), F32),
                   jax.ShapeDtypeStruct((1, 512), F32)],
        compiler_params=_cp("arbitrary"),
    )(dq, dkv, dqk_f, dqk_r, dv_f, dv_r, d_r, dla_f, dla_r, zc, wg2, bias2, cs)


def _gate_weights(w_a2_f, b_a_f, w_a2_b, b_a_b):
    wg2 = jnp.zeros((128, 512), F32)
    wg2 = wg2.at[0:B_GATE_RANK, 0:256].set(w_a2_f).at[B_GATE_RANK:2 * B_GATE_RANK, 256:512].set(w_a2_b)
    bias2 = jnp.concatenate([b_a_f, b_a_b]).reshape(1, 512)
    return wg2.astype(BF16), bias2


_WIN_PERM = ((0, 512), (768, 1280), (1280, 1792), (1792, 2304), (512, 768), (2304, 2336))


def _w_in_to_cat(w_in_full):
    parts = [w_in_full[:, a:b] for a, b in _WIN_PERM]
    parts.append(jnp.zeros((w_in_full.shape[0], ZC_W - PROJ_DIM), w_in_full.dtype))
    return jnp.concatenate(parts, axis=1)


def _cat_to_w_in(d_wcat):
    return jnp.concatenate([d_wcat[:, ZC_Q:ZC_QK], d_wcat[:, ZC_KV:ZC_G], d_wcat[:, ZC_QK:ZC_KV],
                            d_wcat[:, ZC_G:ZC_G + 2 * B_GATE_RANK]], axis=1)


def _mixer_ab_forward(x1, g3, mods, wcat, wg2, bias2, sink, gla_g, w_out, cs, t_len, l_ctx, n_x, pace):
    h = _rms_mod_fwd(x1, g3, mods, 1, n_x, BF16, "mix0_mod")
    zc, la = _proj_fwd(h, wcat, wg2, bias2, cs, "mix0_proj")
    dep = pace("proj", zc)
    o_a = _attn_fwd(zc, sink + dep[0, 0], t_len, l_ctx, "mix0_attn")
    dep = pace("attn", o_a)
    o_f, o_r, spf, spr = _gla_fwd(zc, la, dep, t_len, l_ctx, "mix0_gla")
    dep = pace("gla", o_f)
    cat = _gla_out_fwd(o_a, o_f, o_r, zc, gla_g + dep[0:1, 0:1], t_len, "mix0_glaout")
    x2, y = _matmul_resid(cat, w_out, x1, mods, 5, 1.0, n_x, t_len, "mix0_out")
    return x2, (x1, h, zc, la, o_a, o_f, o_r, spf, spr, cat, y)


def _mixer_ab_backward(dx2, saved, g3, mods, wcat, wg2, bias2, sink, gla_g, w_out, cs, t_len, l_ctx, n_x):
    x1, h, zc, la, o_a, o_f, o_r, spf, spr, cat, y = saved
    rows = x1.shape[0]
    tm = ROW_TILE
    dy, dgate = _gate_dy(dx2, y, mods, 5, 1.0, n_x, t_len, "mix0_dy")
    dcat = _matmul_nt(dy, w_out, "mix0_dcat")
    d_wout = _matmul_tn(
        cat, dy, pl.BlockSpec((tm, D_MODEL), lambda n, k: (k, 0)), pl.BlockSpec((tm, D_MODEL), lambda n, k: (k, 0)),
        (D_MODEL, D_MODEL), pl.BlockSpec((D_MODEL, D_MODEL), lambda n, k: (0, 0)), (1, t_len // tm), "mix0_dwout")
    dos, d_r, d_glag = _gla_out_bwd(dcat, o_f, o_r, zc, gla_g, t_len, "mix0_dglaout")
    dqk_f, dv_f, dla_f, dqk_r, dv_r, dla_r = _gla_bwd(zc, la, spf, spr, dos, t_len, l_ctx, "mix0_dgla")
    dq, dkv, dsink = _attn_bwd(zc, sink, o_a, dcat, t_len, l_ctx, "mix0_dattn")
    dzc, dwg2, dbias2 = _mix_prep(dq, dkv, dqk_f, dqk_r, dv_f, dv_r, d_r, dla_f, dla_r, zc, wg2, bias2, cs, t_len,
                                  "mix0_prep")
    d_wcat = _matmul_tn(
        h, dzc, pl.BlockSpec((tm, D_MODEL), lambda n, k: (k, 0)), pl.BlockSpec((tm, ZC_W), lambda n, k: (k, 0)),
        (D_MODEL, ZC_W), pl.BlockSpec((D_MODEL, ZC_W), lambda n, k: (0, 0)), (1, rows // tm), "mix0_dwin")
    pairs = [(dzc, pl.BlockSpec((tm, ZC_W), lambda i: (i, 0)), wcat, pl.BlockSpec((D_MODEL, ZC_W), lambda i: (0, 0)))]
    dx1, stats = _bwd_dx(pairs, x1, dx2, t_len // tm, g3, mods, 1, n_x, "mix0_dx")
    return dx1, stats, dgate, d_wcat, dwg2, dbias2, dsink, d_glag, d_wout


_PT = 256
_PH = 16


def _pool_band(n, t_len, w, transpose):
    shape = (_PT, _PT + 2 * _PH)
    a = n * _PT + lax.broadcasted_iota(jnp.int32, shape, 0)
    b = n * _PT - _PH + lax.broadcasted_iota(jnp.int32, shape, 1)
    t, s = (b, a) if transpose else (a, b)
    lo = jnp.maximum(t - w // 2, 0)
    hi = jnp.minimum(t + (w - w // 2), t_len)
    inside = (s >= lo) & (s < hi) & (t >= 0) & (t < t_len)
    mean = jnp.where(inside, 1.0 / (hi - lo).astype(F32), 0.0)
    return mean - jnp.where(s == t, 1.0, 0.0)


def _pool_halo(p_ref, c_ref, n_ref):
    return jnp.concatenate([p_ref[_PT - _PH:_PT, :], c_ref[...], n_ref[0:_PH, :]], axis=0)


def _pool_specs(t_len):
    nb = t_len // _PT
    return [pl.BlockSpec((_PT, D_MODEL), lambda n: (jnp.maximum(n - 1, 0), 0)),
            pl.BlockSpec((_PT, D_MODEL), lambda n: (n, 0)),
            pl.BlockSpec((_PT, D_MODEL), lambda n: (jnp.minimum(n + 1, nb - 1), 0))], nb


def _pool_fwd(h, wp, pscale, x1, mods, t_len, name):
    halo_specs, nb = _pool_specs(t_len)

    def body(hp_ref, hc_ref, hn_ref, w_ref, ps_ref, x_ref, m_ref, x2_ref, pooled_ref, ypre_ref):
        n = pl.program_id(0)
        hcat = _pool_halo(hp_ref, hc_ref, hn_ref)
        ys = []
        for gi, w in enumerate(POOL_WINDOWS):
            cols = slice(gi * POOL_GROUP, (gi + 1) * POOL_GROUP)
            pooled = _dot_hi(_pool_band(n, t_len, w, False), hcat[:, cols]).astype(BF16)
            pooled_ref[:, cols] = pooled
            ys.append(_dot(pooled, w_ref[gi]))
        ypre = jnp.concatenate(ys, axis=1)
        ypre_ref[...] = ypre
        x2_ref[...] = x_ref[...] + m_ref[0, 5:6, :] * (ypre * ps_ref[...])

    return pl.pallas_call(
        body, name=name, grid=(nb,),
        in_specs=halo_specs + [pl.BlockSpec((4, POOL_GROUP, POOL_GROUP), lambda n: (0, 0, 0)),
                               pl.BlockSpec((1, D_MODEL), lambda n: (0, 0)),
                               pl.BlockSpec((_PT, D_MODEL), lambda n: (n, 0)),
                               pl.BlockSpec((1, N_MOD, D_MODEL), lambda n: (0, 0, 0))],
        out_specs=[pl.BlockSpec((_PT, D_MODEL), lambda n: (n, 0))] * 3,
        out_shape=[jax.ShapeDtypeStruct((t_len, D_MODEL), F32), jax.ShapeDtypeStruct((t_len, D_MODEL), BF16),
                   jax.ShapeDtypeStruct((t_len, D_MODEL), F32)],
        compiler_params=_cp("parallel"),
    )(h, h, h, wp, pscale, x1, mods)


def _pool_bwd_a(dx2, ypre, wp, pscale, mods, t_len, name):
    nb = t_len // _PT

    def body(d_ref, y_ref, w_ref, ps_ref, m_ref, dyp_ref, dpl_ref, dgate_ref, dps_ref):
        n = pl.program_id(0)
        dv = d_ref[...]
        ypre = y_ref[...]
        ps = ps_ref[...]
        dy = dv * m_ref[0, 5:6, :]
        dyp = (dy * ps).astype(BF16)
        dyp_ref[...] = dyp
        for gi in range(len(POOL_WINDOWS)):
            cols = slice(gi * POOL_GROUP, (gi + 1) * POOL_GROUP)
            dpl_ref[:, cols] = _dot_nt(dyp[:, cols], w_ref[gi])

        @pl.when(n == 0)
        def _():
            dgate_ref[...] = jnp.zeros_like(dgate_ref)
            dps_ref[...] = jnp.zeros_like(dps_ref)

        dgate_ref[...] += jnp.sum(dv * (ypre * ps), axis=0, keepdims=True)
        dps_ref[...] += jnp.sum(dy * ypre, axis=0, keepdims=True)

    return pl.pallas_call(
        body, name=name, grid=(nb,),
        in_specs=[pl.BlockSpec((_PT, D_MODEL), lambda n: (n, 0)),
                  pl.BlockSpec((_PT, D_MODEL), lambda n: (n, 0)),
                  pl.BlockSpec((4, POOL_GROUP, POOL_GROUP), lambda n: (0, 0, 0)),
                  pl.BlockSpec((1, D_MODEL), lambda n: (0, 0)),
                  pl.BlockSpec((1, N_MOD, D_MODEL), lambda n: (0, 0, 0))],
        out_specs=[pl.BlockSpec((_PT, D_MODEL), lambda n: (n, 0)),
                   pl.BlockSpec((_PT, D_MODEL), lambda n: (n, 0)),
                   pl.BlockSpec((1, D_MODEL), lambda n: (0, 0)),
                   pl.BlockSpec((1, D_MODEL), lambda n: (0, 0))],
        out_shape=[jax.ShapeDtypeStruct((t_len, D_MODEL), BF16), jax.ShapeDtypeStruct((t_len, D_MODEL), F32),
                   jax.ShapeDtypeStruct((1, D_MODEL), F32), jax.ShapeDtypeStruct((1, D_MODEL), F32)],
        compiler_params=_cp("arbitrary"),
    )(dx2, ypre, wp, pscale, mods)


def _pool_bwd_dx(dpl, x1, dx2, g3, mods, t_len, name):
    halo_specs, nb = _pool_specs(t_len)

    def body(dp_ref, dc_ref, dn_ref, x_ref, d_ref, g_ref, m_ref, dx_ref, acc_ref):
        n = pl.program_id(0)
        dcat = _pool_halo(dp_ref, dc_ref, dn_ref)
        dhs = []
        for gi, w in enumerate(POOL_WINDOWS):
            cols = slice(gi * POOL_GROUP, (gi + 1) * POOL_GROUP)
            dhs.append(_dot_hi(_pool_band(n, t_len, w, True), dcat[:, cols]))
        dh = jnp.concatenate(dhs, axis=1)
        g = g_ref[1:2, :]
        scale = m_ref[0, 4:5, :]
        dx = _rms_mod_bwd_tail(dh, x_ref[...], g, scale, 0, acc_ref, n == 0)
        dx_ref[...] = d_ref[...] + dx

    return pl.pallas_call(
        body, name=name, grid=(nb,),
        in_specs=halo_specs + [pl.BlockSpec((_PT, D_MODEL), lambda n: (n, 0)),
                               pl.BlockSpec((_PT, D_MODEL), lambda n: (n, 0)),
                               pl.BlockSpec((3, D_MODEL), lambda n: (0, 0)),
                               pl.BlockSpec((1, N_MOD, D_MODEL), lambda n: (0, 0, 0))],
        out_specs=[pl.BlockSpec((_PT, D_MODEL), lambda n: (n, 0)),
                   pl.BlockSpec((2, 3, D_MODEL), lambda n: (0, 0, 0))],
        out_shape=[jax.ShapeDtypeStruct((t_len, D_MODEL), F32), jax.ShapeDtypeStruct((2, 3, D_MODEL), F32)],
        compiler_params=_cp("arbitrary"),
    )(dpl, dpl, dpl, x1, dx2, g3, mods)


def _mixer_pool_forward(x1, g3, mods, wp, pscale, t_len):
    h = _rms_mod_fwd(x1, g3, mods, 1, t_len // ROW_TILE, F32, "mix1_mod")
    x2, pooled, ypre = _pool_fwd(h, wp, pscale, x1, mods, t_len, "mix1_pool")
    return x2, (x1, pooled, ypre)


def _mixer_pool_backward(dx2, saved, g3, mods, wp, pscale, t_len):
    x1, pooled, ypre = saved
    tm = ROW_TILE
    dyp, dpl, dgate, dps = _pool_bwd_a(dx2, ypre, wp, pscale, mods, t_len, "mix1_da")
    d_wp = _matmul_tn(
        pooled, dyp, pl.BlockSpec((tm, POOL_GROUP), lambda g, k: (k, g)),
        pl.BlockSpec((tm, POOL_GROUP), lambda g, k: (k, g)),
        (4, POOL_GROUP, POOL_GROUP), pl.BlockSpec((1, POOL_GROUP, POOL_GROUP), lambda g, k: (g, 0, 0)),
        (4, t_len // tm), "mix1_dwp")
    dx1, stats = _pool_bwd_dx(dpl, x1, dx2, g3, mods, t_len, "mix1_dx")
    return dx1, stats, dgate, dps, d_wp


def _final_loss(x3, final_g, target, name):
    t_len = x3.shape[0]
    tm = ROW_TILE

    def body(x_ref, g_ref, t_ref, dx_ref, loss_ref, dg_ref):
        i = pl.program_id(0)
        xv = x_ref[...]
        g = g_ref[...]
        r = lax.rsqrt(jnp.mean(xv * xv, axis=-1, keepdims=True) + RMS_EPS)
        xhat = xv * r
        err = xhat * g - t_ref[...]
        part = 0.5 * jnp.sum(jnp.mean(err * err, axis=-1, keepdims=True), axis=0, keepdims=True)
        dy = err * (1.0 / D_MODEL)

        @pl.when(i == 0)
        def _():
            loss_ref[...] = jnp.zeros_like(loss_ref)
            dg_ref[...] = jnp.zeros_like(dg_ref)

        loss_ref[...] += jnp.broadcast_to(part, (1, 128))
        dg_ref[...] += jnp.sum(dy * xhat, axis=0, keepdims=True)
        dxh = dy * g
        dx_ref[...] = r * (dxh - xhat * jnp.mean(dxh * xhat, axis=-1, keepdims=True))

    return pl.pallas_call(
        body, name=name, grid=(t_len // tm,),
        in_specs=[pl.BlockSpec((tm, D_MODEL), lambda i: (i, 0)),
                  pl.BlockSpec((1, D_MODEL), lambda i: (0, 0)),
                  pl.BlockSpec((tm, D_MODEL), lambda i: (i, 0))],
        out_specs=[pl.BlockSpec((tm, D_MODEL), lambda i: (i, 0)),
                   pl.BlockSpec((1, 128), lambda i: (0, 0)),
                   pl.BlockSpec((1, D_MODEL), lambda i: (0, 0))],
        out_shape=[jax.ShapeDtypeStruct((t_len, D_MODEL), F32), jax.ShapeDtypeStruct((1, 128), F32),
                   jax.ShapeDtypeStruct((1, D_MODEL), F32)],
        compiler_params=_cp("arbitrary"),
    )(x3, final_g, target)


_CROWS = 16


def _adaln_fwd(c16, w_mod, bias_k, name):
    n_l, _, cols = w_mod.shape

    def body(c_ref, w_ref, b_ref, o_ref):
        cv = c_ref[...]
        sc = (cv * _sigmoid(cv)).astype(BF16)
        o_ref[0] = _dot(sc, w_ref[0].astype(BF16)) + b_ref[0]

    return pl.pallas_call(
        body, name=name, grid=(n_l,),
        in_specs=[pl.BlockSpec((_CROWS, D_MODEL), lambda l: (0, 0)),
                  pl.BlockSpec((1, D_MODEL, cols), lambda l: (l, 0, 0)),
                  pl.BlockSpec((1, 1, cols), lambda l: (l, 0, 0))],
        out_specs=pl.BlockSpec((1, _CROWS, cols), lambda l: (l, 0, 0)),
        out_shape=jax.ShapeDtypeStruct((n_l, _CROWS, cols), F32),
        compiler_params=_cp("parallel"),
    )(c16, w_mod, bias_k)


def _adaln_bwd(c16, d16, w_mod, dmmc_k, name):
    n_l, _, cols = w_mod.shape

    def body(c_ref, d_ref, w_ref, dm_ref, gw_ref, cp_ref):
        layer = pl.program_id(0)
        cv = c_ref[...]
        gw_ref[0] = _dot_tn_hi(cv * _sigmoid(cv), d_ref[0])

        @pl.when(layer == 0)
        def _():
            cp_ref[...] = jnp.sum(w_ref[0] * dm_ref[...], axis=1, keepdims=True)

    return pl.pallas_call(
        body, name=name, grid=(n_l,),
        in_specs=[pl.BlockSpec((_CROWS, D_MODEL), lambda l: (0, 0)),
                  pl.BlockSpec((1, _CROWS, cols), lambda l: (l, 0, 0)),
                  pl.BlockSpec((1, D_MODEL, cols), lambda l: (0, 0, 0)),
                  pl.BlockSpec((1, cols), lambda l: (0, 0))],
        out_specs=[pl.BlockSpec((1, D_MODEL, cols), lambda l: (l, 0, 0)),
                   pl.BlockSpec((D_MODEL, 1), lambda l: (0, 0))],
        out_shape=[jax.ShapeDtypeStruct((n_l, D_MODEL, cols), F32), jax.ShapeDtypeStruct((D_MODEL, 1), F32)],
        compiler_params=_cp("arbitrary"),
    )(c16, d16, w_mod, dmmc_k)


def _cctx_grad(cparts, c_ctx2, name):
    def body(p_ref, c_ref, o_ref):
        tot = ((p_ref[0] + p_ref[2]) + p_ref[4]) + p_ref[6]
        cv = c_ref[...]
        sg = _sigmoid(cv)
        o_ref[...] = tot * (sg * (1.0 + cv * (1.0 - sg)))

    return pl.pallas_call(
        body, name=name, out_shape=jax.ShapeDtypeStruct((8, 128), F32),
        in_specs=[pl.BlockSpec(memory_space=pltpu.VMEM), pl.BlockSpec(memory_space=pltpu.VMEM)],
        out_specs=pl.BlockSpec(memory_space=pltpu.VMEM),
    )(cparts, c_ctx2)


def _sum_devices(ga, name):
    def body(g_ref, o_ref):
        acc = g_ref[0]
        for d in range(1, N_DEV):
            acc = acc + g_ref[d]
        o_ref[...] = acc

    return pl.pallas_call(
        body, name=name, out_shape=jax.ShapeDtypeStruct(ga.shape[1:], F32),
        in_specs=[pl.BlockSpec(memory_space=pltpu.VMEM)], out_specs=pl.BlockSpec(memory_space=pltpu.VMEM),
    )(ga)


def _place():
    return lax.axis_index("x"), lax.axis_index("y"), lax.axis_index("c")


def _flip(a, d):
    return 1 - a if d else a


_CHIP_FLIPS = ((1, 0), (0, 1), (1, 1))


def _allgather_small(v, name):
    r, cc = v.shape

    def body(v_ref, out_ref, send_sems, recv_sems, local_sem):
        x, y, c = _place()
        me = 4 * x + 2 * y + c
        mine = pltpu.make_async_copy(v_ref, out_ref.at[me], local_sem)
        mine.start()
        sends = []
        for k in range(1, N_DEV):
            peer = (_flip(x, (k >> 2) & 1), _flip(y, (k >> 1) & 1), _flip(c, k & 1))
            cp = pltpu.make_async_remote_copy(src_ref=v_ref, dst_ref=out_ref.at[me], send_sem=send_sems.at[k - 1],
                                              recv_sem=recv_sems.at[k - 1], device_id=peer, device_id_type=MESH)
            cp.start()
            sends.append(cp)
        for k in range(1, N_DEV):
            px, py, pc = _flip(x, (k >> 2) & 1), _flip(y, (k >> 1) & 1), _flip(c, k & 1)
            pltpu.make_async_remote_copy(src_ref=v_ref, dst_ref=out_ref.at[4 * px + 2 * py + pc],
                                         send_sem=send_sems.at[k - 1], recv_sem=recv_sems.at[k - 1],
                                         device_id=(px, py, pc), device_id_type=MESH).wait_recv()
        for cp in sends:
            cp.wait_send()
        mine.wait()

    return pl.pallas_call(
        body, name=name, out_shape=jax.ShapeDtypeStruct((N_DEV, r, cc), F32),
        in_specs=[pl.BlockSpec(memory_space=pltpu.VMEM)], out_specs=pl.BlockSpec(memory_space=pltpu.VMEM),
        scratch_shapes=[pltpu.SemaphoreType.DMA((N_DEV - 1,)), pltpu.SemaphoreType.DMA((N_DEV - 1,)),
                        pltpu.SemaphoreType.DMA],
        compiler_params=pltpu.CompilerParams(vmem_limit_bytes=VMEM_LIMIT_BYTES),
    )(v)


_HBM_SPEC = pl.BlockSpec(memory_space=pltpu.HBM)
_SEM_SPEC = pl.BlockSpec(memory_space=pltpu.SEMAPHORE)
_EFFECT = pltpu.SideEffectType.DATAFLOW_SIDE_EFFECTING


def _in_hbm(a):
    return pltpu.with_memory_space_constraint(a, pltpu.HBM)


def _gather_start(arrs, groups, after, name):
    n, n_g = len(arrs), len(groups)

    def body(*refs):
        ins, zones = refs[:n], refs[n:2 * n]
        sems = refs[2 * n + 1:2 * n + 1 + 2 * n_g]
        token = refs[2 * n + 1 + 2 * n_g + 2 * n]
        x, y, c = _place()
        k_me = 2 * x + y
        for g, members in enumerate(groups):
            for t, a in enumerate(members):
                for j, (dx, dy) in enumerate(_CHIP_FLIPS):
                    pltpu.make_async_remote_copy(
                        src_ref=ins[a], dst_ref=zones[a].at[k_me], send_sem=sems[2 * g].at[3 * t + j],
                        recv_sem=sems[2 * g + 1].at[3 * t + j], device_id=(_flip(x, dx), _flip(y, dy), c),
                        device_id_type=MESH).start()
        token[...] = jnp.zeros_like(token)

    k_own = 2 * lax.axis_index("x") + lax.axis_index("y")
    zones = [lax.dynamic_update_slice(lax.empty((N_CHIPS,) + a.shape, a.dtype), a[None], (k_own,) + (0,) * a.ndim)
             for a in arrs]
    sem_shapes = []
    for members in groups:
        sem_shapes += [pltpu.SemaphoreType.DMA((3 * len(members),))] * 2
    outs = pl.pallas_call(
        body, name=name,
        out_shape=sem_shapes + [pltpu.HBM(a.shape, a.dtype) for a in arrs]
        + [pltpu.HBM(z.shape, z.dtype) for z in zones] + [jax.ShapeDtypeStruct((8, 128), F32)],
        in_specs=[_HBM_SPEC] * (2 * n) + [pl.BlockSpec(memory_space=pl.ANY)],
        out_specs=[_SEM_SPEC] * (2 * n_g) + [_HBM_SPEC] * (2 * n) + [pl.BlockSpec(memory_space=pltpu.VMEM)],
        input_output_aliases={i: 2 * n_g + i for i in range(2 * n)},
        compiler_params=pltpu.CompilerParams(has_side_effects=_EFFECT),
    )(*[_in_hbm(a) for a in arrs], *[_in_hbm(z) for z in zones], after)
    sems = outs[:2 * n_g]
    thru = outs[2 * n_g:2 * n_g + n]
    zones = outs[2 * n_g + n:2 * n_g + 2 * n]
    return [(sems[2 * g], sems[2 * g + 1]) for g in range(n_g)], thru, zones, outs[-1]


def _gather_wait(shards, zones, send_sems, recv_sems, after, name):
    m = len(shards)

    def body(*refs):
        ins, zs = refs[:m], refs[m:2 * m]
        ssem, rsem = refs[2 * m], refs[2 * m + 1]
        x, y, c = _place()
        for t in range(m):
            for j, (dx, dy) in enumerate(_CHIP_FLIPS):
                px, py = _flip(x, dx), _flip(y, dy)
                cp = pltpu.make_async_remote_copy(
                    src_ref=ins[t], dst_ref=zs[t].at[2 * px + py], send_sem=ssem.at[3 * t + j],
                    recv_sem=rsem.at[3 * t + j], device_id=(px, py, c), device_id_type=MESH)
                cp.wait_send()
                cp.wait_recv()

    outs = pl.pallas_call(
        body, name=name,
        out_shape=[pltpu.HBM(a.shape, a.dtype) for a in list(shards) + list(zones)],
        in_specs=[_HBM_SPEC] * (2 * m) + [_SEM_SPEC, _SEM_SPEC, pl.BlockSpec(memory_space=pl.ANY)],
        out_specs=[_HBM_SPEC] * (2 * m),
        input_output_aliases={i: i for i in range(2 * m)},
        compiler_params=pltpu.CompilerParams(has_side_effects=_EFFECT),
    )(*shards, *zones, send_sems, recv_sems, after)
    return outs[m:]


def _scatter_start(arrs, name):
    n = len(arrs)

    def body(*refs):
        ins, lands = refs[:n], refs[n:2 * n]
        ssem, rsem = refs[2 * n], refs[2 * n + 1]
        token = refs[2 * n + 2 + 2 * n]
        x, y, c = _place()
        for a in range(n):
            for j, (dx, dy) in enumerate(_CHIP_FLIPS):
                px, py = _flip(x, dx), _flip(y, dy)
                pltpu.make_async_remote_copy(
                    src_ref=ins[a].at[2 * px + py], dst_ref=lands[a].at[j], send_sem=ssem.at[3 * a + j],
                    recv_sem=rsem.at[3 * a + j], device_id=(px, py, c), device_id_type=MESH).start()
        token[...] = jnp.zeros_like(token)

    lands = [lax.empty((3,) + a.shape[1:], a.dtype) for a in arrs]
    outs = pl.pallas_call(
        body, name=name,
        out_shape=[pltpu.SemaphoreType.DMA((3 * n,))] * 2 + [pltpu.HBM(a.shape, a.dtype) for a in arrs]
        + [pltpu.HBM(z.shape, z.dtype) for z in lands] + [jax.ShapeDtypeStruct((8, 128), F32)],
        in_specs=[_HBM_SPEC] * (2 * n),
        out_specs=[_SEM_SPEC] * 2 + [_HBM_SPEC] * (2 * n) + [pl.BlockSpec(memory_space=pltpu.VMEM)],
        input_output_aliases={i: 2 + i for i in range(2 * n)},
        compiler_params=pltpu.CompilerParams(has_side_effects=_EFFECT),
    )(*[_in_hbm(a) for a in arrs], *[_in_hbm(z) for z in lands])
    return outs[0], outs[1], outs[2:2 + n], outs[2 + n:2 + 2 * n], outs[-1]


def _scatter_wait(arrs, lands, send_sems, recv_sems, after, name):
    n = len(arrs)

    def body(*refs):
        ins, lz = refs[:n], refs[n:2 * n]
        ssem, rsem = refs[2 * n], refs[2 * n + 1]
        x, y, c = _place()
        for a in range(n):
            for j, (dx, dy) in enumerate(_CHIP_FLIPS):
                px, py = _flip(x, dx), _flip(y, dy)
                cp = pltpu.make_async_remote_copy(
                    src_ref=ins[a].at[2 * px + py], dst_ref=lz[a].at[j], send_sem=ssem.at[3 * a + j],
                    recv_sem=rsem.at[3 * a + j], device_id=(px, py, c), device_id_type=MESH)
                cp.wait_send()
                cp.wait_recv()

    outs = pl.pallas_call(
        body, name=name,
        out_shape=[pltpu.HBM(a.shape, a.dtype) for a in list(arrs) + list(lands)],
        in_specs=[_HBM_SPEC] * (2 * n) + [_SEM_SPEC, _SEM_SPEC, pl.BlockSpec(memory_space=pl.ANY)],
        out_specs=[_HBM_SPEC] * (2 * n),
        input_output_aliases={i: i for i in range(2 * n)},
        compiler_params=pltpu.CompilerParams(has_side_effects=_EFFECT),
    )(*arrs, *lands, send_sems, recv_sems, after)
    return outs[:n], outs[n:]


def _swap_sibling(arrs, name):
    n = len(arrs)

    def body(*refs):
        ins, outs = refs[:n], refs[n:2 * n]
        send_sems, recv_sems = refs[2 * n:]
        x, y, c = _place()
        sends = []
        for a in range(n):
            cp = pltpu.make_async_remote_copy(src_ref=ins[a], dst_ref=outs[a], send_sem=send_sems.at[a],
                                              recv_sem=recv_sems.at[a], device_id=(x, y, 1 - c), device_id_type=MESH)
            cp.start()
            sends.append(cp)
        for cp in sends:
            cp.wait()

    any_spec = pl.BlockSpec(memory_space=pl.ANY)
    return pl.pallas_call(
        body, name=name,
        out_shape=[jax.ShapeDtypeStruct(a.shape, a.dtype) for a in arrs],
        in_specs=[any_spec] * n, out_specs=[any_spec] * n,
        scratch_shapes=[pltpu.SemaphoreType.DMA((n,)), pltpu.SemaphoreType.DMA((n,))],
    )(*arrs)


def _row_tile(rows, cols):
    for tr in (1024, 512, 256, 128, 64, 32, 16, 8):
        if rows % tr == 0 and tr * cols * 4 <= (1 << 20):
            return tr
    return rows


def _partial_sum(g_full, recv, k_idx, name):
    _, r, c = g_full.shape
    tr = _row_tile(r, c)

    def body(k_ref, g_ref, r_ref, o_ref):
        del k_ref
        acc = g_ref[0].astype(F32)
        for j in range(3):
            acc = acc + r_ref[j].astype(F32)
        o_ref[...] = acc

    return pl.pallas_call(
        body, name=name,
        grid_spec=pltpu.PrefetchScalarGridSpec(
            num_scalar_prefetch=1, grid=(r // tr,),
            in_specs=[pl.BlockSpec((1, tr, c), lambda i, k: (k[0], i, 0)),
                      pl.BlockSpec((3, tr, c), lambda i, k: (0, i, 0))],
            out_specs=pl.BlockSpec((tr, c), lambda i, k: (i, 0))),
        out_shape=jax.ShapeDtypeStruct((r, c), F32),
        compiler_params=_cp("parallel"),
    )(k_idx, g_full, recv)


def _adamw(w3, parts, m3, v3, layer, prev, name):
    n_l, r, c = w3.shape
    tr = _row_tile(r, c)
    n_i = r // tr
    n_p = len(parts)
    c1 = 1.0 - ADAM_B1 ** ADAM_STEP
    c2 = 1.0 - ADAM_B2 ** ADAM_STEP
    stacked = [isinstance(p, tuple) for p in parts]

    def body(*refs):
        w_ref, m_ref, v_ref = refs[0:3]
        g_refs = refs[3:3 + n_p]
        go_ref, d_ref, mo_ref, vo_ref = refs[-4:]
        g = None
        for p in range(n_p):
            term = g_refs[p][0] if stacked[p] else g_refs[p][...]
            g = term if g is None else g + term
        w = w_ref[0]
        m = ADAM_B1 * m_ref[0] + (1.0 - ADAM_B1) * g
        v = ADAM_B2 * v_ref[0] + (1.0 - ADAM_B2) * (g * g)
        m_hat = m / c1
        v_hat = v / c2
        go_ref[0] = g
        d_ref[0] = -ADAM_LR * (m_hat / (jnp.sqrt(v_hat) + ADAM_EPS) + ADAM_WD * w)
        mo_ref[0] = m
        vo_ref[0] = v

    blk = pl.BlockSpec((1, tr, c), lambda i: (layer, i, 0))
    in_specs = [blk, blk, blk]
    args = [w3, m3, v3]
    for part in parts:
        if isinstance(part, tuple):
            in_specs.append(pl.BlockSpec((1, tr, c), functools.partial(lambda idx, i: (idx, i, 0), part[1])))
            args.append(part[0])
        else:
            in_specs.append(pl.BlockSpec((tr, c), lambda i: (i, 0)))
            args.append(part)
    aliases = {}
    if prev is not None:
        in_specs += [pl.BlockSpec(memory_space=pl.ANY)] * 4
        aliases = {len(args) + q: q for q in range(4)}
        args += list(prev)
    shp = jax.ShapeDtypeStruct((n_l, r, c), F32)
    return pl.pallas_call(
        body, name=name, grid=(n_i,), in_specs=in_specs, out_specs=[blk] * 4, out_shape=[shp] * 4,
        input_output_aliases=aliases, compiler_params=_cp("parallel"),
    )(*args)


_SMALL_W = 4096
_PACK_ROWS = 352
_N9 = N_MOD * D_MODEL


def _flat_pad(parts, total):
    flat = jnp.concatenate([p.reshape(-1) for p in parts])
    return jnp.concatenate([flat, jnp.zeros((total - flat.shape[0],), F32)])


def kernel(x, c, ctx, c_ctx, w_mod, b_mod, norm_g, ffn1_wi, ffn1_wo, ffn2_wi, ffn2_wo, w_in, w_a2_f, b_a_f, w_a2_b, b_a_b, sink, gla_g, w_out, w_pool, pool_scale, final_g, loss_target, m_c_ctx, m_w_mod, m_b_mod, m_norm_g, m_ffn1_wi, m_ffn1_wo, m_ffn2_wi, m_ffn2_wo, m_w_in, m_w_a2_f, m_b_a_f, m_w_a2_b, m_b_a_b, m_sink, m_gla_g, m_w_out, m_w_pool, m_pool_scale, m_final_g, v_c_ctx, v_w_mod, v_b_mod, v_norm_g, v_ffn1_wi, v_ffn1_wo, v_ffn2_wi, v_ffn2_wo, v_w_in, v_w_a2_f, v_b_a_f, v_w_a2_b, v_b_a_b, v_sink, v_gla_g, v_w_out, v_w_pool, v_pool_scale, v_final_g):
    t_len, l_ctx = x.shape[1], ctx.shape[1]
    tm = ROW_TILE
    pad = (-(t_len + l_ctx)) % tm
    rows0 = t_len + l_ctx + pad
    n_x = t_len // tm
    xi, yi, ci = _place()
    k_me = 2 * xi + yi
    me = 4 * xi + 2 * yi + ci
    mod_cols = w_mod.shape[2]
    n_grp = len(POOL_WINDOWS)

    small_w = _flat_pad([norm_g, w_a2_f, w_a2_b, pool_scale], _SMALL_W).reshape(_SMALL_W // 128, 128)
    shards = [ffn1_wi[0], ffn1_wi[1], ffn1_wo[0], ffn1_wo[1], ffn2_wi[0], ffn2_wi[1], ffn2_wo[0], ffn2_wo[1],
              w_in[0], w_out[0], w_pool[0].reshape(n_grp * w_pool.shape[2], POOL_GROUP)]

    c_all = _allgather_small(c.reshape(8, 128), "gather_cond").reshape(N_DEV, D_MODEL)
    c16 = jnp.concatenate([c_all, c_ctx[None], jnp.zeros((_CROWS - N_DEV - 1, D_MODEL), F32)], axis=0)
    bias_k = lax.dynamic_slice(b_mod, (0, k_me * mod_cols), (2, mod_cols)).reshape(2, 1, mod_cols)
    mm_k = _adaln_fwd(c16, w_mod, bias_k, "adaln_fwd")
    mm_all = _allgather_small(mm_k.reshape(-1, 128), "gather_mod")

    send_src = [s.astype(BF16) for s in shards] + [small_w]
    groups = ([11, 0], [2], [8, 9], [4], [6], [1], [3], [10, 5], [7])
    started = {}

    def gather_start(g, after):
        members = groups[g]
        sems, thru, zones, token = _gather_start([send_src[a] for a in members], (tuple(range(len(members))),),
                                                 after, "gather_start_%d" % g)
        started[g] = (sems[0], thru, zones)
        return token

    def gather_wait(g, after):
        (ssem, rsem), thru, zones = started[g]
        return dict(zip(groups[g], _gather_wait(thru, zones, ssem, rsem, after, "gather_wait_%d" % g)))

    tok = gather_start(0, mm_all)
    mm_all = mm_all.reshape(N_DEV, 2, _CROWS, mod_cols)
    mm_full = jnp.concatenate([mm_all[2 * k] for k in range(N_CHIPS)], axis=-1)
    mm_x = lax.dynamic_index_in_dim(mm_full, me, axis=1, keepdims=False)
    mm_c = mm_full[:, N_DEV]
    mods = [jnp.stack([mm_x[l].reshape(N_MOD, D_MODEL), mm_c[l].reshape(N_MOD, D_MODEL)]) + tok[0:1, 0:1]
            for l in range(2)]
    gathered = gather_wait(0, mods[0])
    sw = gathered[11].reshape(N_CHIPS, _SMALL_W)
    ng_n = norm_g.size
    a2_n = w_a2_f.size
    norm_g_full = jnp.concatenate([sw[k, :ng_n].reshape(norm_g.shape) for k in range(N_CHIPS)], axis=-1)
    w_a2_f_full = jnp.concatenate([sw[k, ng_n:ng_n + a2_n].reshape(w_a2_f.shape[1:]) for k in range(N_CHIPS)], axis=-1)
    w_a2_b_full = jnp.concatenate(
        [sw[k, ng_n + a2_n:ng_n + 2 * a2_n].reshape(w_a2_b.shape[1:]) for k in range(N_CHIPS)], axis=-1)
    pscale_full = jnp.concatenate(
        [sw[k, ng_n + 2 * a2_n:ng_n + 2 * a2_n + pool_scale.size] for k in range(N_CHIPS)]).reshape(1, D_MODEL)
    wg2, bias2 = _gate_weights(w_a2_f_full, b_a_f[0], w_a2_b_full, b_a_b[0])
    gla_g2 = gla_g.reshape(1, B_DV)
    final_g2 = final_g.reshape(1, D_MODEL)
    cs = _rope_tables(t_len, rows0)

    g3 = [norm_g_full[0], norm_g_full[1]]

    xcat = jnp.concatenate([x[0], ctx[0], jnp.zeros((pad, D_MODEL), F32)], axis=0)
    w1i, w1o, w2i, w2o = [None, None], [None, None], [None, None], [None, None]
    w1i[0] = gathered[0]
    mods_a = mods[0] + gather_start(1, w1i[0])[0:1, 0:1] + gather_start(2, w1i[0])[0:1, 0:1]
    x1, sv_a1, w1o[0] = _ffn_forward(xcat, g3[0], mods_a, 0, w1i[0],
                                     lambda s: (gather_wait(1, s)[2], gather_start(3, s)), n_x, "l0_ffn1")
    gathered = gather_wait(2, x1)
    w_in_full = jnp.concatenate([gathered[8][k] for k in range(N_CHIPS)], axis=1)
    wcat = _w_in_to_cat(w_in_full)
    w_out_full = gathered[9].reshape(D_MODEL, D_MODEL)
    mods_a = mods[0] + gather_start(4, x1)[0:1, 0:1]
    pace_group = {"proj": 5, "attn": 6, "gla": 7}
    x2, sv_am = _mixer_ab_forward(x1, g3[0], mods_a, wcat, wg2, bias2, sink[0], gla_g2, w_out_full, cs,
                                  t_len, l_ctx, n_x, lambda tag, res_: gather_start(pace_group[tag], res_))
    mods_a = mods[0] + gather_start(8, x2)[0:1, 0:1]
    w2i[0], w2o[0] = gather_wait(3, x2)[4], gather_wait(4, x2)[6]
    x3, sv_a2, _ = _ffn_forward(x2, g3[0], mods_a, 2, w2i[0], lambda s: (w2o[0], None), n_x, "l0_ffn2")
    w1i[1], w1o[1] = gather_wait(5, x3)[1], gather_wait(6, x3)[3]
    x4, sv_b1, _ = _ffn_forward(x3, g3[1], mods[1], 0, w1i[1], lambda s: (w1o[1], None), n_x, "l1_ffn1")
    gathered = gather_wait(7, x4)
    w2i[1] = gathered[5]
    wp_full = gathered[10].reshape(N_CHIPS, n_grp, -1, POOL_GROUP).transpose(1, 0, 2, 3).reshape(
        n_grp, POOL_GROUP, POOL_GROUP)
    x5, sv_bm = _mixer_pool_forward(x4, g3[1], mods[1], wp_full, pscale_full, t_len)
    x6, sv_b2, w2o[1] = _ffn_forward(x5, g3[1], mods[1], 2, w2i[1], lambda s: (gather_wait(8, s)[7], None), n_x,
                                     "l1_ffn2")
    dx6, loss_part, d_final_g = _final_loss(x6, final_g2, loss_target[0], "final_loss")
    loss = lax.psum(loss_part[0, 0], ("x", "y", "c"))

    sent = []

    def sender(weight, layer):
        def send(grad, tag):
            nm = "%s_%s_%d" % (weight, tag, layer)
            ssem, rsem, thru, lands, token = _scatter_start([grad], "scatter_start_" + nm)
            sent.append((nm, weight + "_" + tag if tag else weight, layer, thru, lands, ssem, rsem))
            return token[0:1, 0:1]
        return send

    dx5, st_b2, dg_b2 = _ffn_backward(dx6, sv_b2, g3[1], mods[1], 2, w2i[1], w2o[1], n_x, sender("ffn2", 1),
                                      "l1_ffn2_b")
    dx4, st_bm, dg_bm, d_pscale, d_wp = _mixer_pool_backward(dx5, sv_bm, g3[1], mods[1], wp_full, pscale_full, t_len)
    d_wp4 = d_wp.reshape(n_grp, N_CHIPS, -1, POOL_GROUP).transpose(1, 0, 2, 3).reshape(N_CHIPS, -1, POOL_GROUP)
    mods1 = mods[1] + sender("w_pool", 0)(d_wp4, "")
    dx3, st_b1, dg_b1 = _ffn_backward(dx4, sv_b1, g3[1], mods1, 0, w1i[1], w1o[1], n_x, sender("ffn1", 1),
                                      "l1_ffn1_b")
    dx2, st_a2, dg_a2 = _ffn_backward(dx3, sv_a2, g3[0], mods[0], 2, w2i[0], w2o[0], n_x, sender("ffn2", 0),
                                      "l0_ffn2_b")
    dx1, st_am, dg_am, d_wcat, d_wg2, d_bias2, d_sink, d_glag, d_wout = _mixer_ab_backward(
        dx2, sv_am, g3[0], mods[0], wcat, wg2, bias2, sink[0], gla_g2, w_out_full, cs, t_len, l_ctx, n_x)
    d_w_in4 = _cat_to_w_in(d_wcat).reshape(D_MODEL, N_CHIPS, -1).transpose(1, 0, 2)
    mods0 = mods[0] + sender("w_in", 0)(d_w_in4, "") + sender("w_out", 0)(d_wout.reshape(N_CHIPS, -1, D_MODEL), "")
    dx0, st_a1, dg_a1 = _ffn_backward(dx1, sv_a1, g3[0], mods0, 0, w1i[0], w1o[0], n_x, sender("ffn1", 0),
                                      "l0_ffn1_b")
    grad_x = dx0[:t_len][None]

    def mod_row(st1, dg1, stm, dgm, st2, dg2, s):
        return jnp.concatenate([st1[s, 0], st1[s, 1], dg1[s, 0], stm[s, 0], stm[s, 1], dgm[s, 0],
                                st2[s, 0], st2[s, 1], dg2[s, 0]])

    dg_bm2 = jnp.concatenate([dg_bm, jnp.zeros_like(dg_bm)], axis=0)[:, None, :]
    d_mm_x0 = mod_row(st_a1, dg_a1, st_am, dg_am, st_a2, dg_a2, 0)
    d_mm_x1 = mod_row(st_b1, dg_b1, st_bm, dg_bm2, st_b2, dg_b2, 0)
    d_mm_c0 = mod_row(st_a1, dg_a1, st_am, dg_am, st_a2, dg_a2, 1)
    d_norm_g = jnp.stack([jnp.stack([st[0, 2] + st[1, 2] for st in (st_a1, st_am, st_a2)]),
                          jnp.stack([st[0, 2] + st[1, 2] for st in (st_b1, st_bm, st_b2)])])
    rk = B_GATE_RANK
    pack = _flat_pad([d_mm_x0, d_mm_x1, d_mm_c0, d_norm_g, d_bias2, d_wg2[0:rk, 0:256], d_wg2[rk:2 * rk, 256:512],
                      d_sink[:, 0], jnp.zeros((120,), F32), d_glag, d_pscale, d_final_g],
                     _PACK_ROWS * 128).reshape(_PACK_ROWS, 128)
    pack_all = _allgather_small(pack, "gather_small_grads")
    tot = _sum_devices(pack_all, "sum_small_grads").reshape(-1)
    rows_all = pack_all.reshape(N_DEV, -1)
    o = 3 * _N9
    g_norm_g_full = tot[o:o + 6 * D_MODEL].reshape(2, 3, D_MODEL)
    o += 6 * D_MODEL
    g_bias2 = tot[o:o + 512]
    o += 512
    g_w_a2_f_full = tot[o:o + rk * 256].reshape(rk, 256)
    o += rk * 256
    g_w_a2_b_full = tot[o:o + rk * 256].reshape(rk, 256)
    o += rk * 256
    g_sink = tot[o:o + A_HEADS]
    o += 128
    g_gla_g = tot[o:o + B_DV]
    o += B_DV
    g_pscale_full = tot[o:o + D_MODEL]
    o += D_MODEL
    g_final_g = tot[o:o + D_MODEL]
    d_mmc_tot = tot[2 * _N9:3 * _N9]
    g_b_mod = jnp.stack([tot[0:_N9] + d_mmc_tot, tot[_N9:2 * _N9]])

    zrows = jnp.zeros((_CROWS - N_DEV - 1, _N9), F32)
    d16 = jnp.stack([jnp.concatenate([rows_all[:, 0:_N9], d_mmc_tot[None], zrows], axis=0),
                     jnp.concatenate([rows_all[:, _N9:2 * _N9], jnp.zeros((1, _N9), F32), zrows], axis=0)])
    d16_k = lax.dynamic_slice(d16, (0, 0, k_me * mod_cols), (2, _CROWS, mod_cols))
    dmmc_k = lax.dynamic_slice(d_mmc_tot, (k_me * mod_cols,), (mod_cols,)).reshape(1, mod_cols)
    g_w_mod, c_part = _adaln_bwd(c16, d16_k, w_mod, dmmc_k, "adaln_bwd")
    c_parts = _allgather_small(c_part.reshape(8, 128), "gather_cctx")
    g_c_ctx = _cctx_grad(c_parts, c_ctx.reshape(8, 128), "cctx_grad").reshape(D_MODEL)

    def small(w, g, m, v, shape3, nm):
        return [o_.reshape(w.shape) for o_ in _adamw(w.reshape(shape3), [g.reshape(shape3[1:])],
                                                    m.reshape(shape3), v.reshape(shape3), 0, None, "adamw_" + nm)]

    def own(a, axis, size):
        return lax.dynamic_slice_in_dim(a, k_me * size, size, axis=axis)

    res = {}
    res["c_ctx"] = small(c_ctx, g_c_ctx, m_c_ctx, v_c_ctx, (1, 8, 128), "c_ctx")
    upd = _adamw(w_mod, [(g_w_mod, 1)], m_w_mod, v_w_mod, 1, None, "adamw_w_mod_1")
    res["w_mod"] = _adamw(w_mod, [(g_w_mod, 0)], m_w_mod, v_w_mod, 0, upd, "adamw_w_mod_0")
    res["b_mod"] = small(b_mod, g_b_mod, m_b_mod, v_b_mod, (1, 2, _N9), "b_mod")
    res["norm_g"] = small(norm_g, own(g_norm_g_full, 2, norm_g.shape[2]), m_norm_g, v_norm_g,
                          (1, 6, norm_g.shape[2]), "norm_g")
    res["w_a2_f"] = small(w_a2_f, own(g_w_a2_f_full, 1, w_a2_f.shape[2]), m_w_a2_f, v_w_a2_f,
                          (1, rk, w_a2_f.shape[2]), "w_a2_f")
    res["b_a_f"] = small(b_a_f, g_bias2[0:256], m_b_a_f, v_b_a_f, (1, 1, 256), "b_a_f")
    res["w_a2_b"] = small(w_a2_b, own(g_w_a2_b_full, 1, w_a2_b.shape[2]), m_w_a2_b, v_w_a2_b,
                          (1, rk, w_a2_b.shape[2]), "w_a2_b")
    res["b_a_b"] = small(b_a_b, g_bias2[256:512], m_b_a_b, v_b_a_b, (1, 1, 256), "b_a_b")
    res["sink"] = small(sink, g_sink, m_sink, v_sink, (1, 1, A_HEADS), "sink")
    res["gla_g"] = small(gla_g, g_gla_g, m_gla_g, v_gla_g, (1, 1, B_DV), "gla_g")
    res["pool_scale"] = small(pool_scale, own(g_pscale_full, 0, pool_scale.shape[1]), m_pool_scale, v_pool_scale,
                              (1, 1, pool_scale.shape[1]), "pool_scale")
    res["final_g"] = small(final_g, g_final_g, m_final_g, v_final_g, (1, 8, 128), "final_g")

    def as3(a):
        n_l = a.shape[0] if a.ndim == 3 else 1
        return a.reshape(n_l, -1, a.shape[-1])

    big_w = {"ffn1_wi": (ffn1_wi, m_ffn1_wi, v_ffn1_wi), "ffn1_wo": (ffn1_wo, m_ffn1_wo, v_ffn1_wo),
             "ffn2_wi": (ffn2_wi, m_ffn2_wi, v_ffn2_wi), "ffn2_wo": (ffn2_wo, m_ffn2_wo, v_ffn2_wo),
             "w_in": (w_in, m_w_in, v_w_in), "w_out": (w_out, m_w_out, v_w_out), "w_pool": (w_pool, m_w_pool, v_w_pool)}
    k_idx = k_me.reshape(1).astype(jnp.int32)
    chain = res["final_g"][0]
    for lo, hi in ((0, 2), (2, 5), (5, 7), (7, 9), (9, 11)):
        partial = []
        for nm, wname, layer, thru, lands, ssem, rsem in sent[lo:hi]:
            mine, recv = _scatter_wait(thru, lands, ssem, rsem, chain, "scatter_wait_" + nm)
            partial.append(_partial_sum(mine[0], recv[0], k_idx, "partial_sum_" + nm))
        other = _swap_sibling(partial, "swap_partials_%d" % lo)
        for (nm, wname, layer, _, _, _, _), p, q in zip(sent[lo:hi], partial, other):
            w, m, v = big_w[wname]
            res[wname] = _adamw(as3(w), [p, q], as3(m), as3(v), layer, res.get(wname),
                                "adamw_%s_%d" % (wname, layer))
            chain = res[wname][3]
    for wname, (w, _, _) in big_w.items():
        res[wname] = [o_.reshape(w.shape) for o_ in res[wname]]

    names = ["c_ctx", "w_mod", "b_mod", "norm_g", "ffn1_wi", "ffn1_wo", "ffn2_wi", "ffn2_wo", "w_in", "w_a2_f",
             "b_a_f", "w_a2_b", "b_a_b", "sink", "gla_g", "w_out", "w_pool", "pool_scale", "final_g"]
    outs = [loss, grad_x]
    for field in range(4):
        outs += [res[nm][field] for nm in names]
    return tuple(outs)
```

```python
import functools

import jax
import jax.numpy as jnp
import numpy as np
from jax import lax
from jax.experimental import pallas as pl
from jax.experimental.pallas import tpu as pltpu

F32 = jnp.float32
BF16 = jnp.bfloat16

D_MODEL = 1024
N_MOD = 9
D_FF = 2816
RMS_EPS = 1e-6
A_HEADS = 8
A_KV_HEADS = 2
A_HEAD_DIM = 64
WINDOW = 128
ROPE_BASE = 10000.0
GRID_W = 64
B_HEADS = 4
B_DK = 64
B_DV = 128
B_GATE_RANK = 16
B_GATE_NORM = 16.0
B_CHUNK = 64
POOL_WINDOWS = (2, 4, 8, 16)
POOL_GROUP = D_MODEL // len(POOL_WINDOWS)
PROJ_DIM = 2336

ADAM_LR = 0.001
ADAM_B1 = 0.9
ADAM_B2 = 0.999
ADAM_EPS = 1e-08
ADAM_WD = 0.01
ADAM_STEP = 10

N_CHIPS = 4
N_DEV = 8
ROW_TILE = 512
VMEM_LIMIT_BYTES = 56 * 1024 * 1024
MESH = pl.DeviceIdType.MESH

ZC_Q, ZC_QK, ZC_V, ZC_R, ZC_KV, ZC_G, ZC_W = 0, 512, 1024, 1536, 2048, 2304, 2432


def _cp(*sem):
    return pltpu.CompilerParams(dimension_semantics=sem if sem else None, vmem_limit_bytes=VMEM_LIMIT_BYTES)


def _dot(a, b):
    return jnp.dot(a, b, preferred_element_type=F32)


def _dot_nt(a, b):
    return lax.dot_general(a, b, (((1,), (1,)), ((), ())), preferred_element_type=F32)


def _dot_tn(a, b):
    return lax.dot_general(a, b, (((0,), (0,)), ((), ())), preferred_element_type=F32)


def _dot_hi(a, b):
    return jnp.dot(a, b, preferred_element_type=F32, precision=lax.Precision.HIGHEST)


def _dot_tn_hi(a, b):
    return lax.dot_general(a, b, (((0,), (0,)), ((), ())), preferred_element_type=F32,
                           precision=lax.Precision.HIGHEST)


def _sigmoid(x):
    return 1.0 / (1.0 + jnp.exp(-x))


def _stream_of(i, n_x):
    return jnp.where(i >= n_x, 1, 0)


def _rms_mod_fwd(x, g3, mods, j, n_x, out_dtype, name):
    rows = x.shape[0]
    tm = ROW_TILE
    n_i = rows // tm

    def body(x_ref, g_ref, m_ref, o_ref):
        xv = x_ref[...]
        r = lax.rsqrt(jnp.mean(xv * xv, axis=-1, keepdims=True) + RMS_EPS)
        g = g_ref[j:j + 1, :]
        shift = m_ref[0, 3 * j:3 * j + 1, :]
        scale = m_ref[0, 3 * j + 1:3 * j + 2, :]
        o_ref[...] = (((xv * r) * g) * (1.0 + scale) + shift).astype(out_dtype)

    return pl.pallas_call(
        body, name=name, grid=(n_i,),
        in_specs=[pl.BlockSpec((tm, D_MODEL), lambda i: (i, 0)),
                  pl.BlockSpec((3, D_MODEL), lambda i: (0, 0)),
                  pl.BlockSpec((1, N_MOD, D_MODEL), lambda i: (_stream_of(i, n_x), 0, 0))],
        out_specs=pl.BlockSpec((tm, D_MODEL), lambda i: (i, 0)),
        out_shape=jax.ShapeDtypeStruct((rows, D_MODEL), out_dtype),
        compiler_params=_cp("parallel"),
    )(x, g3, mods)


def _rms_mod_bwd_tail(dh, xv, g, scale, stream, acc_ref, first):
    r = lax.rsqrt(jnp.mean(xv * xv, axis=-1, keepdims=True) + RMS_EPS)
    xhat = xv * r
    t1 = jnp.sum(dh, axis=0, keepdims=True)
    t2 = jnp.sum(dh * xhat, axis=0, keepdims=True)
    stats = jnp.concatenate([t1, t2 * g, t2 * (1.0 + scale)], axis=0)

    @pl.when(first)
    def _():
        acc_ref[...] = jnp.zeros_like(acc_ref)

    acc_ref[pl.ds(stream, 1)] += stats[None]
    dxh = dh * (g * (1.0 + scale))
    return r * (dxh - xhat * jnp.mean(dxh * xhat, axis=-1, keepdims=True))


def _ffn_up(x, g3, mods, jmod, n_x, w4, name):
    rows = x.shape[0]
    h = w4.shape[2]
    tm = ROW_TILE
    n_i = rows // tm

    def body(x_ref, g_ref, m_ref, wa_ref, wu_ref, hn_ref, au_ref, s_ref):
        xv = x_ref[...]
        r = lax.rsqrt(jnp.mean(xv * xv, axis=-1, keepdims=True) + RMS_EPS)
        g = g_ref[jmod:jmod + 1, :]
        shift = m_ref[0, 3 * jmod:3 * jmod + 1, :]
        scale = m_ref[0, 3 * jmod + 1:3 * jmod + 2, :]
        hv = (((xv * r) * g) * (1.0 + scale) + shift).astype(BF16)

        @pl.when(pl.program_id(0) == 0)
        def _():
            hn_ref[...] = hv

        a = _dot(hv, wa_ref[0])
        u = _dot(hv, wu_ref[0])
        au_ref[0] = a.astype(BF16)
        au_ref[1] = u.astype(BF16)
        s_ref[...] = (a * _sigmoid(a) * u).astype(BF16)

    return pl.pallas_call(
        body, name=name, grid=(2, n_i),
        in_specs=[pl.BlockSpec((tm, D_MODEL), lambda j, i: (i, 0)),
                  pl.BlockSpec((3, D_MODEL), lambda j, i: (0, 0)),
                  pl.BlockSpec((1, N_MOD, D_MODEL), lambda j, i: (_stream_of(i, n_x), 0, 0)),
                  pl.BlockSpec((1, D_MODEL, h), lambda j, i: (j, 0, 0)),
                  pl.BlockSpec((1, D_MODEL, h), lambda j, i: (j + 2, 0, 0))],
        out_specs=[pl.BlockSpec((tm, D_MODEL), lambda j, i: (jnp.where(j == 0, i, n_i - 1), 0)),
                   pl.BlockSpec((2, tm, h), lambda j, i: (0, i, j)),
                   pl.BlockSpec((tm, h), lambda j, i: (i, j))],
        out_shape=[jax.ShapeDtypeStruct((rows, D_MODEL), BF16),
                   jax.ShapeDtypeStruct((2, rows, 2 * h), BF16),
                   jax.ShapeDtypeStruct((rows, 2 * h), BF16)],
        compiler_params=_cp("arbitrary", "arbitrary"),
    )(x, g3, mods, w4, w4)


def _matmul_resid(a, w, xres, mods, gate_idx, coef, n_x, rows, name):
    k = a.shape[1]
    tm = ROW_TILE
    n_i = rows // tm

    def body(a_ref, w_ref, x_ref, m_ref, o_ref, f_ref):
        f = _dot(a_ref[...], w_ref[...])
        gate = m_ref[0, gate_idx:gate_idx + 1, :]
        f_ref[...] = f
        o_ref[...] = x_ref[...] + (coef * gate) * f

    return pl.pallas_call(
        body, name=name, grid=(n_i,),
        in_specs=[pl.BlockSpec((tm, k), lambda i: (i, 0)),
                  pl.BlockSpec((k, D_MODEL), lambda i: (0, 0)),
                  pl.BlockSpec((tm, D_MODEL), lambda i: (i, 0)),
                  pl.BlockSpec((1, N_MOD, D_MODEL), lambda i: (_stream_of(i, n_x), 0, 0))],
        out_specs=[pl.BlockSpec((tm, D_MODEL), lambda i: (i, 0)),
                   pl.BlockSpec((tm, D_MODEL), lambda i: (i, 0))],
        out_shape=[jax.ShapeDtypeStruct((rows, D_MODEL), F32),
                   jax.ShapeDtypeStruct((rows, D_MODEL), F32)],
        compiler_params=_cp("parallel"),
    )(a, w, xres, mods)


def _gate_dy(dout, f, mods, gate_idx, coef, n_x, rows, name):
    tm = ROW_TILE
    n_i = rows // tm

    def body(d_ref, f_ref, m_ref, dy_ref, acc_ref):
        i = pl.program_id(0)
        dv = d_ref[...]
        gate = m_ref[0, gate_idx:gate_idx + 1, :]
        dy_ref[...] = (dv * (coef * gate)).astype(BF16)

        @pl.when(i == 0)
        def _():
            acc_ref[...] = jnp.zeros_like(acc_ref)

        part = coef * jnp.sum(dv * f_ref[...], axis=0, keepdims=True)
        acc_ref[pl.ds(_stream_of(i, n_x), 1)] += part[None]

    return pl.pallas_call(
        body, name=name, grid=(n_i,),
        in_specs=[pl.BlockSpec((tm, D_MODEL), lambda i: (i, 0)),
                  pl.BlockSpec((tm, D_MODEL), lambda i: (i, 0)),
                  pl.BlockSpec((1, N_MOD, D_MODEL), lambda i: (_stream_of(i, n_x), 0, 0))],
        out_specs=[pl.BlockSpec((tm, D_MODEL), lambda i: (i, 0)),
                   pl.BlockSpec((2, 1, D_MODEL), lambda i: (0, 0, 0))],
        out_shape=[jax.ShapeDtypeStruct((rows, D_MODEL), BF16),
                   jax.ShapeDtypeStruct((2, 1, D_MODEL), F32)],
        compiler_params=_cp("arbitrary"),
    )(dout, f, mods)


def _ffn_bwd_dz(dout, f, mods, gate_idx, coef, n_x, wo2, au, name):
    rows = dout.shape[0]
    h = wo2.shape[1]
    tm = ROW_TILE
    n_i = rows // tm

    def body(d_ref, f_ref, m_ref, wo_ref, au_ref, dy_ref, dz_ref, acc_ref):
        j, i = pl.program_id(0), pl.program_id(1)
        dv = d_ref[...]
        gate = m_ref[0, gate_idx:gate_idx + 1, :]
        dyb = (dv * (coef * gate)).astype(BF16)

        @pl.when((j == 0) & (i == 0))
        def _():
            acc_ref[...] = jnp.zeros_like(acc_ref)

        @pl.when(j == 0)
        def _():
            dy_ref[...] = dyb
            part = coef * jnp.sum(dv * f_ref[...], axis=0, keepdims=True)
            acc_ref[pl.ds(_stream_of(i, n_x), 1)] += part[None]

        ds = _dot_nt(dyb, wo_ref[0])
        a = au_ref[0].astype(F32)
        u = au_ref[1].astype(F32)
        sg = _sigmoid(a)
        dz_ref[0] = (ds * u * (sg * (1.0 + a * (1.0 - sg)))).astype(BF16)
        dz_ref[1] = (ds * (a * sg)).astype(BF16)

    return pl.pallas_call(
        body, name=name, grid=(2, n_i),
        in_specs=[pl.BlockSpec((tm, D_MODEL), lambda j, i: (i, 0)),
                  pl.BlockSpec((tm, D_MODEL), lambda j, i: (jnp.where(j == 0, i, n_i - 1), 0)),
                  pl.BlockSpec((1, N_MOD, D_MODEL), lambda j, i: (_stream_of(i, n_x), 0, 0)),
                  pl.BlockSpec((1, h, D_MODEL), lambda j, i: (j, 0, 0)),
                  pl.BlockSpec((2, tm, h), lambda j, i: (0, i, j))],
        out_specs=[pl.BlockSpec((tm, D_MODEL), lambda j, i: (jnp.where(j == 0, i, n_i - 1), 0)),
                   pl.BlockSpec((2, tm, h), lambda j, i: (0, i, j)),
                   pl.BlockSpec((2, 1, D_MODEL), lambda j, i: (0, 0, 0))],
        out_shape=[jax.ShapeDtypeStruct((rows, D_MODEL), BF16),
                   jax.ShapeDtypeStruct((2, rows, 2 * h), BF16),
                   jax.ShapeDtypeStruct((2, 1, D_MODEL), F32)],
        compiler_params=_cp("arbitrary", "arbitrary"),
    )(dout, f, mods, wo2, au)


def _token_tile(rows):
    for tk in (2048, 1536, 1024):
        if rows % tk == 0:
            return tk
    return ROW_TILE


def _matmul_tn(a, b, a_spec, b_spec, out_shape, out_spec, grid, name):
    nd_a = len(a_spec.block_shape)
    nd_b = len(b_spec.block_shape)
    nd_o = len(out_spec.block_shape)
    k_axis = len(grid) - 1
    n_k = grid[k_axis]

    def body(a_ref, b_ref, o_ref, acc_ref):
        av = a_ref[(0,) * (nd_a - 2)]
        bv = b_ref[(0,) * (nd_b - 2)]
        part = _dot_tn(av, bv)
        k = pl.program_id(k_axis)

        @pl.when(k == 0)
        def _():
            acc_ref[...] = part

        @pl.when(k > 0)
        def _():
            acc_ref[...] += part

        @pl.when(k == n_k - 1)
        def _():
            o_ref[(0,) * (nd_o - 2)] = acc_ref[...].astype(BF16)

    return pl.pallas_call(
        body, name=name, grid=grid, in_specs=[a_spec, b_spec], out_specs=out_spec,
        out_shape=jax.ShapeDtypeStruct(out_shape, BF16),
        scratch_shapes=[pltpu.VMEM(tuple(out_spec.block_shape[-2:]), F32)],
        compiler_params=_cp(*(("arbitrary",) * len(grid))),
    )(a, b)


def _bwd_dx(pairs, x, dres, dres_tiles, g3, mods, j, n_x, name):
    rows = x.shape[0]
    tm = ROW_TILE
    n_i = rows // tm
    n_p = len(pairs)
    nds = [(len(p[1].block_shape), len(p[3].block_shape)) for p in pairs]

    def body(*refs):
        dz_refs = refs[0:2 * n_p:2]
        w_refs = refs[1:2 * n_p:2]
        x_ref, dres_ref, g_ref, m_ref, dx_ref, acc_ref = refs[2 * n_p:]
        i = pl.program_id(0)
        dh = None
        for p in range(n_p):
            dzv = dz_refs[p][(0,) * (nds[p][0] - 2)]
            wv = w_refs[p][(0,) * (nds[p][1] - 2)]
            part = _dot_nt(dzv, wv)
            dh = part if dh is None else dh + part
        g = g_ref[j:j + 1, :]
        scale = m_ref[0, 3 * j + 1:3 * j + 2, :]
        dx = _rms_mod_bwd_tail(dh, x_ref[...], g, scale, _stream_of(i, n_x), acc_ref, i == 0)
        dres_v = jnp.where(i < dres_tiles, dres_ref[...], 0.0)
        dx_ref[...] = dres_v + dx

    in_specs, args = [], []
    for dz, dz_spec, w, w_spec in pairs:
        in_specs += [dz_spec, w_spec]
        args += [dz, w]
    in_specs += [pl.BlockSpec((tm, D_MODEL), lambda i: (i, 0)),
                 pl.BlockSpec((tm, D_MODEL), lambda i: (jnp.minimum(i, dres_tiles - 1), 0)),
                 pl.BlockSpec((3, D_MODEL), lambda i: (0, 0)),
                 pl.BlockSpec((1, N_MOD, D_MODEL), lambda i: (_stream_of(i, n_x), 0, 0))]
    args += [x, dres, g3, mods]
    return pl.pallas_call(
        body, name=name, grid=(n_i,), in_specs=in_specs,
        out_specs=[pl.BlockSpec((tm, D_MODEL), lambda i: (i, 0)),
                   pl.BlockSpec((2, 3, D_MODEL), lambda i: (0, 0, 0))],
        out_shape=[jax.ShapeDtypeStruct((rows, D_MODEL), F32),
                   jax.ShapeDtypeStruct((2, 3, D_MODEL), F32)],
        compiler_params=_cp("arbitrary"),
    )(*args)


def _ffn_forward(x, g3, mods, j, w4_in, w4_out_of, n_x, name):
    rows = x.shape[0]
    hn, au, s = _ffn_up(x, g3, mods, j, n_x, w4_in, name + "_up")
    w4_out, dep = w4_out_of(s)
    if dep is not None:
        mods = mods + dep[0:1, 0:1]
    wo = w4_out.reshape(D_FF, D_MODEL)
    out, f = _matmul_resid(s, wo, x, mods, 3 * j + 2, 0.5, n_x, rows, name + "_down")
    return out, (x, hn, au, s, f), w4_out


def _ffn_backward(dout, saved, g3, mods, j, w4_in, w4_out, n_x, send, name):
    x, hn, au, s, f = saved
    rows = x.shape[0]
    tm = ROW_TILE
    n_i = rows // tm
    h = w4_in.shape[2]
    wo2 = w4_out.reshape(2, h, D_MODEL)
    dy, dz, dgate = _ffn_bwd_dz(dout, f, mods, 3 * j + 2, 0.5, n_x, wo2, au, name + "_dz")
    tk = _token_tile(rows)
    n_k = rows // tk
    d_wi = _matmul_tn(
        hn, dz, pl.BlockSpec((tk, D_MODEL), lambda q, k: (k, 0)),
        pl.BlockSpec((1, tk, h), lambda q, k: (q // 2, k, q % 2)),
        (4, D_MODEL, h), pl.BlockSpec((1, D_MODEL, h), lambda q, k: (q, 0, 0)), (4, n_k), name + "_dwi")
    mods = mods + send(d_wi, "wi")
    d_wo = _matmul_tn(
        s, dy, pl.BlockSpec((tk, h), lambda n, k: (k, n)), pl.BlockSpec((tk, D_MODEL), lambda n, k: (k, 0)),
        (D_FF, D_MODEL), pl.BlockSpec((h, D_MODEL), lambda n, k: (n, 0)), (2, n_k), name + "_dwo")
    mods = mods + send(d_wo.reshape(w4_out.shape), "wo")
    pairs = [(dz, pl.BlockSpec((1, tm, h), functools.partial(lambda q, i: (q // 2, i, q % 2), q)),
              w4_in, pl.BlockSpec((1, D_MODEL, h), functools.partial(lambda q, i: (q, 0, 0), q)))
             for q in range(4)]
    dx, stats = _bwd_dx(pairs, x, dout, n_i, g3, mods, j, n_x, name + "_dx")
    return dx, stats, dgate


def _matmul_nt(a, w, name):
    rows, k = a.shape
    n = w.shape[0]
    tm = ROW_TILE

    def body(a_ref, w_ref, o_ref):
        o_ref[...] = _dot_nt(a_ref[...], w_ref[...])

    return pl.pallas_call(
        body, name=name, grid=(rows // tm,),
        in_specs=[pl.BlockSpec((tm, k), lambda i: (i, 0)), pl.BlockSpec((n, k), lambda i: (0, 0))],
        out_specs=pl.BlockSpec((tm, n), lambda i: (i, 0)),
        out_shape=jax.ShapeDtypeStruct((rows, n), F32),
        compiler_params=_cp("parallel"),
    )(a, w)


def _rope_tables(t_len, rows):
    n = A_HEAD_DIM // 4
    freqs = ROPE_BASE ** (-jnp.arange(n, dtype=F32) / n)
    t = jnp.arange(t_len)
    ang_r = (t // GRID_W).astype(F32)[:, None] * freqs
    ang_c = (t % GRID_W).astype(F32)[:, None] * freqs
    cos = jnp.concatenate([jnp.cos(ang_r), jnp.cos(ang_r), jnp.cos(ang_c), jnp.cos(ang_c)], axis=1)
    sin = jnp.concatenate([-jnp.sin(ang_r), jnp.sin(ang_r), -jnp.sin(ang_c), jnp.sin(ang_c)], axis=1)
    cos = jnp.concatenate([cos, jnp.ones((rows - t_len, A_HEAD_DIM), F32)], axis=0)
    sin = jnp.concatenate([sin, jnp.zeros((rows - t_len, A_HEAD_DIM), F32)], axis=0)
    return jnp.concatenate([cos, cos, sin, sin], axis=1)


def _swap16(x):
    n = x.shape[1]
    lane = lax.broadcasted_iota(jnp.int32, x.shape, 1)
    first = jnp.bitwise_and(lane, 16) == 0
    return jnp.where(first, pltpu.roll(x, n - 16, 1), pltpu.roll(x, 16, 1))


def _log_sigmoid(x):
    return jnp.minimum(x, 0.0) - jnp.log(1.0 + jnp.exp(-jnp.abs(x)))


def _proj_fwd(h, wcat, wg2, bias2, cs, name):
    rows = h.shape[0]
    tm = ROW_TILE

    def body(h_ref, w_ref, wg_ref, b_ref, cs_ref, zc_ref, la_ref):
        z = _dot(h_ref[...], w_ref[...])
        cos = cs_ref[:, 0:128]
        sin = cs_ref[:, 128:256]
        cosq = jnp.concatenate([cos] * 4, axis=1)
        sinq = jnp.concatenate([sin] * 4, axis=1)
        q = z[:, ZC_Q:ZC_QK]
        zc_ref[:, ZC_Q:ZC_QK] = q * cosq + _swap16(q) * sinq
        zc_ref[:, ZC_QK:ZC_KV] = z[:, ZC_QK:ZC_KV]
        kk = z[:, ZC_KV:ZC_KV + 128]
        zc_ref[:, ZC_KV:ZC_KV + 128] = kk * cos + _swap16(kk) * sin
        zc_ref[:, ZC_KV + 128:ZC_W] = z[:, ZC_KV + 128:ZC_W]
        zg = z[:, ZC_G:ZC_W]
        pre = _dot(zg.astype(BF16), wg_ref[...]) + b_ref[...]
        la_ref[...] = _log_sigmoid(pre) / B_GATE_NORM

    return pl.pallas_call(
        body, name=name, grid=(rows // tm,),
        in_specs=[pl.BlockSpec((tm, D_MODEL), lambda i: (i, 0)),
                  pl.BlockSpec((D_MODEL, ZC_W), lambda i: (0, 0)),
                  pl.BlockSpec((128, 512), lambda i: (0, 0)),
                  pl.BlockSpec((1, 512), lambda i: (0, 0)),
                  pl.BlockSpec((tm, 256), lambda i: (i, 0))],
        out_specs=[pl.BlockSpec((tm, ZC_W), lambda i: (i, 0)),
                   pl.BlockSpec((tm, 512), lambda i: (i, 0))],
        out_shape=[jax.ShapeDtypeStruct((rows, ZC_W), F32),
                   jax.ShapeDtypeStruct((rows, 512), F32)],
        compiler_params=_cp("parallel"),
    )(h, wcat, wg2, bias2, cs)


_QB = WINDOW


def _attn_specs(t_len, l_ctx):
    nb = t_len // _QB
    kvb = ZC_KV // 256
    return [pl.BlockSpec(memory_space=pltpu.SMEM),
            pl.BlockSpec((_QB, 512), lambda n: (n, 0)),
            pl.BlockSpec((_QB, 256), lambda n: (jnp.maximum(n - 1, 0), kvb)),
            pl.BlockSpec((_QB, 256), lambda n: (n, kvb)),
            pl.BlockSpec((_QB, 256), lambda n: (n + 1, kvb)),
            pl.BlockSpec((l_ctx, 256), lambda n: (t_len // l_ctx, kvb))], nb


def _attn_probs(n, t_len, sink_ref, qv, kp, kc, kn, kx, g):
    hd = A_HEAD_DIM
    ks = slice(g * hd, (g + 1) * hd)
    vs = slice(128 + g * hd, 128 + (g + 1) * hd)
    kb = jnp.concatenate([kp[:, ks], kc[:, ks], kn[:, ks]], axis=0).astype(BF16)
    vb = jnp.concatenate([kp[:, vs], kc[:, vs], kn[:, vs]], axis=0).astype(BF16)
    kxb = kx[:, ks].astype(BF16)
    vxb = kx[:, vs].astype(BF16)
    qg = jnp.concatenate([qv[:, (4 * g + r) * hd:(4 * g + r + 1) * hd] for r in range(4)], axis=0).astype(BF16)
    qi = lax.broadcasted_iota(jnp.int32, (_QB, 3 * _QB), 0)
    kj = lax.broadcasted_iota(jnp.int32, (_QB, 3 * _QB), 1)
    kpos = n * _QB - _QB + kj
    valid = (kpos >= 0) & (kpos < t_len) & (jnp.abs(kj - _QB - qi) <= WINDOW)
    valid4 = jnp.concatenate([valid] * 4, axis=0)
    scale = hd ** -0.5
    s = jnp.where(valid4, _dot_nt(qg, kb) * scale, -jnp.inf)
    sc = _dot_nt(qg, kxb) * scale
    sk = jnp.concatenate([jnp.full((_QB, 1), sink_ref[4 * g + r], F32) for r in range(4)], axis=0)
    m = jnp.maximum(jnp.maximum(jnp.max(s, axis=-1, keepdims=True), jnp.max(sc, axis=-1, keepdims=True)), sk)
    p = jnp.exp(s - m)
    pc = jnp.exp(sc - m)
    ps = jnp.exp(sk - m)
    inv = 1.0 / (jnp.sum(p, axis=-1, keepdims=True) + jnp.sum(pc, axis=-1, keepdims=True) + ps)
    return p * inv, pc * inv, ps * inv, qg, kb, vb, kxb, vxb


def _attn_fwd(zc, sink, t_len, l_ctx, name):
    in_specs, nb = _attn_specs(t_len, l_ctx)

    def body(sink_ref, q_ref, kp_ref, kc_ref, kn_ref, kx_ref, o_ref):
        n = pl.program_id(0)
        outs = []
        for g in range(A_KV_HEADS):
            p, pc, _, _, _, vb, _, vxb = _attn_probs(
                n, t_len, sink_ref, q_ref[...], kp_ref[...], kc_ref[...], kn_ref[...], kx_ref[...], g)
            o = _dot(p.astype(BF16), vb) + _dot(pc.astype(BF16), vxb)
            outs += [o[r * _QB:(r + 1) * _QB] for r in range(4)]
        o_ref[...] = jnp.concatenate(outs, axis=1)

    return pl.pallas_call(
        body, name=name, grid=(nb,), in_specs=in_specs,
        out_specs=pl.BlockSpec((_QB, 512), lambda n: (n, 0)),
        out_shape=jax.ShapeDtypeStruct((t_len, 512), F32),
        compiler_params=_cp("parallel"),
    )(sink, zc, zc, zc, zc, zc)


def _attn_bwd(zc, sink, o, dcat, t_len, l_ctx, name):
    rows = zc.shape[0]
    in_specs, nb = _attn_specs(t_len, l_ctx)
    in_specs = in_specs + [pl.BlockSpec((_QB, 512), lambda n: (n, 0)), pl.BlockSpec((_QB, 512), lambda n: (n, 0))]
    hd = A_HEAD_DIM
    scale = hd ** -0.5

    def body(sink_ref, q_ref, kp_ref, kc_ref, kn_ref, kx_ref, o_ref, do_ref, dq_ref, dkv_ref, dsink_ref):
        n = pl.program_id(0)

        @pl.when(n == 0)
        def _():
            dkv_ref[...] = jnp.zeros_like(dkv_ref)
            dsink_ref[...] = jnp.zeros_like(dsink_ref)

        ov = o_ref[...]
        dov = do_ref[...]
        dqs, dkbs, dvbs, dkxs, dvxs, dsinks = [], [], [], [], [], []
        for g in range(A_KV_HEADS):
            p, pc, ps, qg, kb, vb, kxb, vxb = _attn_probs(
                n, t_len, sink_ref, q_ref[...], kp_ref[...], kc_ref[...], kn_ref[...], kx_ref[...], g)
            og = jnp.concatenate([ov[:, (4 * g + r) * hd:(4 * g + r + 1) * hd] for r in range(4)], axis=0)
            dog = jnp.concatenate([dov[:, (4 * g + r) * hd:(4 * g + r + 1) * hd] for r in range(4)], axis=0)
            delta = jnp.sum(og * dog, axis=-1, keepdims=True)
            dogb = dog.astype(BF16)
            ds = (p * (_dot_nt(dogb, vb) - delta) * scale).astype(BF16)
            dsc = (pc * (_dot_nt(dogb, vxb) - delta) * scale).astype(BF16)
            dsk = ps * (0.0 - delta)
            dqg = _dot(ds, kb) + _dot(dsc, kxb)
            dqs += [dqg[r * _QB:(r + 1) * _QB] for r in range(4)]
            dkbs.append(_dot_tn(ds, qg))
            dvbs.append(_dot_tn(p.astype(BF16), dogb))
            dkxs.append(_dot_tn(dsc, qg))
            dvxs.append(_dot_tn(pc.astype(BF16), dogb))
            for r in range(4):
                tot = jnp.sum(dsk[r * _QB:(r + 1) * _QB], axis=0, keepdims=True)
                dsinks.append(jnp.broadcast_to(tot, (1, 128)))
        dsink_ref[...] += jnp.concatenate(dsinks, axis=0)
        dq_ref[...] = jnp.concatenate(dqs, axis=1)
        band = jnp.concatenate(dkbs + dvbs, axis=1)
        ctxc = jnp.concatenate(dkxs + dvxs, axis=1)
        r_prev = pl.multiple_of(jnp.maximum(n - 1, 0) * _QB, _QB)
        r_cur = pl.multiple_of(n * _QB, _QB)
        r_next = pl.multiple_of((n + 1) * _QB, _QB)
        dkv_ref[pl.ds(r_prev, _QB), :] += band[0:_QB]
        dkv_ref[pl.ds(r_cur, _QB), :] += band[_QB:2 * _QB]
        dkv_ref[pl.ds(r_next, _QB), :] += band[2 * _QB:3 * _QB]
        dkv_ref[t_len:t_len + l_ctx, :] += ctxc

    return pl.pallas_call(
        body, name=name, grid=(nb,), in_specs=in_specs,
        out_specs=[pl.BlockSpec((_QB, 512), lambda n: (n, 0)),
                   pl.BlockSpec((rows, 256), lambda n: (0, 0)),
                   pl.BlockSpec((8, 128), lambda n: (0, 0))],
        out_shape=[jax.ShapeDtypeStruct((t_len, 512), F32),
                   jax.ShapeDtypeStruct((rows, 256), F32),
                   jax.ShapeDtypeStruct((8, 128), F32)],
        compiler_params=_cp("arbitrary"),
    )(sink, zc, zc, zc, zc, zc, o, dcat)


_GC = B_CHUNK


def _split_bf16(a):
    hi = a.astype(BF16)
    return hi, (a - hi.astype(F32)).astype(BF16)


def _gla_chunk_terms(qk, la, reverse):
    q = qk[:, 0:256]
    k = qk[:, 256:512]
    off = 256 if reverse else 0
    lad = la[:, off:off + 256]
    ii = lax.broadcasted_iota(jnp.int32, (_GC, _GC), 0)
    jj = lax.broadcasted_iota(jnp.int32, (_GC, _GC), 1)
    mask = (jj >= ii) if reverse else (jj <= ii)
    tri = jnp.where(mask, 1.0, 0.0).astype(BF16)
    la_hi, la_lo = _split_bf16(lad)
    g = _dot(tri, la_hi) + _dot(tri, la_lo)
    gl = jnp.sum(lad, axis=0, keepdims=True)
    eg = jnp.exp(g)
    eng = jnp.exp(-g)
    eend = jnp.exp(gl - g)
    sc = B_DK ** -0.5
    qt = q * (sc * eg)
    kt = k * eng
    ke = k * eend
    return mask, tri, gl, eg, eng, eend, qt, kt, ke


def _head(a, hh, width):
    return a[:, hh * width:(hh + 1) * width]


def _gla_fwd(zc, la, dep, t_len, l_ctx, name):
    rows = zc.shape[0]
    n_x = t_len // _GC
    n_c = n_x + l_ctx // _GC
    qkb, vb = ZC_QK // 512, ZC_V // 512

    def ch_f(c):
        return lax.rem(c + n_x, n_c)

    def ch_r(c):
        return n_c - 1 - c

    def body(qkf_ref, vf_ref, laf_ref, qkr_ref, vr_ref, lar_ref, dep_ref, of_ref, or_ref, spf_ref, spr_ref, stf, strv):
        del dep_ref
        c = pl.program_id(0)

        @pl.when(c == 0)
        def _():
            stf[...] = jnp.zeros_like(stf)
            strv[...] = jnp.zeros_like(strv)

        results = []
        for qk_ref, v_ref, la_ref, st, reverse in ((qkf_ref, vf_ref, laf_ref, stf, False),
                                                   (qkr_ref, vr_ref, lar_ref, strv, True)):
            mask, _, gl, _, _, _, qt, kt, ke = _gla_chunk_terms(qk_ref[...], la_ref[...], reverse)
            vbf = v_ref[...].astype(BF16)
            qtb, ktb, keb = qt.astype(BF16), kt.astype(BF16), ke.astype(BF16)
            egl = jnp.exp(gl)
            prevs = [st[hh] for hh in range(B_HEADS)]
            outs, news = [], []
            for hh in range(B_HEADS):
                qth, vh = _head(qtb, hh, B_DK), _head(vbf, hh, B_DV)
                att = jnp.where(mask, _dot_nt(qth, _head(ktb, hh, B_DK)), 0.0)
                outs.append(_dot(att.astype(BF16), vh) + _dot_nt(qth, prevs[hh].astype(BF16)))
                news.append(prevs[hh] * _head(egl, hh, B_DK) + _dot_tn(vh, _head(keb, hh, B_DK)))
            results.append((jnp.concatenate(outs, axis=1), prevs, news))
        for (o_all, prevs, news), o_ref, sp_ref, st in zip(results, (of_ref, or_ref), (spf_ref, spr_ref), (stf, strv)):
            o_ref[...] = o_all
            for hh in range(B_HEADS):
                sp_ref[0, hh] = prevs[hh]
                st[hh] = news[hh]

    st_shape = (B_HEADS, B_DV, B_DK)
    return pl.pallas_call(
        body, name=name, grid=(n_c,),
        in_specs=[pl.BlockSpec((_GC, 512), lambda c: (ch_f(c), qkb)),
                  pl.BlockSpec((_GC, 512), lambda c: (ch_f(c), vb)),
                  pl.BlockSpec((_GC, 512), lambda c: (ch_f(c), 0)),
                  pl.BlockSpec((_GC, 512), lambda c: (ch_r(c), qkb)),
                  pl.BlockSpec((_GC, 512), lambda c: (ch_r(c), vb)),
                  pl.BlockSpec((_GC, 512), lambda c: (ch_r(c), 0)),
                  pl.BlockSpec((8, 128), lambda c: (0, 0))],
        out_specs=[pl.BlockSpec((_GC, 512), lambda c: (ch_f(c), 0)),
                   pl.BlockSpec((_GC, 512), lambda c: (ch_r(c), 0)),
                   pl.BlockSpec((1,) + st_shape, lambda c: (c, 0, 0, 0)),
                   pl.BlockSpec((1,) + st_shape, lambda c: (c, 0, 0, 0))],
        out_shape=[jax.ShapeDtypeStruct((rows, 512), F32), jax.ShapeDtypeStruct((rows, 512), F32),
                   jax.ShapeDtypeStruct((n_c,) + st_shape, F32), jax.ShapeDtypeStruct((n_c,) + st_shape, F32)],
        scratch_shapes=[pltpu.VMEM(st_shape, F32), pltpu.VMEM(st_shape, F32)],
        compiler_params=_cp("arbitrary"),
    )(zc, zc, la, zc, zc, la, dep)


def _gla_bwd(zc, la, spf, spr, dosum, t_len, l_ctx, name):
    rows = zc.shape[0]
    n_x = t_len // _GC
    n_c = n_x + l_ctx // _GC
    n_all = rows // _GC
    qkb, vb = ZC_QK // 512, ZC_V // 512

    def scan_of(c):
        return jnp.maximum(n_c - 1 - c, 0)

    def ch_f(c):
        return jnp.where(c < n_c, lax.rem(scan_of(c) + n_x, n_c), c)

    def ch_r(c):
        return c

    def do_of(ch):
        return jnp.minimum(ch, n_x - 1)

    def body(qkf_ref, vf_ref, laf_ref, spf_ref, dof_ref, qkr_ref, vr_ref, lar_ref, spr_ref, dor_ref,
             dqkf_ref, dvf_ref, dlaf_ref, dqkr_ref, dvr_ref, dlar_ref, dsf, dsr):
        c = pl.program_id(0)

        @pl.when(c == 0)
        def _():
            dsf[...] = jnp.zeros_like(dsf)
            dsr[...] = jnp.zeros_like(dsr)

        @pl.when(c >= n_c)
        def _():
            for r in (dqkf_ref, dvf_ref, dlaf_ref, dqkr_ref, dvr_ref, dlar_ref):
                r[...] = jnp.zeros_like(r)

        @pl.when(c < n_c)
        def _():
            sc = B_DK ** -0.5
            results = []
            for qk_ref, v_ref, la_ref, sp_ref, do_ref, dst, reverse, ch in (
                    (qkf_ref, vf_ref, laf_ref, spf_ref, dof_ref, dsf, False, ch_f(c)),
                    (qkr_ref, vr_ref, lar_ref, spr_ref, dor_ref, dsr, True, ch_r(c))):
                mask, tri, gl, eg, eng, eend, qt, kt, ke = _gla_chunk_terms(qk_ref[...], la_ref[...], reverse)
                vbf = v_ref[...].astype(BF16)
                dob = jnp.where(ch < n_x, do_ref[...], 0.0).astype(BF16)
                qtb, ktb, keb = qt.astype(BF16), kt.astype(BF16), ke.astype(BF16)
                egl = jnp.exp(gl)
                prevs = [sp_ref[0, hh] for hh in range(B_HEADS)]
                dnews = [dst[hh] for hh in range(B_HEADS)]
                dqts, dkts, dkes, dvs, dprevs, dgls = [], [], [], [], [], []
                for hh in range(B_HEADS):
                    qth, kth, keh = _head(qtb, hh, B_DK), _head(ktb, hh, B_DK), _head(keb, hh, B_DK)
                    vh, doh = _head(vbf, hh, B_DV), _head(dob, hh, B_DV)
                    eglh = _head(egl, hh, B_DK)
                    dsb = dnews[hh].astype(BF16)
                    att = jnp.where(mask, _dot_nt(qth, kth), 0.0).astype(BF16)
                    datt = jnp.where(mask, _dot_nt(doh, vh), 0.0).astype(BF16)
                    dqts.append(_dot(datt, kth) + _dot(doh, prevs[hh].astype(BF16)))
                    dkts.append(_dot_tn(datt, qth))
                    dvs.append(_dot_tn(att, doh) + _dot_nt(keh, dsb))
                    dkes.append(_dot(vh, dsb))
                    dprevs.append(dnews[hh] * eglh + _dot_tn(doh, qth))
                    dgls.append(jnp.sum(dnews[hh] * prevs[hh], axis=0, keepdims=True) * eglh)
                dqt = jnp.concatenate(dqts, axis=1)
                dkt = jnp.concatenate(dkts, axis=1)
                dke = jnp.concatenate(dkes, axis=1)
                dgl = jnp.sum(dke * ke, axis=0, keepdims=True) + jnp.concatenate(dgls, axis=1)
                dg_hi, dg_lo = _split_bf16(dqt * qt - dkt * kt - dke * ke)
                dla = _dot_tn(tri, dg_hi) + _dot_tn(tri, dg_lo) + dgl
                dqk = jnp.concatenate([dqt * (sc * eg), dkt * eng + dke * eend], axis=1)
                results.append((dqk, jnp.concatenate(dvs, axis=1), dla, dprevs))
            for (dqk, dv, dla, dprevs), dqk_ref, dv_ref, dla_ref, dst in zip(
                    results, (dqkf_ref, dqkr_ref), (dvf_ref, dvr_ref), (dlaf_ref, dlar_ref), (dsf, dsr)):
                dqk_ref[...] = dqk
                dv_ref[...] = dv
                dla_ref[...] = dla
                for hh in range(B_HEADS):
                    dst[hh] = dprevs[hh]

    st_shape = (B_HEADS, B_DV, B_DK)

    def side(chf):
        return [pl.BlockSpec((_GC, 512), lambda c: (chf(c), qkb)),
                pl.BlockSpec((_GC, 512), lambda c: (chf(c), vb)),
                pl.BlockSpec((_GC, 512), lambda c: (chf(c), 0)),
                pl.BlockSpec((1,) + st_shape, lambda c: (scan_of(c), 0, 0, 0)),
                pl.BlockSpec((_GC, 512), lambda c: (do_of(chf(c)), 0))]

    def out_side(chf):
        return [pl.BlockSpec((_GC, 512), lambda c: (chf(c), 0)),
                pl.BlockSpec((_GC, 512), lambda c: (chf(c), 0)),
                pl.BlockSpec((_GC, 256), lambda c: (chf(c), 0))]

    shp = [jax.ShapeDtypeStruct((rows, 512), F32), jax.ShapeDtypeStruct((rows, 512), F32),
           jax.ShapeDtypeStruct((rows, 256), F32)]
    return pl.pallas_call(
        body, name=name, grid=(n_all,),
        in_specs=side(ch_f) + side(ch_r),
        out_specs=out_side(ch_f) + out_side(ch_r),
        out_shape=shp + shp,
        scratch_shapes=[pltpu.VMEM(st_shape, F32), pltpu.VMEM(st_shape, F32)],
        compiler_params=_cp("arbitrary"),
    )(zc, zc, la, spf, dosum, zc, zc, la, spr, dosum)


def _gla_out_fwd(o_a, o_f, o_r, zc, gla_g, t_len, name):
    tm = ROW_TILE
    rb = ZC_R // 512

    def body(oa_ref, of_ref, or_ref, r_ref, g_ref, cat_ref):
        osum = of_ref[...] + or_ref[...]
        g = g_ref[...]
        pieces = []
        for hh in range(B_HEADS):
            oh = osum[:, hh * B_DV:(hh + 1) * B_DV]
            rs = lax.rsqrt(jnp.mean(oh * oh, axis=-1, keepdims=True) + RMS_EPS)
            pieces.append((oh * rs) * g)
        r = r_ref[...]
        cat_ref[:, 0:512] = oa_ref[...].astype(BF16)
        cat_ref[:, 512:1024] = (jnp.concatenate(pieces, axis=1) * (r * _sigmoid(r))).astype(BF16)

    return pl.pallas_call(
        body, name=name, grid=(t_len // tm,),
        in_specs=[pl.BlockSpec((tm, 512), lambda i: (i, 0)),
                  pl.BlockSpec((tm, 512), lambda i: (i, 0)),
                  pl.BlockSpec((tm, 512), lambda i: (i, 0)),
                  pl.BlockSpec((tm, 512), lambda i: (i, rb)),
                  pl.BlockSpec((1, B_DV), lambda i: (0, 0))],
        out_specs=pl.BlockSpec((tm, D_MODEL), lambda i: (i, 0)),
        out_shape=jax.ShapeDtypeStruct((t_len, D_MODEL), BF16),
        compiler_params=_cp("parallel"),
    )(o_a, o_f, o_r, zc, gla_g)


def _gla_out_bwd(dcat, o_f, o_r, zc, gla_g, t_len, name):
    tm = ROW_TILE
    rb = ZC_R // 512

    def body(d_ref, of_ref, or_ref, r_ref, g_ref, dos_ref, dr_ref, dg_ref):
        i = pl.program_id(0)
        osum = of_ref[...] + or_ref[...]
        g = g_ref[...]
        r = r_ref[...]
        dgo = d_ref[...]
        sg = _sigmoid(r)
        dnrmg = dgo * (r * sg)
        nrms, dos = [], []
        dg_acc = jnp.zeros((1, B_DV), F32)
        for hh in range(B_HEADS):
            oh = osum[:, hh * B_DV:(hh + 1) * B_DV]
            rs = lax.rsqrt(jnp.mean(oh * oh, axis=-1, keepdims=True) + RMS_EPS)
            nrm = oh * rs
            dn = dnrmg[:, hh * B_DV:(hh + 1) * B_DV]
            dg_acc = dg_acc + jnp.sum(dn * nrm, axis=0, keepdims=True)
            dnn = dn * g
            dos.append(rs * (dnn - nrm * jnp.mean(dnn * nrm, axis=-1, keepdims=True)))
            nrms.append(nrm * g)
        dos_ref[...] = jnp.concatenate(dos, axis=1)
        dr_ref[...] = dgo * jnp.concatenate(nrms, axis=1) * (sg * (1.0 + r * (1.0 - sg)))

        @pl.when(i == 0)
        def _():
            dg_ref[...] = jnp.zeros_like(dg_ref)

        dg_ref[...] += dg_acc

    return pl.pallas_call(
        body, name=name, grid=(t_len // tm,),
        in_specs=[pl.BlockSpec((tm, 512), lambda i: (i, 1)),
                  pl.BlockSpec((tm, 512), lambda i: (i, 0)),
                  pl.BlockSpec((tm, 512), lambda i: (i, 0)),
                  pl.BlockSpec((tm, 512), lambda i: (i, rb)),
                  pl.BlockSpec((1, B_DV), lambda i: (0, 0))],
        out_specs=[pl.BlockSpec((tm, 512), lambda i: (i, 0)),
                   pl.BlockSpec((tm, 512), lambda i: (i, 0)),
                   pl.BlockSpec((1, B_DV), lambda i: (0, 0))],
        out_shape=[jax.ShapeDtypeStruct((t_len, 512), F32), jax.ShapeDtypeStruct((t_len, 512), F32),
                   jax.ShapeDtypeStruct((1, B_DV), F32)],
        compiler_params=_cp("arbitrary"),
    )(dcat, o_f, o_r, zc, gla_g)


def _mix_prep(dq, dkv, dqk_f, dqk_r, dv_f, dv_r, d_r, dla_f, dla_r, zc, wg2, bias2, cs, t_len, name):
    rows = zc.shape[0]
    tm = ROW_TILE
    n_x = t_len // tm
    gb = ZC_G // 128

    def xrow(i):
        return jnp.minimum(i, n_x - 1)

    def body(dq_ref, dkv_ref, dqkf_ref, dqkr_ref, dvf_ref, dvr_ref, dr_ref, dlaf_ref, dlar_ref, zg_ref, wg_ref,
             b_ref, cs_ref, dz_ref, dwg_ref, db_ref):
        i = pl.program_id(0)
        is_x = i < n_x
        cos = cs_ref[:, 0:128]
        sin = cs_ref[:, 128:256]
        cosq = jnp.concatenate([cos] * 4, axis=1)
        sinq = jnp.concatenate([sin] * 4, axis=1)
        dqv = jnp.where(is_x, dq_ref[...], 0.0)
        dz_ref[:, ZC_Q:ZC_QK] = (dqv * cosq + _swap16(dqv * sinq)).astype(BF16)
        dz_ref[:, ZC_QK:ZC_V] = (dqkf_ref[...] + dqkr_ref[...]).astype(BF16)
        dz_ref[:, ZC_V:ZC_R] = (dvf_ref[...] + dvr_ref[...]).astype(BF16)
        dz_ref[:, ZC_R:ZC_KV] = jnp.where(is_x, dr_ref[...], 0.0).astype(BF16)
        dk = dkv_ref[:, 0:128]
        dz_ref[:, ZC_KV:ZC_KV + 128] = (dk * cos + _swap16(dk * sin)).astype(BF16)
        dz_ref[:, ZC_KV + 128:ZC_G] = dkv_ref[:, 128:256].astype(BF16)
        zgb = zg_ref[...].astype(BF16)
        wg = wg_ref[...]
        pre = _dot(zgb, wg) + b_ref[...]
        dla = jnp.concatenate([dlaf_ref[...], dlar_ref[...]], axis=1)
        dpre = dla * (_sigmoid(-pre) / B_GATE_NORM)
        dpb = dpre.astype(BF16)
        dz_ref[:, ZC_G:ZC_W] = _dot_nt(dpb, wg).astype(BF16)

        @pl.when(i == 0)
        def _():
            dwg_ref[...] = jnp.zeros_like(dwg_ref)
            db_ref[...] = jnp.zeros_like(db_ref)

        dwg_ref[...] += _dot_tn(zgb, dpb)
        db_ref[...] += jnp.sum(dpre, axis=0, keepdims=True)

    return pl.pallas_call(
        body, name=name, grid=(rows // tm,),
        in_specs=[pl.BlockSpec((tm, 512), lambda i: (xrow(i), 0)),
                  pl.BlockSpec((tm, 256), lambda i: (i, 0)),
                  pl.BlockSpec((tm, 512), lambda i: (i, 0)),
                  pl.BlockSpec((tm, 512), lambda i: (i, 0)),
                  pl.BlockSpec((tm, 512), lambda i: (i, 0)),
                  pl.BlockSpec((tm, 512), lambda i: (i, 0)),
                  pl.BlockSpec((tm, 512), lambda i: (xrow(i), 0)),
                  pl.BlockSpec((tm, 256), lambda i: (i, 0)),
                  pl.BlockSpec((tm, 256), lambda i: (i, 0)),
                  pl.BlockSpec((tm, 128), lambda i: (i, gb)),
                  pl.BlockSpec((128, 512), lambda i: (0, 0)),
                  pl.BlockSpec((1, 512), lambda i: (0, 0)),
                  pl.BlockSpec((tm, 256), lambda i: (i, 0))],
        out_specs=[pl.BlockSpec((tm, ZC_W), lambda i: (i, 0)),
                   pl.BlockSpec((128, 512), lambda i: (0, 0)),
                   pl.BlockSpec((1, 512), lambda i: (0, 0))],
        out_shape=[jax.ShapeDtypeStruct((rows, ZC_W), BF16),
                   jax.ShapeDtypeStruct((128, 512), F32),
                   jax.ShapeDtypeStruct((1, 512), F32)],
        compiler_params=_cp("arbitrary"),
    )(dq, dkv, dqk_f, dqk_r, dv_f, dv_r, d_r, dla_f, dla_r, zc, wg2, bias2, cs)


def _gate_weights(w_a2_f, b_a_f, w_a2_b, b_a_b):
    wg2 = jnp.zeros((128, 512), F32)
    wg2 = wg2.at[0:B_GATE_RANK, 0:256].set(w_a2_f).at[B_GATE_RANK:2 * B_GATE_RANK, 256:512].set(w_a2_b)
    bias2 = jnp.concatenate([b_a_f, b_a_b]).reshape(1, 512)
    return wg2.astype(BF16), bias2


_WIN_PERM = ((0, 512), (768, 1280), (1280, 1792), (1792, 2304), (512, 768), (2304, 2336))


def _w_in_to_cat(w_in_full):
    parts = [w_in_full[:, a:b] for a, b in _WIN_PERM]
    parts.append(jnp.zeros((w_in_full.shape[0], ZC_W - PROJ_DIM), w_in_full.dtype))
    return jnp.concatenate(parts, axis=1)


def _cat_to_w_in(d_wcat):
    return jnp.concatenate([d_wcat[:, ZC_Q:ZC_QK], d_wcat[:, ZC_KV:ZC_G], d_wcat[:, ZC_QK:ZC_KV],
                            d_wcat[:, ZC_G:ZC_G + 2 * B_GATE_RANK]], axis=1)


def _mixer_ab_forward(x1, g3, mods, wcat, wg2, bias2, sink, gla_g, w_out, cs, t_len, l_ctx, n_x, pace):
    h = _rms_mod_fwd(x1, g3, mods, 1, n_x, BF16, "mix0_mod")
    zc, la = _proj_fwd(h, wcat, wg2, bias2, cs, "mix0_proj")
    dep = pace("proj", zc)
    o_a = _attn_fwd(zc, sink + dep[0, 0], t_len, l_ctx, "mix0_attn")
    dep = pace("attn", o_a)
    o_f, o_r, spf, spr = _gla_fwd(zc, la, dep, t_len, l_ctx, "mix0_gla")
    dep = pace("gla", o_f)
    cat = _gla_out_fwd(o_a, o_f, o_r, zc, gla_g + dep[0:1, 0:1], t_len, "mix0_glaout")
    x2, y = _matmul_resid(cat, w_out, x1, mods, 5, 1.0, n_x, t_len, "mix0_out")
    return x2, (x1, h, zc, la, o_a, o_f, o_r, spf, spr, cat, y)


def _mixer_ab_backward(dx2, saved, g3, mods, wcat, wg2, bias2, sink, gla_g, w_out, cs, t_len, l_ctx, n_x):
    x1, h, zc, la, o_a, o_f, o_r, spf, spr, cat, y = saved
    rows = x1.shape[0]
    tm = ROW_TILE
    dy, dgate = _gate_dy(dx2, y, mods, 5, 1.0, n_x, t_len, "mix0_dy")
    dcat = _matmul_nt(dy, w_out, "mix0_dcat")
    tk = _token_tile(t_len)
    d_wout = _matmul_tn(
        cat, dy, pl.BlockSpec((tk, D_MODEL), lambda n, k: (k, 0)), pl.BlockSpec((tk, D_MODEL), lambda n, k: (k, 0)),
        (D_MODEL, D_MODEL), pl.BlockSpec((D_MODEL, D_MODEL), lambda n, k: (0, 0)), (1, t_len // tk), "mix0_dwout")
    dos, d_r, d_glag = _gla_out_bwd(dcat, o_f, o_r, zc, gla_g, t_len, "mix0_dglaout")
    dqk_f, dv_f, dla_f, dqk_r, dv_r, dla_r = _gla_bwd(zc, la, spf, spr, dos, t_len, l_ctx, "mix0_dgla")
    dq, dkv, dsink = _attn_bwd(zc, sink, o_a, dcat, t_len, l_ctx, "mix0_dattn")
    dzc, dwg2, dbias2 = _mix_prep(dq, dkv, dqk_f, dqk_r, dv_f, dv_r, d_r, dla_f, dla_r, zc, wg2, bias2, cs, t_len,
                                  "mix0_prep")
    tk = _token_tile(rows)
    d_wcat = _matmul_tn(
        h, dzc, pl.BlockSpec((tk, D_MODEL), lambda n, k: (k, 0)), pl.BlockSpec((tk, ZC_W), lambda n, k: (k, 0)),
        (D_MODEL, ZC_W), pl.BlockSpec((D_MODEL, ZC_W), lambda n, k: (0, 0)), (1, rows // tk), "mix0_dwin")
    pairs = [(dzc, pl.BlockSpec((tm, ZC_W), lambda i: (i, 0)), wcat, pl.BlockSpec((D_MODEL, ZC_W), lambda i: (0, 0)))]
    dx1, stats = _bwd_dx(pairs, x1, dx2, t_len // tm, g3, mods, 1, n_x, "mix0_dx")
    return dx1, stats, dgate, d_wcat, dwg2, dbias2, dsink, d_glag, d_wout


_PT = 256
_PH = 16


def _pool_band(n, t_len, w, transpose):
    shape = (_PT, _PT + 2 * _PH)
    a = n * _PT + lax.broadcasted_iota(jnp.int32, shape, 0)
    b = n * _PT - _PH + lax.broadcasted_iota(jnp.int32, shape, 1)
    t, s = (b, a) if transpose else (a, b)
    lo = jnp.maximum(t - w // 2, 0)
    hi = jnp.minimum(t + (w - w // 2), t_len)
    inside = (s >= lo) & (s < hi) & (t >= 0) & (t < t_len)
    mean = jnp.where(inside, 1.0 / (hi - lo).astype(F32), 0.0)
    return mean - jnp.where(s == t, 1.0, 0.0)


def _pool_halo(p_ref, c_ref, n_ref):
    return jnp.concatenate([p_ref[_PT - _PH:_PT, :], c_ref[...], n_ref[0:_PH, :]], axis=0)


def _pool_specs(t_len):
    nb = t_len // _PT
    return [pl.BlockSpec((_PT, D_MODEL), lambda n: (jnp.maximum(n - 1, 0), 0)),
            pl.BlockSpec((_PT, D_MODEL), lambda n: (n, 0)),
            pl.BlockSpec((_PT, D_MODEL), lambda n: (jnp.minimum(n + 1, nb - 1), 0))], nb


def _pool_fwd(h, wp, pscale, x1, mods, t_len, name):
    halo_specs, nb = _pool_specs(t_len)

    def body(hp_ref, hc_ref, hn_ref, w_ref, ps_ref, x_ref, m_ref, x2_ref, pooled_ref, ypre_ref):
        n = pl.program_id(0)
        hcat = _pool_halo(hp_ref, hc_ref, hn_ref)
        ys = []
        for gi, w in enumerate(POOL_WINDOWS):
            cols = slice(gi * POOL_GROUP, (gi + 1) * POOL_GROUP)
            pooled = _dot_hi(_pool_band(n, t_len, w, False), hcat[:, cols]).astype(BF16)
            pooled_ref[:, cols] = pooled
            ys.append(_dot(pooled, w_ref[gi]))
        ypre = jnp.concatenate(ys, axis=1)
        ypre_ref[...] = ypre
        x2_ref[...] = x_ref[...] + m_ref[0, 5:6, :] * (ypre * ps_ref[...])

    return pl.pallas_call(
        body, name=name, grid=(nb,),
        in_specs=halo_specs + [pl.BlockSpec((4, POOL_GROUP, POOL_GROUP), lambda n: (0, 0, 0)),
                               pl.BlockSpec((1, D_MODEL), lambda n: (0, 0)),
                               pl.BlockSpec((_PT, D_MODEL), lambda n: (n, 0)),
                               pl.BlockSpec((1, N_MOD, D_MODEL), lambda n: (0, 0, 0))],
        out_specs=[pl.BlockSpec((_PT, D_MODEL), lambda n: (n, 0))] * 3,
        out_shape=[jax.ShapeDtypeStruct((t_len, D_MODEL), F32), jax.ShapeDtypeStruct((t_len, D_MODEL), BF16),
                   jax.ShapeDtypeStruct((t_len, D_MODEL), F32)],
        compiler_params=_cp("parallel"),
    )(h, h, h, wp, pscale, x1, mods)


def _pool_bwd_a(dx2, ypre, wp, pscale, mods, t_len, name):
    nb = t_len // _PT

    def body(d_ref, y_ref, w_ref, ps_ref, m_ref, dyp_ref, dpl_ref, dgate_ref, dps_ref):
        n = pl.program_id(0)
        dv = d_ref[...]
        ypre = y_ref[...]
        ps = ps_ref[...]
        dy = dv * m_ref[0, 5:6, :]
        dyp = (dy * ps).astype(BF16)
        dyp_ref[...] = dyp
        for gi in range(len(POOL_WINDOWS)):
            cols = slice(gi * POOL_GROUP, (gi + 1) * POOL_GROUP)
            dpl_ref[:, cols] = _dot_nt(dyp[:, cols], w_ref[gi])

        @pl.when(n == 0)
        def _():
            dgate_ref[...] = jnp.zeros_like(dgate_ref)
            dps_ref[...] = jnp.zeros_like(dps_ref)

        dgate_ref[...] += jnp.sum(dv * (ypre * ps), axis=0, keepdims=True)
        dps_ref[...] += jnp.sum(dy * ypre, axis=0, keepdims=True)

    return pl.pallas_call(
        body, name=name, grid=(nb,),
        in_specs=[pl.BlockSpec((_PT, D_MODEL), lambda n: (n, 0)),
                  pl.BlockSpec((_PT, D_MODEL), lambda n: (n, 0)),
                  pl.BlockSpec((4, POOL_GROUP, POOL_GROUP), lambda n: (0, 0, 0)),
                  pl.BlockSpec((1, D_MODEL), lambda n: (0, 0)),
                  pl.BlockSpec((1, N_MOD, D_MODEL), lambda n: (0, 0, 0))],
        out_specs=[pl.BlockSpec((_PT, D_MODEL), lambda n: (n, 0)),
                   pl.BlockSpec((_PT, D_MODEL), lambda n: (n, 0)),
                   pl.BlockSpec((1, D_MODEL), lambda n: (0, 0)),
                   pl.BlockSpec((1, D_MODEL), lambda n: (0, 0))],
        out_shape=[jax.ShapeDtypeStruct((t_len, D_MODEL), BF16), jax.ShapeDtypeStruct((t_len, D_MODEL), F32),
                   jax.ShapeDtypeStruct((1, D_MODEL), F32), jax.ShapeDtypeStruct((1, D_MODEL), F32)],
        compiler_params=_cp("arbitrary"),
    )(dx2, ypre, wp, pscale, mods)


def _pool_bwd_dx(dpl, x1, dx2, g3, mods, t_len, name):
    halo_specs, nb = _pool_specs(t_len)

    def body(dp_ref, dc_ref, dn_ref, x_ref, d_ref, g_ref, m_ref, dx_ref, acc_ref):
        n = pl.program_id(0)
        dcat = _pool_halo(dp_ref, dc_ref, dn_ref)
        dhs = []
        for gi, w in enumerate(POOL_WINDOWS):
            cols = slice(gi * POOL_GROUP, (gi + 1) * POOL_GROUP)
            dhs.append(_dot_hi(_pool_band(n, t_len, w, True), dcat[:, cols]))
        dh = jnp.concatenate(dhs, axis=1)
        g = g_ref[1:2, :]
        scale = m_ref[0, 4:5, :]
        dx = _rms_mod_bwd_tail(dh, x_ref[...], g, scale, 0, acc_ref, n == 0)
        dx_ref[...] = d_ref[...] + dx

    return pl.pallas_call(
        body, name=name, grid=(nb,),
        in_specs=halo_specs + [pl.BlockSpec((_PT, D_MODEL), lambda n: (n, 0)),
                               pl.BlockSpec((_PT, D_MODEL), lambda n: (n, 0)),
                               pl.BlockSpec((3, D_MODEL), lambda n: (0, 0)),
                               pl.BlockSpec((1, N_MOD, D_MODEL), lambda n: (0, 0, 0))],
        out_specs=[pl.BlockSpec((_PT, D_MODEL), lambda n: (n, 0)),
                   pl.BlockSpec((2, 3, D_MODEL), lambda n: (0, 0, 0))],
        out_shape=[jax.ShapeDtypeStruct((t_len, D_MODEL), F32), jax.ShapeDtypeStruct((2, 3, D_MODEL), F32)],
        compiler_params=_cp("arbitrary"),
    )(dpl, dpl, dpl, x1, dx2, g3, mods)


def _mixer_pool_forward(x1, g3, mods, wp, pscale, t_len):
    h = _rms_mod_fwd(x1, g3, mods, 1, t_len // ROW_TILE, F32, "mix1_mod")
    x2, pooled, ypre = _pool_fwd(h, wp, pscale, x1, mods, t_len, "mix1_pool")
    return x2, (x1, pooled, ypre)


def _mixer_pool_backward(dx2, saved, g3, mods, wp, pscale, t_len):
    x1, pooled, ypre = saved
    tm = ROW_TILE
    dyp, dpl, dgate, dps = _pool_bwd_a(dx2, ypre, wp, pscale, mods, t_len, "mix1_da")
    d_wp = _matmul_tn(
        pooled, dyp, pl.BlockSpec((tm, POOL_GROUP), lambda g, k: (k, g)),
        pl.BlockSpec((tm, POOL_GROUP), lambda g, k: (k, g)),
        (4, POOL_GROUP, POOL_GROUP), pl.BlockSpec((1, POOL_GROUP, POOL_GROUP), lambda g, k: (g, 0, 0)),
        (4, t_len // tm), "mix1_dwp")
    dx1, stats = _pool_bwd_dx(dpl, x1, dx2, g3, mods, t_len, "mix1_dx")
    return dx1, stats, dgate, dps, d_wp


def _final_loss(x3, final_g, target, name):
    t_len = x3.shape[0]
    tm = ROW_TILE

    def body(x_ref, g_ref, t_ref, dx_ref, loss_ref, dg_ref):
        i = pl.program_id(0)
        xv = x_ref[...]
        g = g_ref[...]
        r = lax.rsqrt(jnp.mean(xv * xv, axis=-1, keepdims=True) + RMS_EPS)
        xhat = xv * r
        err = xhat * g - t_ref[...]
        part = 0.5 * jnp.sum(jnp.mean(err * err, axis=-1, keepdims=True), axis=0, keepdims=True)
        dy = err * (1.0 / D_MODEL)

        @pl.when(i == 0)
        def _():
            loss_ref[...] = jnp.zeros_like(loss_ref)
            dg_ref[...] = jnp.zeros_like(dg_ref)

        loss_ref[...] += jnp.broadcast_to(part, (1, 128))
        dg_ref[...] += jnp.sum(dy * xhat, axis=0, keepdims=True)
        dxh = dy * g
        dx_ref[...] = r * (dxh - xhat * jnp.mean(dxh * xhat, axis=-1, keepdims=True))

    return pl.pallas_call(
        body, name=name, grid=(t_len // tm,),
        in_specs=[pl.BlockSpec((tm, D_MODEL), lambda i: (i, 0)),
                  pl.BlockSpec((1, D_MODEL), lambda i: (0, 0)),
                  pl.BlockSpec((tm, D_MODEL), lambda i: (i, 0))],
        out_specs=[pl.BlockSpec((tm, D_MODEL), lambda i: (i, 0)),
                   pl.BlockSpec((1, 128), lambda i: (0, 0)),
                   pl.BlockSpec((1, D_MODEL), lambda i: (0, 0))],
        out_shape=[jax.ShapeDtypeStruct((t_len, D_MODEL), F32), jax.ShapeDtypeStruct((1, 128), F32),
                   jax.ShapeDtypeStruct((1, D_MODEL), F32)],
        compiler_params=_cp("arbitrary"),
    )(x3, final_g, target)


_CROWS = 16


def _adaln_fwd(c16, w_mod, bias_k, name):
    n_l, _, cols = w_mod.shape

    def body(c_ref, w_ref, b_ref, o_ref):
        cv = c_ref[...]
        sc = (cv * _sigmoid(cv)).astype(BF16)
        o_ref[0] = _dot(sc, w_ref[0].astype(BF16)) + b_ref[0]

    return pl.pallas_call(
        body, name=name, grid=(n_l,),
        in_specs=[pl.BlockSpec((_CROWS, D_MODEL), lambda l: (0, 0)),
                  pl.BlockSpec((1, D_MODEL, cols), lambda l: (l, 0, 0)),
                  pl.BlockSpec((1, 1, cols), lambda l: (l, 0, 0))],
        out_specs=pl.BlockSpec((1, _CROWS, cols), lambda l: (l, 0, 0)),
        out_shape=jax.ShapeDtypeStruct((n_l, _CROWS, cols), F32),
        compiler_params=_cp("parallel"),
    )(c16, w_mod, bias_k)


def _adaln_bwd(c16, d16, w_mod, dmmc_k, name):
    n_l, _, cols = w_mod.shape

    def body(c_ref, d_ref, w_ref, dm_ref, gw_ref, cp_ref):
        layer = pl.program_id(0)
        cv = c_ref[...]
        gw_ref[0] = _dot_tn_hi(cv * _sigmoid(cv), d_ref[0])

        @pl.when(layer == 0)
        def _():
            cp_ref[...] = jnp.sum(w_ref[0] * dm_ref[...], axis=1, keepdims=True)

    return pl.pallas_call(
        body, name=name, grid=(n_l,),
        in_specs=[pl.BlockSpec((_CROWS, D_MODEL), lambda l: (0, 0)),
                  pl.BlockSpec((1, _CROWS, cols), lambda l: (l, 0, 0)),
                  pl.BlockSpec((1, D_MODEL, cols), lambda l: (0, 0, 0)),
                  pl.BlockSpec((1, cols), lambda l: (0, 0))],
        out_specs=[pl.BlockSpec((1, D_MODEL, cols), lambda l: (l, 0, 0)),
                   pl.BlockSpec((D_MODEL, 1), lambda l: (0, 0))],
        out_shape=[jax.ShapeDtypeStruct((n_l, D_MODEL, cols), F32), jax.ShapeDtypeStruct((D_MODEL, 1), F32)],
        compiler_params=_cp("arbitrary"),
    )(c16, d16, w_mod, dmmc_k)


def _cctx_grad(cparts, c_ctx2, name):
    def body(p_ref, c_ref, o_ref):
        tot = ((p_ref[0] + p_ref[2]) + p_ref[4]) + p_ref[6]
        cv = c_ref[...]
        sg = _sigmoid(cv)
        o_ref[...] = tot * (sg * (1.0 + cv * (1.0 - sg)))

    return pl.pallas_call(
        body, name=name, out_shape=jax.ShapeDtypeStruct((8, 128), F32),
        in_specs=[pl.BlockSpec(memory_space=pltpu.VMEM), pl.BlockSpec(memory_space=pltpu.VMEM)],
        out_specs=pl.BlockSpec(memory_space=pltpu.VMEM),
    )(cparts, c_ctx2)


def _sum_devices(ga, name):
    def body(g_ref, o_ref):
        acc = g_ref[0]
        for d in range(1, N_DEV):
            acc = acc + g_ref[d]
        o_ref[...] = acc

    return pl.pallas_call(
        body, name=name, out_shape=jax.ShapeDtypeStruct(ga.shape[1:], F32),
        in_specs=[pl.BlockSpec(memory_space=pltpu.VMEM)], out_specs=pl.BlockSpec(memory_space=pltpu.VMEM),
    )(ga)


def _place():
    return lax.axis_index("x"), lax.axis_index("y"), lax.axis_index("c")


def _flip(a, d):
    return 1 - a if d else a


_CHIP_FLIPS = ((1, 0), (0, 1), (1, 1))


def _allgather_small(v, name):
    r, cc = v.shape

    def body(v_ref, out_ref, send_sems, recv_sems, local_sem):
        x, y, c = _place()
        me = 4 * x + 2 * y + c
        mine = pltpu.make_async_copy(v_ref, out_ref.at[me], local_sem)
        mine.start()
        sends = []
        for k in range(1, N_DEV):
            peer = (_flip(x, (k >> 2) & 1), _flip(y, (k >> 1) & 1), _flip(c, k & 1))
            cp = pltpu.make_async_remote_copy(src_ref=v_ref, dst_ref=out_ref.at[me], send_sem=send_sems.at[k - 1],
                                              recv_sem=recv_sems.at[k - 1], device_id=peer, device_id_type=MESH)
            cp.start()
            sends.append(cp)
        for k in range(1, N_DEV):
            px, py, pc = _flip(x, (k >> 2) & 1), _flip(y, (k >> 1) & 1), _flip(c, k & 1)
            pltpu.make_async_remote_copy(src_ref=v_ref, dst_ref=out_ref.at[4 * px + 2 * py + pc],
                                         send_sem=send_sems.at[k - 1], recv_sem=recv_sems.at[k - 1],
                                         device_id=(px, py, pc), device_id_type=MESH).wait_recv()
        for cp in sends:
            cp.wait_send()
        mine.wait()

    return pl.pallas_call(
        body, name=name, out_shape=jax.ShapeDtypeStruct((N_DEV, r, cc), F32),
        in_specs=[pl.BlockSpec(memory_space=pltpu.VMEM)], out_specs=pl.BlockSpec(memory_space=pltpu.VMEM),
        scratch_shapes=[pltpu.SemaphoreType.DMA((N_DEV - 1,)), pltpu.SemaphoreType.DMA((N_DEV - 1,)),
                        pltpu.SemaphoreType.DMA],
        compiler_params=pltpu.CompilerParams(vmem_limit_bytes=VMEM_LIMIT_BYTES),
    )(v)


_HBM_SPEC = pl.BlockSpec(memory_space=pltpu.HBM)
_SEM_SPEC = pl.BlockSpec(memory_space=pltpu.SEMAPHORE)
_EFFECT = pltpu.SideEffectType.DATAFLOW_SIDE_EFFECTING


def _in_hbm(a):
    return pltpu.with_memory_space_constraint(a, pltpu.HBM)


def _gather_start(arrs, groups, after, name):
    n, n_g = len(arrs), len(groups)

    def body(*refs):
        ins, zones = refs[:n], refs[n:2 * n]
        sems = refs[2 * n + 1:2 * n + 1 + 2 * n_g]
        token = refs[2 * n + 1 + 2 * n_g + 2 * n]
        x, y, c = _place()
        k_me = 2 * x + y
        for g, members in enumerate(groups):
            for t, a in enumerate(members):
                for j, (dx, dy) in enumerate(_CHIP_FLIPS):
                    pltpu.make_async_remote_copy(
                        src_ref=ins[a], dst_ref=zones[a].at[k_me], send_sem=sems[2 * g].at[3 * t + j],
                        recv_sem=sems[2 * g + 1].at[3 * t + j], device_id=(_flip(x, dx), _flip(y, dy), c),
                        device_id_type=MESH).start()
        token[...] = jnp.zeros_like(token)

    k_own = 2 * lax.axis_index("x") + lax.axis_index("y")
    zones = [lax.dynamic_update_slice(lax.empty((N_CHIPS,) + a.shape, a.dtype), a[None], (k_own,) + (0,) * a.ndim)
             for a in arrs]
    sem_shapes = []
    for members in groups:
        sem_shapes += [pltpu.SemaphoreType.DMA((3 * len(members),))] * 2
    outs = pl.pallas_call(
        body, name=name,
        out_shape=sem_shapes + [pltpu.HBM(a.shape, a.dtype) for a in arrs]
        + [pltpu.HBM(z.shape, z.dtype) for z in zones] + [jax.ShapeDtypeStruct((8, 128), F32)],
        in_specs=[_HBM_SPEC] * (2 * n) + [pl.BlockSpec(memory_space=pl.ANY)],
        out_specs=[_SEM_SPEC] * (2 * n_g) + [_HBM_SPEC] * (2 * n) + [pl.BlockSpec(memory_space=pltpu.VMEM)],
        input_output_aliases={i: 2 * n_g + i for i in range(2 * n)},
        compiler_params=pltpu.CompilerParams(has_side_effects=_EFFECT),
    )(*[_in_hbm(a) for a in arrs], *[_in_hbm(z) for z in zones], after)
    sems = outs[:2 * n_g]
    thru = outs[2 * n_g:2 * n_g + n]
    zones = outs[2 * n_g + n:2 * n_g + 2 * n]
    return [(sems[2 * g], sems[2 * g + 1]) for g in range(n_g)], thru, zones, outs[-1]


def _gather_wait(shards, zones, send_sems, recv_sems, after, name):
    m = len(shards)

    def body(*refs):
        ins, zs = refs[:m], refs[m:2 * m]
        ssem, rsem = refs[2 * m], refs[2 * m + 1]
        x, y, c = _place()
        for t in range(m):
            for j, (dx, dy) in enumerate(_CHIP_FLIPS):
                px, py = _flip(x, dx), _flip(y, dy)
                cp = pltpu.make_async_remote_copy(
                    src_ref=ins[t], dst_ref=zs[t].at[2 * px + py], send_sem=ssem.at[3 * t + j],
                    recv_sem=rsem.at[3 * t + j], device_id=(px, py, c), device_id_type=MESH)
                cp.wait_send()
                cp.wait_recv()

    after = list(after) if isinstance(after, (list, tuple)) else [after]
    outs = pl.pallas_call(
        body, name=name,
        out_shape=[pltpu.HBM(a.shape, a.dtype) for a in list(shards) + list(zones)],
        in_specs=[_HBM_SPEC] * (2 * m) + [_SEM_SPEC, _SEM_SPEC] + [pl.BlockSpec(memory_space=pl.ANY)] * len(after),
        out_specs=[_HBM_SPEC] * (2 * m),
        input_output_aliases={i: i for i in range(2 * m)},
        compiler_params=pltpu.CompilerParams(has_side_effects=_EFFECT),
    )(*shards, *zones, send_sems, recv_sems, *after)
    return outs[m:]


def _scatter_start(arrs, name):
    n = len(arrs)

    def body(*refs):
        ins, lands = refs[:n], refs[n:2 * n]
        ssem, rsem = refs[2 * n], refs[2 * n + 1]
        token = refs[2 * n + 2 + 2 * n]
        x, y, c = _place()
        for a in range(n):
            for j, (dx, dy) in enumerate(_CHIP_FLIPS):
                px, py = _flip(x, dx), _flip(y, dy)
                pltpu.make_async_remote_copy(
                    src_ref=ins[a].at[2 * px + py], dst_ref=lands[a].at[j], send_sem=ssem.at[3 * a + j],
                    recv_sem=rsem.at[3 * a + j], device_id=(px, py, c), device_id_type=MESH).start()
        token[...] = jnp.zeros_like(token)

    lands = [lax.empty((3,) + a.shape[1:], a.dtype) for a in arrs]
    outs = pl.pallas_call(
        body, name=name,
        out_shape=[pltpu.SemaphoreType.DMA((3 * n,))] * 2 + [pltpu.HBM(a.shape, a.dtype) for a in arrs]
        + [pltpu.HBM(z.shape, z.dtype) for z in lands] + [jax.ShapeDtypeStruct((8, 128), F32)],
        in_specs=[_HBM_SPEC] * (2 * n),
        out_specs=[_SEM_SPEC] * 2 + [_HBM_SPEC] * (2 * n) + [pl.BlockSpec(memory_space=pltpu.VMEM)],
        input_output_aliases={i: 2 + i for i in range(2 * n)},
        compiler_params=pltpu.CompilerParams(has_side_effects=_EFFECT),
    )(*[_in_hbm(a) for a in arrs], *[_in_hbm(z) for z in lands])
    return outs[0], outs[1], outs[2:2 + n], outs[2 + n:2 + 2 * n], outs[-1]


def _scatter_wait(arrs, lands, send_sems, recv_sems, after, name):
    n = len(arrs)

    def body(*refs):
        ins, lz = refs[:n], refs[n:2 * n]
        ssem, rsem = refs[2 * n], refs[2 * n + 1]
        x, y, c = _place()
        for a in range(n):
            for j, (dx, dy) in enumerate(_CHIP_FLIPS):
                px, py = _flip(x, dx), _flip(y, dy)
                cp = pltpu.make_async_remote_copy(
                    src_ref=ins[a].at[2 * px + py], dst_ref=lz[a].at[j], send_sem=ssem.at[3 * a + j],
                    recv_sem=rsem.at[3 * a + j], device_id=(px, py, c), device_id_type=MESH)
                cp.wait_send()
                cp.wait_recv()

    outs = pl.pallas_call(
        body, name=name,
        out_shape=[pltpu.HBM(a.shape, a.dtype) for a in list(arrs) + list(lands)],
        in_specs=[_HBM_SPEC] * (2 * n) + [_SEM_SPEC, _SEM_SPEC, pl.BlockSpec(memory_space=pl.ANY)],
        out_specs=[_HBM_SPEC] * (2 * n),
        input_output_aliases={i: i for i in range(2 * n)},
        compiler_params=pltpu.CompilerParams(has_side_effects=_EFFECT),
    )(*arrs, *lands, send_sems, recv_sems, after)
    return outs[:n], outs[n:]


def _swap_sibling(arrs, name):
    n = len(arrs)

    def body(*refs):
        ins, outs = refs[:n], refs[n:2 * n]
        send_sems, recv_sems = refs[2 * n:]
        x, y, c = _place()
        sends = []
        for a in range(n):
            cp = pltpu.make_async_remote_copy(src_ref=ins[a], dst_ref=outs[a], send_sem=send_sems.at[a],
                                              recv_sem=recv_sems.at[a], device_id=(x, y, 1 - c), device_id_type=MESH)
            cp.start()
            sends.append(cp)
        for cp in sends:
            cp.wait()

    any_spec = pl.BlockSpec(memory_space=pl.ANY)
    return pl.pallas_call(
        body, name=name,
        out_shape=[jax.ShapeDtypeStruct(a.shape, a.dtype) for a in arrs],
        in_specs=[any_spec] * n, out_specs=[any_spec] * n,
        scratch_shapes=[pltpu.SemaphoreType.DMA((n,)), pltpu.SemaphoreType.DMA((n,))],
    )(*arrs)


def _row_tile(rows, cols):
    for tr in (1024, 512, 256, 128, 64, 32, 16, 8):
        if rows % tr == 0 and tr * cols * 4 <= (1 << 20):
            return tr
    return rows


def _partial_sum(g_full, recv, k_idx, name):
    _, r, c = g_full.shape
    tr = _row_tile(r, c)

    def body(k_ref, g_ref, r_ref, o_ref):
        del k_ref
        acc = g_ref[0].astype(F32)
        for j in range(3):
            acc = acc + r_ref[j].astype(F32)
        o_ref[...] = acc

    return pl.pallas_call(
        body, name=name,
        grid_spec=pltpu.PrefetchScalarGridSpec(
            num_scalar_prefetch=1, grid=(r // tr,),
            in_specs=[pl.BlockSpec((1, tr, c), lambda i, k: (k[0], i, 0)),
                      pl.BlockSpec((3, tr, c), lambda i, k: (0, i, 0))],
            out_specs=pl.BlockSpec((tr, c), lambda i, k: (i, 0))),
        out_shape=jax.ShapeDtypeStruct((r, c), F32),
        compiler_params=_cp("parallel"),
    )(k_idx, g_full, recv)


def _adamw(w3, parts, m3, v3, layer, prev, name):
    n_l, r, c = w3.shape
    tr = _row_tile(r, c)
    n_i = r // tr
    n_p = len(parts)
    c1 = 1.0 - ADAM_B1 ** ADAM_STEP
    c2 = 1.0 - ADAM_B2 ** ADAM_STEP
    stacked = [isinstance(p, tuple) for p in parts]

    def body(*refs):
        w_ref, m_ref, v_ref = refs[0:3]
        g_refs = refs[3:3 + n_p]
        go_ref, d_ref, mo_ref, vo_ref = refs[-4:]
        g = None
        for p in range(n_p):
            term = g_refs[p][0] if stacked[p] else g_refs[p][...]
            g = term if g is None else g + term
        w = w_ref[0]
        m = ADAM_B1 * m_ref[0] + (1.0 - ADAM_B1) * g
        v = ADAM_B2 * v_ref[0] + (1.0 - ADAM_B2) * (g * g)
        m_hat = m / c1
        v_hat = v / c2
        go_ref[0] = g
        d_ref[0] = -ADAM_LR * (m_hat / (jnp.sqrt(v_hat) + ADAM_EPS) + ADAM_WD * w)
        mo_ref[0] = m
        vo_ref[0] = v

    blk = pl.BlockSpec((1, tr, c), lambda i: (layer, i, 0))
    in_specs = [blk, blk, blk]
    args = [w3, m3, v3]
    for part in parts:
        if isinstance(part, tuple):
            in_specs.append(pl.BlockSpec((1, tr, c), functools.partial(lambda idx, i: (idx, i, 0), part[1])))
            args.append(part[0])
        else:
            in_specs.append(pl.BlockSpec((tr, c), lambda i: (i, 0)))
            args.append(part)
    aliases = {}
    if prev is not None:
        in_specs += [pl.BlockSpec(memory_space=pl.ANY)] * 4
        aliases = {len(args) + q: q for q in range(4)}
        args += list(prev)
    shp = jax.ShapeDtypeStruct((n_l, r, c), F32)
    return pl.pallas_call(
        body, name=name, grid=(n_i,), in_specs=in_specs, out_specs=[blk] * 4, out_shape=[shp] * 4,
        input_output_aliases=aliases, compiler_params=_cp("parallel"),
    )(*args)


_SMALL_W = 4096
_PACK_ROWS = 352
_N9 = N_MOD * D_MODEL


def _flat_pad(parts, total):
    flat = jnp.concatenate([p.reshape(-1) for p in parts])
    return jnp.concatenate([flat, jnp.zeros((total - flat.shape[0],), F32)])


def kernel(x, c, ctx, c_ctx, w_mod, b_mod, norm_g, ffn1_wi, ffn1_wo, ffn2_wi, ffn2_wo, w_in, w_a2_f, b_a_f, w_a2_b, b_a_b, sink, gla_g, w_out, w_pool, pool_scale, final_g, loss_target, m_c_ctx, m_w_mod, m_b_mod, m_norm_g, m_ffn1_wi, m_ffn1_wo, m_ffn2_wi, m_ffn2_wo, m_w_in, m_w_a2_f, m_b_a_f, m_w_a2_b, m_b_a_b, m_sink, m_gla_g, m_w_out, m_w_pool, m_pool_scale, m_final_g, v_c_ctx, v_w_mod, v_b_mod, v_norm_g, v_ffn1_wi, v_ffn1_wo, v_ffn2_wi, v_ffn2_wo, v_w_in, v_w_a2_f, v_b_a_f, v_w_a2_b, v_b_a_b, v_sink, v_gla_g, v_w_out, v_w_pool, v_pool_scale, v_final_g):
    t_len, l_ctx = x.shape[1], ctx.shape[1]
    tm = ROW_TILE
    pad = (-(t_len + l_ctx)) % tm
    rows0 = t_len + l_ctx + pad
    n_x = t_len // tm
    xi, yi, ci = _place()
    k_me = 2 * xi + yi
    me = 4 * xi + 2 * yi + ci
    mod_cols = w_mod.shape[2]
    n_grp = len(POOL_WINDOWS)

    small_w = _flat_pad([norm_g, w_a2_f, w_a2_b, pool_scale], _SMALL_W).reshape(_SMALL_W // 128, 128)
    shards = [ffn1_wi[0], ffn1_wi[1], ffn1_wo[0], ffn1_wo[1], ffn2_wi[0], ffn2_wi[1], ffn2_wo[0], ffn2_wo[1],
              w_in[0], w_out[0], w_pool[0].reshape(n_grp * w_pool.shape[2], POOL_GROUP)]

    c_all = _allgather_small(c.reshape(8, 128), "gather_cond").reshape(N_DEV, D_MODEL)
    c16 = jnp.concatenate([c_all, c_ctx[None], jnp.zeros((_CROWS - N_DEV - 1, D_MODEL), F32)], axis=0)
    bias_k = lax.dynamic_slice(b_mod, (0, k_me * mod_cols), (2, mod_cols)).reshape(2, 1, mod_cols)
    mm_k = _adaln_fwd(c16, w_mod, bias_k, "adaln_fwd")
    mm_all = _allgather_small(mm_k.reshape(-1, 128), "gather_mod")

    send_src = [s.astype(BF16) for s in shards] + [small_w]
    groups = ([11, 0], [2], [8, 9], [4], [6], [1], [3], [10, 5], [7])
    started = {}

    def gather_start(g, after):
        members = groups[g]
        sems, thru, zones, token = _gather_start([send_src[a] for a in members], (tuple(range(len(members))),),
                                                 after, "gather_start_%d" % g)
        started[g] = (sems[0], thru, zones)
        return token

    def gather_wait(g, after):
        (ssem, rsem), thru, zones = started[g]
        return dict(zip(groups[g], _gather_wait(thru, zones, ssem, rsem, after, "gather_wait_%d" % g)))

    tok = gather_start(0, mm_all)
    mm_all = mm_all.reshape(N_DEV, 2, _CROWS, mod_cols)
    mm_full = jnp.concatenate([mm_all[2 * k] for k in range(N_CHIPS)], axis=-1)
    mm_x = lax.dynamic_index_in_dim(mm_full, me, axis=1, keepdims=False)
    mm_c = mm_full[:, N_DEV]
    mods = [jnp.stack([mm_x[l].reshape(N_MOD, D_MODEL), mm_c[l].reshape(N_MOD, D_MODEL)]) + tok[0:1, 0:1]
            for l in range(2)]
    cs = _rope_tables(t_len, rows0)
    xcat = jnp.concatenate([x[0], ctx[0], jnp.zeros((pad, D_MODEL), F32)], axis=0)
    gathered = gather_wait(0, [mods[0], cs, xcat])
    sw = gathered[11].reshape(N_CHIPS, _SMALL_W)
    ng_n = norm_g.size
    a2_n = w_a2_f.size
    norm_g_full = jnp.concatenate([sw[k, :ng_n].reshape(norm_g.shape) for k in range(N_CHIPS)], axis=-1)
    w_a2_f_full = jnp.concatenate([sw[k, ng_n:ng_n + a2_n].reshape(w_a2_f.shape[1:]) for k in range(N_CHIPS)], axis=-1)
    w_a2_b_full = jnp.concatenate(
        [sw[k, ng_n + a2_n:ng_n + 2 * a2_n].reshape(w_a2_b.shape[1:]) for k in range(N_CHIPS)], axis=-1)
    pscale_full = jnp.concatenate(
        [sw[k, ng_n + 2 * a2_n:ng_n + 2 * a2_n + pool_scale.size] for k in range(N_CHIPS)]).reshape(1, D_MODEL)
    wg2, bias2 = _gate_weights(w_a2_f_full, b_a_f[0], w_a2_b_full, b_a_b[0])
    gla_g2 = gla_g.reshape(1, B_DV)
    final_g2 = final_g.reshape(1, D_MODEL)

    g3 = [norm_g_full[0], norm_g_full[1]]

    w1i, w1o, w2i, w2o = [None, None], [None, None], [None, None], [None, None]
    w1i[0] = gathered[0]
    mods_a = mods[0] + gather_start(1, w1i[0])[0:1, 0:1] + gather_start(2, w1i[0])[0:1, 0:1]
    x1, sv_a1, w1o[0] = _ffn_forward(xcat, g3[0], mods_a, 0, w1i[0],
                                     lambda s: (gather_wait(1, s)[2], gather_start(3, s)), n_x, "l0_ffn1")
    gathered = gather_wait(2, x1)
    w_in_full = jnp.concatenate([gathered[8][k] for k in range(N_CHIPS)], axis=1)
    wcat = _w_in_to_cat(w_in_full)
    w_out_full = gathered[9].reshape(D_MODEL, D_MODEL)
    mods_a = mods[0] + gather_start(4, x1)[0:1, 0:1]
    pace_group = {"proj": 5, "attn": 6, "gla": 7}
    x2, sv_am = _mixer_ab_forward(x1, g3[0], mods_a, wcat, wg2, bias2, sink[0], gla_g2, w_out_full, cs,
                                  t_len, l_ctx, n_x, lambda tag, res_: gather_start(pace_group[tag], res_))
    mods_a = mods[0] + gather_start(8, x2)[0:1, 0:1]
    w2i[0], w2o[0] = gather_wait(3, x2)[4], gather_wait(4, x2)[6]
    x3, sv_a2, _ = _ffn_forward(x2, g3[0], mods_a, 2, w2i[0], lambda s: (w2o[0], None), n_x, "l0_ffn2")
    w1i[1], w1o[1] = gather_wait(5, x3)[1], gather_wait(6, x3)[3]
    x4, sv_b1, _ = _ffn_forward(x3, g3[1], mods[1], 0, w1i[1], lambda s: (w1o[1], None), n_x, "l1_ffn1")
    gathered = gather_wait(7, x4)
    w2i[1] = gathered[5]
    wp_full = gathered[10].reshape(N_CHIPS, n_grp, -1, POOL_GROUP).transpose(1, 0, 2, 3).reshape(
        n_grp, POOL_GROUP, POOL_GROUP)
    x5, sv_bm = _mixer_pool_forward(x4, g3[1], mods[1], wp_full, pscale_full, t_len)
    x6, sv_b2, w2o[1] = _ffn_forward(x5, g3[1], mods[1], 2, w2i[1], lambda s: (gather_wait(8, s)[7], None), n_x,
                                     "l1_ffn2")
    dx6, loss_part, d_final_g = _final_loss(x6, final_g2, loss_target[0], "final_loss")
    loss = lax.psum(loss_part[0, 0], ("x", "y", "c"))

    sent = []

    def sender(weight, layer):
        def send(grad, tag):
            nm = "%s_%s_%d" % (weight, tag, layer)
            ssem, rsem, thru, lands, token = _scatter_start([grad], "scatter_start_" + nm)
            sent.append((nm, weight + "_" + tag if tag else weight, layer, thru, lands, ssem, rsem))
            return token[0:1, 0:1]
        return send

    dx5, st_b2, dg_b2 = _ffn_backward(dx6, sv_b2, g3[1], mods[1], 2, w2i[1], w2o[1], n_x, sender("ffn2", 1),
                                      "l1_ffn2_b")
    dx4, st_bm, dg_bm, d_pscale, d_wp = _mixer_pool_backward(dx5, sv_bm, g3[1], mods[1], wp_full, pscale_full, t_len)
    d_wp4 = d_wp.reshape(n_grp, N_CHIPS, -1, POOL_GROUP).transpose(1, 0, 2, 3).reshape(N_CHIPS, -1, POOL_GROUP)
    mods1 = mods[1] + sender("w_pool", 0)(d_wp4, "")
    dx3, st_b1, dg_b1 = _ffn_backward(dx4, sv_b1, g3[1], mods1, 0, w1i[1], w1o[1], n_x, sender("ffn1", 1),
                                      "l1_ffn1_b")
    dx2, st_a2, dg_a2 = _ffn_backward(dx3, sv_a2, g3[0], mods[0], 2, w2i[0], w2o[0], n_x, sender("ffn2", 0),
                                      "l0_ffn2_b")
    dx1, st_am, dg_am, d_wcat, d_wg2, d_bias2, d_sink, d_glag, d_wout = _mixer_ab_backward(
        dx2, sv_am, g3[0], mods[0], wcat, wg2, bias2, sink[0], gla_g2, w_out_full, cs, t_len, l_ctx, n_x)
    d_w_in4 = _cat_to_w_in(d_wcat).reshape(D_MODEL, N_CHIPS, -1).transpose(1, 0, 2)
    mods0 = mods[0] + sender("w_in", 0)(d_w_in4, "") + sender("w_out", 0)(d_wout.reshape(N_CHIPS, -1, D_MODEL), "")
    dx0, st_a1, dg_a1 = _ffn_backward(dx1, sv_a1, g3[0], mods0, 0, w1i[0], w1o[0], n_x, sender("ffn1", 0),
                                      "l0_ffn1_b")
    grad_x = dx0[:t_len][None]

    def mod_row(st1, dg1, stm, dgm, st2, dg2, s):
        return jnp.concatenate([st1[s, 0], st1[s, 1], dg1[s, 0], stm[s, 0], stm[s, 1], dgm[s, 0],
                                st2[s, 0], st2[s, 1], dg2[s, 0]])

    dg_bm2 = jnp.concatenate([dg_bm, jnp.zeros_like(dg_bm)], axis=0)[:, None, :]
    d_mm_x0 = mod_row(st_a1, dg_a1, st_am, dg_am, st_a2, dg_a2, 0)
    d_mm_x1 = mod_row(st_b1, dg_b1, st_bm, dg_bm2, st_b2, dg_b2, 0)
    d_mm_c0 = mod_row(st_a1, dg_a1, st_am, dg_am, st_a2, dg_a2, 1)
    d_norm_g = jnp.stack([jnp.stack([st[0, 2] + st[1, 2] for st in (st_a1, st_am, st_a2)]),
                          jnp.stack([st[0, 2] + st[1, 2] for st in (st_b1, st_bm, st_b2)])])
    rk = B_GATE_RANK
    pack = _flat_pad([d_mm_x0, d_mm_x1, d_mm_c0, d_norm_g, d_bias2, d_wg2[0:rk, 0:256], d_wg2[rk:2 * rk, 256:512],
                      d_sink[:, 0], jnp.zeros((120,), F32), d_glag, d_pscale, d_final_g],
                     _PACK_ROWS * 128).reshape(_PACK_ROWS, 128)
    pack_all = _allgather_small(pack, "gather_small_grads")
    tot = _sum_devices(pack_all, "sum_small_grads").reshape(-1)
    rows_all = pack_all.reshape(N_DEV, -1)
    o = 3 * _N9
    g_norm_g_full = tot[o:o + 6 * D_MODEL].reshape(2, 3, D_MODEL)
    o += 6 * D_MODEL
    g_bias2 = tot[o:o + 512]
    o += 512
    g_w_a2_f_full = tot[o:o + rk * 256].reshape(rk, 256)
    o += rk * 256
    g_w_a2_b_full = tot[o:o + rk * 256].reshape(rk, 256)
    o += rk * 256
    g_sink = tot[o:o + A_HEADS]
    o += 128
    g_gla_g = tot[o:o + B_DV]
    o += B_DV
    g_pscale_full = tot[o:o + D_MODEL]
    o += D_MODEL
    g_final_g = tot[o:o + D_MODEL]
    d_mmc_tot = tot[2 * _N9:3 * _N9]
    g_b_mod = jnp.stack([tot[0:_N9] + d_mmc_tot, tot[_N9:2 * _N9]])

    zrows = jnp.zeros((_CROWS - N_DEV - 1, _N9), F32)
    d16 = jnp.stack([jnp.concatenate([rows_all[:, 0:_N9], d_mmc_tot[None], zrows], axis=0),
                     jnp.concatenate([rows_all[:, _N9:2 * _N9], jnp.zeros((1, _N9), F32), zrows], axis=0)])
    d16_k = lax.dynamic_slice(d16, (0, 0, k_me * mod_cols), (2, _CROWS, mod_cols))
    dmmc_k = lax.dynamic_slice(d_mmc_tot, (k_me * mod_cols,), (mod_cols,)).reshape(1, mod_cols)
    g_w_mod, c_part = _adaln_bwd(c16, d16_k, w_mod, dmmc_k, "adaln_bwd")
    c_parts = _allgather_small(c_part.reshape(8, 128), "gather_cctx")
    g_c_ctx = _cctx_grad(c_parts, c_ctx.reshape(8, 128), "cctx_grad").reshape(D_MODEL)

    def small(w, g, m, v, shape3, nm):
        return [o_.reshape(w.shape) for o_ in _adamw(w.reshape(shape3), [g.reshape(shape3[1:])],
                                                    m.reshape(shape3), v.reshape(shape3), 0, None, "adamw_" + nm)]

    def own(a, axis, size):
        return lax.dynamic_slice_in_dim(a, k_me * size, size, axis=axis)

    res = {}
    res["c_ctx"] = small(c_ctx, g_c_ctx, m_c_ctx, v_c_ctx, (1, 8, 128), "c_ctx")
    upd = _adamw(w_mod, [(g_w_mod, 1)], m_w_mod, v_w_mod, 1, None, "adamw_w_mod_1")
    res["w_mod"] = _adamw(w_mod, [(g_w_mod, 0)], m_w_mod, v_w_mod, 0, upd, "adamw_w_mod_0")
    res["b_mod"] = small(b_mod, g_b_mod, m_b_mod, v_b_mod, (1, 2, _N9), "b_mod")
    res["norm_g"] = small(norm_g, own(g_norm_g_full, 2, norm_g.shape[2]), m_norm_g, v_norm_g,
                          (1, 6, norm_g.shape[2]), "norm_g")
    res["w_a2_f"] = small(w_a2_f, own(g_w_a2_f_full, 1, w_a2_f.shape[2]), m_w_a2_f, v_w_a2_f,
                          (1, rk, w_a2_f.shape[2]), "w_a2_f")
    res["b_a_f"] = small(b_a_f, g_bias2[0:256], m_b_a_f, v_b_a_f, (1, 1, 256), "b_a_f")
    res["w_a2_b"] = small(w_a2_b, own(g_w_a2_b_full, 1, w_a2_b.shape[2]), m_w_a2_b, v_w_a2_b,
                          (1, rk, w_a2_b.shape[2]), "w_a2_b")
    res["b_a_b"] = small(b_a_b, g_bias2[256:512], m_b_a_b, v_b_a_b, (1, 1, 256), "b_a_b")
    res["sink"] = small(sink, g_sink, m_sink, v_sink, (1, 1, A_HEADS), "sink")
    res["gla_g"] = small(gla_g, g_gla_g, m_gla_g, v_gla_g, (1, 1, B_DV), "gla_g")
    res["pool_scale"] = small(pool_scale, own(g_pscale_full, 0, pool_scale.shape[1]), m_pool_scale, v_pool_scale,
                              (1, 1, pool_scale.shape[1]), "pool_scale")
    res["final_g"] = small(final_g, g_final_g, m_final_g, v_final_g, (1, 8, 128), "final_g")

    def as3(a):
        n_l = a.shape[0] if a.ndim == 3 else 1
        return a.reshape(n_l, -1, a.shape[-1])

    big_w = {"ffn1_wi": (ffn1_wi, m_ffn1_wi, v_ffn1_wi), "ffn1_wo": (ffn1_wo, m_ffn1_wo, v_ffn1_wo),
             "ffn2_wi": (ffn2_wi, m_ffn2_wi, v_ffn2_wi), "ffn2_wo": (ffn2_wo, m_ffn2_wo, v_ffn2_wo),
             "w_in": (w_in, m_w_in, v_w_in), "w_out": (w_out, m_w_out, v_w_out), "w_pool": (w_pool, m_w_pool, v_w_pool)}
    k_idx = k_me.reshape(1).astype(jnp.int32)
    chain = res["final_g"][0]
    for lo, hi in ((0, 2), (2, 5), (5, 7), (7, 9), (9, 11)):
        partial = []
        for nm, wname, layer, thru, lands, ssem, rsem in sent[lo:hi]:
            mine, recv = _scatter_wait(thru, lands, ssem, rsem, chain, "scatter_wait_" + nm)
            partial.append(_partial_sum(mine[0], recv[0], k_idx, "partial_sum_" + nm))
        other = _swap_sibling(partial, "swap_partials_%d" % lo)
        for (nm, wname, layer, _, _, _, _), p, q in zip(sent[lo:hi], partial, other):
            w, m, v = big_w[wname]
            res[wname] = _adamw(as3(w), [p, q], as3(m), as3(v), layer, res.get(wname),
                                "adamw_%s_%d" % (wname, layer))
            chain = res[wname][3]
    for wname, (w, _, _) in big_w.items():
        res[wname] = [o_.reshape(w.shape) for o_ in res[wname]]

    names = ["c_ctx", "w_mod", "b_mod", "norm_g", "ffn1_wi", "ffn1_wo", "ffn2_wi", "ffn2_wo", "w_in", "w_a2_f",
             "b_a_f", "w_a2_b", "b_a_b", "sink", "gla_g", "w_out", "w_pool", "pool_scale", "final_g"]
    outs = [loss, grad_x]
    for field in range(4):
        outs += [res[nm][field] for nm in names]
    return tuple(outs)
```

```python
import functools

import jax
import jax.numpy as jnp
import numpy as np
from jax import lax
from jax.experimental import pallas as pl
from jax.experimental.pallas import tpu as pltpu

F32 = jnp.float32
BF16 = jnp.bfloat16

D_MODEL = 1024
N_MOD = 9
D_FF = 2816
RMS_EPS = 1e-6
A_HEADS = 8
A_KV_HEADS = 2
A_HEAD_DIM = 64
WINDOW = 128
ROPE_BASE = 10000.0
GRID_W = 64
B_HEADS = 4
B_DK = 64
B_DV = 128
B_GATE_RANK = 16
B_GATE_NORM = 16.0
B_CHUNK = 64
POOL_WINDOWS = (2, 4, 8, 16)
POOL_GROUP = D_MODEL // len(POOL_WINDOWS)
PROJ_DIM = 2336

ADAM_LR = 0.001
ADAM_B1 = 0.9
ADAM_B2 = 0.999
ADAM_EPS = 1e-08
ADAM_WD = 0.01
ADAM_STEP = 10

N_CHIPS = 4
N_DEV = 8
ROW_TILE = 512
VMEM_LIMIT_BYTES = 56 * 1024 * 1024
MESH = pl.DeviceIdType.MESH

ZC_Q, ZC_QK, ZC_V, ZC_R, ZC_KV, ZC_G, ZC_W = 0, 512, 1024, 1536, 2048, 2304, 2432


def _cp(*sem):
    return pltpu.CompilerParams(dimension_semantics=sem if sem else None, vmem_limit_bytes=VMEM_LIMIT_BYTES)


def _dot(a, b):
    return jnp.dot(a, b, preferred_element_type=F32)


def _dot_nt(a, b):
    return lax.dot_general(a, b, (((1,), (1,)), ((), ())), preferred_element_type=F32)


def _dot_tn(a, b):
    return lax.dot_general(a, b, (((0,), (0,)), ((), ())), preferred_element_type=F32)


def _dot_hi(a, b):
    return jnp.dot(a, b, preferred_element_type=F32, precision=lax.Precision.HIGHEST)


def _dot_tn_hi(a, b):
    return lax.dot_general(a, b, (((0,), (0,)), ((), ())), preferred_element_type=F32,
                           precision=lax.Precision.HIGHEST)


def _sigmoid(x):
    return 1.0 / (1.0 + jnp.exp(-x))


def _stream_of(i, n_x):
    return jnp.where(i >= n_x, 1, 0)


def _rms_mod_fwd(x, g3, mods, j, n_x, out_dtype, name):
    rows = x.shape[0]
    tm = ROW_TILE
    n_i = rows // tm

    def body(x_ref, g_ref, m_ref, o_ref):
        xv = x_ref[...]
        r = lax.rsqrt(jnp.mean(xv * xv, axis=-1, keepdims=True) + RMS_EPS)
        g = g_ref[j:j + 1, :]
        shift = m_ref[0, 3 * j:3 * j + 1, :]
        scale = m_ref[0, 3 * j + 1:3 * j + 2, :]
        o_ref[...] = (((xv * r) * g) * (1.0 + scale) + shift).astype(out_dtype)

    return pl.pallas_call(
        body, name=name, grid=(n_i,),
        in_specs=[pl.BlockSpec((tm, D_MODEL), lambda i: (i, 0)),
                  pl.BlockSpec((3, D_MODEL), lambda i: (0, 0)),
                  pl.BlockSpec((1, N_MOD, D_MODEL), lambda i: (_stream_of(i, n_x), 0, 0))],
        out_specs=pl.BlockSpec((tm, D_MODEL), lambda i: (i, 0)),
        out_shape=jax.ShapeDtypeStruct((rows, D_MODEL), out_dtype),
        compiler_params=_cp("parallel"),
    )(x, g3, mods)


def _rms_mod_bwd_tail(dh, xv, g, scale, stream, acc_ref, first):
    r = lax.rsqrt(jnp.mean(xv * xv, axis=-1, keepdims=True) + RMS_EPS)
    xhat = xv * r
    t1 = jnp.sum(dh, axis=0, keepdims=True)
    t2 = jnp.sum(dh * xhat, axis=0, keepdims=True)
    stats = jnp.concatenate([t1, t2 * g, t2 * (1.0 + scale)], axis=0)

    @pl.when(first)
    def _():
        acc_ref[...] = jnp.zeros_like(acc_ref)

    acc_ref[pl.ds(stream, 1)] += stats[None]
    dxh = dh * (g * (1.0 + scale))
    return r * (dxh - xhat * jnp.mean(dxh * xhat, axis=-1, keepdims=True))


def _ffn_up(x, g3, mods, jmod, n_x, w4, name):
    rows = x.shape[0]
    h = w4.shape[2]
    tm = ROW_TILE
    n_i = rows // tm

    def body(x_ref, g_ref, m_ref, wa_ref, wu_ref, hn_ref, au_ref, s_ref):
        xv = x_ref[...]
        r = lax.rsqrt(jnp.mean(xv * xv, axis=-1, keepdims=True) + RMS_EPS)
        g = g_ref[jmod:jmod + 1, :]
        shift = m_ref[0, 3 * jmod:3 * jmod + 1, :]
        scale = m_ref[0, 3 * jmod + 1:3 * jmod + 2, :]
        hv = (((xv * r) * g) * (1.0 + scale) + shift).astype(BF16)

        @pl.when(pl.program_id(0) == 0)
        def _():
            hn_ref[...] = hv

        a = _dot(hv, wa_ref[0])
        u = _dot(hv, wu_ref[0])
        sg = _sigmoid(a)
        silu = a * sg
        au_ref[0] = (u * (sg * (1.0 + a * (1.0 - sg)))).astype(BF16)
        au_ref[1] = silu.astype(BF16)
        s_ref[...] = (silu * u).astype(BF16)

    return pl.pallas_call(
        body, name=name, grid=(2, n_i),
        in_specs=[pl.BlockSpec((tm, D_MODEL), lambda j, i: (i, 0)),
                  pl.BlockSpec((3, D_MODEL), lambda j, i: (0, 0)),
                  pl.BlockSpec((1, N_MOD, D_MODEL), lambda j, i: (_stream_of(i, n_x), 0, 0)),
                  pl.BlockSpec((1, D_MODEL, h), lambda j, i: (j, 0, 0)),
                  pl.BlockSpec((1, D_MODEL, h), lambda j, i: (j + 2, 0, 0))],
        out_specs=[pl.BlockSpec((tm, D_MODEL), lambda j, i: (jnp.where(j == 0, i, n_i - 1), 0)),
                   pl.BlockSpec((2, tm, h), lambda j, i: (0, i, j)),
                   pl.BlockSpec((tm, h), lambda j, i: (i, j))],
        out_shape=[jax.ShapeDtypeStruct((rows, D_MODEL), BF16),
                   jax.ShapeDtypeStruct((2, rows, 2 * h), BF16),
                   jax.ShapeDtypeStruct((rows, 2 * h), BF16)],
        compiler_params=_cp("arbitrary", "arbitrary"),
    )(x, g3, mods, w4, w4)


def _matmul_resid(a, w, xres, mods, gate_idx, coef, n_x, rows, name):
    k = a.shape[1]
    tm = ROW_TILE
    n_i = rows // tm

    def body(a_ref, w_ref, x_ref, m_ref, o_ref, f_ref):
        f = _dot(a_ref[...], w_ref[...])
        gate = m_ref[0, gate_idx:gate_idx + 1, :]
        f_ref[...] = f
        o_ref[...] = x_ref[...] + (coef * gate) * f

    return pl.pallas_call(
        body, name=name, grid=(n_i,),
        in_specs=[pl.BlockSpec((tm, k), lambda i: (i, 0)),
                  pl.BlockSpec((k, D_MODEL), lambda i: (0, 0)),
                  pl.BlockSpec((tm, D_MODEL), lambda i: (i, 0)),
                  pl.BlockSpec((1, N_MOD, D_MODEL), lambda i: (_stream_of(i, n_x), 0, 0))],
        out_specs=[pl.BlockSpec((tm, D_MODEL), lambda i: (i, 0)),
                   pl.BlockSpec((tm, D_MODEL), lambda i: (i, 0))],
        out_shape=[jax.ShapeDtypeStruct((rows, D_MODEL), F32),
                   jax.ShapeDtypeStruct((rows, D_MODEL), F32)],
        compiler_params=_cp("parallel"),
    )(a, w, xres, mods)


def _gate_dy(dout, f, mods, gate_idx, coef, n_x, rows, name):
    tm = ROW_TILE
    n_i = rows // tm

    def body(d_ref, f_ref, m_ref, dy_ref, acc_ref):
        i = pl.program_id(0)
        dv = d_ref[...]
        gate = m_ref[0, gate_idx:gate_idx + 1, :]
        dy_ref[...] = (dv * (coef * gate)).astype(BF16)

        @pl.when(i == 0)
        def _():
            acc_ref[...] = jnp.zeros_like(acc_ref)

        part = coef * jnp.sum(dv * f_ref[...], axis=0, keepdims=True)
        acc_ref[pl.ds(_stream_of(i, n_x), 1)] += part[None]

    return pl.pallas_call(
        body, name=name, grid=(n_i,),
        in_specs=[pl.BlockSpec((tm, D_MODEL), lambda i: (i, 0)),
                  pl.BlockSpec((tm, D_MODEL), lambda i: (i, 0)),
                  pl.BlockSpec((1, N_MOD, D_MODEL), lambda i: (_stream_of(i, n_x), 0, 0))],
        out_specs=[pl.BlockSpec((tm, D_MODEL), lambda i: (i, 0)),
                   pl.BlockSpec((2, 1, D_MODEL), lambda i: (0, 0, 0))],
        out_shape=[jax.ShapeDtypeStruct((rows, D_MODEL), BF16),
                   jax.ShapeDtypeStruct((2, 1, D_MODEL), F32)],
        compiler_params=_cp("arbitrary"),
    )(dout, f, mods)


def _ffn_bwd_dz(dout, f, mods, gate_idx, coef, n_x, wo2, au, name):
    rows = dout.shape[0]
    h = wo2.shape[1]
    tm = ROW_TILE
    n_i = rows // tm

    def body(d_ref, f_ref, m_ref, wo_ref, au_ref, dy_ref, dz_ref, acc_ref):
        j, i = pl.program_id(0), pl.program_id(1)
        dv = d_ref[...]
        gate = m_ref[0, gate_idx:gate_idx + 1, :]
        dyb = (dv * (coef * gate)).astype(BF16)

        @pl.when((j == 0) & (i == 0))
        def _():
            acc_ref[...] = jnp.zeros_like(acc_ref)

        @pl.when(j == 0)
        def _():
            dy_ref[...] = dyb
            part = coef * jnp.sum(dv * f_ref[...], axis=0, keepdims=True)
            acc_ref[pl.ds(_stream_of(i, n_x), 1)] += part[None]

        ds = _dot_nt(dyb, wo_ref[0])
        dz_ref[0] = (ds * au_ref[0].astype(F32)).astype(BF16)
        dz_ref[1] = (ds * au_ref[1].astype(F32)).astype(BF16)

    return pl.pallas_call(
        body, name=name, grid=(2, n_i),
        in_specs=[pl.BlockSpec((tm, D_MODEL), lambda j, i: (i, 0)),
                  pl.BlockSpec((tm, D_MODEL), lambda j, i: (jnp.where(j == 0, i, n_i - 1), 0)),
                  pl.BlockSpec((1, N_MOD, D_MODEL), lambda j, i: (_stream_of(i, n_x), 0, 0)),
                  pl.BlockSpec((1, h, D_MODEL), lambda j, i: (j, 0, 0)),
                  pl.BlockSpec((2, tm, h), lambda j, i: (0, i, j))],
        out_specs=[pl.BlockSpec((tm, D_MODEL), lambda j, i: (jnp.where(j == 0, i, n_i - 1), 0)),
                   pl.BlockSpec((2, tm, h), lambda j, i: (0, i, j)),
                   pl.BlockSpec((2, 1, D_MODEL), lambda j, i: (0, 0, 0))],
        out_shape=[jax.ShapeDtypeStruct((rows, D_MODEL), BF16),
                   jax.ShapeDtypeStruct((2, rows, 2 * h), BF16),
                   jax.ShapeDtypeStruct((2, 1, D_MODEL), F32)],
        compiler_params=_cp("arbitrary", "arbitrary"),
    )(dout, f, mods, wo2, au)


def _token_tile(rows):
    for tk in (2048, 1536, 1024):
        if rows % tk == 0:
            return tk
    return ROW_TILE


def _matmul_tn(a, b, a_spec, b_spec, out_shape, out_spec, grid, name):
    nd_a = len(a_spec.block_shape)
    nd_b = len(b_spec.block_shape)
    nd_o = len(out_spec.block_shape)
    k_axis = len(grid) - 1
    n_k = grid[k_axis]

    def body(a_ref, b_ref, o_ref, acc_ref):
        av = a_ref[(0,) * (nd_a - 2)]
        bv = b_ref[(0,) * (nd_b - 2)]
        part = _dot_tn(av, bv)
        k = pl.program_id(k_axis)

        @pl.when(k == 0)
        def _():
            acc_ref[...] = part

        @pl.when(k > 0)
        def _():
            acc_ref[...] += part

        @pl.when(k == n_k - 1)
        def _():
            o_ref[(0,) * (nd_o - 2)] = acc_ref[...].astype(BF16)

    return pl.pallas_call(
        body, name=name, grid=grid, in_specs=[a_spec, b_spec], out_specs=out_spec,
        out_shape=jax.ShapeDtypeStruct(out_shape, BF16),
        scratch_shapes=[pltpu.VMEM(tuple(out_spec.block_shape[-2:]), F32)],
        compiler_params=_cp(*(("arbitrary",) * len(grid))),
    )(a, b)


def _bwd_dx(pairs, x, dres, dres_tiles, g3, mods, j, n_x, name):
    rows = x.shape[0]
    tm = ROW_TILE
    n_i = rows // tm
    n_p = len(pairs)
    nds = [(len(p[1].block_shape), len(p[3].block_shape)) for p in pairs]

    def body(*refs):
        dz_refs = refs[0:2 * n_p:2]
        w_refs = refs[1:2 * n_p:2]
        x_ref, dres_ref, g_ref, m_ref, dx_ref, acc_ref = refs[2 * n_p:]
        i = pl.program_id(0)
        dh = None
        for p in range(n_p):
            dzv = dz_refs[p][(0,) * (nds[p][0] - 2)]
            wv = w_refs[p][(0,) * (nds[p][1] - 2)]
            part = _dot_nt(dzv, wv)
            dh = part if dh is None else dh + part
        g = g_ref[j:j + 1, :]
        scale = m_ref[0, 3 * j + 1:3 * j + 2, :]
        dx = _rms_mod_bwd_tail(dh, x_ref[...], g, scale, _stream_of(i, n_x), acc_ref, i == 0)
        dres_v = jnp.where(i < dres_tiles, dres_ref[...], 0.0)
        dx_ref[...] = dres_v + dx

    in_specs, args = [], []
    for dz, dz_spec, w, w_spec in pairs:
        in_specs += [dz_spec, w_spec]
        args += [dz, w]
    in_specs += [pl.BlockSpec((tm, D_MODEL), lambda i: (i, 0)),
                 pl.BlockSpec((tm, D_MODEL), lambda i: (jnp.minimum(i, dres_tiles - 1), 0)),
                 pl.BlockSpec((3, D_MODEL), lambda i: (0, 0)),
                 pl.BlockSpec((1, N_MOD, D_MODEL), lambda i: (_stream_of(i, n_x), 0, 0))]
    args += [x, dres, g3, mods]
    return pl.pallas_call(
        body, name=name, grid=(n_i,), in_specs=in_specs,
        out_specs=[pl.BlockSpec((tm, D_MODEL), lambda i: (i, 0)),
                   pl.BlockSpec((2, 3, D_MODEL), lambda i: (0, 0, 0))],
        out_shape=[jax.ShapeDtypeStruct((rows, D_MODEL), F32),
                   jax.ShapeDtypeStruct((2, 3, D_MODEL), F32)],
        compiler_params=_cp("arbitrary"),
    )(*args)


def _ffn_forward(x, g3, mods, j, w4_in, w4_out_of, n_x, name):
    rows = x.shape[0]
    hn, au, s = _ffn_up(x, g3, mods, j, n_x, w4_in, name + "_up")
    w4_out, dep = w4_out_of(s)
    if dep is not None:
        mods = mods + dep[0:1, 0:1]
    wo = w4_out.reshape(D_FF, D_MODEL)
    out, f = _matmul_resid(s, wo, x, mods, 3 * j + 2, 0.5, n_x, rows, name + "_down")
    return out, (x, hn, au, s, f), w4_out


def _ffn_backward(dout, saved, g3, mods, j, w4_in, w4_out, n_x, send, name):
    x, hn, au, s, f = saved
    rows = x.shape[0]
    tm = ROW_TILE
    n_i = rows // tm
    h = w4_in.shape[2]
    wo2 = w4_out.reshape(2, h, D_MODEL)
    dy, dz, dgate = _ffn_bwd_dz(dout, f, mods, 3 * j + 2, 0.5, n_x, wo2, au, name + "_dz")
    tk = _token_tile(rows)
    n_k = rows // tk
    d_wi = _matmul_tn(
        hn, dz, pl.BlockSpec((tk, D_MODEL), lambda q, k: (k, 0)),
        pl.BlockSpec((1, tk, h), lambda q, k: (q // 2, k, q % 2)),
        (4, D_MODEL, h), pl.BlockSpec((1, D_MODEL, h), lambda q, k: (q, 0, 0)), (4, n_k), name + "_dwi")
    mods = mods + send(d_wi, "wi")
    d_wo = _matmul_tn(
        s, dy, pl.BlockSpec((tk, h), lambda n, k: (k, n)), pl.BlockSpec((tk, D_MODEL), lambda n, k: (k, 0)),
        (D_FF, D_MODEL), pl.BlockSpec((h, D_MODEL), lambda n, k: (n, 0)), (2, n_k), name + "_dwo")
    mods = mods + send(d_wo.reshape(w4_out.shape), "wo")
    pairs = [(dz, pl.BlockSpec((1, tm, h), functools.partial(lambda q, i: (q // 2, i, q % 2), q)),
              w4_in, pl.BlockSpec((1, D_MODEL, h), functools.partial(lambda q, i: (q, 0, 0), q)))
             for q in range(4)]
    dx, stats = _bwd_dx(pairs, x, dout, n_i, g3, mods, j, n_x, name + "_dx")
    return dx, stats, dgate


def _matmul_nt(a, w, name):
    rows, k = a.shape
    n = w.shape[0]
    tm = ROW_TILE

    def body(a_ref, w_ref, o_ref):
        o_ref[...] = _dot_nt(a_ref[...], w_ref[...])

    return pl.pallas_call(
        body, name=name, grid=(rows // tm,),
        in_specs=[pl.BlockSpec((tm, k), lambda i: (i, 0)), pl.BlockSpec((n, k), lambda i: (0, 0))],
        out_specs=pl.BlockSpec((tm, n), lambda i: (i, 0)),
        out_shape=jax.ShapeDtypeStruct((rows, n), F32),
        compiler_params=_cp("parallel"),
    )(a, w)


def _rope_tables(t_len, rows):
    n = A_HEAD_DIM // 4
    freqs = ROPE_BASE ** (-jnp.arange(n, dtype=F32) / n)
    t = jnp.arange(t_len)
    ang_r = (t // GRID_W).astype(F32)[:, None] * freqs
    ang_c = (t % GRID_W).astype(F32)[:, None] * freqs
    cos = jnp.concatenate([jnp.cos(ang_r), jnp.cos(ang_r), jnp.cos(ang_c), jnp.cos(ang_c)], axis=1)
    sin = jnp.concatenate([-jnp.sin(ang_r), jnp.sin(ang_r), -jnp.sin(ang_c), jnp.sin(ang_c)], axis=1)
    cos = jnp.concatenate([cos, jnp.ones((rows - t_len, A_HEAD_DIM), F32)], axis=0)
    sin = jnp.concatenate([sin, jnp.zeros((rows - t_len, A_HEAD_DIM), F32)], axis=0)
    return jnp.concatenate([cos, cos, sin, sin], axis=1)


def _swap16(x):
    n = x.shape[1]
    lane = lax.broadcasted_iota(jnp.int32, x.shape, 1)
    first = jnp.bitwise_and(lane, 16) == 0
    return jnp.where(first, pltpu.roll(x, n - 16, 1), pltpu.roll(x, 16, 1))


def _log_sigmoid(x):
    return jnp.minimum(x, 0.0) - jnp.log(1.0 + jnp.exp(-jnp.abs(x)))


def _proj_fwd(h, wcat, wg2, bias2, cs, name):
    rows = h.shape[0]
    tm = ROW_TILE

    def body(h_ref, w_ref, wg_ref, b_ref, cs_ref, zc_ref, la_ref):
        z = _dot(h_ref[...], w_ref[...])
        cos = cs_ref[:, 0:128]
        sin = cs_ref[:, 128:256]
        cosq = jnp.concatenate([cos] * 4, axis=1)
        sinq = jnp.concatenate([sin] * 4, axis=1)
        q = z[:, ZC_Q:ZC_QK]
        zc_ref[:, ZC_Q:ZC_QK] = q * cosq + _swap16(q) * sinq
        zc_ref[:, ZC_QK:ZC_KV] = z[:, ZC_QK:ZC_KV]
        kk = z[:, ZC_KV:ZC_KV + 128]
        zc_ref[:, ZC_KV:ZC_KV + 128] = kk * cos + _swap16(kk) * sin
        zc_ref[:, ZC_KV + 128:ZC_W] = z[:, ZC_KV + 128:ZC_W]
        zg = z[:, ZC_G:ZC_W]
        pre = _dot(zg.astype(BF16), wg_ref[...]) + b_ref[...]
        la_ref[...] = _log_sigmoid(pre) / B_GATE_NORM

    return pl.pallas_call(
        body, name=name, grid=(rows // tm,),
        in_specs=[pl.BlockSpec((tm, D_MODEL), lambda i: (i, 0)),
                  pl.BlockSpec((D_MODEL, ZC_W), lambda i: (0, 0)),
                  pl.BlockSpec((128, 512), lambda i: (0, 0)),
                  pl.BlockSpec((1, 512), lambda i: (0, 0)),
                  pl.BlockSpec((tm, 256), lambda i: (i, 0))],
        out_specs=[pl.BlockSpec((tm, ZC_W), lambda i: (i, 0)),
                   pl.BlockSpec((tm, 512), lambda i: (i, 0))],
        out_shape=[jax.ShapeDtypeStruct((rows, ZC_W), F32),
                   jax.ShapeDtypeStruct((rows, 512), F32)],
        compiler_params=_cp("parallel"),
    )(h, wcat, wg2, bias2, cs)


_QB = WINDOW


def _attn_specs(t_len, l_ctx):
    nb = t_len // _QB
    kvb = ZC_KV // 256
    return [pl.BlockSpec(memory_space=pltpu.SMEM),
            pl.BlockSpec((_QB, 512), lambda n: (n, 0)),
            pl.BlockSpec((_QB, 256), lambda n: (jnp.maximum(n - 1, 0), kvb)),
            pl.BlockSpec((_QB, 256), lambda n: (n, kvb)),
            pl.BlockSpec((_QB, 256), lambda n: (n + 1, kvb)),
            pl.BlockSpec((l_ctx, 256), lambda n: (t_len // l_ctx, kvb))], nb


_HEAD_PAIRS = ((0, 1), (2, 3))


def _attn_keys(kp, kc, kn, kx, g):
    hd = A_HEAD_DIM
    ks = slice(g * hd, (g + 1) * hd)
    vs = slice(128 + g * hd, 128 + (g + 1) * hd)
    kb = jnp.concatenate([kp[:, ks], kc[:, ks], kn[:, ks]], axis=0).astype(BF16)
    vb = jnp.concatenate([kp[:, vs], kc[:, vs], kn[:, vs]], axis=0).astype(BF16)
    return kb, vb, kx[:, ks].astype(BF16), kx[:, vs].astype(BF16)


def _attn_probs(n, t_len, sink_ref, qv, kb, kxb, g, rs):
    hd = A_HEAD_DIM
    qg = jnp.concatenate([qv[:, (4 * g + r) * hd:(4 * g + r + 1) * hd] for r in rs], axis=0).astype(BF16)
    qi = lax.broadcasted_iota(jnp.int32, (_QB, 3 * _QB), 0)
    kj = lax.broadcasted_iota(jnp.int32, (_QB, 3 * _QB), 1)
    kpos = n * _QB - _QB + kj
    valid = (kpos >= 0) & (kpos < t_len) & (jnp.abs(kj - _QB - qi) <= WINDOW)
    valid = jnp.concatenate([valid] * len(rs), axis=0)
    scale = hd ** -0.5
    s = jnp.where(valid, _dot_nt(qg, kb) * scale, -jnp.inf)
    sc = _dot_nt(qg, kxb) * scale
    sk = jnp.concatenate([jnp.full((_QB, 1), sink_ref[4 * g + r], F32) for r in rs], axis=0)
    m = jnp.maximum(jnp.maximum(jnp.max(s, axis=-1, keepdims=True), jnp.max(sc, axis=-1, keepdims=True)), sk)
    p = jnp.exp(s - m)
    pc = jnp.exp(sc - m)
    ps = jnp.exp(sk - m)
    inv = 1.0 / (jnp.sum(p, axis=-1, keepdims=True) + jnp.sum(pc, axis=-1, keepdims=True) + ps)
    return p, pc, ps, inv, qg


def _attn_fwd(zc, sink, t_len, l_ctx, name):
    in_specs, nb = _attn_specs(t_len, l_ctx)

    def body(sink_ref, q_ref, kp_ref, kc_ref, kn_ref, kx_ref, o_ref):
        n = pl.program_id(0)
        qv = q_ref[...]
        outs = []
        for g in range(A_KV_HEADS):
            kb, vb, kxb, vxb = _attn_keys(kp_ref[...], kc_ref[...], kn_ref[...], kx_ref[...], g)
            for rs in _HEAD_PAIRS:
                p, pc, _, inv, _ = _attn_probs(n, t_len, sink_ref, qv, kb, kxb, g, rs)
                o = (_dot(p.astype(BF16), vb) + _dot(pc.astype(BF16), vxb)) * inv
                outs += [o[i * _QB:(i + 1) * _QB] for i in range(len(rs))]
        o_ref[...] = jnp.concatenate(outs, axis=1)

    return pl.pallas_call(
        body, name=name, grid=(nb,), in_specs=in_specs,
        out_specs=pl.BlockSpec((_QB, 512), lambda n: (n, 0)),
        out_shape=jax.ShapeDtypeStruct((t_len, 512), F32),
        compiler_params=_cp("parallel"),
    )(sink, zc, zc, zc, zc, zc)


def _attn_bwd(zc, sink, o, dcat, t_len, l_ctx, name):
    rows = zc.shape[0]
    in_specs, nb = _attn_specs(t_len, l_ctx)
    in_specs = in_specs + [pl.BlockSpec((_QB, 512), lambda n: (n, 0)), pl.BlockSpec((_QB, 512), lambda n: (n, 0))]
    hd = A_HEAD_DIM
    scale = hd ** -0.5

    def body(sink_ref, q_ref, kp_ref, kc_ref, kn_ref, kx_ref, o_ref, do_ref, dq_ref, dkv_ref, dsink_ref):
        n = pl.program_id(0)

        @pl.when(n == 0)
        def _():
            dkv_ref[...] = jnp.zeros_like(dkv_ref)
            dsink_ref[...] = jnp.zeros_like(dsink_ref)

        qv = q_ref[...]
        ov = o_ref[...]
        dov = do_ref[...]
        dqs, dkbs, dvbs, dkxs, dvxs, dsinks = [], [], [], [], [], []
        for g in range(A_KV_HEADS):
            kb, vb, kxb, vxb = _attn_keys(kp_ref[...], kc_ref[...], kn_ref[...], kx_ref[...], g)
            parts = []
            for rs in _HEAD_PAIRS:
                p, pc, ps, inv, qg = _attn_probs(n, t_len, sink_ref, qv, kb, kxb, g, rs)
                og = jnp.concatenate([ov[:, (4 * g + r) * hd:(4 * g + r + 1) * hd] for r in rs], axis=0)
                dog = jnp.concatenate([dov[:, (4 * g + r) * hd:(4 * g + r + 1) * hd] for r in rs], axis=0)
                delta = jnp.sum(og * dog, axis=-1, keepdims=True)
                dogb = dog.astype(BF16)
                pn = p * inv
                pcn = pc * inv
                ds = (pn * (_dot_nt(dogb, vb) - delta) * scale).astype(BF16)
                dsc = (pcn * (_dot_nt(dogb, vxb) - delta) * scale).astype(BF16)
                dsk = (ps * inv) * (0.0 - delta)
                dqg = _dot(ds, kb) + _dot(dsc, kxb)
                dqs += [dqg[i * _QB:(i + 1) * _QB] for i in range(len(rs))]
                parts.append((_dot_tn(ds, qg), _dot_tn(pn.astype(BF16), dogb),
                              _dot_tn(dsc, qg), _dot_tn(pcn.astype(BF16), dogb)))
                for i in range(len(rs)):
                    tot = jnp.sum(dsk[i * _QB:(i + 1) * _QB], axis=0, keepdims=True)
                    dsinks.append(jnp.broadcast_to(tot, (1, 128)))
            dkbs.append(parts[0][0] + parts[1][0])
            dvbs.append(parts[0][1] + parts[1][1])
            dkxs.append(parts[0][2] + parts[1][2])
            dvxs.append(parts[0][3] + parts[1][3])
        dsink_ref[...] += jnp.concatenate(dsinks, axis=0)
        dq_ref[...] = jnp.concatenate(dqs, axis=1)
        band = jnp.concatenate(dkbs + dvbs, axis=1)
        ctxc = jnp.concatenate(dkxs + dvxs, axis=1)
        r_prev = pl.multiple_of(jnp.maximum(n - 1, 0) * _QB, _QB)
        r_cur = pl.multiple_of(n * _QB, _QB)
        r_next = pl.multiple_of((n + 1) * _QB, _QB)
        dkv_ref[pl.ds(r_prev, _QB), :] += band[0:_QB]
        dkv_ref[pl.ds(r_cur, _QB), :] += band[_QB:2 * _QB]
        dkv_ref[pl.ds(r_next, _QB), :] += band[2 * _QB:3 * _QB]
        dkv_ref[t_len:t_len + l_ctx, :] += ctxc

    return pl.pallas_call(
        body, name=name, grid=(nb,), in_specs=in_specs,
        out_specs=[pl.BlockSpec((_QB, 512), lambda n: (n, 0)),
                   pl.BlockSpec((rows, 256), lambda n: (0, 0)),
                   pl.BlockSpec((8, 128), lambda n: (0, 0))],
        out_shape=[jax.ShapeDtypeStruct((t_len, 512), F32),
                   jax.ShapeDtypeStruct((rows, 256), F32),
                   jax.ShapeDtypeStruct((8, 128), F32)],
        compiler_params=_cp("arbitrary"),
    )(sink, zc, zc, zc, zc, zc, o, dcat)


_GC = B_CHUNK


def _split_bf16(a):
    hi = a.astype(BF16)
    return hi, (a - hi.astype(F32)).astype(BF16)


def _gla_chunk_terms(qk, la, reverse):
    q = qk[:, 0:256]
    k = qk[:, 256:512]
    off = 256 if reverse else 0
    lad = la[:, off:off + 256]
    ii = lax.broadcasted_iota(jnp.int32, (_GC, _GC), 0)
    jj = lax.broadcasted_iota(jnp.int32, (_GC, _GC), 1)
    mask = (jj >= ii) if reverse else (jj <= ii)
    tri = jnp.where(mask, 1.0, 0.0).astype(BF16)
    la_hi, la_lo = _split_bf16(lad)
    g = _dot(tri, la_hi) + _dot(tri, la_lo)
    gl = jnp.sum(lad, axis=0, keepdims=True)
    eg = jnp.exp(g)
    eng = jnp.exp(-g)
    eend = jnp.exp(gl - g)
    sc = B_DK ** -0.5
    qt = q * (sc * eg)
    kt = k * eng
    ke = k * eend
    return mask, tri, gl, eg, eng, eend, qt, kt, ke


def _head(a, hh, width):
    return a[:, hh * width:(hh + 1) * width]


def _gla_fwd(zc, la, dep, t_len, l_ctx, name):
    rows = zc.shape[0]
    n_x = t_len // _GC
    n_c = n_x + l_ctx // _GC
    qkb, vb = ZC_QK // 512, ZC_V // 512

    def ch_f(c):
        return lax.rem(c + n_x, n_c)

    def ch_r(c):
        return n_c - 1 - c

    def body(qkf_ref, vf_ref, laf_ref, qkr_ref, vr_ref, lar_ref, dep_ref, of_ref, or_ref, spf_ref, spr_ref, stf, strv):
        del dep_ref
        c = pl.program_id(0)

        @pl.when(c == 0)
        def _():
            stf[...] = jnp.zeros_like(stf)
            strv[...] = jnp.zeros_like(strv)

        results = []
        for qk_ref, v_ref, la_ref, st, reverse in ((qkf_ref, vf_ref, laf_ref, stf, False),
                                                   (qkr_ref, vr_ref, lar_ref, strv, True)):
            mask, _, gl, _, _, _, qt, kt, ke = _gla_chunk_terms(qk_ref[...], la_ref[...], reverse)
            vbf = v_ref[...].astype(BF16)
            qtb, ktb, keb = qt.astype(BF16), kt.astype(BF16), ke.astype(BF16)
            egl = jnp.exp(gl)
            prevs = [st[hh] for hh in range(B_HEADS)]
            outs, news = [], []
            for hh in range(B_HEADS):
                qth, vh = _head(qtb, hh, B_DK), _head(vbf, hh, B_DV)
                att = jnp.where(mask, _dot_nt(qth, _head(ktb, hh, B_DK)), 0.0)
                outs.append(_dot(att.astype(BF16), vh) + _dot_nt(qth, prevs[hh].astype(BF16)))
                news.append(prevs[hh] * _head(egl, hh, B_DK) + _dot_tn(vh, _head(keb, hh, B_DK)))
            results.append((jnp.concatenate(outs, axis=1), prevs, news))
        for (o_all, prevs, news), o_ref, sp_ref, st in zip(results, (of_ref, or_ref), (spf_ref, spr_ref), (stf, strv)):
            o_ref[...] = o_all
            for hh in range(B_HEADS):
                sp_ref[0, hh] = prevs[hh]
                st[hh] = news[hh]

    st_shape = (B_HEADS, B_DV, B_DK)
    return pl.pallas_call(
        body, name=name, grid=(n_c,),
        in_specs=[pl.BlockSpec((_GC, 512), lambda c: (ch_f(c), qkb)),
                  pl.BlockSpec((_GC, 512), lambda c: (ch_f(c), vb)),
                  pl.BlockSpec((_GC, 512), lambda c: (ch_f(c), 0)),
                  pl.BlockSpec((_GC, 512), lambda c: (ch_r(c), qkb)),
                  pl.BlockSpec((_GC, 512), lambda c: (ch_r(c), vb)),
                  pl.BlockSpec((_GC, 512), lambda c: (ch_r(c), 0)),
                  pl.BlockSpec((8, 128), lambda c: (0, 0))],
        out_specs=[pl.BlockSpec((_GC, 512), lambda c: (ch_f(c), 0)),
                   pl.BlockSpec((_GC, 512), lambda c: (ch_r(c), 0)),
                   pl.BlockSpec((1,) + st_shape, lambda c: (c, 0, 0, 0)),
                   pl.BlockSpec((1,) + st_shape, lambda c: (c, 0, 0, 0))],
        out_shape=[jax.ShapeDtypeStruct((rows, 512), F32), jax.ShapeDtypeStruct((rows, 512), F32),
                   jax.ShapeDtypeStruct((n_c,) + st_shape, F32), jax.ShapeDtypeStruct((n_c,) + st_shape, F32)],
        scratch_shapes=[pltpu.VMEM(st_shape, F32), pltpu.VMEM(st_shape, F32)],
        compiler_params=_cp("arbitrary"),
    )(zc, zc, la, zc, zc, la, dep)


def _gla_bwd(zc, la, spf, spr, dosum, t_len, l_ctx, name):
    rows = zc.shape[0]
    n_x = t_len // _GC
    n_c = n_x + l_ctx // _GC
    n_all = rows // _GC
    qkb, vb = ZC_QK // 512, ZC_V // 512

    def scan_of(c):
        return jnp.maximum(n_c - 1 - c, 0)

    def ch_f(c):
        return jnp.where(c < n_c, lax.rem(scan_of(c) + n_x, n_c), c)

    def ch_r(c):
        return c

    def do_of(ch):
        return jnp.minimum(ch, n_x - 1)

    def body(qkf_ref, vf_ref, laf_ref, spf_ref, dof_ref, qkr_ref, vr_ref, lar_ref, spr_ref, dor_ref,
             dqkf_ref, dvf_ref, dlaf_ref, dqkr_ref, dvr_ref, dlar_ref, dsf, dsr):
        c = pl.program_id(0)

        @pl.when(c == 0)
        def _():
            dsf[...] = jnp.zeros_like(dsf)
            dsr[...] = jnp.zeros_like(dsr)

        @pl.when(c >= n_c)
        def _():
            for r in (dqkf_ref, dvf_ref, dlaf_ref, dqkr_ref, dvr_ref, dlar_ref):
                r[...] = jnp.zeros_like(r)

        @pl.when(c < n_c)
        def _():
            sc = B_DK ** -0.5
            results = []
            for qk_ref, v_ref, la_ref, sp_ref, do_ref, dst, reverse, ch in (
                    (qkf_ref, vf_ref, laf_ref, spf_ref, dof_ref, dsf, False, ch_f(c)),
                    (qkr_ref, vr_ref, lar_ref, spr_ref, dor_ref, dsr, True, ch_r(c))):
                mask, tri, gl, eg, eng, eend, qt, kt, ke = _gla_chunk_terms(qk_ref[...], la_ref[...], reverse)
                vbf = v_ref[...].astype(BF16)
                dob = jnp.where(ch < n_x, do_ref[...], 0.0).astype(BF16)
                qtb, ktb, keb = qt.astype(BF16), kt.astype(BF16), ke.astype(BF16)
                egl = jnp.exp(gl)
                prevs = [sp_ref[0, hh] for hh in range(B_HEADS)]
                dnews = [dst[hh] for hh in range(B_HEADS)]
                dqts, dkts, dkes, dvs, dprevs, dgls = [], [], [], [], [], []
                for hh in range(B_HEADS):
                    qth, kth, keh = _head(qtb, hh, B_DK), _head(ktb, hh, B_DK), _head(keb, hh, B_DK)
                    vh, doh = _head(vbf, hh, B_DV), _head(dob, hh, B_DV)
                    eglh = _head(egl, hh, B_DK)
                    dsb = dnews[hh].astype(BF16)
                    att = jnp.where(mask, _dot_nt(qth, kth), 0.0).astype(BF16)
                    datt = jnp.where(mask, _dot_nt(doh, vh), 0.0).astype(BF16)
                    dqts.append(_dot(datt, kth) + _dot(doh, prevs[hh].astype(BF16)))
                    dkts.append(_dot_tn(datt, qth))
                    dvs.append(_dot_tn(att, doh) + _dot_nt(keh, dsb))
                    dkes.append(_dot(vh, dsb))
                    dprevs.append(dnews[hh] * eglh + _dot_tn(doh, qth))
                    dgls.append(jnp.sum(dnews[hh] * prevs[hh], axis=0, keepdims=True) * eglh)
                dqt = jnp.concatenate(dqts, axis=1)
                dkt = jnp.concatenate(dkts, axis=1)
                dke = jnp.concatenate(dkes, axis=1)
                dgl = jnp.sum(dke * ke, axis=0, keepdims=True) + jnp.concatenate(dgls, axis=1)
                dg_hi, dg_lo = _split_bf16(dqt * qt - dkt * kt - dke * ke)
                dla = _dot_tn(tri, dg_hi) + _dot_tn(tri, dg_lo) + dgl
                dqk = jnp.concatenate([dqt * (sc * eg), dkt * eng + dke * eend], axis=1)
                results.append((dqk, jnp.concatenate(dvs, axis=1), dla, dprevs))
            for (dqk, dv, dla, dprevs), dqk_ref, dv_ref, dla_ref, dst in zip(
                    results, (dqkf_ref, dqkr_ref), (dvf_ref, dvr_ref), (dlaf_ref, dlar_ref), (dsf, dsr)):
                dqk_ref[...] = dqk
                dv_ref[...] = dv
                dla_ref[...] = dla
                for hh in range(B_HEADS):
                    dst[hh] = dprevs[hh]

    st_shape = (B_HEADS, B_DV, B_DK)

    def side(chf):
        return [pl.BlockSpec((_GC, 512), lambda c: (chf(c), qkb)),
                pl.BlockSpec((_GC, 512), lambda c: (chf(c), vb)),
                pl.BlockSpec((_GC, 512), lambda c: (chf(c), 0)),
                pl.BlockSpec((1,) + st_shape, lambda c: (scan_of(c), 0, 0, 0)),
                pl.BlockSpec((_GC, 512), lambda c: (do_of(chf(c)), 0))]

    def out_side(chf):
        return [pl.BlockSpec((_GC, 512), lambda c: (chf(c), 0)),
                pl.BlockSpec((_GC, 512), lambda c: (chf(c), 0)),
                pl.BlockSpec((_GC, 256), lambda c: (chf(c), 0))]

    shp = [jax.ShapeDtypeStruct((rows, 512), F32), jax.ShapeDtypeStruct((rows, 512), F32),
           jax.ShapeDtypeStruct((rows, 256), F32)]
    return pl.pallas_call(
        body, name=name, grid=(n_all,),
        in_specs=side(ch_f) + side(ch_r),
        out_specs=out_side(ch_f) + out_side(ch_r),
        out_shape=shp + shp,
        scratch_shapes=[pltpu.VMEM(st_shape, F32), pltpu.VMEM(st_shape, F32)],
        compiler_params=_cp("arbitrary"),
    )(zc, zc, la, spf, dosum, zc, zc, la, spr, dosum)


def _gla_out_fwd(o_a, o_f, o_r, zc, gla_g, t_len, name):
    tm = ROW_TILE
    rb = ZC_R // 512

    def body(oa_ref, of_ref, or_ref, r_ref, g_ref, cat_ref):
        osum = of_ref[...] + or_ref[...]
        g = g_ref[...]
        pieces = []
        for hh in range(B_HEADS):
            oh = osum[:, hh * B_DV:(hh + 1) * B_DV]
            rs = lax.rsqrt(jnp.mean(oh * oh, axis=-1, keepdims=True) + RMS_EPS)
            pieces.append((oh * rs) * g)
        r = r_ref[...]
        cat_ref[:, 0:512] = oa_ref[...].astype(BF16)
        cat_ref[:, 512:1024] = (jnp.concatenate(pieces, axis=1) * (r * _sigmoid(r))).astype(BF16)

    return pl.pallas_call(
        body, name=name, grid=(t_len // tm,),
        in_specs=[pl.BlockSpec((tm, 512), lambda i: (i, 0)),
                  pl.BlockSpec((tm, 512), lambda i: (i, 0)),
                  pl.BlockSpec((tm, 512), lambda i: (i, 0)),
                  pl.BlockSpec((tm, 512), lambda i: (i, rb)),
                  pl.BlockSpec((1, B_DV), lambda i: (0, 0))],
        out_specs=pl.BlockSpec((tm, D_MODEL), lambda i: (i, 0)),
        out_shape=jax.ShapeDtypeStruct((t_len, D_MODEL), BF16),
        compiler_params=_cp("parallel"),
    )(o_a, o_f, o_r, zc, gla_g)


def _gla_out_bwd(dcat, o_f, o_r, zc, gla_g, t_len, name):
    tm = ROW_TILE
    rb = ZC_R // 512

    def body(d_ref, of_ref, or_ref, r_ref, g_ref, dos_ref, dr_ref, dg_ref):
        i = pl.program_id(0)
        osum = of_ref[...] + or_ref[...]
        g = g_ref[...]
        r = r_ref[...]
        dgo = d_ref[...]
        sg = _sigmoid(r)
        dnrmg = dgo * (r * sg)
        nrms, dos = [], []
        dg_acc = jnp.zeros((1, B_DV), F32)
        for hh in range(B_HEADS):
            oh = osum[:, hh * B_DV:(hh + 1) * B_DV]
            rs = lax.rsqrt(jnp.mean(oh * oh, axis=-1, keepdims=True) + RMS_EPS)
            nrm = oh * rs
            dn = dnrmg[:, hh * B_DV:(hh + 1) * B_DV]
            dg_acc = dg_acc + jnp.sum(dn * nrm, axis=0, keepdims=True)
            dnn = dn * g
            dos.append(rs * (dnn - nrm * jnp.mean(dnn * nrm, axis=-1, keepdims=True)))
            nrms.append(nrm * g)
        dos_ref[...] = jnp.concatenate(dos, axis=1)
        dr_ref[...] = dgo * jnp.concatenate(nrms, axis=1) * (sg * (1.0 + r * (1.0 - sg)))

        @pl.when(i == 0)
        def _():
            dg_ref[...] = jnp.zeros_like(dg_ref)

        dg_ref[...] += dg_acc

    return pl.pallas_call(
        body, name=name, grid=(t_len // tm,),
        in_specs=[pl.BlockSpec((tm, 512), lambda i: (i, 1)),
                  pl.BlockSpec((tm, 512), lambda i: (i, 0)),
                  pl.BlockSpec((tm, 512), lambda i: (i, 0)),
                  pl.BlockSpec((tm, 512), lambda i: (i, rb)),
                  pl.BlockSpec((1, B_DV), lambda i: (0, 0))],
        out_specs=[pl.BlockSpec((tm, 512), lambda i: (i, 0)),
                   pl.BlockSpec((tm, 512), lambda i: (i, 0)),
                   pl.BlockSpec((1, B_DV), lambda i: (0, 0))],
        out_shape=[jax.ShapeDtypeStruct((t_len, 512), F32), jax.ShapeDtypeStruct((t_len, 512), F32),
                   jax.ShapeDtypeStruct((1, B_DV), F32)],
        compiler_params=_cp("arbitrary"),
    )(dcat, o_f, o_r, zc, gla_g)


def _mix_prep(dq, dkv, dqk_f, dqk_r, dv_f, dv_r, d_r, dla_f, dla_r, zc, wg2, bias2, cs, t_len, name):
    rows = zc.shape[0]
    tm = ROW_TILE
    n_x = t_len // tm
    gb = ZC_G // 128

    def xrow(i):
        return jnp.minimum(i, n_x - 1)

    def body(dq_ref, dkv_ref, dqkf_ref, dqkr_ref, dvf_ref, dvr_ref, dr_ref, dlaf_ref, dlar_ref, zg_ref, wg_ref,
             b_ref, cs_ref, dz_ref, dwg_ref, db_ref):
        i = pl.program_id(0)
        is_x = i < n_x
        cos = cs_ref[:, 0:128]
        sin = cs_ref[:, 128:256]
        cosq = jnp.concatenate([cos] * 4, axis=1)
        sinq = jnp.concatenate([sin] * 4, axis=1)
        dqv = jnp.where(is_x, dq_ref[...], 0.0)
        dz_ref[:, ZC_Q:ZC_QK] = (dqv * cosq + _swap16(dqv * sinq)).astype(BF16)
        dz_ref[:, ZC_QK:ZC_V] = (dqkf_ref[...] + dqkr_ref[...]).astype(BF16)
        dz_ref[:, ZC_V:ZC_R] = (dvf_ref[...] + dvr_ref[...]).astype(BF16)
        dz_ref[:, ZC_R:ZC_KV] = jnp.where(is_x, dr_ref[...], 0.0).astype(BF16)
        dk = dkv_ref[:, 0:128]
        dz_ref[:, ZC_KV:ZC_KV + 128] = (dk * cos + _swap16(dk * sin)).astype(BF16)
        dz_ref[:, ZC_KV + 128:ZC_G] = dkv_ref[:, 128:256].astype(BF16)
        zgb = zg_ref[...].astype(BF16)
        wg = wg_ref[...]
        pre = _dot(zgb, wg) + b_ref[...]
        dla = jnp.concatenate([dlaf_ref[...], dlar_ref[...]], axis=1)
        dpre = dla * (_sigmoid(-pre) / B_GATE_NORM)
        dpb = dpre.astype(BF16)
        dz_ref[:, ZC_G:ZC_W] = _dot_nt(dpb, wg).astype(BF16)

        @pl.when(i == 0)
        def _():
            dwg_ref[...] = jnp.zeros_like(dwg_ref)
            db_ref[...] = jnp.zeros_like(db_ref)

        dwg_ref[...] += _dot_tn(zgb, dpb)
        db_ref[...] += jnp.sum(dpre, axis=0, keepdims=True)

    return pl.pallas_call(
        body, name=name, grid=(rows // tm,),
        in_specs=[pl.BlockSpec((tm, 512), lambda i: (xrow(i), 0)),
                  pl.BlockSpec((tm, 256), lambda i: (i, 0)),
                  pl.BlockSpec((tm, 512), lambda i: (i, 0)),
                  pl.BlockSpec((tm, 512), lambda i: (i, 0)),
                  pl.BlockSpec((tm, 512), lambda i: (i, 0)),
                  pl.BlockSpec((tm, 512), lambda i: (i, 0)),
                  pl.BlockSpec((tm, 512), lambda i: (xrow(i), 0)),
                  pl.BlockSpec((tm, 256), lambda i: (i, 0)),
                  pl.BlockSpec((tm, 256), lambda i: (i, 0)),
                  pl.BlockSpec((tm, 128), lambda i: (i, gb)),
                  pl.BlockSpec((128, 512), lambda i: (0, 0)),
                  pl.BlockSpec((1, 512), lambda i: (0, 0)),
                  pl.BlockSpec((tm, 256), lambda i: (i, 0))],
        out_specs=[pl.BlockSpec((tm, ZC_W), lambda i: (i, 0)),
                   pl.BlockSpec((128, 512), lambda i: (0, 0)),
                   pl.BlockSpec((1, 512), lambda i: (0, 0))],
        out_shape=[jax.ShapeDtypeStruct((rows, ZC_W), BF16),
                   jax.ShapeDtypeStruct((128, 512), F32),
                   jax.ShapeDtypeStruct((1, 512), F32)],
        compiler_params=_cp("arbitrary"),
    )(dq, dkv, dqk_f, dqk_r, dv_f, dv_r, d_r, dla_f, dla_r, zc, wg2, bias2, cs)


def _gate_weights(w_a2_f, b_a_f, w_a2_b, b_a_b):
    wg2 = jnp.zeros((128, 512), F32)
    wg2 = wg2.at[0:B_GATE_RANK, 0:256].set(w_a2_f).at[B_GATE_RANK:2 * B_GATE_RANK, 256:512].set(w_a2_b)
    bias2 = jnp.concatenate([b_a_f, b_a_b]).reshape(1, 512)
    return wg2.astype(BF16), bias2


_WIN_PERM = ((0, 512), (768, 1280), (1280, 1792), (1792, 2304), (512, 768), (2304, 2336))


def _w_in_to_cat(w_in_full):
    parts = [w_in_full[:, a:b] for a, b in _WIN_PERM]
    parts.append(jnp.zeros((w_in_full.shape[0], ZC_W - PROJ_DIM), w_in_full.dtype))
    return jnp.concatenate(parts, axis=1)


def _cat_to_w_in(d_wcat):
    return jnp.concatenate([d_wcat[:, ZC_Q:ZC_QK], d_wcat[:, ZC_KV:ZC_G], d_wcat[:, ZC_QK:ZC_KV],
                            d_wcat[:, ZC_G:ZC_G + 2 * B_GATE_RANK]], axis=1)


def _mixer_ab_forward(x1, g3, mods, wcat, wg2, bias2, sink, gla_g, w_out, cs, t_len, l_ctx, n_x, pace):
    h = _rms_mod_fwd(x1, g3, mods, 1, n_x, BF16, "mix0_mod")
    zc, la = _proj_fwd(h, wcat, wg2, bias2, cs, "mix0_proj")
    dep = pace("proj", zc)
    o_a = _attn_fwd(zc, sink + dep[0, 0], t_len, l_ctx, "mix0_attn")
    dep = pace("attn", o_a)
    o_f, o_r, spf, spr = _gla_fwd(zc, la, dep, t_len, l_ctx, "mix0_gla")
    dep = pace("gla", o_f)
    cat = _gla_out_fwd(o_a, o_f, o_r, zc, gla_g + dep[0:1, 0:1], t_len, "mix0_glaout")
    x2, y = _matmul_resid(cat, w_out, x1, mods, 5, 1.0, n_x, t_len, "mix0_out")
    return x2, (x1, h, zc, la, o_a, o_f, o_r, spf, spr, cat, y)


def _mixer_ab_backward(dx2, saved, g3, mods, wcat, wg2, bias2, sink, gla_g, w_out, cs, t_len, l_ctx, n_x):
    x1, h, zc, la, o_a, o_f, o_r, spf, spr, cat, y = saved
    rows = x1.shape[0]
    tm = ROW_TILE
    dy, dgate = _gate_dy(dx2, y, mods, 5, 1.0, n_x, t_len, "mix0_dy")
    dcat = _matmul_nt(dy, w_out, "mix0_dcat")
    tk = _token_tile(t_len)
    d_wout = _matmul_tn(
        cat, dy, pl.BlockSpec((tk, D_MODEL), lambda n, k: (k, 0)), pl.BlockSpec((tk, D_MODEL), lambda n, k: (k, 0)),
        (D_MODEL, D_MODEL), pl.BlockSpec((D_MODEL, D_MODEL), lambda n, k: (0, 0)), (1, t_len // tk), "mix0_dwout")
    dos, d_r, d_glag = _gla_out_bwd(dcat, o_f, o_r, zc, gla_g, t_len, "mix0_dglaout")
    dqk_f, dv_f, dla_f, dqk_r, dv_r, dla_r = _gla_bwd(zc, la, spf, spr, dos, t_len, l_ctx, "mix0_dgla")
    dq, dkv, dsink = _attn_bwd(zc, sink, o_a, dcat, t_len, l_ctx, "mix0_dattn")
    dzc, dwg2, dbias2 = _mix_prep(dq, dkv, dqk_f, dqk_r, dv_f, dv_r, d_r, dla_f, dla_r, zc, wg2, bias2, cs, t_len,
                                  "mix0_prep")
    tk = _token_tile(rows)
    d_wcat = _matmul_tn(
        h, dzc, pl.BlockSpec((tk, D_MODEL), lambda n, k: (k, 0)), pl.BlockSpec((tk, ZC_W), lambda n, k: (k, 0)),
        (D_MODEL, ZC_W), pl.BlockSpec((D_MODEL, ZC_W), lambda n, k: (0, 0)), (1, rows // tk), "mix0_dwin")
    pairs = [(dzc, pl.BlockSpec((tm, ZC_W), lambda i: (i, 0)), wcat, pl.BlockSpec((D_MODEL, ZC_W), lambda i: (0, 0)))]
    dx1, stats = _bwd_dx(pairs, x1, dx2, t_len // tm, g3, mods, 1, n_x, "mix0_dx")
    return dx1, stats, dgate, d_wcat, dwg2, dbias2, dsink, d_glag, d_wout


_PT = 256
_PH = 16


def _pool_window(n, t_len, w, transpose):
    shape = (_PT, _PT + 2 * _PH)
    a = n * _PT + lax.broadcasted_iota(jnp.int32, shape, 0)
    b = n * _PT - _PH + lax.broadcasted_iota(jnp.int32, shape, 1)
    t, s = (b, a) if transpose else (a, b)
    lo = jnp.maximum(t - w // 2, 0)
    hi = jnp.minimum(t + (w - w // 2), t_len)
    inside = (s >= lo) & (s < hi) & (t >= 0) & (t < t_len)
    return jnp.where(inside, 1.0, 0.0).astype(BF16)


def _pool_inv_count(first, count, t_len, w):
    t = first + lax.broadcasted_iota(jnp.int32, (count, 1), 0)
    lo = jnp.maximum(t - w // 2, 0)
    hi = jnp.minimum(t + (w - w // 2), t_len)
    return jnp.where((t >= 0) & (t < t_len), 1.0 / jnp.maximum(hi - lo, 1).astype(F32), 0.0)


def _window_sum(win, vals):
    hi, lo = _split_bf16(vals)
    return _dot(win, hi) + _dot(win, lo)


def _pool_halo(p_ref, c_ref, n_ref):
    return jnp.concatenate([p_ref[_PT - _PH:_PT, :], c_ref[...], n_ref[0:_PH, :]], axis=0)


def _pool_specs(t_len):
    nb = t_len // _PT
    return [pl.BlockSpec((_PT, D_MODEL), lambda n: (jnp.maximum(n - 1, 0), 0)),
            pl.BlockSpec((_PT, D_MODEL), lambda n: (n, 0)),
            pl.BlockSpec((_PT, D_MODEL), lambda n: (jnp.minimum(n + 1, nb - 1), 0))], nb


def _pool_fwd(h, wp, pscale, x1, mods, t_len, name):
    halo_specs, nb = _pool_specs(t_len)

    def body(hp_ref, hc_ref, hn_ref, w_ref, ps_ref, x_ref, m_ref, x2_ref, pooled_ref, ypre_ref):
        n = pl.program_id(0)
        hcat = _pool_halo(hp_ref, hc_ref, hn_ref)
        ys = []
        for gi, w in enumerate(POOL_WINDOWS):
            cols = slice(gi * POOL_GROUP, (gi + 1) * POOL_GROUP)
            hg = hcat[:, cols]
            mean = _window_sum(_pool_window(n, t_len, w, False), hg) * _pool_inv_count(n * _PT, _PT, t_len, w)
            pooled = (mean - hg[_PH:_PH + _PT]).astype(BF16)
            pooled_ref[:, cols] = pooled
            ys.append(_dot(pooled, w_ref[gi]))
        ypre = jnp.concatenate(ys, axis=1)
        ypre_ref[...] = ypre
        x2_ref[...] = x_ref[...] + m_ref[0, 5:6, :] * (ypre * ps_ref[...])

    return pl.pallas_call(
        body, name=name, grid=(nb,),
        in_specs=halo_specs + [pl.BlockSpec((4, POOL_GROUP, POOL_GROUP), lambda n: (0, 0, 0)),
                               pl.BlockSpec((1, D_MODEL), lambda n: (0, 0)),
                               pl.BlockSpec((_PT, D_MODEL), lambda n: (n, 0)),
                               pl.BlockSpec((1, N_MOD, D_MODEL), lambda n: (0, 0, 0))],
        out_specs=[pl.BlockSpec((_PT, D_MODEL), lambda n: (n, 0))] * 3,
        out_shape=[jax.ShapeDtypeStruct((t_len, D_MODEL), F32), jax.ShapeDtypeStruct((t_len, D_MODEL), BF16),
                   jax.ShapeDtypeStruct((t_len, D_MODEL), F32)],
        compiler_params=_cp("parallel"),
    )(h, h, h, wp, pscale, x1, mods)


def _pool_bwd_a(dx2, ypre, wp, pscale, mods, t_len, name):
    nb = t_len // _PT

    def body(d_ref, y_ref, w_ref, ps_ref, m_ref, dyp_ref, dpl_ref, dgate_ref, dps_ref):
        n = pl.program_id(0)
        dv = d_ref[...]
        ypre = y_ref[...]
        ps = ps_ref[...]
        dy = dv * m_ref[0, 5:6, :]
        dyp = (dy * ps).astype(BF16)
        dyp_ref[...] = dyp
        for gi in range(len(POOL_WINDOWS)):
            cols = slice(gi * POOL_GROUP, (gi + 1) * POOL_GROUP)
            dpl_ref[:, cols] = _dot_nt(dyp[:, cols], w_ref[gi])

        @pl.when(n == 0)
        def _():
            dgate_ref[...] = jnp.zeros_like(dgate_ref)
            dps_ref[...] = jnp.zeros_like(dps_ref)

        dgate_ref[...] += jnp.sum(dv * (ypre * ps), axis=0, keepdims=True)
        dps_ref[...] += jnp.sum(dy * ypre, axis=0, keepdims=True)

    return pl.pallas_call(
        body, name=name, grid=(nb,),
        in_specs=[pl.BlockSpec((_PT, D_MODEL), lambda n: (n, 0)),
                  pl.BlockSpec((_PT, D_MODEL), lambda n: (n, 0)),
                  pl.BlockSpec((4, POOL_GROUP, POOL_GROUP), lambda n: (0, 0, 0)),
                  pl.BlockSpec((1, D_MODEL), lambda n: (0, 0)),
                  pl.BlockSpec((1, N_MOD, D_MODEL), lambda n: (0, 0, 0))],
        out_specs=[pl.BlockSpec((_PT, D_MODEL), lambda n: (n, 0)),
                   pl.BlockSpec((_PT, D_MODEL), lambda n: (n, 0)),
                   pl.BlockSpec((1, D_MODEL), lambda n: (0, 0)),
                   pl.BlockSpec((1, D_MODEL), lambda n: (0, 0))],
        out_shape=[jax.ShapeDtypeStruct((t_len, D_MODEL), BF16), jax.ShapeDtypeStruct((t_len, D_MODEL), F32),
                   jax.ShapeDtypeStruct((1, D_MODEL), F32), jax.ShapeDtypeStruct((1, D_MODEL), F32)],
        compiler_params=_cp("arbitrary"),
    )(dx2, ypre, wp, pscale, mods)


def _pool_bwd_dx(dpl, x1, dx2, g3, mods, t_len, name):
    halo_specs, nb = _pool_specs(t_len)

    def body(dp_ref, dc_ref, dn_ref, x_ref, d_ref, g_ref, m_ref, dx_ref, acc_ref):
        n = pl.program_id(0)
        dcat = _pool_halo(dp_ref, dc_ref, dn_ref)
        dhs = []
        for gi, w in enumerate(POOL_WINDOWS):
            cols = slice(gi * POOL_GROUP, (gi + 1) * POOL_GROUP)
            dg = dcat[:, cols]
            scaled = dg * _pool_inv_count(n * _PT - _PH, _PT + 2 * _PH, t_len, w)
            dhs.append(_window_sum(_pool_window(n, t_len, w, True), scaled) - dg[_PH:_PH + _PT])
        dh = jnp.concatenate(dhs, axis=1)
        g = g_ref[1:2, :]
        scale = m_ref[0, 4:5, :]
        dx = _rms_mod_bwd_tail(dh, x_ref[...], g, scale, 0, acc_ref, n == 0)
        dx_ref[...] = d_ref[...] + dx

    return pl.pallas_call(
        body, name=name, grid=(nb,),
        in_specs=halo_specs + [pl.BlockSpec((_PT, D_MODEL), lambda n: (n, 0)),
                               pl.BlockSpec((_PT, D_MODEL), lambda n: (n, 0)),
                               pl.BlockSpec((3, D_MODEL), lambda n: (0, 0)),
                               pl.BlockSpec((1, N_MOD, D_MODEL), lambda n: (0, 0, 0))],
        out_specs=[pl.BlockSpec((_PT, D_MODEL), lambda n: (n, 0)),
                   pl.BlockSpec((2, 3, D_MODEL), lambda n: (0, 0, 0))],
        out_shape=[jax.ShapeDtypeStruct((t_len, D_MODEL), F32), jax.ShapeDtypeStruct((2, 3, D_MODEL), F32)],
        compiler_params=_cp("arbitrary"),
    )(dpl, dpl, dpl, x1, dx2, g3, mods)


def _mixer_pool_forward(x1, g3, mods, wp, pscale, t_len):
    h = _rms_mod_fwd(x1, g3, mods, 1, t_len // ROW_TILE, F32, "mix1_mod")
    x2, pooled, ypre = _pool_fwd(h, wp, pscale, x1, mods, t_len, "mix1_pool")
    return x2, (x1, pooled, ypre)


def _mixer_pool_backward(dx2, saved, g3, mods, wp, pscale, t_len):
    x1, pooled, ypre = saved
    tm = ROW_TILE
    dyp, dpl, dgate, dps = _pool_bwd_a(dx2, ypre, wp, pscale, mods, t_len, "mix1_da")
    d_wp = _matmul_tn(
        pooled, dyp, pl.BlockSpec((tm, POOL_GROUP), lambda g, k: (k, g)),
        pl.BlockSpec((tm, POOL_GROUP), lambda g, k: (k, g)),
        (4, POOL_GROUP, POOL_GROUP), pl.BlockSpec((1, POOL_GROUP, POOL_GROUP), lambda g, k: (g, 0, 0)),
        (4, t_len // tm), "mix1_dwp")
    dx1, stats = _pool_bwd_dx(dpl, x1, dx2, g3, mods, t_len, "mix1_dx")
    return dx1, stats, dgate, dps, d_wp


def _final_loss(x3, final_g, target, name):
    t_len = x3.shape[0]
    tm = ROW_TILE

    def body(x_ref, g_ref, t_ref, dx_ref, loss_ref, dg_ref):
        i = pl.program_id(0)
        xv = x_ref[...]
        g = g_ref[...]
        r = lax.rsqrt(jnp.mean(xv * xv, axis=-1, keepdims=True) + RMS_EPS)
        xhat = xv * r
        err = xhat * g - t_ref[...]
        part = 0.5 * jnp.sum(jnp.mean(err * err, axis=-1, keepdims=True), axis=0, keepdims=True)
        dy = err * (1.0 / D_MODEL)

        @pl.when(i == 0)
        def _():
            loss_ref[...] = jnp.zeros_like(loss_ref)
            dg_ref[...] = jnp.zeros_like(dg_ref)

        loss_ref[...] += jnp.broadcast_to(part, (1, 128))
        dg_ref[...] += jnp.sum(dy * xhat, axis=0, keepdims=True)
        dxh = dy * g
        dx_ref[...] = r * (dxh - xhat * jnp.mean(dxh * xhat, axis=-1, keepdims=True))

    return pl.pallas_call(
        body, name=name, grid=(t_len // tm,),
        in_specs=[pl.BlockSpec((tm, D_MODEL), lambda i: (i, 0)),
                  pl.BlockSpec((1, D_MODEL), lambda i: (0, 0)),
                  pl.BlockSpec((tm, D_MODEL), lambda i: (i, 0))],
        out_specs=[pl.BlockSpec((tm, D_MODEL), lambda i: (i, 0)),
                   pl.BlockSpec((1, 128), lambda i: (0, 0)),
                   pl.BlockSpec((1, D_MODEL), lambda i: (0, 0))],
        out_shape=[jax.ShapeDtypeStruct((t_len, D_MODEL), F32), jax.ShapeDtypeStruct((1, 128), F32),
                   jax.ShapeDtypeStruct((1, D_MODEL), F32)],
        compiler_params=_cp("arbitrary"),
    )(x3, final_g, target)


_CROWS = 16


def _adaln_fwd(c16, w_mod, bias_k, name):
    n_l, _, cols = w_mod.shape

    def body(c_ref, w_ref, b_ref, o_ref):
        cv = c_ref[...]
        sc = (cv * _sigmoid(cv)).astype(BF16)
        o_ref[0] = _dot(sc, w_ref[0].astype(BF16)) + b_ref[0]

    return pl.pallas_call(
        body, name=name, grid=(n_l,),
        in_specs=[pl.BlockSpec((_CROWS, D_MODEL), lambda l: (0, 0)),
                  pl.BlockSpec((1, D_MODEL, cols), lambda l: (l, 0, 0)),
                  pl.BlockSpec((1, 1, cols), lambda l: (l, 0, 0))],
        out_specs=pl.BlockSpec((1, _CROWS, cols), lambda l: (l, 0, 0)),
        out_shape=jax.ShapeDtypeStruct((n_l, _CROWS, cols), F32),
        compiler_params=_cp("parallel"),
    )(c16, w_mod, bias_k)


def _adaln_bwd(c16, d16, w_mod, dmmc_k, name):
    n_l, _, cols = w_mod.shape

    def body(c_ref, d_ref, w_ref, dm_ref, gw_ref, cp_ref):
        layer = pl.program_id(0)
        cv = c_ref[...]
        gw_ref[0] = _dot_tn_hi(cv * _sigmoid(cv), d_ref[0])

        @pl.when(layer == 0)
        def _():
            cp_ref[...] = jnp.sum(w_ref[0] * dm_ref[...], axis=1, keepdims=True)

    return pl.pallas_call(
        body, name=name, grid=(n_l,),
        in_specs=[pl.BlockSpec((_CROWS, D_MODEL), lambda l: (0, 0)),
                  pl.BlockSpec((1, _CROWS, cols), lambda l: (l, 0, 0)),
                  pl.BlockSpec((1, D_MODEL, cols), lambda l: (0, 0, 0)),
                  pl.BlockSpec((1, cols), lambda l: (0, 0))],
        out_specs=[pl.BlockSpec((1, D_MODEL, cols), lambda l: (l, 0, 0)),
                   pl.BlockSpec((D_MODEL, 1), lambda l: (0, 0))],
        out_shape=[jax.ShapeDtypeStruct((n_l, D_MODEL, cols), F32), jax.ShapeDtypeStruct((D_MODEL, 1), F32)],
        compiler_params=_cp("arbitrary"),
    )(c16, d16, w_mod, dmmc_k)


def _cctx_grad(cparts, c_ctx2, name):
    def body(p_ref, c_ref, o_ref):
        tot = ((p_ref[0] + p_ref[2]) + p_ref[4]) + p_ref[6]
        cv = c_ref[...]
        sg = _sigmoid(cv)
        o_ref[...] = tot * (sg * (1.0 + cv * (1.0 - sg)))

    return pl.pallas_call(
        body, name=name, out_shape=jax.ShapeDtypeStruct((8, 128), F32),
        in_specs=[pl.BlockSpec(memory_space=pltpu.VMEM), pl.BlockSpec(memory_space=pltpu.VMEM)],
        out_specs=pl.BlockSpec(memory_space=pltpu.VMEM),
    )(cparts, c_ctx2)


def _sum_devices(ga, name):
    def body(g_ref, o_ref):
        acc = g_ref[0]
        for d in range(1, N_DEV):
            acc = acc + g_ref[d]
        o_ref[...] = acc

    return pl.pallas_call(
        body, name=name, out_shape=jax.ShapeDtypeStruct(ga.shape[1:], F32),
        in_specs=[pl.BlockSpec(memory_space=pltpu.VMEM)], out_specs=pl.BlockSpec(memory_space=pltpu.VMEM),
    )(ga)


def _place():
    return lax.axis_index("x"), lax.axis_index("y"), lax.axis_index("c")


def _flip(a, d):
    return 1 - a if d else a


_CHIP_FLIPS = ((1, 0), (0, 1), (1, 1))


def _allgather_small(v, name):
    r, cc = v.shape

    def body(v_ref, out_ref, send_sems, recv_sems, local_sem):
        x, y, c = _place()
        me = 4 * x + 2 * y + c
        mine = pltpu.make_async_copy(v_ref, out_ref.at[me], local_sem)
        mine.start()
        sends = []
        for k in range(1, N_DEV):
            peer = (_flip(x, (k >> 2) & 1), _flip(y, (k >> 1) & 1), _flip(c, k & 1))
            cp = pltpu.make_async_remote_copy(src_ref=v_ref, dst_ref=out_ref.at[me], send_sem=send_sems.at[k - 1],
                                              recv_sem=recv_sems.at[k - 1], device_id=peer, device_id_type=MESH)
            cp.start()
            sends.append(cp)
        for k in range(1, N_DEV):
            px, py, pc = _flip(x, (k >> 2) & 1), _flip(y, (k >> 1) & 1), _flip(c, k & 1)
            pltpu.make_async_remote_copy(src_ref=v_ref, dst_ref=out_ref.at[4 * px + 2 * py + pc],
                                         send_sem=send_sems.at[k - 1], recv_sem=recv_sems.at[k - 1],
                                         device_id=(px, py, pc), device_id_type=MESH).wait_recv()
        for cp in sends:
            cp.wait_send()
        mine.wait()

    return pl.pallas_call(
        body, name=name, out_shape=jax.ShapeDtypeStruct((N_DEV, r, cc), F32),
        in_specs=[pl.BlockSpec(memory_space=pltpu.VMEM)], out_specs=pl.BlockSpec(memory_space=pltpu.VMEM),
        scratch_shapes=[pltpu.SemaphoreType.DMA((N_DEV - 1,)), pltpu.SemaphoreType.DMA((N_DEV - 1,)),
                        pltpu.SemaphoreType.DMA],
        compiler_params=pltpu.CompilerParams(vmem_limit_bytes=VMEM_LIMIT_BYTES),
    )(v)


_HBM_SPEC = pl.BlockSpec(memory_space=pltpu.HBM)
_SEM_SPEC = pl.BlockSpec(memory_space=pltpu.SEMAPHORE)
_EFFECT = pltpu.SideEffectType.DATAFLOW_SIDE_EFFECTING


def _in_hbm(a):
    return pltpu.with_memory_space_constraint(a, pltpu.HBM)


def _gather_start(arrs, groups, after, name):
    n, n_g = len(arrs), len(groups)

    def body(*refs):
        ins, zones = refs[:n], refs[n:2 * n]
        sems = refs[2 * n + 1:2 * n + 1 + 2 * n_g]
        token = refs[2 * n + 1 + 2 * n_g + 2 * n]
        x, y, c = _place()
        k_me = 2 * x + y
        for g, members in enumerate(groups):
            for t, a in enumerate(members):
                for j, (dx, dy) in enumerate(_CHIP_FLIPS):
                    pltpu.make_async_remote_copy(
                        src_ref=ins[a], dst_ref=zones[a].at[k_me], send_sem=sems[2 * g].at[3 * t + j],
                        recv_sem=sems[2 * g + 1].at[3 * t + j], device_id=(_flip(x, dx), _flip(y, dy), c),
                        device_id_type=MESH).start()
        token[...] = jnp.zeros_like(token)

    k_own = 2 * lax.axis_index("x") + lax.axis_index("y")
    zones = [lax.dynamic_update_slice(lax.empty((N_CHIPS,) + a.shape, a.dtype), a[None], (k_own,) + (0,) * a.ndim)
             for a in arrs]
    sem_shapes = []
    for members in groups:
        sem_shapes += [pltpu.SemaphoreType.DMA((3 * len(members),))] * 2
    outs = pl.pallas_call(
        body, name=name,
        out_shape=sem_shapes + [pltpu.HBM(a.shape, a.dtype) for a in arrs]
        + [pltpu.HBM(z.shape, z.dtype) for z in zones] + [jax.ShapeDtypeStruct((8, 128), F32)],
        in_specs=[_HBM_SPEC] * (2 * n) + [pl.BlockSpec(memory_space=pl.ANY)],
        out_specs=[_SEM_SPEC] * (2 * n_g) + [_HBM_SPEC] * (2 * n) + [pl.BlockSpec(memory_space=pltpu.VMEM)],
        input_output_aliases={i: 2 * n_g + i for i in range(2 * n)},
        compiler_params=pltpu.CompilerParams(has_side_effects=_EFFECT),
    )(*[_in_hbm(a) for a in arrs], *[_in_hbm(z) for z in zones], after)
    sems = outs[:2 * n_g]
    thru = outs[2 * n_g:2 * n_g + n]
    zones = outs[2 * n_g + n:2 * n_g + 2 * n]
    return [(sems[2 * g], sems[2 * g + 1]) for g in range(n_g)], thru, zones, outs[-1]


def _gather_wait(shards, zones, send_sems, recv_sems, after, name):
    m = len(shards)

    def body(*refs):
        ins, zs = refs[:m], refs[m:2 * m]
        ssem, rsem = refs[2 * m], refs[2 * m + 1]
        x, y, c = _place()
        for t in range(m):
            for j, (dx, dy) in enumerate(_CHIP_FLIPS):
                px, py = _flip(x, dx), _flip(y, dy)
                cp = pltpu.make_async_remote_copy(
                    src_ref=ins[t], dst_ref=zs[t].at[2 * px + py], send_sem=ssem.at[3 * t + j],
                    recv_sem=rsem.at[3 * t + j], device_id=(px, py, c), device_id_type=MESH)
                cp.wait_send()
                cp.wait_recv()

    after = list(after) if isinstance(after, (list, tuple)) else [after]
    outs = pl.pallas_call(
        body, name=name,
        out_shape=[pltpu.HBM(a.shape, a.dtype) for a in list(shards) + list(zones)],
        in_specs=[_HBM_SPEC] * (2 * m) + [_SEM_SPEC, _SEM_SPEC] + [pl.BlockSpec(memory_space=pl.ANY)] * len(after),
        out_specs=[_HBM_SPEC] * (2 * m),
        input_output_aliases={i: i for i in range(2 * m)},
        compiler_params=pltpu.CompilerParams(has_side_effects=_EFFECT),
    )(*shards, *zones, send_sems, recv_sems, *after)
    return outs[m:]


def _scatter_start(arrs, name):
    n = len(arrs)

    def body(*refs):
        ins, lands = refs[:n], refs[n:2 * n]
        ssem, rsem = refs[2 * n], refs[2 * n + 1]
        token = refs[2 * n + 2 + 2 * n]
        x, y, c = _place()
        for a in range(n):
            for j, (dx, dy) in enumerate(_CHIP_FLIPS):
                px, py = _flip(x, dx), _flip(y, dy)
                pltpu.make_async_remote_copy(
                    src_ref=ins[a].at[2 * px + py], dst_ref=lands[a].at[j], send_sem=ssem.at[3 * a + j],
                    recv_sem=rsem.at[3 * a + j], device_id=(px, py, c), device_id_type=MESH).start()
        token[...] = jnp.zeros_like(token)

    lands = [lax.empty((3,) + a.shape[1:], a.dtype) for a in arrs]
    outs = pl.pallas_call(
        body, name=name,
        out_shape=[pltpu.SemaphoreType.DMA((3 * n,))] * 2 + [pltpu.HBM(a.shape, a.dtype) for a in arrs]
        + [pltpu.HBM(z.shape, z.dtype) for z in lands] + [jax.ShapeDtypeStruct((8, 128), F32)],
        in_specs=[_HBM_SPEC] * (2 * n),
        out_specs=[_SEM_SPEC] * 2 + [_HBM_SPEC] * (2 * n) + [pl.BlockSpec(memory_space=pltpu.VMEM)],
        input_output_aliases={i: 2 + i for i in range(2 * n)},
        compiler_params=pltpu.CompilerParams(has_side_effects=_EFFECT),
    )(*[_in_hbm(a) for a in arrs], *[_in_hbm(z) for z in lands])
    return outs[0], outs[1], outs[2:2 + n], outs[2 + n:2 + 2 * n], outs[-1]


def _scatter_wait(arrs, lands, send_sems, recv_sems, after, name):
    n = len(arrs)

    def body(*refs):
        ins, lz = refs[:n], refs[n:2 * n]
        ssem, rsem = refs[2 * n], refs[2 * n + 1]
        x, y, c = _place()
        for a in range(n):
            for j, (dx, dy) in enumerate(_CHIP_FLIPS):
                px, py = _flip(x, dx), _flip(y, dy)
                cp = pltpu.make_async_remote_copy(
                    src_ref=ins[a].at[2 * px + py], dst_ref=lz[a].at[j], send_sem=ssem.at[3 * a + j],
                    recv_sem=rsem.at[3 * a + j], device_id=(px, py, c), device_id_type=MESH)
                cp.wait_send()
                cp.wait_recv()

    outs = pl.pallas_call(
        body, name=name,
        out_shape=[pltpu.HBM(a.shape, a.dtype) for a in list(arrs) + list(lands)],
        in_specs=[_HBM_SPEC] * (2 * n) + [_SEM_SPEC, _SEM_SPEC, pl.BlockSpec(memory_space=pl.ANY)],
        out_specs=[_HBM_SPEC] * (2 * n),
        input_output_aliases={i: i for i in range(2 * n)},
        compiler_params=pltpu.CompilerParams(has_side_effects=_EFFECT),
    )(*arrs, *lands, send_sems, recv_sems, after)
    return outs[:n], outs[n:]


def _swap_sibling(arrs, name):
    n = len(arrs)

    def body(*refs):
        ins, outs = refs[:n], refs[n:2 * n]
        send_sems, recv_sems = refs[2 * n:]
        x, y, c = _place()
        sends = []
        for a in range(n):
            cp = pltpu.make_async_remote_copy(src_ref=ins[a], dst_ref=outs[a], send_sem=send_sems.at[a],
                                              recv_sem=recv_sems.at[a], device_id=(x, y, 1 - c), device_id_type=MESH)
            cp.start()
            sends.append(cp)
        for cp in sends:
            cp.wait()

    any_spec = pl.BlockSpec(memory_space=pl.ANY)
    return pl.pallas_call(
        body, name=name,
        out_shape=[jax.ShapeDtypeStruct(a.shape, a.dtype) for a in arrs],
        in_specs=[any_spec] * n, out_specs=[any_spec] * n,
        scratch_shapes=[pltpu.SemaphoreType.DMA((n,)), pltpu.SemaphoreType.DMA((n,))],
    )(*arrs)


def _row_tile(rows, cols):
    for tr in (1024, 512, 256, 128, 64, 32, 16, 8):
        if rows % tr == 0 and tr * cols * 4 <= (1 << 20):
            return tr
    return rows


def _partial_sum(g_full, recv, k_idx, name):
    _, r, c = g_full.shape
    tr = _row_tile(r, c)

    def body(k_ref, g_ref, r_ref, o_ref):
        del k_ref
        acc = g_ref[0].astype(F32)
        for j in range(3):
            acc = acc + r_ref[j].astype(F32)
        o_ref[...] = acc

    return pl.pallas_call(
        body, name=name,
        grid_spec=pltpu.PrefetchScalarGridSpec(
            num_scalar_prefetch=1, grid=(r // tr,),
            in_specs=[pl.BlockSpec((1, tr, c), lambda i, k: (k[0], i, 0)),
                      pl.BlockSpec((3, tr, c), lambda i, k: (0, i, 0))],
            out_specs=pl.BlockSpec((tr, c), lambda i, k: (i, 0))),
        out_shape=jax.ShapeDtypeStruct((r, c), F32),
        compiler_params=_cp("parallel"),
    )(k_idx, g_full, recv)


def _adamw(w3, parts, m3, v3, layer, prev, name):
    n_l, r, c = w3.shape
    tr = _row_tile(r, c)
    n_i = r // tr
    n_p = len(parts)
    c1 = 1.0 - ADAM_B1 ** ADAM_STEP
    c2 = 1.0 - ADAM_B2 ** ADAM_STEP
    stacked = [isinstance(p, tuple) for p in parts]

    def body(*refs):
        w_ref, m_ref, v_ref = refs[0:3]
        g_refs = refs[3:3 + n_p]
        go_ref, d_ref, mo_ref, vo_ref = refs[-4:]
        g = None
        for p in range(n_p):
            term = g_refs[p][0] if stacked[p] else g_refs[p][...]
            g = term if g is None else g + term
        w = w_ref[0]
        m = ADAM_B1 * m_ref[0] + (1.0 - ADAM_B1) * g
        v = ADAM_B2 * v_ref[0] + (1.0 - ADAM_B2) * (g * g)
        m_hat = m / c1
        v_hat = v / c2
        go_ref[0] = g
        d_ref[0] = -ADAM_LR * (m_hat / (jnp.sqrt(v_hat) + ADAM_EPS) + ADAM_WD * w)
        mo_ref[0] = m
        vo_ref[0] = v

    blk = pl.BlockSpec((1, tr, c), lambda i: (layer, i, 0))
    in_specs = [blk, blk, blk]
    args = [w3, m3, v3]
    for part in parts:
        if isinstance(part, tuple):
            in_specs.append(pl.BlockSpec((1, tr, c), functools.partial(lambda idx, i: (idx, i, 0), part[1])))
            args.append(part[0])
        else:
            in_specs.append(pl.BlockSpec((tr, c), lambda i: (i, 0)))
            args.append(part)
    aliases = {}
    if prev is not None:
        in_specs += [pl.BlockSpec(memory_space=pl.ANY)] * 4
        aliases = {len(args) + q: q for q in range(4)}
        args += list(prev)
    shp = jax.ShapeDtypeStruct((n_l, r, c), F32)
    return pl.pallas_call(
        body, name=name, grid=(n_i,), in_specs=in_specs, out_specs=[blk] * 4, out_shape=[shp] * 4,
        input_output_aliases=aliases, compiler_params=_cp("parallel"),
    )(*args)


_SMALL_W = 4096
_PACK_ROWS = 352
_N9 = N_MOD * D_MODEL


def _flat_pad(parts, total):
    flat = jnp.concatenate([p.reshape(-1) for p in parts])
    return jnp.concatenate([flat, jnp.zeros((total - flat.shape[0],), F32)])


def kernel(x, c, ctx, c_ctx, w_mod, b_mod, norm_g, ffn1_wi, ffn1_wo, ffn2_wi, ffn2_wo, w_in, w_a2_f, b_a_f, w_a2_b, b_a_b, sink, gla_g, w_out, w_pool, pool_scale, final_g, loss_target, m_c_ctx, m_w_mod, m_b_mod, m_norm_g, m_ffn1_wi, m_ffn1_wo, m_ffn2_wi, m_ffn2_wo, m_w_in, m_w_a2_f, m_b_a_f, m_w_a2_b, m_b_a_b, m_sink, m_gla_g, m_w_out, m_w_pool, m_pool_scale, m_final_g, v_c_ctx, v_w_mod, v_b_mod, v_norm_g, v_ffn1_wi, v_ffn1_wo, v_ffn2_wi, v_ffn2_wo, v_w_in, v_w_a2_f, v_b_a_f, v_w_a2_b, v_b_a_b, v_sink, v_gla_g, v_w_out, v_w_pool, v_pool_scale, v_final_g):
    t_len, l_ctx = x.shape[1], ctx.shape[1]
    tm = ROW_TILE
    pad = (-(t_len + l_ctx)) % tm
    rows0 = t_len + l_ctx + pad
    n_x = t_len // tm
    xi, yi, ci = _place()
    k_me = 2 * xi + yi
    me = 4 * xi + 2 * yi + ci
    mod_cols = w_mod.shape[2]
    n_grp = len(POOL_WINDOWS)

    small_w = _flat_pad([norm_g, w_a2_f, w_a2_b, pool_scale], _SMALL_W).reshape(_SMALL_W // 128, 128)
    shards = [ffn1_wi[0], ffn1_wi[1], ffn1_wo[0], ffn1_wo[1], ffn2_wi[0], ffn2_wi[1], ffn2_wo[0], ffn2_wo[1],
              w_in[0], w_out[0], w_pool[0].reshape(n_grp * w_pool.shape[2], POOL_GROUP)]

    c_all = _allgather_small(c.reshape(8, 128), "gather_cond").reshape(N_DEV, D_MODEL)
    c16 = jnp.concatenate([c_all, c_ctx[None], jnp.zeros((_CROWS - N_DEV - 1, D_MODEL), F32)], axis=0)
    bias_k = lax.dynamic_slice(b_mod, (0, k_me * mod_cols), (2, mod_cols)).reshape(2, 1, mod_cols)
    mm_k = _adaln_fwd(c16, w_mod, bias_k, "adaln_fwd")
    mm_all = _allgather_small(mm_k.reshape(-1, 128), "gather_mod")

    send_src = [s.astype(BF16) for s in shards] + [small_w]
    groups = ([11, 0], [2], [8, 9], [4], [6], [1], [3], [10, 5], [7])
    started = {}

    def gather_start(g, after):
        members = groups[g]
        sems, thru, zones, token = _gather_start([send_src[a] for a in members], (tuple(range(len(members))),),
                                                 after, "gather_start_%d" % g)
        started[g] = (sems[0], thru, zones)
        return token

    def gather_wait(g, after):
        (ssem, rsem), thru, zones = started[g]
        return dict(zip(groups[g], _gather_wait(thru, zones, ssem, rsem, after, "gather_wait_%d" % g)))

    tok = gather_start(0, mm_all)
    mm_all = mm_all.reshape(N_DEV, 2, _CROWS, mod_cols)
    mm_full = jnp.concatenate([mm_all[2 * k] for k in range(N_CHIPS)], axis=-1)
    mm_x = lax.dynamic_index_in_dim(mm_full, me, axis=1, keepdims=False)
    mm_c = mm_full[:, N_DEV]
    mods = [jnp.stack([mm_x[l].reshape(N_MOD, D_MODEL), mm_c[l].reshape(N_MOD, D_MODEL)]) + tok[0:1, 0:1]
            for l in range(2)]
    cs = _rope_tables(t_len, rows0)
    xcat = jnp.concatenate([x[0], ctx[0], jnp.zeros((pad, D_MODEL), F32)], axis=0)
    gathered = gather_wait(0, [mods[0], cs, xcat])
    sw = gathered[11].reshape(N_CHIPS, _SMALL_W)
    ng_n = norm_g.size
    a2_n = w_a2_f.size
    norm_g_full = jnp.concatenate([sw[k, :ng_n].reshape(norm_g.shape) for k in range(N_CHIPS)], axis=-1)
    w_a2_f_full = jnp.concatenate([sw[k, ng_n:ng_n + a2_n].reshape(w_a2_f.shape[1:]) for k in range(N_CHIPS)], axis=-1)
    w_a2_b_full = jnp.concatenate(
        [sw[k, ng_n + a2_n:ng_n + 2 * a2_n].reshape(w_a2_b.shape[1:]) for k in range(N_CHIPS)], axis=-1)
    pscale_full = jnp.concatenate(
        [sw[k, ng_n + 2 * a2_n:ng_n + 2 * a2_n + pool_scale.size] for k in range(N_CHIPS)]).reshape(1, D_MODEL)
    wg2, bias2 = _gate_weights(w_a2_f_full, b_a_f[0], w_a2_b_full, b_a_b[0])
    gla_g2 = gla_g.reshape(1, B_DV)
    final_g2 = final_g.reshape(1, D_MODEL)

    g3 = [norm_g_full[0], norm_g_full[1]]

    w1i, w1o, w2i, w2o = [None, None], [None, None], [None, None], [None, None]
    w1i[0] = gathered[0]
    mods_a = mods[0] + gather_start(1, w1i[0])[0:1, 0:1] + gather_start(2, w1i[0])[0:1, 0:1]
    x1, sv_a1, w1o[0] = _ffn_forward(xcat, g3[0], mods_a, 0, w1i[0],
                                     lambda s: (gather_wait(1, s)[2], gather_start(3, s)), n_x, "l0_ffn1")
    gathered = gather_wait(2, x1)
    w_in_full = jnp.concatenate([gathered[8][k] for k in range(N_CHIPS)], axis=1)
    wcat = _w_in_to_cat(w_in_full)
    w_out_full = gathered[9].reshape(D_MODEL, D_MODEL)
    mods_a = mods[0] + gather_start(4, x1)[0:1, 0:1]
    pace_group = {"proj": 5, "attn": 6, "gla": 7}
    x2, sv_am = _mixer_ab_forward(x1, g3[0], mods_a, wcat, wg2, bias2, sink[0], gla_g2, w_out_full, cs,
                                  t_len, l_ctx, n_x, lambda tag, res_: gather_start(pace_group[tag], res_))
    mods_a = mods[0] + gather_start(8, x2)[0:1, 0:1]
    w2i[0], w2o[0] = gather_wait(3, x2)[4], gather_wait(4, x2)[6]
    x3, sv_a2, _ = _ffn_forward(x2, g3[0], mods_a, 2, w2i[0], lambda s: (w2o[0], None), n_x, "l0_ffn2")
    w1i[1], w1o[1] = gather_wait(5, x3)[1], gather_wait(6, x3)[3]
    x4, sv_b1, _ = _ffn_forward(x3, g3[1], mods[1], 0, w1i[1], lambda s: (w1o[1], None), n_x, "l1_ffn1")
    gathered = gather_wait(7, x4)
    w2i[1] = gathered[5]
    wp_full = gathered[10].reshape(N_CHIPS, n_grp, -1, POOL_GROUP).transpose(1, 0, 2, 3).reshape(
        n_grp, POOL_GROUP, POOL_GROUP)
    x5, sv_bm = _mixer_pool_forward(x4, g3[1], mods[1], wp_full, pscale_full, t_len)
    x6, sv_b2, w2o[1] = _ffn_forward(x5, g3[1], mods[1], 2, w2i[1], lambda s: (gather_wait(8, s)[7], None), n_x,
                                     "l1_ffn2")
    dx6, loss_part, d_final_g = _final_loss(x6, final_g2, loss_target[0], "final_loss")
    loss = lax.psum(loss_part[0, 0], ("x", "y", "c"))

    sent = []

    def sender(weight, layer):
        def send(grad, tag):
            nm = "%s_%s_%d" % (weight, tag, layer)
            ssem, rsem, thru, lands, token = _scatter_start([grad], "scatter_start_" + nm)
            sent.append((nm, weight + "_" + tag if tag else weight, layer, thru, lands, ssem, rsem))
            return token[0:1, 0:1]
        return send

    dx5, st_b2, dg_b2 = _ffn_backward(dx6, sv_b2, g3[1], mods[1], 2, w2i[1], w2o[1], n_x, sender("ffn2", 1),
                                      "l1_ffn2_b")
    dx4, st_bm, dg_bm, d_pscale, d_wp = _mixer_pool_backward(dx5, sv_bm, g3[1], mods[1], wp_full, pscale_full, t_len)
    d_wp4 = d_wp.reshape(n_grp, N_CHIPS, -1, POOL_GROUP).transpose(1, 0, 2, 3).reshape(N_CHIPS, -1, POOL_GROUP)
    mods1 = mods[1] + sender("w_pool", 0)(d_wp4, "")
    dx3, st_b1, dg_b1 = _ffn_backward(dx4, sv_b1, g3[1], mods1, 0, w1i[1], w1o[1], n_x, sender("ffn1", 1),
                                      "l1_ffn1_b")
    dx2, st_a2, dg_a2 = _ffn_backward(dx3, sv_a2, g3[0], mods[0], 2, w2i[0], w2o[0], n_x, sender("ffn2", 0),
                                      "l0_ffn2_b")
    dx1, st_am, dg_am, d_wcat, d_wg2, d_bias2, d_sink, d_glag, d_wout = _mixer_ab_backward(
        dx2, sv_am, g3[0], mods[0], wcat, wg2, bias2, sink[0], gla_g2, w_out_full, cs, t_len, l_ctx, n_x)
    d_w_in4 = _cat_to_w_in(d_wcat).reshape(D_MODEL, N_CHIPS, -1).transpose(1, 0, 2)
    mods0 = mods[0] + sender("w_in", 0)(d_w_in4, "") + sender("w_out", 0)(d_wout.reshape(N_CHIPS, -1, D_MODEL), "")
    dx0, st_a1, dg_a1 = _ffn_backward(dx1, sv_a1, g3[0], mods0, 0, w1i[0], w1o[0], n_x, sender("ffn1", 0),
                                      "l0_ffn1_b")
    grad_x = dx0[:t_len][None]

    def as3(a):
        n_l = a.shape[0] if a.ndim == 3 else 1
        return a.reshape(n_l, -1, a.shape[-1])

    res = {}
    big_w = {"ffn1_wi": (ffn1_wi, m_ffn1_wi, v_ffn1_wi), "ffn1_wo": (ffn1_wo, m_ffn1_wo, v_ffn1_wo),
             "ffn2_wi": (ffn2_wi, m_ffn2_wi, v_ffn2_wi), "ffn2_wo": (ffn2_wo, m_ffn2_wo, v_ffn2_wo),
             "w_in": (w_in, m_w_in, v_w_in), "w_out": (w_out, m_w_out, v_w_out), "w_pool": (w_pool, m_w_pool, v_w_pool)}
    k_idx = k_me.reshape(1).astype(jnp.int32)
    chain = dx0
    for lo, hi in ((0, 2), (2, 5), (5, 7), (7, 9), (9, 11)):
        partial = []
        for nm, wname, layer, thru, lands, ssem, rsem in sent[lo:hi]:
            mine, recv = _scatter_wait(thru, lands, ssem, rsem, chain, "scatter_wait_" + nm)
            partial.append(_partial_sum(mine[0], recv[0], k_idx, "partial_sum_" + nm))
        other = _swap_sibling(partial, "swap_partials_%d" % lo)
        for (nm, wname, layer, _, _, _, _), p, q in zip(sent[lo:hi], partial, other):
            w, m, v = big_w[wname]
            res[wname] = _adamw(as3(w), [p, q], as3(m), as3(v), layer, res.get(wname),
                                "adamw_%s_%d" % (wname, layer))
            chain = res[wname][3]

    def mod_row(st1, dg1, stm, dgm, st2, dg2, s):
        return jnp.concatenate([st1[s, 0], st1[s, 1], dg1[s, 0], stm[s, 0], stm[s, 1], dgm[s, 0],
                                st2[s, 0], st2[s, 1], dg2[s, 0]])

    dg_bm2 = jnp.concatenate([dg_bm, jnp.zeros_like(dg_bm)], axis=0)[:, None, :]
    d_mm_x0 = mod_row(st_a1, dg_a1, st_am, dg_am, st_a2, dg_a2, 0)
    d_mm_x1 = mod_row(st_b1, dg_b1, st_bm, dg_bm2, st_b2, dg_b2, 0)
    d_mm_c0 = mod_row(st_a1, dg_a1, st_am, dg_am, st_a2, dg_a2, 1)
    d_norm_g = jnp.stack([jnp.stack([st[0, 2] + st[1, 2] for st in (st_a1, st_am, st_a2)]),
                          jnp.stack([st[0, 2] + st[1, 2] for st in (st_b1, st_bm, st_b2)])])
    rk = B_GATE_RANK
    pack = _flat_pad([d_mm_x0, d_mm_x1, d_mm_c0, d_norm_g, d_bias2, d_wg2[0:rk, 0:256], d_wg2[rk:2 * rk, 256:512],
                      d_sink[:, 0], jnp.zeros((120,), F32), d_glag, d_pscale, d_final_g],
                     _PACK_ROWS * 128).reshape(_PACK_ROWS, 128)
    pack = pack + 0.0 * chain[0, 0:1, 0:1]
    pack_all = _allgather_small(pack, "gather_small_grads")
    tot = _sum_devices(pack_all, "sum_small_grads").reshape(-1)
    rows_all = pack_all.reshape(N_DEV, -1)
    o = 3 * _N9
    g_norm_g_full = tot[o:o + 6 * D_MODEL].reshape(2, 3, D_MODEL)
    o += 6 * D_MODEL
    g_bias2 = tot[o:o + 512]
    o += 512
    g_w_a2_f_full = tot[o:o + rk * 256].reshape(rk, 256)
    o += rk * 256
    g_w_a2_b_full = tot[o:o + rk * 256].reshape(rk, 256)
    o += rk * 256
    g_sink = tot[o:o + A_HEADS]
    o += 128
    g_gla_g = tot[o:o + B_DV]
    o += B_DV
    g_pscale_full = tot[o:o + D_MODEL]
    o += D_MODEL
    g_final_g = tot[o:o + D_MODEL]
    d_mmc_tot = tot[2 * _N9:3 * _N9]
    g_b_mod = jnp.stack([tot[0:_N9] + d_mmc_tot, tot[_N9:2 * _N9]])

    zrows = jnp.zeros((_CROWS - N_DEV - 1, _N9), F32)
    d16 = jnp.stack([jnp.concatenate([rows_all[:, 0:_N9], d_mmc_tot[None], zrows], axis=0),
                     jnp.concatenate([rows_all[:, _N9:2 * _N9], jnp.zeros((1, _N9), F32), zrows], axis=0)])
    d16_k = lax.dynamic_slice(d16, (0, 0, k_me * mod_cols), (2, _CROWS, mod_cols))
    dmmc_k = lax.dynamic_slice(d_mmc_tot, (k_me * mod_cols,), (mod_cols,)).reshape(1, mod_cols)
    g_w_mod, c_part = _adaln_bwd(c16, d16_k, w_mod, dmmc_k, "adaln_bwd")
    c_parts = _allgather_small(c_part.reshape(8, 128), "gather_cctx")
    g_c_ctx = _cctx_grad(c_parts, c_ctx.reshape(8, 128), "cctx_grad").reshape(D_MODEL)

    def small(w, g, m, v, shape3, nm):
        return [o_.reshape(w.shape) for o_ in _adamw(w.reshape(shape3), [g.reshape(shape3[1:])],
                                                    m.reshape(shape3), v.reshape(shape3), 0, None, "adamw_" + nm)]

    def own(a, axis, size):
        return lax.dynamic_slice_in_dim(a, k_me * size, size, axis=axis)

    res["c_ctx"] = small(c_ctx, g_c_ctx, m_c_ctx, v_c_ctx, (1, 8, 128), "c_ctx")
    upd = _adamw(w_mod, [(g_w_mod, 1)], m_w_mod, v_w_mod, 1, None, "adamw_w_mod_1")
    res["w_mod"] = _adamw(w_mod, [(g_w_mod, 0)], m_w_mod, v_w_mod, 0, upd, "adamw_w_mod_0")
    res["b_mod"] = small(b_mod, g_b_mod, m_b_mod, v_b_mod, (1, 2, _N9), "b_mod")
    res["norm_g"] = small(norm_g, own(g_norm_g_full, 2, norm_g.shape[2]), m_norm_g, v_norm_g,
                          (1, 6, norm_g.shape[2]), "norm_g")
    res["w_a2_f"] = small(w_a2_f, own(g_w_a2_f_full, 1, w_a2_f.shape[2]), m_w_a2_f, v_w_a2_f,
                          (1, rk, w_a2_f.shape[2]), "w_a2_f")
    res["b_a_f"] = small(b_a_f, g_bias2[0:256], m_b_a_f, v_b_a_f, (1, 1, 256), "b_a_f")
    res["w_a2_b"] = small(w_a2_b, own(g_w_a2_b_full, 1, w_a2_b.shape[2]), m_w_a2_b, v_w_a2_b,
                          (1, rk, w_a2_b.shape[2]), "w_a2_b")
    res["b_a_b"] = small(b_a_b, g_bias2[256:512], m_b_a_b, v_b_a_b, (1, 1, 256), "b_a_b")
    res["sink"] = small(sink, g_sink, m_sink, v_sink, (1, 1, A_HEADS), "sink")
    res["gla_g"] = small(gla_g, g_gla_g, m_gla_g, v_gla_g, (1, 1, B_DV), "gla_g")
    res["pool_scale"] = small(pool_scale, own(g_pscale_full, 0, pool_scale.shape[1]), m_pool_scale, v_pool_scale,
                              (1, 1, pool_scale.shape[1]), "pool_scale")
    res["final_g"] = small(final_g, g_final_g, m_final_g, v_final_g, (1, 8, 128), "final_g")
    for wname, (w, _, _) in big_w.items():
        res[wname] = [o_.reshape(w.shape) for o_ in res[wname]]

    names = ["c_ctx", "w_mod", "b_mod", "norm_g", "ffn1_wi", "ffn1_wo", "ffn2_wi", "ffn2_wo", "w_in", "w_a2_f",
             "b_a_f", "w_a2_b", "b_a_b", "sink", "gla_g", "w_out", "w_pool", "pool_scale", "final_g"]
    outs = [loss, grad_x]
    for field in range(4):
        outs += [res[nm][field] for nm in names]
    return tuple(outs)
```

```python
import functools

import jax
import jax.numpy as jnp
import numpy as np
from jax import lax
from jax.experimental import pallas as pl
from jax.experimental.pallas import tpu as pltpu

F32 = jnp.float32
BF16 = jnp.bfloat16

D_MODEL = 1024
N_MOD = 9
D_FF = 2816
RMS_EPS = 1e-6
A_HEADS = 8
A_KV_HEADS = 2
A_HEAD_DIM = 64
WINDOW = 128
ROPE_BASE = 10000.0
GRID_W = 64
B_HEADS = 4
B_DK = 64
B_DV = 128
B_GATE_RANK = 16
B_GATE_NORM = 16.0
B_CHUNK = 64
POOL_WINDOWS = (2, 4, 8, 16)
POOL_GROUP = D_MODEL // len(POOL_WINDOWS)
PROJ_DIM = 2336

ADAM_LR = 0.001
ADAM_B1 = 0.9
ADAM_B2 = 0.999
ADAM_EPS = 1e-08
ADAM_WD = 0.01
ADAM_STEP = 10

N_CHIPS = 4
N_DEV = 8
ROW_TILE = 512
VMEM_LIMIT_BYTES = 56 * 1024 * 1024
MESH = pl.DeviceIdType.MESH

ZC_Q, ZC_QK, ZC_V, ZC_R, ZC_KV, ZC_G, ZC_W = 0, 512, 1024, 1536, 2048, 2304, 2432


def _cp(*sem):
    return pltpu.CompilerParams(dimension_semantics=sem if sem else None, vmem_limit_bytes=VMEM_LIMIT_BYTES)


def _dot(a, b):
    return jnp.dot(a, b, preferred_element_type=F32)


def _dot_nt(a, b):
    return lax.dot_general(a, b, (((1,), (1,)), ((), ())), preferred_element_type=F32)


def _dot_tn(a, b):
    return lax.dot_general(a, b, (((0,), (0,)), ((), ())), preferred_element_type=F32)


def _dot_hi(a, b):
    return jnp.dot(a, b, preferred_element_type=F32, precision=lax.Precision.HIGHEST)


def _dot_tn_hi(a, b):
    return lax.dot_general(a, b, (((0,), (0,)), ((), ())), preferred_element_type=F32,
                           precision=lax.Precision.HIGHEST)


def _sigmoid(x):
    return 1.0 / (1.0 + jnp.exp(-x))


def _stream_of(i, n_x):
    return jnp.where(i >= n_x, 1, 0)


def _rms_mod_fwd(x, g3, mods, j, n_x, out_dtype, name):
    rows = x.shape[0]
    tm = ROW_TILE
    n_i = rows // tm

    def body(x_ref, g_ref, m_ref, o_ref):
        xv = x_ref[...]
        r = lax.rsqrt(jnp.mean(xv * xv, axis=-1, keepdims=True) + RMS_EPS)
        g = g_ref[j:j + 1, :]
        shift = m_ref[0, 3 * j:3 * j + 1, :]
        scale = m_ref[0, 3 * j + 1:3 * j + 2, :]
        o_ref[...] = (((xv * r) * g) * (1.0 + scale) + shift).astype(out_dtype)

    return pl.pallas_call(
        body, name=name, grid=(n_i,),
        in_specs=[pl.BlockSpec((tm, D_MODEL), lambda i: (i, 0)),
                  pl.BlockSpec((3, D_MODEL), lambda i: (0, 0)),
                  pl.BlockSpec((1, N_MOD, D_MODEL), lambda i: (_stream_of(i, n_x), 0, 0))],
        out_specs=pl.BlockSpec((tm, D_MODEL), lambda i: (i, 0)),
        out_shape=jax.ShapeDtypeStruct((rows, D_MODEL), out_dtype),
        compiler_params=_cp("parallel"),
    )(x, g3, mods)


def _rms_mod_bwd_tail(dh, xv, g, scale, stream, acc_ref, first):
    r = lax.rsqrt(jnp.mean(xv * xv, axis=-1, keepdims=True) + RMS_EPS)
    xhat = xv * r
    t1 = jnp.sum(dh, axis=0, keepdims=True)
    t2 = jnp.sum(dh * xhat, axis=0, keepdims=True)
    stats = jnp.concatenate([t1, t2 * g, t2 * (1.0 + scale)], axis=0)

    @pl.when(first)
    def _():
        acc_ref[...] = jnp.zeros_like(acc_ref)

    acc_ref[pl.ds(stream, 1)] += stats[None]
    dxh = dh * (g * (1.0 + scale))
    return r * (dxh - xhat * jnp.mean(dxh * xhat, axis=-1, keepdims=True))


def _ffn_up(x, g3, mods, jmod, n_x, w4, name):
    rows = x.shape[0]
    h = w4.shape[2]
    tm = ROW_TILE
    n_i = rows // tm

    def body(x_ref, g_ref, m_ref, wa_ref, wu_ref, hn_ref, au_ref, s_ref):
        xv = x_ref[...]
        r = lax.rsqrt(jnp.mean(xv * xv, axis=-1, keepdims=True) + RMS_EPS)
        g = g_ref[jmod:jmod + 1, :]
        shift = m_ref[0, 3 * jmod:3 * jmod + 1, :]
        scale = m_ref[0, 3 * jmod + 1:3 * jmod + 2, :]
        hv = (((xv * r) * g) * (1.0 + scale) + shift).astype(BF16)

        @pl.when(pl.program_id(0) == 0)
        def _():
            hn_ref[...] = hv

        a = _dot(hv, wa_ref[0])
        u = _dot(hv, wu_ref[0])
        sg = _sigmoid(a)
        silu = a * sg
        au_ref[0] = (u * (sg * (1.0 + a * (1.0 - sg)))).astype(BF16)
        au_ref[1] = silu.astype(BF16)
        s_ref[...] = (silu * u).astype(BF16)

    return pl.pallas_call(
        body, name=name, grid=(2, n_i),
        in_specs=[pl.BlockSpec((tm, D_MODEL), lambda j, i: (i, 0)),
                  pl.BlockSpec((3, D_MODEL), lambda j, i: (0, 0)),
                  pl.BlockSpec((1, N_MOD, D_MODEL), lambda j, i: (_stream_of(i, n_x), 0, 0)),
                  pl.BlockSpec((1, D_MODEL, h), lambda j, i: (j, 0, 0)),
                  pl.BlockSpec((1, D_MODEL, h), lambda j, i: (j + 2, 0, 0))],
        out_specs=[pl.BlockSpec((tm, D_MODEL), lambda j, i: (jnp.where(j == 0, i, n_i - 1), 0)),
                   pl.BlockSpec((2, tm, h), lambda j, i: (0, i, j)),
                   pl.BlockSpec((tm, h), lambda j, i: (i, j))],
        out_shape=[jax.ShapeDtypeStruct((rows, D_MODEL), BF16),
                   jax.ShapeDtypeStruct((2, rows, 2 * h), BF16),
                   jax.ShapeDtypeStruct((rows, 2 * h), BF16)],
        compiler_params=_cp("arbitrary", "arbitrary"),
    )(x, g3, mods, w4, w4)


def _matmul_resid(a, w, xres, mods, gate_idx, coef, n_x, rows, name):
    k = a.shape[1]
    tm = ROW_TILE
    n_i = rows // tm

    def body(a_ref, w_ref, x_ref, m_ref, o_ref, f_ref):
        f = _dot(a_ref[...], w_ref[...])
        gate = m_ref[0, gate_idx:gate_idx + 1, :]
        f_ref[...] = f
        o_ref[...] = x_ref[...] + (coef * gate) * f

    return pl.pallas_call(
        body, name=name, grid=(n_i,),
        in_specs=[pl.BlockSpec((tm, k), lambda i: (i, 0)),
                  pl.BlockSpec((k, D_MODEL), lambda i: (0, 0)),
                  pl.BlockSpec((tm, D_MODEL), lambda i: (i, 0)),
                  pl.BlockSpec((1, N_MOD, D_MODEL), lambda i: (_stream_of(i, n_x), 0, 0))],
        out_specs=[pl.BlockSpec((tm, D_MODEL), lambda i: (i, 0)),
                   pl.BlockSpec((tm, D_MODEL), lambda i: (i, 0))],
        out_shape=[jax.ShapeDtypeStruct((rows, D_MODEL), F32),
                   jax.ShapeDtypeStruct((rows, D_MODEL), F32)],
        compiler_params=_cp("parallel"),
    )(a, w, xres, mods)


def _gate_dy(dout, f, mods, gate_idx, coef, n_x, rows, name):
    tm = ROW_TILE
    n_i = rows // tm

    def body(d_ref, f_ref, m_ref, dy_ref, acc_ref):
        i = pl.program_id(0)
        dv = d_ref[...]
        gate = m_ref[0, gate_idx:gate_idx + 1, :]
        dy_ref[...] = (dv * (coef * gate)).astype(BF16)

        @pl.when(i == 0)
        def _():
            acc_ref[...] = jnp.zeros_like(acc_ref)

        part = coef * jnp.sum(dv * f_ref[...], axis=0, keepdims=True)
        acc_ref[pl.ds(_stream_of(i, n_x), 1)] += part[None]

    return pl.pallas_call(
        body, name=name, grid=(n_i,),
        in_specs=[pl.BlockSpec((tm, D_MODEL), lambda i: (i, 0)),
                  pl.BlockSpec((tm, D_MODEL), lambda i: (i, 0)),
                  pl.BlockSpec((1, N_MOD, D_MODEL), lambda i: (_stream_of(i, n_x), 0, 0))],
        out_specs=[pl.BlockSpec((tm, D_MODEL), lambda i: (i, 0)),
                   pl.BlockSpec((2, 1, D_MODEL), lambda i: (0, 0, 0))],
        out_shape=[jax.ShapeDtypeStruct((rows, D_MODEL), BF16),
                   jax.ShapeDtypeStruct((2, 1, D_MODEL), F32)],
        compiler_params=_cp("arbitrary"),
    )(dout, f, mods)


def _ffn_bwd_dz(dout, f, mods, gate_idx, coef, n_x, wo2, au, name):
    rows = dout.shape[0]
    h = wo2.shape[1]
    tm = ROW_TILE
    n_i = rows // tm

    def body(d_ref, f_ref, m_ref, wo_ref, au_ref, dy_ref, dz_ref, acc_ref):
        j, i = pl.program_id(0), pl.program_id(1)
        dv = d_ref[...]
        gate = m_ref[0, gate_idx:gate_idx + 1, :]
        dyb = (dv * (coef * gate)).astype(BF16)

        @pl.when((j == 0) & (i == 0))
        def _():
            acc_ref[...] = jnp.zeros_like(acc_ref)

        @pl.when(j == 0)
        def _():
            dy_ref[...] = dyb
            part = coef * jnp.sum(dv * f_ref[...], axis=0, keepdims=True)
            acc_ref[pl.ds(_stream_of(i, n_x), 1)] += part[None]

        ds = _dot_nt(dyb, wo_ref[0])
        dz_ref[0] = (ds * au_ref[0].astype(F32)).astype(BF16)
        dz_ref[1] = (ds * au_ref[1].astype(F32)).astype(BF16)

    return pl.pallas_call(
        body, name=name, grid=(2, n_i),
        in_specs=[pl.BlockSpec((tm, D_MODEL), lambda j, i: (i, 0)),
                  pl.BlockSpec((tm, D_MODEL), lambda j, i: (jnp.where(j == 0, i, n_i - 1), 0)),
                  pl.BlockSpec((1, N_MOD, D_MODEL), lambda j, i: (_stream_of(i, n_x), 0, 0)),
                  pl.BlockSpec((1, h, D_MODEL), lambda j, i: (j, 0, 0)),
                  pl.BlockSpec((2, tm, h), lambda j, i: (0, i, j))],
        out_specs=[pl.BlockSpec((tm, D_MODEL), lambda j, i: (jnp.where(j == 0, i, n_i - 1), 0)),
                   pl.BlockSpec((2, tm, h), lambda j, i: (0, i, j)),
                   pl.BlockSpec((2, 1, D_MODEL), lambda j, i: (0, 0, 0))],
        out_shape=[jax.ShapeDtypeStruct((rows, D_MODEL), BF16),
                   jax.ShapeDtypeStruct((2, rows, 2 * h), BF16),
                   jax.ShapeDtypeStruct((2, 1, D_MODEL), F32)],
        compiler_params=_cp("arbitrary", "arbitrary"),
    )(dout, f, mods, wo2, au)


def _token_tile(rows):
    for tk in (2048, 1536, 1024):
        if rows % tk == 0:
            return tk
    return ROW_TILE


def _matmul_tn(a, b, a_spec, b_spec, out_shape, out_spec, grid, name):
    nd_a = len(a_spec.block_shape)
    nd_b = len(b_spec.block_shape)
    nd_o = len(out_spec.block_shape)
    k_axis = len(grid) - 1
    n_k = grid[k_axis]

    def body(a_ref, b_ref, o_ref, acc_ref):
        av = a_ref[(0,) * (nd_a - 2)]
        bv = b_ref[(0,) * (nd_b - 2)]
        part = _dot_tn(av, bv)
        k = pl.program_id(k_axis)

        @pl.when(k == 0)
        def _():
            acc_ref[...] = part

        @pl.when(k > 0)
        def _():
            acc_ref[...] += part

        @pl.when(k == n_k - 1)
        def _():
            o_ref[(0,) * (nd_o - 2)] = acc_ref[...].astype(BF16)

    return pl.pallas_call(
        body, name=name, grid=grid, in_specs=[a_spec, b_spec], out_specs=out_spec,
        out_shape=jax.ShapeDtypeStruct(out_shape, BF16),
        scratch_shapes=[pltpu.VMEM(tuple(out_spec.block_shape[-2:]), F32)],
        compiler_params=_cp(*(("arbitrary",) * len(grid))),
    )(a, b)


def _bwd_dx(pairs, x, dres, dres_tiles, g3, mods, j, n_x, name, out_tiles=None):
    rows = x.shape[0]
    tm = ROW_TILE
    n_i = rows // tm
    n_o = n_i if out_tiles is None else out_tiles
    n_p = len(pairs)
    nds = [(len(p[1].block_shape), len(p[3].block_shape)) for p in pairs]

    def body(*refs):
        dz_refs = refs[0:2 * n_p:2]
        w_refs = refs[1:2 * n_p:2]
        x_ref, dres_ref, g_ref, m_ref, dx_ref, acc_ref = refs[2 * n_p:]
        i = pl.program_id(0)
        dh = None
        for p in range(n_p):
            dzv = dz_refs[p][(0,) * (nds[p][0] - 2)]
            wv = w_refs[p][(0,) * (nds[p][1] - 2)]
            part = _dot_nt(dzv, wv)
            dh = part if dh is None else dh + part
        g = g_ref[j:j + 1, :]
        scale = m_ref[0, 3 * j + 1:3 * j + 2, :]
        dx = _rms_mod_bwd_tail(dh, x_ref[...], g, scale, _stream_of(i, n_x), acc_ref, i == 0)
        dres_v = jnp.where(i < dres_tiles, dres_ref[...], 0.0)

        @pl.when(i < n_o)
        def _():
            dx_ref[...] = dres_v + dx

    in_specs, args = [], []
    for dz, dz_spec, w, w_spec in pairs:
        in_specs += [dz_spec, w_spec]
        args += [dz, w]
    in_specs += [pl.BlockSpec((tm, D_MODEL), lambda i: (i, 0)),
                 pl.BlockSpec((tm, D_MODEL), lambda i: (jnp.minimum(i, dres_tiles - 1), 0)),
                 pl.BlockSpec((3, D_MODEL), lambda i: (0, 0)),
                 pl.BlockSpec((1, N_MOD, D_MODEL), lambda i: (_stream_of(i, n_x), 0, 0))]
    args += [x, dres, g3, mods]
    return pl.pallas_call(
        body, name=name, grid=(n_i,), in_specs=in_specs,
        out_specs=[pl.BlockSpec((tm, D_MODEL), lambda i: (jnp.minimum(i, n_o - 1), 0)),
                   pl.BlockSpec((2, 3, D_MODEL), lambda i: (0, 0, 0))],
        out_shape=[jax.ShapeDtypeStruct((n_o * tm, D_MODEL), F32),
                   jax.ShapeDtypeStruct((2, 3, D_MODEL), F32)],
        compiler_params=_cp("arbitrary"),
    )(*args)


def _ffn_forward(x, g3, mods, j, w4_in, w4_out_of, n_x, name):
    rows = x.shape[0]
    hn, au, s = _ffn_up(x, g3, mods, j, n_x, w4_in, name + "_up")
    w4_out, dep = w4_out_of(s)
    if dep is not None:
        mods = mods + dep[0:1, 0:1]
    wo = w4_out.reshape(D_FF, D_MODEL)
    out, f = _matmul_resid(s, wo, x, mods, 3 * j + 2, 0.5, n_x, rows, name + "_down")
    return out, (x, hn, au, s, f), w4_out


def _ffn_backward(dout, saved, g3, mods, j, w4_in, w4_out, n_x, send, name, out_tiles=None):
    x, hn, au, s, f = saved
    rows = x.shape[0]
    tm = ROW_TILE
    n_i = rows // tm
    h = w4_in.shape[2]
    wo2 = w4_out.reshape(2, h, D_MODEL)
    dy, dz, dgate = _ffn_bwd_dz(dout, f, mods, 3 * j + 2, 0.5, n_x, wo2, au, name + "_dz")
    tk = _token_tile(rows)
    n_k = rows // tk
    d_wi = _matmul_tn(
        hn, dz, pl.BlockSpec((tk, D_MODEL), lambda q, k: (k, 0)),
        pl.BlockSpec((1, tk, h), lambda q, k: (q // 2, k, q % 2)),
        (4, D_MODEL, h), pl.BlockSpec((1, D_MODEL, h), lambda q, k: (q, 0, 0)), (4, n_k), name + "_dwi")
    mods = mods + send(d_wi, "wi")
    d_wo = _matmul_tn(
        s, dy, pl.BlockSpec((tk, h), lambda n, k: (k, n)), pl.BlockSpec((tk, D_MODEL), lambda n, k: (k, 0)),
        (D_FF, D_MODEL), pl.BlockSpec((h, D_MODEL), lambda n, k: (n, 0)), (2, n_k), name + "_dwo")
    mods = mods + send(d_wo.reshape(w4_out.shape), "wo")
    pairs = [(dz, pl.BlockSpec((1, tm, h), functools.partial(lambda q, i: (q // 2, i, q % 2), q)),
              w4_in, pl.BlockSpec((1, D_MODEL, h), functools.partial(lambda q, i: (q, 0, 0), q)))
             for q in range(4)]
    dx, stats = _bwd_dx(pairs, x, dout, n_i, g3, mods, j, n_x, name + "_dx", out_tiles)
    return dx, stats, dgate


def _matmul_nt(a, w, name):
    rows, k = a.shape
    n = w.shape[0]
    tm = ROW_TILE

    def body(a_ref, w_ref, o_ref):
        o_ref[...] = _dot_nt(a_ref[...], w_ref[...])

    return pl.pallas_call(
        body, name=name, grid=(rows // tm,),
        in_specs=[pl.BlockSpec((tm, k), lambda i: (i, 0)), pl.BlockSpec((n, k), lambda i: (0, 0))],
        out_specs=pl.BlockSpec((tm, n), lambda i: (i, 0)),
        out_shape=jax.ShapeDtypeStruct((rows, n), F32),
        compiler_params=_cp("parallel"),
    )(a, w)


def _rope_tables(t_len, rows):
    n = A_HEAD_DIM // 4
    freqs = ROPE_BASE ** (-jnp.arange(n, dtype=F32) / n)
    t = jnp.arange(t_len)
    ang_r = (t // GRID_W).astype(F32)[:, None] * freqs
    ang_c = (t % GRID_W).astype(F32)[:, None] * freqs
    cos = jnp.concatenate([jnp.cos(ang_r), jnp.cos(ang_r), jnp.cos(ang_c), jnp.cos(ang_c)], axis=1)
    sin = jnp.concatenate([-jnp.sin(ang_r), jnp.sin(ang_r), -jnp.sin(ang_c), jnp.sin(ang_c)], axis=1)
    cos = jnp.concatenate([cos, jnp.ones((rows - t_len, A_HEAD_DIM), F32)], axis=0)
    sin = jnp.concatenate([sin, jnp.zeros((rows - t_len, A_HEAD_DIM), F32)], axis=0)
    return jnp.concatenate([cos, cos, sin, sin], axis=1)


def _swap16(x):
    n = x.shape[1]
    lane = lax.broadcasted_iota(jnp.int32, x.shape, 1)
    first = jnp.bitwise_and(lane, 16) == 0
    return jnp.where(first, pltpu.roll(x, n - 16, 1), pltpu.roll(x, 16, 1))


def _log_sigmoid(x):
    return jnp.minimum(x, 0.0) - jnp.log(1.0 + jnp.exp(-jnp.abs(x)))


def _proj_fwd(h, wcat, wg2, bias2, cs, name):
    rows = h.shape[0]
    tm = ROW_TILE

    def body(h_ref, w_ref, wg_ref, b_ref, cs_ref, zc_ref, la_ref):
        z = _dot(h_ref[...], w_ref[...])
        cos = cs_ref[:, 0:128]
        sin = cs_ref[:, 128:256]
        cosq = jnp.concatenate([cos] * 4, axis=1)
        sinq = jnp.concatenate([sin] * 4, axis=1)
        q = z[:, ZC_Q:ZC_QK]
        zc_ref[:, ZC_Q:ZC_QK] = q * cosq + _swap16(q) * sinq
        zc_ref[:, ZC_QK:ZC_KV] = z[:, ZC_QK:ZC_KV]
        kk = z[:, ZC_KV:ZC_KV + 128]
        zc_ref[:, ZC_KV:ZC_KV + 128] = kk * cos + _swap16(kk) * sin
        zc_ref[:, ZC_KV + 128:ZC_W] = z[:, ZC_KV + 128:ZC_W]
        zg = z[:, ZC_G:ZC_W]
        pre = _dot(zg.astype(BF16), wg_ref[...]) + b_ref[...]
        la_ref[...] = _log_sigmoid(pre) / B_GATE_NORM

    return pl.pallas_call(
        body, name=name, grid=(rows // tm,),
        in_specs=[pl.BlockSpec((tm, D_MODEL), lambda i: (i, 0)),
                  pl.BlockSpec((D_MODEL, ZC_W), lambda i: (0, 0)),
                  pl.BlockSpec((128, 512), lambda i: (0, 0)),
                  pl.BlockSpec((1, 512), lambda i: (0, 0)),
                  pl.BlockSpec((tm, 256), lambda i: (i, 0))],
        out_specs=[pl.BlockSpec((tm, ZC_W), lambda i: (i, 0)),
                   pl.BlockSpec((tm, 512), lambda i: (i, 0))],
        out_shape=[jax.ShapeDtypeStruct((rows, ZC_W), F32),
                   jax.ShapeDtypeStruct((rows, 512), F32)],
        compiler_params=_cp("parallel"),
    )(h, wcat, wg2, bias2, cs)


_QB = WINDOW


def _attn_specs(t_len, l_ctx):
    nb = t_len // _QB
    kvb = ZC_KV // 256
    return [pl.BlockSpec(memory_space=pltpu.SMEM),
            pl.BlockSpec((_QB, 512), lambda n: (n, 0)),
            pl.BlockSpec((_QB, 256), lambda n: (jnp.maximum(n - 1, 0), kvb)),
            pl.BlockSpec((_QB, 256), lambda n: (n, kvb)),
            pl.BlockSpec((_QB, 256), lambda n: (n + 1, kvb)),
            pl.BlockSpec((l_ctx, 256), lambda n: (t_len // l_ctx, kvb))], nb


_HEAD_PAIRS = ((0, 1), (2, 3))


def _attn_keys(kp, kc, kn, kx, g):
    hd = A_HEAD_DIM
    ks = slice(g * hd, (g + 1) * hd)
    vs = slice(128 + g * hd, 128 + (g + 1) * hd)
    kb = jnp.concatenate([kp[:, ks], kc[:, ks], kn[:, ks]], axis=0).astype(BF16)
    vb = jnp.concatenate([kp[:, vs], kc[:, vs], kn[:, vs]], axis=0).astype(BF16)
    return kb, vb, kx[:, ks].astype(BF16), kx[:, vs].astype(BF16)


def _attn_probs(n, t_len, sink_ref, qv, kb, kxb, g, rs):
    hd = A_HEAD_DIM
    qg = jnp.concatenate([qv[:, (4 * g + r) * hd:(4 * g + r + 1) * hd] for r in rs], axis=0).astype(BF16)
    qi = lax.broadcasted_iota(jnp.int32, (_QB, 3 * _QB), 0)
    kj = lax.broadcasted_iota(jnp.int32, (_QB, 3 * _QB), 1)
    kpos = n * _QB - _QB + kj
    valid = (kpos >= 0) & (kpos < t_len) & (jnp.abs(kj - _QB - qi) <= WINDOW)
    valid = jnp.concatenate([valid] * len(rs), axis=0)
    scale = hd ** -0.5
    s = jnp.where(valid, _dot_nt(qg, kb) * scale, -jnp.inf)
    sc = _dot_nt(qg, kxb) * scale
    sk = jnp.concatenate([jnp.full((_QB, 1), sink_ref[4 * g + r], F32) for r in rs], axis=0)
    m = jnp.maximum(jnp.maximum(jnp.max(s, axis=-1, keepdims=True), jnp.max(sc, axis=-1, keepdims=True)), sk)
    p = jnp.exp(s - m)
    pc = jnp.exp(sc - m)
    ps = jnp.exp(sk - m)
    inv = 1.0 / (jnp.sum(p, axis=-1, keepdims=True) + jnp.sum(pc, axis=-1, keepdims=True) + ps)
    return p, pc, ps, inv, qg


def _attn_fwd(zc, sink, t_len, l_ctx, name):
    in_specs, nb = _attn_specs(t_len, l_ctx)

    def body(sink_ref, q_ref, kp_ref, kc_ref, kn_ref, kx_ref, o_ref):
        n = pl.program_id(0)
        qv = q_ref[...]
        outs = []
        for g in range(A_KV_HEADS):
            kb, vb, kxb, vxb = _attn_keys(kp_ref[...], kc_ref[...], kn_ref[...], kx_ref[...], g)
            for rs in _HEAD_PAIRS:
                p, pc, _, inv, _ = _attn_probs(n, t_len, sink_ref, qv, kb, kxb, g, rs)
                o = (_dot(p.astype(BF16), vb) + _dot(pc.astype(BF16), vxb)) * inv
                outs += [o[i * _QB:(i + 1) * _QB] for i in range(len(rs))]
        o_ref[...] = jnp.concatenate(outs, axis=1)

    return pl.pallas_call(
        body, name=name, grid=(nb,), in_specs=in_specs,
        out_specs=pl.BlockSpec((_QB, 512), lambda n: (n, 0)),
        out_shape=jax.ShapeDtypeStruct((t_len, 512), F32),
        compiler_params=_cp("parallel"),
    )(sink, zc, zc, zc, zc, zc)


def _attn_bwd(zc, sink, o, dcat, t_len, l_ctx, name):
    rows = zc.shape[0]
    in_specs, nb = _attn_specs(t_len, l_ctx)
    in_specs = in_specs + [pl.BlockSpec((_QB, 512), lambda n: (n, 0)), pl.BlockSpec((_QB, 512), lambda n: (n, 0))]
    hd = A_HEAD_DIM
    scale = hd ** -0.5

    def body(sink_ref, q_ref, kp_ref, kc_ref, kn_ref, kx_ref, o_ref, do_ref, dq_ref, dkv_ref, dsink_ref):
        n = pl.program_id(0)

        @pl.when(n == 0)
        def _():
            dkv_ref[...] = jnp.zeros_like(dkv_ref)
            dsink_ref[...] = jnp.zeros_like(dsink_ref)

        qv = q_ref[...]
        ov = o_ref[...]
        dov = do_ref[...]
        dqs, dkbs, dvbs, dkxs, dvxs, dsinks = [], [], [], [], [], []
        for g in range(A_KV_HEADS):
            kb, vb, kxb, vxb = _attn_keys(kp_ref[...], kc_ref[...], kn_ref[...], kx_ref[...], g)
            parts = []
            for rs in _HEAD_PAIRS:
                p, pc, ps, inv, qg = _attn_probs(n, t_len, sink_ref, qv, kb, kxb, g, rs)
                og = jnp.concatenate([ov[:, (4 * g + r) * hd:(4 * g + r + 1) * hd] for r in rs], axis=0)
                dog = jnp.concatenate([dov[:, (4 * g + r) * hd:(4 * g + r + 1) * hd] for r in rs], axis=0)
                delta = jnp.sum(og * dog, axis=-1, keepdims=True)
                dogb = dog.astype(BF16)
                pn = p * inv
                pcn = pc * inv
                ds = (pn * (_dot_nt(dogb, vb) - delta) * scale).astype(BF16)
                dsc = (pcn * (_dot_nt(dogb, vxb) - delta) * scale).astype(BF16)
                dsk = (ps * inv) * (0.0 - delta)
                dqg = _dot(ds, kb) + _dot(dsc, kxb)
                dqs += [dqg[i * _QB:(i + 1) * _QB] for i in range(len(rs))]
                parts.append((_dot_tn(ds, qg), _dot_tn(pn.astype(BF16), dogb),
                              _dot_tn(dsc, qg), _dot_tn(pcn.astype(BF16), dogb)))
                for i in range(len(rs)):
                    tot = jnp.sum(dsk[i * _QB:(i + 1) * _QB], axis=0, keepdims=True)
                    dsinks.append(jnp.broadcast_to(tot, (1, 128)))
            dkbs.append(parts[0][0] + parts[1][0])
            dvbs.append(parts[0][1] + parts[1][1])
            dkxs.append(parts[0][2] + parts[1][2])
            dvxs.append(parts[0][3] + parts[1][3])
        dsink_ref[...] += jnp.concatenate(dsinks, axis=0)
        dq_ref[...] = jnp.concatenate(dqs, axis=1)
        band = jnp.concatenate(dkbs + dvbs, axis=1)
        ctxc = jnp.concatenate(dkxs + dvxs, axis=1)
        r_prev = pl.multiple_of(jnp.maximum(n - 1, 0) * _QB, _QB)
        r_cur = pl.multiple_of(n * _QB, _QB)
        r_next = pl.multiple_of((n + 1) * _QB, _QB)
        dkv_ref[pl.ds(r_prev, _QB), :] += band[0:_QB]
        dkv_ref[pl.ds(r_cur, _QB), :] += band[_QB:2 * _QB]
        dkv_ref[pl.ds(r_next, _QB), :] += band[2 * _QB:3 * _QB]
        dkv_ref[t_len:t_len + l_ctx, :] += ctxc

    return pl.pallas_call(
        body, name=name, grid=(nb,), in_specs=in_specs,
        out_specs=[pl.BlockSpec((_QB, 512), lambda n: (n, 0)),
                   pl.BlockSpec((rows, 256), lambda n: (0, 0)),
                   pl.BlockSpec((8, 128), lambda n: (0, 0))],
        out_shape=[jax.ShapeDtypeStruct((t_len, 512), F32),
                   jax.ShapeDtypeStruct((rows, 256), F32),
                   jax.ShapeDtypeStruct((8, 128), F32)],
        compiler_params=_cp("arbitrary"),
    )(sink, zc, zc, zc, zc, zc, o, dcat)


_GC = B_CHUNK


def _split_bf16(a):
    hi = a.astype(BF16)
    return hi, (a - hi.astype(F32)).astype(BF16)


def _gla_chunk_terms(qk, la, reverse):
    q = qk[:, 0:256]
    k = qk[:, 256:512]
    off = 256 if reverse else 0
    lad = la[:, off:off + 256]
    ii = lax.broadcasted_iota(jnp.int32, (_GC, _GC), 0)
    jj = lax.broadcasted_iota(jnp.int32, (_GC, _GC), 1)
    mask = (jj >= ii) if reverse else (jj <= ii)
    tri = jnp.where(mask, 1.0, 0.0).astype(BF16)
    la_hi, la_lo = _split_bf16(lad)
    g = _dot(tri, la_hi) + _dot(tri, la_lo)
    gl = jnp.sum(lad, axis=0, keepdims=True)
    eg = jnp.exp(g)
    eng = jnp.exp(-g)
    eend = jnp.exp(gl - g)
    sc = B_DK ** -0.5
    qt = q * (sc * eg)
    kt = k * eng
    ke = k * eend
    return mask, tri, gl, eg, eng, eend, qt, kt, ke


def _head(a, hh, width):
    return a[:, hh * width:(hh + 1) * width]


def _gla_fwd(zc, la, dep, t_len, l_ctx, name):
    rows = zc.shape[0]
    n_x = t_len // _GC
    n_c = n_x + l_ctx // _GC
    qkb, vb = ZC_QK // 512, ZC_V // 512

    def ch_f(c):
        return lax.rem(c + n_x, n_c)

    def ch_r(c):
        return n_c - 1 - c

    def body(qkf_ref, vf_ref, laf_ref, qkr_ref, vr_ref, lar_ref, dep_ref, of_ref, or_ref, spf_ref, spr_ref, stf, strv):
        del dep_ref
        c = pl.program_id(0)

        @pl.when(c == 0)
        def _():
            stf[...] = jnp.zeros_like(stf)
            strv[...] = jnp.zeros_like(strv)

        results = []
        for qk_ref, v_ref, la_ref, st, reverse in ((qkf_ref, vf_ref, laf_ref, stf, False),
                                                   (qkr_ref, vr_ref, lar_ref, strv, True)):
            mask, _, gl, _, _, _, qt, kt, ke = _gla_chunk_terms(qk_ref[...], la_ref[...], reverse)
            vbf = v_ref[...].astype(BF16)
            qtb, ktb, keb = qt.astype(BF16), kt.astype(BF16), ke.astype(BF16)
            egl = jnp.exp(gl)
            prevs = [st[hh] for hh in range(B_HEADS)]
            outs, news = [], []
            for hh in range(B_HEADS):
                qth, vh = _head(qtb, hh, B_DK), _head(vbf, hh, B_DV)
                att = jnp.where(mask, _dot_nt(qth, _head(ktb, hh, B_DK)), 0.0)
                outs.append(_dot(att.astype(BF16), vh) + _dot_nt(qth, prevs[hh].astype(BF16)))
                news.append(prevs[hh] * _head(egl, hh, B_DK) + _dot_tn(vh, _head(keb, hh, B_DK)))
            results.append((jnp.concatenate(outs, axis=1), prevs, news))
        for (o_all, prevs, news), o_ref, sp_ref, st in zip(results, (of_ref, or_ref), (spf_ref, spr_ref), (stf, strv)):
            o_ref[...] = o_all
            for hh in range(B_HEADS):
                sp_ref[0, hh] = prevs[hh]
                st[hh] = news[hh]

    st_shape = (B_HEADS, B_DV, B_DK)
    return pl.pallas_call(
        body, name=name, grid=(n_c,),
        in_specs=[pl.BlockSpec((_GC, 512), lambda c: (ch_f(c), qkb)),
                  pl.BlockSpec((_GC, 512), lambda c: (ch_f(c), vb)),
                  pl.BlockSpec((_GC, 512), lambda c: (ch_f(c), 0)),
                  pl.BlockSpec((_GC, 512), lambda c: (ch_r(c), qkb)),
                  pl.BlockSpec((_GC, 512), lambda c: (ch_r(c), vb)),
                  pl.BlockSpec((_GC, 512), lambda c: (ch_r(c), 0)),
                  pl.BlockSpec((8, 128), lambda c: (0, 0))],
        out_specs=[pl.BlockSpec((_GC, 512), lambda c: (ch_f(c), 0)),
                   pl.BlockSpec((_GC, 512), lambda c: (ch_r(c), 0)),
                   pl.BlockSpec((1,) + st_shape, lambda c: (c, 0, 0, 0)),
                   pl.BlockSpec((1,) + st_shape, lambda c: (c, 0, 0, 0))],
        out_shape=[jax.ShapeDtypeStruct((rows, 512), F32), jax.ShapeDtypeStruct((rows, 512), F32),
                   jax.ShapeDtypeStruct((n_c,) + st_shape, F32), jax.ShapeDtypeStruct((n_c,) + st_shape, F32)],
        scratch_shapes=[pltpu.VMEM(st_shape, F32), pltpu.VMEM(st_shape, F32)],
        compiler_params=_cp("arbitrary"),
    )(zc, zc, la, zc, zc, la, dep)


def _gla_bwd(zc, la, spf, spr, dosum, t_len, l_ctx, name):
    rows = zc.shape[0]
    n_x = t_len // _GC
    n_c = n_x + l_ctx // _GC
    n_all = rows // _GC
    qkb, vb = ZC_QK // 512, ZC_V // 512

    def scan_of(c):
        return jnp.maximum(n_c - 1 - c, 0)

    def ch_f(c):
        return jnp.where(c < n_c, lax.rem(scan_of(c) + n_x, n_c), c)

    def ch_r(c):
        return c

    def do_of(ch):
        return jnp.minimum(ch, n_x - 1)

    def body(qkf_ref, vf_ref, laf_ref, spf_ref, dof_ref, qkr_ref, vr_ref, lar_ref, spr_ref, dor_ref,
             dqkf_ref, dvf_ref, dlaf_ref, dqkr_ref, dvr_ref, dlar_ref, dsf, dsr):
        c = pl.program_id(0)

        @pl.when(c == 0)
        def _():
            dsf[...] = jnp.zeros_like(dsf)
            dsr[...] = jnp.zeros_like(dsr)

        @pl.when(c >= n_c)
        def _():
            for r in (dqkf_ref, dvf_ref, dlaf_ref, dqkr_ref, dvr_ref, dlar_ref):
                r[...] = jnp.zeros_like(r)

        @pl.when(c < n_c)
        def _():
            sc = B_DK ** -0.5
            results = []
            for qk_ref, v_ref, la_ref, sp_ref, do_ref, dst, reverse, ch in (
                    (qkf_ref, vf_ref, laf_ref, spf_ref, dof_ref, dsf, False, ch_f(c)),
                    (qkr_ref, vr_ref, lar_ref, spr_ref, dor_ref, dsr, True, ch_r(c))):
                mask, tri, gl, eg, eng, eend, qt, kt, ke = _gla_chunk_terms(qk_ref[...], la_ref[...], reverse)
                vbf = v_ref[...].astype(BF16)
                dob = jnp.where(ch < n_x, do_ref[...], 0.0).astype(BF16)
                qtb, ktb, keb = qt.astype(BF16), kt.astype(BF16), ke.astype(BF16)
                egl = jnp.exp(gl)
                prevs = [sp_ref[0, hh] for hh in range(B_HEADS)]
                dnews = [dst[hh] for hh in range(B_HEADS)]
                dqts, dkts, dkes, dvs, dprevs, dgls = [], [], [], [], [], []
                for hh in range(B_HEADS):
                    qth, kth, keh = _head(qtb, hh, B_DK), _head(ktb, hh, B_DK), _head(keb, hh, B_DK)
                    vh, doh = _head(vbf, hh, B_DV), _head(dob, hh, B_DV)
                    eglh = _head(egl, hh, B_DK)
                    dsb = dnews[hh].astype(BF16)
                    att = jnp.where(mask, _dot_nt(qth, kth), 0.0).astype(BF16)
                    datt = jnp.where(mask, _dot_nt(doh, vh), 0.0).astype(BF16)
                    dqts.append(_dot(datt, kth) + _dot(doh, prevs[hh].astype(BF16)))
                    dkts.append(_dot_tn(datt, qth))
                    dvs.append(_dot_tn(att, doh) + _dot_nt(keh, dsb))
                    dkes.append(_dot(vh, dsb))
                    dprevs.append(dnews[hh] * eglh + _dot_tn(doh, qth))
                    dgls.append(jnp.sum(dnews[hh] * prevs[hh], axis=0, keepdims=True) * eglh)
                dqt = jnp.concatenate(dqts, axis=1)
                dkt = jnp.concatenate(dkts, axis=1)
                dke = jnp.concatenate(dkes, axis=1)
                dgl = jnp.sum(dke * ke, axis=0, keepdims=True) + jnp.concatenate(dgls, axis=1)
                dg_hi, dg_lo = _split_bf16(dqt * qt - dkt * kt - dke * ke)
                dla = _dot_tn(tri, dg_hi) + _dot_tn(tri, dg_lo) + dgl
                dqk = jnp.concatenate([dqt * (sc * eg), dkt * eng + dke * eend], axis=1)
                results.append((dqk, jnp.concatenate(dvs, axis=1), dla, dprevs))
            for (dqk, dv, dla, dprevs), dqk_ref, dv_ref, dla_ref, dst in zip(
                    results, (dqkf_ref, dqkr_ref), (dvf_ref, dvr_ref), (dlaf_ref, dlar_ref), (dsf, dsr)):
                dqk_ref[...] = dqk
                dv_ref[...] = dv
                dla_ref[...] = dla
                for hh in range(B_HEADS):
                    dst[hh] = dprevs[hh]

    st_shape = (B_HEADS, B_DV, B_DK)

    def side(chf):
        return [pl.BlockSpec((_GC, 512), lambda c: (chf(c), qkb)),
                pl.BlockSpec((_GC, 512), lambda c: (chf(c), vb)),
                pl.BlockSpec((_GC, 512), lambda c: (chf(c), 0)),
                pl.BlockSpec((1,) + st_shape, lambda c: (scan_of(c), 0, 0, 0)),
                pl.BlockSpec((_GC, 512), lambda c: (do_of(chf(c)), 0))]

    def out_side(chf):
        return [pl.BlockSpec((_GC, 512), lambda c: (chf(c), 0)),
                pl.BlockSpec((_GC, 512), lambda c: (chf(c), 0)),
                pl.BlockSpec((_GC, 256), lambda c: (chf(c), 0))]

    shp = [jax.ShapeDtypeStruct((rows, 512), F32), jax.ShapeDtypeStruct((rows, 512), F32),
           jax.ShapeDtypeStruct((rows, 256), F32)]
    return pl.pallas_call(
        body, name=name, grid=(n_all,),
        in_specs=side(ch_f) + side(ch_r),
        out_specs=out_side(ch_f) + out_side(ch_r),
        out_shape=shp + shp,
        scratch_shapes=[pltpu.VMEM(st_shape, F32), pltpu.VMEM(st_shape, F32)],
        compiler_params=_cp("arbitrary"),
    )(zc, zc, la, spf, dosum, zc, zc, la, spr, dosum)


def _gla_out_fwd(o_a, o_f, o_r, zc, gla_g, t_len, name):
    tm = ROW_TILE
    rb = ZC_R // 512

    def body(oa_ref, of_ref, or_ref, r_ref, g_ref, cat_ref):
        osum = of_ref[...] + or_ref[...]
        g = g_ref[...]
        pieces = []
        for hh in range(B_HEADS):
            oh = osum[:, hh * B_DV:(hh + 1) * B_DV]
            rs = lax.rsqrt(jnp.mean(oh * oh, axis=-1, keepdims=True) + RMS_EPS)
            pieces.append((oh * rs) * g)
        r = r_ref[...]
        cat_ref[:, 0:512] = oa_ref[...].astype(BF16)
        cat_ref[:, 512:1024] = (jnp.concatenate(pieces, axis=1) * (r * _sigmoid(r))).astype(BF16)

    return pl.pallas_call(
        body, name=name, grid=(t_len // tm,),
        in_specs=[pl.BlockSpec((tm, 512), lambda i: (i, 0)),
                  pl.BlockSpec((tm, 512), lambda i: (i, 0)),
                  pl.BlockSpec((tm, 512), lambda i: (i, 0)),
                  pl.BlockSpec((tm, 512), lambda i: (i, rb)),
                  pl.BlockSpec((1, B_DV), lambda i: (0, 0))],
        out_specs=pl.BlockSpec((tm, D_MODEL), lambda i: (i, 0)),
        out_shape=jax.ShapeDtypeStruct((t_len, D_MODEL), BF16),
        compiler_params=_cp("parallel"),
    )(o_a, o_f, o_r, zc, gla_g)


def _gla_out_bwd(dcat, o_f, o_r, zc, gla_g, t_len, name):
    tm = ROW_TILE
    rb = ZC_R // 512

    def body(d_ref, of_ref, or_ref, r_ref, g_ref, dos_ref, dr_ref, dg_ref):
        i = pl.program_id(0)
        osum = of_ref[...] + or_ref[...]
        g = g_ref[...]
        r = r_ref[...]
        dgo = d_ref[...]
        sg = _sigmoid(r)
        dnrmg = dgo * (r * sg)
        nrms, dos = [], []
        dg_acc = jnp.zeros((1, B_DV), F32)
        for hh in range(B_HEADS):
            oh = osum[:, hh * B_DV:(hh + 1) * B_DV]
            rs = lax.rsqrt(jnp.mean(oh * oh, axis=-1, keepdims=True) + RMS_EPS)
            nrm = oh * rs
            dn = dnrmg[:, hh * B_DV:(hh + 1) * B_DV]
            dg_acc = dg_acc + jnp.sum(dn * nrm, axis=0, keepdims=True)
            dnn = dn * g
            dos.append(rs * (dnn - nrm * jnp.mean(dnn * nrm, axis=-1, keepdims=True)))
            nrms.append(nrm * g)
        dos_ref[...] = jnp.concatenate(dos, axis=1)
        dr_ref[...] = dgo * jnp.concatenate(nrms, axis=1) * (sg * (1.0 + r * (1.0 - sg)))

        @pl.when(i == 0)
        def _():
            dg_ref[...] = jnp.zeros_like(dg_ref)

        dg_ref[...] += dg_acc

    return pl.pallas_call(
        body, name=name, grid=(t_len // tm,),
        in_specs=[pl.BlockSpec((tm, 512), lambda i: (i, 1)),
                  pl.BlockSpec((tm, 512), lambda i: (i, 0)),
                  pl.BlockSpec((tm, 512), lambda i: (i, 0)),
                  pl.BlockSpec((tm, 512), lambda i: (i, rb)),
                  pl.BlockSpec((1, B_DV), lambda i: (0, 0))],
        out_specs=[pl.BlockSpec((tm, 512), lambda i: (i, 0)),
                   pl.BlockSpec((tm, 512), lambda i: (i, 0)),
                   pl.BlockSpec((1, B_DV), lambda i: (0, 0))],
        out_shape=[jax.ShapeDtypeStruct((t_len, 512), F32), jax.ShapeDtypeStruct((t_len, 512), F32),
                   jax.ShapeDtypeStruct((1, B_DV), F32)],
        compiler_params=_cp("arbitrary"),
    )(dcat, o_f, o_r, zc, gla_g)


def _mix_prep(dq, dkv, dqk_f, dqk_r, dv_f, dv_r, d_r, dla_f, dla_r, zc, wg2, bias2, cs, t_len, name):
    rows = zc.shape[0]
    tm = ROW_TILE
    n_x = t_len // tm
    gb = ZC_G // 128

    def xrow(i):
        return jnp.minimum(i, n_x - 1)

    def body(dq_ref, dkv_ref, dqkf_ref, dqkr_ref, dvf_ref, dvr_ref, dr_ref, dlaf_ref, dlar_ref, zg_ref, wg_ref,
             b_ref, cs_ref, dz_ref, dwg_ref, db_ref):
        i = pl.program_id(0)
        is_x = i < n_x
        cos = cs_ref[:, 0:128]
        sin = cs_ref[:, 128:256]
        cosq = jnp.concatenate([cos] * 4, axis=1)
        sinq = jnp.concatenate([sin] * 4, axis=1)
        dqv = jnp.where(is_x, dq_ref[...], 0.0)
        dz_ref[:, ZC_Q:ZC_QK] = (dqv * cosq + _swap16(dqv * sinq)).astype(BF16)
        dz_ref[:, ZC_QK:ZC_V] = (dqkf_ref[...] + dqkr_ref[...]).astype(BF16)
        dz_ref[:, ZC_V:ZC_R] = (dvf_ref[...] + dvr_ref[...]).astype(BF16)
        dz_ref[:, ZC_R:ZC_KV] = jnp.where(is_x, dr_ref[...], 0.0).astype(BF16)
        dk = dkv_ref[:, 0:128]
        dz_ref[:, ZC_KV:ZC_KV + 128] = (dk * cos + _swap16(dk * sin)).astype(BF16)
        dz_ref[:, ZC_KV + 128:ZC_G] = dkv_ref[:, 128:256].astype(BF16)
        zgb = zg_ref[...].astype(BF16)
        wg = wg_ref[...]
        pre = _dot(zgb, wg) + b_ref[...]
        dla = jnp.concatenate([dlaf_ref[...], dlar_ref[...]], axis=1)
        dpre = dla * (_sigmoid(-pre) / B_GATE_NORM)
        dpb = dpre.astype(BF16)
        dz_ref[:, ZC_G:ZC_W] = _dot_nt(dpb, wg).astype(BF16)

        @pl.when(i == 0)
        def _():
            dwg_ref[...] = jnp.zeros_like(dwg_ref)
            db_ref[...] = jnp.zeros_like(db_ref)

        dwg_ref[...] += _dot_tn(zgb, dpb)
        db_ref[...] += jnp.sum(dpre, axis=0, keepdims=True)

    return pl.pallas_call(
        body, name=name, grid=(rows // tm,),
        in_specs=[pl.BlockSpec((tm, 512), lambda i: (xrow(i), 0)),
                  pl.BlockSpec((tm, 256), lambda i: (i, 0)),
                  pl.BlockSpec((tm, 512), lambda i: (i, 0)),
                  pl.BlockSpec((tm, 512), lambda i: (i, 0)),
                  pl.BlockSpec((tm, 512), lambda i: (i, 0)),
                  pl.BlockSpec((tm, 512), lambda i: (i, 0)),
                  pl.BlockSpec((tm, 512), lambda i: (xrow(i), 0)),
                  pl.BlockSpec((tm, 256), lambda i: (i, 0)),
                  pl.BlockSpec((tm, 256), lambda i: (i, 0)),
                  pl.BlockSpec((tm, 128), lambda i: (i, gb)),
                  pl.BlockSpec((128, 512), lambda i: (0, 0)),
                  pl.BlockSpec((1, 512), lambda i: (0, 0)),
                  pl.BlockSpec((tm, 256), lambda i: (i, 0))],
        out_specs=[pl.BlockSpec((tm, ZC_W), lambda i: (i, 0)),
                   pl.BlockSpec((128, 512), lambda i: (0, 0)),
                   pl.BlockSpec((1, 512), lambda i: (0, 0))],
        out_shape=[jax.ShapeDtypeStruct((rows, ZC_W), BF16),
                   jax.ShapeDtypeStruct((128, 512), F32),
                   jax.ShapeDtypeStruct((1, 512), F32)],
        compiler_params=_cp("arbitrary"),
    )(dq, dkv, dqk_f, dqk_r, dv_f, dv_r, d_r, dla_f, dla_r, zc, wg2, bias2, cs)


def _gate_weights(w_a2_f, b_a_f, w_a2_b, b_a_b):
    wg2 = jnp.zeros((128, 512), F32)
    wg2 = wg2.at[0:B_GATE_RANK, 0:256].set(w_a2_f).at[B_GATE_RANK:2 * B_GATE_RANK, 256:512].set(w_a2_b)
    bias2 = jnp.concatenate([b_a_f, b_a_b]).reshape(1, 512)
    return wg2.astype(BF16), bias2


_WIN_PERM = ((0, 512), (768, 1280), (1280, 1792), (1792, 2304), (512, 768), (2304, 2336))


def _w_in_to_cat(w_in_full):
    parts = [w_in_full[:, a:b] for a, b in _WIN_PERM]
    parts.append(jnp.zeros((w_in_full.shape[0], ZC_W - PROJ_DIM), w_in_full.dtype))
    return jnp.concatenate(parts, axis=1)


def _cat_to_w_in(d_wcat):
    return jnp.concatenate([d_wcat[:, ZC_Q:ZC_QK], d_wcat[:, ZC_KV:ZC_G], d_wcat[:, ZC_QK:ZC_KV],
                            d_wcat[:, ZC_G:ZC_G + 2 * B_GATE_RANK]], axis=1)


def _mixer_ab_forward(x1, g3, mods, wcat, wg2, bias2, sink, gla_g, w_out, cs, t_len, l_ctx, n_x, pace):
    h = _rms_mod_fwd(x1, g3, mods, 1, n_x, BF16, "mix0_mod")
    zc, la = _proj_fwd(h, wcat, wg2, bias2, cs, "mix0_proj")
    dep = pace("proj", zc)
    o_a = _attn_fwd(zc, sink + dep[0, 0], t_len, l_ctx, "mix0_attn")
    dep = pace("attn", o_a)
    o_f, o_r, spf, spr = _gla_fwd(zc, la, dep, t_len, l_ctx, "mix0_gla")
    dep = pace("gla", o_f)
    cat = _gla_out_fwd(o_a, o_f, o_r, zc, gla_g + dep[0:1, 0:1], t_len, "mix0_glaout")
    x2, y = _matmul_resid(cat, w_out, x1, mods, 5, 1.0, n_x, t_len, "mix0_out")
    return x2, (x1, h, zc, la, o_a, o_f, o_r, spf, spr, cat, y)


def _mixer_ab_backward(dx2, saved, g3, mods, wcat, wg2, bias2, sink, gla_g, w_out, cs, t_len, l_ctx, n_x):
    x1, h, zc, la, o_a, o_f, o_r, spf, spr, cat, y = saved
    rows = x1.shape[0]
    tm = ROW_TILE
    dy, dgate = _gate_dy(dx2, y, mods, 5, 1.0, n_x, t_len, "mix0_dy")
    dcat = _matmul_nt(dy, w_out, "mix0_dcat")
    tk = _token_tile(t_len)
    d_wout = _matmul_tn(
        cat, dy, pl.BlockSpec((tk, D_MODEL), lambda n, k: (k, 0)), pl.BlockSpec((tk, D_MODEL), lambda n, k: (k, 0)),
        (D_MODEL, D_MODEL), pl.BlockSpec((D_MODEL, D_MODEL), lambda n, k: (0, 0)), (1, t_len // tk), "mix0_dwout")
    dos, d_r, d_glag = _gla_out_bwd(dcat, o_f, o_r, zc, gla_g, t_len, "mix0_dglaout")
    dqk_f, dv_f, dla_f, dqk_r, dv_r, dla_r = _gla_bwd(zc, la, spf, spr, dos, t_len, l_ctx, "mix0_dgla")
    dq, dkv, dsink = _attn_bwd(zc, sink, o_a, dcat, t_len, l_ctx, "mix0_dattn")
    dzc, dwg2, dbias2 = _mix_prep(dq, dkv, dqk_f, dqk_r, dv_f, dv_r, d_r, dla_f, dla_r, zc, wg2, bias2, cs, t_len,
                                  "mix0_prep")
    tk = _token_tile(rows)
    d_wcat = _matmul_tn(
        h, dzc, pl.BlockSpec((tk, D_MODEL), lambda n, k: (k, 0)), pl.BlockSpec((tk, ZC_W), lambda n, k: (k, 0)),
        (D_MODEL, ZC_W), pl.BlockSpec((D_MODEL, ZC_W), lambda n, k: (0, 0)), (1, rows // tk), "mix0_dwin")
    pairs = [(dzc, pl.BlockSpec((tm, ZC_W), lambda i: (i, 0)), wcat, pl.BlockSpec((D_MODEL, ZC_W), lambda i: (0, 0)))]
    dx1, stats = _bwd_dx(pairs, x1, dx2, t_len // tm, g3, mods, 1, n_x, "mix0_dx")
    return dx1, stats, dgate, d_wcat, dwg2, dbias2, dsink, d_glag, d_wout


_PT = 256
_PH = 16


def _pool_window(n, t_len, w, transpose):
    shape = (_PT, _PT + 2 * _PH)
    a = n * _PT + lax.broadcasted_iota(jnp.int32, shape, 0)
    b = n * _PT - _PH + lax.broadcasted_iota(jnp.int32, shape, 1)
    t, s = (b, a) if transpose else (a, b)
    lo = jnp.maximum(t - w // 2, 0)
    hi = jnp.minimum(t + (w - w // 2), t_len)
    inside = (s >= lo) & (s < hi) & (t >= 0) & (t < t_len)
    return jnp.where(inside, 1.0, 0.0).astype(BF16)


def _pool_inv_count(first, count, t_len, w):
    t = first + lax.broadcasted_iota(jnp.int32, (count, 1), 0)
    lo = jnp.maximum(t - w // 2, 0)
    hi = jnp.minimum(t + (w - w // 2), t_len)
    return jnp.where((t >= 0) & (t < t_len), 1.0 / jnp.maximum(hi - lo, 1).astype(F32), 0.0)


def _window_sum(win, vals):
    hi, lo = _split_bf16(vals)
    return _dot(win, hi) + _dot(win, lo)


def _pool_halo(p_ref, c_ref, n_ref):
    return jnp.concatenate([p_ref[_PT - _PH:_PT, :], c_ref[...], n_ref[0:_PH, :]], axis=0)


def _pool_specs(t_len):
    nb = t_len // _PT
    return [pl.BlockSpec((_PT, D_MODEL), lambda n: (jnp.maximum(n - 1, 0), 0)),
            pl.BlockSpec((_PT, D_MODEL), lambda n: (n, 0)),
            pl.BlockSpec((_PT, D_MODEL), lambda n: (jnp.minimum(n + 1, nb - 1), 0))], nb


def _pool_fwd(h, wp, pscale, x1, mods, t_len, name):
    halo_specs, nb = _pool_specs(t_len)

    def body(hp_ref, hc_ref, hn_ref, w_ref, ps_ref, x_ref, m_ref, x2_ref, pooled_ref, ypre_ref):
        n = pl.program_id(0)
        hcat = _pool_halo(hp_ref, hc_ref, hn_ref)
        ys = []
        for gi, w in enumerate(POOL_WINDOWS):
            cols = slice(gi * POOL_GROUP, (gi + 1) * POOL_GROUP)
            hg = hcat[:, cols]
            mean = _window_sum(_pool_window(n, t_len, w, False), hg) * _pool_inv_count(n * _PT, _PT, t_len, w)
            pooled = (mean - hg[_PH:_PH + _PT]).astype(BF16)
            pooled_ref[:, cols] = pooled
            ys.append(_dot(pooled, w_ref[gi]))
        ypre = jnp.concatenate(ys, axis=1)
        ypre_ref[...] = ypre
        x2_ref[...] = x_ref[...] + m_ref[0, 5:6, :] * (ypre * ps_ref[...])

    return pl.pallas_call(
        body, name=name, grid=(nb,),
        in_specs=halo_specs + [pl.BlockSpec((4, POOL_GROUP, POOL_GROUP), lambda n: (0, 0, 0)),
                               pl.BlockSpec((1, D_MODEL), lambda n: (0, 0)),
                               pl.BlockSpec((_PT, D_MODEL), lambda n: (n, 0)),
                               pl.BlockSpec((1, N_MOD, D_MODEL), lambda n: (0, 0, 0))],
        out_specs=[pl.BlockSpec((_PT, D_MODEL), lambda n: (n, 0))] * 3,
        out_shape=[jax.ShapeDtypeStruct((t_len, D_MODEL), F32), jax.ShapeDtypeStruct((t_len, D_MODEL), BF16),
                   jax.ShapeDtypeStruct((t_len, D_MODEL), F32)],
        compiler_params=_cp("parallel"),
    )(h, h, h, wp, pscale, x1, mods)


def _pool_bwd_a(dx2, ypre, wp, pscale, mods, t_len, name):
    nb = t_len // _PT

    def body(d_ref, y_ref, w_ref, ps_ref, m_ref, dyp_ref, dpl_ref, dgate_ref, dps_ref):
        n = pl.program_id(0)
        dv = d_ref[...]
        ypre = y_ref[...]
        ps = ps_ref[...]
        dy = dv * m_ref[0, 5:6, :]
        dyp = (dy * ps).astype(BF16)
        dyp_ref[...] = dyp
        for gi in range(len(POOL_WINDOWS)):
            cols = slice(gi * POOL_GROUP, (gi + 1) * POOL_GROUP)
            dpl_ref[:, cols] = _dot_nt(dyp[:, cols], w_ref[gi])

        @pl.when(n == 0)
        def _():
            dgate_ref[...] = jnp.zeros_like(dgate_ref)
            dps_ref[...] = jnp.zeros_like(dps_ref)

        dgate_ref[...] += jnp.sum(dv * (ypre * ps), axis=0, keepdims=True)
        dps_ref[...] += jnp.sum(dy * ypre, axis=0, keepdims=True)

    return pl.pallas_call(
        body, name=name, grid=(nb,),
        in_specs=[pl.BlockSpec((_PT, D_MODEL), lambda n: (n, 0)),
                  pl.BlockSpec((_PT, D_MODEL), lambda n: (n, 0)),
                  pl.BlockSpec((4, POOL_GROUP, POOL_GROUP), lambda n: (0, 0, 0)),
                  pl.BlockSpec((1, D_MODEL), lambda n: (0, 0)),
                  pl.BlockSpec((1, N_MOD, D_MODEL), lambda n: (0, 0, 0))],
        out_specs=[pl.BlockSpec((_PT, D_MODEL), lambda n: (n, 0)),
                   pl.BlockSpec((_PT, D_MODEL), lambda n: (n, 0)),
                   pl.BlockSpec((1, D_MODEL), lambda n: (0, 0)),
                   pl.BlockSpec((1, D_MODEL), lambda n: (0, 0))],
        out_shape=[jax.ShapeDtypeStruct((t_len, D_MODEL), BF16), jax.ShapeDtypeStruct((t_len, D_MODEL), F32),
                   jax.ShapeDtypeStruct((1, D_MODEL), F32), jax.ShapeDtypeStruct((1, D_MODEL), F32)],
        compiler_params=_cp("arbitrary"),
    )(dx2, ypre, wp, pscale, mods)


def _pool_bwd_dx(dpl, x1, dx2, g3, mods, t_len, name):
    halo_specs, nb = _pool_specs(t_len)

    def body(dp_ref, dc_ref, dn_ref, x_ref, d_ref, g_ref, m_ref, dx_ref, acc_ref):
        n = pl.program_id(0)
        dcat = _pool_halo(dp_ref, dc_ref, dn_ref)
        dhs = []
        for gi, w in enumerate(POOL_WINDOWS):
            cols = slice(gi * POOL_GROUP, (gi + 1) * POOL_GROUP)
            dg = dcat[:, cols]
            scaled = dg * _pool_inv_count(n * _PT - _PH, _PT + 2 * _PH, t_len, w)
            dhs.append(_window_sum(_pool_window(n, t_len, w, True), scaled) - dg[_PH:_PH + _PT])
        dh = jnp.concatenate(dhs, axis=1)
        g = g_ref[1:2, :]
        scale = m_ref[0, 4:5, :]
        dx = _rms_mod_bwd_tail(dh, x_ref[...], g, scale, 0, acc_ref, n == 0)
        dx_ref[...] = d_ref[...] + dx

    return pl.pallas_call(
        body, name=name, grid=(nb,),
        in_specs=halo_specs + [pl.BlockSpec((_PT, D_MODEL), lambda n: (n, 0)),
                               pl.BlockSpec((_PT, D_MODEL), lambda n: (n, 0)),
                               pl.BlockSpec((3, D_MODEL), lambda n: (0, 0)),
                               pl.BlockSpec((1, N_MOD, D_MODEL), lambda n: (0, 0, 0))],
        out_specs=[pl.BlockSpec((_PT, D_MODEL), lambda n: (n, 0)),
                   pl.BlockSpec((2, 3, D_MODEL), lambda n: (0, 0, 0))],
        out_shape=[jax.ShapeDtypeStruct((t_len, D_MODEL), F32), jax.ShapeDtypeStruct((2, 3, D_MODEL), F32)],
        compiler_params=_cp("arbitrary"),
    )(dpl, dpl, dpl, x1, dx2, g3, mods)


def _mixer_pool_forward(x1, g3, mods, wp, pscale, t_len):
    h = _rms_mod_fwd(x1, g3, mods, 1, t_len // ROW_TILE, F32, "mix1_mod")
    x2, pooled, ypre = _pool_fwd(h, wp, pscale, x1, mods, t_len, "mix1_pool")
    return x2, (x1, pooled, ypre)


def _mixer_pool_backward(dx2, saved, g3, mods, wp, pscale, t_len):
    x1, pooled, ypre = saved
    tm = ROW_TILE
    dyp, dpl, dgate, dps = _pool_bwd_a(dx2, ypre, wp, pscale, mods, t_len, "mix1_da")
    d_wp = _matmul_tn(
        pooled, dyp, pl.BlockSpec((tm, POOL_GROUP), lambda g, k: (k, g)),
        pl.BlockSpec((tm, POOL_GROUP), lambda g, k: (k, g)),
        (4, POOL_GROUP, POOL_GROUP), pl.BlockSpec((1, POOL_GROUP, POOL_GROUP), lambda g, k: (g, 0, 0)),
        (4, t_len // tm), "mix1_dwp")
    dx1, stats = _pool_bwd_dx(dpl, x1, dx2, g3, mods, t_len, "mix1_dx")
    return dx1, stats, dgate, dps, d_wp


def _final_loss(x3, final_g, target, name):
    t_len = x3.shape[0]
    tm = ROW_TILE

    def body(x_ref, g_ref, t_ref, dx_ref, loss_ref, dg_ref):
        i = pl.program_id(0)
        xv = x_ref[...]
        g = g_ref[...]
        r = lax.rsqrt(jnp.mean(xv * xv, axis=-1, keepdims=True) + RMS_EPS)
        xhat = xv * r
        err = xhat * g - t_ref[...]
        part = 0.5 * jnp.sum(jnp.mean(err * err, axis=-1, keepdims=True), axis=0, keepdims=True)
        dy = err * (1.0 / D_MODEL)

        @pl.when(i == 0)
        def _():
            loss_ref[...] = jnp.zeros_like(loss_ref)
            dg_ref[...] = jnp.zeros_like(dg_ref)

        loss_ref[...] += jnp.broadcast_to(part, (1, 128))
        dg_ref[...] += jnp.sum(dy * xhat, axis=0, keepdims=True)
        dxh = dy * g
        dx_ref[...] = r * (dxh - xhat * jnp.mean(dxh * xhat, axis=-1, keepdims=True))

    return pl.pallas_call(
        body, name=name, grid=(t_len // tm,),
        in_specs=[pl.BlockSpec((tm, D_MODEL), lambda i: (i, 0)),
                  pl.BlockSpec((1, D_MODEL), lambda i: (0, 0)),
                  pl.BlockSpec((tm, D_MODEL), lambda i: (i, 0))],
        out_specs=[pl.BlockSpec((tm, D_MODEL), lambda i: (i, 0)),
                   pl.BlockSpec((1, 128), lambda i: (0, 0)),
                   pl.BlockSpec((1, D_MODEL), lambda i: (0, 0))],
        out_shape=[jax.ShapeDtypeStruct((t_len, D_MODEL), F32), jax.ShapeDtypeStruct((1, 128), F32),
                   jax.ShapeDtypeStruct((1, D_MODEL), F32)],
        compiler_params=_cp("arbitrary"),
    )(x3, final_g, target)


_CROWS = 16


def _adaln_fwd(c16, w_mod, bias_k, name):
    n_l, _, cols = w_mod.shape

    def body(c_ref, w_ref, b_ref, o_ref):
        cv = c_ref[...]
        sc = (cv * _sigmoid(cv)).astype(BF16)
        o_ref[0] = _dot(sc, w_ref[0].astype(BF16)) + b_ref[0]

    return pl.pallas_call(
        body, name=name, grid=(n_l,),
        in_specs=[pl.BlockSpec((_CROWS, D_MODEL), lambda l: (0, 0)),
                  pl.BlockSpec((1, D_MODEL, cols), lambda l: (l, 0, 0)),
                  pl.BlockSpec((1, 1, cols), lambda l: (l, 0, 0))],
        out_specs=pl.BlockSpec((1, _CROWS, cols), lambda l: (l, 0, 0)),
        out_shape=jax.ShapeDtypeStruct((n_l, _CROWS, cols), F32),
        compiler_params=_cp("parallel"),
    )(c16, w_mod, bias_k)


def _adaln_bwd(c16, d16, w_mod, dmmc_k, name):
    n_l, _, cols = w_mod.shape

    def body(c_ref, d_ref, w_ref, dm_ref, gw_ref, cp_ref):
        layer = pl.program_id(0)
        cv = c_ref[...]
        gw_ref[0] = _dot_tn_hi(cv * _sigmoid(cv), d_ref[0])

        @pl.when(layer == 0)
        def _():
            cp_ref[...] = jnp.sum(w_ref[0] * dm_ref[...], axis=1, keepdims=True)

    return pl.pallas_call(
        body, name=name, grid=(n_l,),
        in_specs=[pl.BlockSpec((_CROWS, D_MODEL), lambda l: (0, 0)),
                  pl.BlockSpec((1, _CROWS, cols), lambda l: (l, 0, 0)),
                  pl.BlockSpec((1, D_MODEL, cols), lambda l: (0, 0, 0)),
                  pl.BlockSpec((1, cols), lambda l: (0, 0))],
        out_specs=[pl.BlockSpec((1, D_MODEL, cols), lambda l: (l, 0, 0)),
                   pl.BlockSpec((D_MODEL, 1), lambda l: (0, 0))],
        out_shape=[jax.ShapeDtypeStruct((n_l, D_MODEL, cols), F32), jax.ShapeDtypeStruct((D_MODEL, 1), F32)],
        compiler_params=_cp("arbitrary"),
    )(c16, d16, w_mod, dmmc_k)


def _cctx_grad(cparts, c_ctx2, name):
    def body(p_ref, c_ref, o_ref):
        tot = ((p_ref[0] + p_ref[2]) + p_ref[4]) + p_ref[6]
        cv = c_ref[...]
        sg = _sigmoid(cv)
        o_ref[...] = tot * (sg * (1.0 + cv * (1.0 - sg)))

    return pl.pallas_call(
        body, name=name, out_shape=jax.ShapeDtypeStruct((8, 128), F32),
        in_specs=[pl.BlockSpec(memory_space=pltpu.VMEM), pl.BlockSpec(memory_space=pltpu.VMEM)],
        out_specs=pl.BlockSpec(memory_space=pltpu.VMEM),
    )(cparts, c_ctx2)


def _sum_devices(ga, name):
    def body(g_ref, o_ref):
        acc = g_ref[0]
        for d in range(1, N_DEV):
            acc = acc + g_ref[d]
        o_ref[...] = acc

    return pl.pallas_call(
        body, name=name, out_shape=jax.ShapeDtypeStruct(ga.shape[1:], F32),
        in_specs=[pl.BlockSpec(memory_space=pltpu.VMEM)], out_specs=pl.BlockSpec(memory_space=pltpu.VMEM),
    )(ga)


def _place():
    return lax.axis_index("x"), lax.axis_index("y"), lax.axis_index("c")


def _flip(a, d):
    return 1 - a if d else a


_CHIP_FLIPS = ((1, 0), (0, 1), (1, 1))


def _allgather_small(v, name):
    r, cc = v.shape

    def body(v_ref, out_ref, send_sems, recv_sems, local_sem):
        x, y, c = _place()
        me = 4 * x + 2 * y + c
        mine = pltpu.make_async_copy(v_ref, out_ref.at[me], local_sem)
        mine.start()
        sends = []
        for k in range(1, N_DEV):
            peer = (_flip(x, (k >> 2) & 1), _flip(y, (k >> 1) & 1), _flip(c, k & 1))
            cp = pltpu.make_async_remote_copy(src_ref=v_ref, dst_ref=out_ref.at[me], send_sem=send_sems.at[k - 1],
                                              recv_sem=recv_sems.at[k - 1], device_id=peer, device_id_type=MESH)
            cp.start()
            sends.append(cp)
        for k in range(1, N_DEV):
            px, py, pc = _flip(x, (k >> 2) & 1), _flip(y, (k >> 1) & 1), _flip(c, k & 1)
            pltpu.make_async_remote_copy(src_ref=v_ref, dst_ref=out_ref.at[4 * px + 2 * py + pc],
                                         send_sem=send_sems.at[k - 1], recv_sem=recv_sems.at[k - 1],
                                         device_id=(px, py, pc), device_id_type=MESH).wait_recv()
        for cp in sends:
            cp.wait_send()
        mine.wait()

    return pl.pallas_call(
        body, name=name, out_shape=jax.ShapeDtypeStruct((N_DEV, r, cc), F32),
        in_specs=[pl.BlockSpec(memory_space=pltpu.VMEM)], out_specs=pl.BlockSpec(memory_space=pltpu.VMEM),
        scratch_shapes=[pltpu.SemaphoreType.DMA((N_DEV - 1,)), pltpu.SemaphoreType.DMA((N_DEV - 1,)),
                        pltpu.SemaphoreType.DMA],
        compiler_params=pltpu.CompilerParams(vmem_limit_bytes=VMEM_LIMIT_BYTES),
    )(v)


_HBM_SPEC = pl.BlockSpec(memory_space=pltpu.HBM)
_SEM_SPEC = pl.BlockSpec(memory_space=pltpu.SEMAPHORE)
_EFFECT = pltpu.SideEffectType.DATAFLOW_SIDE_EFFECTING


def _in_hbm(a):
    return pltpu.with_memory_space_constraint(a, pltpu.HBM)


def _gather_start(arrs, groups, after, name):
    n, n_g = len(arrs), len(groups)

    def body(*refs):
        ins, zones = refs[:n], refs[n:2 * n]
        sems = refs[2 * n + 1:2 * n + 1 + 2 * n_g]
        token = refs[2 * n + 1 + 2 * n_g + 2 * n]
        x, y, c = _place()
        k_me = 2 * x + y
        for g, members in enumerate(groups):
            for t, a in enumerate(members):
                for j, (dx, dy) in enumerate(_CHIP_FLIPS):
                    pltpu.make_async_remote_copy(
                        src_ref=ins[a], dst_ref=zones[a].at[k_me], send_sem=sems[2 * g].at[3 * t + j],
                        recv_sem=sems[2 * g + 1].at[3 * t + j], device_id=(_flip(x, dx), _flip(y, dy), c),
                        device_id_type=MESH).start()
        token[...] = jnp.zeros_like(token)

    k_own = 2 * lax.axis_index("x") + lax.axis_index("y")
    zones = [lax.dynamic_update_slice(lax.empty((N_CHIPS,) + a.shape, a.dtype), a[None], (k_own,) + (0,) * a.ndim)
             for a in arrs]
    sem_shapes = []
    for members in groups:
        sem_shapes += [pltpu.SemaphoreType.DMA((3 * len(members),))] * 2
    outs = pl.pallas_call(
        body, name=name,
        out_shape=sem_shapes + [pltpu.HBM(a.shape, a.dtype) for a in arrs]
        + [pltpu.HBM(z.shape, z.dtype) for z in zones] + [jax.ShapeDtypeStruct((8, 128), F32)],
        in_specs=[_HBM_SPEC] * (2 * n) + [pl.BlockSpec(memory_space=pl.ANY)],
        out_specs=[_SEM_SPEC] * (2 * n_g) + [_HBM_SPEC] * (2 * n) + [pl.BlockSpec(memory_space=pltpu.VMEM)],
        input_output_aliases={i: 2 * n_g + i for i in range(2 * n)},
        compiler_params=pltpu.CompilerParams(has_side_effects=_EFFECT),
    )(*[_in_hbm(a) for a in arrs], *[_in_hbm(z) for z in zones], after)
    sems = outs[:2 * n_g]
    thru = outs[2 * n_g:2 * n_g + n]
    zones = outs[2 * n_g + n:2 * n_g + 2 * n]
    return [(sems[2 * g], sems[2 * g + 1]) for g in range(n_g)], thru, zones, outs[-1]


def _gather_wait(shards, zones, send_sems, recv_sems, after, name):
    m = len(shards)

    def body(*refs):
        ins, zs = refs[:m], refs[m:2 * m]
        ssem, rsem = refs[2 * m], refs[2 * m + 1]
        x, y, c = _place()
        for t in range(m):
            for j, (dx, dy) in enumerate(_CHIP_FLIPS):
                px, py = _flip(x, dx), _flip(y, dy)
                cp = pltpu.make_async_remote_copy(
                    src_ref=ins[t], dst_ref=zs[t].at[2 * px + py], send_sem=ssem.at[3 * t + j],
                    recv_sem=rsem.at[3 * t + j], device_id=(px, py, c), device_id_type=MESH)
                cp.wait_send()
                cp.wait_recv()

    after = list(after) if isinstance(after, (list, tuple)) else [after]
    outs = pl.pallas_call(
        body, name=name,
        out_shape=[pltpu.HBM(a.shape, a.dtype) for a in list(shards) + list(zones)],
        in_specs=[_HBM_SPEC] * (2 * m) + [_SEM_SPEC, _SEM_SPEC] + [pl.BlockSpec(memory_space=pl.ANY)] * len(after),
        out_specs=[_HBM_SPEC] * (2 * m),
        input_output_aliases={i: i for i in range(2 * m)},
        compiler_params=pltpu.CompilerParams(has_side_effects=_EFFECT),
    )(*shards, *zones, send_sems, recv_sems, *after)
    return outs[m:]


def _scatter_start(arrs, name):
    n = len(arrs)

    def body(*refs):
        ins, lands = refs[:n], refs[n:2 * n]
        ssem, rsem = refs[2 * n], refs[2 * n + 1]
        token = refs[2 * n + 2 + 2 * n]
        x, y, c = _place()
        for a in range(n):
            for j, (dx, dy) in enumerate(_CHIP_FLIPS):
                px, py = _flip(x, dx), _flip(y, dy)
                pltpu.make_async_remote_copy(
                    src_ref=ins[a].at[2 * px + py], dst_ref=lands[a].at[j], send_sem=ssem.at[3 * a + j],
                    recv_sem=rsem.at[3 * a + j], device_id=(px, py, c), device_id_type=MESH).start()
        token[...] = jnp.zeros_like(token)

    lands = [lax.empty((3,) + a.shape[1:], a.dtype) for a in arrs]
    outs = pl.pallas_call(
        body, name=name,
        out_shape=[pltpu.SemaphoreType.DMA((3 * n,))] * 2 + [pltpu.HBM(a.shape, a.dtype) for a in arrs]
        + [pltpu.HBM(z.shape, z.dtype) for z in lands] + [jax.ShapeDtypeStruct((8, 128), F32)],
        in_specs=[_HBM_SPEC] * (2 * n),
        out_specs=[_SEM_SPEC] * 2 + [_HBM_SPEC] * (2 * n) + [pl.BlockSpec(memory_space=pltpu.VMEM)],
        input_output_aliases={i: 2 + i for i in range(2 * n)},
        compiler_params=pltpu.CompilerParams(has_side_effects=_EFFECT),
    )(*[_in_hbm(a) for a in arrs], *[_in_hbm(z) for z in lands])
    return outs[0], outs[1], outs[2:2 + n], outs[2 + n:2 + 2 * n], outs[-1]


def _scatter_wait(arrs, lands, send_sems, recv_sems, after, name):
    n = len(arrs)

    def body(*refs):
        ins, lz = refs[:n], refs[n:2 * n]
        ssem, rsem = refs[2 * n], refs[2 * n + 1]
        x, y, c = _place()
        for a in range(n):
            for j, (dx, dy) in enumerate(_CHIP_FLIPS):
                px, py = _flip(x, dx), _flip(y, dy)
                cp = pltpu.make_async_remote_copy(
                    src_ref=ins[a].at[2 * px + py], dst_ref=lz[a].at[j], send_sem=ssem.at[3 * a + j],
                    recv_sem=rsem.at[3 * a + j], device_id=(px, py, c), device_id_type=MESH)
                cp.wait_send()
                cp.wait_recv()

    outs = pl.pallas_call(
        body, name=name,
        out_shape=[pltpu.HBM(a.shape, a.dtype) for a in list(arrs) + list(lands)],
        in_specs=[_HBM_SPEC] * (2 * n) + [_SEM_SPEC, _SEM_SPEC, pl.BlockSpec(memory_space=pl.ANY)],
        out_specs=[_HBM_SPEC] * (2 * n),
        input_output_aliases={i: i for i in range(2 * n)},
        compiler_params=pltpu.CompilerParams(has_side_effects=_EFFECT),
    )(*arrs, *lands, send_sems, recv_sems, after)
    return outs[:n], outs[n:]


def _swap_sibling(arrs, name):
    n = len(arrs)

    def body(*refs):
        ins, outs = refs[:n], refs[n:2 * n]
        send_sems, recv_sems = refs[2 * n:]
        x, y, c = _place()
        sends = []
        for a in range(n):
            cp = pltpu.make_async_remote_copy(src_ref=ins[a], dst_ref=outs[a], send_sem=send_sems.at[a],
                                              recv_sem=recv_sems.at[a], device_id=(x, y, 1 - c), device_id_type=MESH)
            cp.start()
            sends.append(cp)
        for cp in sends:
            cp.wait()

    any_spec = pl.BlockSpec(memory_space=pl.ANY)
    return pl.pallas_call(
        body, name=name,
        out_shape=[jax.ShapeDtypeStruct(a.shape, a.dtype) for a in arrs],
        in_specs=[any_spec] * n, out_specs=[any_spec] * n,
        scratch_shapes=[pltpu.SemaphoreType.DMA((n,)), pltpu.SemaphoreType.DMA((n,))],
    )(*arrs)


def _row_tile(rows, cols):
    for tr in (1024, 512, 256, 128, 64, 32, 16, 8):
        if rows % tr == 0 and tr * cols * 4 <= (1 << 20):
            return tr
    return rows


def _partial_sum(g_full, recv, k_idx, name):
    _, r, c = g_full.shape
    tr = _row_tile(r, c)

    def body(k_ref, g_ref, r_ref, o_ref):
        del k_ref
        acc = g_ref[0].astype(F32)
        for j in range(3):
            acc = acc + r_ref[j].astype(F32)
        o_ref[...] = acc

    return pl.pallas_call(
        body, name=name,
        grid_spec=pltpu.PrefetchScalarGridSpec(
            num_scalar_prefetch=1, grid=(r // tr,),
            in_specs=[pl.BlockSpec((1, tr, c), lambda i, k: (k[0], i, 0)),
                      pl.BlockSpec((3, tr, c), lambda i, k: (0, i, 0))],
            out_specs=pl.BlockSpec((tr, c), lambda i, k: (i, 0))),
        out_shape=jax.ShapeDtypeStruct((r, c), F32),
        compiler_params=_cp("parallel"),
    )(k_idx, g_full, recv)


def _adamw(w3, parts, m3, v3, layer, prev, name):
    n_l, r, c = w3.shape
    tr = _row_tile(r, c)
    n_i = r // tr
    n_p = len(parts)
    c1 = 1.0 - ADAM_B1 ** ADAM_STEP
    c2 = 1.0 - ADAM_B2 ** ADAM_STEP
    stacked = [isinstance(p, tuple) for p in parts]

    def body(*refs):
        w_ref, m_ref, v_ref = refs[0:3]
        g_refs = refs[3:3 + n_p]
        go_ref, d_ref, mo_ref, vo_ref = refs[-4:]
        g = None
        for p in range(n_p):
            term = g_refs[p][0] if stacked[p] else g_refs[p][...]
            g = term if g is None else g + term
        w = w_ref[0]
        m = ADAM_B1 * m_ref[0] + (1.0 - ADAM_B1) * g
        v = ADAM_B2 * v_ref[0] + (1.0 - ADAM_B2) * (g * g)
        m_hat = m / c1
        v_hat = v / c2
        go_ref[0] = g
        d_ref[0] = -ADAM_LR * (m_hat / (jnp.sqrt(v_hat) + ADAM_EPS) + ADAM_WD * w)
        mo_ref[0] = m
        vo_ref[0] = v

    blk = pl.BlockSpec((1, tr, c), lambda i: (layer, i, 0))
    in_specs = [blk, blk, blk]
    args = [w3, m3, v3]
    for part in parts:
        if isinstance(part, tuple):
            in_specs.append(pl.BlockSpec((1, tr, c), functools.partial(lambda idx, i: (idx, i, 0), part[1])))
            args.append(part[0])
        else:
            in_specs.append(pl.BlockSpec((tr, c), lambda i: (i, 0)))
            args.append(part)
    aliases = {}
    if prev is not None:
        in_specs += [pl.BlockSpec(memory_space=pl.ANY)] * 4
        aliases = {len(args) + q: q for q in range(4)}
        args += list(prev)
    shp = jax.ShapeDtypeStruct((n_l, r, c), F32)
    return pl.pallas_call(
        body, name=name, grid=(n_i,), in_specs=in_specs, out_specs=[blk] * 4, out_shape=[shp] * 4,
        input_output_aliases=aliases, compiler_params=_cp("parallel"),
    )(*args)


_SMALL_W = 4096
_PACK_ROWS = 352
_N9 = N_MOD * D_MODEL


def _flat_pad(parts, total):
    flat = jnp.concatenate([p.reshape(-1) for p in parts])
    return jnp.concatenate([flat, jnp.zeros((total - flat.shape[0],), F32)])


def kernel(x, c, ctx, c_ctx, w_mod, b_mod, norm_g, ffn1_wi, ffn1_wo, ffn2_wi, ffn2_wo, w_in, w_a2_f, b_a_f, w_a2_b, b_a_b, sink, gla_g, w_out, w_pool, pool_scale, final_g, loss_target, m_c_ctx, m_w_mod, m_b_mod, m_norm_g, m_ffn1_wi, m_ffn1_wo, m_ffn2_wi, m_ffn2_wo, m_w_in, m_w_a2_f, m_b_a_f, m_w_a2_b, m_b_a_b, m_sink, m_gla_g, m_w_out, m_w_pool, m_pool_scale, m_final_g, v_c_ctx, v_w_mod, v_b_mod, v_norm_g, v_ffn1_wi, v_ffn1_wo, v_ffn2_wi, v_ffn2_wo, v_w_in, v_w_a2_f, v_b_a_f, v_w_a2_b, v_b_a_b, v_sink, v_gla_g, v_w_out, v_w_pool, v_pool_scale, v_final_g):
    t_len, l_ctx = x.shape[1], ctx.shape[1]
    tm = ROW_TILE
    pad = (-(t_len + l_ctx)) % tm
    rows0 = t_len + l_ctx + pad
    n_x = t_len // tm
    xi, yi, ci = _place()
    k_me = 2 * xi + yi
    me = 4 * xi + 2 * yi + ci
    mod_cols = w_mod.shape[2]
    n_grp = len(POOL_WINDOWS)

    small_w = _flat_pad([norm_g, w_a2_f, w_a2_b, pool_scale], _SMALL_W).reshape(_SMALL_W // 128, 128)
    shards = [ffn1_wi[0], ffn1_wi[1], ffn1_wo[0], ffn1_wo[1], ffn2_wi[0], ffn2_wi[1], ffn2_wo[0], ffn2_wo[1],
              w_in[0], w_out[0], w_pool[0].reshape(n_grp * w_pool.shape[2], POOL_GROUP)]

    send_src = [s.astype(BF16) for s in shards] + [small_w]
    groups = ([11, 0], [2], [8, 9], [4], [6], [1], [3], [10, 5], [7])
    started = {}

    def gather_start(g, after):
        members = groups[g]
        sems, thru, zones, token = _gather_start([send_src[a] for a in members], (tuple(range(len(members))),),
                                                 after, "gather_start_%d" % g)
        started[g] = (sems[0], thru, zones)
        return token

    def gather_wait(g, after):
        (ssem, rsem), thru, zones = started[g]
        return dict(zip(groups[g], _gather_wait(thru, zones, ssem, rsem, after, "gather_wait_%d" % g)))

    c_all = _allgather_small(c.reshape(8, 128), "gather_cond").reshape(N_DEV, D_MODEL)
    tok = gather_start(0, c_all)
    c16 = jnp.concatenate([c_all, c_ctx[None], jnp.zeros((_CROWS - N_DEV - 1, D_MODEL), F32)], axis=0) + tok[0:1, 0:1]
    bias_k = lax.dynamic_slice(b_mod, (0, k_me * mod_cols), (2, mod_cols)).reshape(2, 1, mod_cols)
    mm_k = _adaln_fwd(c16, w_mod, bias_k, "adaln_fwd")
    mm_all = _allgather_small(mm_k.reshape(-1, 128), "gather_mod").reshape(N_DEV, 2, _CROWS, mod_cols)
    mm_full = jnp.concatenate([mm_all[2 * k] for k in range(N_CHIPS)], axis=-1)
    mm_x = lax.dynamic_index_in_dim(mm_full, me, axis=1, keepdims=False)
    mm_c = mm_full[:, N_DEV]
    mods = [jnp.stack([mm_x[l].reshape(N_MOD, D_MODEL), mm_c[l].reshape(N_MOD, D_MODEL)]) for l in range(2)]
    cs = _rope_tables(t_len, rows0)
    xcat = jnp.concatenate([x[0], ctx[0], jnp.zeros((pad, D_MODEL), F32)], axis=0)
    gathered = gather_wait(0, [mods[0], cs, xcat])
    sw = gathered[11].reshape(N_CHIPS, _SMALL_W)
    ng_n = norm_g.size
    a2_n = w_a2_f.size
    norm_g_full = jnp.concatenate([sw[k, :ng_n].reshape(norm_g.shape) for k in range(N_CHIPS)], axis=-1)
    w_a2_f_full = jnp.concatenate([sw[k, ng_n:ng_n + a2_n].reshape(w_a2_f.shape[1:]) for k in range(N_CHIPS)], axis=-1)
    w_a2_b_full = jnp.concatenate(
        [sw[k, ng_n + a2_n:ng_n + 2 * a2_n].reshape(w_a2_b.shape[1:]) for k in range(N_CHIPS)], axis=-1)
    pscale_full = jnp.concatenate(
        [sw[k, ng_n + 2 * a2_n:ng_n + 2 * a2_n + pool_scale.size] for k in range(N_CHIPS)]).reshape(1, D_MODEL)
    wg2, bias2 = _gate_weights(w_a2_f_full, b_a_f[0], w_a2_b_full, b_a_b[0])
    gla_g2 = gla_g.reshape(1, B_DV)
    final_g2 = final_g.reshape(1, D_MODEL)

    g3 = [norm_g_full[0], norm_g_full[1]]

    w1i, w1o, w2i, w2o = [None, None], [None, None], [None, None], [None, None]
    w1i[0] = gathered[0]
    mods_a = mods[0] + gather_start(1, w1i[0])[0:1, 0:1] + gather_start(2, w1i[0])[0:1, 0:1]
    x1, sv_a1, w1o[0] = _ffn_forward(xcat, g3[0], mods_a, 0, w1i[0],
                                     lambda s: (gather_wait(1, s)[2], gather_start(3, s)), n_x, "l0_ffn1")
    gathered = gather_wait(2, x1)
    w_in_full = jnp.concatenate([gathered[8][k] for k in range(N_CHIPS)], axis=1)
    wcat = _w_in_to_cat(w_in_full)
    w_out_full = gathered[9].reshape(D_MODEL, D_MODEL)
    mods_a = mods[0] + gather_start(4, x1)[0:1, 0:1]
    pace_group = {"proj": 5, "attn": 6, "gla": 7}
    x2, sv_am = _mixer_ab_forward(x1, g3[0], mods_a, wcat, wg2, bias2, sink[0], gla_g2, w_out_full, cs,
                                  t_len, l_ctx, n_x, lambda tag, res_: gather_start(pace_group[tag], res_))
    mods_a = mods[0] + gather_start(8, x2)[0:1, 0:1]
    w2i[0], w2o[0] = gather_wait(3, x2)[4], gather_wait(4, x2)[6]
    x3, sv_a2, _ = _ffn_forward(x2, g3[0], mods_a, 2, w2i[0], lambda s: (w2o[0], None), n_x, "l0_ffn2")
    w1i[1], w1o[1] = gather_wait(5, x3)[1], gather_wait(6, x3)[3]
    x4, sv_b1, _ = _ffn_forward(x3, g3[1], mods[1], 0, w1i[1], lambda s: (w1o[1], None), n_x, "l1_ffn1")
    gathered = gather_wait(7, x4)
    w2i[1] = gathered[5]
    wp_full = gathered[10].reshape(N_CHIPS, n_grp, -1, POOL_GROUP).transpose(1, 0, 2, 3).reshape(
        n_grp, POOL_GROUP, POOL_GROUP)
    x5, sv_bm = _mixer_pool_forward(x4, g3[1], mods[1], wp_full, pscale_full, t_len)
    x6, sv_b2, w2o[1] = _ffn_forward(x5, g3[1], mods[1], 2, w2i[1], lambda s: (gather_wait(8, s)[7], None), n_x,
                                     "l1_ffn2")
    dx6, loss_part, d_final_g = _final_loss(x6, final_g2, loss_target[0], "final_loss")
    loss = lax.psum(loss_part[0, 0], ("x", "y", "c"))

    sent = []

    def sender(weight, layer):
        def send(grad, tag):
            nm = "%s_%s_%d" % (weight, tag, layer)
            ssem, rsem, thru, lands, token = _scatter_start([grad], "scatter_start_" + nm)
            sent.append((nm, weight + "_" + tag if tag else weight, layer, thru, lands, ssem, rsem))
            return token[0:1, 0:1]
        return send

    dx5, st_b2, dg_b2 = _ffn_backward(dx6, sv_b2, g3[1], mods[1], 2, w2i[1], w2o[1], n_x, sender("ffn2", 1),
                                      "l1_ffn2_b")
    dx4, st_bm, dg_bm, d_pscale, d_wp = _mixer_pool_backward(dx5, sv_bm, g3[1], mods[1], wp_full, pscale_full, t_len)
    d_wp4 = d_wp.reshape(n_grp, N_CHIPS, -1, POOL_GROUP).transpose(1, 0, 2, 3).reshape(N_CHIPS, -1, POOL_GROUP)
    mods1 = mods[1] + sender("w_pool", 0)(d_wp4, "")
    dx3, st_b1, dg_b1 = _ffn_backward(dx4, sv_b1, g3[1], mods1, 0, w1i[1], w1o[1], n_x, sender("ffn1", 1),
                                      "l1_ffn1_b")
    dx2, st_a2, dg_a2 = _ffn_backward(dx3, sv_a2, g3[0], mods[0], 2, w2i[0], w2o[0], n_x, sender("ffn2", 0),
                                      "l0_ffn2_b")
    dx1, st_am, dg_am, d_wcat, d_wg2, d_bias2, d_sink, d_glag, d_wout = _mixer_ab_backward(
        dx2, sv_am, g3[0], mods[0], wcat, wg2, bias2, sink[0], gla_g2, w_out_full, cs, t_len, l_ctx, n_x)
    d_w_in4 = _cat_to_w_in(d_wcat).reshape(D_MODEL, N_CHIPS, -1).transpose(1, 0, 2)
    mods0 = mods[0] + sender("w_in", 0)(d_w_in4, "") + sender("w_out", 0)(d_wout.reshape(N_CHIPS, -1, D_MODEL), "")
    dx0, st_a1, dg_a1 = _ffn_backward(dx1, sv_a1, g3[0], mods0, 0, w1i[0], w1o[0], n_x, sender("ffn1", 0),
                                      "l0_ffn1_b", out_tiles=n_x)
    grad_x = dx0[None]

    def as3(a):
        n_l = a.shape[0] if a.ndim == 3 else 1
        return a.reshape(n_l, -1, a.shape[-1])

    res = {}
    big_w = {"ffn1_wi": (ffn1_wi, m_ffn1_wi, v_ffn1_wi), "ffn1_wo": (ffn1_wo, m_ffn1_wo, v_ffn1_wo),
             "ffn2_wi": (ffn2_wi, m_ffn2_wi, v_ffn2_wi), "ffn2_wo": (ffn2_wo, m_ffn2_wo, v_ffn2_wo),
             "w_in": (w_in, m_w_in, v_w_in), "w_out": (w_out, m_w_out, v_w_out), "w_pool": (w_pool, m_w_pool, v_w_pool)}
    k_idx = k_me.reshape(1).astype(jnp.int32)
    chain = dx0
    for lo, hi in ((0, 2), (2, 5), (5, 7), (7, 9), (9, 11)):
        partial = []
        for nm, wname, layer, thru, lands, ssem, rsem in sent[lo:hi]:
            mine, recv = _scatter_wait(thru, lands, ssem, rsem, chain, "scatter_wait_" + nm)
            partial.append(_partial_sum(mine[0], recv[0], k_idx, "partial_sum_" + nm))
        other = _swap_sibling(partial, "swap_partials_%d" % lo)
        for (nm, wname, layer, _, _, _, _), p, q in zip(sent[lo:hi], partial, other):
            w, m, v = big_w[wname]
            res[wname] = _adamw(as3(w), [p, q], as3(m), as3(v), layer, res.get(wname),
                                "adamw_%s_%d" % (wname, layer))
            chain = res[wname][3]

    def mod_row(st1, dg1, stm, dgm, st2, dg2, s):
        return jnp.concatenate([st1[s, 0], st1[s, 1], dg1[s, 0], stm[s, 0], stm[s, 1], dgm[s, 0],
                                st2[s, 0], st2[s, 1], dg2[s, 0]])

    dg_bm2 = jnp.concatenate([dg_bm, jnp.zeros_like(dg_bm)], axis=0)[:, None, :]
    d_mm_x0 = mod_row(st_a1, dg_a1, st_am, dg_am, st_a2, dg_a2, 0)
    d_mm_x1 = mod_row(st_b1, dg_b1, st_bm, dg_bm2, st_b2, dg_b2, 0)
    d_mm_c0 = mod_row(st_a1, dg_a1, st_am, dg_am, st_a2, dg_a2, 1)
    d_norm_g = jnp.stack([jnp.stack([st[0, 2] + st[1, 2] for st in (st_a1, st_am, st_a2)]),
                          jnp.stack([st[0, 2] + st[1, 2] for st in (st_b1, st_bm, st_b2)])])
    rk = B_GATE_RANK
    pack = _flat_pad([d_mm_x0, d_mm_x1, d_mm_c0, d_norm_g, d_bias2, d_wg2[0:rk, 0:256], d_wg2[rk:2 * rk, 256:512],
                      d_sink[:, 0], jnp.zeros((120,), F32), d_glag, d_pscale, d_final_g],
                     _PACK_ROWS * 128).reshape(_PACK_ROWS, 128)
    pack = pack + 0.0 * chain[0, 0:1, 0:1]
    pack_all = _allgather_small(pack, "gather_small_grads")
    tot = _sum_devices(pack_all, "sum_small_grads").reshape(-1)
    rows_all = pack_all.reshape(N_DEV, -1)
    o = 3 * _N9
    g_norm_g_full = tot[o:o + 6 * D_MODEL].reshape(2, 3, D_MODEL)
    o += 6 * D_MODEL
    g_bias2 = tot[o:o + 512]
    o += 512
    g_w_a2_f_full = tot[o:o + rk * 256].reshape(rk, 256)
    o += rk * 256
    g_w_a2_b_full = tot[o:o + rk * 256].reshape(rk, 256)
    o += rk * 256
    g_sink = tot[o:o + A_HEADS]
    o += 128
    g_gla_g = tot[o:o + B_DV]
    o += B_DV
    g_pscale_full = tot[o:o + D_MODEL]
    o += D_MODEL
    g_final_g = tot[o:o + D_MODEL]
    d_mmc_tot = tot[2 * _N9:3 * _N9]
    g_b_mod = jnp.stack([tot[0:_N9] + d_mmc_tot, tot[_N9:2 * _N9]])

    zrows = jnp.zeros((_CROWS - N_DEV - 1, _N9), F32)
    d16 = jnp.stack([jnp.concatenate([rows_all[:, 0:_N9], d_mmc_tot[None], zrows], axis=0),
                     jnp.concatenate([rows_all[:, _N9:2 * _N9], jnp.zeros((1, _N9), F32), zrows], axis=0)])
    d16_k = lax.dynamic_slice(d16, (0, 0, k_me * mod_cols), (2, _CROWS, mod_cols))
    dmmc_k = lax.dynamic_slice(d_mmc_tot, (k_me * mod_cols,), (mod_cols,)).reshape(1, mod_cols)
    g_w_mod, c_part = _adaln_bwd(c16, d16_k, w_mod, dmmc_k, "adaln_bwd")
    c_parts = _allgather_small(c_part.reshape(8, 128), "gather_cctx")
    g_c_ctx = _cctx_grad(c_parts, c_ctx.reshape(8, 128), "cctx_grad").reshape(D_MODEL)

    def small(w, g, m, v, shape3, nm):
        return [o_.reshape(w.shape) for o_ in _adamw(w.reshape(shape3), [g.reshape(shape3[1:])],
                                                    m.reshape(shape3), v.reshape(shape3), 0, None, "adamw_" + nm)]

    def own(a, axis, size):
        return lax.dynamic_slice_in_dim(a, k_me * size, size, axis=axis)

    res["c_ctx"] = small(c_ctx, g_c_ctx, m_c_ctx, v_c_ctx, (1, 8, 128), "c_ctx")
    upd = _adamw(w_mod, [(g_w_mod, 1)], m_w_mod, v_w_mod, 1, None, "adamw_w_mod_1")
    res["w_mod"] = _adamw(w_mod, [(g_w_mod, 0)], m_w_mod, v_w_mod, 0, upd, "adamw_w_mod_0")
    res["b_mod"] = small(b_mod, g_b_mod, m_b_mod, v_b_mod, (1, 2, _N9), "b_mod")
    res["norm_g"] = small(norm_g, own(g_norm_g_full, 2, norm_g.shape[2]), m_norm_g, v_norm_g,
                          (1, 6, norm_g.shape[2]), "norm_g")
    res["w_a2_f"] = small(w_a2_f, own(g_w_a2_f_full, 1, w_a2_f.shape[2]), m_w_a2_f, v_w_a2_f,
                          (1, rk, w_a2_f.shape[2]), "w_a2_f")
    res["b_a_f"] = small(b_a_f, g_bias2[0:256], m_b_a_f, v_b_a_f, (1, 1, 256), "b_a_f")
    res["w_a2_b"] = small(w_a2_b, own(g_w_a2_b_full, 1, w_a2_b.shape[2]), m_w_a2_b, v_w_a2_b,
                          (1, rk, w_a2_b.shape[2]), "w_a2_b")
    res["b_a_b"] = small(b_a_b, g_bias2[256:512], m_b_a_b, v_b_a_b, (1, 1, 256), "b_a_b")
    res["sink"] = small(sink, g_sink, m_sink, v_sink, (1, 1, A_HEADS), "sink")
    res["gla_g"] = small(gla_g, g_gla_g, m_gla_g, v_gla_g, (1, 1, B_DV), "gla_g")
    res["pool_scale"] = small(pool_scale, own(g_pscale_full, 0, pool_scale.shape[1]), m_pool_scale, v_pool_scale,
                              (1, 1, pool_scale.shape[1]), "pool_scale")
    res["final_g"] = small(final_g, g_final_g, m_final_g, v_final_g, (1, 8, 128), "final_g")
    for wname, (w, _, _) in big_w.items():
        res[wname] = [o_.reshape(w.shape) for o_ in res[wname]]

    names = ["c_ctx", "w_mod", "b_mod", "norm_g", "ffn1_wi", "ffn1_wo", "ffn2_wi", "ffn2_wo", "w_in", "w_a2_f",
             "b_a_f", "w_a2_b", "b_a_b", "sink", "gla_g", "w_out", "w_pool", "pool_scale", "final_g"]
    outs = [loss, grad_x]
    for field in range(4):
        outs += [res[nm][field] for nm in names]
    return tuple(outs)
```

```python
import functools

import jax
import jax.numpy as jnp
import numpy as np
from jax import lax
from jax.experimental import pallas as pl
from jax.experimental.pallas import tpu as pltpu

F32 = jnp.float32
BF16 = jnp.bfloat16

D_MODEL = 1024
N_MOD = 9
D_FF = 2816
RMS_EPS = 1e-6
A_HEADS = 8
A_KV_HEADS = 2
A_HEAD_DIM = 64
WINDOW = 128
ROPE_BASE = 10000.0
GRID_W = 64
B_HEADS = 4
B_DK = 64
B_DV = 128
B_GATE_RANK = 16
B_GATE_NORM = 16.0
B_CHUNK = 64
POOL_WINDOWS = (2, 4, 8, 16)
POOL_GROUP = D_MODEL // len(POOL_WINDOWS)
PROJ_DIM = 2336

ADAM_LR = 0.001
ADAM_B1 = 0.9
ADAM_B2 = 0.999
ADAM_EPS = 1e-08
ADAM_WD = 0.01
ADAM_STEP = 10

N_CHIPS = 4
N_DEV = 8
ROW_TILE = 512
VMEM_LIMIT_BYTES = 56 * 1024 * 1024
MESH = pl.DeviceIdType.MESH

ZC_Q, ZC_QK, ZC_V, ZC_R, ZC_KV, ZC_G, ZC_W = 0, 512, 1024, 1536, 2048, 2304, 2432


def _cp(*sem):
    return pltpu.CompilerParams(dimension_semantics=sem if sem else None, vmem_limit_bytes=VMEM_LIMIT_BYTES)


def _dot(a, b):
    return jnp.dot(a, b, preferred_element_type=F32)


def _dot_nt(a, b):
    return lax.dot_general(a, b, (((1,), (1,)), ((), ())), preferred_element_type=F32)


def _dot_tn(a, b):
    return lax.dot_general(a, b, (((0,), (0,)), ((), ())), preferred_element_type=F32)


def _dot_hi(a, b):
    return jnp.dot(a, b, preferred_element_type=F32, precision=lax.Precision.HIGHEST)


def _dot_tn_hi(a, b):
    return lax.dot_general(a, b, (((0,), (0,)), ((), ())), preferred_element_type=F32,
                           precision=lax.Precision.HIGHEST)


def _sigmoid(x):
    return 1.0 / (1.0 + jnp.exp(-x))


MXU_COLS = 256


def _col_chunks(n):
    return [(c0, min(MXU_COLS, n - c0)) for c0 in range(0, n, MXU_COLS)]


WIDE_ROW_TILE = 1024


def _matmul_row_tile(rows, n_x):
    if rows % WIDE_ROW_TILE == 0 and n_x * ROW_TILE >= rows:
        return WIDE_ROW_TILE
    return ROW_TILE


def _resident(block_shape, index_map):
    return pl.BlockSpec(block_shape, index_map, pipeline_mode=pl.Buffered(1))


def _stream_of(i, n_x):
    return jnp.where(i >= n_x, 1, 0)


def _rms_mod_fwd(x, g3, mods, j, n_x, out_dtype, name):
    rows = x.shape[0]
    tm = ROW_TILE
    n_i = rows // tm

    def body(x_ref, g_ref, m_ref, o_ref):
        xv = x_ref[...]
        r = lax.rsqrt(jnp.mean(xv * xv, axis=-1, keepdims=True) + RMS_EPS)
        g = g_ref[j:j + 1, :]
        shift = m_ref[0, 3 * j:3 * j + 1, :]
        scale = m_ref[0, 3 * j + 1:3 * j + 2, :]
        o_ref[...] = (((xv * r) * g) * (1.0 + scale) + shift).astype(out_dtype)

    return pl.pallas_call(
        body, name=name, grid=(n_i,),
        in_specs=[pl.BlockSpec((tm, D_MODEL), lambda i: (i, 0)),
                  pl.BlockSpec((3, D_MODEL), lambda i: (0, 0)),
                  pl.BlockSpec((1, N_MOD, D_MODEL), lambda i: (_stream_of(i, n_x), 0, 0))],
        out_specs=pl.BlockSpec((tm, D_MODEL), lambda i: (i, 0)),
        out_shape=jax.ShapeDtypeStruct((rows, D_MODEL), out_dtype),
        compiler_params=_cp("parallel"),
    )(x, g3, mods)


def _rms_mod_bwd_tail(dh, xv, g, scale, stream, acc_ref, first):
    r = lax.rsqrt(jnp.mean(xv * xv, axis=-1, keepdims=True) + RMS_EPS)
    xhat = xv * r
    t1 = jnp.sum(dh, axis=0, keepdims=True)
    t2 = jnp.sum(dh * xhat, axis=0, keepdims=True)
    stats = jnp.concatenate([t1, t2 * g, t2 * (1.0 + scale)], axis=0)

    @pl.when(first)
    def _():
        acc_ref[...] = jnp.zeros_like(acc_ref)

    acc_ref[pl.ds(stream, 1)] += stats[None]
    dxh = dh * (g * (1.0 + scale))
    return r * (dxh - xhat * jnp.mean(dxh * xhat, axis=-1, keepdims=True))


def _ffn_up(x, g3, mods, jmod, n_x, w4, name):
    rows = x.shape[0]
    h = w4.shape[2]
    tm = _matmul_row_tile(rows, n_x)
    n_i = rows // tm

    def body(x_ref, g_ref, m_ref, wa_ref, wu_ref, hn_ref, au_ref, s_ref):
        xv = x_ref[...]
        r = lax.rsqrt(jnp.mean(xv * xv, axis=-1, keepdims=True) + RMS_EPS)
        g = g_ref[jmod:jmod + 1, :]
        shift = m_ref[0, 3 * jmod:3 * jmod + 1, :]
        scale = m_ref[0, 3 * jmod + 1:3 * jmod + 2, :]
        hv = (((xv * r) * g) * (1.0 + scale) + shift).astype(BF16)

        @pl.when(pl.program_id(0) == 0)
        def _():
            hn_ref[...] = hv

        for c0, cw in _col_chunks(h):
            cols = slice(c0, c0 + cw)
            a = _dot(hv, wa_ref[0, :, cols])
            u = _dot(hv, wu_ref[0, :, cols])
            sg = _sigmoid(a)
            silu = a * sg
            au_ref[0, :, cols] = (u * (sg * (1.0 + a * (1.0 - sg)))).astype(BF16)
            au_ref[1, :, cols] = silu.astype(BF16)
            s_ref[:, cols] = (silu * u).astype(BF16)

    return pl.pallas_call(
        body, name=name, grid=(2, n_i),
        in_specs=[pl.BlockSpec((tm, D_MODEL), lambda j, i: (i, 0)),
                  pl.BlockSpec((3, D_MODEL), lambda j, i: (0, 0)),
                  pl.BlockSpec((1, N_MOD, D_MODEL), lambda j, i: (_stream_of(i, n_x), 0, 0)),
                  pl.BlockSpec((1, D_MODEL, h), lambda j, i: (j, 0, 0)),
                  pl.BlockSpec((1, D_MODEL, h), lambda j, i: (j + 2, 0, 0))],
        out_specs=[pl.BlockSpec((tm, D_MODEL), lambda j, i: (jnp.where(j == 0, i, n_i - 1), 0)),
                   pl.BlockSpec((2, tm, h), lambda j, i: (0, i, j)),
                   pl.BlockSpec((tm, h), lambda j, i: (i, j))],
        out_shape=[jax.ShapeDtypeStruct((rows, D_MODEL), BF16),
                   jax.ShapeDtypeStruct((2, rows, 2 * h), BF16),
                   jax.ShapeDtypeStruct((rows, 2 * h), BF16)],
        compiler_params=_cp("arbitrary", "arbitrary"),
    )(x, g3, mods, w4, w4)


def _matmul_resid(a, w, xres, mods, gate_idx, coef, n_x, rows, name):
    k = a.shape[1]
    tm = _matmul_row_tile(rows, n_x)
    n_i = rows // tm

    def body(a_ref, w_ref, x_ref, m_ref, o_ref, f_ref):
        av = a_ref[...]
        for c0, cw in _col_chunks(D_MODEL):
            cols = slice(c0, c0 + cw)
            f = _dot(av, w_ref[:, cols])
            f_ref[:, cols] = f
            o_ref[:, cols] = x_ref[:, cols] + (coef * m_ref[0, gate_idx:gate_idx + 1, cols]) * f

    return pl.pallas_call(
        body, name=name, grid=(n_i,),
        in_specs=[pl.BlockSpec((tm, k), lambda i: (i, 0)),
                  _resident((k, D_MODEL), lambda i: (0, 0)),
                  pl.BlockSpec((tm, D_MODEL), lambda i: (i, 0)),
                  pl.BlockSpec((1, N_MOD, D_MODEL), lambda i: (_stream_of(i, n_x), 0, 0))],
        out_specs=[pl.BlockSpec((tm, D_MODEL), lambda i: (i, 0)),
                   pl.BlockSpec((tm, D_MODEL), lambda i: (i, 0))],
        out_shape=[jax.ShapeDtypeStruct((rows, D_MODEL), F32),
                   jax.ShapeDtypeStruct((rows, D_MODEL), F32)],
        compiler_params=_cp("parallel"),
    )(a, w, xres, mods)


def _gate_dy(dout, f, mods, gate_idx, coef, n_x, rows, name):
    tm = ROW_TILE
    n_i = rows // tm

    def body(d_ref, f_ref, m_ref, dy_ref, acc_ref):
        i = pl.program_id(0)
        dv = d_ref[...]
        gate = m_ref[0, gate_idx:gate_idx + 1, :]
        dy_ref[...] = (dv * (coef * gate)).astype(BF16)

        @pl.when(i == 0)
        def _():
            acc_ref[...] = jnp.zeros_like(acc_ref)

        part = coef * jnp.sum(dv * f_ref[...], axis=0, keepdims=True)
        acc_ref[pl.ds(_stream_of(i, n_x), 1)] += part[None]

    return pl.pallas_call(
        body, name=name, grid=(n_i,),
        in_specs=[pl.BlockSpec((tm, D_MODEL), lambda i: (i, 0)),
                  pl.BlockSpec((tm, D_MODEL), lambda i: (i, 0)),
                  pl.BlockSpec((1, N_MOD, D_MODEL), lambda i: (_stream_of(i, n_x), 0, 0))],
        out_specs=[pl.BlockSpec((tm, D_MODEL), lambda i: (i, 0)),
                   pl.BlockSpec((2, 1, D_MODEL), lambda i: (0, 0, 0))],
        out_shape=[jax.ShapeDtypeStruct((rows, D_MODEL), BF16),
                   jax.ShapeDtypeStruct((2, 1, D_MODEL), F32)],
        compiler_params=_cp("arbitrary"),
    )(dout, f, mods)


def _ffn_bwd_dz(dout, f, mods, gate_idx, coef, n_x, wo2, au, name):
    rows = dout.shape[0]
    h = wo2.shape[1]
    tm = ROW_TILE
    n_i = rows // tm

    def body(d_ref, f_ref, m_ref, wo_ref, au_ref, dy_ref, dz_ref, acc_ref):
        j, i = pl.program_id(0), pl.program_id(1)
        dv = d_ref[...]
        gate = m_ref[0, gate_idx:gate_idx + 1, :]
        dyb = (dv * (coef * gate)).astype(BF16)

        @pl.when((j == 0) & (i == 0))
        def _():
            acc_ref[...] = jnp.zeros_like(acc_ref)

        @pl.when(j == 0)
        def _():
            dy_ref[...] = dyb
            part = coef * jnp.sum(dv * f_ref[...], axis=0, keepdims=True)
            acc_ref[pl.ds(_stream_of(i, n_x), 1)] += part[None]

        for c0, cw in _col_chunks(h):
            cols = slice(c0, c0 + cw)
            ds = _dot_nt(dyb, wo_ref[0, cols, :])
            dz_ref[0, :, cols] = (ds * au_ref[0, :, cols].astype(F32)).astype(BF16)
            dz_ref[1, :, cols] = (ds * au_ref[1, :, cols].astype(F32)).astype(BF16)

    return pl.pallas_call(
        body, name=name, grid=(2, n_i),
        in_specs=[pl.BlockSpec((tm, D_MODEL), lambda j, i: (i, 0)),
                  pl.BlockSpec((tm, D_MODEL), lambda j, i: (jnp.where(j == 0, i, n_i - 1), 0)),
                  pl.BlockSpec((1, N_MOD, D_MODEL), lambda j, i: (_stream_of(i, n_x), 0, 0)),
                  pl.BlockSpec((1, h, D_MODEL), lambda j, i: (j, 0, 0)),
                  pl.BlockSpec((2, tm, h), lambda j, i: (0, i, j))],
        out_specs=[pl.BlockSpec((tm, D_MODEL), lambda j, i: (jnp.where(j == 0, i, n_i - 1), 0)),
                   pl.BlockSpec((2, tm, h), lambda j, i: (0, i, j)),
                   pl.BlockSpec((2, 1, D_MODEL), lambda j, i: (0, 0, 0))],
        out_shape=[jax.ShapeDtypeStruct((rows, D_MODEL), BF16),
                   jax.ShapeDtypeStruct((2, rows, 2 * h), BF16),
                   jax.ShapeDtypeStruct((2, 1, D_MODEL), F32)],
        compiler_params=_cp("arbitrary", "arbitrary"),
    )(dout, f, mods, wo2, au)


def _token_tile(rows):
    for tk in (2048, 1536, 1024):
        if rows % tk == 0:
            return tk
    return ROW_TILE


def _matmul_tn(a, b, a_spec, b_spec, out_shape, out_spec, grid, name):
    nd_a = len(a_spec.block_shape)
    nd_b = len(b_spec.block_shape)
    nd_o = len(out_spec.block_shape)
    k_axis = len(grid) - 1
    n_k = grid[k_axis]

    def body(a_ref, b_ref, o_ref, acc_ref):
        av = a_ref[(0,) * (nd_a - 2)]
        bv = b_ref[(0,) * (nd_b - 2)]
        part = _dot_tn(av, bv)
        k = pl.program_id(k_axis)

        @pl.when(k == 0)
        def _():
            acc_ref[...] = part

        @pl.when(k > 0)
        def _():
            acc_ref[...] += part

        @pl.when(k == n_k - 1)
        def _():
            o_ref[(0,) * (nd_o - 2)] = acc_ref[...].astype(BF16)

    return pl.pallas_call(
        body, name=name, grid=grid, in_specs=[a_spec, b_spec], out_specs=out_spec,
        out_shape=jax.ShapeDtypeStruct(out_shape, BF16),
        scratch_shapes=[pltpu.VMEM(tuple(out_spec.block_shape[-2:]), F32)],
        compiler_params=_cp(*(("arbitrary",) * len(grid))),
    )(a, b)


def _bwd_dx(pairs, x, dres, dres_tiles, g3, mods, j, n_x, name, out_tiles=None):
    rows = x.shape[0]
    tm = ROW_TILE
    n_i = rows // tm
    n_o = n_i if out_tiles is None else out_tiles
    n_p = len(pairs)
    nds = [(len(p[1].block_shape), len(p[3].block_shape)) for p in pairs]

    def body(*refs):
        dz_refs = refs[0:2 * n_p:2]
        w_refs = refs[1:2 * n_p:2]
        x_ref, dres_ref, g_ref, m_ref, dx_ref, acc_ref = refs[2 * n_p:]
        i = pl.program_id(0)
        dzs = [dz_refs[p][(0,) * (nds[p][0] - 2)] for p in range(n_p)]
        pieces = []
        for c0, cw in _col_chunks(D_MODEL):
            acc = None
            for p in range(n_p):
                lead = (0,) * (nds[p][1] - 2)
                part = _dot_nt(dzs[p], w_refs[p][lead + (slice(c0, c0 + cw), slice(None))])
                acc = part if acc is None else acc + part
            pieces.append(acc)
        dh = jnp.concatenate(pieces, axis=1)
        g = g_ref[j:j + 1, :]
        scale = m_ref[0, 3 * j + 1:3 * j + 2, :]
        dx = _rms_mod_bwd_tail(dh, x_ref[...], g, scale, _stream_of(i, n_x), acc_ref, i == 0)
        dres_v = jnp.where(i < dres_tiles, dres_ref[...], 0.0)

        @pl.when(i < n_o)
        def _():
            dx_ref[...] = dres_v + dx

    in_specs, args = [], []
    for dz, dz_spec, w, w_spec in pairs:
        in_specs += [dz_spec, w_spec]
        args += [dz, w]
    in_specs += [pl.BlockSpec((tm, D_MODEL), lambda i: (i, 0)),
                 pl.BlockSpec((tm, D_MODEL), lambda i: (jnp.minimum(i, dres_tiles - 1), 0)),
                 pl.BlockSpec((3, D_MODEL), lambda i: (0, 0)),
                 pl.BlockSpec((1, N_MOD, D_MODEL), lambda i: (_stream_of(i, n_x), 0, 0))]
    args += [x, dres, g3, mods]
    return pl.pallas_call(
        body, name=name, grid=(n_i,), in_specs=in_specs,
        out_specs=[pl.BlockSpec((tm, D_MODEL), lambda i: (jnp.minimum(i, n_o - 1), 0)),
                   pl.BlockSpec((2, 3, D_MODEL), lambda i: (0, 0, 0))],
        out_shape=[jax.ShapeDtypeStruct((n_o * tm, D_MODEL), F32),
                   jax.ShapeDtypeStruct((2, 3, D_MODEL), F32)],
        compiler_params=_cp("arbitrary"),
    )(*args)


def _ffn_forward(x, g3, mods, j, w4_in, w4_out_of, n_x, name):
    rows = x.shape[0]
    hn, au, s = _ffn_up(x, g3, mods, j, n_x, w4_in, name + "_up")
    w4_out, dep = w4_out_of(s)
    if dep is not None:
        mods = mods + dep[0:1, 0:1]
    wo = w4_out.reshape(D_FF, D_MODEL)
    out, f = _matmul_resid(s, wo, x, mods, 3 * j + 2, 0.5, n_x, rows, name + "_down")
    return out, (x, hn, au, s, f), w4_out


def _ffn_backward(dout, saved, g3, mods, j, w4_in, w4_out, n_x, send, name, out_tiles=None):
    x, hn, au, s, f = saved
    rows = x.shape[0]
    tm = ROW_TILE
    n_i = rows // tm
    h = w4_in.shape[2]
    wo2 = w4_out.reshape(2, h, D_MODEL)
    dy, dz, dgate = _ffn_bwd_dz(dout, f, mods, 3 * j + 2, 0.5, n_x, wo2, au, name + "_dz")
    tk = _token_tile(rows)
    n_k = rows // tk
    d_wi = _matmul_tn(
        hn, dz, pl.BlockSpec((tk, D_MODEL), lambda q, k: (k, 0)),
        pl.BlockSpec((1, tk, h), lambda q, k: (q // 2, k, q % 2)),
        (4, D_MODEL, h), pl.BlockSpec((1, D_MODEL, h), lambda q, k: (q, 0, 0)), (4, n_k), name + "_dwi")
    mods = mods + send(d_wi, "wi")
    d_wo = _matmul_tn(
        s, dy, pl.BlockSpec((tk, h), lambda n, k: (k, n)), pl.BlockSpec((tk, D_MODEL), lambda n, k: (k, 0)),
        (D_FF, D_MODEL), pl.BlockSpec((h, D_MODEL), lambda n, k: (n, 0)), (2, n_k), name + "_dwo")
    mods = mods + send(d_wo.reshape(w4_out.shape), "wo")
    pairs = [(dz, pl.BlockSpec((1, tm, h), functools.partial(lambda q, i: (q // 2, i, q % 2), q)),
              w4_in, pl.BlockSpec((1, D_MODEL, h), functools.partial(lambda q, i: (q, 0, 0), q)))
             for q in range(4)]
    dx, stats = _bwd_dx(pairs, x, dout, n_i, g3, mods, j, n_x, name + "_dx", out_tiles)
    return dx, stats, dgate


def _matmul_nt(a, w, name):
    rows, k = a.shape
    n = w.shape[0]
    tm = ROW_TILE

    def body(a_ref, w_ref, o_ref):
        o_ref[...] = _dot_nt(a_ref[...], w_ref[...])

    return pl.pallas_call(
        body, name=name, grid=(rows // tm,),
        in_specs=[pl.BlockSpec((tm, k), lambda i: (i, 0)), pl.BlockSpec((n, k), lambda i: (0, 0))],
        out_specs=pl.BlockSpec((tm, n), lambda i: (i, 0)),
        out_shape=jax.ShapeDtypeStruct((rows, n), F32),
        compiler_params=_cp("parallel"),
    )(a, w)


def _rope_tables(t_len, rows):
    n = A_HEAD_DIM // 4
    freqs = ROPE_BASE ** (-jnp.arange(n, dtype=F32) / n)
    t = jnp.arange(t_len)
    ang_r = (t // GRID_W).astype(F32)[:, None] * freqs
    ang_c = (t % GRID_W).astype(F32)[:, None] * freqs
    cos = jnp.concatenate([jnp.cos(ang_r), jnp.cos(ang_r), jnp.cos(ang_c), jnp.cos(ang_c)], axis=1)
    sin = jnp.concatenate([-jnp.sin(ang_r), jnp.sin(ang_r), -jnp.sin(ang_c), jnp.sin(ang_c)], axis=1)
    cos = jnp.concatenate([cos, jnp.ones((rows - t_len, A_HEAD_DIM), F32)], axis=0)
    sin = jnp.concatenate([sin, jnp.zeros((rows - t_len, A_HEAD_DIM), F32)], axis=0)
    return jnp.concatenate([cos, cos, sin, sin], axis=1)


def _swap16(x):
    n = x.shape[1]
    lane = lax.broadcasted_iota(jnp.int32, x.shape, 1)
    first = jnp.bitwise_and(lane, 16) == 0
    return jnp.where(first, pltpu.roll(x, n - 16, 1), pltpu.roll(x, 16, 1))


def _log_sigmoid(x):
    return jnp.minimum(x, 0.0) - jnp.log(1.0 + jnp.exp(-jnp.abs(x)))


def _proj_fwd(h, wcat, wg2, bias2, cs, name):
    rows = h.shape[0]
    tm = ROW_TILE

    def body(h_ref, w_ref, wg_ref, b_ref, cs_ref, zc_ref, la_ref):
        z = _dot(h_ref[...], w_ref[...])
        cos = cs_ref[:, 0:128]
        sin = cs_ref[:, 128:256]
        cosq = jnp.concatenate([cos] * 4, axis=1)
        sinq = jnp.concatenate([sin] * 4, axis=1)
        q = z[:, ZC_Q:ZC_QK]
        zc_ref[:, ZC_Q:ZC_QK] = q * cosq + _swap16(q) * sinq
        zc_ref[:, ZC_QK:ZC_KV] = z[:, ZC_QK:ZC_KV]
        kk = z[:, ZC_KV:ZC_KV + 128]
        zc_ref[:, ZC_KV:ZC_KV + 128] = kk * cos + _swap16(kk) * sin
        zc_ref[:, ZC_KV + 128:ZC_W] = z[:, ZC_KV + 128:ZC_W]
        zg = z[:, ZC_G:ZC_W]
        pre = _dot(zg.astype(BF16), wg_ref[...]) + b_ref[...]
        la_ref[...] = _log_sigmoid(pre) / B_GATE_NORM

    return pl.pallas_call(
        body, name=name, grid=(rows // tm,),
        in_specs=[pl.BlockSpec((tm, D_MODEL), lambda i: (i, 0)),
                  pl.BlockSpec((D_MODEL, ZC_W), lambda i: (0, 0)),
                  pl.BlockSpec((128, 512), lambda i: (0, 0)),
                  pl.BlockSpec((1, 512), lambda i: (0, 0)),
                  pl.BlockSpec((tm, 256), lambda i: (i, 0))],
        out_specs=[pl.BlockSpec((tm, ZC_W), lambda i: (i, 0)),
                   pl.BlockSpec((tm, 512), lambda i: (i, 0))],
        out_shape=[jax.ShapeDtypeStruct((rows, ZC_W), F32),
                   jax.ShapeDtypeStruct((rows, 512), F32)],
        compiler_params=_cp("parallel"),
    )(h, wcat, wg2, bias2, cs)


_QB = WINDOW


def _attn_specs(t_len, l_ctx):
    nb = t_len // _QB
    kvb = ZC_KV // 256
    return [pl.BlockSpec(memory_space=pltpu.SMEM),
            pl.BlockSpec((_QB, 512), lambda n: (n, 0)),
            pl.BlockSpec((_QB, 256), lambda n: (jnp.maximum(n - 1, 0), kvb)),
            pl.BlockSpec((_QB, 256), lambda n: (n, kvb)),
            pl.BlockSpec((_QB, 256), lambda n: (n + 1, kvb)),
            pl.BlockSpec((l_ctx, 256), lambda n: (t_len // l_ctx, kvb))], nb


_HEAD_PAIRS = ((0, 1), (2, 3))


def _attn_keys(kp, kc, kn, kx, g):
    hd = A_HEAD_DIM
    ks = slice(g * hd, (g + 1) * hd)
    vs = slice(128 + g * hd, 128 + (g + 1) * hd)
    kb = jnp.concatenate([kp[:, ks], kc[:, ks], kn[:, ks]], axis=0).astype(BF16)
    vb = jnp.concatenate([kp[:, vs], kc[:, vs], kn[:, vs]], axis=0).astype(BF16)
    return kb, vb, kx[:, ks].astype(BF16), kx[:, vs].astype(BF16)


def _attn_probs(n, t_len, sink_ref, qv, kb, kxb, g, rs):
    hd = A_HEAD_DIM
    qg = jnp.concatenate([qv[:, (4 * g + r) * hd:(4 * g + r + 1) * hd] for r in rs], axis=0).astype(BF16)
    qi = lax.broadcasted_iota(jnp.int32, (_QB, 3 * _QB), 0)
    kj = lax.broadcasted_iota(jnp.int32, (_QB, 3 * _QB), 1)
    kpos = n * _QB - _QB + kj
    valid = (kpos >= 0) & (kpos < t_len) & (jnp.abs(kj - _QB - qi) <= WINDOW)
    valid = jnp.concatenate([valid] * len(rs), axis=0)
    scale = hd ** -0.5
    s = jnp.where(valid, _dot_nt(qg, kb) * scale, -jnp.inf)
    sc = _dot_nt(qg, kxb) * scale
    sk = jnp.concatenate([jnp.full((_QB, 1), sink_ref[4 * g + r], F32) for r in rs], axis=0)
    m = jnp.maximum(jnp.maximum(jnp.max(s, axis=-1, keepdims=True), jnp.max(sc, axis=-1, keepdims=True)), sk)
    p = jnp.exp(s - m)
    pc = jnp.exp(sc - m)
    ps = jnp.exp(sk - m)
    inv = 1.0 / (jnp.sum(p, axis=-1, keepdims=True) + jnp.sum(pc, axis=-1, keepdims=True) + ps)
    return p, pc, ps, inv, qg


def _attn_fwd(zc, sink, t_len, l_ctx, name):
    in_specs, nb = _attn_specs(t_len, l_ctx)

    def body(sink_ref, q_ref, kp_ref, kc_ref, kn_ref, kx_ref, o_ref):
        n = pl.program_id(0)
        qv = q_ref[...]
        outs = []
        for g in range(A_KV_HEADS):
            kb, vb, kxb, vxb = _attn_keys(kp_ref[...], kc_ref[...], kn_ref[...], kx_ref[...], g)
            for rs in _HEAD_PAIRS:
                p, pc, _, inv, _ = _attn_probs(n, t_len, sink_ref, qv, kb, kxb, g, rs)
                o = (_dot(p.astype(BF16), vb) + _dot(pc.astype(BF16), vxb)) * inv
                outs += [o[i * _QB:(i + 1) * _QB] for i in range(len(rs))]
        o_ref[...] = jnp.concatenate(outs, axis=1)

    return pl.pallas_call(
        body, name=name, grid=(nb,), in_specs=in_specs,
        out_specs=pl.BlockSpec((_QB, 512), lambda n: (n, 0)),
        out_shape=jax.ShapeDtypeStruct((t_len, 512), F32),
        compiler_params=_cp("parallel"),
    )(sink, zc, zc, zc, zc, zc)


def _attn_bwd(zc, sink, o, dcat, t_len, l_ctx, name):
    rows = zc.shape[0]
    in_specs, nb = _attn_specs(t_len, l_ctx)
    in_specs = in_specs + [pl.BlockSpec((_QB, 512), lambda n: (n, 0)), pl.BlockSpec((_QB, 512), lambda n: (n, 0))]
    hd = A_HEAD_DIM
    scale = hd ** -0.5

    def body(sink_ref, q_ref, kp_ref, kc_ref, kn_ref, kx_ref, o_ref, do_ref, dq_ref, dkv_ref, dsink_ref):
        n = pl.program_id(0)

        @pl.when(n == 0)
        def _():
            dkv_ref[...] = jnp.zeros_like(dkv_ref)
            dsink_ref[...] = jnp.zeros_like(dsink_ref)

        qv = q_ref[...]
        ov = o_ref[...]
        dov = do_ref[...]
        dqs, dkbs, dvbs, dkxs, dvxs, dsinks = [], [], [], [], [], []
        for g in range(A_KV_HEADS):
            kb, vb, kxb, vxb = _attn_keys(kp_ref[...], kc_ref[...], kn_ref[...], kx_ref[...], g)
            parts = []
            for rs in _HEAD_PAIRS:
                p, pc, ps, inv, qg = _attn_probs(n, t_len, sink_ref, qv, kb, kxb, g, rs)
                og = jnp.concatenate([ov[:, (4 * g + r) * hd:(4 * g + r + 1) * hd] for r in rs], axis=0)
                dog = jnp.concatenate([dov[:, (4 * g + r) * hd:(4 * g + r + 1) * hd] for r in rs], axis=0)
                delta = jnp.sum(og * dog, axis=-1, keepdims=True)
                dogb = dog.astype(BF16)
                pn = p * inv
                pcn = pc * inv
                ds = (pn * (_dot_nt(dogb, vb) - delta) * scale).astype(BF16)
                dsc = (pcn * (_dot_nt(dogb, vxb) - delta) * scale).astype(BF16)
                dsk = (ps * inv) * (0.0 - delta)
                dqg = _dot(ds, kb) + _dot(dsc, kxb)
                dqs += [dqg[i * _QB:(i + 1) * _QB] for i in range(len(rs))]
                parts.append((_dot_tn(ds, qg), _dot_tn(pn.astype(BF16), dogb),
                              _dot_tn(dsc, qg), _dot_tn(pcn.astype(BF16), dogb)))
                for i in range(len(rs)):
                    tot = jnp.sum(dsk[i * _QB:(i + 1) * _QB], axis=0, keepdims=True)
                    dsinks.append(jnp.broadcast_to(tot, (1, 128)))
            dkbs.append(parts[0][0] + parts[1][0])
            dvbs.append(parts[0][1] + parts[1][1])
            dkxs.append(parts[0][2] + parts[1][2])
            dvxs.append(parts[0][3] + parts[1][3])
        dsink_ref[...] += jnp.concatenate(dsinks, axis=0)
        dq_ref[...] = jnp.concatenate(dqs, axis=1)
        band = jnp.concatenate(dkbs + dvbs, axis=1)
        ctxc = jnp.concatenate(dkxs + dvxs, axis=1)
        r_prev = pl.multiple_of(jnp.maximum(n - 1, 0) * _QB, _QB)
        r_cur = pl.multiple_of(n * _QB, _QB)
        r_next = pl.multiple_of((n + 1) * _QB, _QB)
        dkv_ref[pl.ds(r_prev, _QB), :] += band[0:_QB]
        dkv_ref[pl.ds(r_cur, _QB), :] += band[_QB:2 * _QB]
        dkv_ref[pl.ds(r_next, _QB), :] += band[2 * _QB:3 * _QB]
        dkv_ref[t_len:t_len + l_ctx, :] += ctxc

    return pl.pallas_call(
        body, name=name, grid=(nb,), in_specs=in_specs,
        out_specs=[pl.BlockSpec((_QB, 512), lambda n: (n, 0)),
                   pl.BlockSpec((rows, 256), lambda n: (0, 0)),
                   pl.BlockSpec((8, 128), lambda n: (0, 0))],
        out_shape=[jax.ShapeDtypeStruct((t_len, 512), F32),
                   jax.ShapeDtypeStruct((rows, 256), F32),
                   jax.ShapeDtypeStruct((8, 128), F32)],
        compiler_params=_cp("arbitrary"),
    )(sink, zc, zc, zc, zc, zc, o, dcat)


_GC = B_CHUNK


def _split_bf16(a):
    hi = a.astype(BF16)
    return hi, (a - hi.astype(F32)).astype(BF16)


def _gla_chunk_terms(qk, la, reverse):
    q = qk[:, 0:256]
    k = qk[:, 256:512]
    off = 256 if reverse else 0
    lad = la[:, off:off + 256]
    ii = lax.broadcasted_iota(jnp.int32, (_GC, _GC), 0)
    jj = lax.broadcasted_iota(jnp.int32, (_GC, _GC), 1)
    mask = (jj >= ii) if reverse else (jj <= ii)
    tri = jnp.where(mask, 1.0, 0.0).astype(BF16)
    la_hi, la_lo = _split_bf16(lad)
    g = _dot(tri, la_hi) + _dot(tri, la_lo)
    gl = jnp.sum(lad, axis=0, keepdims=True)
    eg = jnp.exp(g)
    eng = jnp.exp(-g)
    eend = jnp.exp(gl - g)
    sc = B_DK ** -0.5
    qt = q * (sc * eg)
    kt = k * eng
    ke = k * eend
    return mask, tri, gl, eg, eng, eend, qt, kt, ke


def _head(a, hh, width):
    return a[:, hh * width:(hh + 1) * width]


def _same_head(rows, cols, row_shift, col_shift):
    r = jnp.right_shift(lax.broadcasted_iota(jnp.int32, (rows, cols), 0), row_shift)
    c = jnp.right_shift(lax.broadcasted_iota(jnp.int32, (rows, cols), 1), col_shift)
    return r == c


def _block_diag_rows(x, col_shift):
    tiled = jnp.concatenate([x] * B_HEADS, axis=0)
    return jnp.where(_same_head(tiled.shape[0], tiled.shape[1], 6, col_shift), tiled, jnp.zeros_like(tiled))


def _fold_heads(x):
    c = x.shape[0] // B_HEADS
    return (x[0:c] + x[c:2 * c]) + (x[2 * c:3 * c] + x[3 * c:4 * c])


def _chunk_mask4(reverse):
    ii = lax.broadcasted_iota(jnp.int32, (_GC, B_HEADS * _GC), 0)
    jj = jnp.bitwise_and(lax.broadcasted_iota(jnp.int32, (_GC, B_HEADS * _GC), 1), _GC - 1)
    return (jj >= ii) if reverse else (jj <= ii)


_ST_SHAPE = (B_HEADS * B_DV, B_HEADS * B_DK)


def _state_blocks(t):
    return [t[hh * B_DV:(hh + 1) * B_DV, hh * B_DK:(hh + 1) * B_DK] for hh in range(B_HEADS)]


def _state_from_blocks(blocks):
    full = jnp.concatenate([jnp.concatenate([b] * B_HEADS, axis=1) for b in blocks], axis=0)
    return jnp.where(_same_head(_ST_SHAPE[0], _ST_SHAPE[1], 7, 6), full, 0.0)


def _gla_fwd(zc, la, dep, t_len, l_ctx, name):
    rows = zc.shape[0]
    n_x = t_len // _GC
    n_c = n_x + l_ctx // _GC
    qkb, vb = ZC_QK // 512, ZC_V // 512

    def ch_f(c):
        return lax.rem(c + n_x, n_c)

    def ch_r(c):
        return n_c - 1 - c

    def body(qkf_ref, vf_ref, laf_ref, qkr_ref, vr_ref, lar_ref, dep_ref, of_ref, or_ref, spf_ref, spr_ref, stf, strv):
        del dep_ref
        c = pl.program_id(0)

        @pl.when(c == 0)
        def _():
            stf[...] = jnp.zeros_like(stf)
            strv[...] = jnp.zeros_like(strv)

        results = []
        for qk_ref, v_ref, la_ref, st, reverse in ((qkf_ref, vf_ref, laf_ref, stf, False),
                                                   (qkr_ref, vr_ref, lar_ref, strv, True)):
            mask, _, gl, _, _, _, qt, kt, ke = _gla_chunk_terms(qk_ref[...], la_ref[...], reverse)
            vbf = v_ref[...].astype(BF16)
            qtb, keb = qt.astype(BF16), ke.astype(BF16)
            kbd = _block_diag_rows(kt.astype(BF16), 6)
            vbd = _block_diag_rows(vbf, 7)
            mask4 = _chunk_mask4(reverse)
            t_prev = st[...]
            att = jnp.where(mask4, _dot_nt(qtb, kbd), 0.0).astype(BF16)
            o_all = _dot(att, vbd) + _dot_nt(qtb, t_prev.astype(BF16))
            t_new = t_prev * jnp.exp(gl) + jnp.where(_same_head(_ST_SHAPE[0], _ST_SHAPE[1], 7, 6),
                                                     _dot_tn(vbf, keb), 0.0)
            results.append((o_all, t_prev, t_new))
        for (o_all, t_prev, t_new), o_ref, sp_ref, st in zip(results, (of_ref, or_ref), (spf_ref, spr_ref), (stf, strv)):
            o_ref[...] = o_all
            for hh, blk in enumerate(_state_blocks(t_prev)):
                sp_ref[0, hh] = blk
            st[...] = t_new

    st_shape = (B_HEADS, B_DV, B_DK)
    return pl.pallas_call(
        body, name=name, grid=(n_c,),
        in_specs=[pl.BlockSpec((_GC, 512), lambda c: (ch_f(c), qkb)),
                  pl.BlockSpec((_GC, 512), lambda c: (ch_f(c), vb)),
                  pl.BlockSpec((_GC, 512), lambda c: (ch_f(c), 0)),
                  pl.BlockSpec((_GC, 512), lambda c: (ch_r(c), qkb)),
                  pl.BlockSpec((_GC, 512), lambda c: (ch_r(c), vb)),
                  pl.BlockSpec((_GC, 512), lambda c: (ch_r(c), 0)),
                  pl.BlockSpec((8, 128), lambda c: (0, 0))],
        out_specs=[pl.BlockSpec((_GC, 512), lambda c: (ch_f(c), 0)),
                   pl.BlockSpec((_GC, 512), lambda c: (ch_r(c), 0)),
                   pl.BlockSpec((1,) + st_shape, lambda c: (c, 0, 0, 0)),
                   pl.BlockSpec((1,) + st_shape, lambda c: (c, 0, 0, 0))],
        out_shape=[jax.ShapeDtypeStruct((rows, 512), F32), jax.ShapeDtypeStruct((rows, 512), F32),
                   jax.ShapeDtypeStruct((n_c,) + st_shape, F32), jax.ShapeDtypeStruct((n_c,) + st_shape, F32)],
        scratch_shapes=[pltpu.VMEM(_ST_SHAPE, F32), pltpu.VMEM(_ST_SHAPE, F32)],
        compiler_params=_cp("arbitrary"),
    )(zc, zc, la, zc, zc, la, dep)


def _gla_bwd(zc, la, spf, spr, dosum, t_len, l_ctx, name):
    rows = zc.shape[0]
    n_x = t_len // _GC
    n_c = n_x + l_ctx // _GC
    n_all = rows // _GC
    qkb, vb = ZC_QK // 512, ZC_V // 512

    def scan_of(c):
        return jnp.maximum(n_c - 1 - c, 0)

    def ch_f(c):
        return jnp.where(c < n_c, lax.rem(scan_of(c) + n_x, n_c), c)

    def ch_r(c):
        return c

    def do_of(ch):
        return jnp.minimum(ch, n_x - 1)

    def body(qkf_ref, vf_ref, laf_ref, spf_ref, dof_ref, qkr_ref, vr_ref, lar_ref, spr_ref, dor_ref,
             dqkf_ref, dvf_ref, dlaf_ref, dqkr_ref, dvr_ref, dlar_ref, dsf, dsr):
        c = pl.program_id(0)

        @pl.when(c == 0)
        def _():
            dsf[...] = jnp.zeros_like(dsf)
            dsr[...] = jnp.zeros_like(dsr)

        @pl.when(c >= n_c)
        def _():
            for r in (dqkf_ref, dvf_ref, dlaf_ref, dqkr_ref, dvr_ref, dlar_ref):
                r[...] = jnp.zeros_like(r)

        @pl.when(c < n_c)
        def _():
            sc = B_DK ** -0.5
            results = []
            for qk_ref, v_ref, la_ref, sp_ref, do_ref, dst, reverse, ch in (
                    (qkf_ref, vf_ref, laf_ref, spf_ref, dof_ref, dsf, False, ch_f(c)),
                    (qkr_ref, vr_ref, lar_ref, spr_ref, dor_ref, dsr, True, ch_r(c))):
                mask, tri, gl, eg, eng, eend, qt, kt, ke = _gla_chunk_terms(qk_ref[...], la_ref[...], reverse)
                vbf = v_ref[...].astype(BF16)
                dob = jnp.where(ch < n_x, do_ref[...], 0.0).astype(BF16)
                qtb, keb = qt.astype(BF16), ke.astype(BF16)
                kbd = _block_diag_rows(kt.astype(BF16), 6)
                vbd = _block_diag_rows(vbf, 7)
                mask4 = _chunk_mask4(reverse)
                egl = jnp.exp(gl)
                t_prev = _state_from_blocks([sp_ref[0, hh] for hh in range(B_HEADS)])
                dt_new = dst[...]
                tpb, dtb = t_prev.astype(BF16), dt_new.astype(BF16)
                att = jnp.where(mask4, _dot_nt(qtb, kbd), 0.0).astype(BF16)
                datt = jnp.where(mask4, _dot_nt(dob, vbd), 0.0).astype(BF16)
                dqt = _dot(datt, kbd) + _dot(dob, tpb)
                dkt = _fold_heads(jnp.where(_same_head(256, 256, 6, 6), _dot_tn(datt, qtb), 0.0))
                dv = _fold_heads(jnp.where(_same_head(256, 512, 6, 7), _dot_tn(att, dob), 0.0)) + _dot_nt(keb, dtb)
                dke = _dot(vbf, dtb)
                dt_prev = dt_new * egl + jnp.where(_same_head(_ST_SHAPE[0], _ST_SHAPE[1], 7, 6),
                                                   _dot_tn(dob, qtb), 0.0)
                dgl = (jnp.sum(dke * ke, axis=0, keepdims=True)
                       + jnp.sum(dt_new * t_prev, axis=0, keepdims=True) * egl)
                dg_hi, dg_lo = _split_bf16(dqt * qt - dkt * kt - dke * ke)
                dla = _dot_tn(tri, dg_hi) + _dot_tn(tri, dg_lo) + dgl
                dqk = jnp.concatenate([dqt * (sc * eg), dkt * eng + dke * eend], axis=1)
                results.append((dqk, dv, dla, dt_prev))
            for (dqk, dv, dla, dt_prev), dqk_ref, dv_ref, dla_ref, dst in zip(
                    results, (dqkf_ref, dqkr_ref), (dvf_ref, dvr_ref), (dlaf_ref, dlar_ref), (dsf, dsr)):
                dqk_ref[...] = dqk
                dv_ref[...] = dv
                dla_ref[...] = dla
                dst[...] = dt_prev

    st_shape = (B_HEADS, B_DV, B_DK)

    def side(chf):
        return [pl.BlockSpec((_GC, 512), lambda c: (chf(c), qkb)),
                pl.BlockSpec((_GC, 512), lambda c: (chf(c), vb)),
                pl.BlockSpec((_GC, 512), lambda c: (chf(c), 0)),
                pl.BlockSpec((1,) + st_shape, lambda c: (scan_of(c), 0, 0, 0)),
                pl.BlockSpec((_GC, 512), lambda c: (do_of(chf(c)), 0))]

    def out_side(chf):
        return [pl.BlockSpec((_GC, 512), lambda c: (chf(c), 0)),
                pl.BlockSpec((_GC, 512), lambda c: (chf(c), 0)),
                pl.BlockSpec((_GC, 256), lambda c: (chf(c), 0))]

    shp = [jax.ShapeDtypeStruct((rows, 512), F32), jax.ShapeDtypeStruct((rows, 512), F32),
           jax.ShapeDtypeStruct((rows, 256), F32)]
    return pl.pallas_call(
        body, name=name, grid=(n_all,),
        in_specs=side(ch_f) + side(ch_r),
        out_specs=out_side(ch_f) + out_side(ch_r),
        out_shape=shp + shp,
        scratch_shapes=[pltpu.VMEM(_ST_SHAPE, F32), pltpu.VMEM(_ST_SHAPE, F32)],
        compiler_params=_cp("arbitrary"),
    )(zc, zc, la, spf, dosum, zc, zc, la, spr, dosum)


def _gla_out_fwd(o_a, o_f, o_r, zc, gla_g, t_len, name):
    tm = ROW_TILE
    rb = ZC_R // 512

    def body(oa_ref, of_ref, or_ref, r_ref, g_ref, cat_ref):
        osum = of_ref[...] + or_ref[...]
        g = g_ref[...]
        pieces = []
        for hh in range(B_HEADS):
            oh = osum[:, hh * B_DV:(hh + 1) * B_DV]
            rs = lax.rsqrt(jnp.mean(oh * oh, axis=-1, keepdims=True) + RMS_EPS)
            pieces.append((oh * rs) * g)
        r = r_ref[...]
        cat_ref[:, 0:512] = oa_ref[...].astype(BF16)
        cat_ref[:, 512:1024] = (jnp.concatenate(pieces, axis=1) * (r * _sigmoid(r))).astype(BF16)

    return pl.pallas_call(
        body, name=name, grid=(t_len // tm,),
        in_specs=[pl.BlockSpec((tm, 512), lambda i: (i, 0)),
                  pl.BlockSpec((tm, 512), lambda i: (i, 0)),
                  pl.BlockSpec((tm, 512), lambda i: (i, 0)),
                  pl.BlockSpec((tm, 512), lambda i: (i, rb)),
                  pl.BlockSpec((1, B_DV), lambda i: (0, 0))],
        out_specs=pl.BlockSpec((tm, D_MODEL), lambda i: (i, 0)),
        out_shape=jax.ShapeDtypeStruct((t_len, D_MODEL), BF16),
        compiler_params=_cp("parallel"),
    )(o_a, o_f, o_r, zc, gla_g)


def _gla_out_bwd(dcat, o_f, o_r, zc, gla_g, t_len, name):
    tm = ROW_TILE
    rb = ZC_R // 512

    def body(d_ref, of_ref, or_ref, r_ref, g_ref, dos_ref, dr_ref, dg_ref):
        i = pl.program_id(0)
        osum = of_ref[...] + or_ref[...]
        g = g_ref[...]
        r = r_ref[...]
        dgo = d_ref[...]
        sg = _sigmoid(r)
        dnrmg = dgo * (r * sg)
        nrms, dos = [], []
        dg_acc = jnp.zeros((1, B_DV), F32)
        for hh in range(B_HEADS):
            oh = osum[:, hh * B_DV:(hh + 1) * B_DV]
            rs = lax.rsqrt(jnp.mean(oh * oh, axis=-1, keepdims=True) + RMS_EPS)
            nrm = oh * rs
            dn = dnrmg[:, hh * B_DV:(hh + 1) * B_DV]
            dg_acc = dg_acc + jnp.sum(dn * nrm, axis=0, keepdims=True)
            dnn = dn * g
            dos.append(rs * (dnn - nrm * jnp.mean(dnn * nrm, axis=-1, keepdims=True)))
            nrms.append(nrm * g)
        dos_ref[...] = jnp.concatenate(dos, axis=1)
        dr_ref[...] = dgo * jnp.concatenate(nrms, axis=1) * (sg * (1.0 + r * (1.0 - sg)))

        @pl.when(i == 0)
        def _():
            dg_ref[...] = jnp.zeros_like(dg_ref)

        dg_ref[...] += dg_acc

    return pl.pallas_call(
        body, name=name, grid=(t_len // tm,),
        in_specs=[pl.BlockSpec((tm, 512), lambda i: (i, 1)),
                  pl.BlockSpec((tm, 512), lambda i: (i, 0)),
                  pl.BlockSpec((tm, 512), lambda i: (i, 0)),
                  pl.BlockSpec((tm, 512), lambda i: (i, rb)),
                  pl.BlockSpec((1, B_DV), lambda i: (0, 0))],
        out_specs=[pl.BlockSpec((tm, 512), lambda i: (i, 0)),
                   pl.BlockSpec((tm, 512), lambda i: (i, 0)),
                   pl.BlockSpec((1, B_DV), lambda i: (0, 0))],
        out_shape=[jax.ShapeDtypeStruct((t_len, 512), F32), jax.ShapeDtypeStruct((t_len, 512), F32),
                   jax.ShapeDtypeStruct((1, B_DV), F32)],
        compiler_params=_cp("arbitrary"),
    )(dcat, o_f, o_r, zc, gla_g)


def _mix_prep(dq, dkv, dqk_f, dqk_r, dv_f, dv_r, d_r, dla_f, dla_r, zc, wg2, bias2, cs, t_len, name):
    rows = zc.shape[0]
    tm = ROW_TILE
    n_x = t_len // tm
    gb = ZC_G // 128

    def xrow(i):
        return jnp.minimum(i, n_x - 1)

    def body(dq_ref, dkv_ref, dqkf_ref, dqkr_ref, dvf_ref, dvr_ref, dr_ref, dlaf_ref, dlar_ref, zg_ref, wg_ref,
             b_ref, cs_ref, dz_ref, dwg_ref, db_ref):
        i = pl.program_id(0)
        is_x = i < n_x
        cos = cs_ref[:, 0:128]
        sin = cs_ref[:, 128:256]
        cosq = jnp.concatenate([cos] * 4, axis=1)
        sinq = jnp.concatenate([sin] * 4, axis=1)
        dqv = jnp.where(is_x, dq_ref[...], 0.0)
        dz_ref[:, ZC_Q:ZC_QK] = (dqv * cosq + _swap16(dqv * sinq)).astype(BF16)
        dz_ref[:, ZC_QK:ZC_V] = (dqkf_ref[...] + dqkr_ref[...]).astype(BF16)
        dz_ref[:, ZC_V:ZC_R] = (dvf_ref[...] + dvr_ref[...]).astype(BF16)
        dz_ref[:, ZC_R:ZC_KV] = jnp.where(is_x, dr_ref[...], 0.0).astype(BF16)
        dk = dkv_ref[:, 0:128]
        dz_ref[:, ZC_KV:ZC_KV + 128] = (dk * cos + _swap16(dk * sin)).astype(BF16)
        dz_ref[:, ZC_KV + 128:ZC_G] = dkv_ref[:, 128:256].astype(BF16)
        zgb = zg_ref[...].astype(BF16)
        wg = wg_ref[...]
        pre = _dot(zgb, wg) + b_ref[...]
        dla = jnp.concatenate([dlaf_ref[...], dlar_ref[...]], axis=1)
        dpre = dla * (_sigmoid(-pre) / B_GATE_NORM)
        dpb = dpre.astype(BF16)
        dz_ref[:, ZC_G:ZC_W] = _dot_nt(dpb, wg).astype(BF16)

        @pl.when(i == 0)
        def _():
            dwg_ref[...] = jnp.zeros_like(dwg_ref)
            db_ref[...] = jnp.zeros_like(db_ref)

        dwg_ref[...] += _dot_tn(zgb, dpb)
        db_ref[...] += jnp.sum(dpre, axis=0, keepdims=True)

    return pl.pallas_call(
        body, name=name, grid=(rows // tm,),
        in_specs=[pl.BlockSpec((tm, 512), lambda i: (xrow(i), 0)),
                  pl.BlockSpec((tm, 256), lambda i: (i, 0)),
                  pl.BlockSpec((tm, 512), lambda i: (i, 0)),
                  pl.BlockSpec((tm, 512), lambda i: (i, 0)),
                  pl.BlockSpec((tm, 512), lambda i: (i, 0)),
                  pl.BlockSpec((tm, 512), lambda i: (i, 0)),
                  pl.BlockSpec((tm, 512), lambda i: (xrow(i), 0)),
                  pl.BlockSpec((tm, 256), lambda i: (i, 0)),
                  pl.BlockSpec((tm, 256), lambda i: (i, 0)),
                  pl.BlockSpec((tm, 128), lambda i: (i, gb)),
                  pl.BlockSpec((128, 512), lambda i: (0, 0)),
                  pl.BlockSpec((1, 512), lambda i: (0, 0)),
                  pl.BlockSpec((tm, 256), lambda i: (i, 0))],
        out_specs=[pl.BlockSpec((tm, ZC_W), lambda i: (i, 0)),
                   pl.BlockSpec((128, 512), lambda i: (0, 0)),
                   pl.BlockSpec((1, 512), lambda i: (0, 0))],
        out_shape=[jax.ShapeDtypeStruct((rows, ZC_W), BF16),
                   jax.ShapeDtypeStruct((128, 512), F32),
                   jax.ShapeDtypeStruct((1, 512), F32)],
        compiler_params=_cp("arbitrary"),
    )(dq, dkv, dqk_f, dqk_r, dv_f, dv_r, d_r, dla_f, dla_r, zc, wg2, bias2, cs)


def _gate_weights(w_a2_f, b_a_f, w_a2_b, b_a_b):
    wg2 = jnp.zeros((128, 512), F32)
    wg2 = wg2.at[0:B_GATE_RANK, 0:256].set(w_a2_f).at[B_GATE_RANK:2 * B_GATE_RANK, 256:512].set(w_a2_b)
    bias2 = jnp.concatenate([b_a_f, b_a_b]).reshape(1, 512)
    return wg2.astype(BF16), bias2


_WIN_PERM = ((0, 512), (768, 1280), (1280, 1792), (1792, 2304), (512, 768), (2304, 2336))


def _w_in_to_cat(w_in_full):
    parts = [w_in_full[:, a:b] for a, b in _WIN_PERM]
    parts.append(jnp.zeros((w_in_full.shape[0], ZC_W - PROJ_DIM), w_in_full.dtype))
    return jnp.concatenate(parts, axis=1)


def _cat_to_w_in(d_wcat):
    return jnp.concatenate([d_wcat[:, ZC_Q:ZC_QK], d_wcat[:, ZC_KV:ZC_G], d_wcat[:, ZC_QK:ZC_KV],
                            d_wcat[:, ZC_G:ZC_G + 2 * B_GATE_RANK]], axis=1)


def _mixer_ab_forward(x1, g3, mods, wcat, wg2, bias2, sink, gla_g, w_out, cs, t_len, l_ctx, n_x, pace):
    h = _rms_mod_fwd(x1, g3, mods, 1, n_x, BF16, "mix0_mod")
    zc, la = _proj_fwd(h, wcat, wg2, bias2, cs, "mix0_proj")
    dep = pace("proj", zc)
    o_a = _attn_fwd(zc, sink + dep[0, 0], t_len, l_ctx, "mix0_attn")
    dep = pace("attn", o_a)
    o_f, o_r, spf, spr = _gla_fwd(zc, la, dep, t_len, l_ctx, "mix0_gla")
    dep = pace("gla", o_f)
    cat = _gla_out_fwd(o_a, o_f, o_r, zc, gla_g + dep[0:1, 0:1], t_len, "mix0_glaout")
    x2, y = _matmul_resid(cat, w_out, x1, mods, 5, 1.0, n_x, t_len, "mix0_out")
    return x2, (x1, h, zc, la, o_a, o_f, o_r, spf, spr, cat, y)


def _mixer_ab_backward(dx2, saved, g3, mods, wcat, wg2, bias2, sink, gla_g, w_out, cs, t_len, l_ctx, n_x):
    x1, h, zc, la, o_a, o_f, o_r, spf, spr, cat, y = saved
    rows = x1.shape[0]
    tm = ROW_TILE
    dy, dgate = _gate_dy(dx2, y, mods, 5, 1.0, n_x, t_len, "mix0_dy")
    dcat = _matmul_nt(dy, w_out, "mix0_dcat")
    tk = _token_tile(t_len)
    d_wout = _matmul_tn(
        cat, dy, pl.BlockSpec((tk, D_MODEL), lambda n, k: (k, 0)), pl.BlockSpec((tk, D_MODEL), lambda n, k: (k, 0)),
        (D_MODEL, D_MODEL), pl.BlockSpec((D_MODEL, D_MODEL), lambda n, k: (0, 0)), (1, t_len // tk), "mix0_dwout")
    dos, d_r, d_glag = _gla_out_bwd(dcat, o_f, o_r, zc, gla_g, t_len, "mix0_dglaout")
    dqk_f, dv_f, dla_f, dqk_r, dv_r, dla_r = _gla_bwd(zc, la, spf, spr, dos, t_len, l_ctx, "mix0_dgla")
    dq, dkv, dsink = _attn_bwd(zc, sink, o_a, dcat, t_len, l_ctx, "mix0_dattn")
    dzc, dwg2, dbias2 = _mix_prep(dq, dkv, dqk_f, dqk_r, dv_f, dv_r, d_r, dla_f, dla_r, zc, wg2, bias2, cs, t_len,
                                  "mix0_prep")
    tk = _token_tile(rows)
    d_wcat = _matmul_tn(
        h, dzc, pl.BlockSpec((tk, D_MODEL), lambda n, k: (k, 0)), pl.BlockSpec((tk, ZC_W), lambda n, k: (k, 0)),
        (D_MODEL, ZC_W), pl.BlockSpec((D_MODEL, ZC_W), lambda n, k: (0, 0)), (1, rows // tk), "mix0_dwin")
    pairs = [(dzc, pl.BlockSpec((tm, ZC_W), lambda i: (i, 0)), wcat, pl.BlockSpec((D_MODEL, ZC_W), lambda i: (0, 0)))]
    dx1, stats = _bwd_dx(pairs, x1, dx2, t_len // tm, g3, mods, 1, n_x, "mix0_dx")
    return dx1, stats, dgate, d_wcat, dwg2, dbias2, dsink, d_glag, d_wout


_PT = 256
_PH = 16


def _pool_window(n, t_len, w, transpose):
    shape = (_PT, _PT + 2 * _PH)
    a = n * _PT + lax.broadcasted_iota(jnp.int32, shape, 0)
    b = n * _PT - _PH + lax.broadcasted_iota(jnp.int32, shape, 1)
    t, s = (b, a) if transpose else (a, b)
    lo = jnp.maximum(t - w // 2, 0)
    hi = jnp.minimum(t + (w - w // 2), t_len)
    inside = (s >= lo) & (s < hi) & (t >= 0) & (t < t_len)
    return jnp.where(inside, 1.0, 0.0).astype(BF16)


def _pool_inv_count(first, count, t_len, w):
    t = first + lax.broadcasted_iota(jnp.int32, (count, 1), 0)
    lo = jnp.maximum(t - w // 2, 0)
    hi = jnp.minimum(t + (w - w // 2), t_len)
    return jnp.where((t >= 0) & (t < t_len), 1.0 / jnp.maximum(hi - lo, 1).astype(F32), 0.0)


def _window_sum(win, vals):
    hi, lo = _split_bf16(vals)
    return _dot(win, hi) + _dot(win, lo)


def _pool_halo(p_ref, c_ref, n_ref):
    return jnp.concatenate([p_ref[_PT - _PH:_PT, :], c_ref[...], n_ref[0:_PH, :]], axis=0)


def _pool_specs(t_len):
    nb = t_len // _PT
    return [pl.BlockSpec((_PT, D_MODEL), lambda n: (jnp.maximum(n - 1, 0), 0)),
            pl.BlockSpec((_PT, D_MODEL), lambda n: (n, 0)),
            pl.BlockSpec((_PT, D_MODEL), lambda n: (jnp.minimum(n + 1, nb - 1), 0))], nb


def _pool_fwd(h, wp, pscale, x1, mods, t_len, name):
    halo_specs, nb = _pool_specs(t_len)

    def body(hp_ref, hc_ref, hn_ref, w_ref, ps_ref, x_ref, m_ref, x2_ref, pooled_ref, ypre_ref):
        n = pl.program_id(0)
        hcat = _pool_halo(hp_ref, hc_ref, hn_ref)
        ys = []
        for gi, w in enumerate(POOL_WINDOWS):
            cols = slice(gi * POOL_GROUP, (gi + 1) * POOL_GROUP)
            hg = hcat[:, cols]
            mean = _window_sum(_pool_window(n, t_len, w, False), hg) * _pool_inv_count(n * _PT, _PT, t_len, w)
            pooled = (mean - hg[_PH:_PH + _PT]).astype(BF16)
            pooled_ref[:, cols] = pooled
            ys.append(_dot(pooled, w_ref[gi]))
        ypre = jnp.concatenate(ys, axis=1)
        ypre_ref[...] = ypre
        x2_ref[...] = x_ref[...] + m_ref[0, 5:6, :] * (ypre * ps_ref[...])

    return pl.pallas_call(
        body, name=name, grid=(nb,),
        in_specs=halo_specs + [pl.BlockSpec((4, POOL_GROUP, POOL_GROUP), lambda n: (0, 0, 0)),
                               pl.BlockSpec((1, D_MODEL), lambda n: (0, 0)),
                               pl.BlockSpec((_PT, D_MODEL), lambda n: (n, 0)),
                               pl.BlockSpec((1, N_MOD, D_MODEL), lambda n: (0, 0, 0))],
        out_specs=[pl.BlockSpec((_PT, D_MODEL), lambda n: (n, 0))] * 3,
        out_shape=[jax.ShapeDtypeStruct((t_len, D_MODEL), F32), jax.ShapeDtypeStruct((t_len, D_MODEL), BF16),
                   jax.ShapeDtypeStruct((t_len, D_MODEL), F32)],
        compiler_params=_cp("parallel"),
    )(h, h, h, wp, pscale, x1, mods)


def _pool_bwd_a(dx2, ypre, wp, pscale, mods, t_len, name):
    nb = t_len // _PT

    def body(d_ref, y_ref, w_ref, ps_ref, m_ref, dyp_ref, dpl_ref, dgate_ref, dps_ref):
        n = pl.program_id(0)
        dv = d_ref[...]
        ypre = y_ref[...]
        ps = ps_ref[...]
        dy = dv * m_ref[0, 5:6, :]
        dyp = (dy * ps).astype(BF16)
        dyp_ref[...] = dyp
        for gi in range(len(POOL_WINDOWS)):
            cols = slice(gi * POOL_GROUP, (gi + 1) * POOL_GROUP)
            dpl_ref[:, cols] = _dot_nt(dyp[:, cols], w_ref[gi])

        @pl.when(n == 0)
        def _():
            dgate_ref[...] = jnp.zeros_like(dgate_ref)
            dps_ref[...] = jnp.zeros_like(dps_ref)

        dgate_ref[...] += jnp.sum(dv * (ypre * ps), axis=0, keepdims=True)
        dps_ref[...] += jnp.sum(dy * ypre, axis=0, keepdims=True)

    return pl.pallas_call(
        body, name=name, grid=(nb,),
        in_specs=[pl.BlockSpec((_PT, D_MODEL), lambda n: (n, 0)),
                  pl.BlockSpec((_PT, D_MODEL), lambda n: (n, 0)),
                  pl.BlockSpec((4, POOL_GROUP, POOL_GROUP), lambda n: (0, 0, 0)),
                  pl.BlockSpec((1, D_MODEL), lambda n: (0, 0)),
                  pl.BlockSpec((1, N_MOD, D_MODEL), lambda n: (0, 0, 0))],
        out_specs=[pl.BlockSpec((_PT, D_MODEL), lambda n: (n, 0)),
                   pl.BlockSpec((_PT, D_MODEL), lambda n: (n, 0)),
                   pl.BlockSpec((1, D_MODEL), lambda n: (0, 0)),
                   pl.BlockSpec((1, D_MODEL), lambda n: (0, 0))],
        out_shape=[jax.ShapeDtypeStruct((t_len, D_MODEL), BF16), jax.ShapeDtypeStruct((t_len, D_MODEL), F32),
                   jax.ShapeDtypeStruct((1, D_MODEL), F32), jax.ShapeDtypeStruct((1, D_MODEL), F32)],
        compiler_params=_cp("arbitrary"),
    )(dx2, ypre, wp, pscale, mods)


def _pool_bwd_dx(dpl, x1, dx2, g3, mods, t_len, name):
    halo_specs, nb = _pool_specs(t_len)

    def body(dp_ref, dc_ref, dn_ref, x_ref, d_ref, g_ref, m_ref, dx_ref, acc_ref):
        n = pl.program_id(0)
        dcat = _pool_halo(dp_ref, dc_ref, dn_ref)
        dhs = []
        for gi, w in enumerate(POOL_WINDOWS):
            cols = slice(gi * POOL_GROUP, (gi + 1) * POOL_GROUP)
            dg = dcat[:, cols]
            scaled = dg * _pool_inv_count(n * _PT - _PH, _PT + 2 * _PH, t_len, w)
            dhs.append(_window_sum(_pool_window(n, t_len, w, True), scaled) - dg[_PH:_PH + _PT])
        dh = jnp.concatenate(dhs, axis=1)
        g = g_ref[1:2, :]
        scale = m_ref[0, 4:5, :]
        dx = _rms_mod_bwd_tail(dh, x_ref[...], g, scale, 0, acc_ref, n == 0)
        dx_ref[...] = d_ref[...] + dx

    return pl.pallas_call(
        body, name=name, grid=(nb,),
        in_specs=halo_specs + [pl.BlockSpec((_PT, D_MODEL), lambda n: (n, 0)),
                               pl.BlockSpec((_PT, D_MODEL), lambda n: (n, 0)),
                               pl.BlockSpec((3, D_MODEL), lambda n: (0, 0)),
                               pl.BlockSpec((1, N_MOD, D_MODEL), lambda n: (0, 0, 0))],
        out_specs=[pl.BlockSpec((_PT, D_MODEL), lambda n: (n, 0)),
                   pl.BlockSpec((2, 3, D_MODEL), lambda n: (0, 0, 0))],
        out_shape=[jax.ShapeDtypeStruct((t_len, D_MODEL), F32), jax.ShapeDtypeStruct((2, 3, D_MODEL), F32)],
        compiler_params=_cp("arbitrary"),
    )(dpl, dpl, dpl, x1, dx2, g3, mods)


def _mixer_pool_forward(x1, g3, mods, wp, pscale, t_len):
    h = _rms_mod_fwd(x1, g3, mods, 1, t_len // ROW_TILE, F32, "mix1_mod")
    x2, pooled, ypre = _pool_fwd(h, wp, pscale, x1, mods, t_len, "mix1_pool")
    return x2, (x1, pooled, ypre)


def _mixer_pool_backward(dx2, saved, g3, mods, wp, pscale, t_len):
    x1, pooled, ypre = saved
    tm = ROW_TILE
    dyp, dpl, dgate, dps = _pool_bwd_a(dx2, ypre, wp, pscale, mods, t_len, "mix1_da")
    d_wp = _matmul_tn(
        pooled, dyp, pl.BlockSpec((tm, POOL_GROUP), lambda g, k: (k, g)),
        pl.BlockSpec((tm, POOL_GROUP), lambda g, k: (k, g)),
        (4, POOL_GROUP, POOL_GROUP), pl.BlockSpec((1, POOL_GROUP, POOL_GROUP), lambda g, k: (g, 0, 0)),
        (4, t_len // tm), "mix1_dwp")
    dx1, stats = _pool_bwd_dx(dpl, x1, dx2, g3, mods, t_len, "mix1_dx")
    return dx1, stats, dgate, dps, d_wp


def _final_loss(x3, final_g, target, name):
    t_len = x3.shape[0]
    tm = ROW_TILE

    def body(x_ref, g_ref, t_ref, dx_ref, loss_ref, dg_ref):
        i = pl.program_id(0)
        xv = x_ref[...]
        g = g_ref[...]
        r = lax.rsqrt(jnp.mean(xv * xv, axis=-1, keepdims=True) + RMS_EPS)
        xhat = xv * r
        err = xhat * g - t_ref[...]
        part = 0.5 * jnp.sum(jnp.mean(err * err, axis=-1, keepdims=True), axis=0, keepdims=True)
        dy = err * (1.0 / D_MODEL)

        @pl.when(i == 0)
        def _():
            loss_ref[...] = jnp.zeros_like(loss_ref)
            dg_ref[...] = jnp.zeros_like(dg_ref)

        loss_ref[...] += jnp.broadcast_to(part, (1, 128))
        dg_ref[...] += jnp.sum(dy * xhat, axis=0, keepdims=True)
        dxh = dy * g
        dx_ref[...] = r * (dxh - xhat * jnp.mean(dxh * xhat, axis=-1, keepdims=True))

    return pl.pallas_call(
        body, name=name, grid=(t_len // tm,),
        in_specs=[pl.BlockSpec((tm, D_MODEL), lambda i: (i, 0)),
                  pl.BlockSpec((1, D_MODEL), lambda i: (0, 0)),
                  pl.BlockSpec((tm, D_MODEL), lambda i: (i, 0))],
        out_specs=[pl.BlockSpec((tm, D_MODEL), lambda i: (i, 0)),
                   pl.BlockSpec((1, 128), lambda i: (0, 0)),
                   pl.BlockSpec((1, D_MODEL), lambda i: (0, 0))],
        out_shape=[jax.ShapeDtypeStruct((t_len, D_MODEL), F32), jax.ShapeDtypeStruct((1, 128), F32),
                   jax.ShapeDtypeStruct((1, D_MODEL), F32)],
        compiler_params=_cp("arbitrary"),
    )(x3, final_g, target)


_CROWS = 16


def _adaln_fwd(c16, w_mod, bias_k, name):
    n_l, _, cols = w_mod.shape

    def body(c_ref, w_ref, b_ref, o_ref):
        cv = c_ref[...]
        sc = (cv * _sigmoid(cv)).astype(BF16)
        o_ref[0] = _dot(sc, w_ref[0].astype(BF16)) + b_ref[0]

    return pl.pallas_call(
        body, name=name, grid=(n_l,),
        in_specs=[pl.BlockSpec((_CROWS, D_MODEL), lambda l: (0, 0)),
                  pl.BlockSpec((1, D_MODEL, cols), lambda l: (l, 0, 0)),
                  pl.BlockSpec((1, 1, cols), lambda l: (l, 0, 0))],
        out_specs=pl.BlockSpec((1, _CROWS, cols), lambda l: (l, 0, 0)),
        out_shape=jax.ShapeDtypeStruct((n_l, _CROWS, cols), F32),
        compiler_params=_cp("parallel"),
    )(c16, w_mod, bias_k)


def _adaln_bwd(c16, d16, w_mod, dmmc_k, name):
    n_l, _, cols = w_mod.shape

    def body(c_ref, d_ref, w_ref, dm_ref, gw_ref, cp_ref):
        layer = pl.program_id(0)
        cv = c_ref[...]
        gw_ref[0] = _dot_tn_hi(cv * _sigmoid(cv), d_ref[0])

        @pl.when(layer == 0)
        def _():
            cp_ref[...] = jnp.sum(w_ref[0] * dm_ref[...], axis=1, keepdims=True)

    return pl.pallas_call(
        body, name=name, grid=(n_l,),
        in_specs=[pl.BlockSpec((_CROWS, D_MODEL), lambda l: (0, 0)),
                  pl.BlockSpec((1, _CROWS, cols), lambda l: (l, 0, 0)),
                  pl.BlockSpec((1, D_MODEL, cols), lambda l: (0, 0, 0)),
                  pl.BlockSpec((1, cols), lambda l: (0, 0))],
        out_specs=[pl.BlockSpec((1, D_MODEL, cols), lambda l: (l, 0, 0)),
                   pl.BlockSpec((D_MODEL, 1), lambda l: (0, 0))],
        out_shape=[jax.ShapeDtypeStruct((n_l, D_MODEL, cols), F32), jax.ShapeDtypeStruct((D_MODEL, 1), F32)],
        compiler_params=_cp("arbitrary"),
    )(c16, d16, w_mod, dmmc_k)


def _cctx_grad(cparts, c_ctx2, name):
    def body(p_ref, c_ref, o_ref):
        tot = ((p_ref[0] + p_ref[2]) + p_ref[4]) + p_ref[6]
        cv = c_ref[...]
        sg = _sigmoid(cv)
        o_ref[...] = tot * (sg * (1.0 + cv * (1.0 - sg)))

    return pl.pallas_call(
        body, name=name, out_shape=jax.ShapeDtypeStruct((8, 128), F32),
        in_specs=[pl.BlockSpec(memory_space=pltpu.VMEM), pl.BlockSpec(memory_space=pltpu.VMEM)],
        out_specs=pl.BlockSpec(memory_space=pltpu.VMEM),
    )(cparts, c_ctx2)


def _sum_devices(ga, name):
    def body(g_ref, o_ref):
        acc = g_ref[0]
        for d in range(1, N_DEV):
            acc = acc + g_ref[d]
        o_ref[...] = acc

    return pl.pallas_call(
        body, name=name, out_shape=jax.ShapeDtypeStruct(ga.shape[1:], F32),
        in_specs=[pl.BlockSpec(memory_space=pltpu.VMEM)], out_specs=pl.BlockSpec(memory_space=pltpu.VMEM),
    )(ga)


def _place():
    return lax.axis_index("x"), lax.axis_index("y"), lax.axis_index("c")


def _flip(a, d):
    return 1 - a if d else a


_CHIP_FLIPS = ((1, 0), (0, 1), (1, 1))


def _allgather_small(v, name, after=()):
    r, cc = v.shape

    def body(v_ref, *rest):
        out_ref, send_sems, recv_sems, local_sem = rest[-4:]
        x, y, c = _place()
        me = 4 * x + 2 * y + c
        mine = pltpu.make_async_copy(v_ref, out_ref.at[me], local_sem)
        mine.start()
        sends = []
        for k in range(1, N_DEV):
            peer = (_flip(x, (k >> 2) & 1), _flip(y, (k >> 1) & 1), _flip(c, k & 1))
            cp = pltpu.make_async_remote_copy(src_ref=v_ref, dst_ref=out_ref.at[me], send_sem=send_sems.at[k - 1],
                                              recv_sem=recv_sems.at[k - 1], device_id=peer, device_id_type=MESH)
            cp.start()
            sends.append(cp)
        for k in range(1, N_DEV):
            px, py, pc = _flip(x, (k >> 2) & 1), _flip(y, (k >> 1) & 1), _flip(c, k & 1)
            pltpu.make_async_remote_copy(src_ref=v_ref, dst_ref=out_ref.at[4 * px + 2 * py + pc],
                                         send_sem=send_sems.at[k - 1], recv_sem=recv_sems.at[k - 1],
                                         device_id=(px, py, pc), device_id_type=MESH).wait_recv()
        for cp in sends:
            cp.wait_send()
        mine.wait()

    return pl.pallas_call(
        body, name=name, out_shape=jax.ShapeDtypeStruct((N_DEV, r, cc), F32),
        in_specs=[pl.BlockSpec(memory_space=pltpu.VMEM)] + [pl.BlockSpec(memory_space=pl.ANY)] * len(after),
        out_specs=pl.BlockSpec(memory_space=pltpu.VMEM),
        scratch_shapes=[pltpu.SemaphoreType.DMA((N_DEV - 1,)), pltpu.SemaphoreType.DMA((N_DEV - 1,)),
                        pltpu.SemaphoreType.DMA],
        compiler_params=pltpu.CompilerParams(vmem_limit_bytes=VMEM_LIMIT_BYTES),
    )(v, *after)


_HBM_SPEC = pl.BlockSpec(memory_space=pltpu.HBM)
_SEM_SPEC = pl.BlockSpec(memory_space=pltpu.SEMAPHORE)
_EFFECT = pltpu.SideEffectType.DATAFLOW_SIDE_EFFECTING


def _in_hbm(a):
    return pltpu.with_memory_space_constraint(a, pltpu.HBM)


def _gather_start(arrs, groups, after, name):
    n, n_g = len(arrs), len(groups)

    def body(*refs):
        ins, zones = refs[:n], refs[n:2 * n]
        sems = refs[2 * n + 1:2 * n + 1 + 2 * n_g]
        token = refs[2 * n + 1 + 2 * n_g + 2 * n]
        x, y, c = _place()
        k_me = 2 * x + y
        for g, members in enumerate(groups):
            for t, a in enumerate(members):
                for j, (dx, dy) in enumerate(_CHIP_FLIPS):
                    pltpu.make_async_remote_copy(
                        src_ref=ins[a], dst_ref=zones[a].at[k_me], send_sem=sems[2 * g].at[3 * t + j],
                        recv_sem=sems[2 * g + 1].at[3 * t + j], device_id=(_flip(x, dx), _flip(y, dy), c),
                        device_id_type=MESH).start()
        token[...] = jnp.zeros_like(token)

    k_own = 2 * lax.axis_index("x") + lax.axis_index("y")
    zones = [lax.dynamic_update_slice(lax.empty((N_CHIPS,) + a.shape, a.dtype), a[None], (k_own,) + (0,) * a.ndim)
             for a in arrs]
    sem_shapes = []
    for members in groups:
        sem_shapes += [pltpu.SemaphoreType.DMA((3 * len(members),))] * 2
    outs = pl.pallas_call(
        body, name=name,
        out_shape=sem_shapes + [pltpu.HBM(a.shape, a.dtype) for a in arrs]
        + [pltpu.HBM(z.shape, z.dtype) for z in zones] + [jax.ShapeDtypeStruct((8, 128), F32)],
        in_specs=[_HBM_SPEC] * (2 * n) + [pl.BlockSpec(memory_space=pl.ANY)],
        out_specs=[_SEM_SPEC] * (2 * n_g) + [_HBM_SPEC] * (2 * n) + [pl.BlockSpec(memory_space=pltpu.VMEM)],
        input_output_aliases={i: 2 * n_g + i for i in range(2 * n)},
        compiler_params=pltpu.CompilerParams(has_side_effects=_EFFECT),
    )(*[_in_hbm(a) for a in arrs], *[_in_hbm(z) for z in zones], after)
    sems = outs[:2 * n_g]
    thru = outs[2 * n_g:2 * n_g + n]
    zones = outs[2 * n_g + n:2 * n_g + 2 * n]
    return [(sems[2 * g], sems[2 * g + 1]) for g in range(n_g)], thru, zones, outs[-1]


def _gather_wait(shards, zones, send_sems, recv_sems, after, name):
    m = len(shards)

    def body(*refs):
        ins, zs = refs[:m], refs[m:2 * m]
        ssem, rsem = refs[2 * m], refs[2 * m + 1]
        x, y, c = _place()
        for t in range(m):
            for j, (dx, dy) in enumerate(_CHIP_FLIPS):
                px, py = _flip(x, dx), _flip(y, dy)
                cp = pltpu.make_async_remote_copy(
                    src_ref=ins[t], dst_ref=zs[t].at[2 * px + py], send_sem=ssem.at[3 * t + j],
                    recv_sem=rsem.at[3 * t + j], device_id=(px, py, c), device_id_type=MESH)
                cp.wait_send()
                cp.wait_recv()

    after = list(after) if isinstance(after, (list, tuple)) else [after]
    outs = pl.pallas_call(
        body, name=name,
        out_shape=[pltpu.HBM(a.shape, a.dtype) for a in list(shards) + list(zones)],
        in_specs=[_HBM_SPEC] * (2 * m) + [_SEM_SPEC, _SEM_SPEC] + [pl.BlockSpec(memory_space=pl.ANY)] * len(after),
        out_specs=[_HBM_SPEC] * (2 * m),
        input_output_aliases={i: i for i in range(2 * m)},
        compiler_params=pltpu.CompilerParams(has_side_effects=_EFFECT),
    )(*shards, *zones, send_sems, recv_sems, *after)
    return outs[m:]


def _scatter_start(arrs, name):
    n = len(arrs)

    def body(*refs):
        ins, lands = refs[:n], refs[n:2 * n]
        ssem, rsem = refs[2 * n], refs[2 * n + 1]
        token = refs[2 * n + 2 + 2 * n]
        x, y, c = _place()
        for a in range(n):
            for j, (dx, dy) in enumerate(_CHIP_FLIPS):
                px, py = _flip(x, dx), _flip(y, dy)
                pltpu.make_async_remote_copy(
                    src_ref=ins[a].at[2 * px + py], dst_ref=lands[a].at[j], send_sem=ssem.at[3 * a + j],
                    recv_sem=rsem.at[3 * a + j], device_id=(px, py, c), device_id_type=MESH).start()
        token[...] = jnp.zeros_like(token)

    lands = [lax.empty((3,) + a.shape[1:], a.dtype) for a in arrs]
    outs = pl.pallas_call(
        body, name=name,
        out_shape=[pltpu.SemaphoreType.DMA((3 * n,))] * 2 + [pltpu.HBM(a.shape, a.dtype) for a in arrs]
        + [pltpu.HBM(z.shape, z.dtype) for z in lands] + [jax.ShapeDtypeStruct((8, 128), F32)],
        in_specs=[_HBM_SPEC] * (2 * n),
        out_specs=[_SEM_SPEC] * 2 + [_HBM_SPEC] * (2 * n) + [pl.BlockSpec(memory_space=pltpu.VMEM)],
        input_output_aliases={i: 2 + i for i in range(2 * n)},
        compiler_params=pltpu.CompilerParams(has_side_effects=_EFFECT),
    )(*[_in_hbm(a) for a in arrs], *[_in_hbm(z) for z in lands])
    return outs[0], outs[1], outs[2:2 + n], outs[2 + n:2 + 2 * n], outs[-1]


def _scatter_wait(arrs, lands, send_sems, recv_sems, after, name):
    n = len(arrs)

    def body(*refs):
        ins, lz = refs[:n], refs[n:2 * n]
        ssem, rsem = refs[2 * n], refs[2 * n + 1]
        x, y, c = _place()
        for a in range(n):
            for j, (dx, dy) in enumerate(_CHIP_FLIPS):
                px, py = _flip(x, dx), _flip(y, dy)
                cp = pltpu.make_async_remote_copy(
                    src_ref=ins[a].at[2 * px + py], dst_ref=lz[a].at[j], send_sem=ssem.at[3 * a + j],
                    recv_sem=rsem.at[3 * a + j], device_id=(px, py, c), device_id_type=MESH)
                cp.wait_send()
                cp.wait_recv()

    outs = pl.pallas_call(
        body, name=name,
        out_shape=[pltpu.HBM(a.shape, a.dtype) for a in list(arrs) + list(lands)],
        in_specs=[_HBM_SPEC] * (2 * n) + [_SEM_SPEC, _SEM_SPEC, pl.BlockSpec(memory_space=pl.ANY)],
        out_specs=[_HBM_SPEC] * (2 * n),
        input_output_aliases={i: i for i in range(2 * n)},
        compiler_params=pltpu.CompilerParams(has_side_effects=_EFFECT),
    )(*arrs, *lands, send_sems, recv_sems, after)
    return outs[:n], outs[n:]


def _swap_sibling(arrs, name):
    n = len(arrs)

    def body(*refs):
        ins, outs = refs[:n], refs[n:2 * n]
        send_sems, recv_sems = refs[2 * n:]
        x, y, c = _place()
        sends = []
        for a in range(n):
            cp = pltpu.make_async_remote_copy(src_ref=ins[a], dst_ref=outs[a], send_sem=send_sems.at[a],
                                              recv_sem=recv_sems.at[a], device_id=(x, y, 1 - c), device_id_type=MESH)
            cp.start()
            sends.append(cp)
        for cp in sends:
            cp.wait()

    any_spec = pl.BlockSpec(memory_space=pl.ANY)
    return pl.pallas_call(
        body, name=name,
        out_shape=[jax.ShapeDtypeStruct(a.shape, a.dtype) for a in arrs],
        in_specs=[any_spec] * n, out_specs=[any_spec] * n,
        scratch_shapes=[pltpu.SemaphoreType.DMA((n,)), pltpu.SemaphoreType.DMA((n,))],
    )(*arrs)


def _row_tile(rows, cols):
    for tr in (1024, 512, 256, 128, 64, 32, 16, 8):
        if rows % tr == 0 and tr * cols * 4 <= (1 << 20):
            return tr
    return rows


def _partial_sum(g_full, recv, k_idx, name):
    _, r, c = g_full.shape
    tr = _row_tile(r, c)

    def body(k_ref, g_ref, r_ref, o_ref):
        del k_ref
        acc = g_ref[0].astype(F32)
        for j in range(3):
            acc = acc + r_ref[j].astype(F32)
        o_ref[...] = acc

    return pl.pallas_call(
        body, name=name,
        grid_spec=pltpu.PrefetchScalarGridSpec(
            num_scalar_prefetch=1, grid=(r // tr,),
            in_specs=[pl.BlockSpec((1, tr, c), lambda i, k: (k[0], i, 0)),
                      pl.BlockSpec((3, tr, c), lambda i, k: (0, i, 0))],
            out_specs=pl.BlockSpec((tr, c), lambda i, k: (i, 0))),
        out_shape=jax.ShapeDtypeStruct((r, c), F32),
        compiler_params=_cp("parallel"),
    )(k_idx, g_full, recv)


def _adamw(w3, parts, m3, v3, layer, prev, name):
    n_l, r, c = w3.shape
    tr = _row_tile(r, c)
    n_i = r // tr
    n_p = len(parts)
    c1 = 1.0 - ADAM_B1 ** ADAM_STEP
    c2 = 1.0 - ADAM_B2 ** ADAM_STEP
    stacked = [isinstance(p, tuple) for p in parts]

    def body(*refs):
        w_ref, m_ref, v_ref = refs[0:3]
        g_refs = refs[3:3 + n_p]
        go_ref, d_ref, mo_ref, vo_ref = refs[-4:]
        g = None
        for p in range(n_p):
            term = g_refs[p][0] if stacked[p] else g_refs[p][...]
            g = term if g is None else g + term
        w = w_ref[0]
        m = ADAM_B1 * m_ref[0] + (1.0 - ADAM_B1) * g
        v = ADAM_B2 * v_ref[0] + (1.0 - ADAM_B2) * (g * g)
        m_hat = m / c1
        v_hat = v / c2
        go_ref[0] = g
        d_ref[0] = -ADAM_LR * (m_hat / (jnp.sqrt(v_hat) + ADAM_EPS) + ADAM_WD * w)
        mo_ref[0] = m
        vo_ref[0] = v

    blk = pl.BlockSpec((1, tr, c), lambda i: (layer, i, 0))
    in_specs = [blk, blk, blk]
    args = [w3, m3, v3]
    for part in parts:
        if isinstance(part, tuple):
            in_specs.append(pl.BlockSpec((1, tr, c), functools.partial(lambda idx, i: (idx, i, 0), part[1])))
            args.append(part[0])
        else:
            in_specs.append(pl.BlockSpec((tr, c), lambda i: (i, 0)))
            args.append(part)
    aliases = {}
    if prev is not None:
        in_specs += [pl.BlockSpec(memory_space=pl.ANY)] * 4
        aliases = {len(args) + q: q for q in range(4)}
        args += list(prev)
    shp = jax.ShapeDtypeStruct((n_l, r, c), F32)
    return pl.pallas_call(
        body, name=name, grid=(n_i,), in_specs=in_specs, out_specs=[blk] * 4, out_shape=[shp] * 4,
        input_output_aliases=aliases, compiler_params=_cp("parallel"),
    )(*args)


_SMALL_W = 4096
_PACK_ROWS = 352
_N9 = N_MOD * D_MODEL


def _flat_pad(parts, total):
    flat = jnp.concatenate([p.reshape(-1) for p in parts])
    return jnp.concatenate([flat, jnp.zeros((total - flat.shape[0],), F32)])


def kernel(x, c, ctx, c_ctx, w_mod, b_mod, norm_g, ffn1_wi, ffn1_wo, ffn2_wi, ffn2_wo, w_in, w_a2_f, b_a_f, w_a2_b, b_a_b, sink, gla_g, w_out, w_pool, pool_scale, final_g, loss_target, m_c_ctx, m_w_mod, m_b_mod, m_norm_g, m_ffn1_wi, m_ffn1_wo, m_ffn2_wi, m_ffn2_wo, m_w_in, m_w_a2_f, m_b_a_f, m_w_a2_b, m_b_a_b, m_sink, m_gla_g, m_w_out, m_w_pool, m_pool_scale, m_final_g, v_c_ctx, v_w_mod, v_b_mod, v_norm_g, v_ffn1_wi, v_ffn1_wo, v_ffn2_wi, v_ffn2_wo, v_w_in, v_w_a2_f, v_b_a_f, v_w_a2_b, v_b_a_b, v_sink, v_gla_g, v_w_out, v_w_pool, v_pool_scale, v_final_g):
    t_len, l_ctx = x.shape[1], ctx.shape[1]
    tm = ROW_TILE
    pad = (-(t_len + l_ctx)) % tm
    rows0 = t_len + l_ctx + pad
    n_x = t_len // tm
    xi, yi, ci = _place()
    k_me = 2 * xi + yi
    me = 4 * xi + 2 * yi + ci
    mod_cols = w_mod.shape[2]
    n_grp = len(POOL_WINDOWS)

    small_w = _flat_pad([norm_g, w_a2_f, w_a2_b, pool_scale], _SMALL_W).reshape(_SMALL_W // 128, 128)
    shards = [ffn1_wi[0], ffn1_wi[1], ffn1_wo[0], ffn1_wo[1], ffn2_wi[0], ffn2_wi[1], ffn2_wo[0], ffn2_wo[1],
              w_in[0], w_out[0], w_pool[0].reshape(n_grp * w_pool.shape[2], POOL_GROUP)]

    send_src = [s.astype(BF16) for s in shards] + [small_w]
    groups = ([11, 0], [2], [8, 9], [4], [6], [1], [3], [10, 5], [7])
    started = {}

    def gather_start(g, after):
        members = groups[g]
        sems, thru, zones, token = _gather_start([send_src[a] for a in members], (tuple(range(len(members))),),
                                                 after, "gather_start_%d" % g)
        started[g] = (sems[0], thru, zones)
        return token

    def gather_wait(g, after):
        (ssem, rsem), thru, zones = started[g]
        return dict(zip(groups[g], _gather_wait(thru, zones, ssem, rsem, after, "gather_wait_%d" % g)))

    c_all = _allgather_small(c.reshape(8, 128), "gather_cond").reshape(N_DEV, D_MODEL)
    tok = gather_start(0, c_all)
    c16 = jnp.concatenate([c_all, c_ctx[None], jnp.zeros((_CROWS - N_DEV - 1, D_MODEL), F32)], axis=0) + tok[0:1, 0:1]
    bias_k = lax.dynamic_slice(b_mod, (0, k_me * mod_cols), (2, mod_cols)).reshape(2, 1, mod_cols)
    mm_k = _adaln_fwd(c16, w_mod, bias_k, "adaln_fwd")
    cs = _rope_tables(t_len, rows0)
    xcat = jnp.concatenate([x[0], ctx[0], jnp.zeros((pad, D_MODEL), F32)], axis=0)
    mm_all = _allgather_small(mm_k.reshape(-1, 128), "gather_mod", (cs, xcat)).reshape(N_DEV, 2, _CROWS, mod_cols)
    mm_full = jnp.concatenate([mm_all[2 * k] for k in range(N_CHIPS)], axis=-1)
    mm_x = lax.dynamic_index_in_dim(mm_full, me, axis=1, keepdims=False)
    mm_c = mm_full[:, N_DEV]
    mods = [jnp.stack([mm_x[l].reshape(N_MOD, D_MODEL), mm_c[l].reshape(N_MOD, D_MODEL)]) for l in range(2)]
    gathered = gather_wait(0, mods[0])
    sw = gathered[11].reshape(N_CHIPS, _SMALL_W)
    ng_n = norm_g.size
    a2_n = w_a2_f.size
    norm_g_full = jnp.concatenate([sw[k, :ng_n].reshape(norm_g.shape) for k in range(N_CHIPS)], axis=-1)
    w_a2_f_full = jnp.concatenate([sw[k, ng_n:ng_n + a2_n].reshape(w_a2_f.shape[1:]) for k in range(N_CHIPS)], axis=-1)
    w_a2_b_full = jnp.concatenate(
        [sw[k, ng_n + a2_n:ng_n + 2 * a2_n].reshape(w_a2_b.shape[1:]) for k in range(N_CHIPS)], axis=-1)
    pscale_full = jnp.concatenate(
        [sw[k, ng_n + 2 * a2_n:ng_n + 2 * a2_n + pool_scale.size] for k in range(N_CHIPS)]).reshape(1, D_MODEL)
    wg2, bias2 = _gate_weights(w_a2_f_full, b_a_f[0], w_a2_b_full, b_a_b[0])
    gla_g2 = gla_g.reshape(1, B_DV)
    final_g2 = final_g.reshape(1, D_MODEL)

    g3 = [norm_g_full[0], norm_g_full[1]]

    w1i, w1o, w2i, w2o = [None, None], [None, None], [None, None], [None, None]
    w1i[0] = gathered[0]
    mods_a = mods[0] + gather_start(1, w1i[0])[0:1, 0:1] + gather_start(2, w1i[0])[0:1, 0:1]
    x1, sv_a1, w1o[0] = _ffn_forward(xcat, g3[0], mods_a, 0, w1i[0],
                                     lambda s: (gather_wait(1, s)[2], gather_start(3, s)), n_x, "l0_ffn1")
    gathered = gather_wait(2, x1)
    w_in_full = jnp.concatenate([gathered[8][k] for k in range(N_CHIPS)], axis=1)
    wcat = _w_in_to_cat(w_in_full)
    w_out_full = gathered[9].reshape(D_MODEL, D_MODEL)
    mods_a = mods[0] + gather_start(4, x1)[0:1, 0:1]
    pace_group = {"proj": 5, "attn": 6, "gla": 7}
    x2, sv_am = _mixer_ab_forward(x1, g3[0], mods_a, wcat, wg2, bias2, sink[0], gla_g2, w_out_full, cs,
                                  t_len, l_ctx, n_x, lambda tag, res_: gather_start(pace_group[tag], res_))
    mods_a = mods[0] + gather_start(8, x2)[0:1, 0:1]
    w2i[0], w2o[0] = gather_wait(3, x2)[4], gather_wait(4, x2)[6]
    x3, sv_a2, _ = _ffn_forward(x2, g3[0], mods_a, 2, w2i[0], lambda s: (w2o[0], None), n_x, "l0_ffn2")
    w1i[1], w1o[1] = gather_wait(5, x3)[1], gather_wait(6, x3)[3]
    x4, sv_b1, _ = _ffn_forward(x3, g3[1], mods[1], 0, w1i[1], lambda s: (w1o[1], None), n_x, "l1_ffn1")
    gathered = gather_wait(7, x4)
    w2i[1] = gathered[5]
    wp_full = gathered[10].reshape(N_CHIPS, n_grp, -1, POOL_GROUP).transpose(1, 0, 2, 3).reshape(
        n_grp, POOL_GROUP, POOL_GROUP)
    x5, sv_bm = _mixer_pool_forward(x4, g3[1], mods[1], wp_full, pscale_full, t_len)
    x6, sv_b2, w2o[1] = _ffn_forward(x5, g3[1], mods[1], 2, w2i[1], lambda s: (gather_wait(8, s)[7], None), n_x,
                                     "l1_ffn2")
    dx6, loss_part, d_final_g = _final_loss(x6, final_g2, loss_target[0], "final_loss")
    loss = lax.psum(loss_part[0, 0], ("x", "y", "c"))

    sent = []

    def sender(weight, layer):
        def send(grad, tag):
            nm = "%s_%s_%d" % (weight, tag, layer)
            ssem, rsem, thru, lands, token = _scatter_start([grad], "scatter_start_" + nm)
            sent.append((nm, weight + "_" + tag if tag else weight, layer, thru, lands, ssem, rsem))
            return token[0:1, 0:1]
        return send

    dx5, st_b2, dg_b2 = _ffn_backward(dx6, sv_b2, g3[1], mods[1], 2, w2i[1], w2o[1], n_x, sender("ffn2", 1),
                                      "l1_ffn2_b")
    dx4, st_bm, dg_bm, d_pscale, d_wp = _mixer_pool_backward(dx5, sv_bm, g3[1], mods[1], wp_full, pscale_full, t_len)
    d_wp4 = d_wp.reshape(n_grp, N_CHIPS, -1, POOL_GROUP).transpose(1, 0, 2, 3).reshape(N_CHIPS, -1, POOL_GROUP)
    mods1 = mods[1] + sender("w_pool", 0)(d_wp4, "")
    dx3, st_b1, dg_b1 = _ffn_backward(dx4, sv_b1, g3[1], mods1, 0, w1i[1], w1o[1], n_x, sender("ffn1", 1),
                                      "l1_ffn1_b")
    dx2, st_a2, dg_a2 = _ffn_backward(dx3, sv_a2, g3[0], mods[0], 2, w2i[0], w2o[0], n_x, sender("ffn2", 0),
                                      "l0_ffn2_b")
    dx1, st_am, dg_am, d_wcat, d_wg2, d_bias2, d_sink, d_glag, d_wout = _mixer_ab_backward(
        dx2, sv_am, g3[0], mods[0], wcat, wg2, bias2, sink[0], gla_g2, w_out_full, cs, t_len, l_ctx, n_x)
    d_w_in4 = _cat_to_w_in(d_wcat).reshape(D_MODEL, N_CHIPS, -1).transpose(1, 0, 2)
    mods0 = mods[0] + sender("w_in", 0)(d_w_in4, "") + sender("w_out", 0)(d_wout.reshape(N_CHIPS, -1, D_MODEL), "")
    dx0, st_a1, dg_a1 = _ffn_backward(dx1, sv_a1, g3[0], mods0, 0, w1i[0], w1o[0], n_x, sender("ffn1", 0),
                                      "l0_ffn1_b", out_tiles=n_x)
    grad_x = dx0[None]

    def as3(a):
        n_l = a.shape[0] if a.ndim == 3 else 1
        return a.reshape(n_l, -1, a.shape[-1])

    res = {}
    big_w = {"ffn1_wi": (ffn1_wi, m_ffn1_wi, v_ffn1_wi), "ffn1_wo": (ffn1_wo, m_ffn1_wo, v_ffn1_wo),
             "ffn2_wi": (ffn2_wi, m_ffn2_wi, v_ffn2_wi), "ffn2_wo": (ffn2_wo, m_ffn2_wo, v_ffn2_wo),
             "w_in": (w_in, m_w_in, v_w_in), "w_out": (w_out, m_w_out, v_w_out), "w_pool": (w_pool, m_w_pool, v_w_pool)}
    k_idx = k_me.reshape(1).astype(jnp.int32)
    chain = dx0
    for lo, hi in ((0, 2), (2, 5), (5, 7), (7, 9), (9, 11)):
        partial = []
        for nm, wname, layer, thru, lands, ssem, rsem in sent[lo:hi]:
            mine, recv = _scatter_wait(thru, lands, ssem, rsem, chain, "scatter_wait_" + nm)
            partial.append(_partial_sum(mine[0], recv[0], k_idx, "partial_sum_" + nm))
        other = _swap_sibling(partial, "swap_partials_%d" % lo)
        for (nm, wname, layer, _, _, _, _), p, q in zip(sent[lo:hi], partial, other):
            w, m, v = big_w[wname]
            res[wname] = _adamw(as3(w), [p, q], as3(m), as3(v), layer, res.get(wname),
                                "adamw_%s_%d" % (wname, layer))
            chain = res[wname][3]

    def mod_row(st1, dg1, stm, dgm, st2, dg2, s):
        return jnp.concatenate([st1[s, 0], st1[s, 1], dg1[s, 0], stm[s, 0], stm[s, 1], dgm[s, 0],
                                st2[s, 0], st2[s, 1], dg2[s, 0]])

    dg_bm2 = jnp.concatenate([dg_bm, jnp.zeros_like(dg_bm)], axis=0)[:, None, :]
    d_mm_x0 = mod_row(st_a1, dg_a1, st_am, dg_am, st_a2, dg_a2, 0)
    d_mm_x1 = mod_row(st_b1, dg_b1, st_bm, dg_bm2, st_b2, dg_b2, 0)
    d_mm_c0 = mod_row(st_a1, dg_a1, st_am, dg_am, st_a2, dg_a2, 1)
    d_norm_g = jnp.stack([jnp.stack([st[0, 2] + st[1, 2] for st in (st_a1, st_am, st_a2)]),
                          jnp.stack([st[0, 2] + st[1, 2] for st in (st_b1, st_bm, st_b2)])])
    rk = B_GATE_RANK
    pack = _flat_pad([d_mm_x0, d_mm_x1, d_mm_c0, d_norm_g, d_bias2, d_wg2[0:rk, 0:256], d_wg2[rk:2 * rk, 256:512],
                      d_sink[:, 0], jnp.zeros((120,), F32), d_glag, d_pscale, d_final_g],
                     _PACK_ROWS * 128).reshape(_PACK_ROWS, 128)
    pack = pack + 0.0 * chain[0, 0:1, 0:1]
    pack_all = _allgather_small(pack, "gather_small_grads")
    tot = _sum_devices(pack_all, "sum_small_grads").reshape(-1)
    rows_all = pack_all.reshape(N_DEV, -1)
    o = 3 * _N9
    g_norm_g_full = tot[o:o + 6 * D_MODEL].reshape(2, 3, D_MODEL)
    o += 6 * D_MODEL
    g_bias2 = tot[o:o + 512]
    o += 512
    g_w_a2_f_full = tot[o:o + rk * 256].reshape(rk, 256)
    o += rk * 256
    g_w_a2_b_full = tot[o:o + rk * 256].reshape(rk, 256)
    o += rk * 256
    g_sink = tot[o:o + A_HEADS]
    o += 128
    g_gla_g = tot[o:o + B_DV]
    o += B_DV
    g_pscale_full = tot[o:o + D_MODEL]
    o += D_MODEL
    g_final_g = tot[o:o + D_MODEL]
    d_mmc_tot = tot[2 * _N9:3 * _N9]
    g_b_mod = jnp.stack([tot[0:_N9] + d_mmc_tot, tot[_N9:2 * _N9]])

    zrows = jnp.zeros((_CROWS - N_DEV - 1, _N9), F32)
    d16 = jnp.stack([jnp.concatenate([rows_all[:, 0:_N9], d_mmc_tot[None], zrows], axis=0),
                     jnp.concatenate([rows_all[:, _N9:2 * _N9], jnp.zeros((1, _N9), F32), zrows], axis=0)])
    d16_k = lax.dynamic_slice(d16, (0, 0, k_me * mod_cols), (2, _CROWS, mod_cols))
    dmmc_k = lax.dynamic_slice(d_mmc_tot, (k_me * mod_cols,), (mod_cols,)).reshape(1, mod_cols)
    g_w_mod, c_part = _adaln_bwd(c16, d16_k, w_mod, dmmc_k, "adaln_bwd")
    c_parts = _allgather_small(c_part.reshape(8, 128), "gather_cctx")
    g_c_ctx = _cctx_grad(c_parts, c_ctx.reshape(8, 128), "cctx_grad").reshape(D_MODEL)

    def small(w, g, m, v, shape3, nm):
        return [o_.reshape(w.shape) for o_ in _adamw(w.reshape(shape3), [g.reshape(shape3[1:])],
                                                    m.reshape(shape3), v.reshape(shape3), 0, None, "adamw_" + nm)]

    def own(a, axis, size):
        return lax.dynamic_slice_in_dim(a, k_me * size, size, axis=axis)

    res["c_ctx"] = small(c_ctx, g_c_ctx, m_c_ctx, v_c_ctx, (1, 8, 128), "c_ctx")
    upd = _adamw(w_mod, [(g_w_mod, 1)], m_w_mod, v_w_mod, 1, None, "adamw_w_mod_1")
    res["w_mod"] = _adamw(w_mod, [(g_w_mod, 0)], m_w_mod, v_w_mod, 0, upd, "adamw_w_mod_0")
    res["b_mod"] = small(b_mod, g_b_mod, m_b_mod, v_b_mod, (1, 2, _N9), "b_mod")
    res["norm_g"] = small(norm_g, own(g_norm_g_full, 2, norm_g.shape[2]), m_norm_g, v_norm_g,
                          (1, 6, norm_g.shape[2]), "norm_g")
    res["w_a2_f"] = small(w_a2_f, own(g_w_a2_f_full, 1, w_a2_f.shape[2]), m_w_a2_f, v_w_a2_f,
                          (1, rk, w_a2_f.shape[2]), "w_a2_f")
    res["b_a_f"] = small(b_a_f, g_bias2[0:256], m_b_a_f, v_b_a_f, (1, 1, 256), "b_a_f")
    res["w_a2_b"] = small(w_a2_b, own(g_w_a2_b_full, 1, w_a2_b.shape[2]), m_w_a2_b, v_w_a2_b,
                          (1, rk, w_a2_b.shape[2]), "w_a2_b")
    res["b_a_b"] = small(b_a_b, g_bias2[256:512], m_b_a_b, v_b_a_b, (1, 1, 256), "b_a_b")
    res["sink"] = small(sink, g_sink, m_sink, v_sink, (1, 1, A_HEADS), "sink")
    res["gla_g"] = small(gla_g, g_gla_g, m_gla_g, v_gla_g, (1, 1, B_DV), "gla_g")
    res["pool_scale"] = small(pool_scale, own(g_pscale_full, 0, pool_scale.shape[1]), m_pool_scale, v_pool_scale,
                              (1, 1, pool_scale.shape[1]), "pool_scale")
    res["final_g"] = small(final_g, g_final_g, m_final_g, v_final_g, (1, 8, 128), "final_g")
    for wname, (w, _, _) in big_w.items():
        res[wname] = [o_.reshape(w.shape) for o_ in res[wname]]

    names = ["c_ctx", "w_mod", "b_mod", "norm_g", "ffn1_wi", "ffn1_wo", "ffn2_wi", "ffn2_wo", "w_in", "w_a2_f",
             "b_a_f", "w_a2_b", "b_a_b", "sink", "gla_g", "w_out", "w_pool", "pool_scale", "final_g"]
    outs = [loss, grad_x]
    for field in range(4):
        outs += [res[nm][field] for nm in names]
    return tuple(outs)
```

```python
import functools

import jax
import jax.numpy as jnp
import numpy as np
from jax import lax
from jax.experimental import pallas as pl
from jax.experimental.pallas import tpu as pltpu

F32 = jnp.float32
BF16 = jnp.bfloat16

D_MODEL = 1024
N_MOD = 9
D_FF = 2816
RMS_EPS = 1e-6
A_HEADS = 8
A_KV_HEADS = 2
A_HEAD_DIM = 64
WINDOW = 128
ROPE_BASE = 10000.0
GRID_W = 64
B_HEADS = 4
B_DK = 64
B_DV = 128
B_GATE_RANK = 16
B_GATE_NORM = 16.0
B_CHUNK = 64
POOL_WINDOWS = (2, 4, 8, 16)
POOL_GROUP = D_MODEL // len(POOL_WINDOWS)
PROJ_DIM = 2336

ADAM_LR = 0.001
ADAM_B1 = 0.9
ADAM_B2 = 0.999
ADAM_EPS = 1e-08
ADAM_WD = 0.01
ADAM_STEP = 10

N_CHIPS = 4
N_DEV = 8
ROW_TILE = 512
VMEM_LIMIT_BYTES = 56 * 1024 * 1024
MESH = pl.DeviceIdType.MESH

ZC_Q, ZC_QK, ZC_V, ZC_R, ZC_KV, ZC_G, ZC_W = 0, 512, 1024, 1536, 2048, 2304, 2432


def _cp(*sem):
    return pltpu.CompilerParams(dimension_semantics=sem if sem else None, vmem_limit_bytes=VMEM_LIMIT_BYTES)


def _dot(a, b):
    return jnp.dot(a, b, preferred_element_type=F32)


def _dot_nt(a, b):
    return lax.dot_general(a, b, (((1,), (1,)), ((), ())), preferred_element_type=F32)


def _dot_tn(a, b):
    return lax.dot_general(a, b, (((0,), (0,)), ((), ())), preferred_element_type=F32)


def _dot_hi(a, b):
    return jnp.dot(a, b, preferred_element_type=F32, precision=lax.Precision.HIGHEST)


def _dot_tn_hi(a, b):
    return lax.dot_general(a, b, (((0,), (0,)), ((), ())), preferred_element_type=F32,
                           precision=lax.Precision.HIGHEST)


def _sigmoid(x):
    return 1.0 / (1.0 + jnp.exp(-x))


MXU_COLS = 256


def _col_chunks(n):
    return [(c0, min(MXU_COLS, n - c0)) for c0 in range(0, n, MXU_COLS)]


WIDE_ROW_TILE = 1024


def _matmul_row_tile(rows, n_x):
    if rows % WIDE_ROW_TILE == 0 and n_x * ROW_TILE >= rows:
        return WIDE_ROW_TILE
    return ROW_TILE


def _resident(block_shape, index_map):
    return pl.BlockSpec(block_shape, index_map, pipeline_mode=pl.Buffered(1))


def _stream_of(i, n_x):
    return jnp.where(i >= n_x, 1, 0)


def _rms_mod_fwd(x, g3, mods, j, n_x, out_dtype, name):
    rows = x.shape[0]
    tm = ROW_TILE
    n_i = rows // tm

    def body(x_ref, g_ref, m_ref, o_ref):
        xv = x_ref[...]
        r = lax.rsqrt(jnp.mean(xv * xv, axis=-1, keepdims=True) + RMS_EPS)
        g = g_ref[j:j + 1, :]
        shift = m_ref[0, 3 * j:3 * j + 1, :]
        scale = m_ref[0, 3 * j + 1:3 * j + 2, :]
        o_ref[...] = (((xv * r) * g) * (1.0 + scale) + shift).astype(out_dtype)

    return pl.pallas_call(
        body, name=name, grid=(n_i,),
        in_specs=[pl.BlockSpec((tm, D_MODEL), lambda i: (i, 0)),
                  pl.BlockSpec((3, D_MODEL), lambda i: (0, 0)),
                  pl.BlockSpec((1, N_MOD, D_MODEL), lambda i: (_stream_of(i, n_x), 0, 0))],
        out_specs=pl.BlockSpec((tm, D_MODEL), lambda i: (i, 0)),
        out_shape=jax.ShapeDtypeStruct((rows, D_MODEL), out_dtype),
        compiler_params=_cp("parallel"),
    )(x, g3, mods)


def _rms_mod_bwd_tail(dh, xv, g, scale, stream, acc_ref, first):
    r = lax.rsqrt(jnp.mean(xv * xv, axis=-1, keepdims=True) + RMS_EPS)
    xhat = xv * r
    t1 = jnp.sum(dh, axis=0, keepdims=True)
    t2 = jnp.sum(dh * xhat, axis=0, keepdims=True)
    stats = jnp.concatenate([t1, t2 * g, t2 * (1.0 + scale)], axis=0)

    @pl.when(first)
    def _():
        acc_ref[...] = jnp.zeros_like(acc_ref)

    acc_ref[pl.ds(stream, 1)] += stats[None]
    dxh = dh * (g * (1.0 + scale))
    return r * (dxh - xhat * jnp.mean(dxh * xhat, axis=-1, keepdims=True))


def _ffn_up(x, g3, mods, jmod, n_x, w4, name):
    rows = x.shape[0]
    h = w4.shape[2]
    tm = _matmul_row_tile(rows, n_x)
    n_i = rows // tm

    def body(x_ref, g_ref, m_ref, wa_ref, wu_ref, hn_ref, au_ref, s_ref):
        xv = x_ref[...]
        r = lax.rsqrt(jnp.mean(xv * xv, axis=-1, keepdims=True) + RMS_EPS)
        g = g_ref[jmod:jmod + 1, :]
        shift = m_ref[0, 3 * jmod:3 * jmod + 1, :]
        scale = m_ref[0, 3 * jmod + 1:3 * jmod + 2, :]
        hv = (((xv * r) * g) * (1.0 + scale) + shift).astype(BF16)

        @pl.when(pl.program_id(0) == 0)
        def _():
            hn_ref[...] = hv

        for c0, cw in _col_chunks(h):
            cols = slice(c0, c0 + cw)
            a = _dot(hv, wa_ref[0, :, cols])
            u = _dot(hv, wu_ref[0, :, cols])
            sg = _sigmoid(a)
            silu = a * sg
            au_ref[0, :, cols] = (u * (sg * (1.0 + a * (1.0 - sg)))).astype(BF16)
            au_ref[1, :, cols] = silu.astype(BF16)
            s_ref[:, cols] = (silu * u).astype(BF16)

    return pl.pallas_call(
        body, name=name, grid=(2, n_i),
        in_specs=[pl.BlockSpec((tm, D_MODEL), lambda j, i: (i, 0)),
                  pl.BlockSpec((3, D_MODEL), lambda j, i: (0, 0)),
                  pl.BlockSpec((1, N_MOD, D_MODEL), lambda j, i: (_stream_of(i, n_x), 0, 0)),
                  pl.BlockSpec((1, D_MODEL, h), lambda j, i: (j, 0, 0)),
                  pl.BlockSpec((1, D_MODEL, h), lambda j, i: (j + 2, 0, 0))],
        out_specs=[pl.BlockSpec((tm, D_MODEL), lambda j, i: (jnp.where(j == 0, i, n_i - 1), 0)),
                   pl.BlockSpec((2, tm, h), lambda j, i: (0, i, j)),
                   pl.BlockSpec((tm, h), lambda j, i: (i, j))],
        out_shape=[jax.ShapeDtypeStruct((rows, D_MODEL), BF16),
                   jax.ShapeDtypeStruct((2, rows, 2 * h), BF16),
                   jax.ShapeDtypeStruct((rows, 2 * h), BF16)],
        compiler_params=_cp("arbitrary", "arbitrary"),
    )(x, g3, mods, w4, w4)


def _matmul_resid(a, w, xres, mods, gate_idx, coef, n_x, rows, name):
    k = a.shape[1]
    tm = _matmul_row_tile(rows, n_x)
    n_i = rows // tm

    def body(a_ref, w_ref, x_ref, m_ref, o_ref, f_ref):
        av = a_ref[...]
        for c0, cw in _col_chunks(D_MODEL):
            cols = slice(c0, c0 + cw)
            f = _dot(av, w_ref[:, cols])
            f_ref[:, cols] = f.astype(BF16)
            o_ref[:, cols] = x_ref[:, cols] + (coef * m_ref[0, gate_idx:gate_idx + 1, cols]) * f

    return pl.pallas_call(
        body, name=name, grid=(n_i,),
        in_specs=[pl.BlockSpec((tm, k), lambda i: (i, 0)),
                  _resident((k, D_MODEL), lambda i: (0, 0)),
                  pl.BlockSpec((tm, D_MODEL), lambda i: (i, 0)),
                  pl.BlockSpec((1, N_MOD, D_MODEL), lambda i: (_stream_of(i, n_x), 0, 0))],
        out_specs=[pl.BlockSpec((tm, D_MODEL), lambda i: (i, 0)),
                   pl.BlockSpec((tm, D_MODEL), lambda i: (i, 0))],
        out_shape=[jax.ShapeDtypeStruct((rows, D_MODEL), F32),
                   jax.ShapeDtypeStruct((rows, D_MODEL), BF16)],
        compiler_params=_cp("parallel"),
    )(a, w, xres, mods)


def _gate_dy(dout, f, mods, gate_idx, coef, n_x, rows, w, name):
    tm = ROW_TILE
    n_i = rows // tm
    n_out = w.shape[0]

    def body(d_ref, f_ref, m_ref, w_ref, dy_ref, da_ref, acc_ref):
        i = pl.program_id(0)
        dv = d_ref[...]
        gate = m_ref[0, gate_idx:gate_idx + 1, :]
        dyb = (dv * (coef * gate)).astype(BF16)
        dy_ref[...] = dyb
        da_ref[...] = _dot_nt(dyb, w_ref[...])

        @pl.when(i == 0)
        def _():
            acc_ref[...] = jnp.zeros_like(acc_ref)

        part = coef * jnp.sum(dv * f_ref[...].astype(F32), axis=0, keepdims=True)
        acc_ref[pl.ds(_stream_of(i, n_x), 1)] += part[None]

    return pl.pallas_call(
        body, name=name, grid=(n_i,),
        in_specs=[pl.BlockSpec((tm, D_MODEL), lambda i: (i, 0)),
                  pl.BlockSpec((tm, D_MODEL), lambda i: (i, 0)),
                  pl.BlockSpec((1, N_MOD, D_MODEL), lambda i: (_stream_of(i, n_x), 0, 0)),
                  pl.BlockSpec((n_out, D_MODEL), lambda i: (0, 0))],
        out_specs=[pl.BlockSpec((tm, D_MODEL), lambda i: (i, 0)),
                   pl.BlockSpec((tm, n_out), lambda i: (i, 0)),
                   pl.BlockSpec((2, 1, D_MODEL), lambda i: (0, 0, 0))],
        out_shape=[jax.ShapeDtypeStruct((rows, D_MODEL), BF16),
                   jax.ShapeDtypeStruct((rows, n_out), F32),
                   jax.ShapeDtypeStruct((2, 1, D_MODEL), F32)],
        compiler_params=_cp("arbitrary"),
    )(dout, f, mods, w)


def _ffn_bwd_dz(dout, f, mods, gate_idx, coef, n_x, wo2, au, name):
    rows = dout.shape[0]
    h = wo2.shape[1]
    tm = ROW_TILE
    n_i = rows // tm

    def body(d_ref, f_ref, m_ref, wo_ref, au_ref, dy_ref, dz_ref, acc_ref):
        j, i = pl.program_id(0), pl.program_id(1)
        dv = d_ref[...]
        gate = m_ref[0, gate_idx:gate_idx + 1, :]
        dyb = (dv * (coef * gate)).astype(BF16)

        @pl.when((j == 0) & (i == 0))
        def _():
            acc_ref[...] = jnp.zeros_like(acc_ref)

        @pl.when(j == 0)
        def _():
            dy_ref[...] = dyb
            part = coef * jnp.sum(dv * f_ref[...].astype(F32), axis=0, keepdims=True)
            acc_ref[pl.ds(_stream_of(i, n_x), 1)] += part[None]

        for c0, cw in _col_chunks(h):
            cols = slice(c0, c0 + cw)
            ds = _dot_nt(dyb, wo_ref[0, cols, :])
            dz_ref[0, :, cols] = (ds * au_ref[0, :, cols].astype(F32)).astype(BF16)
            dz_ref[1, :, cols] = (ds * au_ref[1, :, cols].astype(F32)).astype(BF16)

    return pl.pallas_call(
        body, name=name, grid=(2, n_i),
        in_specs=[pl.BlockSpec((tm, D_MODEL), lambda j, i: (i, 0)),
                  pl.BlockSpec((tm, D_MODEL), lambda j, i: (jnp.where(j == 0, i, n_i - 1), 0)),
                  pl.BlockSpec((1, N_MOD, D_MODEL), lambda j, i: (_stream_of(i, n_x), 0, 0)),
                  pl.BlockSpec((1, h, D_MODEL), lambda j, i: (j, 0, 0)),
                  pl.BlockSpec((2, tm, h), lambda j, i: (0, i, j))],
        out_specs=[pl.BlockSpec((tm, D_MODEL), lambda j, i: (jnp.where(j == 0, i, n_i - 1), 0)),
                   pl.BlockSpec((2, tm, h), lambda j, i: (0, i, j)),
                   pl.BlockSpec((2, 1, D_MODEL), lambda j, i: (0, 0, 0))],
        out_shape=[jax.ShapeDtypeStruct((rows, D_MODEL), BF16),
                   jax.ShapeDtypeStruct((2, rows, 2 * h), BF16),
                   jax.ShapeDtypeStruct((2, 1, D_MODEL), F32)],
        compiler_params=_cp("arbitrary", "arbitrary"),
    )(dout, f, mods, wo2, au)


def _token_tile(rows):
    for tk in (2048, 1536, 1024):
        if rows % tk == 0:
            return tk
    return ROW_TILE


def _matmul_tn(a, b, a_spec, b_spec, out_shape, out_spec, grid, name):
    nd_a = len(a_spec.block_shape)
    nd_b = len(b_spec.block_shape)
    nd_o = len(out_spec.block_shape)
    k_axis = len(grid) - 1
    n_k = grid[k_axis]

    def body(a_ref, b_ref, o_ref, acc_ref):
        av = a_ref[(0,) * (nd_a - 2)]
        bv = b_ref[(0,) * (nd_b - 2)]
        part = _dot_tn(av, bv)
        k = pl.program_id(k_axis)

        @pl.when(k == 0)
        def _():
            acc_ref[...] = part

        @pl.when(k > 0)
        def _():
            acc_ref[...] += part

        @pl.when(k == n_k - 1)
        def _():
            o_ref[(0,) * (nd_o - 2)] = acc_ref[...].astype(BF16)

    return pl.pallas_call(
        body, name=name, grid=grid, in_specs=[a_spec, b_spec], out_specs=out_spec,
        out_shape=jax.ShapeDtypeStruct(out_shape, BF16),
        scratch_shapes=[pltpu.VMEM(tuple(out_spec.block_shape[-2:]), F32)],
        compiler_params=_cp(*(("arbitrary",) * len(grid))),
    )(a, b)


def _bwd_dx(pairs, x, dres, dres_tiles, g3, mods, j, n_x, name, out_tiles=None):
    rows = x.shape[0]
    tm = ROW_TILE
    n_i = rows // tm
    n_o = n_i if out_tiles is None else out_tiles
    n_p = len(pairs)
    nds = [(len(p[1].block_shape), len(p[3].block_shape)) for p in pairs]

    def body(*refs):
        dz_refs = refs[0:2 * n_p:2]
        w_refs = refs[1:2 * n_p:2]
        x_ref, dres_ref, g_ref, m_ref, dx_ref, acc_ref = refs[2 * n_p:]
        i = pl.program_id(0)
        dzs = [dz_refs[p][(0,) * (nds[p][0] - 2)] for p in range(n_p)]
        pieces = []
        for c0, cw in _col_chunks(D_MODEL):
            acc = None
            for p in range(n_p):
                lead = (0,) * (nds[p][1] - 2)
                part = _dot_nt(dzs[p], w_refs[p][lead + (slice(c0, c0 + cw), slice(None))])
                acc = part if acc is None else acc + part
            pieces.append(acc)
        dh = jnp.concatenate(pieces, axis=1)
        g = g_ref[j:j + 1, :]
        scale = m_ref[0, 3 * j + 1:3 * j + 2, :]
        dx = _rms_mod_bwd_tail(dh, x_ref[...], g, scale, _stream_of(i, n_x), acc_ref, i == 0)
        dres_v = jnp.where(i < dres_tiles, dres_ref[...], 0.0)

        @pl.when(i < n_o)
        def _():
            dx_ref[...] = dres_v + dx

    in_specs, args = [], []
    for dz, dz_spec, w, w_spec in pairs:
        in_specs += [dz_spec, w_spec]
        args += [dz, w]
    in_specs += [pl.BlockSpec((tm, D_MODEL), lambda i: (i, 0)),
                 pl.BlockSpec((tm, D_MODEL), lambda i: (jnp.minimum(i, dres_tiles - 1), 0)),
                 pl.BlockSpec((3, D_MODEL), lambda i: (0, 0)),
                 pl.BlockSpec((1, N_MOD, D_MODEL), lambda i: (_stream_of(i, n_x), 0, 0))]
    args += [x, dres, g3, mods]
    return pl.pallas_call(
        body, name=name, grid=(n_i,), in_specs=in_specs,
        out_specs=[pl.BlockSpec((tm, D_MODEL), lambda i: (jnp.minimum(i, n_o - 1), 0)),
                   pl.BlockSpec((2, 3, D_MODEL), lambda i: (0, 0, 0))],
        out_shape=[jax.ShapeDtypeStruct((n_o * tm, D_MODEL), F32),
                   jax.ShapeDtypeStruct((2, 3, D_MODEL), F32)],
        compiler_params=_cp("arbitrary"),
    )(*args)


def _ffn_forward(x, g3, mods, j, w4_in, w4_out_of, n_x, name):
    rows = x.shape[0]
    hn, au, s = _ffn_up(x, g3, mods, j, n_x, w4_in, name + "_up")
    w4_out, dep = w4_out_of(s)
    if dep is not None:
        mods = mods + dep[0:1, 0:1]
    wo = w4_out.reshape(D_FF, D_MODEL)
    out, f = _matmul_resid(s, wo, x, mods, 3 * j + 2, 0.5, n_x, rows, name + "_down")
    return out, (x, hn, au, s, f), w4_out


def _ffn_backward(dout, saved, g3, mods, j, w4_in, w4_out, n_x, send, name, out_tiles=None):
    x, hn, au, s, f = saved
    rows = x.shape[0]
    tm = ROW_TILE
    n_i = rows // tm
    h = w4_in.shape[2]
    wo2 = w4_out.reshape(2, h, D_MODEL)
    dy, dz, dgate = _ffn_bwd_dz(dout, f, mods, 3 * j + 2, 0.5, n_x, wo2, au, name + "_dz")
    tk = _token_tile(rows)
    n_k = rows // tk
    d_wi = _matmul_tn(
        hn, dz, pl.BlockSpec((tk, D_MODEL), lambda q, k: (k, 0)),
        pl.BlockSpec((1, tk, h), lambda q, k: (q // 2, k, q % 2)),
        (4, D_MODEL, h), pl.BlockSpec((1, D_MODEL, h), lambda q, k: (q, 0, 0)), (4, n_k), name + "_dwi")
    mods = mods + send(d_wi, "wi")
    d_wo = _matmul_tn(
        s, dy, pl.BlockSpec((tk, h), lambda n, k: (k, n)), pl.BlockSpec((tk, D_MODEL), lambda n, k: (k, 0)),
        (D_FF, D_MODEL), pl.BlockSpec((h, D_MODEL), lambda n, k: (n, 0)), (2, n_k), name + "_dwo")
    mods = mods + send(d_wo.reshape(w4_out.shape), "wo")
    pairs = [(dz, pl.BlockSpec((1, tm, h), functools.partial(lambda q, i: (q // 2, i, q % 2), q)),
              w4_in, pl.BlockSpec((1, D_MODEL, h), functools.partial(lambda q, i: (q, 0, 0), q)))
             for q in range(4)]
    dx, stats = _bwd_dx(pairs, x, dout, n_i, g3, mods, j, n_x, name + "_dx", out_tiles)
    return dx, stats, dgate


def _rope_tables(t_len, rows):
    n = A_HEAD_DIM // 4
    freqs = ROPE_BASE ** (-jnp.arange(n, dtype=F32) / n)
    t = jnp.arange(t_len)
    ang_r = (t // GRID_W).astype(F32)[:, None] * freqs
    ang_c = (t % GRID_W).astype(F32)[:, None] * freqs
    cos = jnp.concatenate([jnp.cos(ang_r), jnp.cos(ang_r), jnp.cos(ang_c), jnp.cos(ang_c)], axis=1)
    sin = jnp.concatenate([-jnp.sin(ang_r), jnp.sin(ang_r), -jnp.sin(ang_c), jnp.sin(ang_c)], axis=1)
    cos = jnp.concatenate([cos, jnp.ones((rows - t_len, A_HEAD_DIM), F32)], axis=0)
    sin = jnp.concatenate([sin, jnp.zeros((rows - t_len, A_HEAD_DIM), F32)], axis=0)
    return jnp.concatenate([cos, cos, sin, sin], axis=1)


def _swap16(x):
    n = x.shape[1]
    lane = lax.broadcasted_iota(jnp.int32, x.shape, 1)
    first = jnp.bitwise_and(lane, 16) == 0
    return jnp.where(first, pltpu.roll(x, n - 16, 1), pltpu.roll(x, 16, 1))


def _log_sigmoid(x):
    return jnp.minimum(x, 0.0) - jnp.log(1.0 + jnp.exp(-jnp.abs(x)))


def _proj_fwd(x, g3, mods, n_x, wcat, wg2, bias2, cs, name):
    rows = x.shape[0]
    tm = ROW_TILE

    def body(x_ref, g_ref, m_ref, w_ref, wg_ref, b_ref, cs_ref, h_ref, zc_ref, la_ref):
        xv = x_ref[...]
        r = lax.rsqrt(jnp.mean(xv * xv, axis=-1, keepdims=True) + RMS_EPS)
        hv = (((xv * r) * g_ref[1:2, :]) * (1.0 + m_ref[0, 4:5, :]) + m_ref[0, 3:4, :]).astype(BF16)
        h_ref[...] = hv
        z = _dot(hv, w_ref[...])
        cos = cs_ref[:, 0:128]
        sin = cs_ref[:, 128:256]
        cosq = jnp.concatenate([cos] * 4, axis=1)
        sinq = jnp.concatenate([sin] * 4, axis=1)
        q = z[:, ZC_Q:ZC_QK]
        zc_ref[:, ZC_Q:ZC_QK] = q * cosq + _swap16(q) * sinq
        zc_ref[:, ZC_QK:ZC_KV] = z[:, ZC_QK:ZC_KV]
        kk = z[:, ZC_KV:ZC_KV + 128]
        zc_ref[:, ZC_KV:ZC_KV + 128] = kk * cos + _swap16(kk) * sin
        zc_ref[:, ZC_KV + 128:ZC_W] = z[:, ZC_KV + 128:ZC_W]
        zg = z[:, ZC_G:ZC_W]
        pre = _dot(zg.astype(BF16), wg_ref[...]) + b_ref[...]
        la_ref[...] = _log_sigmoid(pre) / B_GATE_NORM

    return pl.pallas_call(
        body, name=name, grid=(rows // tm,),
        in_specs=[pl.BlockSpec((tm, D_MODEL), lambda i: (i, 0)),
                  pl.BlockSpec((3, D_MODEL), lambda i: (0, 0)),
                  pl.BlockSpec((1, N_MOD, D_MODEL), lambda i: (_stream_of(i, n_x), 0, 0)),
                  pl.BlockSpec((D_MODEL, ZC_W), lambda i: (0, 0)),
                  pl.BlockSpec((128, 512), lambda i: (0, 0)),
                  pl.BlockSpec((1, 512), lambda i: (0, 0)),
                  pl.BlockSpec((tm, 256), lambda i: (i, 0))],
        out_specs=[pl.BlockSpec((tm, D_MODEL), lambda i: (i, 0)),
                   pl.BlockSpec((tm, ZC_W), lambda i: (i, 0)),
                   pl.BlockSpec((tm, 512), lambda i: (i, 0))],
        out_shape=[jax.ShapeDtypeStruct((rows, D_MODEL), BF16),
                   jax.ShapeDtypeStruct((rows, ZC_W), F32),
                   jax.ShapeDtypeStruct((rows, 512), F32)],
        compiler_params=_cp("parallel"),
    )(x, g3, mods, wcat, wg2, bias2, cs)


_QB = WINDOW


def _attn_specs(t_len, l_ctx):
    nb = t_len // _QB
    kvb = ZC_KV // 256
    return [pl.BlockSpec(memory_space=pltpu.SMEM),
            pl.BlockSpec((_QB, 512), lambda n: (n, 0)),
            pl.BlockSpec((_QB, 256), lambda n: (jnp.maximum(n - 1, 0), kvb)),
            pl.BlockSpec((_QB, 256), lambda n: (n, kvb)),
            pl.BlockSpec((_QB, 256), lambda n: (n + 1, kvb)),
            pl.BlockSpec((l_ctx, 256), lambda n: (t_len // l_ctx, kvb))], nb


_HEAD_PAIRS = ((0, 1), (2, 3))


def _attn_keys(kp, kc, kn, kx, g):
    hd = A_HEAD_DIM
    ks = slice(g * hd, (g + 1) * hd)
    vs = slice(128 + g * hd, 128 + (g + 1) * hd)
    kb = jnp.concatenate([kp[:, ks], kc[:, ks], kn[:, ks]], axis=0).astype(BF16)
    vb = jnp.concatenate([kp[:, vs], kc[:, vs], kn[:, vs]], axis=0).astype(BF16)
    return kb, vb, kx[:, ks].astype(BF16), kx[:, vs].astype(BF16)


def _attn_probs(n, t_len, sink_ref, qv, kb, kxb, g, rs):
    hd = A_HEAD_DIM
    qg = jnp.concatenate([qv[:, (4 * g + r) * hd:(4 * g + r + 1) * hd] for r in rs], axis=0).astype(BF16)
    qi = lax.broadcasted_iota(jnp.int32, (_QB, 3 * _QB), 0)
    kj = lax.broadcasted_iota(jnp.int32, (_QB, 3 * _QB), 1)
    kpos = n * _QB - _QB + kj
    valid = (kpos >= 0) & (kpos < t_len) & (jnp.abs(kj - _QB - qi) <= WINDOW)
    valid = jnp.concatenate([valid] * len(rs), axis=0)
    scale = hd ** -0.5
    s = jnp.where(valid, _dot_nt(qg, kb) * scale, -jnp.inf)
    sc = _dot_nt(qg, kxb) * scale
    sk = jnp.concatenate([jnp.full((_QB, 1), sink_ref[4 * g + r], F32) for r in rs], axis=0)
    m = jnp.maximum(jnp.maximum(jnp.max(s, axis=-1, keepdims=True), jnp.max(sc, axis=-1, keepdims=True)), sk)
    p = jnp.exp(s - m)
    pc = jnp.exp(sc - m)
    ps = jnp.exp(sk - m)
    inv = 1.0 / (jnp.sum(p, axis=-1, keepdims=True) + jnp.sum(pc, axis=-1, keepdims=True) + ps)
    return p, pc, ps, inv, qg


def _attn_fwd(zc, sink, t_len, l_ctx, name):
    in_specs, nb = _attn_specs(t_len, l_ctx)

    def body(sink_ref, q_ref, kp_ref, kc_ref, kn_ref, kx_ref, o_ref):
        n = pl.program_id(0)
        qv = q_ref[...]
        outs = []
        for g in range(A_KV_HEADS):
            kb, vb, kxb, vxb = _attn_keys(kp_ref[...], kc_ref[...], kn_ref[...], kx_ref[...], g)
            for rs in _HEAD_PAIRS:
                p, pc, _, inv, _ = _attn_probs(n, t_len, sink_ref, qv, kb, kxb, g, rs)
                o = (_dot(p.astype(BF16), vb) + _dot(pc.astype(BF16), vxb)) * inv
                outs += [o[i * _QB:(i + 1) * _QB] for i in range(len(rs))]
        o_ref[...] = jnp.concatenate(outs, axis=1)

    return pl.pallas_call(
        body, name=name, grid=(nb,), in_specs=in_specs,
        out_specs=pl.BlockSpec((_QB, 512), lambda n: (n, 0)),
        out_shape=jax.ShapeDtypeStruct((t_len, 512), F32),
        compiler_params=_cp("parallel"),
    )(sink, zc, zc, zc, zc, zc)


def _attn_bwd(zc, sink, o, dcat, t_len, l_ctx, name):
    rows = zc.shape[0]
    in_specs, nb = _attn_specs(t_len, l_ctx)
    in_specs = in_specs + [pl.BlockSpec((_QB, 512), lambda n: (n, 0)), pl.BlockSpec((_QB, 512), lambda n: (n, 0))]
    hd = A_HEAD_DIM
    scale = hd ** -0.5

    def body(sink_ref, q_ref, kp_ref, kc_ref, kn_ref, kx_ref, o_ref, do_ref, dq_ref, dkv_ref, dsink_ref):
        n = pl.program_id(0)

        @pl.when(n == 0)
        def _():
            dkv_ref[...] = jnp.zeros_like(dkv_ref)
            dsink_ref[...] = jnp.zeros_like(dsink_ref)

        qv = q_ref[...]
        ov = o_ref[...]
        dov = do_ref[...]
        dqs, dkbs, dvbs, dkxs, dvxs, dsinks = [], [], [], [], [], []
        for g in range(A_KV_HEADS):
            kb, vb, kxb, vxb = _attn_keys(kp_ref[...], kc_ref[...], kn_ref[...], kx_ref[...], g)
            parts = []
            for rs in _HEAD_PAIRS:
                p, pc, ps, inv, qg = _attn_probs(n, t_len, sink_ref, qv, kb, kxb, g, rs)
                og = jnp.concatenate([ov[:, (4 * g + r) * hd:(4 * g + r + 1) * hd] for r in rs], axis=0)
                dog = jnp.concatenate([dov[:, (4 * g + r) * hd:(4 * g + r + 1) * hd] for r in rs], axis=0)
                delta = jnp.sum(og * dog, axis=-1, keepdims=True)
                dogb = dog.astype(BF16)
                pn = p * inv
                pcn = pc * inv
                ds = (pn * (_dot_nt(dogb, vb) - delta) * scale).astype(BF16)
                dsc = (pcn * (_dot_nt(dogb, vxb) - delta) * scale).astype(BF16)
                dsk = (ps * inv) * (0.0 - delta)
                dqg = _dot(ds, kb) + _dot(dsc, kxb)
                dqs += [dqg[i * _QB:(i + 1) * _QB] for i in range(len(rs))]
                parts.append((_dot_tn(ds, qg), _dot_tn(pn.astype(BF16), dogb),
                              _dot_tn(dsc, qg), _dot_tn(pcn.astype(BF16), dogb)))
                for i in range(len(rs)):
                    tot = jnp.sum(dsk[i * _QB:(i + 1) * _QB], axis=0, keepdims=True)
                    dsinks.append(jnp.broadcast_to(tot, (1, 128)))
            dkbs.append(parts[0][0] + parts[1][0])
            dvbs.append(parts[0][1] + parts[1][1])
            dkxs.append(parts[0][2] + parts[1][2])
            dvxs.append(parts[0][3] + parts[1][3])
        dsink_ref[...] += jnp.concatenate(dsinks, axis=0)
        dq_ref[...] = jnp.concatenate(dqs, axis=1)
        band = jnp.concatenate(dkbs + dvbs, axis=1)
        ctxc = jnp.concatenate(dkxs + dvxs, axis=1)
        r_prev = pl.multiple_of(jnp.maximum(n - 1, 0) * _QB, _QB)
        r_cur = pl.multiple_of(n * _QB, _QB)
        r_next = pl.multiple_of((n + 1) * _QB, _QB)
        dkv_ref[pl.ds(r_prev, _QB), :] += band[0:_QB]
        dkv_ref[pl.ds(r_cur, _QB), :] += band[_QB:2 * _QB]
        dkv_ref[pl.ds(r_next, _QB), :] += band[2 * _QB:3 * _QB]
        dkv_ref[t_len:t_len + l_ctx, :] += ctxc

    return pl.pallas_call(
        body, name=name, grid=(nb,), in_specs=in_specs,
        out_specs=[pl.BlockSpec((_QB, 512), lambda n: (n, 0)),
                   pl.BlockSpec((rows, 256), lambda n: (0, 0)),
                   pl.BlockSpec((8, 128), lambda n: (0, 0))],
        out_shape=[jax.ShapeDtypeStruct((t_len, 512), F32),
                   jax.ShapeDtypeStruct((rows, 256), F32),
                   jax.ShapeDtypeStruct((8, 128), F32)],
        compiler_params=_cp("arbitrary"),
    )(sink, zc, zc, zc, zc, zc, o, dcat)


_GC = B_CHUNK


def _split_bf16(a):
    hi = a.astype(BF16)
    return hi, (a - hi.astype(F32)).astype(BF16)


def _gla_chunk_terms(qk, la, reverse):
    q = qk[:, 0:256]
    k = qk[:, 256:512]
    off = 256 if reverse else 0
    lad = la[:, off:off + 256]
    ii = lax.broadcasted_iota(jnp.int32, (_GC, _GC), 0)
    jj = lax.broadcasted_iota(jnp.int32, (_GC, _GC), 1)
    mask = (jj >= ii) if reverse else (jj <= ii)
    tri = jnp.where(mask, 1.0, 0.0).astype(BF16)
    la_hi, la_lo = _split_bf16(lad)
    g = _dot(tri, la_hi) + _dot(tri, la_lo)
    gl = jnp.sum(lad, axis=0, keepdims=True)
    eg = jnp.exp(g)
    eng = jnp.exp(-g)
    eend = jnp.exp(gl - g)
    sc = B_DK ** -0.5
    qt = q * (sc * eg)
    kt = k * eng
    ke = k * eend
    return mask, tri, gl, eg, eng, eend, qt, kt, ke


def _head(a, hh, width):
    return a[:, hh * width:(hh + 1) * width]


def _same_head(rows, cols, row_shift, col_shift):
    r = jnp.right_shift(lax.broadcasted_iota(jnp.int32, (rows, cols), 0), row_shift)
    c = jnp.right_shift(lax.broadcasted_iota(jnp.int32, (rows, cols), 1), col_shift)
    return r == c


def _block_diag_rows(x, col_shift):
    tiled = jnp.concatenate([x] * B_HEADS, axis=0)
    return jnp.where(_same_head(tiled.shape[0], tiled.shape[1], 6, col_shift), tiled, jnp.zeros_like(tiled))


def _fold_heads(x):
    c = x.shape[0] // B_HEADS
    return (x[0:c] + x[c:2 * c]) + (x[2 * c:3 * c] + x[3 * c:4 * c])


def _chunk_mask4(reverse):
    ii = lax.broadcasted_iota(jnp.int32, (_GC, B_HEADS * _GC), 0)
    jj = jnp.bitwise_and(lax.broadcasted_iota(jnp.int32, (_GC, B_HEADS * _GC), 1), _GC - 1)
    return (jj >= ii) if reverse else (jj <= ii)


_ST_SHAPE = (B_HEADS * B_DV, B_HEADS * B_DK)


def _state_blocks(t):
    return [t[hh * B_DV:(hh + 1) * B_DV, hh * B_DK:(hh + 1) * B_DK] for hh in range(B_HEADS)]


def _state_from_blocks(blocks):
    full = jnp.concatenate([jnp.concatenate([b] * B_HEADS, axis=1) for b in blocks], axis=0)
    return jnp.where(_same_head(_ST_SHAPE[0], _ST_SHAPE[1], 7, 6), full, 0.0)


def _gla_fwd(zc, la, dep, t_len, l_ctx, name):
    rows = zc.shape[0]
    n_x = t_len // _GC
    n_c = n_x + l_ctx // _GC
    qkb, vb = ZC_QK // 512, ZC_V // 512

    def ch_f(c):
        return lax.rem(c + n_x, n_c)

    def ch_r(c):
        return n_c - 1 - c

    def body(qkf_ref, vf_ref, laf_ref, qkr_ref, vr_ref, lar_ref, dep_ref, of_ref, or_ref, spf_ref, spr_ref, stf, strv):
        del dep_ref
        c = pl.program_id(0)

        @pl.when(c == 0)
        def _():
            stf[...] = jnp.zeros_like(stf)
            strv[...] = jnp.zeros_like(strv)

        results = []
        for qk_ref, v_ref, la_ref, st, reverse in ((qkf_ref, vf_ref, laf_ref, stf, False),
                                                   (qkr_ref, vr_ref, lar_ref, strv, True)):
            mask, _, gl, _, _, _, qt, kt, ke = _gla_chunk_terms(qk_ref[...], la_ref[...], reverse)
            vbf = v_ref[...].astype(BF16)
            qtb, keb = qt.astype(BF16), ke.astype(BF16)
            kbd = _block_diag_rows(kt.astype(BF16), 6)
            vbd = _block_diag_rows(vbf, 7)
            mask4 = _chunk_mask4(reverse)
            t_prev = st[...]
            att = jnp.where(mask4, _dot_nt(qtb, kbd), 0.0).astype(BF16)
            o_all = _dot(att, vbd) + _dot_nt(qtb, t_prev.astype(BF16))
            t_new = t_prev * jnp.exp(gl) + jnp.where(_same_head(_ST_SHAPE[0], _ST_SHAPE[1], 7, 6),
                                                     _dot_tn(vbf, keb), 0.0)
            results.append((o_all, t_prev, t_new))
        for (o_all, t_prev, t_new), o_ref, sp_ref, st in zip(results, (of_ref, or_ref), (spf_ref, spr_ref), (stf, strv)):
            o_ref[...] = o_all
            for hh, blk in enumerate(_state_blocks(t_prev)):
                sp_ref[0, hh] = blk
            st[...] = t_new

    st_shape = (B_HEADS, B_DV, B_DK)
    return pl.pallas_call(
        body, name=name, grid=(n_c,),
        in_specs=[pl.BlockSpec((_GC, 512), lambda c: (ch_f(c), qkb)),
                  pl.BlockSpec((_GC, 512), lambda c: (ch_f(c), vb)),
                  pl.BlockSpec((_GC, 512), lambda c: (ch_f(c), 0)),
                  pl.BlockSpec((_GC, 512), lambda c: (ch_r(c), qkb)),
                  pl.BlockSpec((_GC, 512), lambda c: (ch_r(c), vb)),
                  pl.BlockSpec((_GC, 512), lambda c: (ch_r(c), 0)),
                  pl.BlockSpec((8, 128), lambda c: (0, 0))],
        out_specs=[pl.BlockSpec((_GC, 512), lambda c: (ch_f(c), 0)),
                   pl.BlockSpec((_GC, 512), lambda c: (ch_r(c), 0)),
                   pl.BlockSpec((1,) + st_shape, lambda c: (c, 0, 0, 0)),
                   pl.BlockSpec((1,) + st_shape, lambda c: (c, 0, 0, 0))],
        out_shape=[jax.ShapeDtypeStruct((rows, 512), F32), jax.ShapeDtypeStruct((rows, 512), F32),
                   jax.ShapeDtypeStruct((n_c,) + st_shape, F32), jax.ShapeDtypeStruct((n_c,) + st_shape, F32)],
        scratch_shapes=[pltpu.VMEM(_ST_SHAPE, F32), pltpu.VMEM(_ST_SHAPE, F32)],
        compiler_params=_cp("arbitrary"),
    )(zc, zc, la, zc, zc, la, dep)


def _gla_bwd(zc, la, spf, spr, dosum, t_len, l_ctx, name):
    rows = zc.shape[0]
    n_x = t_len // _GC
    n_c = n_x + l_ctx // _GC
    n_all = rows // _GC
    qkb, vb = ZC_QK // 512, ZC_V // 512

    def scan_of(c):
        return jnp.maximum(n_c - 1 - c, 0)

    def ch_f(c):
        return jnp.where(c < n_c, lax.rem(scan_of(c) + n_x, n_c), c)

    def ch_r(c):
        return c

    def do_of(ch):
        return jnp.minimum(ch, n_x - 1)

    def body(qkf_ref, vf_ref, laf_ref, spf_ref, dof_ref, qkr_ref, vr_ref, lar_ref, spr_ref, dor_ref,
             dqkf_ref, dvf_ref, dlaf_ref, dqkr_ref, dvr_ref, dlar_ref, dsf, dsr):
        c = pl.program_id(0)

        @pl.when(c == 0)
        def _():
            dsf[...] = jnp.zeros_like(dsf)
            dsr[...] = jnp.zeros_like(dsr)

        @pl.when(c >= n_c)
        def _():
            for r in (dqkf_ref, dvf_ref, dlaf_ref, dqkr_ref, dvr_ref, dlar_ref):
                r[...] = jnp.zeros_like(r)

        @pl.when(c < n_c)
        def _():
            sc = B_DK ** -0.5
            results = []
            for qk_ref, v_ref, la_ref, sp_ref, do_ref, dst, reverse, ch in (
                    (qkf_ref, vf_ref, laf_ref, spf_ref, dof_ref, dsf, False, ch_f(c)),
                    (qkr_ref, vr_ref, lar_ref, spr_ref, dor_ref, dsr, True, ch_r(c))):
                mask, tri, gl, eg, eng, eend, qt, kt, ke = _gla_chunk_terms(qk_ref[...], la_ref[...], reverse)
                vbf = v_ref[...].astype(BF16)
                dob = jnp.where(ch < n_x, do_ref[...], 0.0).astype(BF16)
                qtb, keb = qt.astype(BF16), ke.astype(BF16)
                kbd = _block_diag_rows(kt.astype(BF16), 6)
                vbd = _block_diag_rows(vbf, 7)
                mask4 = _chunk_mask4(reverse)
                egl = jnp.exp(gl)
                t_prev = _state_from_blocks([sp_ref[0, hh] for hh in range(B_HEADS)])
                dt_new = dst[...]
                tpb, dtb = t_prev.astype(BF16), dt_new.astype(BF16)
                att = jnp.where(mask4, _dot_nt(qtb, kbd), 0.0).astype(BF16)
                datt = jnp.where(mask4, _dot_nt(dob, vbd), 0.0).astype(BF16)
                dqt = _dot(datt, kbd) + _dot(dob, tpb)
                dkt = _fold_heads(jnp.where(_same_head(256, 256, 6, 6), _dot_tn(datt, qtb), 0.0))
                dv = _fold_heads(jnp.where(_same_head(256, 512, 6, 7), _dot_tn(att, dob), 0.0)) + _dot_nt(keb, dtb)
                dke = _dot(vbf, dtb)
                dt_prev = dt_new * egl + jnp.where(_same_head(_ST_SHAPE[0], _ST_SHAPE[1], 7, 6),
                                                   _dot_tn(dob, qtb), 0.0)
                dgl = (jnp.sum(dke * ke, axis=0, keepdims=True)
                       + jnp.sum(dt_new * t_prev, axis=0, keepdims=True) * egl)
                dg_hi, dg_lo = _split_bf16(dqt * qt - dkt * kt - dke * ke)
                dla = _dot_tn(tri, dg_hi) + _dot_tn(tri, dg_lo) + dgl
                dqk = jnp.concatenate([dqt * (sc * eg), dkt * eng + dke * eend], axis=1)
                results.append((dqk, dv, dla, dt_prev))
            for (dqk, dv, dla, dt_prev), dqk_ref, dv_ref, dla_ref, dst in zip(
                    results, (dqkf_ref, dqkr_ref), (dvf_ref, dvr_ref), (dlaf_ref, dlar_ref), (dsf, dsr)):
                dqk_ref[...] = dqk
                dv_ref[...] = dv
                dla_ref[...] = dla
                dst[...] = dt_prev

    st_shape = (B_HEADS, B_DV, B_DK)

    def side(chf):
        return [pl.BlockSpec((_GC, 512), lambda c: (chf(c), qkb)),
                pl.BlockSpec((_GC, 512), lambda c: (chf(c), vb)),
                pl.BlockSpec((_GC, 512), lambda c: (chf(c), 0)),
                pl.BlockSpec((1,) + st_shape, lambda c: (scan_of(c), 0, 0, 0)),
                pl.BlockSpec((_GC, 512), lambda c: (do_of(chf(c)), 0))]

    def out_side(chf):
        return [pl.BlockSpec((_GC, 512), lambda c: (chf(c), 0)),
                pl.BlockSpec((_GC, 512), lambda c: (chf(c), 0)),
                pl.BlockSpec((_GC, 256), lambda c: (chf(c), 0))]

    shp = [jax.ShapeDtypeStruct((rows, 512), F32), jax.ShapeDtypeStruct((rows, 512), F32),
           jax.ShapeDtypeStruct((rows, 256), F32)]
    return pl.pallas_call(
        body, name=name, grid=(n_all,),
        in_specs=side(ch_f) + side(ch_r),
        out_specs=out_side(ch_f) + out_side(ch_r),
        out_shape=shp + shp,
        scratch_shapes=[pltpu.VMEM(_ST_SHAPE, F32), pltpu.VMEM(_ST_SHAPE, F32)],
        compiler_params=_cp("arbitrary"),
    )(zc, zc, la, spf, dosum, zc, zc, la, spr, dosum)


def _gla_out_fwd(o_a, o_f, o_r, zc, gla_g, t_len, name):
    tm = ROW_TILE
    rb = ZC_R // 512

    def body(oa_ref, of_ref, or_ref, r_ref, g_ref, cat_ref):
        osum = of_ref[...] + or_ref[...]
        g = g_ref[...]
        pieces = []
        for hh in range(B_HEADS):
            oh = osum[:, hh * B_DV:(hh + 1) * B_DV]
            rs = lax.rsqrt(jnp.mean(oh * oh, axis=-1, keepdims=True) + RMS_EPS)
            pieces.append((oh * rs) * g)
        r = r_ref[...]
        cat_ref[:, 0:512] = oa_ref[...].astype(BF16)
        cat_ref[:, 512:1024] = (jnp.concatenate(pieces, axis=1) * (r * _sigmoid(r))).astype(BF16)

    return pl.pallas_call(
        body, name=name, grid=(t_len // tm,),
        in_specs=[pl.BlockSpec((tm, 512), lambda i: (i, 0)),
                  pl.BlockSpec((tm, 512), lambda i: (i, 0)),
                  pl.BlockSpec((tm, 512), lambda i: (i, 0)),
                  pl.BlockSpec((tm, 512), lambda i: (i, rb)),
                  pl.BlockSpec((1, B_DV), lambda i: (0, 0))],
        out_specs=pl.BlockSpec((tm, D_MODEL), lambda i: (i, 0)),
        out_shape=jax.ShapeDtypeStruct((t_len, D_MODEL), BF16),
        compiler_params=_cp("parallel"),
    )(o_a, o_f, o_r, zc, gla_g)


def _gla_out_bwd(dcat, o_f, o_r, zc, gla_g, t_len, name):
    tm = ROW_TILE
    rb = ZC_R // 512

    def body(d_ref, of_ref, or_ref, r_ref, g_ref, dos_ref, dr_ref, dg_ref):
        i = pl.program_id(0)
        osum = of_ref[...] + or_ref[...]
        g = g_ref[...]
        r = r_ref[...]
        dgo = d_ref[...]
        sg = _sigmoid(r)
        dnrmg = dgo * (r * sg)
        nrms, dos = [], []
        dg_acc = jnp.zeros((1, B_DV), F32)
        for hh in range(B_HEADS):
            oh = osum[:, hh * B_DV:(hh + 1) * B_DV]
            rs = lax.rsqrt(jnp.mean(oh * oh, axis=-1, keepdims=True) + RMS_EPS)
            nrm = oh * rs
            dn = dnrmg[:, hh * B_DV:(hh + 1) * B_DV]
            dg_acc = dg_acc + jnp.sum(dn * nrm, axis=0, keepdims=True)
            dnn = dn * g
            dos.append(rs * (dnn - nrm * jnp.mean(dnn * nrm, axis=-1, keepdims=True)))
            nrms.append(nrm * g)
        dos_ref[...] = jnp.concatenate(dos, axis=1)
        dr_ref[...] = dgo * jnp.concatenate(nrms, axis=1) * (sg * (1.0 + r * (1.0 - sg)))

        @pl.when(i == 0)
        def _():
            dg_ref[...] = jnp.zeros_like(dg_ref)

        dg_ref[...] += dg_acc

    return pl.pallas_call(
        body, name=name, grid=(t_len // tm,),
        in_specs=[pl.BlockSpec((tm, 512), lambda i: (i, 1)),
                  pl.BlockSpec((tm, 512), lambda i: (i, 0)),
                  pl.BlockSpec((tm, 512), lambda i: (i, 0)),
                  pl.BlockSpec((tm, 512), lambda i: (i, rb)),
                  pl.BlockSpec((1, B_DV), lambda i: (0, 0))],
        out_specs=[pl.BlockSpec((tm, 512), lambda i: (i, 0)),
                   pl.BlockSpec((tm, 512), lambda i: (i, 0)),
                   pl.BlockSpec((1, B_DV), lambda i: (0, 0))],
        out_shape=[jax.ShapeDtypeStruct((t_len, 512), F32), jax.ShapeDtypeStruct((t_len, 512), F32),
                   jax.ShapeDtypeStruct((1, B_DV), F32)],
        compiler_params=_cp("arbitrary"),
    )(dcat, o_f, o_r, zc, gla_g)


def _mix_prep(dq, dkv, dqk_f, dqk_r, dv_f, dv_r, d_r, dla_f, dla_r, zc, wg2, bias2, cs, t_len, name):
    rows = zc.shape[0]
    tm = ROW_TILE
    n_x = t_len // tm
    gb = ZC_G // 128

    def xrow(i):
        return jnp.minimum(i, n_x - 1)

    def body(dq_ref, dkv_ref, dqkf_ref, dqkr_ref, dvf_ref, dvr_ref, dr_ref, dlaf_ref, dlar_ref, zg_ref, wg_ref,
             b_ref, cs_ref, dz_ref, dwg_ref, db_ref):
        i = pl.program_id(0)
        is_x = i < n_x
        cos = cs_ref[:, 0:128]
        sin = cs_ref[:, 128:256]
        cosq = jnp.concatenate([cos] * 4, axis=1)
        sinq = jnp.concatenate([sin] * 4, axis=1)
        dqv = jnp.where(is_x, dq_ref[...], 0.0)
        dz_ref[:, ZC_Q:ZC_QK] = (dqv * cosq + _swap16(dqv * sinq)).astype(BF16)
        dz_ref[:, ZC_QK:ZC_V] = (dqkf_ref[...] + dqkr_ref[...]).astype(BF16)
        dz_ref[:, ZC_V:ZC_R] = (dvf_ref[...] + dvr_ref[...]).astype(BF16)
        dz_ref[:, ZC_R:ZC_KV] = jnp.where(is_x, dr_ref[...], 0.0).astype(BF16)
        dk = dkv_ref[:, 0:128]
        dz_ref[:, ZC_KV:ZC_KV + 128] = (dk * cos + _swap16(dk * sin)).astype(BF16)
        dz_ref[:, ZC_KV + 128:ZC_G] = dkv_ref[:, 128:256].astype(BF16)
        zgb = zg_ref[...].astype(BF16)
        wg = wg_ref[...]
        pre = _dot(zgb, wg) + b_ref[...]
        dla = jnp.concatenate([dlaf_ref[...], dlar_ref[...]], axis=1)
        dpre = dla * (_sigmoid(-pre) / B_GATE_NORM)
        dpb = dpre.astype(BF16)
        dz_ref[:, ZC_G:ZC_W] = _dot_nt(dpb, wg).astype(BF16)

        @pl.when(i == 0)
        def _():
            dwg_ref[...] = jnp.zeros_like(dwg_ref)
            db_ref[...] = jnp.zeros_like(db_ref)

        dwg_ref[...] += _dot_tn(zgb, dpb)
        db_ref[...] += jnp.sum(dpre, axis=0, keepdims=True)

    return pl.pallas_call(
        body, name=name, grid=(rows // tm,),
        in_specs=[pl.BlockSpec((tm, 512), lambda i: (xrow(i), 0)),
                  pl.BlockSpec((tm, 256), lambda i: (i, 0)),
                  pl.BlockSpec((tm, 512), lambda i: (i, 0)),
                  pl.BlockSpec((tm, 512), lambda i: (i, 0)),
                  pl.BlockSpec((tm, 512), lambda i: (i, 0)),
                  pl.BlockSpec((tm, 512), lambda i: (i, 0)),
                  pl.BlockSpec((tm, 512), lambda i: (xrow(i), 0)),
                  pl.BlockSpec((tm, 256), lambda i: (i, 0)),
                  pl.BlockSpec((tm, 256), lambda i: (i, 0)),
                  pl.BlockSpec((tm, 128), lambda i: (i, gb)),
                  pl.BlockSpec((128, 512), lambda i: (0, 0)),
                  pl.BlockSpec((1, 512), lambda i: (0, 0)),
                  pl.BlockSpec((tm, 256), lambda i: (i, 0))],
        out_specs=[pl.BlockSpec((tm, ZC_W), lambda i: (i, 0)),
                   pl.BlockSpec((128, 512), lambda i: (0, 0)),
                   pl.BlockSpec((1, 512), lambda i: (0, 0))],
        out_shape=[jax.ShapeDtypeStruct((rows, ZC_W), BF16),
                   jax.ShapeDtypeStruct((128, 512), F32),
                   jax.ShapeDtypeStruct((1, 512), F32)],
        compiler_params=_cp("arbitrary"),
    )(dq, dkv, dqk_f, dqk_r, dv_f, dv_r, d_r, dla_f, dla_r, zc, wg2, bias2, cs)


def _gate_weights(w_a2_f, b_a_f, w_a2_b, b_a_b):
    wg2 = jnp.zeros((128, 512), F32)
    wg2 = wg2.at[0:B_GATE_RANK, 0:256].set(w_a2_f).at[B_GATE_RANK:2 * B_GATE_RANK, 256:512].set(w_a2_b)
    bias2 = jnp.concatenate([b_a_f, b_a_b]).reshape(1, 512)
    return wg2.astype(BF16), bias2


_WIN_PERM = ((0, 512), (768, 1280), (1280, 1792), (1792, 2304), (512, 768), (2304, 2336))


def _w_in_to_cat(w_in_full):
    parts = [w_in_full[:, a:b] for a, b in _WIN_PERM]
    parts.append(jnp.zeros((w_in_full.shape[0], ZC_W - PROJ_DIM), w_in_full.dtype))
    return jnp.concatenate(parts, axis=1)


def _cat_to_w_in(d_wcat):
    return jnp.concatenate([d_wcat[:, ZC_Q:ZC_QK], d_wcat[:, ZC_KV:ZC_G], d_wcat[:, ZC_QK:ZC_KV],
                            d_wcat[:, ZC_G:ZC_G + 2 * B_GATE_RANK]], axis=1)


def _mixer_ab_forward(x1, g3, mods, wcat, wg2, bias2, sink, gla_g, w_out, cs, t_len, l_ctx, n_x, pace):
    h, zc, la = _proj_fwd(x1, g3, mods, n_x, wcat, wg2, bias2, cs, "mix0_proj")
    dep = pace("proj", zc)
    o_a = _attn_fwd(zc, sink + dep[0, 0], t_len, l_ctx, "mix0_attn")
    dep = pace("attn", o_a)
    o_f, o_r, spf, spr = _gla_fwd(zc, la, dep, t_len, l_ctx, "mix0_gla")
    dep = pace("gla", o_f)
    cat = _gla_out_fwd(o_a, o_f, o_r, zc, gla_g + dep[0:1, 0:1], t_len, "mix0_glaout")
    x2, y = _matmul_resid(cat, w_out, x1, mods, 5, 1.0, n_x, t_len, "mix0_out")
    return x2, (x1, h, zc, la, o_a, o_f, o_r, spf, spr, cat, y)


def _mixer_ab_backward(dx2, saved, g3, mods, wcat, wg2, bias2, sink, gla_g, w_out, cs, t_len, l_ctx, n_x):
    x1, h, zc, la, o_a, o_f, o_r, spf, spr, cat, y = saved
    rows = x1.shape[0]
    tm = ROW_TILE
    dy, dcat, dgate = _gate_dy(dx2, y, mods, 5, 1.0, n_x, t_len, w_out, "mix0_dy")
    tk = _token_tile(t_len)
    d_wout = _matmul_tn(
        cat, dy, pl.BlockSpec((tk, D_MODEL), lambda n, k: (k, 0)), pl.BlockSpec((tk, D_MODEL), lambda n, k: (k, 0)),
        (D_MODEL, D_MODEL), pl.BlockSpec((D_MODEL, D_MODEL), lambda n, k: (0, 0)), (1, t_len // tk), "mix0_dwout")
    dos, d_r, d_glag = _gla_out_bwd(dcat, o_f, o_r, zc, gla_g, t_len, "mix0_dglaout")
    dqk_f, dv_f, dla_f, dqk_r, dv_r, dla_r = _gla_bwd(zc, la, spf, spr, dos, t_len, l_ctx, "mix0_dgla")
    dq, dkv, dsink = _attn_bwd(zc, sink, o_a, dcat, t_len, l_ctx, "mix0_dattn")
    dzc, dwg2, dbias2 = _mix_prep(dq, dkv, dqk_f, dqk_r, dv_f, dv_r, d_r, dla_f, dla_r, zc, wg2, bias2, cs, t_len,
                                  "mix0_prep")
    tk = _token_tile(rows)
    d_wcat = _matmul_tn(
        h, dzc, pl.BlockSpec((tk, D_MODEL), lambda n, k: (k, 0)), pl.BlockSpec((tk, ZC_W), lambda n, k: (k, 0)),
        (D_MODEL, ZC_W), pl.BlockSpec((D_MODEL, ZC_W), lambda n, k: (0, 0)), (1, rows // tk), "mix0_dwin")
    pairs = [(dzc, pl.BlockSpec((tm, ZC_W), lambda i: (i, 0)), wcat, pl.BlockSpec((D_MODEL, ZC_W), lambda i: (0, 0)))]
    dx1, stats = _bwd_dx(pairs, x1, dx2, t_len // tm, g3, mods, 1, n_x, "mix0_dx")
    return dx1, stats, dgate, d_wcat, dwg2, dbias2, dsink, d_glag, d_wout


_PT = 256
_PH = 16


def _pool_window(n, t_len, w, transpose):
    shape = (_PT, _PT + 2 * _PH)
    a = n * _PT + lax.broadcasted_iota(jnp.int32, shape, 0)
    b = n * _PT - _PH + lax.broadcasted_iota(jnp.int32, shape, 1)
    t, s = (b, a) if transpose else (a, b)
    lo = jnp.maximum(t - w // 2, 0)
    hi = jnp.minimum(t + (w - w // 2), t_len)
    inside = (s >= lo) & (s < hi) & (t >= 0) & (t < t_len)
    return jnp.where(inside, 1.0, 0.0).astype(BF16)


def _pool_inv_count(first, count, t_len, w):
    t = first + lax.broadcasted_iota(jnp.int32, (count, 1), 0)
    lo = jnp.maximum(t - w // 2, 0)
    hi = jnp.minimum(t + (w - w // 2), t_len)
    return jnp.where((t >= 0) & (t < t_len), 1.0 / jnp.maximum(hi - lo, 1).astype(F32), 0.0)


def _window_sum(win, vals):
    hi, lo = _split_bf16(vals)
    return _dot(win, hi) + _dot(win, lo)


def _pool_halo(p_ref, c_ref, n_ref):
    return jnp.concatenate([p_ref[_PT - _PH:_PT, :], c_ref[...], n_ref[0:_PH, :]], axis=0)


def _pool_specs(t_len):
    nb = t_len // _PT
    return [pl.BlockSpec((_PT, D_MODEL), lambda n: (jnp.maximum(n - 1, 0), 0)),
            pl.BlockSpec((_PT, D_MODEL), lambda n: (n, 0)),
            pl.BlockSpec((_PT, D_MODEL), lambda n: (jnp.minimum(n + 1, nb - 1), 0))], nb


def _pool_fwd(h, wp, pscale, x1, mods, t_len, name):
    halo_specs, nb = _pool_specs(t_len)

    def body(hp_ref, hc_ref, hn_ref, w_ref, ps_ref, x_ref, m_ref, x2_ref, pooled_ref, ypre_ref):
        n = pl.program_id(0)
        hcat = _pool_halo(hp_ref, hc_ref, hn_ref)
        ys = []
        for gi, w in enumerate(POOL_WINDOWS):
            cols = slice(gi * POOL_GROUP, (gi + 1) * POOL_GROUP)
            hg = hcat[:, cols]
            mean = _window_sum(_pool_window(n, t_len, w, False), hg) * _pool_inv_count(n * _PT, _PT, t_len, w)
            pooled = (mean - hg[_PH:_PH + _PT]).astype(BF16)
            pooled_ref[:, cols] = pooled
            ys.append(_dot(pooled, w_ref[gi]))
        ypre = jnp.concatenate(ys, axis=1)
        ypre_ref[...] = ypre
        x2_ref[...] = x_ref[...] + m_ref[0, 5:6, :] * (ypre * ps_ref[...])

    return pl.pallas_call(
        body, name=name, grid=(nb,),
        in_specs=halo_specs + [pl.BlockSpec((4, POOL_GROUP, POOL_GROUP), lambda n: (0, 0, 0)),
                               pl.BlockSpec((1, D_MODEL), lambda n: (0, 0)),
                               pl.BlockSpec((_PT, D_MODEL), lambda n: (n, 0)),
                               pl.BlockSpec((1, N_MOD, D_MODEL), lambda n: (0, 0, 0))],
        out_specs=[pl.BlockSpec((_PT, D_MODEL), lambda n: (n, 0))] * 3,
        out_shape=[jax.ShapeDtypeStruct((t_len, D_MODEL), F32), jax.ShapeDtypeStruct((t_len, D_MODEL), BF16),
                   jax.ShapeDtypeStruct((t_len, D_MODEL), F32)],
        compiler_params=_cp("parallel"),
    )(h, h, h, wp, pscale, x1, mods)


def _pool_bwd_a(dx2, ypre, wp, pscale, mods, t_len, name):
    nb = t_len // _PT

    def body(d_ref, y_ref, w_ref, ps_ref, m_ref, dyp_ref, dpl_ref, dgate_ref, dps_ref):
        n = pl.program_id(0)
        dv = d_ref[...]
        ypre = y_ref[...]
        ps = ps_ref[...]
        dy = dv * m_ref[0, 5:6, :]
        dyp = (dy * ps).astype(BF16)
        dyp_ref[...] = dyp
        for gi in range(len(POOL_WINDOWS)):
            cols = slice(gi * POOL_GROUP, (gi + 1) * POOL_GROUP)
            dpl_ref[:, cols] = _dot_nt(dyp[:, cols], w_ref[gi])

        @pl.when(n == 0)
        def _():
            dgate_ref[...] = jnp.zeros_like(dgate_ref)
            dps_ref[...] = jnp.zeros_like(dps_ref)

        dgate_ref[...] += jnp.sum(dv * (ypre * ps), axis=0, keepdims=True)
        dps_ref[...] += jnp.sum(dy * ypre, axis=0, keepdims=True)

    return pl.pallas_call(
        body, name=name, grid=(nb,),
        in_specs=[pl.BlockSpec((_PT, D_MODEL), lambda n: (n, 0)),
                  pl.BlockSpec((_PT, D_MODEL), lambda n: (n, 0)),
                  pl.BlockSpec((4, POOL_GROUP, POOL_GROUP), lambda n: (0, 0, 0)),
                  pl.BlockSpec((1, D_MODEL), lambda n: (0, 0)),
                  pl.BlockSpec((1, N_MOD, D_MODEL), lambda n: (0, 0, 0))],
        out_specs=[pl.BlockSpec((_PT, D_MODEL), lambda n: (n, 0)),
                   pl.BlockSpec((_PT, D_MODEL), lambda n: (n, 0)),
                   pl.BlockSpec((1, D_MODEL), lambda n: (0, 0)),
                   pl.BlockSpec((1, D_MODEL), lambda n: (0, 0))],
        out_shape=[jax.ShapeDtypeStruct((t_len, D_MODEL), BF16), jax.ShapeDtypeStruct((t_len, D_MODEL), F32),
                   jax.ShapeDtypeStruct((1, D_MODEL), F32), jax.ShapeDtypeStruct((1, D_MODEL), F32)],
        compiler_params=_cp("arbitrary"),
    )(dx2, ypre, wp, pscale, mods)


def _pool_bwd_dx(dpl, x1, dx2, g3, mods, t_len, name):
    halo_specs, nb = _pool_specs(t_len)

    def body(dp_ref, dc_ref, dn_ref, x_ref, d_ref, g_ref, m_ref, dx_ref, acc_ref):
        n = pl.program_id(0)
        dcat = _pool_halo(dp_ref, dc_ref, dn_ref)
        dhs = []
        for gi, w in enumerate(POOL_WINDOWS):
            cols = slice(gi * POOL_GROUP, (gi + 1) * POOL_GROUP)
            dg = dcat[:, cols]
            scaled = dg * _pool_inv_count(n * _PT - _PH, _PT + 2 * _PH, t_len, w)
            dhs.append(_window_sum(_pool_window(n, t_len, w, True), scaled) - dg[_PH:_PH + _PT])
        dh = jnp.concatenate(dhs, axis=1)
        g = g_ref[1:2, :]
        scale = m_ref[0, 4:5, :]
        dx = _rms_mod_bwd_tail(dh, x_ref[...], g, scale, 0, acc_ref, n == 0)
        dx_ref[...] = d_ref[...] + dx

    return pl.pallas_call(
        body, name=name, grid=(nb,),
        in_specs=halo_specs + [pl.BlockSpec((_PT, D_MODEL), lambda n: (n, 0)),
                               pl.BlockSpec((_PT, D_MODEL), lambda n: (n, 0)),
                               pl.BlockSpec((3, D_MODEL), lambda n: (0, 0)),
                               pl.BlockSpec((1, N_MOD, D_MODEL), lambda n: (0, 0, 0))],
        out_specs=[pl.BlockSpec((_PT, D_MODEL), lambda n: (n, 0)),
                   pl.BlockSpec((2, 3, D_MODEL), lambda n: (0, 0, 0))],
        out_shape=[jax.ShapeDtypeStruct((t_len, D_MODEL), F32), jax.ShapeDtypeStruct((2, 3, D_MODEL), F32)],
        compiler_params=_cp("arbitrary"),
    )(dpl, dpl, dpl, x1, dx2, g3, mods)


def _mixer_pool_forward(x1, g3, mods, wp, pscale, t_len):
    h = _rms_mod_fwd(x1, g3, mods, 1, t_len // ROW_TILE, F32, "mix1_mod")
    x2, pooled, ypre = _pool_fwd(h, wp, pscale, x1, mods, t_len, "mix1_pool")
    return x2, (x1, pooled, ypre)


def _mixer_pool_backward(dx2, saved, g3, mods, wp, pscale, t_len):
    x1, pooled, ypre = saved
    tm = ROW_TILE
    dyp, dpl, dgate, dps = _pool_bwd_a(dx2, ypre, wp, pscale, mods, t_len, "mix1_da")
    d_wp = _matmul_tn(
        pooled, dyp, pl.BlockSpec((tm, POOL_GROUP), lambda g, k: (k, g)),
        pl.BlockSpec((tm, POOL_GROUP), lambda g, k: (k, g)),
        (4, POOL_GROUP, POOL_GROUP), pl.BlockSpec((1, POOL_GROUP, POOL_GROUP), lambda g, k: (g, 0, 0)),
        (4, t_len // tm), "mix1_dwp")
    dx1, stats = _pool_bwd_dx(dpl, x1, dx2, g3, mods, t_len, "mix1_dx")
    return dx1, stats, dgate, dps, d_wp


def _final_loss(x3, final_g, target, name):
    t_len = x3.shape[0]
    tm = ROW_TILE

    def body(x_ref, g_ref, t_ref, dx_ref, loss_ref, dg_ref):
        i = pl.program_id(0)
        xv = x_ref[...]
        g = g_ref[...]
        r = lax.rsqrt(jnp.mean(xv * xv, axis=-1, keepdims=True) + RMS_EPS)
        xhat = xv * r
        err = xhat * g - t_ref[...]
        part = 0.5 * jnp.sum(jnp.mean(err * err, axis=-1, keepdims=True), axis=0, keepdims=True)
        dy = err * (1.0 / D_MODEL)

        @pl.when(i == 0)
        def _():
            loss_ref[...] = jnp.zeros_like(loss_ref)
            dg_ref[...] = jnp.zeros_like(dg_ref)

        loss_ref[...] += jnp.broadcast_to(part, (1, 128))
        dg_ref[...] += jnp.sum(dy * xhat, axis=0, keepdims=True)
        dxh = dy * g
        dx_ref[...] = r * (dxh - xhat * jnp.mean(dxh * xhat, axis=-1, keepdims=True))

    return pl.pallas_call(
        body, name=name, grid=(t_len // tm,),
        in_specs=[pl.BlockSpec((tm, D_MODEL), lambda i: (i, 0)),
                  pl.BlockSpec((1, D_MODEL), lambda i: (0, 0)),
                  pl.BlockSpec((tm, D_MODEL), lambda i: (i, 0))],
        out_specs=[pl.BlockSpec((tm, D_MODEL), lambda i: (i, 0)),
                   pl.BlockSpec((1, 128), lambda i: (0, 0)),
                   pl.BlockSpec((1, D_MODEL), lambda i: (0, 0))],
        out_shape=[jax.ShapeDtypeStruct((t_len, D_MODEL), F32), jax.ShapeDtypeStruct((1, 128), F32),
                   jax.ShapeDtypeStruct((1, D_MODEL), F32)],
        compiler_params=_cp("arbitrary"),
    )(x3, final_g, target)


_CROWS = 16


def _adaln_fwd(c16, w_mod, bias_k, name):
    n_l, _, cols = w_mod.shape

    def body(c_ref, w_ref, b_ref, o_ref):
        cv = c_ref[...]
        sc = (cv * _sigmoid(cv)).astype(BF16)
        o_ref[0] = _dot(sc, w_ref[0].astype(BF16)) + b_ref[0]

    return pl.pallas_call(
        body, name=name, grid=(n_l,),
        in_specs=[pl.BlockSpec((_CROWS, D_MODEL), lambda l: (0, 0)),
                  pl.BlockSpec((1, D_MODEL, cols), lambda l: (l, 0, 0)),
                  pl.BlockSpec((1, 1, cols), lambda l: (l, 0, 0))],
        out_specs=pl.BlockSpec((1, _CROWS, cols), lambda l: (l, 0, 0)),
        out_shape=jax.ShapeDtypeStruct((n_l, _CROWS, cols), F32),
        compiler_params=_cp("parallel"),
    )(c16, w_mod, bias_k)


def _adaln_bwd(c16, d16, w_mod, dmmc_k, name):
    n_l, _, cols = w_mod.shape

    def body(c_ref, d_ref, w_ref, dm_ref, gw_ref, cp_ref):
        layer = pl.program_id(0)
        cv = c_ref[...]
        gw_ref[0] = _dot_tn_hi(cv * _sigmoid(cv), d_ref[0])

        @pl.when(layer == 0)
        def _():
            cp_ref[...] = jnp.sum(w_ref[0] * dm_ref[...], axis=1, keepdims=True)

    return pl.pallas_call(
        body, name=name, grid=(n_l,),
        in_specs=[pl.BlockSpec((_CROWS, D_MODEL), lambda l: (0, 0)),
                  pl.BlockSpec((1, _CROWS, cols), lambda l: (l, 0, 0)),
                  pl.BlockSpec((1, D_MODEL, cols), lambda l: (0, 0, 0)),
                  pl.BlockSpec((1, cols), lambda l: (0, 0))],
        out_specs=[pl.BlockSpec((1, D_MODEL, cols), lambda l: (l, 0, 0)),
                   pl.BlockSpec((D_MODEL, 1), lambda l: (0, 0))],
        out_shape=[jax.ShapeDtypeStruct((n_l, D_MODEL, cols), F32), jax.ShapeDtypeStruct((D_MODEL, 1), F32)],
        compiler_params=_cp("arbitrary"),
    )(c16, d16, w_mod, dmmc_k)


def _cctx_grad(cparts, c_ctx2, name):
    def body(p_ref, c_ref, o_ref):
        tot = ((p_ref[0] + p_ref[2]) + p_ref[4]) + p_ref[6]
        cv = c_ref[...]
        sg = _sigmoid(cv)
        o_ref[...] = tot * (sg * (1.0 + cv * (1.0 - sg)))

    return pl.pallas_call(
        body, name=name, out_shape=jax.ShapeDtypeStruct((8, 128), F32),
        in_specs=[pl.BlockSpec(memory_space=pltpu.VMEM), pl.BlockSpec(memory_space=pltpu.VMEM)],
        out_specs=pl.BlockSpec(memory_space=pltpu.VMEM),
    )(cparts, c_ctx2)


def _sum_devices(ga, name):
    def body(g_ref, o_ref):
        acc = g_ref[0]
        for d in range(1, N_DEV):
            acc = acc + g_ref[d]
        o_ref[...] = acc

    return pl.pallas_call(
        body, name=name, out_shape=jax.ShapeDtypeStruct(ga.shape[1:], F32),
        in_specs=[pl.BlockSpec(memory_space=pltpu.VMEM)], out_specs=pl.BlockSpec(memory_space=pltpu.VMEM),
    )(ga)


def _place():
    return lax.axis_index("x"), lax.axis_index("y"), lax.axis_index("c")


def _flip(a, d):
    return 1 - a if d else a


_CHIP_FLIPS = ((1, 0), (0, 1), (1, 1))


def _allgather_small(v, name, after=()):
    r, cc = v.shape

    def body(v_ref, *rest):
        out_ref, send_sems, recv_sems, local_sem = rest[-4:]
        x, y, c = _place()
        me = 4 * x + 2 * y + c
        mine = pltpu.make_async_copy(v_ref, out_ref.at[me], local_sem)
        mine.start()
        sends = []
        for k in range(1, N_DEV):
            peer = (_flip(x, (k >> 2) & 1), _flip(y, (k >> 1) & 1), _flip(c, k & 1))
            cp = pltpu.make_async_remote_copy(src_ref=v_ref, dst_ref=out_ref.at[me], send_sem=send_sems.at[k - 1],
                                              recv_sem=recv_sems.at[k - 1], device_id=peer, device_id_type=MESH)
            cp.start()
            sends.append(cp)
        for k in range(1, N_DEV):
            px, py, pc = _flip(x, (k >> 2) & 1), _flip(y, (k >> 1) & 1), _flip(c, k & 1)
            pltpu.make_async_remote_copy(src_ref=v_ref, dst_ref=out_ref.at[4 * px + 2 * py + pc],
                                         send_sem=send_sems.at[k - 1], recv_sem=recv_sems.at[k - 1],
                                         device_id=(px, py, pc), device_id_type=MESH).wait_recv()
        for cp in sends:
            cp.wait_send()
        mine.wait()

    return pl.pallas_call(
        body, name=name, out_shape=jax.ShapeDtypeStruct((N_DEV, r, cc), F32),
        in_specs=[pl.BlockSpec(memory_space=pltpu.VMEM)] + [pl.BlockSpec(memory_space=pl.ANY)] * len(after),
        out_specs=pl.BlockSpec(memory_space=pltpu.VMEM),
        scratch_shapes=[pltpu.SemaphoreType.DMA((N_DEV - 1,)), pltpu.SemaphoreType.DMA((N_DEV - 1,)),
                        pltpu.SemaphoreType.DMA],
        compiler_params=pltpu.CompilerParams(vmem_limit_bytes=VMEM_LIMIT_BYTES),
    )(v, *after)


_HBM_SPEC = pl.BlockSpec(memory_space=pltpu.HBM)
_SEM_SPEC = pl.BlockSpec(memory_space=pltpu.SEMAPHORE)
_EFFECT = pltpu.SideEffectType.DATAFLOW_SIDE_EFFECTING


def _in_hbm(a):
    return pltpu.with_memory_space_constraint(a, pltpu.HBM)


def _gather_start(arrs, groups, after, name):
    n, n_g = len(arrs), len(groups)

    def body(*refs):
        ins, zones = refs[:n], refs[n:2 * n]
        sems = refs[2 * n + 1:2 * n + 1 + 2 * n_g]
        token = refs[2 * n + 1 + 2 * n_g + 2 * n]
        x, y, c = _place()
        k_me = 2 * x + y
        for g, members in enumerate(groups):
            for t, a in enumerate(members):
                for j, (dx, dy) in enumerate(_CHIP_FLIPS):
                    pltpu.make_async_remote_copy(
                        src_ref=ins[a], dst_ref=zones[a].at[k_me], send_sem=sems[2 * g].at[3 * t + j],
                        recv_sem=sems[2 * g + 1].at[3 * t + j], device_id=(_flip(x, dx), _flip(y, dy), c),
                        device_id_type=MESH).start()
        token[...] = jnp.zeros_like(token)

    k_own = 2 * lax.axis_index("x") + lax.axis_index("y")
    zones = [lax.dynamic_update_slice(lax.empty((N_CHIPS,) + a.shape, a.dtype), a[None], (k_own,) + (0,) * a.ndim)
             for a in arrs]
    sem_shapes = []
    for members in groups:
        sem_shapes += [pltpu.SemaphoreType.DMA((3 * len(members),))] * 2
    outs = pl.pallas_call(
        body, name=name,
        out_shape=sem_shapes + [pltpu.HBM(a.shape, a.dtype) for a in arrs]
        + [pltpu.HBM(z.shape, z.dtype) for z in zones] + [jax.ShapeDtypeStruct((8, 128), F32)],
        in_specs=[_HBM_SPEC] * (2 * n) + [pl.BlockSpec(memory_space=pl.ANY)],
        out_specs=[_SEM_SPEC] * (2 * n_g) + [_HBM_SPEC] * (2 * n) + [pl.BlockSpec(memory_space=pltpu.VMEM)],
        input_output_aliases={i: 2 * n_g + i for i in range(2 * n)},
        compiler_params=pltpu.CompilerParams(has_side_effects=_EFFECT),
    )(*[_in_hbm(a) for a in arrs], *[_in_hbm(z) for z in zones], after)
    sems = outs[:2 * n_g]
    thru = outs[2 * n_g:2 * n_g + n]
    zones = outs[2 * n_g + n:2 * n_g + 2 * n]
    return [(sems[2 * g], sems[2 * g + 1]) for g in range(n_g)], thru, zones, outs[-1]


def _gather_wait(shards, zones, send_sems, recv_sems, after, name):
    m = len(shards)

    def body(*refs):
        ins, zs = refs[:m], refs[m:2 * m]
        ssem, rsem = refs[2 * m], refs[2 * m + 1]
        x, y, c = _place()
        for t in range(m):
            for j, (dx, dy) in enumerate(_CHIP_FLIPS):
                px, py = _flip(x, dx), _flip(y, dy)
                cp = pltpu.make_async_remote_copy(
                    src_ref=ins[t], dst_ref=zs[t].at[2 * px + py], send_sem=ssem.at[3 * t + j],
                    recv_sem=rsem.at[3 * t + j], device_id=(px, py, c), device_id_type=MESH)
                cp.wait_send()
                cp.wait_recv()

    after = list(after) if isinstance(after, (list, tuple)) else [after]
    outs = pl.pallas_call(
        body, name=name,
        out_shape=[pltpu.HBM(a.shape, a.dtype) for a in list(shards) + list(zones)],
        in_specs=[_HBM_SPEC] * (2 * m) + [_SEM_SPEC, _SEM_SPEC] + [pl.BlockSpec(memory_space=pl.ANY)] * len(after),
        out_specs=[_HBM_SPEC] * (2 * m),
        input_output_aliases={i: i for i in range(2 * m)},
        compiler_params=pltpu.CompilerParams(has_side_effects=_EFFECT),
    )(*shards, *zones, send_sems, recv_sems, *after)
    return outs[m:]


def _scatter_start(arrs, name):
    n = len(arrs)

    def body(*refs):
        ins, lands = refs[:n], refs[n:2 * n]
        ssem, rsem = refs[2 * n], refs[2 * n + 1]
        token = refs[2 * n + 2 + 2 * n]
        x, y, c = _place()
        for a in range(n):
            for j, (dx, dy) in enumerate(_CHIP_FLIPS):
                px, py = _flip(x, dx), _flip(y, dy)
                pltpu.make_async_remote_copy(
                    src_ref=ins[a].at[2 * px + py], dst_ref=lands[a].at[j], send_sem=ssem.at[3 * a + j],
                    recv_sem=rsem.at[3 * a + j], device_id=(px, py, c), device_id_type=MESH).start()
        token[...] = jnp.zeros_like(token)

    lands = [lax.empty((3,) + a.shape[1:], a.dtype) for a in arrs]
    outs = pl.pallas_call(
        body, name=name,
        out_shape=[pltpu.SemaphoreType.DMA((3 * n,))] * 2 + [pltpu.HBM(a.shape, a.dtype) for a in arrs]
        + [pltpu.HBM(z.shape, z.dtype) for z in lands] + [jax.ShapeDtypeStruct((8, 128), F32)],
        in_specs=[_HBM_SPEC] * (2 * n),
        out_specs=[_SEM_SPEC] * 2 + [_HBM_SPEC] * (2 * n) + [pl.BlockSpec(memory_space=pltpu.VMEM)],
        input_output_aliases={i: 2 + i for i in range(2 * n)},
        compiler_params=pltpu.CompilerParams(has_side_effects=_EFFECT),
    )(*[_in_hbm(a) for a in arrs], *[_in_hbm(z) for z in lands])
    return outs[0], outs[1], outs[2:2 + n], outs[2 + n:2 + 2 * n], outs[-1]


def _scatter_wait(arrs, lands, send_sems, recv_sems, after, name):
    n = len(arrs)

    def body(*refs):
        ins, lz = refs[:n], refs[n:2 * n]
        ssem, rsem = refs[2 * n], refs[2 * n + 1]
        x, y, c = _place()
        for a in range(n):
            for j, (dx, dy) in enumerate(_CHIP_FLIPS):
                px, py = _flip(x, dx), _flip(y, dy)
                cp = pltpu.make_async_remote_copy(
                    src_ref=ins[a].at[2 * px + py], dst_ref=lz[a].at[j], send_sem=ssem.at[3 * a + j],
                    recv_sem=rsem.at[3 * a + j], device_id=(px, py, c), device_id_type=MESH)
                cp.wait_send()
                cp.wait_recv()

    outs = pl.pallas_call(
        body, name=name,
        out_shape=[pltpu.HBM(a.shape, a.dtype) for a in list(arrs) + list(lands)],
        in_specs=[_HBM_SPEC] * (2 * n) + [_SEM_SPEC, _SEM_SPEC, pl.BlockSpec(memory_space=pl.ANY)],
        out_specs=[_HBM_SPEC] * (2 * n),
        input_output_aliases={i: i for i in range(2 * n)},
        compiler_params=pltpu.CompilerParams(has_side_effects=_EFFECT),
    )(*arrs, *lands, send_sems, recv_sems, after)
    return outs[:n], outs[n:]


def _swap_start(arrs, name):
    n = len(arrs)

    def body(*refs):
        ins, lands = refs[:n], refs[n:2 * n]
        ssem, rsem = refs[2 * n], refs[2 * n + 1]
        token = refs[2 * n + 2 + 2 * n]
        x, y, c = _place()
        for a in range(n):
            pltpu.make_async_remote_copy(src_ref=ins[a], dst_ref=lands[a], send_sem=ssem.at[a], recv_sem=rsem.at[a],
                                         device_id=(x, y, 1 - c), device_id_type=MESH).start()
        token[...] = jnp.zeros_like(token)

    lands = [lax.empty(a.shape, a.dtype) for a in arrs]
    outs = pl.pallas_call(
        body, name=name,
        out_shape=[pltpu.SemaphoreType.DMA((n,))] * 2 + [pltpu.HBM(a.shape, a.dtype) for a in arrs]
        + [pltpu.HBM(z.shape, z.dtype) for z in lands] + [jax.ShapeDtypeStruct((8, 128), F32)],
        in_specs=[_HBM_SPEC] * (2 * n),
        out_specs=[_SEM_SPEC] * 2 + [_HBM_SPEC] * (2 * n) + [pl.BlockSpec(memory_space=pltpu.VMEM)],
        input_output_aliases={i: 2 + i for i in range(2 * n)},
        compiler_params=pltpu.CompilerParams(has_side_effects=_EFFECT),
    )(*[_in_hbm(a) for a in arrs], *[_in_hbm(z) for z in lands])
    return outs[0], outs[1], outs[2:2 + n], outs[2 + n:2 + 2 * n], outs[-1]


def _swap_wait(arrs, lands, send_sems, recv_sems, after, name):
    n = len(arrs)

    def body(*refs):
        ins, lz = refs[:n], refs[n:2 * n]
        ssem, rsem = refs[2 * n], refs[2 * n + 1]
        x, y, c = _place()
        for a in range(n):
            cp = pltpu.make_async_remote_copy(src_ref=ins[a], dst_ref=lz[a], send_sem=ssem.at[a], recv_sem=rsem.at[a],
                                              device_id=(x, y, 1 - c), device_id_type=MESH)
            cp.wait_send()
            cp.wait_recv()

    outs = pl.pallas_call(
        body, name=name,
        out_shape=[pltpu.HBM(a.shape, a.dtype) for a in list(arrs) + list(lands)],
        in_specs=[_HBM_SPEC] * (2 * n) + [_SEM_SPEC, _SEM_SPEC, pl.BlockSpec(memory_space=pl.ANY)],
        out_specs=[_HBM_SPEC] * (2 * n),
        input_output_aliases={i: i for i in range(2 * n)},
        compiler_params=pltpu.CompilerParams(has_side_effects=_EFFECT),
    )(*arrs, *lands, send_sems, recv_sems, after)
    return outs[:n], outs[n:]


def _row_tile(rows, cols):
    for tr in (1024, 512, 256, 128, 64, 32, 16, 8):
        if rows % tr == 0 and tr * cols * 4 <= (1 << 20):
            return tr
    return rows


def _partial_sum(g_full, recv, k_idx, name):
    _, r, c = g_full.shape
    tr = _row_tile(r, c)

    def body(k_ref, g_ref, r_ref, o_ref):
        del k_ref
        acc = g_ref[0].astype(F32)
        for j in range(3):
            acc = acc + r_ref[j].astype(F32)
        o_ref[...] = acc

    return pl.pallas_call(
        body, name=name,
        grid_spec=pltpu.PrefetchScalarGridSpec(
            num_scalar_prefetch=1, grid=(r // tr,),
            in_specs=[pl.BlockSpec((1, tr, c), lambda i, k: (k[0], i, 0)),
                      pl.BlockSpec((3, tr, c), lambda i, k: (0, i, 0))],
            out_specs=pl.BlockSpec((tr, c), lambda i, k: (i, 0))),
        out_shape=jax.ShapeDtypeStruct((r, c), F32),
        compiler_params=_cp("parallel"),
    )(k_idx, g_full, recv)


def _adamw(w3, parts, m3, v3, layer, prev, name):
    n_l, r, c = w3.shape
    tr = _row_tile(r, c)
    n_i = r // tr
    n_p = len(parts)
    c1 = 1.0 - ADAM_B1 ** ADAM_STEP
    c2 = 1.0 - ADAM_B2 ** ADAM_STEP
    stacked = [isinstance(p, tuple) for p in parts]

    def body(*refs):
        w_ref, m_ref, v_ref = refs[0:3]
        g_refs = refs[3:3 + n_p]
        go_ref, d_ref, mo_ref, vo_ref = refs[-4:]
        g = None
        for p in range(n_p):
            term = g_refs[p][0] if stacked[p] else g_refs[p][...]
            g = term if g is None else g + term
        w = w_ref[0]
        m = ADAM_B1 * m_ref[0] + (1.0 - ADAM_B1) * g
        v = ADAM_B2 * v_ref[0] + (1.0 - ADAM_B2) * (g * g)
        m_hat = m / c1
        v_hat = v / c2
        go_ref[0] = g
        d_ref[0] = -ADAM_LR * (m_hat / (jnp.sqrt(v_hat) + ADAM_EPS) + ADAM_WD * w)
        mo_ref[0] = m
        vo_ref[0] = v

    blk = pl.BlockSpec((1, tr, c), lambda i: (layer, i, 0))
    in_specs = [blk, blk, blk]
    args = [w3, m3, v3]
    for part in parts:
        if isinstance(part, tuple):
            in_specs.append(pl.BlockSpec((1, tr, c), functools.partial(lambda idx, i: (idx, i, 0), part[1])))
            args.append(part[0])
        else:
            in_specs.append(pl.BlockSpec((tr, c), lambda i: (i, 0)))
            args.append(part)
    aliases = {}
    if prev is not None:
        in_specs += [pl.BlockSpec(memory_space=pl.ANY)] * 4
        aliases = {len(args) + q: q for q in range(4)}
        args += list(prev)
    shp = jax.ShapeDtypeStruct((n_l, r, c), F32)
    return pl.pallas_call(
        body, name=name, grid=(n_i,), in_specs=in_specs, out_specs=[blk] * 4, out_shape=[shp] * 4,
        input_output_aliases=aliases, compiler_params=_cp("parallel"),
    )(*args)


_SMALL_W = 4096
_PACK_ROWS = 352
_N9 = N_MOD * D_MODEL


def _flat_pad(parts, total):
    flat = jnp.concatenate([p.reshape(-1) for p in parts])
    return jnp.concatenate([flat, jnp.zeros((total - flat.shape[0],), F32)])


def kernel(x, c, ctx, c_ctx, w_mod, b_mod, norm_g, ffn1_wi, ffn1_wo, ffn2_wi, ffn2_wo, w_in, w_a2_f, b_a_f, w_a2_b, b_a_b, sink, gla_g, w_out, w_pool, pool_scale, final_g, loss_target, m_c_ctx, m_w_mod, m_b_mod, m_norm_g, m_ffn1_wi, m_ffn1_wo, m_ffn2_wi, m_ffn2_wo, m_w_in, m_w_a2_f, m_b_a_f, m_w_a2_b, m_b_a_b, m_sink, m_gla_g, m_w_out, m_w_pool, m_pool_scale, m_final_g, v_c_ctx, v_w_mod, v_b_mod, v_norm_g, v_ffn1_wi, v_ffn1_wo, v_ffn2_wi, v_ffn2_wo, v_w_in, v_w_a2_f, v_b_a_f, v_w_a2_b, v_b_a_b, v_sink, v_gla_g, v_w_out, v_w_pool, v_pool_scale, v_final_g):
    t_len, l_ctx = x.shape[1], ctx.shape[1]
    tm = ROW_TILE
    pad = (-(t_len + l_ctx)) % tm
    rows0 = t_len + l_ctx + pad
    n_x = t_len // tm
    xi, yi, ci = _place()
    k_me = 2 * xi + yi
    me = 4 * xi + 2 * yi + ci
    mod_cols = w_mod.shape[2]
    n_grp = len(POOL_WINDOWS)

    small_w = _flat_pad([norm_g, w_a2_f, w_a2_b, pool_scale], _SMALL_W).reshape(_SMALL_W // 128, 128)
    shards = [ffn1_wi[0], ffn1_wi[1], ffn1_wo[0], ffn1_wo[1], ffn2_wi[0], ffn2_wi[1], ffn2_wo[0], ffn2_wo[1],
              w_in[0], w_out[0], w_pool[0].reshape(n_grp * w_pool.shape[2], POOL_GROUP)]

    send_src = [s.astype(BF16) for s in shards] + [small_w]
    groups = ([11, 0], [2], [8, 9], [4], [6], [1], [3], [10, 5], [7])
    started = {}

    def gather_start(g, after):
        members = groups[g]
        sems, thru, zones, token = _gather_start([send_src[a] for a in members], (tuple(range(len(members))),),
                                                 after, "gather_start_%d" % g)
        started[g] = (sems[0], thru, zones)
        return token

    def gather_wait(g, after):
        (ssem, rsem), thru, zones = started[g]
        return dict(zip(groups[g], _gather_wait(thru, zones, ssem, rsem, after, "gather_wait_%d" % g)))

    c_all = _allgather_small(c.reshape(8, 128), "gather_cond").reshape(N_DEV, D_MODEL)
    tok = gather_start(0, c_all)
    c16 = jnp.concatenate([c_all, c_ctx[None], jnp.zeros((_CROWS - N_DEV - 1, D_MODEL), F32)], axis=0) + tok[0:1, 0:1]
    bias_k = lax.dynamic_slice(b_mod, (0, k_me * mod_cols), (2, mod_cols)).reshape(2, 1, mod_cols)
    mm_k = _adaln_fwd(c16, w_mod, bias_k, "adaln_fwd")
    cs = _rope_tables(t_len, rows0)
    xcat = jnp.concatenate([x[0], ctx[0], jnp.zeros((pad, D_MODEL), F32)], axis=0)
    mm_all = _allgather_small(mm_k.reshape(-1, 128), "gather_mod", (cs, xcat)).reshape(N_DEV, 2, _CROWS, mod_cols)
    mm_full = jnp.concatenate([mm_all[2 * k] for k in range(N_CHIPS)], axis=-1)
    mm_x = lax.dynamic_index_in_dim(mm_full, me, axis=1, keepdims=False)
    mm_c = mm_full[:, N_DEV]
    mods = [jnp.stack([mm_x[l].reshape(N_MOD, D_MODEL), mm_c[l].reshape(N_MOD, D_MODEL)]) for l in range(2)]
    gathered = gather_wait(0, mods[0])
    sw = gathered[11].reshape(N_CHIPS, _SMALL_W)
    ng_n = norm_g.size
    a2_n = w_a2_f.size
    norm_g_full = jnp.concatenate([sw[k, :ng_n].reshape(norm_g.shape) for k in range(N_CHIPS)], axis=-1)
    w_a2_f_full = jnp.concatenate([sw[k, ng_n:ng_n + a2_n].reshape(w_a2_f.shape[1:]) for k in range(N_CHIPS)], axis=-1)
    w_a2_b_full = jnp.concatenate(
        [sw[k, ng_n + a2_n:ng_n + 2 * a2_n].reshape(w_a2_b.shape[1:]) for k in range(N_CHIPS)], axis=-1)
    pscale_full = jnp.concatenate(
        [sw[k, ng_n + 2 * a2_n:ng_n + 2 * a2_n + pool_scale.size] for k in range(N_CHIPS)]).reshape(1, D_MODEL)
    wg2, bias2 = _gate_weights(w_a2_f_full, b_a_f[0], w_a2_b_full, b_a_b[0])
    gla_g2 = gla_g.reshape(1, B_DV)
    final_g2 = final_g.reshape(1, D_MODEL)

    g3 = [norm_g_full[0], norm_g_full[1]]

    w1i, w1o, w2i, w2o = [None, None], [None, None], [None, None], [None, None]
    w1i[0] = gathered[0]
    mods_a = mods[0] + gather_start(1, w1i[0])[0:1, 0:1] + gather_start(2, w1i[0])[0:1, 0:1]
    x1, sv_a1, w1o[0] = _ffn_forward(xcat, g3[0], mods_a, 0, w1i[0],
                                     lambda s: (gather_wait(1, s)[2], gather_start(3, s)), n_x, "l0_ffn1")
    gathered = gather_wait(2, x1)
    w_in_full = jnp.concatenate([gathered[8][k] for k in range(N_CHIPS)], axis=1)
    wcat = _w_in_to_cat(w_in_full)
    w_out_full = gathered[9].reshape(D_MODEL, D_MODEL)
    mods_a = mods[0] + gather_start(4, x1)[0:1, 0:1]
    pace_group = {"proj": 5, "attn": 6, "gla": 7}
    x2, sv_am = _mixer_ab_forward(x1, g3[0], mods_a, wcat, wg2, bias2, sink[0], gla_g2, w_out_full, cs,
                                  t_len, l_ctx, n_x, lambda tag, res_: gather_start(pace_group[tag], res_))
    mods_a = mods[0] + gather_start(8, x2)[0:1, 0:1]
    w2i[0], w2o[0] = gather_wait(3, x2)[4], gather_wait(4, x2)[6]
    x3, sv_a2, _ = _ffn_forward(x2, g3[0], mods_a, 2, w2i[0], lambda s: (w2o[0], None), n_x, "l0_ffn2")
    w1i[1], w1o[1] = gather_wait(5, x3)[1], gather_wait(6, x3)[3]
    x4, sv_b1, _ = _ffn_forward(x3, g3[1], mods[1], 0, w1i[1], lambda s: (w1o[1], None), n_x, "l1_ffn1")
    gathered = gather_wait(7, x4)
    w2i[1] = gathered[5]
    wp_full = gathered[10].reshape(N_CHIPS, n_grp, -1, POOL_GROUP).transpose(1, 0, 2, 3).reshape(
        n_grp, POOL_GROUP, POOL_GROUP)
    x5, sv_bm = _mixer_pool_forward(x4, g3[1], mods[1], wp_full, pscale_full, t_len)
    x6, sv_b2, w2o[1] = _ffn_forward(x5, g3[1], mods[1], 2, w2i[1], lambda s: (gather_wait(8, s)[7], None), n_x,
                                     "l1_ffn2")
    dx6, loss_part, d_final_g = _final_loss(x6, final_g2, loss_target[0], "final_loss")
    loss = lax.psum(loss_part[0, 0], ("x", "y", "c"))

    sent = []

    def sender(weight, layer):
        def send(grad, tag):
            nm = "%s_%s_%d" % (weight, tag, layer)
            ssem, rsem, thru, lands, token = _scatter_start([grad], "scatter_start_" + nm)
            sent.append((nm, weight + "_" + tag if tag else weight, layer, thru, lands, ssem, rsem))
            return token[0:1, 0:1]
        return send

    dx5, st_b2, dg_b2 = _ffn_backward(dx6, sv_b2, g3[1], mods[1], 2, w2i[1], w2o[1], n_x, sender("ffn2", 1),
                                      "l1_ffn2_b")
    dx4, st_bm, dg_bm, d_pscale, d_wp = _mixer_pool_backward(dx5, sv_bm, g3[1], mods[1], wp_full, pscale_full, t_len)
    d_wp4 = d_wp.reshape(n_grp, N_CHIPS, -1, POOL_GROUP).transpose(1, 0, 2, 3).reshape(N_CHIPS, -1, POOL_GROUP)
    mods1 = mods[1] + sender("w_pool", 0)(d_wp4, "")
    dx3, st_b1, dg_b1 = _ffn_backward(dx4, sv_b1, g3[1], mods1, 0, w1i[1], w1o[1], n_x, sender("ffn1", 1),
                                      "l1_ffn1_b")
    dx2, st_a2, dg_a2 = _ffn_backward(dx3, sv_a2, g3[0], mods[0], 2, w2i[0], w2o[0], n_x, sender("ffn2", 0),
                                      "l0_ffn2_b")
    dx1, st_am, dg_am, d_wcat, d_wg2, d_bias2, d_sink, d_glag, d_wout = _mixer_ab_backward(
        dx2, sv_am, g3[0], mods[0], wcat, wg2, bias2, sink[0], gla_g2, w_out_full, cs, t_len, l_ctx, n_x)
    d_w_in4 = _cat_to_w_in(d_wcat).reshape(D_MODEL, N_CHIPS, -1).transpose(1, 0, 2)
    mods0 = mods[0] + sender("w_in", 0)(d_w_in4, "") + sender("w_out", 0)(d_wout.reshape(N_CHIPS, -1, D_MODEL), "")
    dx0, st_a1, dg_a1 = _ffn_backward(dx1, sv_a1, g3[0], mods0, 0, w1i[0], w1o[0], n_x, sender("ffn1", 0),
                                      "l0_ffn1_b", out_tiles=n_x)
    grad_x = dx0[None]

    def as3(a):
        n_l = a.shape[0] if a.ndim == 3 else 1
        return a.reshape(n_l, -1, a.shape[-1])

    res = {}
    big_w = {"ffn1_wi": (ffn1_wi, m_ffn1_wi, v_ffn1_wi), "ffn1_wo": (ffn1_wo, m_ffn1_wo, v_ffn1_wo),
             "ffn2_wi": (ffn2_wi, m_ffn2_wi, v_ffn2_wi), "ffn2_wo": (ffn2_wo, m_ffn2_wo, v_ffn2_wo),
             "w_in": (w_in, m_w_in, v_w_in), "w_out": (w_out, m_w_out, v_w_out), "w_pool": (w_pool, m_w_pool, v_w_pool)}
    k_idx = k_me.reshape(1).astype(jnp.int32)
    chain = dx0
    def finish(swap, after):
        lo, hi, s_sem, r_sem, s_thru, s_lands = swap
        mine, other = _swap_wait(s_thru, s_lands, s_sem, r_sem, after, "swap_wait_%d" % lo)
        last = after
        for (nm, wname, layer, _, _, _, _), p, q in zip(sent[lo:hi], mine, other):
            w, m, v = big_w[wname]
            res[wname] = _adamw(as3(w), [p, q], as3(m), as3(v), layer, res.get(wname),
                                "adamw_%s_%d" % (wname, layer))
            last = res[wname][3]
        return last

    swap = None
    for lo, hi in ((0, 2), (2, 5), (5, 7), (7, 9), (9, 11)):
        partial = []
        for nm, wname, layer, thru, lands, ssem, rsem in sent[lo:hi]:
            mine, recv = _scatter_wait(thru, lands, ssem, rsem, chain, "scatter_wait_" + nm)
            partial.append(_partial_sum(mine[0], recv[0], k_idx, "partial_sum_" + nm))
        s_sem, r_sem, s_thru, s_lands, token = _swap_start(partial, "swap_start_%d" % lo)
        if swap is not None:
            chain = finish(swap, token)
        swap = (lo, hi, s_sem, r_sem, s_thru, s_lands)

    def mod_row(st1, dg1, stm, dgm, st2, dg2, s):
        return jnp.concatenate([st1[s, 0], st1[s, 1], dg1[s, 0], stm[s, 0], stm[s, 1], dgm[s, 0],
                                st2[s, 0], st2[s, 1], dg2[s, 0]])

    dg_bm2 = jnp.concatenate([dg_bm, jnp.zeros_like(dg_bm)], axis=0)[:, None, :]
    d_mm_x0 = mod_row(st_a1, dg_a1, st_am, dg_am, st_a2, dg_a2, 0)
    d_mm_x1 = mod_row(st_b1, dg_b1, st_bm, dg_bm2, st_b2, dg_b2, 0)
    d_mm_c0 = mod_row(st_a1, dg_a1, st_am, dg_am, st_a2, dg_a2, 1)
    d_norm_g = jnp.stack([jnp.stack([st[0, 2] + st[1, 2] for st in (st_a1, st_am, st_a2)]),
                          jnp.stack([st[0, 2] + st[1, 2] for st in (st_b1, st_bm, st_b2)])])
    rk = B_GATE_RANK
    pack = _flat_pad([d_mm_x0, d_mm_x1, d_mm_c0, d_norm_g, d_bias2, d_wg2[0:rk, 0:256], d_wg2[rk:2 * rk, 256:512],
                      d_sink[:, 0], jnp.zeros((120,), F32), d_glag, d_pscale, d_final_g],
                     _PACK_ROWS * 128).reshape(_PACK_ROWS, 128)
    pack = pack + 0.0 * chain[0, 0:1, 0:1]
    pack_all = _allgather_small(pack, "gather_small_grads")
    tot = _sum_devices(pack_all, "sum_small_grads").reshape(-1)
    rows_all = pack_all.reshape(N_DEV, -1)
    o = 3 * _N9
    g_norm_g_full = tot[o:o + 6 * D_MODEL].reshape(2, 3, D_MODEL)
    o += 6 * D_MODEL
    g_bias2 = tot[o:o + 512]
    o += 512
    g_w_a2_f_full = tot[o:o + rk * 256].reshape(rk, 256)
    o += rk * 256
    g_w_a2_b_full = tot[o:o + rk * 256].reshape(rk, 256)
    o += rk * 256
    g_sink = tot[o:o + A_HEADS]
    o += 128
    g_gla_g = tot[o:o + B_DV]
    o += B_DV
    g_pscale_full = tot[o:o + D_MODEL]
    o += D_MODEL
    g_final_g = tot[o:o + D_MODEL]
    d_mmc_tot = tot[2 * _N9:3 * _N9]
    g_b_mod = jnp.stack([tot[0:_N9] + d_mmc_tot, tot[_N9:2 * _N9]])

    zrows = jnp.zeros((_CROWS - N_DEV - 1, _N9), F32)
    d16 = jnp.stack([jnp.concatenate([rows_all[:, 0:_N9], d_mmc_tot[None], zrows], axis=0),
                     jnp.concatenate([rows_all[:, _N9:2 * _N9], jnp.zeros((1, _N9), F32), zrows], axis=0)])
    d16_k = lax.dynamic_slice(d16, (0, 0, k_me * mod_cols), (2, _CROWS, mod_cols))
    dmmc_k = lax.dynamic_slice(d_mmc_tot, (k_me * mod_cols,), (mod_cols,)).reshape(1, mod_cols)
    g_w_mod, c_part = _adaln_bwd(c16, d16_k, w_mod, dmmc_k, "adaln_bwd")
    c_parts = _allgather_small(c_part.reshape(8, 128), "gather_cctx")
    g_c_ctx = _cctx_grad(c_parts, c_ctx.reshape(8, 128), "cctx_grad").reshape(D_MODEL)

    def small(w, g, m, v, shape3, nm):
        return [o_.reshape(w.shape) for o_ in _adamw(w.reshape(shape3), [g.reshape(shape3[1:])],
                                                    m.reshape(shape3), v.reshape(shape3), 0, None, "adamw_" + nm)]

    def own(a, axis, size):
        return lax.dynamic_slice_in_dim(a, k_me * size, size, axis=axis)

    res["c_ctx"] = small(c_ctx, g_c_ctx, m_c_ctx, v_c_ctx, (1, 8, 128), "c_ctx")
    upd = _adamw(w_mod, [(g_w_mod, 1)], m_w_mod, v_w_mod, 1, None, "adamw_w_mod_1")
    res["w_mod"] = _adamw(w_mod, [(g_w_mod, 0)], m_w_mod, v_w_mod, 0, upd, "adamw_w_mod_0")
    res["b_mod"] = small(b_mod, g_b_mod, m_b_mod, v_b_mod, (1, 2, _N9), "b_mod")
    res["norm_g"] = small(norm_g, own(g_norm_g_full, 2, norm_g.shape[2]), m_norm_g, v_norm_g,
                          (1, 6, norm_g.shape[2]), "norm_g")
    res["w_a2_f"] = small(w_a2_f, own(g_w_a2_f_full, 1, w_a2_f.shape[2]), m_w_a2_f, v_w_a2_f,
                          (1, rk, w_a2_f.shape[2]), "w_a2_f")
    res["b_a_f"] = small(b_a_f, g_bias2[0:256], m_b_a_f, v_b_a_f, (1, 1, 256), "b_a_f")
    res["w_a2_b"] = small(w_a2_b, own(g_w_a2_b_full, 1, w_a2_b.shape[2]), m_w_a2_b, v_w_a2_b,
                          (1, rk, w_a2_b.shape[2]), "w_a2_b")
    res["b_a_b"] = small(b_a_b, g_bias2[256:512], m_b_a_b, v_b_a_b, (1, 1, 256), "b_a_b")
    res["sink"] = small(sink, g_sink, m_sink, v_sink, (1, 1, A_HEADS), "sink")
    res["gla_g"] = small(gla_g, g_gla_g, m_gla_g, v_gla_g, (1, 1, B_DV), "gla_g")
    res["pool_scale"] = small(pool_scale, own(g_pscale_full, 0, pool_scale.shape[1]), m_pool_scale, v_pool_scale,
                              (1, 1, pool_scale.shape[1]), "pool_scale")
    res["final_g"] = small(final_g, g_final_g, m_final_g, v_final_g, (1, 8, 128), "final_g")
    finish(swap, res["final_g"][0])
    for wname, (w, _, _) in big_w.items():
        res[wname] = [o_.reshape(w.shape) for o_ in res[wname]]

    names = ["c_ctx", "w_mod", "b_mod", "norm_g", "ffn1_wi", "ffn1_wo", "ffn2_wi", "ffn2_wo", "w_in", "w_a2_f",
             "b_a_f", "w_a2_b", "b_a_b", "sink", "gla_g", "w_out", "w_pool", "pool_scale", "final_g"]
    outs = [loss, grad_x]
    for field in range(4):
        outs += [res[nm][field] for nm in names]
    return tuple(outs)
```

```python
import functools

import jax
import jax.numpy as jnp
from jax import lax
from jax.experimental import pallas as pl
from jax.experimental.pallas import tpu as pltpu

F32 = jnp.float32
BF16 = jnp.bfloat16

D_MODEL = 1024
N_MOD = 9
D_FF = 2816
RMS_EPS = 1e-6
A_HEADS = 8
A_KV_HEADS = 2
A_HEAD_DIM = 64
WINDOW = 128
ROPE_BASE = 10000.0
GRID_W = 64
B_HEADS = 4
B_DK = 64
B_DV = 128
B_GATE_RANK = 16
B_GATE_NORM = 16.0
B_CHUNK = 64
POOL_WINDOWS = (2, 4, 8, 16)
POOL_GROUP = D_MODEL // len(POOL_WINDOWS)
PROJ_DIM = 2336

ADAM_LR = 0.001
ADAM_B1 = 0.9
ADAM_B2 = 0.999
ADAM_EPS = 1e-08
ADAM_WD = 0.01
ADAM_STEP = 10

N_CHIPS = 4
N_DEV = 8
ROW_TILE = 512
VMEM_LIMIT_BYTES = 56 * 1024 * 1024
MESH = pl.DeviceIdType.MESH

ZC_Q, ZC_QK, ZC_V, ZC_R, ZC_KV, ZC_G, ZC_W = 0, 512, 1024, 1536, 2048, 2304, 2432


def _cp(*sem):
    return pltpu.CompilerParams(dimension_semantics=sem if sem else None, vmem_limit_bytes=VMEM_LIMIT_BYTES)


def _dot(a, b):
    return jnp.dot(a, b, preferred_element_type=F32)


def _dot_nt(a, b):
    return lax.dot_general(a, b, (((1,), (1,)), ((), ())), preferred_element_type=F32)


def _dot_tn(a, b):
    return lax.dot_general(a, b, (((0,), (0,)), ((), ())), preferred_element_type=F32)


def _dot_tn_hi(a, b):
    return lax.dot_general(a, b, (((0,), (0,)), ((), ())), preferred_element_type=F32,
                           precision=lax.Precision.HIGHEST)


def _sigmoid(x):
    return 1.0 / (1.0 + jnp.exp(-x))


MXU_COLS = 256


def _col_chunks(n):
    return [(c0, min(MXU_COLS, n - c0)) for c0 in range(0, n, MXU_COLS)]


WIDE_ROW_TILE = 1024


def _matmul_row_tile(rows, n_x):
    if rows % WIDE_ROW_TILE == 0 and n_x * ROW_TILE >= rows:
        return WIDE_ROW_TILE
    return ROW_TILE


def _resident(block_shape, index_map):
    return pl.BlockSpec(block_shape, index_map, pipeline_mode=pl.Buffered(1))


def _stream_of(i, n_x):
    return jnp.where(i >= n_x, 1, 0)


def _rms_mod_fwd(x, g3, mods, j, n_x, out_dtype, name):
    rows = x.shape[0]
    tm = ROW_TILE
    n_i = rows // tm

    def body(x_ref, g_ref, m_ref, o_ref):
        xv = x_ref[...]
        r = lax.rsqrt(jnp.mean(xv * xv, axis=-1, keepdims=True) + RMS_EPS)
        g = g_ref[j:j + 1, :]
        shift = m_ref[0, 3 * j:3 * j + 1, :]
        scale = m_ref[0, 3 * j + 1:3 * j + 2, :]
        o_ref[...] = (((xv * r) * g) * (1.0 + scale) + shift).astype(out_dtype)

    return pl.pallas_call(
        body, name=name, grid=(n_i,),
        in_specs=[pl.BlockSpec((tm, D_MODEL), lambda i: (i, 0)),
                  pl.BlockSpec((3, D_MODEL), lambda i: (0, 0)),
                  pl.BlockSpec((1, N_MOD, D_MODEL), lambda i: (_stream_of(i, n_x), 0, 0))],
        out_specs=pl.BlockSpec((tm, D_MODEL), lambda i: (i, 0)),
        out_shape=jax.ShapeDtypeStruct((rows, D_MODEL), out_dtype),
        compiler_params=_cp("parallel"),
    )(x, g3, mods)


def _rms_mod_bwd_tail(dh, xv, g, scale, stream, acc_ref, first):
    r = lax.rsqrt(jnp.mean(xv * xv, axis=-1, keepdims=True) + RMS_EPS)
    xhat = xv * r
    t1 = jnp.sum(dh, axis=0, keepdims=True)
    t2 = jnp.sum(dh * xhat, axis=0, keepdims=True)
    stats = jnp.concatenate([t1, t2 * g, t2 * (1.0 + scale)], axis=0)

    @pl.when(first)
    def _():
        acc_ref[...] = jnp.zeros_like(acc_ref)

    acc_ref[pl.ds(stream, 1)] += stats[None]
    dxh = dh * (g * (1.0 + scale))
    return r * (dxh - xhat * jnp.mean(dxh * xhat, axis=-1, keepdims=True))


def _ffn_up(x, g3, mods, jmod, n_x, w4, name):
    rows = x.shape[0]
    h = w4.shape[2]
    tm = _matmul_row_tile(rows, n_x)
    n_i = rows // tm

    def body(x_ref, g_ref, m_ref, wa_ref, wu_ref, hn_ref, au_ref, s_ref):
        xv = x_ref[...]
        r = lax.rsqrt(jnp.mean(xv * xv, axis=-1, keepdims=True) + RMS_EPS)
        g = g_ref[jmod:jmod + 1, :]
        shift = m_ref[0, 3 * jmod:3 * jmod + 1, :]
        scale = m_ref[0, 3 * jmod + 1:3 * jmod + 2, :]
        hv = (((xv * r) * g) * (1.0 + scale) + shift).astype(BF16)

        @pl.when(pl.program_id(0) == 0)
        def _():
            hn_ref[...] = hv

        for c0, cw in _col_chunks(h):
            cols = slice(c0, c0 + cw)
            a = _dot(hv, wa_ref[0, :, cols])
            u = _dot(hv, wu_ref[0, :, cols])
            sg = _sigmoid(a)
            silu = a * sg
            au_ref[0, :, cols] = (u * (sg * (1.0 + a * (1.0 - sg)))).astype(BF16)
            au_ref[1, :, cols] = silu.astype(BF16)
            s_ref[:, cols] = (silu * u).astype(BF16)

    return pl.pallas_call(
        body, name=name, grid=(2, n_i),
        in_specs=[pl.BlockSpec((tm, D_MODEL), lambda j, i: (i, 0)),
                  pl.BlockSpec((3, D_MODEL), lambda j, i: (0, 0)),
                  pl.BlockSpec((1, N_MOD, D_MODEL), lambda j, i: (_stream_of(i, n_x), 0, 0)),
                  pl.BlockSpec((1, D_MODEL, h), lambda j, i: (j, 0, 0)),
                  pl.BlockSpec((1, D_MODEL, h), lambda j, i: (j + 2, 0, 0))],
        out_specs=[pl.BlockSpec((tm, D_MODEL), lambda j, i: (jnp.where(j == 0, i, n_i - 1), 0)),
                   pl.BlockSpec((2, tm, h), lambda j, i: (0, i, j)),
                   pl.BlockSpec((tm, h), lambda j, i: (i, j))],
        out_shape=[jax.ShapeDtypeStruct((rows, D_MODEL), BF16),
                   jax.ShapeDtypeStruct((2, rows, 2 * h), BF16),
                   jax.ShapeDtypeStruct((rows, 2 * h), BF16)],
        compiler_params=_cp("arbitrary", "arbitrary"),
    )(x, g3, mods, w4, w4)


def _matmul_resid(a, w, xres, mods, gate_idx, coef, n_x, rows, name):
    k = a.shape[1]
    tm = _matmul_row_tile(rows, n_x)
    n_i = rows // tm

    def body(a_ref, w_ref, x_ref, m_ref, o_ref, f_ref):
        av = a_ref[...]
        for c0, cw in _col_chunks(D_MODEL):
            cols = slice(c0, c0 + cw)
            f = _dot(av, w_ref[:, cols])
            f_ref[:, cols] = f.astype(BF16)
            o_ref[:, cols] = x_ref[:, cols] + (coef * m_ref[0, gate_idx:gate_idx + 1, cols]) * f

    return pl.pallas_call(
        body, name=name, grid=(n_i,),
        in_specs=[pl.BlockSpec((tm, k), lambda i: (i, 0)),
                  _resident((k, D_MODEL), lambda i: (0, 0)),
                  pl.BlockSpec((tm, D_MODEL), lambda i: (i, 0)),
                  pl.BlockSpec((1, N_MOD, D_MODEL), lambda i: (_stream_of(i, n_x), 0, 0))],
        out_specs=[pl.BlockSpec((tm, D_MODEL), lambda i: (i, 0)),
                   pl.BlockSpec((tm, D_MODEL), lambda i: (i, 0))],
        out_shape=[jax.ShapeDtypeStruct((rows, D_MODEL), F32),
                   jax.ShapeDtypeStruct((rows, D_MODEL), BF16)],
        compiler_params=_cp("parallel"),
    )(a, w, xres, mods)


def _gate_dy(dout, f, mods, gate_idx, coef, n_x, rows, w, name):
    tm = ROW_TILE
    n_i = rows // tm
    n_out = w.shape[0]

    def body(d_ref, f_ref, m_ref, w_ref, dy_ref, da_ref, acc_ref):
        i = pl.program_id(0)
        dv = d_ref[...]
        gate = m_ref[0, gate_idx:gate_idx + 1, :]
        dyb = (dv * (coef * gate)).astype(BF16)
        dy_ref[...] = dyb
        da_ref[...] = _dot_nt(dyb, w_ref[...])

        @pl.when(i == 0)
        def _():
            acc_ref[...] = jnp.zeros_like(acc_ref)

        part = coef * jnp.sum(dv * f_ref[...].astype(F32), axis=0, keepdims=True)
        acc_ref[pl.ds(_stream_of(i, n_x), 1)] += part[None]

    return pl.pallas_call(
        body, name=name, grid=(n_i,),
        in_specs=[pl.BlockSpec((tm, D_MODEL), lambda i: (i, 0)),
                  pl.BlockSpec((tm, D_MODEL), lambda i: (i, 0)),
                  pl.BlockSpec((1, N_MOD, D_MODEL), lambda i: (_stream_of(i, n_x), 0, 0)),
                  pl.BlockSpec((n_out, D_MODEL), lambda i: (0, 0))],
        out_specs=[pl.BlockSpec((tm, D_MODEL), lambda i: (i, 0)),
                   pl.BlockSpec((tm, n_out), lambda i: (i, 0)),
                   pl.BlockSpec((2, 1, D_MODEL), lambda i: (0, 0, 0))],
        out_shape=[jax.ShapeDtypeStruct((rows, D_MODEL), BF16),
                   jax.ShapeDtypeStruct((rows, n_out), F32),
                   jax.ShapeDtypeStruct((2, 1, D_MODEL), F32)],
        compiler_params=_cp("arbitrary"),
    )(dout, f, mods, w)


def _ffn_bwd_dz(dout, f, mods, gate_idx, coef, n_x, wo2, au, name):
    rows = dout.shape[0]
    h = wo2.shape[1]
    tm = ROW_TILE
    n_i = rows // tm

    def body(d_ref, f_ref, m_ref, wo_ref, au_ref, dy_ref, dz_ref, acc_ref):
        j, i = pl.program_id(0), pl.program_id(1)
        dv = d_ref[...]
        gate = m_ref[0, gate_idx:gate_idx + 1, :]
        dyb = (dv * (coef * gate)).astype(BF16)

        @pl.when((j == 0) & (i == 0))
        def _():
            acc_ref[...] = jnp.zeros_like(acc_ref)

        @pl.when(j == 0)
        def _():
            dy_ref[...] = dyb
            part = coef * jnp.sum(dv * f_ref[...].astype(F32), axis=0, keepdims=True)
            acc_ref[pl.ds(_stream_of(i, n_x), 1)] += part[None]

        for c0, cw in _col_chunks(h):
            cols = slice(c0, c0 + cw)
            ds = _dot_nt(dyb, wo_ref[0, cols, :])
            dz_ref[0, :, cols] = (ds * au_ref[0, :, cols].astype(F32)).astype(BF16)
            dz_ref[1, :, cols] = (ds * au_ref[1, :, cols].astype(F32)).astype(BF16)

    return pl.pallas_call(
        body, name=name, grid=(2, n_i),
        in_specs=[pl.BlockSpec((tm, D_MODEL), lambda j, i: (i, 0)),
                  pl.BlockSpec((tm, D_MODEL), lambda j, i: (jnp.where(j == 0, i, n_i - 1), 0)),
                  pl.BlockSpec((1, N_MOD, D_MODEL), lambda j, i: (_stream_of(i, n_x), 0, 0)),
                  pl.BlockSpec((1, h, D_MODEL), lambda j, i: (j, 0, 0)),
                  pl.BlockSpec((2, tm, h), lambda j, i: (0, i, j))],
        out_specs=[pl.BlockSpec((tm, D_MODEL), lambda j, i: (jnp.where(j == 0, i, n_i - 1), 0)),
                   pl.BlockSpec((2, tm, h), lambda j, i: (0, i, j)),
                   pl.BlockSpec((2, 1, D_MODEL), lambda j, i: (0, 0, 0))],
        out_shape=[jax.ShapeDtypeStruct((rows, D_MODEL), BF16),
                   jax.ShapeDtypeStruct((2, rows, 2 * h), BF16),
                   jax.ShapeDtypeStruct((2, 1, D_MODEL), F32)],
        compiler_params=_cp("arbitrary", "arbitrary"),
    )(dout, f, mods, wo2, au)


def _token_tile(rows):
    for tk in (2048, 1536, 1024):
        if rows % tk == 0:
            return tk
    return ROW_TILE


def _matmul_tn(a, b, a_spec, b_spec, out_shape, out_spec, grid, name):
    nd_a = len(a_spec.block_shape)
    nd_b = len(b_spec.block_shape)
    nd_o = len(out_spec.block_shape)
    k_axis = len(grid) - 1
    n_k = grid[k_axis]

    def body(a_ref, b_ref, o_ref, acc_ref):
        av = a_ref[(0,) * (nd_a - 2)]
        bv = b_ref[(0,) * (nd_b - 2)]
        part = _dot_tn(av, bv)
        k = pl.program_id(k_axis)

        @pl.when(k == 0)
        def _():
            acc_ref[...] = part

        @pl.when(k > 0)
        def _():
            acc_ref[...] += part

        @pl.when(k == n_k - 1)
        def _():
            o_ref[(0,) * (nd_o - 2)] = acc_ref[...].astype(BF16)

    return pl.pallas_call(
        body, name=name, grid=grid, in_specs=[a_spec, b_spec], out_specs=out_spec,
        out_shape=jax.ShapeDtypeStruct(out_shape, BF16),
        scratch_shapes=[pltpu.VMEM(tuple(out_spec.block_shape[-2:]), F32)],
        compiler_params=_cp(*(("arbitrary",) * len(grid))),
    )(a, b)


def _bwd_dx(pairs, x, dres, dres_tiles, g3, mods, j, n_x, name, out_tiles=None):
    rows = x.shape[0]
    tm = ROW_TILE
    n_i = rows // tm
    n_o = n_i if out_tiles is None else out_tiles
    n_p = len(pairs)
    nds = [(len(p[1].block_shape), len(p[3].block_shape)) for p in pairs]

    def body(*refs):
        dz_refs = refs[0:2 * n_p:2]
        w_refs = refs[1:2 * n_p:2]
        x_ref, dres_ref, g_ref, m_ref, dx_ref, acc_ref = refs[2 * n_p:]
        i = pl.program_id(0)
        dzs = [dz_refs[p][(0,) * (nds[p][0] - 2)] for p in range(n_p)]
        pieces = []
        for c0, cw in _col_chunks(D_MODEL):
            acc = None
            for p in range(n_p):
                lead = (0,) * (nds[p][1] - 2)
                part = _dot_nt(dzs[p], w_refs[p][lead + (slice(c0, c0 + cw), slice(None))])
                acc = part if acc is None else acc + part
            pieces.append(acc)
        dh = jnp.concatenate(pieces, axis=1)
        g = g_ref[j:j + 1, :]
        scale = m_ref[0, 3 * j + 1:3 * j + 2, :]
        dx = _rms_mod_bwd_tail(dh, x_ref[...], g, scale, _stream_of(i, n_x), acc_ref, i == 0)
        dres_v = jnp.where(i < dres_tiles, dres_ref[...], 0.0)

        @pl.when(i < n_o)
        def _():
            dx_ref[...] = dres_v + dx

    in_specs, args = [], []
    for dz, dz_spec, w, w_spec in pairs:
        in_specs += [dz_spec, w_spec]
        args += [dz, w]
    in_specs += [pl.BlockSpec((tm, D_MODEL), lambda i: (i, 0)),
                 pl.BlockSpec((tm, D_MODEL), lambda i: (jnp.minimum(i, dres_tiles - 1), 0)),
                 pl.BlockSpec((3, D_MODEL), lambda i: (0, 0)),
                 pl.BlockSpec((1, N_MOD, D_MODEL), lambda i: (_stream_of(i, n_x), 0, 0))]
    args += [x, dres, g3, mods]
    return pl.pallas_call(
        body, name=name, grid=(n_i,), in_specs=in_specs,
        out_specs=[pl.BlockSpec((tm, D_MODEL), lambda i: (jnp.minimum(i, n_o - 1), 0)),
                   pl.BlockSpec((2, 3, D_MODEL), lambda i: (0, 0, 0))],
        out_shape=[jax.ShapeDtypeStruct((n_o * tm, D_MODEL), F32),
                   jax.ShapeDtypeStruct((2, 3, D_MODEL), F32)],
        compiler_params=_cp("arbitrary"),
    )(*args)


def _ffn_forward(x, g3, mods, j, w4_in, w4_out_of, n_x, name):
    rows = x.shape[0]
    hn, au, s = _ffn_up(x, g3, mods, j, n_x, w4_in, name + "_up")
    w4_out, dep = w4_out_of(s)
    if dep is not None:
        mods = mods + dep[0:1, 0:1]
    wo = w4_out.reshape(D_FF, D_MODEL)
    out, f = _matmul_resid(s, wo, x, mods, 3 * j + 2, 0.5, n_x, rows, name + "_down")
    return out, (x, hn, au, s, f), w4_out


def _ffn_backward(dout, saved, g3, mods, j, w4_in, w4_out, n_x, send, name, out_tiles=None):
    x, hn, au, s, f = saved
    rows = x.shape[0]
    tm = ROW_TILE
    n_i = rows // tm
    h = w4_in.shape[2]
    wo2 = w4_out.reshape(2, h, D_MODEL)
    dy, dz, dgate = _ffn_bwd_dz(dout, f, mods, 3 * j + 2, 0.5, n_x, wo2, au, name + "_dz")
    tk = _token_tile(rows)
    n_k = rows // tk
    d_wi = _matmul_tn(
        hn, dz, pl.BlockSpec((tk, D_MODEL), lambda q, k: (k, 0)),
        pl.BlockSpec((1, tk, h), lambda q, k: (q // 2, k, q % 2)),
        (4, D_MODEL, h), pl.BlockSpec((1, D_MODEL, h), lambda q, k: (q, 0, 0)), (4, n_k), name + "_dwi")
    mods = mods + send(d_wi, "wi")
    d_wo = _matmul_tn(
        s, dy, pl.BlockSpec((tk, h), lambda n, k: (k, n)), pl.BlockSpec((tk, D_MODEL), lambda n, k: (k, 0)),
        (D_FF, D_MODEL), pl.BlockSpec((h, D_MODEL), lambda n, k: (n, 0)), (2, n_k), name + "_dwo")
    mods = mods + send(d_wo.reshape(w4_out.shape), "wo")
    pairs = [(dz, pl.BlockSpec((1, tm, h), functools.partial(lambda q, i: (q // 2, i, q % 2), q)),
              w4_in, pl.BlockSpec((1, D_MODEL, h), functools.partial(lambda q, i: (q, 0, 0), q)))
             for q in range(4)]
    dx, stats = _bwd_dx(pairs, x, dout, n_i, g3, mods, j, n_x, name + "_dx", out_tiles)
    return dx, stats, dgate


def _rope_tables(t_len, rows):
    n = A_HEAD_DIM // 4
    freqs = ROPE_BASE ** (-jnp.arange(n, dtype=F32) / n)
    t = jnp.arange(t_len)
    ang_r = (t // GRID_W).astype(F32)[:, None] * freqs
    ang_c = (t % GRID_W).astype(F32)[:, None] * freqs
    cos = jnp.concatenate([jnp.cos(ang_r), jnp.cos(ang_r), jnp.cos(ang_c), jnp.cos(ang_c)], axis=1)
    sin = jnp.concatenate([-jnp.sin(ang_r), jnp.sin(ang_r), -jnp.sin(ang_c), jnp.sin(ang_c)], axis=1)
    cos = jnp.concatenate([cos, jnp.ones((rows - t_len, A_HEAD_DIM), F32)], axis=0)
    sin = jnp.concatenate([sin, jnp.zeros((rows - t_len, A_HEAD_DIM), F32)], axis=0)
    return jnp.concatenate([cos, cos, sin, sin], axis=1)


def _swap16(x):
    n = x.shape[1]
    lane = lax.broadcasted_iota(jnp.int32, x.shape, 1)
    first = jnp.bitwise_and(lane, 16) == 0
    return jnp.where(first, pltpu.roll(x, n - 16, 1), pltpu.roll(x, 16, 1))


def _log_sigmoid(x):
    return jnp.minimum(x, 0.0) - jnp.log(1.0 + jnp.exp(-jnp.abs(x)))


def _proj_fwd(x, g3, mods, n_x, wcat, wg2, bias2, cs, name):
    rows = x.shape[0]
    tm = ROW_TILE

    def body(x_ref, g_ref, m_ref, w_ref, wg_ref, b_ref, cs_ref, h_ref, zc_ref, la_ref):
        xv = x_ref[...]
        r = lax.rsqrt(jnp.mean(xv * xv, axis=-1, keepdims=True) + RMS_EPS)
        hv = (((xv * r) * g_ref[1:2, :]) * (1.0 + m_ref[0, 4:5, :]) + m_ref[0, 3:4, :]).astype(BF16)
        h_ref[...] = hv
        z = _dot(hv, w_ref[...])
        cos = cs_ref[:, 0:128]
        sin = cs_ref[:, 128:256]
        cosq = jnp.concatenate([cos] * 4, axis=1)
        sinq = jnp.concatenate([sin] * 4, axis=1)
        q = z[:, ZC_Q:ZC_QK]
        zc_ref[:, ZC_Q:ZC_QK] = q * cosq + _swap16(q) * sinq
        zc_ref[:, ZC_QK:ZC_KV] = z[:, ZC_QK:ZC_KV]
        kk = z[:, ZC_KV:ZC_KV + 128]
        zc_ref[:, ZC_KV:ZC_KV + 128] = kk * cos + _swap16(kk) * sin
        zc_ref[:, ZC_KV + 128:ZC_W] = z[:, ZC_KV + 128:ZC_W]
        zg = z[:, ZC_G:ZC_W]
        pre = _dot(zg.astype(BF16), wg_ref[...]) + b_ref[...]
        la_ref[...] = _log_sigmoid(pre) / B_GATE_NORM

    return pl.pallas_call(
        body, name=name, grid=(rows // tm,),
        in_specs=[pl.BlockSpec((tm, D_MODEL), lambda i: (i, 0)),
                  pl.BlockSpec((3, D_MODEL), lambda i: (0, 0)),
                  pl.BlockSpec((1, N_MOD, D_MODEL), lambda i: (_stream_of(i, n_x), 0, 0)),
                  pl.BlockSpec((D_MODEL, ZC_W), lambda i: (0, 0)),
                  pl.BlockSpec((128, 512), lambda i: (0, 0)),
                  pl.BlockSpec((1, 512), lambda i: (0, 0)),
                  pl.BlockSpec((tm, 256), lambda i: (i, 0))],
        out_specs=[pl.BlockSpec((tm, D_MODEL), lambda i: (i, 0)),
                   pl.BlockSpec((tm, ZC_W), lambda i: (i, 0)),
                   pl.BlockSpec((tm, 512), lambda i: (i, 0))],
        out_shape=[jax.ShapeDtypeStruct((rows, D_MODEL), BF16),
                   jax.ShapeDtypeStruct((rows, ZC_W), F32),
                   jax.ShapeDtypeStruct((rows, 512), F32)],
        compiler_params=_cp("parallel"),
    )(x, g3, mods, wcat, wg2, bias2, cs)


_QB = WINDOW


def _attn_specs(t_len, l_ctx):
    nb = t_len // _QB
    kvb = ZC_KV // 256
    return [pl.BlockSpec(memory_space=pltpu.SMEM),
            pl.BlockSpec((_QB, 512), lambda n: (n, 0)),
            pl.BlockSpec((_QB, 256), lambda n: (jnp.maximum(n - 1, 0), kvb)),
            pl.BlockSpec((_QB, 256), lambda n: (n, kvb)),
            pl.BlockSpec((_QB, 256), lambda n: (n + 1, kvb)),
            pl.BlockSpec((l_ctx, 256), lambda n: (t_len // l_ctx, kvb))], nb


_HEAD_PAIRS = ((0, 1), (2, 3))


def _attn_keys(kp, kc, kn, kx, g):
    hd = A_HEAD_DIM
    ks = slice(g * hd, (g + 1) * hd)
    vs = slice(128 + g * hd, 128 + (g + 1) * hd)
    kb = jnp.concatenate([kp[:, ks], kc[:, ks], kn[:, ks]], axis=0).astype(BF16)
    vb = jnp.concatenate([kp[:, vs], kc[:, vs], kn[:, vs]], axis=0).astype(BF16)
    return kb, vb, kx[:, ks].astype(BF16), kx[:, vs].astype(BF16)


def _attn_probs(n, t_len, sink_ref, qv, kb, kxb, g, rs):
    hd = A_HEAD_DIM
    qg = jnp.concatenate([qv[:, (4 * g + r) * hd:(4 * g + r + 1) * hd] for r in rs], axis=0).astype(BF16)
    qi = lax.broadcasted_iota(jnp.int32, (_QB, 3 * _QB), 0)
    kj = lax.broadcasted_iota(jnp.int32, (_QB, 3 * _QB), 1)
    kpos = n * _QB - _QB + kj
    valid = (kpos >= 0) & (kpos < t_len) & (jnp.abs(kj - _QB - qi) <= WINDOW)
    valid = jnp.concatenate([valid] * len(rs), axis=0)
    scale = hd ** -0.5
    s = jnp.where(valid, _dot_nt(qg, kb) * scale, -jnp.inf)
    sc = _dot_nt(qg, kxb) * scale
    sk = jnp.concatenate([jnp.full((_QB, 1), sink_ref[4 * g + r], F32) for r in rs], axis=0)
    m = jnp.maximum(jnp.maximum(jnp.max(s, axis=-1, keepdims=True), jnp.max(sc, axis=-1, keepdims=True)), sk)
    p = jnp.exp(s - m)
    pc = jnp.exp(sc - m)
    ps = jnp.exp(sk - m)
    inv = 1.0 / (jnp.sum(p, axis=-1, keepdims=True) + jnp.sum(pc, axis=-1, keepdims=True) + ps)
    return p, pc, ps, inv, qg


def _attn_fwd(zc, sink, t_len, l_ctx, name):
    in_specs, nb = _attn_specs(t_len, l_ctx)

    def body(sink_ref, q_ref, kp_ref, kc_ref, kn_ref, kx_ref, o_ref):
        n = pl.program_id(0)
        qv = q_ref[...]
        outs = []
        for g in range(A_KV_HEADS):
            kb, vb, kxb, vxb = _attn_keys(kp_ref[...], kc_ref[...], kn_ref[...], kx_ref[...], g)
            for rs in _HEAD_PAIRS:
                p, pc, _, inv, _ = _attn_probs(n, t_len, sink_ref, qv, kb, kxb, g, rs)
                o = (_dot(p.astype(BF16), vb) + _dot(pc.astype(BF16), vxb)) * inv
                outs += [o[i * _QB:(i + 1) * _QB] for i in range(len(rs))]
        o_ref[...] = jnp.concatenate(outs, axis=1)

    return pl.pallas_call(
        body, name=name, grid=(nb,), in_specs=in_specs,
        out_specs=pl.BlockSpec((_QB, 512), lambda n: (n, 0)),
        out_shape=jax.ShapeDtypeStruct((t_len, 512), F32),
        compiler_params=_cp("parallel"),
    )(sink, zc, zc, zc, zc, zc)


def _attn_bwd(zc, sink, o, dcat, t_len, l_ctx, name):
    rows = zc.shape[0]
    in_specs, nb = _attn_specs(t_len, l_ctx)
    in_specs = in_specs + [pl.BlockSpec((_QB, 512), lambda n: (n, 0)), pl.BlockSpec((_QB, 512), lambda n: (n, 0))]
    hd = A_HEAD_DIM
    scale = hd ** -0.5

    def body(sink_ref, q_ref, kp_ref, kc_ref, kn_ref, kx_ref, o_ref, do_ref, dq_ref, dkv_ref, dsink_ref):
        n = pl.program_id(0)

        @pl.when(n == 0)
        def _():
            dkv_ref[...] = jnp.zeros_like(dkv_ref)
            dsink_ref[...] = jnp.zeros_like(dsink_ref)

        qv = q_ref[...]
        ov = o_ref[...]
        dov = do_ref[...]
        dqs, dkbs, dvbs, dkxs, dvxs, dsinks = [], [], [], [], [], []
        for g in range(A_KV_HEADS):
            kb, vb, kxb, vxb = _attn_keys(kp_ref[...], kc_ref[...], kn_ref[...], kx_ref[...], g)
            parts = []
            for rs in _HEAD_PAIRS:
                p, pc, ps, inv, qg = _attn_probs(n, t_len, sink_ref, qv, kb, kxb, g, rs)
                og = jnp.concatenate([ov[:, (4 * g + r) * hd:(4 * g + r + 1) * hd] for r in rs], axis=0)
                dog = jnp.concatenate([dov[:, (4 * g + r) * hd:(4 * g + r + 1) * hd] for r in rs], axis=0)
                delta = jnp.sum(og * dog, axis=-1, keepdims=True)
                dogb = dog.astype(BF16)
                pn = p * inv
                pcn = pc * inv
                ds = (pn * (_dot_nt(dogb, vb) - delta) * scale).astype(BF16)
                dsc = (pcn * (_dot_nt(dogb, vxb) - delta) * scale).astype(BF16)
                dsk = (ps * inv) * (0.0 - delta)
                dqg = _dot(ds, kb) + _dot(dsc, kxb)
                dqs += [dqg[i * _QB:(i + 1) * _QB] for i in range(len(rs))]
                parts.append((_dot_tn(ds, qg), _dot_tn(pn.astype(BF16), dogb),
                              _dot_tn(dsc, qg), _dot_tn(pcn.astype(BF16), dogb)))
                for i in range(len(rs)):
                    tot = jnp.sum(dsk[i * _QB:(i + 1) * _QB], axis=0, keepdims=True)
                    dsinks.append(jnp.broadcast_to(tot, (1, 128)))
            dkbs.append(parts[0][0] + parts[1][0])
            dvbs.append(parts[0][1] + parts[1][1])
            dkxs.append(parts[0][2] + parts[1][2])
            dvxs.append(parts[0][3] + parts[1][3])
        dsink_ref[...] += jnp.concatenate(dsinks, axis=0)
        dq_ref[...] = jnp.concatenate(dqs, axis=1)
        band = jnp.concatenate(dkbs + dvbs, axis=1)
        ctxc = jnp.concatenate(dkxs + dvxs, axis=1)
        r_prev = pl.multiple_of(jnp.maximum(n - 1, 0) * _QB, _QB)
        r_cur = pl.multiple_of(n * _QB, _QB)
        r_next = pl.multiple_of((n + 1) * _QB, _QB)
        dkv_ref[pl.ds(r_prev, _QB), :] += band[0:_QB]
        dkv_ref[pl.ds(r_cur, _QB), :] += band[_QB:2 * _QB]
        dkv_ref[pl.ds(r_next, _QB), :] += band[2 * _QB:3 * _QB]
        dkv_ref[t_len:t_len + l_ctx, :] += ctxc

    return pl.pallas_call(
        body, name=name, grid=(nb,), in_specs=in_specs,
        out_specs=[pl.BlockSpec((_QB, 512), lambda n: (n, 0)),
                   pl.BlockSpec((rows, 256), lambda n: (0, 0)),
                   pl.BlockSpec((8, 128), lambda n: (0, 0))],
        out_shape=[jax.ShapeDtypeStruct((t_len, 512), F32),
                   jax.ShapeDtypeStruct((rows, 256), F32),
                   jax.ShapeDtypeStruct((8, 128), F32)],
        compiler_params=_cp("arbitrary"),
    )(sink, zc, zc, zc, zc, zc, o, dcat)


_GC = B_CHUNK


def _split_bf16(a):
    hi = a.astype(BF16)
    return hi, (a - hi.astype(F32)).astype(BF16)


def _gla_chunk_terms(qk, la, reverse):
    q = qk[:, 0:256]
    k = qk[:, 256:512]
    off = 256 if reverse else 0
    lad = la[:, off:off + 256]
    ii = lax.broadcasted_iota(jnp.int32, (_GC, _GC), 0)
    jj = lax.broadcasted_iota(jnp.int32, (_GC, _GC), 1)
    mask = (jj >= ii) if reverse else (jj <= ii)
    tri = jnp.where(mask, 1.0, 0.0).astype(BF16)
    la_hi, la_lo = _split_bf16(lad)
    g = _dot(tri, la_hi) + _dot(tri, la_lo)
    gl = jnp.sum(lad, axis=0, keepdims=True)
    eg = jnp.exp(g)
    eng = jnp.exp(-g)
    eend = jnp.exp(gl - g)
    sc = B_DK ** -0.5
    qt = q * (sc * eg)
    kt = k * eng
    ke = k * eend
    return mask, tri, gl, eg, eng, eend, qt, kt, ke


def _same_head(rows, cols, row_shift, col_shift):
    r = jnp.right_shift(lax.broadcasted_iota(jnp.int32, (rows, cols), 0), row_shift)
    c = jnp.right_shift(lax.broadcasted_iota(jnp.int32, (rows, cols), 1), col_shift)
    return r == c


def _block_diag_rows(x, col_shift):
    tiled = jnp.concatenate([x] * B_HEADS, axis=0)
    return jnp.where(_same_head(tiled.shape[0], tiled.shape[1], 6, col_shift), tiled, jnp.zeros_like(tiled))


def _fold_heads(x):
    c = x.shape[0] // B_HEADS
    return (x[0:c] + x[c:2 * c]) + (x[2 * c:3 * c] + x[3 * c:4 * c])


def _chunk_mask4(reverse):
    ii = lax.broadcasted_iota(jnp.int32, (_GC, B_HEADS * _GC), 0)
    jj = jnp.bitwise_and(lax.broadcasted_iota(jnp.int32, (_GC, B_HEADS * _GC), 1), _GC - 1)
    return (jj >= ii) if reverse else (jj <= ii)


_ST_SHAPE = (B_HEADS * B_DV, B_HEADS * B_DK)


def _state_blocks(t):
    return [t[hh * B_DV:(hh + 1) * B_DV, hh * B_DK:(hh + 1) * B_DK] for hh in range(B_HEADS)]


def _state_from_blocks(blocks):
    full = jnp.concatenate([jnp.concatenate([b] * B_HEADS, axis=1) for b in blocks], axis=0)
    return jnp.where(_same_head(_ST_SHAPE[0], _ST_SHAPE[1], 7, 6), full, 0.0)


def _gla_fwd(zc, la, dep, t_len, l_ctx, name):
    rows = zc.shape[0]
    n_x = t_len // _GC
    n_c = n_x + l_ctx // _GC
    qkb, vb = ZC_QK // 512, ZC_V // 512

    def ch_f(c):
        return lax.rem(c + n_x, n_c)

    def ch_r(c):
        return n_c - 1 - c

    def body(qkf_ref, vf_ref, laf_ref, qkr_ref, vr_ref, lar_ref, dep_ref, of_ref, or_ref, spf_ref, spr_ref, stf, strv):
        del dep_ref
        c = pl.program_id(0)

        @pl.when(c == 0)
        def _():
            stf[...] = jnp.zeros_like(stf)
            strv[...] = jnp.zeros_like(strv)

        results = []
        for qk_ref, v_ref, la_ref, st, reverse in ((qkf_ref, vf_ref, laf_ref, stf, False),
                                                   (qkr_ref, vr_ref, lar_ref, strv, True)):
            mask, _, gl, _, _, _, qt, kt, ke = _gla_chunk_terms(qk_ref[...], la_ref[...], reverse)
            vbf = v_ref[...].astype(BF16)
            qtb, keb = qt.astype(BF16), ke.astype(BF16)
            kbd = _block_diag_rows(kt.astype(BF16), 6)
            vbd = _block_diag_rows(vbf, 7)
            mask4 = _chunk_mask4(reverse)
            t_prev = st[...]
            att = jnp.where(mask4, _dot_nt(qtb, kbd), 0.0).astype(BF16)
            o_all = _dot(att, vbd) + _dot_nt(qtb, t_prev.astype(BF16))
            t_new = t_prev * jnp.exp(gl) + jnp.where(_same_head(_ST_SHAPE[0], _ST_SHAPE[1], 7, 6),
                                                     _dot_tn(vbf, keb), 0.0)
            results.append((o_all, t_prev, t_new))
        for (o_all, t_prev, t_new), o_ref, sp_ref, st in zip(results, (of_ref, or_ref), (spf_ref, spr_ref), (stf, strv)):
            o_ref[...] = o_all
            for hh, blk in enumerate(_state_blocks(t_prev)):
                sp_ref[0, hh] = blk
            st[...] = t_new

    st_shape = (B_HEADS, B_DV, B_DK)
    return pl.pallas_call(
        body, name=name, grid=(n_c,),
        in_specs=[pl.BlockSpec((_GC, 512), lambda c: (ch_f(c), qkb)),
                  pl.BlockSpec((_GC, 512), lambda c: (ch_f(c), vb)),
                  pl.BlockSpec((_GC, 512), lambda c: (ch_f(c), 0)),
                  pl.BlockSpec((_GC, 512), lambda c: (ch_r(c), qkb)),
                  pl.BlockSpec((_GC, 512), lambda c: (ch_r(c), vb)),
                  pl.BlockSpec((_GC, 512), lambda c: (ch_r(c), 0)),
                  pl.BlockSpec((8, 128), lambda c: (0, 0))],
        out_specs=[pl.BlockSpec((_GC, 512), lambda c: (ch_f(c), 0)),
                   pl.BlockSpec((_GC, 512), lambda c: (ch_r(c), 0)),
                   pl.BlockSpec((1,) + st_shape, lambda c: (c, 0, 0, 0)),
                   pl.BlockSpec((1,) + st_shape, lambda c: (c, 0, 0, 0))],
        out_shape=[jax.ShapeDtypeStruct((rows, 512), F32), jax.ShapeDtypeStruct((rows, 512), F32),
                   jax.ShapeDtypeStruct((n_c,) + st_shape, F32), jax.ShapeDtypeStruct((n_c,) + st_shape, F32)],
        scratch_shapes=[pltpu.VMEM(_ST_SHAPE, F32), pltpu.VMEM(_ST_SHAPE, F32)],
        compiler_params=_cp("arbitrary"),
    )(zc, zc, la, zc, zc, la, dep)


def _gla_bwd(zc, la, spf, spr, dosum, t_len, l_ctx, name):
    rows = zc.shape[0]
    n_x = t_len // _GC
    n_c = n_x + l_ctx // _GC
    n_all = rows // _GC
    qkb, vb = ZC_QK // 512, ZC_V // 512

    def scan_of(c):
        return jnp.maximum(n_c - 1 - c, 0)

    def ch_f(c):
        return jnp.where(c < n_c, lax.rem(scan_of(c) + n_x, n_c), c)

    def ch_r(c):
        return c

    def do_of(ch):
        return jnp.minimum(ch, n_x - 1)

    def body(qkf_ref, vf_ref, laf_ref, spf_ref, dof_ref, qkr_ref, vr_ref, lar_ref, spr_ref, dor_ref,
             dqkf_ref, dvf_ref, dlaf_ref, dqkr_ref, dvr_ref, dlar_ref, dsf, dsr):
        c = pl.program_id(0)

        @pl.when(c == 0)
        def _():
            dsf[...] = jnp.zeros_like(dsf)
            dsr[...] = jnp.zeros_like(dsr)

        @pl.when(c >= n_c)
        def _():
            for r in (dqkf_ref, dvf_ref, dlaf_ref, dqkr_ref, dvr_ref, dlar_ref):
                r[...] = jnp.zeros_like(r)

        @pl.when(c < n_c)
        def _():
            sc = B_DK ** -0.5
            results = []
            for qk_ref, v_ref, la_ref, sp_ref, do_ref, dst, reverse, ch in (
                    (qkf_ref, vf_ref, laf_ref, spf_ref, dof_ref, dsf, False, ch_f(c)),
                    (qkr_ref, vr_ref, lar_ref, spr_ref, dor_ref, dsr, True, ch_r(c))):
                mask, tri, gl, eg, eng, eend, qt, kt, ke = _gla_chunk_terms(qk_ref[...], la_ref[...], reverse)
                vbf = v_ref[...].astype(BF16)
                dob = jnp.where(ch < n_x, do_ref[...], 0.0).astype(BF16)
                qtb, keb = qt.astype(BF16), ke.astype(BF16)
                kbd = _block_diag_rows(kt.astype(BF16), 6)
                vbd = _block_diag_rows(vbf, 7)
                mask4 = _chunk_mask4(reverse)
                egl = jnp.exp(gl)
                t_prev = _state_from_blocks([sp_ref[0, hh] for hh in range(B_HEADS)])
                dt_new = dst[...]
                tpb, dtb = t_prev.astype(BF16), dt_new.astype(BF16)
                att = jnp.where(mask4, _dot_nt(qtb, kbd), 0.0).astype(BF16)
                datt = jnp.where(mask4, _dot_nt(dob, vbd), 0.0).astype(BF16)
                dqt = _dot(datt, kbd) + _dot(dob, tpb)
                dkt = _fold_heads(jnp.where(_same_head(256, 256, 6, 6), _dot_tn(datt, qtb), 0.0))
                dv = _fold_heads(jnp.where(_same_head(256, 512, 6, 7), _dot_tn(att, dob), 0.0)) + _dot_nt(keb, dtb)
                dke = _dot(vbf, dtb)
                dt_prev = dt_new * egl + jnp.where(_same_head(_ST_SHAPE[0], _ST_SHAPE[1], 7, 6),
                                                   _dot_tn(dob, qtb), 0.0)
                dgl = (jnp.sum(dke * ke, axis=0, keepdims=True)
                       + jnp.sum(dt_new * t_prev, axis=0, keepdims=True) * egl)
                dg_hi, dg_lo = _split_bf16(dqt * qt - dkt * kt - dke * ke)
                dla = _dot_tn(tri, dg_hi) + _dot_tn(tri, dg_lo) + dgl
                dqk = jnp.concatenate([dqt * (sc * eg), dkt * eng + dke * eend], axis=1)
                results.append((dqk, dv, dla, dt_prev))
            for (dqk, dv, dla, dt_prev), dqk_ref, dv_ref, dla_ref, dst in zip(
                    results, (dqkf_ref, dqkr_ref), (dvf_ref, dvr_ref), (dlaf_ref, dlar_ref), (dsf, dsr)):
                dqk_ref[...] = dqk
                dv_ref[...] = dv
                dla_ref[...] = dla
                dst[...] = dt_prev

    st_shape = (B_HEADS, B_DV, B_DK)

    def side(chf):
        return [pl.BlockSpec((_GC, 512), lambda c: (chf(c), qkb)),
                pl.BlockSpec((_GC, 512), lambda c: (chf(c), vb)),
                pl.BlockSpec((_GC, 512), lambda c: (chf(c), 0)),
                pl.BlockSpec((1,) + st_shape, lambda c: (scan_of(c), 0, 0, 0)),
                pl.BlockSpec((_GC, 512), lambda c: (do_of(chf(c)), 0))]

    def out_side(chf):
        return [pl.BlockSpec((_GC, 512), lambda c: (chf(c), 0)),
                pl.BlockSpec((_GC, 512), lambda c: (chf(c), 0)),
                pl.BlockSpec((_GC, 256), lambda c: (chf(c), 0))]

    shp = [jax.ShapeDtypeStruct((rows, 512), F32), jax.ShapeDtypeStruct((rows, 512), F32),
           jax.ShapeDtypeStruct((rows, 256), F32)]
    return pl.pallas_call(
        body, name=name, grid=(n_all,),
        in_specs=side(ch_f) + side(ch_r),
        out_specs=out_side(ch_f) + out_side(ch_r),
        out_shape=shp + shp,
        scratch_shapes=[pltpu.VMEM(_ST_SHAPE, F32), pltpu.VMEM(_ST_SHAPE, F32)],
        compiler_params=_cp("arbitrary"),
    )(zc, zc, la, spf, dosum, zc, zc, la, spr, dosum)


def _gla_out_fwd(o_a, o_f, o_r, zc, gla_g, t_len, name):
    tm = ROW_TILE
    rb = ZC_R // 512

    def body(oa_ref, of_ref, or_ref, r_ref, g_ref, cat_ref):
        osum = of_ref[...] + or_ref[...]
        g = g_ref[...]
        pieces = []
        for hh in range(B_HEADS):
            oh = osum[:, hh * B_DV:(hh + 1) * B_DV]
            rs = lax.rsqrt(jnp.mean(oh * oh, axis=-1, keepdims=True) + RMS_EPS)
            pieces.append((oh * rs) * g)
        r = r_ref[...]
        cat_ref[:, 0:512] = oa_ref[...].astype(BF16)
        cat_ref[:, 512:1024] = (jnp.concatenate(pieces, axis=1) * (r * _sigmoid(r))).astype(BF16)

    return pl.pallas_call(
        body, name=name, grid=(t_len // tm,),
        in_specs=[pl.BlockSpec((tm, 512), lambda i: (i, 0)),
                  pl.BlockSpec((tm, 512), lambda i: (i, 0)),
                  pl.BlockSpec((tm, 512), lambda i: (i, 0)),
                  pl.BlockSpec((tm, 512), lambda i: (i, rb)),
                  pl.BlockSpec((1, B_DV), lambda i: (0, 0))],
        out_specs=pl.BlockSpec((tm, D_MODEL), lambda i: (i, 0)),
        out_shape=jax.ShapeDtypeStruct((t_len, D_MODEL), BF16),
        compiler_params=_cp("parallel"),
    )(o_a, o_f, o_r, zc, gla_g)


def _gla_out_bwd(dcat, o_f, o_r, zc, gla_g, t_len, name):
    tm = ROW_TILE
    rb = ZC_R // 512

    def body(d_ref, of_ref, or_ref, r_ref, g_ref, dos_ref, dr_ref, dg_ref):
        i = pl.program_id(0)
        osum = of_ref[...] + or_ref[...]
        g = g_ref[...]
        r = r_ref[...]
        dgo = d_ref[...]
        sg = _sigmoid(r)
        dnrmg = dgo * (r * sg)
        nrms, dos = [], []
        dg_acc = jnp.zeros((1, B_DV), F32)
        for hh in range(B_HEADS):
            oh = osum[:, hh * B_DV:(hh + 1) * B_DV]
            rs = lax.rsqrt(jnp.mean(oh * oh, axis=-1, keepdims=True) + RMS_EPS)
            nrm = oh * rs
            dn = dnrmg[:, hh * B_DV:(hh + 1) * B_DV]
            dg_acc = dg_acc + jnp.sum(dn * nrm, axis=0, keepdims=True)
            dnn = dn * g
            dos.append(rs * (dnn - nrm * jnp.mean(dnn * nrm, axis=-1, keepdims=True)))
            nrms.append(nrm * g)
        dos_ref[...] = jnp.concatenate(dos, axis=1)
        dr_ref[...] = dgo * jnp.concatenate(nrms, axis=1) * (sg * (1.0 + r * (1.0 - sg)))

        @pl.when(i == 0)
        def _():
            dg_ref[...] = jnp.zeros_like(dg_ref)

        dg_ref[...] += dg_acc

    return pl.pallas_call(
        body, name=name, grid=(t_len // tm,),
        in_specs=[pl.BlockSpec((tm, 512), lambda i: (i, 1)),
                  pl.BlockSpec((tm, 512), lambda i: (i, 0)),
                  pl.BlockSpec((tm, 512), lambda i: (i, 0)),
                  pl.BlockSpec((tm, 512), lambda i: (i, rb)),
                  pl.BlockSpec((1, B_DV), lambda i: (0, 0))],
        out_specs=[pl.BlockSpec((tm, 512), lambda i: (i, 0)),
                   pl.BlockSpec((tm, 512), lambda i: (i, 0)),
                   pl.BlockSpec((1, B_DV), lambda i: (0, 0))],
        out_shape=[jax.ShapeDtypeStruct((t_len, 512), F32), jax.ShapeDtypeStruct((t_len, 512), F32),
                   jax.ShapeDtypeStruct((1, B_DV), F32)],
        compiler_params=_cp("arbitrary"),
    )(dcat, o_f, o_r, zc, gla_g)


def _mix_prep(dq, dkv, dqk_f, dqk_r, dv_f, dv_r, d_r, dla_f, dla_r, zc, wg2, bias2, cs, t_len, name):
    rows = zc.shape[0]
    tm = ROW_TILE
    n_x = t_len // tm
    gb = ZC_G // 128

    def xrow(i):
        return jnp.minimum(i, n_x - 1)

    def body(dq_ref, dkv_ref, dqkf_ref, dqkr_ref, dvf_ref, dvr_ref, dr_ref, dlaf_ref, dlar_ref, zg_ref, wg_ref,
             b_ref, cs_ref, dz_ref, dwg_ref, db_ref):
        i = pl.program_id(0)
        is_x = i < n_x
        cos = cs_ref[:, 0:128]
        sin = cs_ref[:, 128:256]
        cosq = jnp.concatenate([cos] * 4, axis=1)
        sinq = jnp.concatenate([sin] * 4, axis=1)
        dqv = jnp.where(is_x, dq_ref[...], 0.0)
        dz_ref[:, ZC_Q:ZC_QK] = (dqv * cosq + _swap16(dqv * sinq)).astype(BF16)
        dz_ref[:, ZC_QK:ZC_V] = (dqkf_ref[...] + dqkr_ref[...]).astype(BF16)
        dz_ref[:, ZC_V:ZC_R] = (dvf_ref[...] + dvr_ref[...]).astype(BF16)
        dz_ref[:, ZC_R:ZC_KV] = jnp.where(is_x, dr_ref[...], 0.0).astype(BF16)
        dk = dkv_ref[:, 0:128]
        dz_ref[:, ZC_KV:ZC_KV + 128] = (dk * cos + _swap16(dk * sin)).astype(BF16)
        dz_ref[:, ZC_KV + 128:ZC_G] = dkv_ref[:, 128:256].astype(BF16)
        zgb = zg_ref[...].astype(BF16)
        wg = wg_ref[...]
        pre = _dot(zgb, wg) + b_ref[...]
        dla = jnp.concatenate([dlaf_ref[...], dlar_ref[...]], axis=1)
        dpre = dla * (_sigmoid(-pre) / B_GATE_NORM)
        dpb = dpre.astype(BF16)
        dz_ref[:, ZC_G:ZC_W] = _dot_nt(dpb, wg).astype(BF16)

        @pl.when(i == 0)
        def _():
            dwg_ref[...] = jnp.zeros_like(dwg_ref)
            db_ref[...] = jnp.zeros_like(db_ref)

        dwg_ref[...] += _dot_tn(zgb, dpb)
        db_ref[...] += jnp.sum(dpre, axis=0, keepdims=True)

    return pl.pallas_call(
        body, name=name, grid=(rows // tm,),
        in_specs=[pl.BlockSpec((tm, 512), lambda i: (xrow(i), 0)),
                  pl.BlockSpec((tm, 256), lambda i: (i, 0)),
                  pl.BlockSpec((tm, 512), lambda i: (i, 0)),
                  pl.BlockSpec((tm, 512), lambda i: (i, 0)),
                  pl.BlockSpec((tm, 512), lambda i: (i, 0)),
                  pl.BlockSpec((tm, 512), lambda i: (i, 0)),
                  pl.BlockSpec((tm, 512), lambda i: (xrow(i), 0)),
                  pl.BlockSpec((tm, 256), lambda i: (i, 0)),
                  pl.BlockSpec((tm, 256), lambda i: (i, 0)),
                  pl.BlockSpec((tm, 128), lambda i: (i, gb)),
                  pl.BlockSpec((128, 512), lambda i: (0, 0)),
                  pl.BlockSpec((1, 512), lambda i: (0, 0)),
                  pl.BlockSpec((tm, 256), lambda i: (i, 0))],
        out_specs=[pl.BlockSpec((tm, ZC_W), lambda i: (i, 0)),
                   pl.BlockSpec((128, 512), lambda i: (0, 0)),
                   pl.BlockSpec((1, 512), lambda i: (0, 0))],
        out_shape=[jax.ShapeDtypeStruct((rows, ZC_W), BF16),
                   jax.ShapeDtypeStruct((128, 512), F32),
                   jax.ShapeDtypeStruct((1, 512), F32)],
        compiler_params=_cp("arbitrary"),
    )(dq, dkv, dqk_f, dqk_r, dv_f, dv_r, d_r, dla_f, dla_r, zc, wg2, bias2, cs)


def _gate_weights(w_a2_f, b_a_f, w_a2_b, b_a_b):
    wg2 = jnp.zeros((128, 512), F32)
    wg2 = wg2.at[0:B_GATE_RANK, 0:256].set(w_a2_f).at[B_GATE_RANK:2 * B_GATE_RANK, 256:512].set(w_a2_b)
    bias2 = jnp.concatenate([b_a_f, b_a_b]).reshape(1, 512)
    return wg2.astype(BF16), bias2


_WIN_PERM = ((0, 512), (768, 1280), (1280, 1792), (1792, 2304), (512, 768), (2304, 2336))


def _w_in_to_cat(w_in_full):
    parts = [w_in_full[:, a:b] for a, b in _WIN_PERM]
    parts.append(jnp.zeros((w_in_full.shape[0], ZC_W - PROJ_DIM), w_in_full.dtype))
    return jnp.concatenate(parts, axis=1)


def _cat_to_w_in(d_wcat):
    return jnp.concatenate([d_wcat[:, ZC_Q:ZC_QK], d_wcat[:, ZC_KV:ZC_G], d_wcat[:, ZC_QK:ZC_KV],
                            d_wcat[:, ZC_G:ZC_G + 2 * B_GATE_RANK]], axis=1)


def _mixer_ab_forward(x1, g3, mods, wcat, wg2, bias2, sink, gla_g, w_out, cs, t_len, l_ctx, n_x, pace):
    h, zc, la = _proj_fwd(x1, g3, mods, n_x, wcat, wg2, bias2, cs, "mix0_proj")
    dep = pace("proj", zc)
    o_a = _attn_fwd(zc, sink + dep[0, 0], t_len, l_ctx, "mix0_attn")
    dep = pace("attn", o_a)
    o_f, o_r, spf, spr = _gla_fwd(zc, la, dep, t_len, l_ctx, "mix0_gla")
    dep = pace("gla", o_f)
    cat = _gla_out_fwd(o_a, o_f, o_r, zc, gla_g + dep[0:1, 0:1], t_len, "mix0_glaout")
    x2, y = _matmul_resid(cat, w_out, x1, mods, 5, 1.0, n_x, t_len, "mix0_out")
    return x2, (x1, h, zc, la, o_a, o_f, o_r, spf, spr, cat, y)


def _mixer_ab_backward(dx2, saved, g3, mods, wcat, wg2, bias2, sink, gla_g, w_out, cs, t_len, l_ctx, n_x):
    x1, h, zc, la, o_a, o_f, o_r, spf, spr, cat, y = saved
    rows = x1.shape[0]
    tm = ROW_TILE
    dy, dcat, dgate = _gate_dy(dx2, y, mods, 5, 1.0, n_x, t_len, w_out, "mix0_dy")
    tk = _token_tile(t_len)
    d_wout = _matmul_tn(
        cat, dy, pl.BlockSpec((tk, D_MODEL), lambda n, k: (k, 0)), pl.BlockSpec((tk, D_MODEL), lambda n, k: (k, 0)),
        (D_MODEL, D_MODEL), pl.BlockSpec((D_MODEL, D_MODEL), lambda n, k: (0, 0)), (1, t_len // tk), "mix0_dwout")
    dos, d_r, d_glag = _gla_out_bwd(dcat, o_f, o_r, zc, gla_g, t_len, "mix0_dglaout")
    dqk_f, dv_f, dla_f, dqk_r, dv_r, dla_r = _gla_bwd(zc, la, spf, spr, dos, t_len, l_ctx, "mix0_dgla")
    dq, dkv, dsink = _attn_bwd(zc, sink, o_a, dcat, t_len, l_ctx, "mix0_dattn")
    dzc, dwg2, dbias2 = _mix_prep(dq, dkv, dqk_f, dqk_r, dv_f, dv_r, d_r, dla_f, dla_r, zc, wg2, bias2, cs, t_len,
                                  "mix0_prep")
    tk = _token_tile(rows)
    d_wcat = _matmul_tn(
        h, dzc, pl.BlockSpec((tk, D_MODEL), lambda n, k: (k, 0)), pl.BlockSpec((tk, ZC_W), lambda n, k: (k, 0)),
        (D_MODEL, ZC_W), pl.BlockSpec((D_MODEL, ZC_W), lambda n, k: (0, 0)), (1, rows // tk), "mix0_dwin")
    pairs = [(dzc, pl.BlockSpec((tm, ZC_W), lambda i: (i, 0)), wcat, pl.BlockSpec((D_MODEL, ZC_W), lambda i: (0, 0)))]
    dx1, stats = _bwd_dx(pairs, x1, dx2, t_len // tm, g3, mods, 1, n_x, "mix0_dx")
    return dx1, stats, dgate, d_wcat, dwg2, dbias2, dsink, d_glag, d_wout


_PT = 256
_PH = 16


def _pool_window(n, t_len, w, transpose):
    shape = (_PT, _PT + 2 * _PH)
    a = n * _PT + lax.broadcasted_iota(jnp.int32, shape, 0)
    b = n * _PT - _PH + lax.broadcasted_iota(jnp.int32, shape, 1)
    t, s = (b, a) if transpose else (a, b)
    lo = jnp.maximum(t - w // 2, 0)
    hi = jnp.minimum(t + (w - w // 2), t_len)
    inside = (s >= lo) & (s < hi) & (t >= 0) & (t < t_len)
    return jnp.where(inside, 1.0, 0.0).astype(BF16)


def _pool_inv_count(first, count, t_len, w):
    t = first + lax.broadcasted_iota(jnp.int32, (count, 1), 0)
    lo = jnp.maximum(t - w // 2, 0)
    hi = jnp.minimum(t + (w - w // 2), t_len)
    return jnp.where((t >= 0) & (t < t_len), 1.0 / jnp.maximum(hi - lo, 1).astype(F32), 0.0)


def _window_sum(win, vals):
    hi, lo = _split_bf16(vals)
    return _dot(win, hi) + _dot(win, lo)


def _pool_halo(p_ref, c_ref, n_ref):
    return jnp.concatenate([p_ref[_PT - _PH:_PT, :], c_ref[...], n_ref[0:_PH, :]], axis=0)


def _pool_specs(t_len):
    nb = t_len // _PT
    return [pl.BlockSpec((_PT, D_MODEL), lambda n: (jnp.maximum(n - 1, 0), 0)),
            pl.BlockSpec((_PT, D_MODEL), lambda n: (n, 0)),
            pl.BlockSpec((_PT, D_MODEL), lambda n: (jnp.minimum(n + 1, nb - 1), 0))], nb


def _pool_fwd(h, wp, pscale, x1, mods, t_len, name):
    halo_specs, nb = _pool_specs(t_len)

    def body(hp_ref, hc_ref, hn_ref, w_ref, ps_ref, x_ref, m_ref, x2_ref, pooled_ref, ypre_ref):
        n = pl.program_id(0)
        hcat = _pool_halo(hp_ref, hc_ref, hn_ref)
        ys = []
        for gi, w in enumerate(POOL_WINDOWS):
            cols = slice(gi * POOL_GROUP, (gi + 1) * POOL_GROUP)
            hg = hcat[:, cols]
            mean = _window_sum(_pool_window(n, t_len, w, False), hg) * _pool_inv_count(n * _PT, _PT, t_len, w)
            pooled = (mean - hg[_PH:_PH + _PT]).astype(BF16)
            pooled_ref[:, cols] = pooled
            ys.append(_dot(pooled, w_ref[gi]))
        ypre = jnp.concatenate(ys, axis=1)
        ypre_ref[...] = ypre
        x2_ref[...] = x_ref[...] + m_ref[0, 5:6, :] * (ypre * ps_ref[...])

    return pl.pallas_call(
        body, name=name, grid=(nb,),
        in_specs=halo_specs + [pl.BlockSpec((4, POOL_GROUP, POOL_GROUP), lambda n: (0, 0, 0)),
                               pl.BlockSpec((1, D_MODEL), lambda n: (0, 0)),
                               pl.BlockSpec((_PT, D_MODEL), lambda n: (n, 0)),
                               pl.BlockSpec((1, N_MOD, D_MODEL), lambda n: (0, 0, 0))],
        out_specs=[pl.BlockSpec((_PT, D_MODEL), lambda n: (n, 0))] * 3,
        out_shape=[jax.ShapeDtypeStruct((t_len, D_MODEL), F32), jax.ShapeDtypeStruct((t_len, D_MODEL), BF16),
                   jax.ShapeDtypeStruct((t_len, D_MODEL), F32)],
        compiler_params=_cp("parallel"),
    )(h, h, h, wp, pscale, x1, mods)


def _pool_bwd_a(dx2, ypre, wp, pscale, mods, t_len, name):
    nb = t_len // _PT

    def body(d_ref, y_ref, w_ref, ps_ref, m_ref, dyp_ref, dpl_ref, dgate_ref, dps_ref):
        n = pl.program_id(0)
        dv = d_ref[...]
        ypre = y_ref[...]
        ps = ps_ref[...]
        dy = dv * m_ref[0, 5:6, :]
        dyp = (dy * ps).astype(BF16)
        dyp_ref[...] = dyp
        for gi in range(len(POOL_WINDOWS)):
            cols = slice(gi * POOL_GROUP, (gi + 1) * POOL_GROUP)
            dpl_ref[:, cols] = _dot_nt(dyp[:, cols], w_ref[gi])

        @pl.when(n == 0)
        def _():
            dgate_ref[...] = jnp.zeros_like(dgate_ref)
            dps_ref[...] = jnp.zeros_like(dps_ref)

        dgate_ref[...] += jnp.sum(dv * (ypre * ps), axis=0, keepdims=True)
        dps_ref[...] += jnp.sum(dy * ypre, axis=0, keepdims=True)

    return pl.pallas_call(
        body, name=name, grid=(nb,),
        in_specs=[pl.BlockSpec((_PT, D_MODEL), lambda n: (n, 0)),
                  pl.BlockSpec((_PT, D_MODEL), lambda n: (n, 0)),
                  pl.BlockSpec((4, POOL_GROUP, POOL_GROUP), lambda n: (0, 0, 0)),
                  pl.BlockSpec((1, D_MODEL), lambda n: (0, 0)),
                  pl.BlockSpec((1, N_MOD, D_MODEL), lambda n: (0, 0, 0))],
        out_specs=[pl.BlockSpec((_PT, D_MODEL), lambda n: (n, 0)),
                   pl.BlockSpec((_PT, D_MODEL), lambda n: (n, 0)),
                   pl.BlockSpec((1, D_MODEL), lambda n: (0, 0)),
                   pl.BlockSpec((1, D_MODEL), lambda n: (0, 0))],
        out_shape=[jax.ShapeDtypeStruct((t_len, D_MODEL), BF16), jax.ShapeDtypeStruct((t_len, D_MODEL), F32),
                   jax.ShapeDtypeStruct((1, D_MODEL), F32), jax.ShapeDtypeStruct((1, D_MODEL), F32)],
        compiler_params=_cp("arbitrary"),
    )(dx2, ypre, wp, pscale, mods)


def _pool_bwd_dx(dpl, x1, dx2, g3, mods, t_len, name):
    halo_specs, nb = _pool_specs(t_len)

    def body(dp_ref, dc_ref, dn_ref, x_ref, d_ref, g_ref, m_ref, dx_ref, acc_ref):
        n = pl.program_id(0)
        dcat = _pool_halo(dp_ref, dc_ref, dn_ref)
        dhs = []
        for gi, w in enumerate(POOL_WINDOWS):
            cols = slice(gi * POOL_GROUP, (gi + 1) * POOL_GROUP)
            dg = dcat[:, cols]
            scaled = dg * _pool_inv_count(n * _PT - _PH, _PT + 2 * _PH, t_len, w)
            dhs.append(_window_sum(_pool_window(n, t_len, w, True), scaled) - dg[_PH:_PH + _PT])
        dh = jnp.concatenate(dhs, axis=1)
        g = g_ref[1:2, :]
        scale = m_ref[0, 4:5, :]
        dx = _rms_mod_bwd_tail(dh, x_ref[...], g, scale, 0, acc_ref, n == 0)
        dx_ref[...] = d_ref[...] + dx

    return pl.pallas_call(
        body, name=name, grid=(nb,),
        in_specs=halo_specs + [pl.BlockSpec((_PT, D_MODEL), lambda n: (n, 0)),
                               pl.BlockSpec((_PT, D_MODEL), lambda n: (n, 0)),
                               pl.BlockSpec((3, D_MODEL), lambda n: (0, 0)),
                               pl.BlockSpec((1, N_MOD, D_MODEL), lambda n: (0, 0, 0))],
        out_specs=[pl.BlockSpec((_PT, D_MODEL), lambda n: (n, 0)),
                   pl.BlockSpec((2, 3, D_MODEL), lambda n: (0, 0, 0))],
        out_shape=[jax.ShapeDtypeStruct((t_len, D_MODEL), F32), jax.ShapeDtypeStruct((2, 3, D_MODEL), F32)],
        compiler_params=_cp("arbitrary"),
    )(dpl, dpl, dpl, x1, dx2, g3, mods)


def _mixer_pool_forward(x1, g3, mods, wp, pscale, t_len):
    h = _rms_mod_fwd(x1, g3, mods, 1, t_len // ROW_TILE, F32, "mix1_mod")
    x2, pooled, ypre = _pool_fwd(h, wp, pscale, x1, mods, t_len, "mix1_pool")
    return x2, (x1, pooled, ypre)


def _mixer_pool_backward(dx2, saved, g3, mods, wp, pscale, t_len):
    x1, pooled, ypre = saved
    tm = ROW_TILE
    dyp, dpl, dgate, dps = _pool_bwd_a(dx2, ypre, wp, pscale, mods, t_len, "mix1_da")
    d_wp = _matmul_tn(
        pooled, dyp, pl.BlockSpec((tm, POOL_GROUP), lambda g, k: (k, g)),
        pl.BlockSpec((tm, POOL_GROUP), lambda g, k: (k, g)),
        (4, POOL_GROUP, POOL_GROUP), pl.BlockSpec((1, POOL_GROUP, POOL_GROUP), lambda g, k: (g, 0, 0)),
        (4, t_len // tm), "mix1_dwp")
    dx1, stats = _pool_bwd_dx(dpl, x1, dx2, g3, mods, t_len, "mix1_dx")
    return dx1, stats, dgate, dps, d_wp


def _final_loss(x3, final_g, target, name):
    t_len = x3.shape[0]
    tm = ROW_TILE

    def body(x_ref, g_ref, t_ref, dx_ref, loss_ref, dg_ref):
        i = pl.program_id(0)
        xv = x_ref[...]
        g = g_ref[...]
        r = lax.rsqrt(jnp.mean(xv * xv, axis=-1, keepdims=True) + RMS_EPS)
        xhat = xv * r
        err = xhat * g - t_ref[...]
        part = 0.5 * jnp.sum(jnp.mean(err * err, axis=-1, keepdims=True), axis=0, keepdims=True)
        dy = err * (1.0 / D_MODEL)

        @pl.when(i == 0)
        def _():
            loss_ref[...] = jnp.zeros_like(loss_ref)
            dg_ref[...] = jnp.zeros_like(dg_ref)

        loss_ref[...] += jnp.broadcast_to(part, (1, 128))
        dg_ref[...] += jnp.sum(dy * xhat, axis=0, keepdims=True)
        dxh = dy * g
        dx_ref[...] = r * (dxh - xhat * jnp.mean(dxh * xhat, axis=-1, keepdims=True))

    return pl.pallas_call(
        body, name=name, grid=(t_len // tm,),
        in_specs=[pl.BlockSpec((tm, D_MODEL), lambda i: (i, 0)),
                  pl.BlockSpec((1, D_MODEL), lambda i: (0, 0)),
                  pl.BlockSpec((tm, D_MODEL), lambda i: (i, 0))],
        out_specs=[pl.BlockSpec((tm, D_MODEL), lambda i: (i, 0)),
                   pl.BlockSpec((1, 128), lambda i: (0, 0)),
                   pl.BlockSpec((1, D_MODEL), lambda i: (0, 0))],
        out_shape=[jax.ShapeDtypeStruct((t_len, D_MODEL), F32), jax.ShapeDtypeStruct((1, 128), F32),
                   jax.ShapeDtypeStruct((1, D_MODEL), F32)],
        compiler_params=_cp("arbitrary"),
    )(x3, final_g, target)


_CROWS = 16


def _adaln_fwd(c16, w_mod, bias_k, name):
    n_l, _, cols = w_mod.shape

    def body(c_ref, w_ref, b_ref, o_ref):
        cv = c_ref[...]
        sc = (cv * _sigmoid(cv)).astype(BF16)
        o_ref[0] = _dot(sc, w_ref[0].astype(BF16)) + b_ref[0]

    return pl.pallas_call(
        body, name=name, grid=(n_l,),
        in_specs=[pl.BlockSpec((_CROWS, D_MODEL), lambda l: (0, 0)),
                  pl.BlockSpec((1, D_MODEL, cols), lambda l: (l, 0, 0)),
                  pl.BlockSpec((1, 1, cols), lambda l: (l, 0, 0))],
        out_specs=pl.BlockSpec((1, _CROWS, cols), lambda l: (l, 0, 0)),
        out_shape=jax.ShapeDtypeStruct((n_l, _CROWS, cols), F32),
        compiler_params=_cp("parallel"),
    )(c16, w_mod, bias_k)


def _adaln_bwd(c16, d16, w_mod, dmmc_k, name):
    n_l, _, cols = w_mod.shape

    def body(c_ref, d_ref, w_ref, dm_ref, gw_ref, cp_ref):
        layer = pl.program_id(0)
        cv = c_ref[...]
        gw_ref[0] = _dot_tn_hi(cv * _sigmoid(cv), d_ref[0])

        @pl.when(layer == 0)
        def _():
            cp_ref[...] = jnp.sum(w_ref[0] * dm_ref[...], axis=1, keepdims=True)

    return pl.pallas_call(
        body, name=name, grid=(n_l,),
        in_specs=[pl.BlockSpec((_CROWS, D_MODEL), lambda l: (0, 0)),
                  pl.BlockSpec((1, _CROWS, cols), lambda l: (l, 0, 0)),
                  pl.BlockSpec((1, D_MODEL, cols), lambda l: (0, 0, 0)),
                  pl.BlockSpec((1, cols), lambda l: (0, 0))],
        out_specs=[pl.BlockSpec((1, D_MODEL, cols), lambda l: (l, 0, 0)),
                   pl.BlockSpec((D_MODEL, 1), lambda l: (0, 0))],
        out_shape=[jax.ShapeDtypeStruct((n_l, D_MODEL, cols), F32), jax.ShapeDtypeStruct((D_MODEL, 1), F32)],
        compiler_params=_cp("arbitrary"),
    )(c16, d16, w_mod, dmmc_k)


def _cctx_grad(cparts, c_ctx2, name):
    def body(p_ref, c_ref, o_ref):
        tot = ((p_ref[0] + p_ref[2]) + p_ref[4]) + p_ref[6]
        cv = c_ref[...]
        sg = _sigmoid(cv)
        o_ref[...] = tot * (sg * (1.0 + cv * (1.0 - sg)))

    return pl.pallas_call(
        body, name=name, out_shape=jax.ShapeDtypeStruct((8, 128), F32),
        in_specs=[pl.BlockSpec(memory_space=pltpu.VMEM), pl.BlockSpec(memory_space=pltpu.VMEM)],
        out_specs=pl.BlockSpec(memory_space=pltpu.VMEM),
    )(cparts, c_ctx2)


def _sum_devices(ga, name):
    def body(g_ref, o_ref):
        acc = g_ref[0]
        for d in range(1, N_DEV):
            acc = acc + g_ref[d]
        o_ref[...] = acc

    return pl.pallas_call(
        body, name=name, out_shape=jax.ShapeDtypeStruct(ga.shape[1:], F32),
        in_specs=[pl.BlockSpec(memory_space=pltpu.VMEM)], out_specs=pl.BlockSpec(memory_space=pltpu.VMEM),
    )(ga)


def _place():
    return lax.axis_index("x"), lax.axis_index("y"), lax.axis_index("c")


def _flip(a, d):
    return 1 - a if d else a


_CHIP_FLIPS = ((1, 0), (0, 1), (1, 1))


def _allgather_small(v, name, after=()):
    r, cc = v.shape

    def body(v_ref, *rest):
        out_ref, send_sems, recv_sems, local_sem = rest[-4:]
        x, y, c = _place()
        me = 4 * x + 2 * y + c
        mine = pltpu.make_async_copy(v_ref, out_ref.at[me], local_sem)
        mine.start()
        sends = []
        for k in range(1, N_DEV):
            peer = (_flip(x, (k >> 2) & 1), _flip(y, (k >> 1) & 1), _flip(c, k & 1))
            cp = pltpu.make_async_remote_copy(src_ref=v_ref, dst_ref=out_ref.at[me], send_sem=send_sems.at[k - 1],
                                              recv_sem=recv_sems.at[k - 1], device_id=peer, device_id_type=MESH)
            cp.start()
            sends.append(cp)
        for k in range(1, N_DEV):
            px, py, pc = _flip(x, (k >> 2) & 1), _flip(y, (k >> 1) & 1), _flip(c, k & 1)
            pltpu.make_async_remote_copy(src_ref=v_ref, dst_ref=out_ref.at[4 * px + 2 * py + pc],
                                         send_sem=send_sems.at[k - 1], recv_sem=recv_sems.at[k - 1],
                                         device_id=(px, py, pc), device_id_type=MESH).wait_recv()
        for cp in sends:
            cp.wait_send()
        mine.wait()

    return pl.pallas_call(
        body, name=name, out_shape=jax.ShapeDtypeStruct((N_DEV, r, cc), F32),
        in_specs=[pl.BlockSpec(memory_space=pltpu.VMEM)] + [pl.BlockSpec(memory_space=pl.ANY)] * len(after),
        out_specs=pl.BlockSpec(memory_space=pltpu.VMEM),
        scratch_shapes=[pltpu.SemaphoreType.DMA((N_DEV - 1,)), pltpu.SemaphoreType.DMA((N_DEV - 1,)),
                        pltpu.SemaphoreType.DMA],
        compiler_params=pltpu.CompilerParams(vmem_limit_bytes=VMEM_LIMIT_BYTES),
    )(v, *after)


_HBM_SPEC = pl.BlockSpec(memory_space=pltpu.HBM)
_SEM_SPEC = pl.BlockSpec(memory_space=pltpu.SEMAPHORE)
_EFFECT = pltpu.SideEffectType.DATAFLOW_SIDE_EFFECTING


def _in_hbm(a):
    return pltpu.with_memory_space_constraint(a, pltpu.HBM)


def _gather_start(arrs, groups, after, name):
    n, n_g = len(arrs), len(groups)

    def body(*refs):
        ins, zones = refs[:n], refs[n:2 * n]
        sems = refs[2 * n + 1:2 * n + 1 + 2 * n_g]
        token = refs[2 * n + 1 + 2 * n_g + 2 * n]
        x, y, c = _place()
        k_me = 2 * x + y
        for g, members in enumerate(groups):
            for t, a in enumerate(members):
                for j, (dx, dy) in enumerate(_CHIP_FLIPS):
                    pltpu.make_async_remote_copy(
                        src_ref=ins[a], dst_ref=zones[a].at[k_me], send_sem=sems[2 * g].at[3 * t + j],
                        recv_sem=sems[2 * g + 1].at[3 * t + j], device_id=(_flip(x, dx), _flip(y, dy), c),
                        device_id_type=MESH).start()
        token[...] = jnp.zeros_like(token)

    k_own = 2 * lax.axis_index("x") + lax.axis_index("y")
    zones = [lax.dynamic_update_slice(lax.empty((N_CHIPS,) + a.shape, a.dtype), a[None], (k_own,) + (0,) * a.ndim)
             for a in arrs]
    sem_shapes = []
    for members in groups:
        sem_shapes += [pltpu.SemaphoreType.DMA((3 * len(members),))] * 2
    outs = pl.pallas_call(
        body, name=name,
        out_shape=sem_shapes + [pltpu.HBM(a.shape, a.dtype) for a in arrs]
        + [pltpu.HBM(z.shape, z.dtype) for z in zones] + [jax.ShapeDtypeStruct((8, 128), F32)],
        in_specs=[_HBM_SPEC] * (2 * n) + [pl.BlockSpec(memory_space=pl.ANY)],
        out_specs=[_SEM_SPEC] * (2 * n_g) + [_HBM_SPEC] * (2 * n) + [pl.BlockSpec(memory_space=pltpu.VMEM)],
        input_output_aliases={i: 2 * n_g + i for i in range(2 * n)},
        compiler_params=pltpu.CompilerParams(has_side_effects=_EFFECT),
    )(*[_in_hbm(a) for a in arrs], *[_in_hbm(z) for z in zones], after)
    sems = outs[:2 * n_g]
    thru = outs[2 * n_g:2 * n_g + n]
    zones = outs[2 * n_g + n:2 * n_g + 2 * n]
    return [(sems[2 * g], sems[2 * g + 1]) for g in range(n_g)], thru, zones, outs[-1]


def _gather_wait(shards, zones, send_sems, recv_sems, after, name):
    m = len(shards)

    def body(*refs):
        ins, zs = refs[:m], refs[m:2 * m]
        ssem, rsem = refs[2 * m], refs[2 * m + 1]
        x, y, c = _place()
        for t in range(m):
            for j, (dx, dy) in enumerate(_CHIP_FLIPS):
                px, py = _flip(x, dx), _flip(y, dy)
                cp = pltpu.make_async_remote_copy(
                    src_ref=ins[t], dst_ref=zs[t].at[2 * px + py], send_sem=ssem.at[3 * t + j],
                    recv_sem=rsem.at[3 * t + j], device_id=(px, py, c), device_id_type=MESH)
                cp.wait_send()
                cp.wait_recv()

    after = list(after) if isinstance(after, (list, tuple)) else [after]
    outs = pl.pallas_call(
        body, name=name,
        out_shape=[pltpu.HBM(a.shape, a.dtype) for a in list(shards) + list(zones)],
        in_specs=[_HBM_SPEC] * (2 * m) + [_SEM_SPEC, _SEM_SPEC] + [pl.BlockSpec(memory_space=pl.ANY)] * len(after),
        out_specs=[_HBM_SPEC] * (2 * m),
        input_output_aliases={i: i for i in range(2 * m)},
        compiler_params=pltpu.CompilerParams(has_side_effects=_EFFECT),
    )(*shards, *zones, send_sems, recv_sems, *after)
    return outs[m:]


def _scatter_start(arrs, name):
    n = len(arrs)

    def body(*refs):
        ins, lands = refs[:n], refs[n:2 * n]
        ssem, rsem = refs[2 * n], refs[2 * n + 1]
        token = refs[2 * n + 2 + 2 * n]
        x, y, c = _place()
        for a in range(n):
            for j, (dx, dy) in enumerate(_CHIP_FLIPS):
                px, py = _flip(x, dx), _flip(y, dy)
                pltpu.make_async_remote_copy(
                    src_ref=ins[a].at[2 * px + py], dst_ref=lands[a].at[j], send_sem=ssem.at[3 * a + j],
                    recv_sem=rsem.at[3 * a + j], device_id=(px, py, c), device_id_type=MESH).start()
        token[...] = jnp.zeros_like(token)

    lands = [lax.empty((3,) + a.shape[1:], a.dtype) for a in arrs]
    outs = pl.pallas_call(
        body, name=name,
        out_shape=[pltpu.SemaphoreType.DMA((3 * n,))] * 2 + [pltpu.HBM(a.shape, a.dtype) for a in arrs]
        + [pltpu.HBM(z.shape, z.dtype) for z in lands] + [jax.ShapeDtypeStruct((8, 128), F32)],
        in_specs=[_HBM_SPEC] * (2 * n),
        out_specs=[_SEM_SPEC] * 2 + [_HBM_SPEC] * (2 * n) + [pl.BlockSpec(memory_space=pltpu.VMEM)],
        input_output_aliases={i: 2 + i for i in range(2 * n)},
        compiler_params=pltpu.CompilerParams(has_side_effects=_EFFECT),
    )(*[_in_hbm(a) for a in arrs], *[_in_hbm(z) for z in lands])
    return outs[0], outs[1], outs[2:2 + n], outs[2 + n:2 + 2 * n], outs[-1]


def _scatter_wait(arrs, lands, send_sems, recv_sems, after, name):
    n = len(arrs)

    def body(*refs):
        ins, lz = refs[:n], refs[n:2 * n]
        ssem, rsem = refs[2 * n], refs[2 * n + 1]
        x, y, c = _place()
        for a in range(n):
            for j, (dx, dy) in enumerate(_CHIP_FLIPS):
                px, py = _flip(x, dx), _flip(y, dy)
                cp = pltpu.make_async_remote_copy(
                    src_ref=ins[a].at[2 * px + py], dst_ref=lz[a].at[j], send_sem=ssem.at[3 * a + j],
                    recv_sem=rsem.at[3 * a + j], device_id=(px, py, c), device_id_type=MESH)
                cp.wait_send()
                cp.wait_recv()

    outs = pl.pallas_call(
        body, name=name,
        out_shape=[pltpu.HBM(a.shape, a.dtype) for a in list(arrs) + list(lands)],
        in_specs=[_HBM_SPEC] * (2 * n) + [_SEM_SPEC, _SEM_SPEC, pl.BlockSpec(memory_space=pl.ANY)],
        out_specs=[_HBM_SPEC] * (2 * n),
        input_output_aliases={i: i for i in range(2 * n)},
        compiler_params=pltpu.CompilerParams(has_side_effects=_EFFECT),
    )(*arrs, *lands, send_sems, recv_sems, after)
    return outs[:n], outs[n:]


def _swap_start(arrs, name):
    n = len(arrs)

    def body(*refs):
        ins, lands = refs[:n], refs[n:2 * n]
        ssem, rsem = refs[2 * n], refs[2 * n + 1]
        token = refs[2 * n + 2 + 2 * n]
        x, y, c = _place()
        for a in range(n):
            pltpu.make_async_remote_copy(src_ref=ins[a], dst_ref=lands[a], send_sem=ssem.at[a], recv_sem=rsem.at[a],
                                         device_id=(x, y, 1 - c), device_id_type=MESH).start()
        token[...] = jnp.zeros_like(token)

    lands = [lax.empty(a.shape, a.dtype) for a in arrs]
    outs = pl.pallas_call(
        body, name=name,
        out_shape=[pltpu.SemaphoreType.DMA((n,))] * 2 + [pltpu.HBM(a.shape, a.dtype) for a in arrs]
        + [pltpu.HBM(z.shape, z.dtype) for z in lands] + [jax.ShapeDtypeStruct((8, 128), F32)],
        in_specs=[_HBM_SPEC] * (2 * n),
        out_specs=[_SEM_SPEC] * 2 + [_HBM_SPEC] * (2 * n) + [pl.BlockSpec(memory_space=pltpu.VMEM)],
        input_output_aliases={i: 2 + i for i in range(2 * n)},
        compiler_params=pltpu.CompilerParams(has_side_effects=_EFFECT),
    )(*[_in_hbm(a) for a in arrs], *[_in_hbm(z) for z in lands])
    return outs[0], outs[1], outs[2:2 + n], outs[2 + n:2 + 2 * n], outs[-1]


def _swap_wait(arrs, lands, send_sems, recv_sems, after, name):
    n = len(arrs)

    def body(*refs):
        ins, lz = refs[:n], refs[n:2 * n]
        ssem, rsem = refs[2 * n], refs[2 * n + 1]
        x, y, c = _place()
        for a in range(n):
            cp = pltpu.make_async_remote_copy(src_ref=ins[a], dst_ref=lz[a], send_sem=ssem.at[a], recv_sem=rsem.at[a],
                                              device_id=(x, y, 1 - c), device_id_type=MESH)
            cp.wait_send()
            cp.wait_recv()

    outs = pl.pallas_call(
        body, name=name,
        out_shape=[pltpu.HBM(a.shape, a.dtype) for a in list(arrs) + list(lands)],
        in_specs=[_HBM_SPEC] * (2 * n) + [_SEM_SPEC, _SEM_SPEC, pl.BlockSpec(memory_space=pl.ANY)],
        out_specs=[_HBM_SPEC] * (2 * n),
        input_output_aliases={i: i for i in range(2 * n)},
        compiler_params=pltpu.CompilerParams(has_side_effects=_EFFECT),
    )(*arrs, *lands, send_sems, recv_sems, after)
    return outs[:n], outs[n:]


def _row_tile(rows, cols):
    for tr in (1024, 512, 256, 128, 64, 32, 16, 8):
        if rows % tr == 0 and tr * cols * 4 <= (1 << 20):
            return tr
    return rows


def _partial_sum(g_full, recv, k_idx, name):
    _, r, c = g_full.shape
    tr = _row_tile(r, c)

    def body(k_ref, g_ref, r_ref, o_ref):
        del k_ref
        acc = g_ref[0].astype(F32)
        for j in range(3):
            acc = acc + r_ref[j].astype(F32)
        o_ref[...] = acc

    return pl.pallas_call(
        body, name=name,
        grid_spec=pltpu.PrefetchScalarGridSpec(
            num_scalar_prefetch=1, grid=(r // tr,),
            in_specs=[pl.BlockSpec((1, tr, c), lambda i, k: (k[0], i, 0)),
                      pl.BlockSpec((3, tr, c), lambda i, k: (0, i, 0))],
            out_specs=pl.BlockSpec((tr, c), lambda i, k: (i, 0))),
        out_shape=jax.ShapeDtypeStruct((r, c), F32),
        compiler_params=_cp("parallel"),
    )(k_idx, g_full, recv)


def _adamw(w3, parts, m3, v3, layer, prev, name):
    n_l, r, c = w3.shape
    tr = _row_tile(r, c)
    n_i = r // tr
    n_p = len(parts)
    c1 = 1.0 - ADAM_B1 ** ADAM_STEP
    c2 = 1.0 - ADAM_B2 ** ADAM_STEP
    stacked = [isinstance(p, tuple) for p in parts]

    def body(*refs):
        w_ref, m_ref, v_ref = refs[0:3]
        g_refs = refs[3:3 + n_p]
        go_ref, d_ref, mo_ref, vo_ref = refs[-4:]
        g = None
        for p in range(n_p):
            term = g_refs[p][0] if stacked[p] else g_refs[p][...]
            g = term if g is None else g + term
        w = w_ref[0]
        m = ADAM_B1 * m_ref[0] + (1.0 - ADAM_B1) * g
        v = ADAM_B2 * v_ref[0] + (1.0 - ADAM_B2) * (g * g)
        m_hat = m / c1
        v_hat = v / c2
        go_ref[0] = g
        d_ref[0] = -ADAM_LR * (m_hat / (jnp.sqrt(v_hat) + ADAM_EPS) + ADAM_WD * w)
        mo_ref[0] = m
        vo_ref[0] = v

    blk = pl.BlockSpec((1, tr, c), lambda i: (layer, i, 0))
    in_specs = [blk, blk, blk]
    args = [w3, m3, v3]
    for part in parts:
        if isinstance(part, tuple):
            in_specs.append(pl.BlockSpec((1, tr, c), functools.partial(lambda idx, i: (idx, i, 0), part[1])))
            args.append(part[0])
        else:
            in_specs.append(pl.BlockSpec((tr, c), lambda i: (i, 0)))
            args.append(part)
    aliases = {}
    if prev is not None:
        in_specs += [pl.BlockSpec(memory_space=pl.ANY)] * 4
        aliases = {len(args) + q: q for q in range(4)}
        args += list(prev)
    shp = jax.ShapeDtypeStruct((n_l, r, c), F32)
    return pl.pallas_call(
        body, name=name, grid=(n_i,), in_specs=in_specs, out_specs=[blk] * 4, out_shape=[shp] * 4,
        input_output_aliases=aliases, compiler_params=_cp("parallel"),
    )(*args)


_SMALL_W = 4096
_PACK_ROWS = 352
_N9 = N_MOD * D_MODEL


def _flat_pad(parts, total):
    flat = jnp.concatenate([p.reshape(-1) for p in parts])
    return jnp.concatenate([flat, jnp.zeros((total - flat.shape[0],), F32)])


def kernel(x, c, ctx, c_ctx, w_mod, b_mod, norm_g, ffn1_wi, ffn1_wo, ffn2_wi, ffn2_wo, w_in, w_a2_f, b_a_f, w_a2_b, b_a_b, sink, gla_g, w_out, w_pool, pool_scale, final_g, loss_target, m_c_ctx, m_w_mod, m_b_mod, m_norm_g, m_ffn1_wi, m_ffn1_wo, m_ffn2_wi, m_ffn2_wo, m_w_in, m_w_a2_f, m_b_a_f, m_w_a2_b, m_b_a_b, m_sink, m_gla_g, m_w_out, m_w_pool, m_pool_scale, m_final_g, v_c_ctx, v_w_mod, v_b_mod, v_norm_g, v_ffn1_wi, v_ffn1_wo, v_ffn2_wi, v_ffn2_wo, v_w_in, v_w_a2_f, v_b_a_f, v_w_a2_b, v_b_a_b, v_sink, v_gla_g, v_w_out, v_w_pool, v_pool_scale, v_final_g):
    t_len, l_ctx = x.shape[1], ctx.shape[1]
    tm = ROW_TILE
    pad = (-(t_len + l_ctx)) % tm
    rows0 = t_len + l_ctx + pad
    n_x = t_len // tm
    xi, yi, ci = _place()
    k_me = 2 * xi + yi
    me = 4 * xi + 2 * yi + ci
    mod_cols = w_mod.shape[2]
    n_grp = len(POOL_WINDOWS)

    small_w = _flat_pad([norm_g, w_a2_f, w_a2_b, pool_scale], _SMALL_W).reshape(_SMALL_W // 128, 128)
    shards = [ffn1_wi[0], ffn1_wi[1], ffn1_wo[0], ffn1_wo[1], ffn2_wi[0], ffn2_wi[1], ffn2_wo[0], ffn2_wo[1],
              w_in[0], w_out[0], w_pool[0].reshape(n_grp * w_pool.shape[2], POOL_GROUP)]

    send_src = [s.astype(BF16) for s in shards] + [small_w]
    groups = ([11, 0], [2], [8, 9], [4], [6], [1], [3], [10, 5], [7])
    started = {}

    def gather_start(gs, after):
        members, index, pos = [], [], 0
        for g in gs:
            members += groups[g]
            index.append(tuple(range(pos, pos + len(groups[g]))))
            pos += len(groups[g])
        sems, thru, zones, token = _gather_start([send_src[a] for a in members], tuple(index), after,
                                                 "gather_start_%d" % gs[0])
        for k, (g, idx) in enumerate(zip(gs, index)):
            started[g] = (sems[k], [thru[i] for i in idx], [zones[i] for i in idx])
        return token

    def gather_wait(g, after):
        (ssem, rsem), thru, zones = started[g]
        return dict(zip(groups[g], _gather_wait(thru, zones, ssem, rsem, after, "gather_wait_%d" % g)))

    c_all = _allgather_small(c.reshape(8, 128), "gather_cond").reshape(N_DEV, D_MODEL)
    tok = gather_start((0,), c_all)
    c16 = jnp.concatenate([c_all, c_ctx[None], jnp.zeros((_CROWS - N_DEV - 1, D_MODEL), F32)], axis=0) + tok[0:1, 0:1]
    bias_k = lax.dynamic_slice(b_mod, (0, k_me * mod_cols), (2, mod_cols)).reshape(2, 1, mod_cols)
    mm_k = _adaln_fwd(c16, w_mod, bias_k, "adaln_fwd")
    cs = _rope_tables(t_len, rows0)
    xcat = jnp.concatenate([x[0], ctx[0], jnp.zeros((pad, D_MODEL), F32)], axis=0)
    mm_all = _allgather_small(mm_k.reshape(-1, 128), "gather_mod", (cs, xcat)).reshape(N_DEV, 2, _CROWS, mod_cols)
    mm_full = jnp.concatenate([mm_all[2 * k] for k in range(N_CHIPS)], axis=-1)
    mm_x = lax.dynamic_index_in_dim(mm_full, me, axis=1, keepdims=False)
    mm_c = mm_full[:, N_DEV]
    mods = [jnp.stack([mm_x[l].reshape(N_MOD, D_MODEL), mm_c[l].reshape(N_MOD, D_MODEL)]) for l in range(2)]
    gathered = gather_wait(0, mods[0])
    sw = gathered[11].reshape(N_CHIPS, _SMALL_W)
    ng_n = norm_g.size
    a2_n = w_a2_f.size
    norm_g_full = jnp.concatenate([sw[k, :ng_n].reshape(norm_g.shape) for k in range(N_CHIPS)], axis=-1)
    w_a2_f_full = jnp.concatenate([sw[k, ng_n:ng_n + a2_n].reshape(w_a2_f.shape[1:]) for k in range(N_CHIPS)], axis=-1)
    w_a2_b_full = jnp.concatenate(
        [sw[k, ng_n + a2_n:ng_n + 2 * a2_n].reshape(w_a2_b.shape[1:]) for k in range(N_CHIPS)], axis=-1)
    pscale_full = jnp.concatenate(
        [sw[k, ng_n + 2 * a2_n:ng_n + 2 * a2_n + pool_scale.size] for k in range(N_CHIPS)]).reshape(1, D_MODEL)
    wg2, bias2 = _gate_weights(w_a2_f_full, b_a_f[0], w_a2_b_full, b_a_b[0])
    gla_g2 = gla_g.reshape(1, B_DV)
    final_g2 = final_g.reshape(1, D_MODEL)

    g3 = [norm_g_full[0], norm_g_full[1]]

    w1i, w1o, w2i, w2o = [None, None], [None, None], [None, None], [None, None]
    w1i[0] = gathered[0]
    mods_a = mods[0] + gather_start((1, 2), w1i[0])[0:1, 0:1]
    x1, sv_a1, w1o[0] = _ffn_forward(xcat, g3[0], mods_a, 0, w1i[0],
                                     lambda s: (gather_wait(1, s)[2], gather_start((3, 4), s)), n_x, "l0_ffn1")
    gathered = gather_wait(2, x1)
    w_in_full = jnp.concatenate([gathered[8][k] for k in range(N_CHIPS)], axis=1)
    wcat = _w_in_to_cat(w_in_full)
    w_out_full = gathered[9].reshape(D_MODEL, D_MODEL)
    pace_groups = {"proj": (5, 6), "attn": (7, 8)}
    no_dep = jnp.zeros((8, 128), F32)
    x2, sv_am = _mixer_ab_forward(
        x1, g3[0], mods[0], wcat, wg2, bias2, sink[0], gla_g2, w_out_full, cs, t_len, l_ctx, n_x,
        lambda tag, res_: gather_start(pace_groups[tag], res_) if tag in pace_groups else no_dep)
    mods_a = mods[0]
    w2i[0], w2o[0] = gather_wait(3, x2)[4], gather_wait(4, x2)[6]
    x3, sv_a2, _ = _ffn_forward(x2, g3[0], mods_a, 2, w2i[0], lambda s: (w2o[0], None), n_x, "l0_ffn2")
    w1i[1], w1o[1] = gather_wait(5, x3)[1], gather_wait(6, x3)[3]
    x4, sv_b1, _ = _ffn_forward(x3, g3[1], mods[1], 0, w1i[1], lambda s: (w1o[1], None), n_x, "l1_ffn1")
    gathered = gather_wait(7, x4)
    w2i[1] = gathered[5]
    wp_full = gathered[10].reshape(N_CHIPS, n_grp, -1, POOL_GROUP).transpose(1, 0, 2, 3).reshape(
        n_grp, POOL_GROUP, POOL_GROUP)
    x5, sv_bm = _mixer_pool_forward(x4, g3[1], mods[1], wp_full, pscale_full, t_len)
    x6, sv_b2, w2o[1] = _ffn_forward(x5, g3[1], mods[1], 2, w2i[1], lambda s: (gather_wait(8, s)[7], None), n_x,
                                     "l1_ffn2")
    dx6, loss_part, d_final_g = _final_loss(x6, final_g2, loss_target[0], "final_loss")
    loss = lax.psum(loss_part[0, 0], ("x", "y", "c"))

    sent = []

    def sender(weight, layer):
        def send(grad, tag):
            nm = "%s_%s_%d" % (weight, tag, layer)
            ssem, rsem, thru, lands, token = _scatter_start([grad], "scatter_start_" + nm)
            sent.append((nm, weight + "_" + tag if tag else weight, layer, thru, lands, ssem, rsem))
            return token[0:1, 0:1]
        return send

    dx5, st_b2, dg_b2 = _ffn_backward(dx6, sv_b2, g3[1], mods[1], 2, w2i[1], w2o[1], n_x, sender("ffn2", 1),
                                      "l1_ffn2_b")
    dx4, st_bm, dg_bm, d_pscale, d_wp = _mixer_pool_backward(dx5, sv_bm, g3[1], mods[1], wp_full, pscale_full, t_len)
    d_wp4 = d_wp.reshape(n_grp, N_CHIPS, -1, POOL_GROUP).transpose(1, 0, 2, 3).reshape(N_CHIPS, -1, POOL_GROUP)
    mods1 = mods[1] + sender("w_pool", 0)(d_wp4, "")
    dx3, st_b1, dg_b1 = _ffn_backward(dx4, sv_b1, g3[1], mods1, 0, w1i[1], w1o[1], n_x, sender("ffn1", 1),
                                      "l1_ffn1_b")
    dx2, st_a2, dg_a2 = _ffn_backward(dx3, sv_a2, g3[0], mods[0], 2, w2i[0], w2o[0], n_x, sender("ffn2", 0),
                                      "l0_ffn2_b")
    dx1, st_am, dg_am, d_wcat, d_wg2, d_bias2, d_sink, d_glag, d_wout = _mixer_ab_backward(
        dx2, sv_am, g3[0], mods[0], wcat, wg2, bias2, sink[0], gla_g2, w_out_full, cs, t_len, l_ctx, n_x)
    d_w_in4 = _cat_to_w_in(d_wcat).reshape(D_MODEL, N_CHIPS, -1).transpose(1, 0, 2)
    mods0 = mods[0] + sender("w_in", 0)(d_w_in4, "") + sender("w_out", 0)(d_wout.reshape(N_CHIPS, -1, D_MODEL), "")
    dx0, st_a1, dg_a1 = _ffn_backward(dx1, sv_a1, g3[0], mods0, 0, w1i[0], w1o[0], n_x, sender("ffn1", 0),
                                      "l0_ffn1_b", out_tiles=n_x)
    grad_x = dx0[None]

    def as3(a):
        n_l = a.shape[0] if a.ndim == 3 else 1
        return a.reshape(n_l, -1, a.shape[-1])

    res = {}
    big_w = {"ffn1_wi": (ffn1_wi, m_ffn1_wi, v_ffn1_wi), "ffn1_wo": (ffn1_wo, m_ffn1_wo, v_ffn1_wo),
             "ffn2_wi": (ffn2_wi, m_ffn2_wi, v_ffn2_wi), "ffn2_wo": (ffn2_wo, m_ffn2_wo, v_ffn2_wo),
             "w_in": (w_in, m_w_in, v_w_in), "w_out": (w_out, m_w_out, v_w_out), "w_pool": (w_pool, m_w_pool, v_w_pool)}
    k_idx = k_me.reshape(1).astype(jnp.int32)
    chain = dx0
    def finish(swap, after):
        lo, hi, s_sem, r_sem, s_thru, s_lands = swap
        mine, other = _swap_wait(s_thru, s_lands, s_sem, r_sem, after, "swap_wait_%d" % lo)
        last = after
        for (nm, wname, layer, _, _, _, _), p, q in zip(sent[lo:hi], mine, other):
            w, m, v = big_w[wname]
            res[wname] = _adamw(as3(w), [p, q], as3(m), as3(v), layer, res.get(wname),
                                "adamw_%s_%d" % (wname, layer))
            last = res[wname][3]
        return last

    swap = None
    for lo, hi in ((0, 2), (2, 5), (5, 7), (7, 9), (9, 11)):
        partial = []
        for nm, wname, layer, thru, lands, ssem, rsem in sent[lo:hi]:
            mine, recv = _scatter_wait(thru, lands, ssem, rsem, chain, "scatter_wait_" + nm)
            partial.append(_partial_sum(mine[0], recv[0], k_idx, "partial_sum_" + nm))
        s_sem, r_sem, s_thru, s_lands, token = _swap_start(partial, "swap_start_%d" % lo)
        if swap is not None:
            chain = finish(swap, token)
        swap = (lo, hi, s_sem, r_sem, s_thru, s_lands)

    def mod_row(st1, dg1, stm, dgm, st2, dg2, s):
        return jnp.concatenate([st1[s, 0], st1[s, 1], dg1[s, 0], stm[s, 0], stm[s, 1], dgm[s, 0],
                                st2[s, 0], st2[s, 1], dg2[s, 0]])

    dg_bm2 = jnp.concatenate([dg_bm, jnp.zeros_like(dg_bm)], axis=0)[:, None, :]
    d_mm_x0 = mod_row(st_a1, dg_a1, st_am, dg_am, st_a2, dg_a2, 0)
    d_mm_x1 = mod_row(st_b1, dg_b1, st_bm, dg_bm2, st_b2, dg_b2, 0)
    d_mm_c0 = mod_row(st_a1, dg_a1, st_am, dg_am, st_a2, dg_a2, 1)
    d_norm_g = jnp.stack([jnp.stack([st[0, 2] + st[1, 2] for st in (st_a1, st_am, st_a2)]),
                          jnp.stack([st[0, 2] + st[1, 2] for st in (st_b1, st_bm, st_b2)])])
    rk = B_GATE_RANK
    pack = _flat_pad([d_mm_x0, d_mm_x1, d_mm_c0, d_norm_g, d_bias2, d_wg2[0:rk, 0:256], d_wg2[rk:2 * rk, 256:512],
                      d_sink[:, 0], jnp.zeros((120,), F32), d_glag, d_pscale, d_final_g],
                     _PACK_ROWS * 128).reshape(_PACK_ROWS, 128)
    pack = pack + 0.0 * chain[0, 0:1, 0:1]
    pack_all = _allgather_small(pack, "gather_small_grads")
    tot = _sum_devices(pack_all, "sum_small_grads").reshape(-1)
    rows_all = pack_all.reshape(N_DEV, -1)
    o = 3 * _N9
    g_norm_g_full = tot[o:o + 6 * D_MODEL].reshape(2, 3, D_MODEL)
    o += 6 * D_MODEL
    g_bias2 = tot[o:o + 512]
    o += 512
    g_w_a2_f_full = tot[o:o + rk * 256].reshape(rk, 256)
    o += rk * 256
    g_w_a2_b_full = tot[o:o + rk * 256].reshape(rk, 256)
    o += rk * 256
    g_sink = tot[o:o + A_HEADS]
    o += 128
    g_gla_g = tot[o:o + B_DV]
    o += B_DV
    g_pscale_full = tot[o:o + D_MODEL]
    o += D_MODEL
    g_final_g = tot[o:o + D_MODEL]
    d_mmc_tot = tot[2 * _N9:3 * _N9]
    g_b_mod = jnp.stack([tot[0:_N9] + d_mmc_tot, tot[_N9:2 * _N9]])

    zrows = jnp.zeros((_CROWS - N_DEV - 1, _N9), F32)
    d16 = jnp.stack([jnp.concatenate([rows_all[:, 0:_N9], d_mmc_tot[None], zrows], axis=0),
                     jnp.concatenate([rows_all[:, _N9:2 * _N9], jnp.zeros((1, _N9), F32), zrows], axis=0)])
    d16_k = lax.dynamic_slice(d16, (0, 0, k_me * mod_cols), (2, _CROWS, mod_cols))
    dmmc_k = lax.dynamic_slice(d_mmc_tot, (k_me * mod_cols,), (mod_cols,)).reshape(1, mod_cols)
    g_w_mod, c_part = _adaln_bwd(c16, d16_k, w_mod, dmmc_k, "adaln_bwd")
    c_parts = _allgather_small(c_part.reshape(8, 128), "gather_cctx")
    g_c_ctx = _cctx_grad(c_parts, c_ctx.reshape(8, 128), "cctx_grad").reshape(D_MODEL)

    def small(w, g, m, v, shape3, nm):
        return [o_.reshape(w.shape) for o_ in _adamw(w.reshape(shape3), [g.reshape(shape3[1:])],
                                                    m.reshape(shape3), v.reshape(shape3), 0, None, "adamw_" + nm)]

    def own(a, axis, size):
        return lax.dynamic_slice_in_dim(a, k_me * size, size, axis=axis)

    res["c_ctx"] = small(c_ctx, g_c_ctx, m_c_ctx, v_c_ctx, (1, 8, 128), "c_ctx")
    upd = _adamw(w_mod, [(g_w_mod, 1)], m_w_mod, v_w_mod, 1, None, "adamw_w_mod_1")
    res["w_mod"] = _adamw(w_mod, [(g_w_mod, 0)], m_w_mod, v_w_mod, 0, upd, "adamw_w_mod_0")
    res["b_mod"] = small(b_mod, g_b_mod, m_b_mod, v_b_mod, (1, 2, _N9), "b_mod")
    res["norm_g"] = small(norm_g, own(g_norm_g_full, 2, norm_g.shape[2]), m_norm_g, v_norm_g,
                          (1, 6, norm_g.shape[2]), "norm_g")
    res["w_a2_f"] = small(w_a2_f, own(g_w_a2_f_full, 1, w_a2_f.shape[2]), m_w_a2_f, v_w_a2_f,
                          (1, rk, w_a2_f.shape[2]), "w_a2_f")
    res["b_a_f"] = small(b_a_f, g_bias2[0:256], m_b_a_f, v_b_a_f, (1, 1, 256), "b_a_f")
    res["w_a2_b"] = small(w_a2_b, own(g_w_a2_b_full, 1, w_a2_b.shape[2]), m_w_a2_b, v_w_a2_b,
                          (1, rk, w_a2_b.shape[2]), "w_a2_b")
    res["b_a_b"] = small(b_a_b, g_bias2[256:512], m_b_a_b, v_b_a_b, (1, 1, 256), "b_a_b")
    res["sink"] = small(sink, g_sink, m_sink, v_sink, (1, 1, A_HEADS), "sink")
    res["gla_g"] = small(gla_g, g_gla_g, m_gla_g, v_gla_g, (1, 1, B_DV), "gla_g")
    res["pool_scale"] = small(pool_scale, own(g_pscale_full, 0, pool_scale.shape[1]), m_pool_scale, v_pool_scale,
                              (1, 1, pool_scale.shape[1]), "pool_scale")
    res["final_g"] = small(final_g, g_final_g, m_final_g, v_final_g, (1, 8, 128), "final_g")
    finish(swap, res["final_g"][0])
    for wname, (w, _, _) in big_w.items():
        res[wname] = [o_.reshape(w.shape) for o_ in res[wname]]

    names = ["c_ctx", "w_mod", "b_mod", "norm_g", "ffn1_wi", "ffn1_wo", "ffn2_wi", "ffn2_wo", "w_in", "w_a2_f",
             "b_a_f", "w_a2_b", "b_a_b", "sink", "gla_g", "w_out", "w_pool", "pool_scale", "final_g"]
    outs = [loss, grad_x]
    for field in range(4):
        outs += [res[nm][field] for nm in names]
    return tuple(outs)
```

```python
import functools

import jax
import jax.numpy as jnp
from jax import lax
from jax.experimental import pallas as pl
from jax.experimental.pallas import tpu as pltpu

F32 = jnp.float32
BF16 = jnp.bfloat16

D_MODEL = 1024
N_MOD = 9
D_FF = 2816
RMS_EPS = 1e-6
A_HEADS = 8
A_KV_HEADS = 2
A_HEAD_DIM = 64
WINDOW = 128
ROPE_BASE = 10000.0
GRID_W = 64
B_HEADS = 4
B_DK = 64
B_DV = 128
B_GATE_RANK = 16
B_GATE_NORM = 16.0
B_CHUNK = 64
POOL_WINDOWS = (2, 4, 8, 16)
POOL_GROUP = D_MODEL // len(POOL_WINDOWS)
PROJ_DIM = 2336

ADAM_LR = 0.001
ADAM_B1 = 0.9
ADAM_B2 = 0.999
ADAM_EPS = 1e-08
ADAM_WD = 0.01
ADAM_STEP = 10

N_CHIPS = 4
N_DEV = 8
ROW_TILE = 512
VMEM_LIMIT_BYTES = 56 * 1024 * 1024
MESH = pl.DeviceIdType.MESH

ZC_Q, ZC_QK, ZC_V, ZC_R, ZC_KV, ZC_G, ZC_W = 0, 512, 1024, 1536, 2048, 2304, 2432


def _cp(*sem):
    return pltpu.CompilerParams(dimension_semantics=sem if sem else None, vmem_limit_bytes=VMEM_LIMIT_BYTES)


def _dot(a, b):
    return jnp.dot(a, b, preferred_element_type=F32)


def _dot_nt(a, b):
    return lax.dot_general(a, b, (((1,), (1,)), ((), ())), preferred_element_type=F32)


def _dot_tn(a, b):
    return lax.dot_general(a, b, (((0,), (0,)), ((), ())), preferred_element_type=F32)


def _dot_tn_hi(a, b):
    return lax.dot_general(a, b, (((0,), (0,)), ((), ())), preferred_element_type=F32,
                           precision=lax.Precision.HIGHEST)


def _sigmoid(x):
    return 1.0 / (1.0 + jnp.exp(-x))


MXU_COLS = 256


def _col_chunks(n):
    return [(c0, min(MXU_COLS, n - c0)) for c0 in range(0, n, MXU_COLS)]


WIDE_ROW_TILE = 1024


def _matmul_row_tile(rows, n_x):
    if rows % WIDE_ROW_TILE == 0 and n_x * ROW_TILE >= rows:
        return WIDE_ROW_TILE
    return ROW_TILE


def _resident(block_shape, index_map):
    return pl.BlockSpec(block_shape, index_map, pipeline_mode=pl.Buffered(1))


def _stream_of(i, n_x):
    return jnp.where(i >= n_x, 1, 0)


def _rms_mod_fwd(x, g3, mods, j, n_x, out_dtype, name):
    rows = x.shape[0]
    tm = ROW_TILE
    n_i = rows // tm

    def body(x_ref, g_ref, m_ref, o_ref):
        xv = x_ref[...]
        r = lax.rsqrt(jnp.mean(xv * xv, axis=-1, keepdims=True) + RMS_EPS)
        g = g_ref[j:j + 1, :]
        shift = m_ref[0, 3 * j:3 * j + 1, :]
        scale = m_ref[0, 3 * j + 1:3 * j + 2, :]
        o_ref[...] = (((xv * r) * g) * (1.0 + scale) + shift).astype(out_dtype)

    return pl.pallas_call(
        body, name=name, grid=(n_i,),
        in_specs=[pl.BlockSpec((tm, D_MODEL), lambda i: (i, 0)),
                  pl.BlockSpec((3, D_MODEL), lambda i: (0, 0)),
                  pl.BlockSpec((1, N_MOD, D_MODEL), lambda i: (_stream_of(i, n_x), 0, 0))],
        out_specs=pl.BlockSpec((tm, D_MODEL), lambda i: (i, 0)),
        out_shape=jax.ShapeDtypeStruct((rows, D_MODEL), out_dtype),
        compiler_params=_cp("parallel"),
    )(x, g3, mods)


def _rms_mod_bwd_tail(dh, xv, g, scale, stream, acc_ref, first):
    r = lax.rsqrt(jnp.mean(xv * xv, axis=-1, keepdims=True) + RMS_EPS)
    xhat = xv * r
    t1 = jnp.sum(dh, axis=0, keepdims=True)
    t2 = jnp.sum(dh * xhat, axis=0, keepdims=True)
    stats = jnp.concatenate([t1, t2 * g, t2 * (1.0 + scale)], axis=0)

    @pl.when(first)
    def _():
        acc_ref[...] = jnp.zeros_like(acc_ref)

    acc_ref[pl.ds(stream, 1)] += stats[None]
    dxh = dh * (g * (1.0 + scale))
    return r * (dxh - xhat * jnp.mean(dxh * xhat, axis=-1, keepdims=True))


def _ffn_up(x, g3, mods, jmod, n_x, w4, name):
    rows = x.shape[0]
    h = w4.shape[2]
    tm = _matmul_row_tile(rows, n_x)
    n_i = rows // tm

    def body(x_ref, g_ref, m_ref, wa_ref, wu_ref, hn_ref, au_ref, s_ref):
        xv = x_ref[...]
        r = lax.rsqrt(jnp.mean(xv * xv, axis=-1, keepdims=True) + RMS_EPS)
        g = g_ref[jmod:jmod + 1, :]
        shift = m_ref[0, 3 * jmod:3 * jmod + 1, :]
        scale = m_ref[0, 3 * jmod + 1:3 * jmod + 2, :]
        hv = (((xv * r) * g) * (1.0 + scale) + shift).astype(BF16)

        @pl.when(pl.program_id(0) == 0)
        def _():
            hn_ref[...] = hv

        for c0, cw in _col_chunks(h):
            cols = slice(c0, c0 + cw)
            a = _dot(hv, wa_ref[0, :, cols])
            u = _dot(hv, wu_ref[0, :, cols])
            sg = _sigmoid(a)
            silu = a * sg
            au_ref[0, :, cols] = (u * (sg * (1.0 + a * (1.0 - sg)))).astype(BF16)
            au_ref[1, :, cols] = silu.astype(BF16)
            s_ref[:, cols] = (silu * u).astype(BF16)

    return pl.pallas_call(
        body, name=name, grid=(2, n_i),
        in_specs=[pl.BlockSpec((tm, D_MODEL), lambda j, i: (i, 0)),
                  pl.BlockSpec((3, D_MODEL), lambda j, i: (0, 0)),
                  pl.BlockSpec((1, N_MOD, D_MODEL), lambda j, i: (_stream_of(i, n_x), 0, 0)),
                  pl.BlockSpec((1, D_MODEL, h), lambda j, i: (j, 0, 0)),
                  pl.BlockSpec((1, D_MODEL, h), lambda j, i: (j + 2, 0, 0))],
        out_specs=[pl.BlockSpec((tm, D_MODEL), lambda j, i: (jnp.where(j == 0, i, n_i - 1), 0)),
                   pl.BlockSpec((2, tm, h), lambda j, i: (0, i, j)),
                   pl.BlockSpec((tm, h), lambda j, i: (i, j))],
        out_shape=[jax.ShapeDtypeStruct((rows, D_MODEL), BF16),
                   jax.ShapeDtypeStruct((2, rows, 2 * h), BF16),
                   jax.ShapeDtypeStruct((rows, 2 * h), BF16)],
        compiler_params=_cp("arbitrary", "arbitrary"),
    )(x, g3, mods, w4, w4)


def _matmul_resid(a, w, xres, mods, gate_idx, coef, n_x, rows, name):
    k = a.shape[1]
    tm = _matmul_row_tile(rows, n_x)
    n_i = rows // tm

    def body(a_ref, w_ref, x_ref, m_ref, o_ref, f_ref):
        av = a_ref[...]
        for c0, cw in _col_chunks(D_MODEL):
            cols = slice(c0, c0 + cw)
            f = _dot(av, w_ref[:, cols])
            f_ref[:, cols] = f.astype(BF16)
            o_ref[:, cols] = x_ref[:, cols] + (coef * m_ref[0, gate_idx:gate_idx + 1, cols]) * f

    return pl.pallas_call(
        body, name=name, grid=(n_i,),
        in_specs=[pl.BlockSpec((tm, k), lambda i: (i, 0)),
                  _resident((k, D_MODEL), lambda i: (0, 0)),
                  pl.BlockSpec((tm, D_MODEL), lambda i: (i, 0)),
                  pl.BlockSpec((1, N_MOD, D_MODEL), lambda i: (_stream_of(i, n_x), 0, 0))],
        out_specs=[pl.BlockSpec((tm, D_MODEL), lambda i: (i, 0)),
                   pl.BlockSpec((tm, D_MODEL), lambda i: (i, 0))],
        out_shape=[jax.ShapeDtypeStruct((rows, D_MODEL), F32),
                   jax.ShapeDtypeStruct((rows, D_MODEL), BF16)],
        compiler_params=_cp("parallel"),
    )(a, w, xres, mods)


def _gate_dy(dout, f, mods, gate_idx, coef, n_x, rows, w, name):
    tm = ROW_TILE
    n_i = rows // tm
    n_out = w.shape[0]

    def body(d_ref, f_ref, m_ref, w_ref, dy_ref, da_ref, acc_ref):
        i = pl.program_id(0)
        dv = d_ref[...]
        gate = m_ref[0, gate_idx:gate_idx + 1, :]
        dyb = (dv * (coef * gate)).astype(BF16)
        dy_ref[...] = dyb
        da_ref[...] = _dot_nt(dyb, w_ref[...])

        @pl.when(i == 0)
        def _():
            acc_ref[...] = jnp.zeros_like(acc_ref)

        part = coef * jnp.sum(dv * f_ref[...].astype(F32), axis=0, keepdims=True)
        acc_ref[pl.ds(_stream_of(i, n_x), 1)] += part[None]

    return pl.pallas_call(
        body, name=name, grid=(n_i,),
        in_specs=[pl.BlockSpec((tm, D_MODEL), lambda i: (i, 0)),
                  pl.BlockSpec((tm, D_MODEL), lambda i: (i, 0)),
                  pl.BlockSpec((1, N_MOD, D_MODEL), lambda i: (_stream_of(i, n_x), 0, 0)),
                  pl.BlockSpec((n_out, D_MODEL), lambda i: (0, 0))],
        out_specs=[pl.BlockSpec((tm, D_MODEL), lambda i: (i, 0)),
                   pl.BlockSpec((tm, n_out), lambda i: (i, 0)),
                   pl.BlockSpec((2, 1, D_MODEL), lambda i: (0, 0, 0))],
        out_shape=[jax.ShapeDtypeStruct((rows, D_MODEL), BF16),
                   jax.ShapeDtypeStruct((rows, n_out), F32),
                   jax.ShapeDtypeStruct((2, 1, D_MODEL), F32)],
        compiler_params=_cp("arbitrary"),
    )(dout, f, mods, w)


def _ffn_bwd_dz(dout, f, mods, gate_idx, coef, n_x, wo2, au, name):
    rows = dout.shape[0]
    h = wo2.shape[1]
    tm = ROW_TILE
    n_i = rows // tm

    def body(d_ref, f_ref, m_ref, wo_ref, au_ref, dy_ref, dz_ref, acc_ref):
        j, i = pl.program_id(0), pl.program_id(1)
        dv = d_ref[...]
        gate = m_ref[0, gate_idx:gate_idx + 1, :]
        dyb = (dv * (coef * gate)).astype(BF16)

        @pl.when((j == 0) & (i == 0))
        def _():
            acc_ref[...] = jnp.zeros_like(acc_ref)

        @pl.when(j == 0)
        def _():
            dy_ref[...] = dyb
            part = coef * jnp.sum(dv * f_ref[...].astype(F32), axis=0, keepdims=True)
            acc_ref[pl.ds(_stream_of(i, n_x), 1)] += part[None]

        for c0, cw in _col_chunks(h):
            cols = slice(c0, c0 + cw)
            ds = _dot_nt(dyb, wo_ref[0, cols, :])
            dz_ref[0, :, cols] = (ds * au_ref[0, :, cols].astype(F32)).astype(BF16)
            dz_ref[1, :, cols] = (ds * au_ref[1, :, cols].astype(F32)).astype(BF16)

    return pl.pallas_call(
        body, name=name, grid=(2, n_i),
        in_specs=[pl.BlockSpec((tm, D_MODEL), lambda j, i: (i, 0)),
                  pl.BlockSpec((tm, D_MODEL), lambda j, i: (jnp.where(j == 0, i, n_i - 1), 0)),
                  pl.BlockSpec((1, N_MOD, D_MODEL), lambda j, i: (_stream_of(i, n_x), 0, 0)),
                  pl.BlockSpec((1, h, D_MODEL), lambda j, i: (j, 0, 0)),
                  pl.BlockSpec((2, tm, h), lambda j, i: (0, i, j))],
        out_specs=[pl.BlockSpec((tm, D_MODEL), lambda j, i: (jnp.where(j == 0, i, n_i - 1), 0)),
                   pl.BlockSpec((2, tm, h), lambda j, i: (0, i, j)),
                   pl.BlockSpec((2, 1, D_MODEL), lambda j, i: (0, 0, 0))],
        out_shape=[jax.ShapeDtypeStruct((rows, D_MODEL), BF16),
                   jax.ShapeDtypeStruct((2, rows, 2 * h), BF16),
                   jax.ShapeDtypeStruct((2, 1, D_MODEL), F32)],
        compiler_params=_cp("arbitrary", "arbitrary"),
    )(dout, f, mods, wo2, au)


def _token_tile(rows):
    for tk in (2048, 1536, 1024):
        if rows % tk == 0:
            return tk
    return ROW_TILE


def _matmul_tn(a, b, a_spec, b_spec, out_shape, out_spec, grid, name):
    nd_a = len(a_spec.block_shape)
    nd_b = len(b_spec.block_shape)
    nd_o = len(out_spec.block_shape)
    k_axis = len(grid) - 1
    n_k = grid[k_axis]

    def body(a_ref, b_ref, o_ref, acc_ref):
        av = a_ref[(0,) * (nd_a - 2)]
        bv = b_ref[(0,) * (nd_b - 2)]
        part = _dot_tn(av, bv)
        k = pl.program_id(k_axis)

        @pl.when(k == 0)
        def _():
            acc_ref[...] = part

        @pl.when(k > 0)
        def _():
            acc_ref[...] += part

        @pl.when(k == n_k - 1)
        def _():
            o_ref[(0,) * (nd_o - 2)] = acc_ref[...].astype(BF16)

    return pl.pallas_call(
        body, name=name, grid=grid, in_specs=[a_spec, b_spec], out_specs=out_spec,
        out_shape=jax.ShapeDtypeStruct(out_shape, BF16),
        scratch_shapes=[pltpu.VMEM(tuple(out_spec.block_shape[-2:]), F32)],
        compiler_params=_cp(*(("arbitrary",) * len(grid))),
    )(a, b)


def _bwd_dx(pairs, x, dres, dres_tiles, g3, mods, j, n_x, name, out_tiles=None):
    rows = x.shape[0]
    tm = ROW_TILE
    n_i = rows // tm
    n_o = n_i if out_tiles is None else out_tiles
    n_p = len(pairs)
    nds = [(len(p[1].block_shape), len(p[3].block_shape)) for p in pairs]

    def body(*refs):
        dz_refs = refs[0:2 * n_p:2]
        w_refs = refs[1:2 * n_p:2]
        x_ref, dres_ref, g_ref, m_ref, dx_ref, acc_ref = refs[2 * n_p:]
        i = pl.program_id(0)
        dzs = [dz_refs[p][(0,) * (nds[p][0] - 2)] for p in range(n_p)]
        pieces = []
        for c0, cw in _col_chunks(D_MODEL):
            acc = None
            for p in range(n_p):
                lead = (0,) * (nds[p][1] - 2)
                part = _dot_nt(dzs[p], w_refs[p][lead + (slice(c0, c0 + cw), slice(None))])
                acc = part if acc is None else acc + part
            pieces.append(acc)
        dh = jnp.concatenate(pieces, axis=1)
        g = g_ref[j:j + 1, :]
        scale = m_ref[0, 3 * j + 1:3 * j + 2, :]
        dx = _rms_mod_bwd_tail(dh, x_ref[...], g, scale, _stream_of(i, n_x), acc_ref, i == 0)
        dres_v = jnp.where(i < dres_tiles, dres_ref[...], 0.0)

        @pl.when(i < n_o)
        def _():
            dx_ref[...] = dres_v + dx

    in_specs, args = [], []
    for dz, dz_spec, w, w_spec in pairs:
        in_specs += [dz_spec, w_spec]
        args += [dz, w]
    in_specs += [pl.BlockSpec((tm, D_MODEL), lambda i: (i, 0)),
                 pl.BlockSpec((tm, D_MODEL), lambda i: (jnp.minimum(i, dres_tiles - 1), 0)),
                 pl.BlockSpec((3, D_MODEL), lambda i: (0, 0)),
                 pl.BlockSpec((1, N_MOD, D_MODEL), lambda i: (_stream_of(i, n_x), 0, 0))]
    args += [x, dres, g3, mods]
    return pl.pallas_call(
        body, name=name, grid=(n_i,), in_specs=in_specs,
        out_specs=[pl.BlockSpec((tm, D_MODEL), lambda i: (jnp.minimum(i, n_o - 1), 0)),
                   pl.BlockSpec((2, 3, D_MODEL), lambda i: (0, 0, 0))],
        out_shape=[jax.ShapeDtypeStruct((n_o * tm, D_MODEL), F32),
                   jax.ShapeDtypeStruct((2, 3, D_MODEL), F32)],
        compiler_params=_cp("arbitrary"),
    )(*args)


def _ffn_forward(x, g3, mods, j, w4_in, w4_out_of, n_x, name):
    rows = x.shape[0]
    hn, au, s = _ffn_up(x, g3, mods, j, n_x, w4_in, name + "_up")
    w4_out, dep = w4_out_of(s)
    if dep is not None:
        mods = mods + dep[0:1, 0:1]
    wo = w4_out.reshape(D_FF, D_MODEL)
    out, f = _matmul_resid(s, wo, x, mods, 3 * j + 2, 0.5, n_x, rows, name + "_down")
    return out, (x, hn, au, s, f), w4_out


def _ffn_backward(dout, saved, g3, mods, j, w4_in, w4_out, n_x, send, name, out_tiles=None):
    x, hn, au, s, f = saved
    rows = x.shape[0]
    tm = ROW_TILE
    n_i = rows // tm
    h = w4_in.shape[2]
    wo2 = w4_out.reshape(2, h, D_MODEL)
    dy, dz, dgate = _ffn_bwd_dz(dout, f, mods, 3 * j + 2, 0.5, n_x, wo2, au, name + "_dz")
    tk = _token_tile(rows)
    n_k = rows // tk
    d_wi = _matmul_tn(
        hn, dz, pl.BlockSpec((tk, D_MODEL), lambda q, k: (k, 0)),
        pl.BlockSpec((1, tk, h), lambda q, k: (q // 2, k, q % 2)),
        (4, D_MODEL, h), pl.BlockSpec((1, D_MODEL, h), lambda q, k: (q, 0, 0)), (4, n_k), name + "_dwi")
    d_wo = _matmul_tn(
        s, dy, pl.BlockSpec((tk, h), lambda n, k: (k, n)), pl.BlockSpec((tk, D_MODEL), lambda n, k: (k, 0)),
        (D_FF, D_MODEL), pl.BlockSpec((h, D_MODEL), lambda n, k: (n, 0)), (2, n_k), name + "_dwo")
    mods = mods + send({"wi": d_wi, "wo": d_wo.reshape(w4_out.shape)})
    pairs = [(dz, pl.BlockSpec((1, tm, h), functools.partial(lambda q, i: (q // 2, i, q % 2), q)),
              w4_in, pl.BlockSpec((1, D_MODEL, h), functools.partial(lambda q, i: (q, 0, 0), q)))
             for q in range(4)]
    dx, stats = _bwd_dx(pairs, x, dout, n_i, g3, mods, j, n_x, name + "_dx", out_tiles)
    return dx, stats, dgate


def _rope_tables(t_len, rows):
    n = A_HEAD_DIM // 4
    freqs = ROPE_BASE ** (-jnp.arange(n, dtype=F32) / n)
    t = jnp.arange(t_len)
    ang_r = (t // GRID_W).astype(F32)[:, None] * freqs
    ang_c = (t % GRID_W).astype(F32)[:, None] * freqs
    cos = jnp.concatenate([jnp.cos(ang_r), jnp.cos(ang_r), jnp.cos(ang_c), jnp.cos(ang_c)], axis=1)
    sin = jnp.concatenate([-jnp.sin(ang_r), jnp.sin(ang_r), -jnp.sin(ang_c), jnp.sin(ang_c)], axis=1)
    cos = jnp.concatenate([cos, jnp.ones((rows - t_len, A_HEAD_DIM), F32)], axis=0)
    sin = jnp.concatenate([sin, jnp.zeros((rows - t_len, A_HEAD_DIM), F32)], axis=0)
    return jnp.concatenate([cos, cos, sin, sin], axis=1)


def _swap16(x):
    n = x.shape[1]
    lane = lax.broadcasted_iota(jnp.int32, x.shape, 1)
    first = jnp.bitwise_and(lane, 16) == 0
    return jnp.where(first, pltpu.roll(x, n - 16, 1), pltpu.roll(x, 16, 1))


def _log_sigmoid(x):
    return jnp.minimum(x, 0.0) - jnp.log(1.0 + jnp.exp(-jnp.abs(x)))


def _proj_fwd(x, g3, mods, n_x, wcat, wg2, bias2, cs, name):
    rows = x.shape[0]
    tm = ROW_TILE

    def body(x_ref, g_ref, m_ref, w_ref, wg_ref, b_ref, cs_ref, h_ref, zc_ref, la_ref):
        xv = x_ref[...]
        r = lax.rsqrt(jnp.mean(xv * xv, axis=-1, keepdims=True) + RMS_EPS)
        hv = (((xv * r) * g_ref[1:2, :]) * (1.0 + m_ref[0, 4:5, :]) + m_ref[0, 3:4, :]).astype(BF16)
        h_ref[...] = hv
        z = _dot(hv, w_ref[...])
        cos = cs_ref[:, 0:128]
        sin = cs_ref[:, 128:256]
        cosq = jnp.concatenate([cos] * 4, axis=1)
        sinq = jnp.concatenate([sin] * 4, axis=1)
        q = z[:, ZC_Q:ZC_QK]
        zc_ref[:, ZC_Q:ZC_QK] = q * cosq + _swap16(q) * sinq
        zc_ref[:, ZC_QK:ZC_KV] = z[:, ZC_QK:ZC_KV]
        kk = z[:, ZC_KV:ZC_KV + 128]
        zc_ref[:, ZC_KV:ZC_KV + 128] = kk * cos + _swap16(kk) * sin
        zc_ref[:, ZC_KV + 128:ZC_W] = z[:, ZC_KV + 128:ZC_W]
        zg = z[:, ZC_G:ZC_W]
        pre = _dot(zg.astype(BF16), wg_ref[...]) + b_ref[...]
        la_ref[...] = _log_sigmoid(pre) / B_GATE_NORM

    return pl.pallas_call(
        body, name=name, grid=(rows // tm,),
        in_specs=[pl.BlockSpec((tm, D_MODEL), lambda i: (i, 0)),
                  pl.BlockSpec((3, D_MODEL), lambda i: (0, 0)),
                  pl.BlockSpec((1, N_MOD, D_MODEL), lambda i: (_stream_of(i, n_x), 0, 0)),
                  pl.BlockSpec((D_MODEL, ZC_W), lambda i: (0, 0)),
                  pl.BlockSpec((128, 512), lambda i: (0, 0)),
                  pl.BlockSpec((1, 512), lambda i: (0, 0)),
                  pl.BlockSpec((tm, 256), lambda i: (i, 0))],
        out_specs=[pl.BlockSpec((tm, D_MODEL), lambda i: (i, 0)),
                   pl.BlockSpec((tm, ZC_W), lambda i: (i, 0)),
                   pl.BlockSpec((tm, 512), lambda i: (i, 0))],
        out_shape=[jax.ShapeDtypeStruct((rows, D_MODEL), BF16),
                   jax.ShapeDtypeStruct((rows, ZC_W), F32),
                   jax.ShapeDtypeStruct((rows, 512), F32)],
        compiler_params=_cp("parallel"),
    )(x, g3, mods, wcat, wg2, bias2, cs)


_QB = WINDOW


def _attn_specs(t_len, l_ctx):
    nb = t_len // _QB
    kvb = ZC_KV // 256
    return [pl.BlockSpec(memory_space=pltpu.SMEM),
            pl.BlockSpec((_QB, 512), lambda n: (n, 0)),
            pl.BlockSpec((_QB, 256), lambda n: (jnp.maximum(n - 1, 0), kvb)),
            pl.BlockSpec((_QB, 256), lambda n: (n, kvb)),
            pl.BlockSpec((_QB, 256), lambda n: (n + 1, kvb)),
            pl.BlockSpec((l_ctx, 256), lambda n: (t_len // l_ctx, kvb))], nb


_HEAD_PAIRS = ((0, 1), (2, 3))


def _attn_keys(kp, kc, kn, kx, g):
    hd = A_HEAD_DIM
    ks = slice(g * hd, (g + 1) * hd)
    vs = slice(128 + g * hd, 128 + (g + 1) * hd)
    kb = jnp.concatenate([kp[:, ks], kc[:, ks], kn[:, ks]], axis=0).astype(BF16)
    vb = jnp.concatenate([kp[:, vs], kc[:, vs], kn[:, vs]], axis=0).astype(BF16)
    return kb, vb, kx[:, ks].astype(BF16), kx[:, vs].astype(BF16)


def _attn_probs(n, t_len, sink_ref, qv, kb, kxb, g, rs):
    hd = A_HEAD_DIM
    qg = jnp.concatenate([qv[:, (4 * g + r) * hd:(4 * g + r + 1) * hd] for r in rs], axis=0).astype(BF16)
    qi = lax.broadcasted_iota(jnp.int32, (_QB, 3 * _QB), 0)
    kj = lax.broadcasted_iota(jnp.int32, (_QB, 3 * _QB), 1)
    kpos = n * _QB - _QB + kj
    valid = (kpos >= 0) & (kpos < t_len) & (jnp.abs(kj - _QB - qi) <= WINDOW)
    valid = jnp.concatenate([valid] * len(rs), axis=0)
    scale = hd ** -0.5
    s = jnp.where(valid, _dot_nt(qg, kb) * scale, -jnp.inf)
    sc = _dot_nt(qg, kxb) * scale
    sk = jnp.concatenate([jnp.full((_QB, 1), sink_ref[4 * g + r], F32) for r in rs], axis=0)
    m = jnp.maximum(jnp.maximum(jnp.max(s, axis=-1, keepdims=True), jnp.max(sc, axis=-1, keepdims=True)), sk)
    p = jnp.exp(s - m)
    pc = jnp.exp(sc - m)
    ps = jnp.exp(sk - m)
    inv = 1.0 / (jnp.sum(p, axis=-1, keepdims=True) + jnp.sum(pc, axis=-1, keepdims=True) + ps)
    return p, pc, ps, inv, qg


def _attn_fwd(zc, sink, t_len, l_ctx, name):
    in_specs, nb = _attn_specs(t_len, l_ctx)

    def body(sink_ref, q_ref, kp_ref, kc_ref, kn_ref, kx_ref, o_ref):
        n = pl.program_id(0)
        qv = q_ref[...]
        outs = []
        for g in range(A_KV_HEADS):
            kb, vb, kxb, vxb = _attn_keys(kp_ref[...], kc_ref[...], kn_ref[...], kx_ref[...], g)
            for rs in _HEAD_PAIRS:
                p, pc, _, inv, _ = _attn_probs(n, t_len, sink_ref, qv, kb, kxb, g, rs)
                o = (_dot(p.astype(BF16), vb) + _dot(pc.astype(BF16), vxb)) * inv
                outs += [o[i * _QB:(i + 1) * _QB] for i in range(len(rs))]
        o_ref[...] = jnp.concatenate(outs, axis=1)

    return pl.pallas_call(
        body, name=name, grid=(nb,), in_specs=in_specs,
        out_specs=pl.BlockSpec((_QB, 512), lambda n: (n, 0)),
        out_shape=jax.ShapeDtypeStruct((t_len, 512), F32),
        compiler_params=_cp("parallel"),
    )(sink, zc, zc, zc, zc, zc)


def _attn_bwd(zc, sink, o, dcat, t_len, l_ctx, name):
    rows = zc.shape[0]
    in_specs, nb = _attn_specs(t_len, l_ctx)
    in_specs = in_specs + [pl.BlockSpec((_QB, 512), lambda n: (n, 0)), pl.BlockSpec((_QB, 512), lambda n: (n, 0))]
    hd = A_HEAD_DIM
    scale = hd ** -0.5

    def body(sink_ref, q_ref, kp_ref, kc_ref, kn_ref, kx_ref, o_ref, do_ref, dq_ref, dkv_ref, dsink_ref):
        n = pl.program_id(0)

        @pl.when(n == 0)
        def _():
            dkv_ref[...] = jnp.zeros_like(dkv_ref)
            dsink_ref[...] = jnp.zeros_like(dsink_ref)

        qv = q_ref[...]
        ov = o_ref[...]
        dov = do_ref[...]
        dqs, dkbs, dvbs, dkxs, dvxs, dsinks = [], [], [], [], [], []
        for g in range(A_KV_HEADS):
            kb, vb, kxb, vxb = _attn_keys(kp_ref[...], kc_ref[...], kn_ref[...], kx_ref[...], g)
            parts = []
            for rs in _HEAD_PAIRS:
                p, pc, ps, inv, qg = _attn_probs(n, t_len, sink_ref, qv, kb, kxb, g, rs)
                og = jnp.concatenate([ov[:, (4 * g + r) * hd:(4 * g + r + 1) * hd] for r in rs], axis=0)
                dog = jnp.concatenate([dov[:, (4 * g + r) * hd:(4 * g + r + 1) * hd] for r in rs], axis=0)
                delta = jnp.sum(og * dog, axis=-1, keepdims=True)
                dogb = dog.astype(BF16)
                pn = p * inv
                pcn = pc * inv
                ds = (pn * (_dot_nt(dogb, vb) - delta) * scale).astype(BF16)
                dsc = (pcn * (_dot_nt(dogb, vxb) - delta) * scale).astype(BF16)
                dsk = (ps * inv) * (0.0 - delta)
                dqg = _dot(ds, kb) + _dot(dsc, kxb)
                dqs += [dqg[i * _QB:(i + 1) * _QB] for i in range(len(rs))]
                parts.append((_dot_tn(ds, qg), _dot_tn(pn.astype(BF16), dogb),
                              _dot_tn(dsc, qg), _dot_tn(pcn.astype(BF16), dogb)))
                for i in range(len(rs)):
                    tot = jnp.sum(dsk[i * _QB:(i + 1) * _QB], axis=0, keepdims=True)
                    dsinks.append(jnp.broadcast_to(tot, (1, 128)))
            dkbs.append(parts[0][0] + parts[1][0])
            dvbs.append(parts[0][1] + parts[1][1])
            dkxs.append(parts[0][2] + parts[1][2])
            dvxs.append(parts[0][3] + parts[1][3])
        dsink_ref[...] += jnp.concatenate(dsinks, axis=0)
        dq_ref[...] = jnp.concatenate(dqs, axis=1)
        band = jnp.concatenate(dkbs + dvbs, axis=1)
        ctxc = jnp.concatenate(dkxs + dvxs, axis=1)
        r_prev = pl.multiple_of(jnp.maximum(n - 1, 0) * _QB, _QB)
        r_cur = pl.multiple_of(n * _QB, _QB)
        r_next = pl.multiple_of((n + 1) * _QB, _QB)
        dkv_ref[pl.ds(r_prev, _QB), :] += band[0:_QB]
        dkv_ref[pl.ds(r_cur, _QB), :] += band[_QB:2 * _QB]
        dkv_ref[pl.ds(r_next, _QB), :] += band[2 * _QB:3 * _QB]
        dkv_ref[t_len:t_len + l_ctx, :] += ctxc

    return pl.pallas_call(
        body, name=name, grid=(nb,), in_specs=in_specs,
        out_specs=[pl.BlockSpec((_QB, 512), lambda n: (n, 0)),
                   pl.BlockSpec((rows, 256), lambda n: (0, 0)),
                   pl.BlockSpec((8, 128), lambda n: (0, 0))],
        out_shape=[jax.ShapeDtypeStruct((t_len, 512), F32),
                   jax.ShapeDtypeStruct((rows, 256), F32),
                   jax.ShapeDtypeStruct((8, 128), F32)],
        compiler_params=_cp("arbitrary"),
    )(sink, zc, zc, zc, zc, zc, o, dcat)


_GC = B_CHUNK


def _split_bf16(a):
    hi = a.astype(BF16)
    return hi, (a - hi.astype(F32)).astype(BF16)


def _gla_chunk_terms(qk, la, reverse):
    q = qk[:, 0:256]
    k = qk[:, 256:512]
    off = 256 if reverse else 0
    lad = la[:, off:off + 256]
    ii = lax.broadcasted_iota(jnp.int32, (_GC, _GC), 0)
    jj = lax.broadcasted_iota(jnp.int32, (_GC, _GC), 1)
    mask = (jj >= ii) if reverse else (jj <= ii)
    tri = jnp.where(mask, 1.0, 0.0).astype(BF16)
    la_hi, la_lo = _split_bf16(lad)
    g = _dot(tri, la_hi) + _dot(tri, la_lo)
    gl = jnp.sum(lad, axis=0, keepdims=True)
    eg = jnp.exp(g)
    eng = jnp.exp(-g)
    eend = jnp.exp(gl - g)
    sc = B_DK ** -0.5
    qt = q * (sc * eg)
    kt = k * eng
    ke = k * eend
    return mask, tri, gl, eg, eng, eend, qt, kt, ke


def _same_head(rows, cols, row_shift, col_shift):
    r = jnp.right_shift(lax.broadcasted_iota(jnp.int32, (rows, cols), 0), row_shift)
    c = jnp.right_shift(lax.broadcasted_iota(jnp.int32, (rows, cols), 1), col_shift)
    return r == c


def _block_diag_rows(x, col_shift):
    tiled = jnp.concatenate([x] * B_HEADS, axis=0)
    return jnp.where(_same_head(tiled.shape[0], tiled.shape[1], 6, col_shift), tiled, jnp.zeros_like(tiled))


def _fold_heads(x):
    c = x.shape[0] // B_HEADS
    return (x[0:c] + x[c:2 * c]) + (x[2 * c:3 * c] + x[3 * c:4 * c])


def _chunk_mask4(reverse):
    ii = lax.broadcasted_iota(jnp.int32, (_GC, B_HEADS * _GC), 0)
    jj = jnp.bitwise_and(lax.broadcasted_iota(jnp.int32, (_GC, B_HEADS * _GC), 1), _GC - 1)
    return (jj >= ii) if reverse else (jj <= ii)


_ST_SHAPE = (B_HEADS * B_DV, B_HEADS * B_DK)


def _state_blocks(t):
    return [t[hh * B_DV:(hh + 1) * B_DV, hh * B_DK:(hh + 1) * B_DK] for hh in range(B_HEADS)]


def _state_from_blocks(blocks):
    full = jnp.concatenate([jnp.concatenate([b] * B_HEADS, axis=1) for b in blocks], axis=0)
    return jnp.where(_same_head(_ST_SHAPE[0], _ST_SHAPE[1], 7, 6), full, 0.0)


def _gla_fwd(zc, la, dep, t_len, l_ctx, name):
    rows = zc.shape[0]
    n_x = t_len // _GC
    n_c = n_x + l_ctx // _GC
    qkb, vb = ZC_QK // 512, ZC_V // 512

    def ch_f(c):
        return lax.rem(c + n_x, n_c)

    def ch_r(c):
        return n_c - 1 - c

    def body(qkf_ref, vf_ref, laf_ref, qkr_ref, vr_ref, lar_ref, dep_ref, of_ref, or_ref, spf_ref, spr_ref, stf, strv):
        del dep_ref
        c = pl.program_id(0)

        @pl.when(c == 0)
        def _():
            stf[...] = jnp.zeros_like(stf)
            strv[...] = jnp.zeros_like(strv)

        results = []
        for qk_ref, v_ref, la_ref, st, reverse in ((qkf_ref, vf_ref, laf_ref, stf, False),
                                                   (qkr_ref, vr_ref, lar_ref, strv, True)):
            mask, _, gl, _, _, _, qt, kt, ke = _gla_chunk_terms(qk_ref[...], la_ref[...], reverse)
            vbf = v_ref[...].astype(BF16)
            qtb, keb = qt.astype(BF16), ke.astype(BF16)
            kbd = _block_diag_rows(kt.astype(BF16), 6)
            vbd = _block_diag_rows(vbf, 7)
            mask4 = _chunk_mask4(reverse)
            t_prev = st[...]
            att = jnp.where(mask4, _dot_nt(qtb, kbd), 0.0).astype(BF16)
            o_all = _dot(att, vbd) + _dot_nt(qtb, t_prev.astype(BF16))
            t_new = t_prev * jnp.exp(gl) + jnp.where(_same_head(_ST_SHAPE[0], _ST_SHAPE[1], 7, 6),
                                                     _dot_tn(vbf, keb), 0.0)
            results.append((o_all, t_prev, t_new))
        for (o_all, t_prev, t_new), o_ref, sp_ref, st in zip(results, (of_ref, or_ref), (spf_ref, spr_ref), (stf, strv)):
            o_ref[...] = o_all
            for hh, blk in enumerate(_state_blocks(t_prev)):
                sp_ref[0, hh] = blk
            st[...] = t_new

    st_shape = (B_HEADS, B_DV, B_DK)
    return pl.pallas_call(
        body, name=name, grid=(n_c,),
        in_specs=[pl.BlockSpec((_GC, 512), lambda c: (ch_f(c), qkb)),
                  pl.BlockSpec((_GC, 512), lambda c: (ch_f(c), vb)),
                  pl.BlockSpec((_GC, 512), lambda c: (ch_f(c), 0)),
                  pl.BlockSpec((_GC, 512), lambda c: (ch_r(c), qkb)),
                  pl.BlockSpec((_GC, 512), lambda c: (ch_r(c), vb)),
                  pl.BlockSpec((_GC, 512), lambda c: (ch_r(c), 0)),
                  pl.BlockSpec((8, 128), lambda c: (0, 0))],
        out_specs=[pl.BlockSpec((_GC, 512), lambda c: (ch_f(c), 0)),
                   pl.BlockSpec((_GC, 512), lambda c: (ch_r(c), 0)),
                   pl.BlockSpec((1,) + st_shape, lambda c: (c, 0, 0, 0)),
                   pl.BlockSpec((1,) + st_shape, lambda c: (c, 0, 0, 0))],
        out_shape=[jax.ShapeDtypeStruct((rows, 512), F32), jax.ShapeDtypeStruct((rows, 512), F32),
                   jax.ShapeDtypeStruct((n_c,) + st_shape, F32), jax.ShapeDtypeStruct((n_c,) + st_shape, F32)],
        scratch_shapes=[pltpu.VMEM(_ST_SHAPE, F32), pltpu.VMEM(_ST_SHAPE, F32)],
        compiler_params=_cp("arbitrary"),
    )(zc, zc, la, zc, zc, la, dep)


def _gla_bwd(zc, la, spf, spr, dosum, t_len, l_ctx, name):
    rows = zc.shape[0]
    n_x = t_len // _GC
    n_c = n_x + l_ctx // _GC
    n_all = rows // _GC
    qkb, vb = ZC_QK // 512, ZC_V // 512

    def scan_of(c):
        return jnp.maximum(n_c - 1 - c, 0)

    def ch_f(c):
        return jnp.where(c < n_c, lax.rem(scan_of(c) + n_x, n_c), c)

    def ch_r(c):
        return c

    def do_of(ch):
        return jnp.minimum(ch, n_x - 1)

    def body(qkf_ref, vf_ref, laf_ref, spf_ref, dof_ref, qkr_ref, vr_ref, lar_ref, spr_ref, dor_ref,
             dqkf_ref, dvf_ref, dlaf_ref, dqkr_ref, dvr_ref, dlar_ref, dsf, dsr):
        c = pl.program_id(0)

        @pl.when(c == 0)
        def _():
            dsf[...] = jnp.zeros_like(dsf)
            dsr[...] = jnp.zeros_like(dsr)

        @pl.when(c >= n_c)
        def _():
            for r in (dqkf_ref, dvf_ref, dlaf_ref, dqkr_ref, dvr_ref, dlar_ref):
                r[...] = jnp.zeros_like(r)

        @pl.when(c < n_c)
        def _():
            sc = B_DK ** -0.5
            results = []
            for qk_ref, v_ref, la_ref, sp_ref, do_ref, dst, reverse, ch in (
                    (qkf_ref, vf_ref, laf_ref, spf_ref, dof_ref, dsf, False, ch_f(c)),
                    (qkr_ref, vr_ref, lar_ref, spr_ref, dor_ref, dsr, True, ch_r(c))):
                mask, tri, gl, eg, eng, eend, qt, kt, ke = _gla_chunk_terms(qk_ref[...], la_ref[...], reverse)
                vbf = v_ref[...].astype(BF16)
                dob = jnp.where(ch < n_x, do_ref[...], 0.0).astype(BF16)
                qtb, keb = qt.astype(BF16), ke.astype(BF16)
                kbd = _block_diag_rows(kt.astype(BF16), 6)
                vbd = _block_diag_rows(vbf, 7)
                mask4 = _chunk_mask4(reverse)
                egl = jnp.exp(gl)
                t_prev = _state_from_blocks([sp_ref[0, hh] for hh in range(B_HEADS)])
                dt_new = dst[...]
                tpb, dtb = t_prev.astype(BF16), dt_new.astype(BF16)
                att = jnp.where(mask4, _dot_nt(qtb, kbd), 0.0).astype(BF16)
                datt = jnp.where(mask4, _dot_nt(dob, vbd), 0.0).astype(BF16)
                dqt = _dot(datt, kbd) + _dot(dob, tpb)
                dkt = _fold_heads(jnp.where(_same_head(256, 256, 6, 6), _dot_tn(datt, qtb), 0.0))
                dv = _fold_heads(jnp.where(_same_head(256, 512, 6, 7), _dot_tn(att, dob), 0.0)) + _dot_nt(keb, dtb)
                dke = _dot(vbf, dtb)
                dt_prev = dt_new * egl + jnp.where(_same_head(_ST_SHAPE[0], _ST_SHAPE[1], 7, 6),
                                                   _dot_tn(dob, qtb), 0.0)
                dgl = (jnp.sum(dke * ke, axis=0, keepdims=True)
                       + jnp.sum(dt_new * t_prev, axis=0, keepdims=True) * egl)
                dg_hi, dg_lo = _split_bf16(dqt * qt - dkt * kt - dke * ke)
                dla = _dot_tn(tri, dg_hi) + _dot_tn(tri, dg_lo) + dgl
                dqk = jnp.concatenate([dqt * (sc * eg), dkt * eng + dke * eend], axis=1)
                results.append((dqk, dv, dla, dt_prev))
            for (dqk, dv, dla, dt_prev), dqk_ref, dv_ref, dla_ref, dst in zip(
                    results, (dqkf_ref, dqkr_ref), (dvf_ref, dvr_ref), (dlaf_ref, dlar_ref), (dsf, dsr)):
                dqk_ref[...] = dqk
                dv_ref[...] = dv
                dla_ref[...] = dla
                dst[...] = dt_prev

    st_shape = (B_HEADS, B_DV, B_DK)

    def side(chf):
        return [pl.BlockSpec((_GC, 512), lambda c: (chf(c), qkb)),
                pl.BlockSpec((_GC, 512), lambda c: (chf(c), vb)),
                pl.BlockSpec((_GC, 512), lambda c: (chf(c), 0)),
                pl.BlockSpec((1,) + st_shape, lambda c: (scan_of(c), 0, 0, 0)),
                pl.BlockSpec((_GC, 512), lambda c: (do_of(chf(c)), 0))]

    def out_side(chf):
        return [pl.BlockSpec((_GC, 512), lambda c: (chf(c), 0)),
                pl.BlockSpec((_GC, 512), lambda c: (chf(c), 0)),
                pl.BlockSpec((_GC, 256), lambda c: (chf(c), 0))]

    shp = [jax.ShapeDtypeStruct((rows, 512), F32), jax.ShapeDtypeStruct((rows, 512), F32),
           jax.ShapeDtypeStruct((rows, 256), F32)]
    return pl.pallas_call(
        body, name=name, grid=(n_all,),
        in_specs=side(ch_f) + side(ch_r),
        out_specs=out_side(ch_f) + out_side(ch_r),
        out_shape=shp + shp,
        scratch_shapes=[pltpu.VMEM(_ST_SHAPE, F32), pltpu.VMEM(_ST_SHAPE, F32)],
        compiler_params=_cp("arbitrary"),
    )(zc, zc, la, spf, dosum, zc, zc, la, spr, dosum)


def _gla_out_fwd(o_a, o_f, o_r, zc, gla_g, t_len, name):
    tm = ROW_TILE
    rb = ZC_R // 512

    def body(oa_ref, of_ref, or_ref, r_ref, g_ref, cat_ref):
        osum = of_ref[...] + or_ref[...]
        g = g_ref[...]
        pieces = []
        for hh in range(B_HEADS):
            oh = osum[:, hh * B_DV:(hh + 1) * B_DV]
            rs = lax.rsqrt(jnp.mean(oh * oh, axis=-1, keepdims=True) + RMS_EPS)
            pieces.append((oh * rs) * g)
        r = r_ref[...]
        cat_ref[:, 0:512] = oa_ref[...].astype(BF16)
        cat_ref[:, 512:1024] = (jnp.concatenate(pieces, axis=1) * (r * _sigmoid(r))).astype(BF16)

    return pl.pallas_call(
        body, name=name, grid=(t_len // tm,),
        in_specs=[pl.BlockSpec((tm, 512), lambda i: (i, 0)),
                  pl.BlockSpec((tm, 512), lambda i: (i, 0)),
                  pl.BlockSpec((tm, 512), lambda i: (i, 0)),
                  pl.BlockSpec((tm, 512), lambda i: (i, rb)),
                  pl.BlockSpec((1, B_DV), lambda i: (0, 0))],
        out_specs=pl.BlockSpec((tm, D_MODEL), lambda i: (i, 0)),
        out_shape=jax.ShapeDtypeStruct((t_len, D_MODEL), BF16),
        compiler_params=_cp("parallel"),
    )(o_a, o_f, o_r, zc, gla_g)


def _gla_out_bwd(dcat, o_f, o_r, zc, gla_g, t_len, name):
    tm = ROW_TILE
    rb = ZC_R // 512

    def body(d_ref, of_ref, or_ref, r_ref, g_ref, dos_ref, dr_ref, dg_ref):
        i = pl.program_id(0)
        osum = of_ref[...] + or_ref[...]
        g = g_ref[...]
        r = r_ref[...]
        dgo = d_ref[...]
        sg = _sigmoid(r)
        dnrmg = dgo * (r * sg)
        nrms, dos = [], []
        dg_acc = jnp.zeros((1, B_DV), F32)
        for hh in range(B_HEADS):
            oh = osum[:, hh * B_DV:(hh + 1) * B_DV]
            rs = lax.rsqrt(jnp.mean(oh * oh, axis=-1, keepdims=True) + RMS_EPS)
            nrm = oh * rs
            dn = dnrmg[:, hh * B_DV:(hh + 1) * B_DV]
            dg_acc = dg_acc + jnp.sum(dn * nrm, axis=0, keepdims=True)
            dnn = dn * g
            dos.append(rs * (dnn - nrm * jnp.mean(dnn * nrm, axis=-1, keepdims=True)))
            nrms.append(nrm * g)
        dos_ref[...] = jnp.concatenate(dos, axis=1)
        dr_ref[...] = dgo * jnp.concatenate(nrms, axis=1) * (sg * (1.0 + r * (1.0 - sg)))

        @pl.when(i == 0)
        def _():
            dg_ref[...] = jnp.zeros_like(dg_ref)

        dg_ref[...] += dg_acc

    return pl.pallas_call(
        body, name=name, grid=(t_len // tm,),
        in_specs=[pl.BlockSpec((tm, 512), lambda i: (i, 1)),
                  pl.BlockSpec((tm, 512), lambda i: (i, 0)),
                  pl.BlockSpec((tm, 512), lambda i: (i, 0)),
                  pl.BlockSpec((tm, 512), lambda i: (i, rb)),
                  pl.BlockSpec((1, B_DV), lambda i: (0, 0))],
        out_specs=[pl.BlockSpec((tm, 512), lambda i: (i, 0)),
                   pl.BlockSpec((tm, 512), lambda i: (i, 0)),
                   pl.BlockSpec((1, B_DV), lambda i: (0, 0))],
        out_shape=[jax.ShapeDtypeStruct((t_len, 512), F32), jax.ShapeDtypeStruct((t_len, 512), F32),
                   jax.ShapeDtypeStruct((1, B_DV), F32)],
        compiler_params=_cp("arbitrary"),
    )(dcat, o_f, o_r, zc, gla_g)


def _mix_prep(dq, dkv, dqk_f, dqk_r, dv_f, dv_r, d_r, dla_f, dla_r, zc, wg2, bias2, cs, t_len, name):
    rows = zc.shape[0]
    tm = ROW_TILE
    n_x = t_len // tm
    gb = ZC_G // 128

    def xrow(i):
        return jnp.minimum(i, n_x - 1)

    def body(dq_ref, dkv_ref, dqkf_ref, dqkr_ref, dvf_ref, dvr_ref, dr_ref, dlaf_ref, dlar_ref, zg_ref, wg_ref,
             b_ref, cs_ref, dz_ref, dwg_ref, db_ref):
        i = pl.program_id(0)
        is_x = i < n_x
        cos = cs_ref[:, 0:128]
        sin = cs_ref[:, 128:256]
        cosq = jnp.concatenate([cos] * 4, axis=1)
        sinq = jnp.concatenate([sin] * 4, axis=1)
        dqv = jnp.where(is_x, dq_ref[...], 0.0)
        dz_ref[:, ZC_Q:ZC_QK] = (dqv * cosq + _swap16(dqv * sinq)).astype(BF16)
        dz_ref[:, ZC_QK:ZC_V] = (dqkf_ref[...] + dqkr_ref[...]).astype(BF16)
        dz_ref[:, ZC_V:ZC_R] = (dvf_ref[...] + dvr_ref[...]).astype(BF16)
        dz_ref[:, ZC_R:ZC_KV] = jnp.where(is_x, dr_ref[...], 0.0).astype(BF16)
        dk = dkv_ref[:, 0:128]
        dz_ref[:, ZC_KV:ZC_KV + 128] = (dk * cos + _swap16(dk * sin)).astype(BF16)
        dz_ref[:, ZC_KV + 128:ZC_G] = dkv_ref[:, 128:256].astype(BF16)
        zgb = zg_ref[...].astype(BF16)
        wg = wg_ref[...]
        pre = _dot(zgb, wg) + b_ref[...]
        dla = jnp.concatenate([dlaf_ref[...], dlar_ref[...]], axis=1)
        dpre = dla * (_sigmoid(-pre) / B_GATE_NORM)
        dpb = dpre.astype(BF16)
        dz_ref[:, ZC_G:ZC_W] = _dot_nt(dpb, wg).astype(BF16)

        @pl.when(i == 0)
        def _():
            dwg_ref[...] = jnp.zeros_like(dwg_ref)
            db_ref[...] = jnp.zeros_like(db_ref)

        dwg_ref[...] += _dot_tn(zgb, dpb)
        db_ref[...] += jnp.sum(dpre, axis=0, keepdims=True)

    return pl.pallas_call(
        body, name=name, grid=(rows // tm,),
        in_specs=[pl.BlockSpec((tm, 512), lambda i: (xrow(i), 0)),
                  pl.BlockSpec((tm, 256), lambda i: (i, 0)),
                  pl.BlockSpec((tm, 512), lambda i: (i, 0)),
                  pl.BlockSpec((tm, 512), lambda i: (i, 0)),
                  pl.BlockSpec((tm, 512), lambda i: (i, 0)),
                  pl.BlockSpec((tm, 512), lambda i: (i, 0)),
                  pl.BlockSpec((tm, 512), lambda i: (xrow(i), 0)),
                  pl.BlockSpec((tm, 256), lambda i: (i, 0)),
                  pl.BlockSpec((tm, 256), lambda i: (i, 0)),
                  pl.BlockSpec((tm, 128), lambda i: (i, gb)),
                  pl.BlockSpec((128, 512), lambda i: (0, 0)),
                  pl.BlockSpec((1, 512), lambda i: (0, 0)),
                  pl.BlockSpec((tm, 256), lambda i: (i, 0))],
        out_specs=[pl.BlockSpec((tm, ZC_W), lambda i: (i, 0)),
                   pl.BlockSpec((128, 512), lambda i: (0, 0)),
                   pl.BlockSpec((1, 512), lambda i: (0, 0))],
        out_shape=[jax.ShapeDtypeStruct((rows, ZC_W), BF16),
                   jax.ShapeDtypeStruct((128, 512), F32),
                   jax.ShapeDtypeStruct((1, 512), F32)],
        compiler_params=_cp("arbitrary"),
    )(dq, dkv, dqk_f, dqk_r, dv_f, dv_r, d_r, dla_f, dla_r, zc, wg2, bias2, cs)


def _gate_weights(w_a2_f, b_a_f, w_a2_b, b_a_b):
    wg2 = jnp.zeros((128, 512), F32)
    wg2 = wg2.at[0:B_GATE_RANK, 0:256].set(w_a2_f).at[B_GATE_RANK:2 * B_GATE_RANK, 256:512].set(w_a2_b)
    bias2 = jnp.concatenate([b_a_f, b_a_b]).reshape(1, 512)
    return wg2.astype(BF16), bias2


_WIN_PERM = ((0, 512), (768, 1280), (1280, 1792), (1792, 2304), (512, 768), (2304, 2336))


def _w_in_to_cat(w_in_full):
    parts = [w_in_full[:, a:b] for a, b in _WIN_PERM]
    parts.append(jnp.zeros((w_in_full.shape[0], ZC_W - PROJ_DIM), w_in_full.dtype))
    return jnp.concatenate(parts, axis=1)


def _cat_to_w_in(d_wcat):
    return jnp.concatenate([d_wcat[:, ZC_Q:ZC_QK], d_wcat[:, ZC_KV:ZC_G], d_wcat[:, ZC_QK:ZC_KV],
                            d_wcat[:, ZC_G:ZC_G + 2 * B_GATE_RANK]], axis=1)


def _mixer_ab_forward(x1, g3, mods, wcat, wg2, bias2, sink, gla_g, w_out, cs, t_len, l_ctx, n_x, pace):
    h, zc, la = _proj_fwd(x1, g3, mods, n_x, wcat, wg2, bias2, cs, "mix0_proj")
    dep = pace("proj", zc)
    o_a = _attn_fwd(zc, sink + dep[0, 0], t_len, l_ctx, "mix0_attn")
    dep = pace("attn", o_a)
    o_f, o_r, spf, spr = _gla_fwd(zc, la, dep, t_len, l_ctx, "mix0_gla")
    dep = pace("gla", o_f)
    cat = _gla_out_fwd(o_a, o_f, o_r, zc, gla_g + dep[0:1, 0:1], t_len, "mix0_glaout")
    x2, y = _matmul_resid(cat, w_out, x1, mods, 5, 1.0, n_x, t_len, "mix0_out")
    return x2, (x1, h, zc, la, o_a, o_f, o_r, spf, spr, cat, y)


def _mixer_ab_backward(dx2, saved, g3, mods, wcat, wg2, bias2, sink, gla_g, w_out, cs, t_len, l_ctx, n_x):
    x1, h, zc, la, o_a, o_f, o_r, spf, spr, cat, y = saved
    rows = x1.shape[0]
    tm = ROW_TILE
    dy, dcat, dgate = _gate_dy(dx2, y, mods, 5, 1.0, n_x, t_len, w_out, "mix0_dy")
    tk = _token_tile(t_len)
    d_wout = _matmul_tn(
        cat, dy, pl.BlockSpec((tk, D_MODEL), lambda n, k: (k, 0)), pl.BlockSpec((tk, D_MODEL), lambda n, k: (k, 0)),
        (D_MODEL, D_MODEL), pl.BlockSpec((D_MODEL, D_MODEL), lambda n, k: (0, 0)), (1, t_len // tk), "mix0_dwout")
    dos, d_r, d_glag = _gla_out_bwd(dcat, o_f, o_r, zc, gla_g, t_len, "mix0_dglaout")
    dqk_f, dv_f, dla_f, dqk_r, dv_r, dla_r = _gla_bwd(zc, la, spf, spr, dos, t_len, l_ctx, "mix0_dgla")
    dq, dkv, dsink = _attn_bwd(zc, sink, o_a, dcat, t_len, l_ctx, "mix0_dattn")
    dzc, dwg2, dbias2 = _mix_prep(dq, dkv, dqk_f, dqk_r, dv_f, dv_r, d_r, dla_f, dla_r, zc, wg2, bias2, cs, t_len,
                                  "mix0_prep")
    tk = _token_tile(rows)
    d_wcat = _matmul_tn(
        h, dzc, pl.BlockSpec((tk, D_MODEL), lambda n, k: (k, 0)), pl.BlockSpec((tk, ZC_W), lambda n, k: (k, 0)),
        (D_MODEL, ZC_W), pl.BlockSpec((D_MODEL, ZC_W), lambda n, k: (0, 0)), (1, rows // tk), "mix0_dwin")
    pairs = [(dzc, pl.BlockSpec((tm, ZC_W), lambda i: (i, 0)), wcat, pl.BlockSpec((D_MODEL, ZC_W), lambda i: (0, 0)))]
    dx1, stats = _bwd_dx(pairs, x1, dx2, t_len // tm, g3, mods, 1, n_x, "mix0_dx")
    return dx1, stats, dgate, d_wcat, dwg2, dbias2, dsink, d_glag, d_wout


_PT = 256
_PH = 16


def _pool_window(n, t_len, w, transpose):
    shape = (_PT, _PT + 2 * _PH)
    a = n * _PT + lax.broadcasted_iota(jnp.int32, shape, 0)
    b = n * _PT - _PH + lax.broadcasted_iota(jnp.int32, shape, 1)
    t, s = (b, a) if transpose else (a, b)
    lo = jnp.maximum(t - w // 2, 0)
    hi = jnp.minimum(t + (w - w // 2), t_len)
    inside = (s >= lo) & (s < hi) & (t >= 0) & (t < t_len)
    return jnp.where(inside, 1.0, 0.0).astype(BF16)


def _pool_inv_count(first, count, t_len, w):
    t = first + lax.broadcasted_iota(jnp.int32, (count, 1), 0)
    lo = jnp.maximum(t - w // 2, 0)
    hi = jnp.minimum(t + (w - w // 2), t_len)
    return jnp.where((t >= 0) & (t < t_len), 1.0 / jnp.maximum(hi - lo, 1).astype(F32), 0.0)


def _window_sum(win, vals):
    hi, lo = _split_bf16(vals)
    return _dot(win, hi) + _dot(win, lo)


def _pool_halo(p_ref, c_ref, n_ref):
    return jnp.concatenate([p_ref[_PT - _PH:_PT, :], c_ref[...], n_ref[0:_PH, :]], axis=0)


def _pool_specs(t_len):
    nb = t_len // _PT
    return [pl.BlockSpec((_PT, D_MODEL), lambda n: (jnp.maximum(n - 1, 0), 0)),
            pl.BlockSpec((_PT, D_MODEL), lambda n: (n, 0)),
            pl.BlockSpec((_PT, D_MODEL), lambda n: (jnp.minimum(n + 1, nb - 1), 0))], nb


def _pool_fwd(h, wp, pscale, x1, mods, t_len, name):
    halo_specs, nb = _pool_specs(t_len)

    def body(hp_ref, hc_ref, hn_ref, w_ref, ps_ref, x_ref, m_ref, x2_ref, pooled_ref, ypre_ref):
        n = pl.program_id(0)
        hcat = _pool_halo(hp_ref, hc_ref, hn_ref)
        ys = []
        for gi, w in enumerate(POOL_WINDOWS):
            cols = slice(gi * POOL_GROUP, (gi + 1) * POOL_GROUP)
            hg = hcat[:, cols]
            mean = _window_sum(_pool_window(n, t_len, w, False), hg) * _pool_inv_count(n * _PT, _PT, t_len, w)
            pooled = (mean - hg[_PH:_PH + _PT]).astype(BF16)
            pooled_ref[:, cols] = pooled
            ys.append(_dot(pooled, w_ref[gi]))
        ypre = jnp.concatenate(ys, axis=1)
        ypre_ref[...] = ypre
        x2_ref[...] = x_ref[...] + m_ref[0, 5:6, :] * (ypre * ps_ref[...])

    return pl.pallas_call(
        body, name=name, grid=(nb,),
        in_specs=halo_specs + [pl.BlockSpec((4, POOL_GROUP, POOL_GROUP), lambda n: (0, 0, 0)),
                               pl.BlockSpec((1, D_MODEL), lambda n: (0, 0)),
                               pl.BlockSpec((_PT, D_MODEL), lambda n: (n, 0)),
                               pl.BlockSpec((1, N_MOD, D_MODEL), lambda n: (0, 0, 0))],
        out_specs=[pl.BlockSpec((_PT, D_MODEL), lambda n: (n, 0))] * 3,
        out_shape=[jax.ShapeDtypeStruct((t_len, D_MODEL), F32), jax.ShapeDtypeStruct((t_len, D_MODEL), BF16),
                   jax.ShapeDtypeStruct((t_len, D_MODEL), F32)],
        compiler_params=_cp("parallel"),
    )(h, h, h, wp, pscale, x1, mods)


def _pool_bwd_a(dx2, ypre, wp, pscale, mods, t_len, name):
    nb = t_len // _PT

    def body(d_ref, y_ref, w_ref, ps_ref, m_ref, dyp_ref, dpl_ref, dgate_ref, dps_ref):
        n = pl.program_id(0)
        dv = d_ref[...]
        ypre = y_ref[...]
        ps = ps_ref[...]
        dy = dv * m_ref[0, 5:6, :]
        dyp = (dy * ps).astype(BF16)
        dyp_ref[...] = dyp
        for gi in range(len(POOL_WINDOWS)):
            cols = slice(gi * POOL_GROUP, (gi + 1) * POOL_GROUP)
            dpl_ref[:, cols] = _dot_nt(dyp[:, cols], w_ref[gi])

        @pl.when(n == 0)
        def _():
            dgate_ref[...] = jnp.zeros_like(dgate_ref)
            dps_ref[...] = jnp.zeros_like(dps_ref)

        dgate_ref[...] += jnp.sum(dv * (ypre * ps), axis=0, keepdims=True)
        dps_ref[...] += jnp.sum(dy * ypre, axis=0, keepdims=True)

    return pl.pallas_call(
        body, name=name, grid=(nb,),
        in_specs=[pl.BlockSpec((_PT, D_MODEL), lambda n: (n, 0)),
                  pl.BlockSpec((_PT, D_MODEL), lambda n: (n, 0)),
                  pl.BlockSpec((4, POOL_GROUP, POOL_GROUP), lambda n: (0, 0, 0)),
                  pl.BlockSpec((1, D_MODEL), lambda n: (0, 0)),
                  pl.BlockSpec((1, N_MOD, D_MODEL), lambda n: (0, 0, 0))],
        out_specs=[pl.BlockSpec((_PT, D_MODEL), lambda n: (n, 0)),
                   pl.BlockSpec((_PT, D_MODEL), lambda n: (n, 0)),
                   pl.BlockSpec((1, D_MODEL), lambda n: (0, 0)),
                   pl.BlockSpec((1, D_MODEL), lambda n: (0, 0))],
        out_shape=[jax.ShapeDtypeStruct((t_len, D_MODEL), BF16), jax.ShapeDtypeStruct((t_len, D_MODEL), F32),
                   jax.ShapeDtypeStruct((1, D_MODEL), F32), jax.ShapeDtypeStruct((1, D_MODEL), F32)],
        compiler_params=_cp("arbitrary"),
    )(dx2, ypre, wp, pscale, mods)


def _pool_bwd_dx(dpl, x1, dx2, g3, mods, t_len, name):
    halo_specs, nb = _pool_specs(t_len)

    def body(dp_ref, dc_ref, dn_ref, x_ref, d_ref, g_ref, m_ref, dx_ref, acc_ref):
        n = pl.program_id(0)
        dcat = _pool_halo(dp_ref, dc_ref, dn_ref)
        dhs = []
        for gi, w in enumerate(POOL_WINDOWS):
            cols = slice(gi * POOL_GROUP, (gi + 1) * POOL_GROUP)
            dg = dcat[:, cols]
            scaled = dg * _pool_inv_count(n * _PT - _PH, _PT + 2 * _PH, t_len, w)
            dhs.append(_window_sum(_pool_window(n, t_len, w, True), scaled) - dg[_PH:_PH + _PT])
        dh = jnp.concatenate(dhs, axis=1)
        g = g_ref[1:2, :]
        scale = m_ref[0, 4:5, :]
        dx = _rms_mod_bwd_tail(dh, x_ref[...], g, scale, 0, acc_ref, n == 0)
        dx_ref[...] = d_ref[...] + dx

    return pl.pallas_call(
        body, name=name, grid=(nb,),
        in_specs=halo_specs + [pl.BlockSpec((_PT, D_MODEL), lambda n: (n, 0)),
                               pl.BlockSpec((_PT, D_MODEL), lambda n: (n, 0)),
                               pl.BlockSpec((3, D_MODEL), lambda n: (0, 0)),
                               pl.BlockSpec((1, N_MOD, D_MODEL), lambda n: (0, 0, 0))],
        out_specs=[pl.BlockSpec((_PT, D_MODEL), lambda n: (n, 0)),
                   pl.BlockSpec((2, 3, D_MODEL), lambda n: (0, 0, 0))],
        out_shape=[jax.ShapeDtypeStruct((t_len, D_MODEL), F32), jax.ShapeDtypeStruct((2, 3, D_MODEL), F32)],
        compiler_params=_cp("arbitrary"),
    )(dpl, dpl, dpl, x1, dx2, g3, mods)


def _mixer_pool_forward(x1, g3, mods, wp, pscale, t_len):
    h = _rms_mod_fwd(x1, g3, mods, 1, t_len // ROW_TILE, F32, "mix1_mod")
    x2, pooled, ypre = _pool_fwd(h, wp, pscale, x1, mods, t_len, "mix1_pool")
    return x2, (x1, pooled, ypre)


def _mixer_pool_backward(dx2, saved, g3, mods, wp, pscale, t_len):
    x1, pooled, ypre = saved
    tm = ROW_TILE
    dyp, dpl, dgate, dps = _pool_bwd_a(dx2, ypre, wp, pscale, mods, t_len, "mix1_da")
    d_wp = _matmul_tn(
        pooled, dyp, pl.BlockSpec((tm, POOL_GROUP), lambda g, k: (k, g)),
        pl.BlockSpec((tm, POOL_GROUP), lambda g, k: (k, g)),
        (4, POOL_GROUP, POOL_GROUP), pl.BlockSpec((1, POOL_GROUP, POOL_GROUP), lambda g, k: (g, 0, 0)),
        (4, t_len // tm), "mix1_dwp")
    dx1, stats = _pool_bwd_dx(dpl, x1, dx2, g3, mods, t_len, "mix1_dx")
    return dx1, stats, dgate, dps, d_wp


def _final_loss(x3, final_g, target, name):
    t_len = x3.shape[0]
    tm = ROW_TILE

    def body(x_ref, g_ref, t_ref, dx_ref, loss_ref, dg_ref):
        i = pl.program_id(0)
        xv = x_ref[...]
        g = g_ref[...]
        r = lax.rsqrt(jnp.mean(xv * xv, axis=-1, keepdims=True) + RMS_EPS)
        xhat = xv * r
        err = xhat * g - t_ref[...]
        part = 0.5 * jnp.sum(jnp.mean(err * err, axis=-1, keepdims=True), axis=0, keepdims=True)
        dy = err * (1.0 / D_MODEL)

        @pl.when(i == 0)
        def _():
            loss_ref[...] = jnp.zeros_like(loss_ref)
            dg_ref[...] = jnp.zeros_like(dg_ref)

        loss_ref[...] += jnp.broadcast_to(part, (1, 128))
        dg_ref[...] += jnp.sum(dy * xhat, axis=0, keepdims=True)
        dxh = dy * g
        dx_ref[...] = r * (dxh - xhat * jnp.mean(dxh * xhat, axis=-1, keepdims=True))

    return pl.pallas_call(
        body, name=name, grid=(t_len // tm,),
        in_specs=[pl.BlockSpec((tm, D_MODEL), lambda i: (i, 0)),
                  pl.BlockSpec((1, D_MODEL), lambda i: (0, 0)),
                  pl.BlockSpec((tm, D_MODEL), lambda i: (i, 0))],
        out_specs=[pl.BlockSpec((tm, D_MODEL), lambda i: (i, 0)),
                   pl.BlockSpec((1, 128), lambda i: (0, 0)),
                   pl.BlockSpec((1, D_MODEL), lambda i: (0, 0))],
        out_shape=[jax.ShapeDtypeStruct((t_len, D_MODEL), F32), jax.ShapeDtypeStruct((1, 128), F32),
                   jax.ShapeDtypeStruct((1, D_MODEL), F32)],
        compiler_params=_cp("arbitrary"),
    )(x3, final_g, target)


_CROWS = 16


def _adaln_fwd(c16, w_mod, bias_k, name):
    n_l, _, cols = w_mod.shape

    def body(c_ref, w_ref, b_ref, o_ref):
        cv = c_ref[...]
        sc = (cv * _sigmoid(cv)).astype(BF16)
        o_ref[0] = _dot(sc, w_ref[0].astype(BF16)) + b_ref[0]

    return pl.pallas_call(
        body, name=name, grid=(n_l,),
        in_specs=[pl.BlockSpec((_CROWS, D_MODEL), lambda l: (0, 0)),
                  pl.BlockSpec((1, D_MODEL, cols), lambda l: (l, 0, 0)),
                  pl.BlockSpec((1, 1, cols), lambda l: (l, 0, 0))],
        out_specs=pl.BlockSpec((1, _CROWS, cols), lambda l: (l, 0, 0)),
        out_shape=jax.ShapeDtypeStruct((n_l, _CROWS, cols), F32),
        compiler_params=_cp("parallel"),
    )(c16, w_mod, bias_k)


def _adaln_bwd(c16, d16, w_mod, dmmc_k, name):
    n_l, _, cols = w_mod.shape

    def body(c_ref, d_ref, w_ref, dm_ref, gw_ref, cp_ref):
        layer = pl.program_id(0)
        cv = c_ref[...]
        gw_ref[0] = _dot_tn_hi(cv * _sigmoid(cv), d_ref[0])

        @pl.when(layer == 0)
        def _():
            cp_ref[...] = jnp.sum(w_ref[0] * dm_ref[...], axis=1, keepdims=True)

    return pl.pallas_call(
        body, name=name, grid=(n_l,),
        in_specs=[pl.BlockSpec((_CROWS, D_MODEL), lambda l: (0, 0)),
                  pl.BlockSpec((1, _CROWS, cols), lambda l: (l, 0, 0)),
                  pl.BlockSpec((1, D_MODEL, cols), lambda l: (0, 0, 0)),
                  pl.BlockSpec((1, cols), lambda l: (0, 0))],
        out_specs=[pl.BlockSpec((1, D_MODEL, cols), lambda l: (l, 0, 0)),
                   pl.BlockSpec((D_MODEL, 1), lambda l: (0, 0))],
        out_shape=[jax.ShapeDtypeStruct((n_l, D_MODEL, cols), F32), jax.ShapeDtypeStruct((D_MODEL, 1), F32)],
        compiler_params=_cp("arbitrary"),
    )(c16, d16, w_mod, dmmc_k)


def _cctx_grad(cparts, c_ctx2, name):
    def body(p_ref, c_ref, o_ref):
        tot = ((p_ref[0] + p_ref[2]) + p_ref[4]) + p_ref[6]
        cv = c_ref[...]
        sg = _sigmoid(cv)
        o_ref[...] = tot * (sg * (1.0 + cv * (1.0 - sg)))

    return pl.pallas_call(
        body, name=name, out_shape=jax.ShapeDtypeStruct((8, 128), F32),
        in_specs=[pl.BlockSpec(memory_space=pltpu.VMEM), pl.BlockSpec(memory_space=pltpu.VMEM)],
        out_specs=pl.BlockSpec(memory_space=pltpu.VMEM),
    )(cparts, c_ctx2)


def _sum_devices(ga, name):
    def body(g_ref, o_ref):
        acc = g_ref[0]
        for d in range(1, N_DEV):
            acc = acc + g_ref[d]
        o_ref[...] = acc

    return pl.pallas_call(
        body, name=name, out_shape=jax.ShapeDtypeStruct(ga.shape[1:], F32),
        in_specs=[pl.BlockSpec(memory_space=pltpu.VMEM)], out_specs=pl.BlockSpec(memory_space=pltpu.VMEM),
    )(ga)


def _place():
    return lax.axis_index("x"), lax.axis_index("y"), lax.axis_index("c")


def _flip(a, d):
    return 1 - a if d else a


_CHIP_FLIPS = ((1, 0), (0, 1), (1, 1))


def _allgather_small(v, name, after=()):
    r, cc = v.shape

    def body(v_ref, *rest):
        out_ref, send_sems, recv_sems, local_sem = rest[-4:]
        x, y, c = _place()
        me = 4 * x + 2 * y + c
        mine = pltpu.make_async_copy(v_ref, out_ref.at[me], local_sem)
        mine.start()
        sends = []
        for k in range(1, N_DEV):
            peer = (_flip(x, (k >> 2) & 1), _flip(y, (k >> 1) & 1), _flip(c, k & 1))
            cp = pltpu.make_async_remote_copy(src_ref=v_ref, dst_ref=out_ref.at[me], send_sem=send_sems.at[k - 1],
                                              recv_sem=recv_sems.at[k - 1], device_id=peer, device_id_type=MESH)
            cp.start()
            sends.append(cp)
        for k in range(1, N_DEV):
            px, py, pc = _flip(x, (k >> 2) & 1), _flip(y, (k >> 1) & 1), _flip(c, k & 1)
            pltpu.make_async_remote_copy(src_ref=v_ref, dst_ref=out_ref.at[4 * px + 2 * py + pc],
                                         send_sem=send_sems.at[k - 1], recv_sem=recv_sems.at[k - 1],
                                         device_id=(px, py, pc), device_id_type=MESH).wait_recv()
        for cp in sends:
            cp.wait_send()
        mine.wait()

    return pl.pallas_call(
        body, name=name, out_shape=jax.ShapeDtypeStruct((N_DEV, r, cc), F32),
        in_specs=[pl.BlockSpec(memory_space=pltpu.VMEM)] + [pl.BlockSpec(memory_space=pl.ANY)] * len(after),
        out_specs=pl.BlockSpec(memory_space=pltpu.VMEM),
        scratch_shapes=[pltpu.SemaphoreType.DMA((N_DEV - 1,)), pltpu.SemaphoreType.DMA((N_DEV - 1,)),
                        pltpu.SemaphoreType.DMA],
        compiler_params=pltpu.CompilerParams(vmem_limit_bytes=VMEM_LIMIT_BYTES),
    )(v, *after)


_HBM_SPEC = pl.BlockSpec(memory_space=pltpu.HBM)
_SEM_SPEC = pl.BlockSpec(memory_space=pltpu.SEMAPHORE)
_EFFECT = pltpu.SideEffectType.DATAFLOW_SIDE_EFFECTING


def _in_hbm(a):
    return pltpu.with_memory_space_constraint(a, pltpu.HBM)


def _gather_start(arrs, groups, after, name):
    n, n_g = len(arrs), len(groups)

    def body(*refs):
        ins, zones = refs[:n], refs[n:2 * n]
        sems = refs[2 * n + 1:2 * n + 1 + 2 * n_g]
        token = refs[2 * n + 1 + 2 * n_g + 2 * n]
        x, y, c = _place()
        k_me = 2 * x + y
        for g, members in enumerate(groups):
            for t, a in enumerate(members):
                for j, (dx, dy) in enumerate(_CHIP_FLIPS):
                    pltpu.make_async_remote_copy(
                        src_ref=ins[a], dst_ref=zones[a].at[k_me], send_sem=sems[2 * g].at[3 * t + j],
                        recv_sem=sems[2 * g + 1].at[3 * t + j], device_id=(_flip(x, dx), _flip(y, dy), c),
                        device_id_type=MESH).start()
        token[...] = jnp.zeros_like(token)

    k_own = 2 * lax.axis_index("x") + lax.axis_index("y")
    zones = [lax.dynamic_update_slice(lax.empty((N_CHIPS,) + a.shape, a.dtype), a[None], (k_own,) + (0,) * a.ndim)
             for a in arrs]
    sem_shapes = []
    for members in groups:
        sem_shapes += [pltpu.SemaphoreType.DMA((3 * len(members),))] * 2
    outs = pl.pallas_call(
        body, name=name,
        out_shape=sem_shapes + [pltpu.HBM(a.shape, a.dtype) for a in arrs]
        + [pltpu.HBM(z.shape, z.dtype) for z in zones] + [jax.ShapeDtypeStruct((8, 128), F32)],
        in_specs=[_HBM_SPEC] * (2 * n) + [pl.BlockSpec(memory_space=pl.ANY)],
        out_specs=[_SEM_SPEC] * (2 * n_g) + [_HBM_SPEC] * (2 * n) + [pl.BlockSpec(memory_space=pltpu.VMEM)],
        input_output_aliases={i: 2 * n_g + i for i in range(2 * n)},
        compiler_params=pltpu.CompilerParams(has_side_effects=_EFFECT),
    )(*[_in_hbm(a) for a in arrs], *[_in_hbm(z) for z in zones], after)
    sems = outs[:2 * n_g]
    thru = outs[2 * n_g:2 * n_g + n]
    zones = outs[2 * n_g + n:2 * n_g + 2 * n]
    return [(sems[2 * g], sems[2 * g + 1]) for g in range(n_g)], thru, zones, outs[-1]


def _gather_wait(shards, zones, send_sems, recv_sems, after, name):
    m = len(shards)

    def body(*refs):
        ins, zs = refs[:m], refs[m:2 * m]
        ssem, rsem = refs[2 * m], refs[2 * m + 1]
        x, y, c = _place()
        for t in range(m):
            for j, (dx, dy) in enumerate(_CHIP_FLIPS):
                px, py = _flip(x, dx), _flip(y, dy)
                cp = pltpu.make_async_remote_copy(
                    src_ref=ins[t], dst_ref=zs[t].at[2 * px + py], send_sem=ssem.at[3 * t + j],
                    recv_sem=rsem.at[3 * t + j], device_id=(px, py, c), device_id_type=MESH)
                cp.wait_send()
                cp.wait_recv()

    after = list(after) if isinstance(after, (list, tuple)) else [after]
    outs = pl.pallas_call(
        body, name=name,
        out_shape=[pltpu.HBM(a.shape, a.dtype) for a in list(shards) + list(zones)],
        in_specs=[_HBM_SPEC] * (2 * m) + [_SEM_SPEC, _SEM_SPEC] + [pl.BlockSpec(memory_space=pl.ANY)] * len(after),
        out_specs=[_HBM_SPEC] * (2 * m),
        input_output_aliases={i: i for i in range(2 * m)},
        compiler_params=pltpu.CompilerParams(has_side_effects=_EFFECT),
    )(*shards, *zones, send_sems, recv_sems, *after)
    return outs[m:]


def _scatter_start(arrs, name):
    n = len(arrs)

    def body(*refs):
        ins, lands = refs[:n], refs[n:2 * n]
        ssem, rsem = refs[2 * n], refs[2 * n + 1]
        token = refs[2 * n + 2 + 2 * n]
        x, y, c = _place()
        for a in range(n):
            for j, (dx, dy) in enumerate(_CHIP_FLIPS):
                px, py = _flip(x, dx), _flip(y, dy)
                pltpu.make_async_remote_copy(
                    src_ref=ins[a].at[2 * px + py], dst_ref=lands[a].at[j], send_sem=ssem.at[3 * a + j],
                    recv_sem=rsem.at[3 * a + j], device_id=(px, py, c), device_id_type=MESH).start()
        token[...] = jnp.zeros_like(token)

    lands = [lax.empty((3,) + a.shape[1:], a.dtype) for a in arrs]
    outs = pl.pallas_call(
        body, name=name,
        out_shape=[pltpu.SemaphoreType.DMA((3 * n,))] * 2 + [pltpu.HBM(a.shape, a.dtype) for a in arrs]
        + [pltpu.HBM(z.shape, z.dtype) for z in lands] + [jax.ShapeDtypeStruct((8, 128), F32)],
        in_specs=[_HBM_SPEC] * (2 * n),
        out_specs=[_SEM_SPEC] * 2 + [_HBM_SPEC] * (2 * n) + [pl.BlockSpec(memory_space=pltpu.VMEM)],
        input_output_aliases={i: 2 + i for i in range(2 * n)},
        compiler_params=pltpu.CompilerParams(has_side_effects=_EFFECT),
    )(*[_in_hbm(a) for a in arrs], *[_in_hbm(z) for z in lands])
    return outs[0], outs[1], outs[2:2 + n], outs[2 + n:2 + 2 * n], outs[-1]


def _scatter_wait(arrs, lands, send_sems, recv_sems, after, name):
    n = len(arrs)

    def body(*refs):
        ins, lz = refs[:n], refs[n:2 * n]
        ssem, rsem = refs[2 * n], refs[2 * n + 1]
        x, y, c = _place()
        for a in range(n):
            for j, (dx, dy) in enumerate(_CHIP_FLIPS):
                px, py = _flip(x, dx), _flip(y, dy)
                cp = pltpu.make_async_remote_copy(
                    src_ref=ins[a].at[2 * px + py], dst_ref=lz[a].at[j], send_sem=ssem.at[3 * a + j],
                    recv_sem=rsem.at[3 * a + j], device_id=(px, py, c), device_id_type=MESH)
                cp.wait_send()
                cp.wait_recv()

    outs = pl.pallas_call(
        body, name=name,
        out_shape=[pltpu.HBM(a.shape, a.dtype) for a in list(arrs) + list(lands)],
        in_specs=[_HBM_SPEC] * (2 * n) + [_SEM_SPEC, _SEM_SPEC, pl.BlockSpec(memory_space=pl.ANY)],
        out_specs=[_HBM_SPEC] * (2 * n),
        input_output_aliases={i: i for i in range(2 * n)},
        compiler_params=pltpu.CompilerParams(has_side_effects=_EFFECT),
    )(*arrs, *lands, send_sems, recv_sems, after)
    return outs[:n], outs[n:]


def _swap_start(arrs, name):
    n = len(arrs)

    def body(*refs):
        ins, lands = refs[:n], refs[n:2 * n]
        ssem, rsem = refs[2 * n], refs[2 * n + 1]
        token = refs[2 * n + 2 + 2 * n]
        x, y, c = _place()
        for a in range(n):
            pltpu.make_async_remote_copy(src_ref=ins[a], dst_ref=lands[a], send_sem=ssem.at[a], recv_sem=rsem.at[a],
                                         device_id=(x, y, 1 - c), device_id_type=MESH).start()
        token[...] = jnp.zeros_like(token)

    lands = [lax.empty(a.shape, a.dtype) for a in arrs]
    outs = pl.pallas_call(
        body, name=name,
        out_shape=[pltpu.SemaphoreType.DMA((n,))] * 2 + [pltpu.HBM(a.shape, a.dtype) for a in arrs]
        + [pltpu.HBM(z.shape, z.dtype) for z in lands] + [jax.ShapeDtypeStruct((8, 128), F32)],
        in_specs=[_HBM_SPEC] * (2 * n),
        out_specs=[_SEM_SPEC] * 2 + [_HBM_SPEC] * (2 * n) + [pl.BlockSpec(memory_space=pltpu.VMEM)],
        input_output_aliases={i: 2 + i for i in range(2 * n)},
        compiler_params=pltpu.CompilerParams(has_side_effects=_EFFECT),
    )(*[_in_hbm(a) for a in arrs], *[_in_hbm(z) for z in lands])
    return outs[0], outs[1], outs[2:2 + n], outs[2 + n:2 + 2 * n], outs[-1]


def _swap_wait(arrs, lands, send_sems, recv_sems, after, name):
    n = len(arrs)

    def body(*refs):
        ins, lz = refs[:n], refs[n:2 * n]
        ssem, rsem = refs[2 * n], refs[2 * n + 1]
        x, y, c = _place()
        for a in range(n):
            cp = pltpu.make_async_remote_copy(src_ref=ins[a], dst_ref=lz[a], send_sem=ssem.at[a], recv_sem=rsem.at[a],
                                              device_id=(x, y, 1 - c), device_id_type=MESH)
            cp.wait_send()
            cp.wait_recv()

    outs = pl.pallas_call(
        body, name=name,
        out_shape=[pltpu.HBM(a.shape, a.dtype) for a in list(arrs) + list(lands)],
        in_specs=[_HBM_SPEC] * (2 * n) + [_SEM_SPEC, _SEM_SPEC, pl.BlockSpec(memory_space=pl.ANY)],
        out_specs=[_HBM_SPEC] * (2 * n),
        input_output_aliases={i: i for i in range(2 * n)},
        compiler_params=pltpu.CompilerParams(has_side_effects=_EFFECT),
    )(*arrs, *lands, send_sems, recv_sems, after)
    return outs[:n], outs[n:]


def _row_tile(rows, cols):
    for tr in (1024, 512, 256, 128, 64, 32, 16, 8):
        if rows % tr == 0 and tr * cols * 4 <= (1 << 20):
            return tr
    return rows


def _partial_sum(g_full, recv, k_idx, name):
    _, r, c = g_full.shape
    tr = _row_tile(r, c)

    def body(k_ref, g_ref, r_ref, o_ref):
        del k_ref
        acc = g_ref[0].astype(F32)
        for j in range(3):
            acc = acc + r_ref[j].astype(F32)
        o_ref[...] = acc

    return pl.pallas_call(
        body, name=name,
        grid_spec=pltpu.PrefetchScalarGridSpec(
            num_scalar_prefetch=1, grid=(r // tr,),
            in_specs=[pl.BlockSpec((1, tr, c), lambda i, k: (k[0], i, 0)),
                      pl.BlockSpec((3, tr, c), lambda i, k: (0, i, 0))],
            out_specs=pl.BlockSpec((tr, c), lambda i, k: (i, 0))),
        out_shape=jax.ShapeDtypeStruct((r, c), F32),
        compiler_params=_cp("parallel"),
    )(k_idx, g_full, recv)


def _adamw(w3, parts, m3, v3, layer, prev, name):
    n_l, r, c = w3.shape
    tr = _row_tile(r, c)
    n_i = r // tr
    n_p = len(parts)
    c1 = 1.0 - ADAM_B1 ** ADAM_STEP
    c2 = 1.0 - ADAM_B2 ** ADAM_STEP
    stacked = [isinstance(p, tuple) for p in parts]

    def body(*refs):
        w_ref, m_ref, v_ref = refs[0:3]
        g_refs = refs[3:3 + n_p]
        go_ref, d_ref, mo_ref, vo_ref = refs[-4:]
        g = None
        for p in range(n_p):
            term = g_refs[p][0] if stacked[p] else g_refs[p][...]
            g = term if g is None else g + term
        w = w_ref[0]
        m = ADAM_B1 * m_ref[0] + (1.0 - ADAM_B1) * g
        v = ADAM_B2 * v_ref[0] + (1.0 - ADAM_B2) * (g * g)
        m_hat = m / c1
        v_hat = v / c2
        go_ref[0] = g
        d_ref[0] = -ADAM_LR * (m_hat / (jnp.sqrt(v_hat) + ADAM_EPS) + ADAM_WD * w)
        mo_ref[0] = m
        vo_ref[0] = v

    blk = pl.BlockSpec((1, tr, c), lambda i: (layer, i, 0))
    in_specs = [blk, blk, blk]
    args = [w3, m3, v3]
    for part in parts:
        if isinstance(part, tuple):
            in_specs.append(pl.BlockSpec((1, tr, c), functools.partial(lambda idx, i: (idx, i, 0), part[1])))
            args.append(part[0])
        else:
            in_specs.append(pl.BlockSpec((tr, c), lambda i: (i, 0)))
            args.append(part)
    aliases = {}
    if prev is not None:
        in_specs += [pl.BlockSpec(memory_space=pl.ANY)] * 4
        aliases = {len(args) + q: q for q in range(4)}
        args += list(prev)
    shp = jax.ShapeDtypeStruct((n_l, r, c), F32)
    return pl.pallas_call(
        body, name=name, grid=(n_i,), in_specs=in_specs, out_specs=[blk] * 4, out_shape=[shp] * 4,
        input_output_aliases=aliases, compiler_params=_cp("parallel"),
    )(*args)


_SMALL_W = 4096
_PACK_ROWS = 352
_N9 = N_MOD * D_MODEL


def _flat_pad(parts, total):
    flat = jnp.concatenate([p.reshape(-1) for p in parts])
    return jnp.concatenate([flat, jnp.zeros((total - flat.shape[0],), F32)])


def kernel(x, c, ctx, c_ctx, w_mod, b_mod, norm_g, ffn1_wi, ffn1_wo, ffn2_wi, ffn2_wo, w_in, w_a2_f, b_a_f, w_a2_b, b_a_b, sink, gla_g, w_out, w_pool, pool_scale, final_g, loss_target, m_c_ctx, m_w_mod, m_b_mod, m_norm_g, m_ffn1_wi, m_ffn1_wo, m_ffn2_wi, m_ffn2_wo, m_w_in, m_w_a2_f, m_b_a_f, m_w_a2_b, m_b_a_b, m_sink, m_gla_g, m_w_out, m_w_pool, m_pool_scale, m_final_g, v_c_ctx, v_w_mod, v_b_mod, v_norm_g, v_ffn1_wi, v_ffn1_wo, v_ffn2_wi, v_ffn2_wo, v_w_in, v_w_a2_f, v_b_a_f, v_w_a2_b, v_b_a_b, v_sink, v_gla_g, v_w_out, v_w_pool, v_pool_scale, v_final_g):
    t_len, l_ctx = x.shape[1], ctx.shape[1]
    tm = ROW_TILE
    pad = (-(t_len + l_ctx)) % tm
    rows0 = t_len + l_ctx + pad
    n_x = t_len // tm
    xi, yi, ci = _place()
    k_me = 2 * xi + yi
    me = 4 * xi + 2 * yi + ci
    mod_cols = w_mod.shape[2]
    n_grp = len(POOL_WINDOWS)

    small_w = _flat_pad([norm_g, w_a2_f, w_a2_b, pool_scale], _SMALL_W).reshape(_SMALL_W // 128, 128)
    shards = [ffn1_wi[0], ffn1_wi[1], ffn1_wo[0], ffn1_wo[1], ffn2_wi[0], ffn2_wi[1], ffn2_wo[0], ffn2_wo[1],
              w_in[0], w_out[0], w_pool[0].reshape(n_grp * w_pool.shape[2], POOL_GROUP)]

    send_src = [s.astype(BF16) for s in shards] + [small_w]
    groups = ([11, 0], [2], [8, 9], [4], [6], [1], [3], [10, 5], [7])
    started = {}

    def gather_start(gs, after):
        members, index, pos = [], [], 0
        for g in gs:
            members += groups[g]
            index.append(tuple(range(pos, pos + len(groups[g]))))
            pos += len(groups[g])
        sems, thru, zones, token = _gather_start([send_src[a] for a in members], tuple(index), after,
                                                 "gather_start_%d" % gs[0])
        for k, (g, idx) in enumerate(zip(gs, index)):
            started[g] = (sems[k], [thru[i] for i in idx], [zones[i] for i in idx])
        return token

    def gather_wait(g, after):
        (ssem, rsem), thru, zones = started[g]
        return dict(zip(groups[g], _gather_wait(thru, zones, ssem, rsem, after, "gather_wait_%d" % g)))

    c_all = _allgather_small(c.reshape(8, 128), "gather_cond").reshape(N_DEV, D_MODEL)
    tok = gather_start((0,), c_all)
    c16 = jnp.concatenate([c_all, c_ctx[None], jnp.zeros((_CROWS - N_DEV - 1, D_MODEL), F32)], axis=0) + tok[0:1, 0:1]
    bias_k = lax.dynamic_slice(b_mod, (0, k_me * mod_cols), (2, mod_cols)).reshape(2, 1, mod_cols)
    mm_k = _adaln_fwd(c16, w_mod, bias_k, "adaln_fwd")
    cs = _rope_tables(t_len, rows0)
    xcat = jnp.concatenate([x[0], ctx[0], jnp.zeros((pad, D_MODEL), F32)], axis=0)
    mm_all = _allgather_small(mm_k.reshape(-1, 128), "gather_mod", (cs, xcat)).reshape(N_DEV, 2, _CROWS, mod_cols)
    mm_full = jnp.concatenate([mm_all[2 * k] for k in range(N_CHIPS)], axis=-1)
    mm_x = lax.dynamic_index_in_dim(mm_full, me, axis=1, keepdims=False)
    mm_c = mm_full[:, N_DEV]
    mods = [jnp.stack([mm_x[l].reshape(N_MOD, D_MODEL), mm_c[l].reshape(N_MOD, D_MODEL)]) for l in range(2)]
    gathered = gather_wait(0, mods[0])
    sw = gathered[11].reshape(N_CHIPS, _SMALL_W)
    ng_n = norm_g.size
    a2_n = w_a2_f.size
    norm_g_full = jnp.concatenate([sw[k, :ng_n].reshape(norm_g.shape) for k in range(N_CHIPS)], axis=-1)
    w_a2_f_full = jnp.concatenate([sw[k, ng_n:ng_n + a2_n].reshape(w_a2_f.shape[1:]) for k in range(N_CHIPS)], axis=-1)
    w_a2_b_full = jnp.concatenate(
        [sw[k, ng_n + a2_n:ng_n + 2 * a2_n].reshape(w_a2_b.shape[1:]) for k in range(N_CHIPS)], axis=-1)
    pscale_full = jnp.concatenate(
        [sw[k, ng_n + 2 * a2_n:ng_n + 2 * a2_n + pool_scale.size] for k in range(N_CHIPS)]).reshape(1, D_MODEL)
    wg2, bias2 = _gate_weights(w_a2_f_full, b_a_f[0], w_a2_b_full, b_a_b[0])
    gla_g2 = gla_g.reshape(1, B_DV)
    final_g2 = final_g.reshape(1, D_MODEL)

    g3 = [norm_g_full[0], norm_g_full[1]]

    w1i, w1o, w2i, w2o = [None, None], [None, None], [None, None], [None, None]
    w1i[0] = gathered[0]
    mods_a = mods[0] + gather_start((1, 2), w1i[0])[0:1, 0:1]
    x1, sv_a1, w1o[0] = _ffn_forward(xcat, g3[0], mods_a, 0, w1i[0],
                                     lambda s: (gather_wait(1, s)[2], gather_start((3, 4), s)), n_x, "l0_ffn1")
    gathered = gather_wait(2, x1)
    w_in_full = jnp.concatenate([gathered[8][k] for k in range(N_CHIPS)], axis=1)
    wcat = _w_in_to_cat(w_in_full)
    w_out_full = gathered[9].reshape(D_MODEL, D_MODEL)
    pace_groups = {"proj": (5, 6), "attn": (7, 8)}
    no_dep = jnp.zeros((8, 128), F32)
    x2, sv_am = _mixer_ab_forward(
        x1, g3[0], mods[0], wcat, wg2, bias2, sink[0], gla_g2, w_out_full, cs, t_len, l_ctx, n_x,
        lambda tag, res_: gather_start(pace_groups[tag], res_) if tag in pace_groups else no_dep)
    mods_a = mods[0]
    w2i[0], w2o[0] = gather_wait(3, x2)[4], gather_wait(4, x2)[6]
    x3, sv_a2, _ = _ffn_forward(x2, g3[0], mods_a, 2, w2i[0], lambda s: (w2o[0], None), n_x, "l0_ffn2")
    w1i[1], w1o[1] = gather_wait(5, x3)[1], gather_wait(6, x3)[3]
    x4, sv_b1, _ = _ffn_forward(x3, g3[1], mods[1], 0, w1i[1], lambda s: (w1o[1], None), n_x, "l1_ffn1")
    gathered = gather_wait(7, x4)
    w2i[1] = gathered[5]
    wp_full = gathered[10].reshape(N_CHIPS, n_grp, -1, POOL_GROUP).transpose(1, 0, 2, 3).reshape(
        n_grp, POOL_GROUP, POOL_GROUP)
    x5, sv_bm = _mixer_pool_forward(x4, g3[1], mods[1], wp_full, pscale_full, t_len)
    x6, sv_b2, w2o[1] = _ffn_forward(x5, g3[1], mods[1], 2, w2i[1], lambda s: (gather_wait(8, s)[7], None), n_x,
                                     "l1_ffn2")
    dx6, loss_part, d_final_g = _final_loss(x6, final_g2, loss_target[0], "final_loss")
    loss = lax.psum(loss_part[0, 0], ("x", "y", "c"))

    sent = []

    def sender(weight, layer):
        def send(grads):
            nm = "%s_%d" % (weight or "_".join(grads), layer)
            ssem, rsem, thru, lands, token = _scatter_start(list(grads.values()), "scatter_start_" + nm)
            targets = [((weight + "_" + tag) if weight else tag, layer) for tag in grads]
            sent.append((nm, targets, thru, lands, ssem, rsem))
            return token[0:1, 0:1]
        return send

    dx5, st_b2, dg_b2 = _ffn_backward(dx6, sv_b2, g3[1], mods[1], 2, w2i[1], w2o[1], n_x, sender("ffn2", 1),
                                      "l1_ffn2_b")
    dx4, st_bm, dg_bm, d_pscale, d_wp = _mixer_pool_backward(dx5, sv_bm, g3[1], mods[1], wp_full, pscale_full, t_len)
    d_wp4 = d_wp.reshape(n_grp, N_CHIPS, -1, POOL_GROUP).transpose(1, 0, 2, 3).reshape(N_CHIPS, -1, POOL_GROUP)
    mods1 = mods[1] + sender("", 0)({"w_pool": d_wp4})
    dx3, st_b1, dg_b1 = _ffn_backward(dx4, sv_b1, g3[1], mods1, 0, w1i[1], w1o[1], n_x, sender("ffn1", 1),
                                      "l1_ffn1_b")
    dx2, st_a2, dg_a2 = _ffn_backward(dx3, sv_a2, g3[0], mods[0], 2, w2i[0], w2o[0], n_x, sender("ffn2", 0),
                                      "l0_ffn2_b")
    dx1, st_am, dg_am, d_wcat, d_wg2, d_bias2, d_sink, d_glag, d_wout = _mixer_ab_backward(
        dx2, sv_am, g3[0], mods[0], wcat, wg2, bias2, sink[0], gla_g2, w_out_full, cs, t_len, l_ctx, n_x)
    d_w_in4 = _cat_to_w_in(d_wcat).reshape(D_MODEL, N_CHIPS, -1).transpose(1, 0, 2)
    mods0 = mods[0] + sender("", 0)({"w_in": d_w_in4, "w_out": d_wout.reshape(N_CHIPS, -1, D_MODEL)})
    dx0, st_a1, dg_a1 = _ffn_backward(dx1, sv_a1, g3[0], mods0, 0, w1i[0], w1o[0], n_x, sender("ffn1", 0),
                                      "l0_ffn1_b", out_tiles=n_x)
    grad_x = dx0[None]

    def as3(a):
        n_l = a.shape[0] if a.ndim == 3 else 1
        return a.reshape(n_l, -1, a.shape[-1])

    res = {}
    big_w = {"ffn1_wi": (ffn1_wi, m_ffn1_wi, v_ffn1_wi), "ffn1_wo": (ffn1_wo, m_ffn1_wo, v_ffn1_wo),
             "ffn2_wi": (ffn2_wi, m_ffn2_wi, v_ffn2_wi), "ffn2_wo": (ffn2_wo, m_ffn2_wo, v_ffn2_wo),
             "w_in": (w_in, m_w_in, v_w_in), "w_out": (w_out, m_w_out, v_w_out), "w_pool": (w_pool, m_w_pool, v_w_pool)}
    k_idx = k_me.reshape(1).astype(jnp.int32)
    chain = dx0
    def finish(swap, after):
        lo, targets, s_sem, r_sem, s_thru, s_lands = swap
        mine, other = _swap_wait(s_thru, s_lands, s_sem, r_sem, after, "swap_wait_%d" % lo)
        last = after
        for (wname, layer), p, q in zip(targets, mine, other):
            w, m, v = big_w[wname]
            res[wname] = _adamw(as3(w), [p, q], as3(m), as3(v), layer, res.get(wname),
                                "adamw_%s_%d" % (wname, layer))
            last = res[wname][3]
        return last

    swap = None
    for lo, hi in ((0, 1), (1, 3), (3, 4), (4, 5), (5, 6)):
        partial, targets = [], []
        for nm, sent_targets, thru, lands, ssem, rsem in sent[lo:hi]:
            mine, recv = _scatter_wait(thru, lands, ssem, rsem, chain, "scatter_wait_" + nm)
            for k, (wname, layer) in enumerate(sent_targets):
                partial.append(_partial_sum(mine[k], recv[k], k_idx, "partial_sum_%s_%d" % (wname, layer)))
                targets.append((wname, layer))
        s_sem, r_sem, s_thru, s_lands, token = _swap_start(partial, "swap_start_%d" % lo)
        if swap is not None:
            chain = finish(swap, token)
        swap = (lo, targets, s_sem, r_sem, s_thru, s_lands)

    def mod_row(st1, dg1, stm, dgm, st2, dg2, s):
        return jnp.concatenate([st1[s, 0], st1[s, 1], dg1[s, 0], stm[s, 0], stm[s, 1], dgm[s, 0],
                                st2[s, 0], st2[s, 1], dg2[s, 0]])

    dg_bm2 = jnp.concatenate([dg_bm, jnp.zeros_like(dg_bm)], axis=0)[:, None, :]
    d_mm_x0 = mod_row(st_a1, dg_a1, st_am, dg_am, st_a2, dg_a2, 0)
    d_mm_x1 = mod_row(st_b1, dg_b1, st_bm, dg_bm2, st_b2, dg_b2, 0)
    d_mm_c0 = mod_row(st_a1, dg_a1, st_am, dg_am, st_a2, dg_a2, 1)
    d_norm_g = jnp.stack([jnp.stack([st[0, 2] + st[1, 2] for st in (st_a1, st_am, st_a2)]),
                          jnp.stack([st[0, 2] + st[1, 2] for st in (st_b1, st_bm, st_b2)])])
    rk = B_GATE_RANK
    pack = _flat_pad([d_mm_x0, d_mm_x1, d_mm_c0, d_norm_g, d_bias2, d_wg2[0:rk, 0:256], d_wg2[rk:2 * rk, 256:512],
                      d_sink[:, 0], jnp.zeros((120,), F32), d_glag, d_pscale, d_final_g],
                     _PACK_ROWS * 128).reshape(_PACK_ROWS, 128)
    pack = pack + 0.0 * chain[0, 0:1, 0:1]
    pack_all = _allgather_small(pack, "gather_small_grads")
    tot = _sum_devices(pack_all, "sum_small_grads").reshape(-1)
    rows_all = pack_all.reshape(N_DEV, -1)
    o = 3 * _N9
    g_norm_g_full = tot[o:o + 6 * D_MODEL].reshape(2, 3, D_MODEL)
    o += 6 * D_MODEL
    g_bias2 = tot[o:o + 512]
    o += 512
    g_w_a2_f_full = tot[o:o + rk * 256].reshape(rk, 256)
    o += rk * 256
    g_w_a2_b_full = tot[o:o + rk * 256].reshape(rk, 256)
    o += rk * 256
    g_sink = tot[o:o + A_HEADS]
    o += 128
    g_gla_g = tot[o:o + B_DV]
    o += B_DV
    g_pscale_full = tot[o:o + D_MODEL]
    o += D_MODEL
    g_final_g = tot[o:o + D_MODEL]
    d_mmc_tot = tot[2 * _N9:3 * _N9]
    g_b_mod = jnp.stack([tot[0:_N9] + d_mmc_tot, tot[_N9:2 * _N9]])

    zrows = jnp.zeros((_CROWS - N_DEV - 1, _N9), F32)
    d16 = jnp.stack([jnp.concatenate([rows_all[:, 0:_N9], d_mmc_tot[None], zrows], axis=0),
                     jnp.concatenate([rows_all[:, _N9:2 * _N9], jnp.zeros((1, _N9), F32), zrows], axis=0)])
    d16_k = lax.dynamic_slice(d16, (0, 0, k_me * mod_cols), (2, _CROWS, mod_cols))
    dmmc_k = lax.dynamic_slice(d_mmc_tot, (k_me * mod_cols,), (mod_cols,)).reshape(1, mod_cols)
    g_w_mod, c_part = _adaln_bwd(c16, d16_k, w_mod, dmmc_k, "adaln_bwd")
    c_parts = _allgather_small(c_part.reshape(8, 128), "gather_cctx")
    g_c_ctx = _cctx_grad(c_parts, c_ctx.reshape(8, 128), "cctx_grad").reshape(D_MODEL)

    def small(w, g, m, v, shape3, nm):
        return [o_.reshape(w.shape) for o_ in _adamw(w.reshape(shape3), [g.reshape(shape3[1:])],
                                                    m.reshape(shape3), v.reshape(shape3), 0, None, "adamw_" + nm)]

    def own(a, axis, size):
        return lax.dynamic_slice_in_dim(a, k_me * size, size, axis=axis)

    res["c_ctx"] = small(c_ctx, g_c_ctx, m_c_ctx, v_c_ctx, (1, 8, 128), "c_ctx")
    upd = _adamw(w_mod, [(g_w_mod, 1)], m_w_mod, v_w_mod, 1, None, "adamw_w_mod_1")
    res["w_mod"] = _adamw(w_mod, [(g_w_mod, 0)], m_w_mod, v_w_mod, 0, upd, "adamw_w_mod_0")
    res["b_mod"] = small(b_mod, g_b_mod, m_b_mod, v_b_mod, (1, 2, _N9), "b_mod")
    res["norm_g"] = small(norm_g, own(g_norm_g_full, 2, norm_g.shape[2]), m_norm_g, v_norm_g,
                          (1, 6, norm_g.shape[2]), "norm_g")
    res["w_a2_f"] = small(w_a2_f, own(g_w_a2_f_full, 1, w_a2_f.shape[2]), m_w_a2_f, v_w_a2_f,
                          (1, rk, w_a2_f.shape[2]), "w_a2_f")
    res["b_a_f"] = small(b_a_f, g_bias2[0:256], m_b_a_f, v_b_a_f, (1, 1, 256), "b_a_f")
    res["w_a2_b"] = small(w_a2_b, own(g_w_a2_b_full, 1, w_a2_b.shape[2]), m_w_a2_b, v_w_a2_b,
                          (1, rk, w_a2_b.shape[2]), "w_a2_b")
    res["b_a_b"] = small(b_a_b, g_bias2[256:512], m_b_a_b, v_b_a_b, (1, 1, 256), "b_a_b")
    res["sink"] = small(sink, g_sink, m_sink, v_sink, (1, 1, A_HEADS), "sink")
    res["gla_g"] = small(gla_g, g_gla_g, m_gla_g, v_gla_g, (1, 1, B_DV), "gla_g")
    res["pool_scale"] = small(pool_scale, own(g_pscale_full, 0, pool_scale.shape[1]), m_pool_scale, v_pool_scale,
                              (1, 1, pool_scale.shape[1]), "pool_scale")
    res["final_g"] = small(final_g, g_final_g, m_final_g, v_final_g, (1, 8, 128), "final_g")
    finish(swap, res["final_g"][0])
    for wname, (w, _, _) in big_w.items():
        res[wname] = [o_.reshape(w.shape) for o_ in res[wname]]

    names = ["c_ctx", "w_mod", "b_mod", "norm_g", "ffn1_wi", "ffn1_wo", "ffn2_wi", "ffn2_wo", "w_in", "w_a2_f",
             "b_a_f", "w_a2_b", "b_a_b", "sink", "gla_g", "w_out", "w_pool", "pool_scale", "final_g"]
    outs = [loss, grad_x]
    for field in range(4):
        outs += [res[nm][field] for nm in names]
    return tuple(outs)
```

```python
import functools

import jax
import jax.numpy as jnp
from jax import lax
from jax.experimental import pallas as pl
from jax.experimental.pallas import tpu as pltpu

F32 = jnp.float32
BF16 = jnp.bfloat16

D_MODEL = 1024
N_MOD = 9
D_FF = 2816
RMS_EPS = 1e-6
A_HEADS = 8
A_KV_HEADS = 2
A_HEAD_DIM = 64
WINDOW = 128
ROPE_BASE = 10000.0
GRID_W = 64
B_HEADS = 4
B_DK = 64
B_DV = 128
B_GATE_RANK = 16
B_GATE_NORM = 16.0
B_CHUNK = 64
POOL_WINDOWS = (2, 4, 8, 16)
POOL_GROUP = D_MODEL // len(POOL_WINDOWS)
PROJ_DIM = 2336

ADAM_LR = 0.001
ADAM_B1 = 0.9
ADAM_B2 = 0.999
ADAM_EPS = 1e-08
ADAM_WD = 0.01
ADAM_STEP = 10

N_CHIPS = 4
N_DEV = 8
ROW_TILE = 512
VMEM_LIMIT_BYTES = 56 * 1024 * 1024
MESH = pl.DeviceIdType.MESH

ZC_Q, ZC_QK, ZC_V, ZC_R, ZC_KV, ZC_G, ZC_W = 0, 512, 1024, 1536, 2048, 2304, 2432


def _cp(*sem):
    return pltpu.CompilerParams(dimension_semantics=sem if sem else None, vmem_limit_bytes=VMEM_LIMIT_BYTES)


def _dot(a, b):
    return jnp.dot(a, b, preferred_element_type=F32)


def _dot_nt(a, b):
    return lax.dot_general(a, b, (((1,), (1,)), ((), ())), preferred_element_type=F32)


def _dot_tn(a, b):
    return lax.dot_general(a, b, (((0,), (0,)), ((), ())), preferred_element_type=F32)


def _dot_tn_hi(a, b):
    return lax.dot_general(a, b, (((0,), (0,)), ((), ())), preferred_element_type=F32,
                           precision=lax.Precision.HIGHEST)


def _sigmoid(x):
    return 1.0 / (1.0 + jnp.exp(-x))


MXU_COLS = 256


def _col_chunks(n):
    return [(c0, min(MXU_COLS, n - c0)) for c0 in range(0, n, MXU_COLS)]


WIDE_ROW_TILE = 1024


def _matmul_row_tile(rows, n_x):
    if rows % WIDE_ROW_TILE == 0 and n_x * ROW_TILE >= rows:
        return WIDE_ROW_TILE
    return ROW_TILE


def _resident(block_shape, index_map):
    return pl.BlockSpec(block_shape, index_map, pipeline_mode=pl.Buffered(1))


def _stream_of(i, n_x):
    return jnp.where(i >= n_x, 1, 0)


def _rms_mod_fwd(x, g3, mods, j, n_x, out_dtype, name):
    rows = x.shape[0]
    tm = ROW_TILE
    n_i = rows // tm

    def body(x_ref, g_ref, m_ref, o_ref):
        xv = x_ref[...]
        r = lax.rsqrt(jnp.mean(xv * xv, axis=-1, keepdims=True) + RMS_EPS)
        g = g_ref[j:j + 1, :]
        shift = m_ref[0, 3 * j:3 * j + 1, :]
        scale = m_ref[0, 3 * j + 1:3 * j + 2, :]
        o_ref[...] = (((xv * r) * g) * (1.0 + scale) + shift).astype(out_dtype)

    return pl.pallas_call(
        body, name=name, grid=(n_i,),
        in_specs=[pl.BlockSpec((tm, D_MODEL), lambda i: (i, 0)),
                  pl.BlockSpec((3, D_MODEL), lambda i: (0, 0)),
                  pl.BlockSpec((1, N_MOD, D_MODEL), lambda i: (_stream_of(i, n_x), 0, 0))],
        out_specs=pl.BlockSpec((tm, D_MODEL), lambda i: (i, 0)),
        out_shape=jax.ShapeDtypeStruct((rows, D_MODEL), out_dtype),
        compiler_params=_cp("parallel"),
    )(x, g3, mods)


def _rms_mod_bwd_tail(dh, xv, g, scale, stream, acc_ref, first):
    r = lax.rsqrt(jnp.mean(xv * xv, axis=-1, keepdims=True) + RMS_EPS)
    xhat = xv * r
    t1 = jnp.sum(dh, axis=0, keepdims=True)
    t2 = jnp.sum(dh * xhat, axis=0, keepdims=True)
    stats = jnp.concatenate([t1, t2 * g, t2 * (1.0 + scale)], axis=0)

    @pl.when(first)
    def _():
        acc_ref[...] = jnp.zeros_like(acc_ref)

    acc_ref[pl.ds(stream, 1)] += stats[None]
    dxh = dh * (g * (1.0 + scale))
    return r * (dxh - xhat * jnp.mean(dxh * xhat, axis=-1, keepdims=True))


def _ffn_up(x, g3, mods, jmod, n_x, w4, name):
    rows = x.shape[0]
    h = w4.shape[2]
    tm = _matmul_row_tile(rows, n_x)
    n_i = rows // tm

    def body(x_ref, g_ref, m_ref, wa_ref, wu_ref, hn_ref, au_ref, s_ref):
        xv = x_ref[...]
        r = lax.rsqrt(jnp.mean(xv * xv, axis=-1, keepdims=True) + RMS_EPS)
        g = g_ref[jmod:jmod + 1, :]
        shift = m_ref[0, 3 * jmod:3 * jmod + 1, :]
        scale = m_ref[0, 3 * jmod + 1:3 * jmod + 2, :]
        hv = (((xv * r) * g) * (1.0 + scale) + shift).astype(BF16)

        @pl.when(pl.program_id(0) == 0)
        def _():
            hn_ref[...] = hv

        for c0, cw in _col_chunks(h):
            cols = slice(c0, c0 + cw)
            a = _dot(hv, wa_ref[0, :, cols])
            u = _dot(hv, wu_ref[0, :, cols])
            sg = _sigmoid(a)
            silu = a * sg
            au_ref[0, :, cols] = (u * (sg * (1.0 + a * (1.0 - sg)))).astype(BF16)
            au_ref[1, :, cols] = silu.astype(BF16)
            s_ref[:, cols] = (silu * u).astype(BF16)

    return pl.pallas_call(
        body, name=name, grid=(2, n_i),
        in_specs=[pl.BlockSpec((tm, D_MODEL), lambda j, i: (i, 0)),
                  pl.BlockSpec((3, D_MODEL), lambda j, i: (0, 0)),
                  pl.BlockSpec((1, N_MOD, D_MODEL), lambda j, i: (_stream_of(i, n_x), 0, 0)),
                  pl.BlockSpec((1, D_MODEL, h), lambda j, i: (j, 0, 0)),
                  pl.BlockSpec((1, D_MODEL, h), lambda j, i: (j + 2, 0, 0))],
        out_specs=[pl.BlockSpec((tm, D_MODEL), lambda j, i: (jnp.where(j == 0, i, n_i - 1), 0)),
                   pl.BlockSpec((2, tm, h), lambda j, i: (0, i, j)),
                   pl.BlockSpec((tm, h), lambda j, i: (i, j))],
        out_shape=[jax.ShapeDtypeStruct((rows, D_MODEL), BF16),
                   jax.ShapeDtypeStruct((2, rows, 2 * h), BF16),
                   jax.ShapeDtypeStruct((rows, 2 * h), BF16)],
        compiler_params=_cp("arbitrary", "arbitrary"),
    )(x, g3, mods, w4, w4)


def _matmul_resid(a, w, xres, mods, gate_idx, coef, n_x, rows, name):
    k = a.shape[1]
    tm = _matmul_row_tile(rows, n_x)
    n_i = rows // tm

    def body(a_ref, w_ref, x_ref, m_ref, o_ref, f_ref):
        av = a_ref[...]
        for c0, cw in _col_chunks(D_MODEL):
            cols = slice(c0, c0 + cw)
            f = _dot(av, w_ref[:, cols])
            f_ref[:, cols] = f.astype(BF16)
            o_ref[:, cols] = x_ref[:, cols] + (coef * m_ref[0, gate_idx:gate_idx + 1, cols]) * f

    return pl.pallas_call(
        body, name=name, grid=(n_i,),
        in_specs=[pl.BlockSpec((tm, k), lambda i: (i, 0)),
                  _resident((k, D_MODEL), lambda i: (0, 0)),
                  pl.BlockSpec((tm, D_MODEL), lambda i: (i, 0)),
                  pl.BlockSpec((1, N_MOD, D_MODEL), lambda i: (_stream_of(i, n_x), 0, 0))],
        out_specs=[pl.BlockSpec((tm, D_MODEL), lambda i: (i, 0)),
                   pl.BlockSpec((tm, D_MODEL), lambda i: (i, 0))],
        out_shape=[jax.ShapeDtypeStruct((rows, D_MODEL), F32),
                   jax.ShapeDtypeStruct((rows, D_MODEL), BF16)],
        compiler_params=_cp("parallel"),
    )(a, w, xres, mods)


def _gate_dy(dout, f, mods, gate_idx, coef, n_x, rows, w, name):
    tm = ROW_TILE
    n_i = rows // tm
    n_out = w.shape[0]

    def body(d_ref, f_ref, m_ref, w_ref, dy_ref, da_ref, acc_ref):
        i = pl.program_id(0)
        dv = d_ref[...]
        gate = m_ref[0, gate_idx:gate_idx + 1, :]
        dyb = (dv * (coef * gate)).astype(BF16)
        dy_ref[...] = dyb
        da_ref[...] = _dot_nt(dyb, w_ref[...])

        @pl.when(i == 0)
        def _():
            acc_ref[...] = jnp.zeros_like(acc_ref)

        part = coef * jnp.sum(dv * f_ref[...].astype(F32), axis=0, keepdims=True)
        acc_ref[pl.ds(_stream_of(i, n_x), 1)] += part[None]

    return pl.pallas_call(
        body, name=name, grid=(n_i,),
        in_specs=[pl.BlockSpec((tm, D_MODEL), lambda i: (i, 0)),
                  pl.BlockSpec((tm, D_MODEL), lambda i: (i, 0)),
                  pl.BlockSpec((1, N_MOD, D_MODEL), lambda i: (_stream_of(i, n_x), 0, 0)),
                  pl.BlockSpec((n_out, D_MODEL), lambda i: (0, 0))],
        out_specs=[pl.BlockSpec((tm, D_MODEL), lambda i: (i, 0)),
                   pl.BlockSpec((tm, n_out), lambda i: (i, 0)),
                   pl.BlockSpec((2, 1, D_MODEL), lambda i: (0, 0, 0))],
        out_shape=[jax.ShapeDtypeStruct((rows, D_MODEL), BF16),
                   jax.ShapeDtypeStruct((rows, n_out), F32),
                   jax.ShapeDtypeStruct((2, 1, D_MODEL), F32)],
        compiler_params=_cp("arbitrary"),
    )(dout, f, mods, w)


def _ffn_bwd_dz(dout, f, mods, gate_idx, coef, n_x, wo2, au, name):
    rows = dout.shape[0]
    h = wo2.shape[1]
    tm = ROW_TILE
    n_i = rows // tm

    def body(d_ref, f_ref, m_ref, wo_ref, au_ref, dy_ref, dz_ref, acc_ref):
        j, i = pl.program_id(0), pl.program_id(1)
        dv = d_ref[...]
        gate = m_ref[0, gate_idx:gate_idx + 1, :]
        dyb = (dv * (coef * gate)).astype(BF16)

        @pl.when((j == 0) & (i == 0))
        def _():
            acc_ref[...] = jnp.zeros_like(acc_ref)

        @pl.when(j == 0)
        def _():
            dy_ref[...] = dyb
            part = coef * jnp.sum(dv * f_ref[...].astype(F32), axis=0, keepdims=True)
            acc_ref[pl.ds(_stream_of(i, n_x), 1)] += part[None]

        for c0, cw in _col_chunks(h):
            cols = slice(c0, c0 + cw)
            ds = _dot_nt(dyb, wo_ref[0, cols, :])
            dz_ref[0, :, cols] = (ds * au_ref[0, :, cols].astype(F32)).astype(BF16)
            dz_ref[1, :, cols] = (ds * au_ref[1, :, cols].astype(F32)).astype(BF16)

    return pl.pallas_call(
        body, name=name, grid=(2, n_i),
        in_specs=[pl.BlockSpec((tm, D_MODEL), lambda j, i: (i, 0)),
                  pl.BlockSpec((tm, D_MODEL), lambda j, i: (jnp.where(j == 0, i, n_i - 1), 0)),
                  pl.BlockSpec((1, N_MOD, D_MODEL), lambda j, i: (_stream_of(i, n_x), 0, 0)),
                  pl.BlockSpec((1, h, D_MODEL), lambda j, i: (j, 0, 0)),
                  pl.BlockSpec((2, tm, h), lambda j, i: (0, i, j))],
        out_specs=[pl.BlockSpec((tm, D_MODEL), lambda j, i: (jnp.where(j == 0, i, n_i - 1), 0)),
                   pl.BlockSpec((2, tm, h), lambda j, i: (0, i, j)),
                   pl.BlockSpec((2, 1, D_MODEL), lambda j, i: (0, 0, 0))],
        out_shape=[jax.ShapeDtypeStruct((rows, D_MODEL), BF16),
                   jax.ShapeDtypeStruct((2, rows, 2 * h), BF16),
                   jax.ShapeDtypeStruct((2, 1, D_MODEL), F32)],
        compiler_params=_cp("arbitrary", "arbitrary"),
    )(dout, f, mods, wo2, au)


def _token_tile(rows):
    for tk in (2048, 1536, 1024):
        if rows % tk == 0:
            return tk
    return ROW_TILE


def _matmul_tn(a, b, a_spec, b_spec, out_shape, out_spec, grid, name):
    nd_a = len(a_spec.block_shape)
    nd_b = len(b_spec.block_shape)
    nd_o = len(out_spec.block_shape)
    k_axis = len(grid) - 1
    n_k = grid[k_axis]

    def body(a_ref, b_ref, o_ref, acc_ref):
        av = a_ref[(0,) * (nd_a - 2)]
        bv = b_ref[(0,) * (nd_b - 2)]
        part = _dot_tn(av, bv)
        k = pl.program_id(k_axis)

        @pl.when(k == 0)
        def _():
            acc_ref[...] = part

        @pl.when(k > 0)
        def _():
            acc_ref[...] += part

        @pl.when(k == n_k - 1)
        def _():
            o_ref[(0,) * (nd_o - 2)] = acc_ref[...].astype(BF16)

    return pl.pallas_call(
        body, name=name, grid=grid, in_specs=[a_spec, b_spec], out_specs=out_spec,
        out_shape=jax.ShapeDtypeStruct(out_shape, BF16),
        scratch_shapes=[pltpu.VMEM(tuple(out_spec.block_shape[-2:]), F32)],
        compiler_params=_cp(*(("arbitrary",) * len(grid))),
    )(a, b)


def _bwd_dx(pairs, x, dres, dres_tiles, g3, mods, j, n_x, name, out_tiles=None):
    rows = x.shape[0]
    tm = ROW_TILE
    n_i = rows // tm
    n_o = n_i if out_tiles is None else out_tiles
    n_p = len(pairs)
    nds = [(len(p[1].block_shape), len(p[3].block_shape)) for p in pairs]

    def body(*refs):
        dz_refs = refs[0:2 * n_p:2]
        w_refs = refs[1:2 * n_p:2]
        x_ref, dres_ref, g_ref, m_ref, dx_ref, acc_ref = refs[2 * n_p:]
        i = pl.program_id(0)
        dzs = [dz_refs[p][(0,) * (nds[p][0] - 2)] for p in range(n_p)]
        pieces = []
        for c0, cw in _col_chunks(D_MODEL):
            acc = None
            for p in range(n_p):
                lead = (0,) * (nds[p][1] - 2)
                part = _dot_nt(dzs[p], w_refs[p][lead + (slice(c0, c0 + cw), slice(None))])
                acc = part if acc is None else acc + part
            pieces.append(acc)
        dh = jnp.concatenate(pieces, axis=1)
        g = g_ref[j:j + 1, :]
        scale = m_ref[0, 3 * j + 1:3 * j + 2, :]
        dx = _rms_mod_bwd_tail(dh, x_ref[...], g, scale, _stream_of(i, n_x), acc_ref, i == 0)
        dres_v = jnp.where(i < dres_tiles, dres_ref[...], 0.0)

        @pl.when(i < n_o)
        def _():
            dx_ref[...] = dres_v + dx

    in_specs, args = [], []
    for dz, dz_spec, w, w_spec in pairs:
        in_specs += [dz_spec, w_spec]
        args += [dz, w]
    in_specs += [pl.BlockSpec((tm, D_MODEL), lambda i: (i, 0)),
                 pl.BlockSpec((tm, D_MODEL), lambda i: (jnp.minimum(i, dres_tiles - 1), 0)),
                 pl.BlockSpec((3, D_MODEL), lambda i: (0, 0)),
                 pl.BlockSpec((1, N_MOD, D_MODEL), lambda i: (_stream_of(i, n_x), 0, 0))]
    args += [x, dres, g3, mods]
    return pl.pallas_call(
        body, name=name, grid=(n_i,), in_specs=in_specs,
        out_specs=[pl.BlockSpec((tm, D_MODEL), lambda i: (jnp.minimum(i, n_o - 1), 0)),
                   pl.BlockSpec((2, 3, D_MODEL), lambda i: (0, 0, 0))],
        out_shape=[jax.ShapeDtypeStruct((n_o * tm, D_MODEL), F32),
                   jax.ShapeDtypeStruct((2, 3, D_MODEL), F32)],
        compiler_params=_cp("arbitrary"),
    )(*args)


def _ffn_forward(x, g3, mods, j, w4_in, w4_out_of, n_x, name):
    rows = x.shape[0]
    hn, au, s = _ffn_up(x, g3, mods, j, n_x, w4_in, name + "_up")
    w4_out, dep = w4_out_of(s)
    if dep is not None:
        mods = mods + dep[0:1, 0:1]
    wo = w4_out.reshape(D_FF, D_MODEL)
    out, f = _matmul_resid(s, wo, x, mods, 3 * j + 2, 0.5, n_x, rows, name + "_down")
    return out, (x, hn, au, s, f), w4_out


def _ffn_backward(dout, saved, g3, mods, j, w4_in, w4_out, n_x, send, name, out_tiles=None):
    x, hn, au, s, f = saved
    rows = x.shape[0]
    tm = ROW_TILE
    n_i = rows // tm
    h = w4_in.shape[2]
    wo2 = w4_out.reshape(2, h, D_MODEL)
    dy, dz, dgate = _ffn_bwd_dz(dout, f, mods, 3 * j + 2, 0.5, n_x, wo2, au, name + "_dz")
    tk = _token_tile(rows)
    n_k = rows // tk
    d_wi = _matmul_tn(
        hn, dz, pl.BlockSpec((tk, D_MODEL), lambda q, k: (k, 0)),
        pl.BlockSpec((1, tk, h), lambda q, k: (q // 2, k, q % 2)),
        (4, D_MODEL, h), pl.BlockSpec((1, D_MODEL, h), lambda q, k: (q, 0, 0)), (4, n_k), name + "_dwi")
    d_wo = _matmul_tn(
        s, dy, pl.BlockSpec((tk, h), lambda n, k: (k, n)), pl.BlockSpec((tk, D_MODEL), lambda n, k: (k, 0)),
        (D_FF, D_MODEL), pl.BlockSpec((h, D_MODEL), lambda n, k: (n, 0)), (2, n_k), name + "_dwo")
    mods = mods + send({"wi": d_wi, "wo": d_wo.reshape(w4_out.shape)})
    pairs = [(dz, pl.BlockSpec((1, tm, h), functools.partial(lambda q, i: (q // 2, i, q % 2), q)),
              w4_in, pl.BlockSpec((1, D_MODEL, h), functools.partial(lambda q, i: (q, 0, 0), q)))
             for q in range(4)]
    dx, stats = _bwd_dx(pairs, x, dout, n_i, g3, mods, j, n_x, name + "_dx", out_tiles)
    return dx, stats, dgate


def _rope_tables(t_len, rows):
    n = A_HEAD_DIM // 4
    freqs = ROPE_BASE ** (-jnp.arange(n, dtype=F32) / n)
    t = jnp.arange(t_len)
    ang_r = (t // GRID_W).astype(F32)[:, None] * freqs
    ang_c = (t % GRID_W).astype(F32)[:, None] * freqs
    cos = jnp.concatenate([jnp.cos(ang_r), jnp.cos(ang_r), jnp.cos(ang_c), jnp.cos(ang_c)], axis=1)
    sin = jnp.concatenate([-jnp.sin(ang_r), jnp.sin(ang_r), -jnp.sin(ang_c), jnp.sin(ang_c)], axis=1)
    cos = jnp.concatenate([cos, jnp.ones((rows - t_len, A_HEAD_DIM), F32)], axis=0)
    sin = jnp.concatenate([sin, jnp.zeros((rows - t_len, A_HEAD_DIM), F32)], axis=0)
    return jnp.concatenate([cos, cos, sin, sin], axis=1)


def _swap16(x):
    n = x.shape[1]
    lane = lax.broadcasted_iota(jnp.int32, x.shape, 1)
    first = jnp.bitwise_and(lane, 16) == 0
    return jnp.where(first, pltpu.roll(x, n - 16, 1), pltpu.roll(x, 16, 1))


def _log_sigmoid(x):
    return jnp.minimum(x, 0.0) - jnp.log(1.0 + jnp.exp(-jnp.abs(x)))


def _proj_fwd(x, g3, mods, n_x, wcat, wg2, bias2, cs, name):
    rows = x.shape[0]
    tm = ROW_TILE

    def body(x_ref, g_ref, m_ref, w_ref, wg_ref, b_ref, cs_ref, h_ref, zc_ref, la_ref):
        xv = x_ref[...]
        r = lax.rsqrt(jnp.mean(xv * xv, axis=-1, keepdims=True) + RMS_EPS)
        hv = (((xv * r) * g_ref[1:2, :]) * (1.0 + m_ref[0, 4:5, :]) + m_ref[0, 3:4, :]).astype(BF16)
        h_ref[...] = hv
        z = _dot(hv, w_ref[...])
        cos = cs_ref[:, 0:128]
        sin = cs_ref[:, 128:256]
        cosq = jnp.concatenate([cos] * 4, axis=1)
        sinq = jnp.concatenate([sin] * 4, axis=1)
        q = z[:, ZC_Q:ZC_QK]
        zc_ref[:, ZC_Q:ZC_QK] = q * cosq + _swap16(q) * sinq
        zc_ref[:, ZC_QK:ZC_KV] = z[:, ZC_QK:ZC_KV]
        kk = z[:, ZC_KV:ZC_KV + 128]
        zc_ref[:, ZC_KV:ZC_KV + 128] = kk * cos + _swap16(kk) * sin
        zc_ref[:, ZC_KV + 128:ZC_W] = z[:, ZC_KV + 128:ZC_W]
        zg = z[:, ZC_G:ZC_W]
        pre = _dot(zg.astype(BF16), wg_ref[...]) + b_ref[...]
        la_ref[...] = _log_sigmoid(pre) / B_GATE_NORM

    return pl.pallas_call(
        body, name=name, grid=(rows // tm,),
        in_specs=[pl.BlockSpec((tm, D_MODEL), lambda i: (i, 0)),
                  pl.BlockSpec((3, D_MODEL), lambda i: (0, 0)),
                  pl.BlockSpec((1, N_MOD, D_MODEL), lambda i: (_stream_of(i, n_x), 0, 0)),
                  pl.BlockSpec((D_MODEL, ZC_W), lambda i: (0, 0)),
                  pl.BlockSpec((128, 512), lambda i: (0, 0)),
                  pl.BlockSpec((1, 512), lambda i: (0, 0)),
                  pl.BlockSpec((tm, 256), lambda i: (i, 0))],
        out_specs=[pl.BlockSpec((tm, D_MODEL), lambda i: (i, 0)),
                   pl.BlockSpec((tm, ZC_W), lambda i: (i, 0)),
                   pl.BlockSpec((tm, 512), lambda i: (i, 0))],
        out_shape=[jax.ShapeDtypeStruct((rows, D_MODEL), BF16),
                   jax.ShapeDtypeStruct((rows, ZC_W), F32),
                   jax.ShapeDtypeStruct((rows, 512), F32)],
        compiler_params=_cp("parallel"),
    )(x, g3, mods, wcat, wg2, bias2, cs)


_QB = WINDOW


def _attn_specs(t_len, l_ctx):
    nb = t_len // _QB
    kvb = ZC_KV // 256
    return [pl.BlockSpec(memory_space=pltpu.SMEM),
            pl.BlockSpec((_QB, 512), lambda n: (n, 0)),
            pl.BlockSpec((_QB, 256), lambda n: (jnp.maximum(n - 1, 0), kvb)),
            pl.BlockSpec((_QB, 256), lambda n: (n, kvb)),
            pl.BlockSpec((_QB, 256), lambda n: (n + 1, kvb)),
            pl.BlockSpec((l_ctx, 256), lambda n: (t_len // l_ctx, kvb))], nb


_HEAD_PAIRS = ((0, 1), (2, 3))


def _attn_keys(kp, kc, kn, kx, g):
    hd = A_HEAD_DIM
    ks = slice(g * hd, (g + 1) * hd)
    vs = slice(128 + g * hd, 128 + (g + 1) * hd)
    kb = jnp.concatenate([kp[:, ks], kc[:, ks], kn[:, ks]], axis=0).astype(BF16)
    vb = jnp.concatenate([kp[:, vs], kc[:, vs], kn[:, vs]], axis=0).astype(BF16)
    return kb, vb, kx[:, ks].astype(BF16), kx[:, vs].astype(BF16)


def _attn_probs(n, t_len, sink_ref, qv, kb, kxb, g, rs):
    hd = A_HEAD_DIM
    qg = jnp.concatenate([qv[:, (4 * g + r) * hd:(4 * g + r + 1) * hd] for r in rs], axis=0).astype(BF16)
    qi = lax.broadcasted_iota(jnp.int32, (_QB, 3 * _QB), 0)
    kj = lax.broadcasted_iota(jnp.int32, (_QB, 3 * _QB), 1)
    kpos = n * _QB - _QB + kj
    valid = (kpos >= 0) & (kpos < t_len) & (jnp.abs(kj - _QB - qi) <= WINDOW)
    valid = jnp.concatenate([valid] * len(rs), axis=0)
    scale = hd ** -0.5
    s = jnp.where(valid, _dot_nt(qg, kb) * scale, -jnp.inf)
    sc = _dot_nt(qg, kxb) * scale
    sk = jnp.concatenate([jnp.full((_QB, 1), sink_ref[4 * g + r], F32) for r in rs], axis=0)
    m = jnp.maximum(jnp.maximum(jnp.max(s, axis=-1, keepdims=True), jnp.max(sc, axis=-1, keepdims=True)), sk)
    p = jnp.exp(s - m)
    pc = jnp.exp(sc - m)
    ps = jnp.exp(sk - m)
    inv = 1.0 / (jnp.sum(p, axis=-1, keepdims=True) + jnp.sum(pc, axis=-1, keepdims=True) + ps)
    return p, pc, ps, inv, qg


def _attn_fwd(zc, sink, t_len, l_ctx, name):
    in_specs, nb = _attn_specs(t_len, l_ctx)

    def body(sink_ref, q_ref, kp_ref, kc_ref, kn_ref, kx_ref, o_ref):
        n = pl.program_id(0)
        qv = q_ref[...]
        outs = []
        for g in range(A_KV_HEADS):
            kb, vb, kxb, vxb = _attn_keys(kp_ref[...], kc_ref[...], kn_ref[...], kx_ref[...], g)
            for rs in _HEAD_PAIRS:
                p, pc, _, inv, _ = _attn_probs(n, t_len, sink_ref, qv, kb, kxb, g, rs)
                o = (_dot(p.astype(BF16), vb) + _dot(pc.astype(BF16), vxb)) * inv
                outs += [o[i * _QB:(i + 1) * _QB] for i in range(len(rs))]
        o_ref[...] = jnp.concatenate(outs, axis=1)

    return pl.pallas_call(
        body, name=name, grid=(nb,), in_specs=in_specs,
        out_specs=pl.BlockSpec((_QB, 512), lambda n: (n, 0)),
        out_shape=jax.ShapeDtypeStruct((t_len, 512), F32),
        compiler_params=_cp("parallel"),
    )(sink, zc, zc, zc, zc, zc)


def _attn_bwd(zc, sink, o, dcat, t_len, l_ctx, name):
    rows = zc.shape[0]
    in_specs, nb = _attn_specs(t_len, l_ctx)
    in_specs = in_specs + [pl.BlockSpec((_QB, 512), lambda n: (n, 0)), pl.BlockSpec((_QB, 512), lambda n: (n, 0))]
    hd = A_HEAD_DIM
    scale = hd ** -0.5

    def body(sink_ref, q_ref, kp_ref, kc_ref, kn_ref, kx_ref, o_ref, do_ref, dq_ref, dkv_ref, dsink_ref):
        n = pl.program_id(0)

        @pl.when(n == 0)
        def _():
            dkv_ref[...] = jnp.zeros_like(dkv_ref)
            dsink_ref[...] = jnp.zeros_like(dsink_ref)

        qv = q_ref[...]
        ov = o_ref[...]
        dov = do_ref[...]
        dqs, dkbs, dvbs, dkxs, dvxs, dsinks = [], [], [], [], [], []
        for g in range(A_KV_HEADS):
            kb, vb, kxb, vxb = _attn_keys(kp_ref[...], kc_ref[...], kn_ref[...], kx_ref[...], g)
            parts = []
            for rs in _HEAD_PAIRS:
                p, pc, ps, inv, qg = _attn_probs(n, t_len, sink_ref, qv, kb, kxb, g, rs)
                og = jnp.concatenate([ov[:, (4 * g + r) * hd:(4 * g + r + 1) * hd] for r in rs], axis=0)
                dog = jnp.concatenate([dov[:, (4 * g + r) * hd:(4 * g + r + 1) * hd] for r in rs], axis=0)
                delta = jnp.sum(og * dog, axis=-1, keepdims=True)
                dogb = dog.astype(BF16)
                pn = p * inv
                pcn = pc * inv
                ds = (pn * (_dot_nt(dogb, vb) - delta) * scale).astype(BF16)
                dsc = (pcn * (_dot_nt(dogb, vxb) - delta) * scale).astype(BF16)
                dsk = (ps * inv) * (0.0 - delta)
                dqg = _dot(ds, kb) + _dot(dsc, kxb)
                dqs += [dqg[i * _QB:(i + 1) * _QB] for i in range(len(rs))]
                parts.append((_dot_tn(ds, qg), _dot_tn(pn.astype(BF16), dogb),
                              _dot_tn(dsc, qg), _dot_tn(pcn.astype(BF16), dogb)))
                for i in range(len(rs)):
                    tot = jnp.sum(dsk[i * _QB:(i + 1) * _QB], axis=0, keepdims=True)
                    dsinks.append(jnp.broadcast_to(tot, (1, 128)))
            dkbs.append(parts[0][0] + parts[1][0])
            dvbs.append(parts[0][1] + parts[1][1])
            dkxs.append(parts[0][2] + parts[1][2])
            dvxs.append(parts[0][3] + parts[1][3])
        dsink_ref[...] += jnp.concatenate(dsinks, axis=0)
        dq_ref[...] = jnp.concatenate(dqs, axis=1)
        band = jnp.concatenate(dkbs + dvbs, axis=1)
        ctxc = jnp.concatenate(dkxs + dvxs, axis=1)
        r_prev = pl.multiple_of(jnp.maximum(n - 1, 0) * _QB, _QB)
        r_cur = pl.multiple_of(n * _QB, _QB)
        r_next = pl.multiple_of((n + 1) * _QB, _QB)
        dkv_ref[pl.ds(r_prev, _QB), :] += band[0:_QB]
        dkv_ref[pl.ds(r_cur, _QB), :] += band[_QB:2 * _QB]
        dkv_ref[pl.ds(r_next, _QB), :] += band[2 * _QB:3 * _QB]
        dkv_ref[t_len:t_len + l_ctx, :] += ctxc

    return pl.pallas_call(
        body, name=name, grid=(nb,), in_specs=in_specs,
        out_specs=[pl.BlockSpec((_QB, 512), lambda n: (n, 0)),
                   pl.BlockSpec((rows, 256), lambda n: (0, 0)),
                   pl.BlockSpec((8, 128), lambda n: (0, 0))],
        out_shape=[jax.ShapeDtypeStruct((t_len, 512), F32),
                   jax.ShapeDtypeStruct((rows, 256), F32),
                   jax.ShapeDtypeStruct((8, 128), F32)],
        compiler_params=_cp("arbitrary"),
    )(sink, zc, zc, zc, zc, zc, o, dcat)


_GC = B_CHUNK


def _split_bf16(a):
    hi = a.astype(BF16)
    return hi, (a - hi.astype(F32)).astype(BF16)


def _gla_chunk_terms(qk, la, reverse):
    q = qk[:, 0:256]
    k = qk[:, 256:512]
    off = 256 if reverse else 0
    lad = la[:, off:off + 256]
    ii = lax.broadcasted_iota(jnp.int32, (_GC, _GC), 0)
    jj = lax.broadcasted_iota(jnp.int32, (_GC, _GC), 1)
    mask = (jj >= ii) if reverse else (jj <= ii)
    tri = jnp.where(mask, 1.0, 0.0).astype(BF16)
    la_hi, la_lo = _split_bf16(lad)
    g = _dot(tri, la_hi) + _dot(tri, la_lo)
    gl = jnp.sum(lad, axis=0, keepdims=True)
    eg = jnp.exp(g)
    eng = jnp.exp(-g)
    eend = jnp.exp(gl - g)
    sc = B_DK ** -0.5
    qt = q * (sc * eg)
    kt = k * eng
    ke = k * eend
    return mask, tri, gl, eg, eng, eend, qt, kt, ke


def _same_head(rows, cols, row_shift, col_shift):
    r = jnp.right_shift(lax.broadcasted_iota(jnp.int32, (rows, cols), 0), row_shift)
    c = jnp.right_shift(lax.broadcasted_iota(jnp.int32, (rows, cols), 1), col_shift)
    return r == c


def _block_diag_rows(x, col_shift):
    tiled = jnp.concatenate([x] * B_HEADS, axis=0)
    return jnp.where(_same_head(tiled.shape[0], tiled.shape[1], 6, col_shift), tiled, jnp.zeros_like(tiled))


def _fold_heads(x):
    c = x.shape[0] // B_HEADS
    return (x[0:c] + x[c:2 * c]) + (x[2 * c:3 * c] + x[3 * c:4 * c])


def _chunk_mask4(reverse):
    ii = lax.broadcasted_iota(jnp.int32, (_GC, B_HEADS * _GC), 0)
    jj = jnp.bitwise_and(lax.broadcasted_iota(jnp.int32, (_GC, B_HEADS * _GC), 1), _GC - 1)
    return (jj >= ii) if reverse else (jj <= ii)


_ST_SHAPE = (B_HEADS * B_DV, B_HEADS * B_DK)


def _state_blocks(t):
    return [t[hh * B_DV:(hh + 1) * B_DV, hh * B_DK:(hh + 1) * B_DK] for hh in range(B_HEADS)]


def _state_from_blocks(blocks):
    full = jnp.concatenate([jnp.concatenate([b] * B_HEADS, axis=1) for b in blocks], axis=0)
    return jnp.where(_same_head(_ST_SHAPE[0], _ST_SHAPE[1], 7, 6), full, 0.0)


def _gla_fwd(zc, la, dep, t_len, l_ctx, name):
    rows = zc.shape[0]
    n_x = t_len // _GC
    n_c = n_x + l_ctx // _GC
    qkb, vb = ZC_QK // 512, ZC_V // 512

    def ch_f(c):
        return lax.rem(c + n_x, n_c)

    def ch_r(c):
        return n_c - 1 - c

    def body(qkf_ref, vf_ref, laf_ref, qkr_ref, vr_ref, lar_ref, dep_ref, of_ref, or_ref, spf_ref, spr_ref, stf, strv):
        del dep_ref
        c = pl.program_id(0)

        @pl.when(c == 0)
        def _():
            stf[...] = jnp.zeros_like(stf)
            strv[...] = jnp.zeros_like(strv)

        results = []
        for qk_ref, v_ref, la_ref, st, reverse in ((qkf_ref, vf_ref, laf_ref, stf, False),
                                                   (qkr_ref, vr_ref, lar_ref, strv, True)):
            mask, _, gl, _, _, _, qt, kt, ke = _gla_chunk_terms(qk_ref[...], la_ref[...], reverse)
            vbf = v_ref[...].astype(BF16)
            qtb, keb = qt.astype(BF16), ke.astype(BF16)
            kbd = _block_diag_rows(kt.astype(BF16), 6)
            vbd = _block_diag_rows(vbf, 7)
            mask4 = _chunk_mask4(reverse)
            t_prev = st[...]
            att = jnp.where(mask4, _dot_nt(qtb, kbd), 0.0).astype(BF16)
            o_all = _dot(att, vbd) + _dot_nt(qtb, t_prev.astype(BF16))
            t_new = t_prev * jnp.exp(gl) + jnp.where(_same_head(_ST_SHAPE[0], _ST_SHAPE[1], 7, 6),
                                                     _dot_tn(vbf, keb), 0.0)
            results.append((o_all, t_prev, t_new))
        for (o_all, t_prev, t_new), o_ref, sp_ref, st in zip(results, (of_ref, or_ref), (spf_ref, spr_ref), (stf, strv)):
            o_ref[...] = o_all
            for hh, blk in enumerate(_state_blocks(t_prev)):
                sp_ref[0, hh] = blk
            st[...] = t_new

    st_shape = (B_HEADS, B_DV, B_DK)
    return pl.pallas_call(
        body, name=name, grid=(n_c,),
        in_specs=[pl.BlockSpec((_GC, 512), lambda c: (ch_f(c), qkb)),
                  pl.BlockSpec((_GC, 512), lambda c: (ch_f(c), vb)),
                  pl.BlockSpec((_GC, 512), lambda c: (ch_f(c), 0)),
                  pl.BlockSpec((_GC, 512), lambda c: (ch_r(c), qkb)),
                  pl.BlockSpec((_GC, 512), lambda c: (ch_r(c), vb)),
                  pl.BlockSpec((_GC, 512), lambda c: (ch_r(c), 0)),
                  pl.BlockSpec((8, 128), lambda c: (0, 0))],
        out_specs=[pl.BlockSpec((_GC, 512), lambda c: (ch_f(c), 0)),
                   pl.BlockSpec((_GC, 512), lambda c: (ch_r(c), 0)),
                   pl.BlockSpec((1,) + st_shape, lambda c: (c, 0, 0, 0)),
                   pl.BlockSpec((1,) + st_shape, lambda c: (c, 0, 0, 0))],
        out_shape=[jax.ShapeDtypeStruct((rows, 512), F32), jax.ShapeDtypeStruct((rows, 512), F32),
                   jax.ShapeDtypeStruct((n_c,) + st_shape, F32), jax.ShapeDtypeStruct((n_c,) + st_shape, F32)],
        scratch_shapes=[pltpu.VMEM(_ST_SHAPE, F32), pltpu.VMEM(_ST_SHAPE, F32)],
        compiler_params=_cp("arbitrary"),
    )(zc, zc, la, zc, zc, la, dep)


def _gla_bwd(zc, la, spf, spr, dosum, t_len, l_ctx, name):
    rows = zc.shape[0]
    n_x = t_len // _GC
    n_c = n_x + l_ctx // _GC
    n_all = rows // _GC
    qkb, vb = ZC_QK // 512, ZC_V // 512

    def scan_of(c):
        return jnp.maximum(n_c - 1 - c, 0)

    def ch_f(c):
        return jnp.where(c < n_c, lax.rem(scan_of(c) + n_x, n_c), c)

    def ch_r(c):
        return c

    def do_of(ch):
        return jnp.minimum(ch, n_x - 1)

    def body(qkf_ref, vf_ref, laf_ref, spf_ref, dof_ref, qkr_ref, vr_ref, lar_ref, spr_ref, dor_ref,
             dqkf_ref, dvf_ref, dlaf_ref, dqkr_ref, dvr_ref, dlar_ref, dsf, dsr):
        c = pl.program_id(0)

        @pl.when(c == 0)
        def _():
            dsf[...] = jnp.zeros_like(dsf)
            dsr[...] = jnp.zeros_like(dsr)

        @pl.when(c >= n_c)
        def _():
            for r in (dqkf_ref, dvf_ref, dlaf_ref, dqkr_ref, dvr_ref, dlar_ref):
                r[...] = jnp.zeros_like(r)

        @pl.when(c < n_c)
        def _():
            sc = B_DK ** -0.5
            results = []
            for qk_ref, v_ref, la_ref, sp_ref, do_ref, dst, reverse, ch in (
                    (qkf_ref, vf_ref, laf_ref, spf_ref, dof_ref, dsf, False, ch_f(c)),
                    (qkr_ref, vr_ref, lar_ref, spr_ref, dor_ref, dsr, True, ch_r(c))):
                mask, tri, gl, eg, eng, eend, qt, kt, ke = _gla_chunk_terms(qk_ref[...], la_ref[...], reverse)
                vbf = v_ref[...].astype(BF16)
                dob = jnp.where(ch < n_x, do_ref[...], 0.0).astype(BF16)
                qtb, keb = qt.astype(BF16), ke.astype(BF16)
                kbd = _block_diag_rows(kt.astype(BF16), 6)
                vbd = _block_diag_rows(vbf, 7)
                mask4 = _chunk_mask4(reverse)
                egl = jnp.exp(gl)
                t_prev = _state_from_blocks([sp_ref[0, hh] for hh in range(B_HEADS)])
                dt_new = dst[...]
                tpb, dtb = t_prev.astype(BF16), dt_new.astype(BF16)
                att = jnp.where(mask4, _dot_nt(qtb, kbd), 0.0).astype(BF16)
                datt = jnp.where(mask4, _dot_nt(dob, vbd), 0.0).astype(BF16)
                dqt = _dot(datt, kbd) + _dot(dob, tpb)
                dkt = _fold_heads(jnp.where(_same_head(256, 256, 6, 6), _dot_tn(datt, qtb), 0.0))
                dv = _fold_heads(jnp.where(_same_head(256, 512, 6, 7), _dot_tn(att, dob), 0.0)) + _dot_nt(keb, dtb)
                dke = _dot(vbf, dtb)
                dt_prev = dt_new * egl + jnp.where(_same_head(_ST_SHAPE[0], _ST_SHAPE[1], 7, 6),
                                                   _dot_tn(dob, qtb), 0.0)
                dgl = (jnp.sum(dke * ke, axis=0, keepdims=True)
                       + jnp.sum(dt_new * t_prev, axis=0, keepdims=True) * egl)
                dg_hi, dg_lo = _split_bf16(dqt * qt - dkt * kt - dke * ke)
                dla = _dot_tn(tri, dg_hi) + _dot_tn(tri, dg_lo) + dgl
                dqk = jnp.concatenate([dqt * (sc * eg), dkt * eng + dke * eend], axis=1)
                results.append((dqk, dv, dla, dt_prev))
            for (dqk, dv, dla, dt_prev), dqk_ref, dv_ref, dla_ref, dst in zip(
                    results, (dqkf_ref, dqkr_ref), (dvf_ref, dvr_ref), (dlaf_ref, dlar_ref), (dsf, dsr)):
                dqk_ref[...] = dqk
                dv_ref[...] = dv
                dla_ref[...] = dla
                dst[...] = dt_prev

    st_shape = (B_HEADS, B_DV, B_DK)

    def side(chf):
        return [pl.BlockSpec((_GC, 512), lambda c: (chf(c), qkb)),
                pl.BlockSpec((_GC, 512), lambda c: (chf(c), vb)),
                pl.BlockSpec((_GC, 512), lambda c: (chf(c), 0)),
                pl.BlockSpec((1,) + st_shape, lambda c: (scan_of(c), 0, 0, 0)),
                pl.BlockSpec((_GC, 512), lambda c: (do_of(chf(c)), 0))]

    def out_side(chf):
        return [pl.BlockSpec((_GC, 512), lambda c: (chf(c), 0)),
                pl.BlockSpec((_GC, 512), lambda c: (chf(c), 0)),
                pl.BlockSpec((_GC, 256), lambda c: (chf(c), 0))]

    shp = [jax.ShapeDtypeStruct((rows, 512), F32), jax.ShapeDtypeStruct((rows, 512), F32),
           jax.ShapeDtypeStruct((rows, 256), F32)]
    return pl.pallas_call(
        body, name=name, grid=(n_all,),
        in_specs=side(ch_f) + side(ch_r),
        out_specs=out_side(ch_f) + out_side(ch_r),
        out_shape=shp + shp,
        scratch_shapes=[pltpu.VMEM(_ST_SHAPE, F32), pltpu.VMEM(_ST_SHAPE, F32)],
        compiler_params=_cp("arbitrary"),
    )(zc, zc, la, spf, dosum, zc, zc, la, spr, dosum)


def _gla_out_fwd(o_a, o_f, o_r, zc, gla_g, t_len, name):
    tm = ROW_TILE
    rb = ZC_R // 512

    def body(oa_ref, of_ref, or_ref, r_ref, g_ref, cat_ref):
        osum = of_ref[...] + or_ref[...]
        g = g_ref[...]
        pieces = []
        for hh in range(B_HEADS):
            oh = osum[:, hh * B_DV:(hh + 1) * B_DV]
            rs = lax.rsqrt(jnp.mean(oh * oh, axis=-1, keepdims=True) + RMS_EPS)
            pieces.append((oh * rs) * g)
        r = r_ref[...]
        cat_ref[:, 0:512] = oa_ref[...].astype(BF16)
        cat_ref[:, 512:1024] = (jnp.concatenate(pieces, axis=1) * (r * _sigmoid(r))).astype(BF16)

    return pl.pallas_call(
        body, name=name, grid=(t_len // tm,),
        in_specs=[pl.BlockSpec((tm, 512), lambda i: (i, 0)),
                  pl.BlockSpec((tm, 512), lambda i: (i, 0)),
                  pl.BlockSpec((tm, 512), lambda i: (i, 0)),
                  pl.BlockSpec((tm, 512), lambda i: (i, rb)),
                  pl.BlockSpec((1, B_DV), lambda i: (0, 0))],
        out_specs=pl.BlockSpec((tm, D_MODEL), lambda i: (i, 0)),
        out_shape=jax.ShapeDtypeStruct((t_len, D_MODEL), BF16),
        compiler_params=_cp("parallel"),
    )(o_a, o_f, o_r, zc, gla_g)


def _gla_out_bwd(dcat, o_f, o_r, zc, gla_g, t_len, name):
    tm = ROW_TILE
    rb = ZC_R // 512

    def body(d_ref, of_ref, or_ref, r_ref, g_ref, dos_ref, dr_ref, dg_ref):
        i = pl.program_id(0)
        osum = of_ref[...] + or_ref[...]
        g = g_ref[...]
        r = r_ref[...]
        dgo = d_ref[...]
        sg = _sigmoid(r)
        dnrmg = dgo * (r * sg)
        nrms, dos = [], []
        dg_acc = jnp.zeros((1, B_DV), F32)
        for hh in range(B_HEADS):
            oh = osum[:, hh * B_DV:(hh + 1) * B_DV]
            rs = lax.rsqrt(jnp.mean(oh * oh, axis=-1, keepdims=True) + RMS_EPS)
            nrm = oh * rs
            dn = dnrmg[:, hh * B_DV:(hh + 1) * B_DV]
            dg_acc = dg_acc + jnp.sum(dn * nrm, axis=0, keepdims=True)
            dnn = dn * g
            dos.append(rs * (dnn - nrm * jnp.mean(dnn * nrm, axis=-1, keepdims=True)))
            nrms.append(nrm * g)
        dos_ref[...] = jnp.concatenate(dos, axis=1)
        dr_ref[...] = dgo * jnp.concatenate(nrms, axis=1) * (sg * (1.0 + r * (1.0 - sg)))

        @pl.when(i == 0)
        def _():
            dg_ref[...] = jnp.zeros_like(dg_ref)

        dg_ref[...] += dg_acc

    return pl.pallas_call(
        body, name=name, grid=(t_len // tm,),
        in_specs=[pl.BlockSpec((tm, 512), lambda i: (i, 1)),
                  pl.BlockSpec((tm, 512), lambda i: (i, 0)),
                  pl.BlockSpec((tm, 512), lambda i: (i, 0)),
                  pl.BlockSpec((tm, 512), lambda i: (i, rb)),
                  pl.BlockSpec((1, B_DV), lambda i: (0, 0))],
        out_specs=[pl.BlockSpec((tm, 512), lambda i: (i, 0)),
                   pl.BlockSpec((tm, 512), lambda i: (i, 0)),
                   pl.BlockSpec((1, B_DV), lambda i: (0, 0))],
        out_shape=[jax.ShapeDtypeStruct((t_len, 512), F32), jax.ShapeDtypeStruct((t_len, 512), F32),
                   jax.ShapeDtypeStruct((1, B_DV), F32)],
        compiler_params=_cp("arbitrary"),
    )(dcat, o_f, o_r, zc, gla_g)


def _mix_prep(dq, dkv, dqk_f, dqk_r, dv_f, dv_r, d_r, dla_f, dla_r, zc, wg2, bias2, cs, t_len, name):
    rows = zc.shape[0]
    tm = ROW_TILE
    n_x = t_len // tm
    gb = ZC_G // 128

    def xrow(i):
        return jnp.minimum(i, n_x - 1)

    def body(dq_ref, dkv_ref, dqkf_ref, dqkr_ref, dvf_ref, dvr_ref, dr_ref, dlaf_ref, dlar_ref, zg_ref, wg_ref,
             b_ref, cs_ref, dz_ref, dwg_ref, db_ref):
        i = pl.program_id(0)
        is_x = i < n_x
        cos = cs_ref[:, 0:128]
        sin = cs_ref[:, 128:256]
        cosq = jnp.concatenate([cos] * 4, axis=1)
        sinq = jnp.concatenate([sin] * 4, axis=1)
        dqv = jnp.where(is_x, dq_ref[...], 0.0)
        dz_ref[:, ZC_Q:ZC_QK] = (dqv * cosq + _swap16(dqv * sinq)).astype(BF16)
        dz_ref[:, ZC_QK:ZC_V] = (dqkf_ref[...] + dqkr_ref[...]).astype(BF16)
        dz_ref[:, ZC_V:ZC_R] = (dvf_ref[...] + dvr_ref[...]).astype(BF16)
        dz_ref[:, ZC_R:ZC_KV] = jnp.where(is_x, dr_ref[...], 0.0).astype(BF16)
        dk = dkv_ref[:, 0:128]
        dz_ref[:, ZC_KV:ZC_KV + 128] = (dk * cos + _swap16(dk * sin)).astype(BF16)
        dz_ref[:, ZC_KV + 128:ZC_G] = dkv_ref[:, 128:256].astype(BF16)
        zgb = zg_ref[...].astype(BF16)
        wg = wg_ref[...]
        pre = _dot(zgb, wg) + b_ref[...]
        dla = jnp.concatenate([dlaf_ref[...], dlar_ref[...]], axis=1)
        dpre = dla * (_sigmoid(-pre) / B_GATE_NORM)
        dpb = dpre.astype(BF16)
        dz_ref[:, ZC_G:ZC_W] = _dot_nt(dpb, wg).astype(BF16)

        @pl.when(i == 0)
        def _():
            dwg_ref[...] = jnp.zeros_like(dwg_ref)
            db_ref[...] = jnp.zeros_like(db_ref)

        dwg_ref[...] += _dot_tn(zgb, dpb)
        db_ref[...] += jnp.sum(dpre, axis=0, keepdims=True)

    return pl.pallas_call(
        body, name=name, grid=(rows // tm,),
        in_specs=[pl.BlockSpec((tm, 512), lambda i: (xrow(i), 0)),
                  pl.BlockSpec((tm, 256), lambda i: (i, 0)),
                  pl.BlockSpec((tm, 512), lambda i: (i, 0)),
                  pl.BlockSpec((tm, 512), lambda i: (i, 0)),
                  pl.BlockSpec((tm, 512), lambda i: (i, 0)),
                  pl.BlockSpec((tm, 512), lambda i: (i, 0)),
                  pl.BlockSpec((tm, 512), lambda i: (xrow(i), 0)),
                  pl.BlockSpec((tm, 256), lambda i: (i, 0)),
                  pl.BlockSpec((tm, 256), lambda i: (i, 0)),
                  pl.BlockSpec((tm, 128), lambda i: (i, gb)),
                  pl.BlockSpec((128, 512), lambda i: (0, 0)),
                  pl.BlockSpec((1, 512), lambda i: (0, 0)),
                  pl.BlockSpec((tm, 256), lambda i: (i, 0))],
        out_specs=[pl.BlockSpec((tm, ZC_W), lambda i: (i, 0)),
                   pl.BlockSpec((128, 512), lambda i: (0, 0)),
                   pl.BlockSpec((1, 512), lambda i: (0, 0))],
        out_shape=[jax.ShapeDtypeStruct((rows, ZC_W), BF16),
                   jax.ShapeDtypeStruct((128, 512), F32),
                   jax.ShapeDtypeStruct((1, 512), F32)],
        compiler_params=_cp("arbitrary"),
    )(dq, dkv, dqk_f, dqk_r, dv_f, dv_r, d_r, dla_f, dla_r, zc, wg2, bias2, cs)


def _gate_weights(w_a2_f, b_a_f, w_a2_b, b_a_b):
    wg2 = jnp.zeros((128, 512), F32)
    wg2 = wg2.at[0:B_GATE_RANK, 0:256].set(w_a2_f).at[B_GATE_RANK:2 * B_GATE_RANK, 256:512].set(w_a2_b)
    bias2 = jnp.concatenate([b_a_f, b_a_b]).reshape(1, 512)
    return wg2.astype(BF16), bias2


_WIN_PERM = ((0, 512), (768, 1280), (1280, 1792), (1792, 2304), (512, 768), (2304, 2336))


def _w_in_to_cat(w_in_full):
    parts = [w_in_full[:, a:b] for a, b in _WIN_PERM]
    parts.append(jnp.zeros((w_in_full.shape[0], ZC_W - PROJ_DIM), w_in_full.dtype))
    return jnp.concatenate(parts, axis=1)


def _cat_to_w_in(d_wcat):
    return jnp.concatenate([d_wcat[:, ZC_Q:ZC_QK], d_wcat[:, ZC_KV:ZC_G], d_wcat[:, ZC_QK:ZC_KV],
                            d_wcat[:, ZC_G:ZC_G + 2 * B_GATE_RANK]], axis=1)


def _mixer_ab_forward(x1, g3, mods, wcat, wg2, bias2, sink, gla_g, w_out, cs, t_len, l_ctx, n_x, pace):
    h, zc, la = _proj_fwd(x1, g3, mods, n_x, wcat, wg2, bias2, cs, "mix0_proj")
    dep = pace("proj", zc)
    o_a = _attn_fwd(zc, sink + dep[0, 0], t_len, l_ctx, "mix0_attn")
    dep = pace("attn", o_a)
    o_f, o_r, spf, spr = _gla_fwd(zc, la, dep, t_len, l_ctx, "mix0_gla")
    dep = pace("gla", o_f)
    cat = _gla_out_fwd(o_a, o_f, o_r, zc, gla_g + dep[0:1, 0:1], t_len, "mix0_glaout")
    x2, y = _matmul_resid(cat, w_out, x1, mods, 5, 1.0, n_x, t_len, "mix0_out")
    return x2, (x1, h, zc, la, o_a, o_f, o_r, spf, spr, cat, y)


def _mixer_ab_backward(dx2, saved, g3, mods, wcat, wg2, bias2, sink, gla_g, w_out, cs, t_len, l_ctx, n_x):
    x1, h, zc, la, o_a, o_f, o_r, spf, spr, cat, y = saved
    rows = x1.shape[0]
    tm = ROW_TILE
    dy, dcat, dgate = _gate_dy(dx2, y, mods, 5, 1.0, n_x, t_len, w_out, "mix0_dy")
    tk = _token_tile(t_len)
    d_wout = _matmul_tn(
        cat, dy, pl.BlockSpec((tk, D_MODEL), lambda n, k: (k, 0)), pl.BlockSpec((tk, D_MODEL), lambda n, k: (k, 0)),
        (D_MODEL, D_MODEL), pl.BlockSpec((D_MODEL, D_MODEL), lambda n, k: (0, 0)), (1, t_len // tk), "mix0_dwout")
    dos, d_r, d_glag = _gla_out_bwd(dcat, o_f, o_r, zc, gla_g, t_len, "mix0_dglaout")
    dqk_f, dv_f, dla_f, dqk_r, dv_r, dla_r = _gla_bwd(zc, la, spf, spr, dos, t_len, l_ctx, "mix0_dgla")
    dq, dkv, dsink = _attn_bwd(zc, sink, o_a, dcat, t_len, l_ctx, "mix0_dattn")
    dzc, dwg2, dbias2 = _mix_prep(dq, dkv, dqk_f, dqk_r, dv_f, dv_r, d_r, dla_f, dla_r, zc, wg2, bias2, cs, t_len,
                                  "mix0_prep")
    tk = _token_tile(rows)
    d_wcat = _matmul_tn(
        h, dzc, pl.BlockSpec((tk, D_MODEL), lambda n, k: (k, 0)), pl.BlockSpec((tk, ZC_W), lambda n, k: (k, 0)),
        (D_MODEL, ZC_W), pl.BlockSpec((D_MODEL, ZC_W), lambda n, k: (0, 0)), (1, rows // tk), "mix0_dwin")
    pairs = [(dzc, pl.BlockSpec((tm, ZC_W), lambda i: (i, 0)), wcat, pl.BlockSpec((D_MODEL, ZC_W), lambda i: (0, 0)))]
    dx1, stats = _bwd_dx(pairs, x1, dx2, t_len // tm, g3, mods, 1, n_x, "mix0_dx")
    return dx1, stats, dgate, d_wcat, dwg2, dbias2, dsink, d_glag, d_wout


_PT = 256
_PH = 16


def _pool_window(n, t_len, w, transpose):
    shape = (_PT, _PT + 2 * _PH)
    a = n * _PT + lax.broadcasted_iota(jnp.int32, shape, 0)
    b = n * _PT - _PH + lax.broadcasted_iota(jnp.int32, shape, 1)
    t, s = (b, a) if transpose else (a, b)
    lo = jnp.maximum(t - w // 2, 0)
    hi = jnp.minimum(t + (w - w // 2), t_len)
    inside = (s >= lo) & (s < hi) & (t >= 0) & (t < t_len)
    return jnp.where(inside, 1.0, 0.0).astype(BF16)


def _pool_inv_count(first, count, t_len, w):
    t = first + lax.broadcasted_iota(jnp.int32, (count, 1), 0)
    lo = jnp.maximum(t - w // 2, 0)
    hi = jnp.minimum(t + (w - w // 2), t_len)
    return jnp.where((t >= 0) & (t < t_len), 1.0 / jnp.maximum(hi - lo, 1).astype(F32), 0.0)


def _window_sum(win, vals):
    hi, lo = _split_bf16(vals)
    return _dot(win, hi) + _dot(win, lo)


def _pool_halo(p_ref, c_ref, n_ref):
    return jnp.concatenate([p_ref[_PT - _PH:_PT, :], c_ref[...], n_ref[0:_PH, :]], axis=0)


def _pool_specs(t_len):
    nb = t_len // _PT
    return [pl.BlockSpec((_PT, D_MODEL), lambda n: (jnp.maximum(n - 1, 0), 0)),
            pl.BlockSpec((_PT, D_MODEL), lambda n: (n, 0)),
            pl.BlockSpec((_PT, D_MODEL), lambda n: (jnp.minimum(n + 1, nb - 1), 0))], nb


def _pool_fwd(h, wp, pscale, x1, mods, t_len, name):
    halo_specs, nb = _pool_specs(t_len)

    def body(hp_ref, hc_ref, hn_ref, w_ref, ps_ref, x_ref, m_ref, x2_ref, pooled_ref, ypre_ref):
        n = pl.program_id(0)
        hcat = _pool_halo(hp_ref, hc_ref, hn_ref)
        ys = []
        for gi, w in enumerate(POOL_WINDOWS):
            cols = slice(gi * POOL_GROUP, (gi + 1) * POOL_GROUP)
            hg = hcat[:, cols]
            mean = _window_sum(_pool_window(n, t_len, w, False), hg) * _pool_inv_count(n * _PT, _PT, t_len, w)
            pooled = (mean - hg[_PH:_PH + _PT]).astype(BF16)
            pooled_ref[:, cols] = pooled
            ys.append(_dot(pooled, w_ref[gi]))
        ypre = jnp.concatenate(ys, axis=1)
        ypre_ref[...] = ypre
        x2_ref[...] = x_ref[...] + m_ref[0, 5:6, :] * (ypre * ps_ref[...])

    return pl.pallas_call(
        body, name=name, grid=(nb,),
        in_specs=halo_specs + [pl.BlockSpec((4, POOL_GROUP, POOL_GROUP), lambda n: (0, 0, 0)),
                               pl.BlockSpec((1, D_MODEL), lambda n: (0, 0)),
                               pl.BlockSpec((_PT, D_MODEL), lambda n: (n, 0)),
                               pl.BlockSpec((1, N_MOD, D_MODEL), lambda n: (0, 0, 0))],
        out_specs=[pl.BlockSpec((_PT, D_MODEL), lambda n: (n, 0))] * 3,
        out_shape=[jax.ShapeDtypeStruct((t_len, D_MODEL), F32), jax.ShapeDtypeStruct((t_len, D_MODEL), BF16),
                   jax.ShapeDtypeStruct((t_len, D_MODEL), F32)],
        compiler_params=_cp("parallel"),
    )(h, h, h, wp, pscale, x1, mods)


def _pool_bwd_a(dx2, ypre, wp, pscale, mods, t_len, name):
    nb = t_len // _PT

    def body(d_ref, y_ref, w_ref, ps_ref, m_ref, dyp_ref, dpl_ref, dgate_ref, dps_ref):
        n = pl.program_id(0)
        dv = d_ref[...]
        ypre = y_ref[...]
        ps = ps_ref[...]
        dy = dv * m_ref[0, 5:6, :]
        dyp = (dy * ps).astype(BF16)
        dyp_ref[...] = dyp
        for gi in range(len(POOL_WINDOWS)):
            cols = slice(gi * POOL_GROUP, (gi + 1) * POOL_GROUP)
            dpl_ref[:, cols] = _dot_nt(dyp[:, cols], w_ref[gi])

        @pl.when(n == 0)
        def _():
            dgate_ref[...] = jnp.zeros_like(dgate_ref)
            dps_ref[...] = jnp.zeros_like(dps_ref)

        dgate_ref[...] += jnp.sum(dv * (ypre * ps), axis=0, keepdims=True)
        dps_ref[...] += jnp.sum(dy * ypre, axis=0, keepdims=True)

    return pl.pallas_call(
        body, name=name, grid=(nb,),
        in_specs=[pl.BlockSpec((_PT, D_MODEL), lambda n: (n, 0)),
                  pl.BlockSpec((_PT, D_MODEL), lambda n: (n, 0)),
                  pl.BlockSpec((4, POOL_GROUP, POOL_GROUP), lambda n: (0, 0, 0)),
                  pl.BlockSpec((1, D_MODEL), lambda n: (0, 0)),
                  pl.BlockSpec((1, N_MOD, D_MODEL), lambda n: (0, 0, 0))],
        out_specs=[pl.BlockSpec((_PT, D_MODEL), lambda n: (n, 0)),
                   pl.BlockSpec((_PT, D_MODEL), lambda n: (n, 0)),
                   pl.BlockSpec((1, D_MODEL), lambda n: (0, 0)),
                   pl.BlockSpec((1, D_MODEL), lambda n: (0, 0))],
        out_shape=[jax.ShapeDtypeStruct((t_len, D_MODEL), BF16), jax.ShapeDtypeStruct((t_len, D_MODEL), F32),
                   jax.ShapeDtypeStruct((1, D_MODEL), F32), jax.ShapeDtypeStruct((1, D_MODEL), F32)],
        compiler_params=_cp("arbitrary"),
    )(dx2, ypre, wp, pscale, mods)


def _pool_bwd_dx(dpl, x1, dx2, g3, mods, t_len, name):
    halo_specs, nb = _pool_specs(t_len)

    def body(dp_ref, dc_ref, dn_ref, x_ref, d_ref, g_ref, m_ref, dx_ref, acc_ref):
        n = pl.program_id(0)
        dcat = _pool_halo(dp_ref, dc_ref, dn_ref)
        dhs = []
        for gi, w in enumerate(POOL_WINDOWS):
            cols = slice(gi * POOL_GROUP, (gi + 1) * POOL_GROUP)
            dg = dcat[:, cols]
            scaled = dg * _pool_inv_count(n * _PT - _PH, _PT + 2 * _PH, t_len, w)
            dhs.append(_window_sum(_pool_window(n, t_len, w, True), scaled) - dg[_PH:_PH + _PT])
        dh = jnp.concatenate(dhs, axis=1)
        g = g_ref[1:2, :]
        scale = m_ref[0, 4:5, :]
        dx = _rms_mod_bwd_tail(dh, x_ref[...], g, scale, 0, acc_ref, n == 0)
        dx_ref[...] = d_ref[...] + dx

    return pl.pallas_call(
        body, name=name, grid=(nb,),
        in_specs=halo_specs + [pl.BlockSpec((_PT, D_MODEL), lambda n: (n, 0)),
                               pl.BlockSpec((_PT, D_MODEL), lambda n: (n, 0)),
                               pl.BlockSpec((3, D_MODEL), lambda n: (0, 0)),
                               pl.BlockSpec((1, N_MOD, D_MODEL), lambda n: (0, 0, 0))],
        out_specs=[pl.BlockSpec((_PT, D_MODEL), lambda n: (n, 0)),
                   pl.BlockSpec((2, 3, D_MODEL), lambda n: (0, 0, 0))],
        out_shape=[jax.ShapeDtypeStruct((t_len, D_MODEL), F32), jax.ShapeDtypeStruct((2, 3, D_MODEL), F32)],
        compiler_params=_cp("arbitrary"),
    )(dpl, dpl, dpl, x1, dx2, g3, mods)


def _mixer_pool_forward(x1, g3, mods, wp, pscale, t_len):
    h = _rms_mod_fwd(x1, g3, mods, 1, t_len // ROW_TILE, F32, "mix1_mod")
    x2, pooled, ypre = _pool_fwd(h, wp, pscale, x1, mods, t_len, "mix1_pool")
    return x2, (x1, pooled, ypre)


def _mixer_pool_backward(dx2, saved, g3, mods, wp, pscale, t_len):
    x1, pooled, ypre = saved
    tm = ROW_TILE
    dyp, dpl, dgate, dps = _pool_bwd_a(dx2, ypre, wp, pscale, mods, t_len, "mix1_da")
    d_wp = _matmul_tn(
        pooled, dyp, pl.BlockSpec((tm, POOL_GROUP), lambda g, k: (k, g)),
        pl.BlockSpec((tm, POOL_GROUP), lambda g, k: (k, g)),
        (4, POOL_GROUP, POOL_GROUP), pl.BlockSpec((1, POOL_GROUP, POOL_GROUP), lambda g, k: (g, 0, 0)),
        (4, t_len // tm), "mix1_dwp")
    dx1, stats = _pool_bwd_dx(dpl, x1, dx2, g3, mods, t_len, "mix1_dx")
    return dx1, stats, dgate, dps, d_wp


def _final_loss(x3, final_g, target, name):
    t_len = x3.shape[0]
    tm = ROW_TILE

    def body(x_ref, g_ref, t_ref, dx_ref, loss_ref, dg_ref):
        i = pl.program_id(0)
        xv = x_ref[...]
        g = g_ref[...]
        r = lax.rsqrt(jnp.mean(xv * xv, axis=-1, keepdims=True) + RMS_EPS)
        xhat = xv * r
        err = xhat * g - t_ref[...]
        part = 0.5 * jnp.sum(jnp.mean(err * err, axis=-1, keepdims=True), axis=0, keepdims=True)
        dy = err * (1.0 / D_MODEL)

        @pl.when(i == 0)
        def _():
            loss_ref[...] = jnp.zeros_like(loss_ref)
            dg_ref[...] = jnp.zeros_like(dg_ref)

        loss_ref[...] += jnp.broadcast_to(part, (1, 128))
        dg_ref[...] += jnp.sum(dy * xhat, axis=0, keepdims=True)
        dxh = dy * g
        dx_ref[...] = r * (dxh - xhat * jnp.mean(dxh * xhat, axis=-1, keepdims=True))

    return pl.pallas_call(
        body, name=name, grid=(t_len // tm,),
        in_specs=[pl.BlockSpec((tm, D_MODEL), lambda i: (i, 0)),
                  pl.BlockSpec((1, D_MODEL), lambda i: (0, 0)),
                  pl.BlockSpec((tm, D_MODEL), lambda i: (i, 0))],
        out_specs=[pl.BlockSpec((tm, D_MODEL), lambda i: (i, 0)),
                   pl.BlockSpec((1, 128), lambda i: (0, 0)),
                   pl.BlockSpec((1, D_MODEL), lambda i: (0, 0))],
        out_shape=[jax.ShapeDtypeStruct((t_len, D_MODEL), F32), jax.ShapeDtypeStruct((1, 128), F32),
                   jax.ShapeDtypeStruct((1, D_MODEL), F32)],
        compiler_params=_cp("arbitrary"),
    )(x3, final_g, target)


_CROWS = 16


def _adaln_fwd(c16, w_mod, bias_k, name):
    n_l, _, cols = w_mod.shape

    def body(c_ref, w_ref, b_ref, o_ref):
        cv = c_ref[...]
        sc = (cv * _sigmoid(cv)).astype(BF16)
        o_ref[0] = _dot(sc, w_ref[0].astype(BF16)) + b_ref[0]

    return pl.pallas_call(
        body, name=name, grid=(n_l,),
        in_specs=[pl.BlockSpec((_CROWS, D_MODEL), lambda l: (0, 0)),
                  pl.BlockSpec((1, D_MODEL, cols), lambda l: (l, 0, 0)),
                  pl.BlockSpec((1, 1, cols), lambda l: (l, 0, 0))],
        out_specs=pl.BlockSpec((1, _CROWS, cols), lambda l: (l, 0, 0)),
        out_shape=jax.ShapeDtypeStruct((n_l, _CROWS, cols), F32),
        compiler_params=_cp("parallel"),
    )(c16, w_mod, bias_k)


def _adaln_bwd(c16, d16, w_mod, dmmc_k, name):
    n_l, _, cols = w_mod.shape

    def body(c_ref, d_ref, w_ref, dm_ref, gw_ref, cp_ref):
        layer = pl.program_id(0)
        cv = c_ref[...]
        gw_ref[0] = _dot_tn_hi(cv * _sigmoid(cv), d_ref[0])

        @pl.when(layer == 0)
        def _():
            cp_ref[...] = jnp.sum(w_ref[0] * dm_ref[...], axis=1, keepdims=True)

    return pl.pallas_call(
        body, name=name, grid=(n_l,),
        in_specs=[pl.BlockSpec((_CROWS, D_MODEL), lambda l: (0, 0)),
                  pl.BlockSpec((1, _CROWS, cols), lambda l: (l, 0, 0)),
                  pl.BlockSpec((1, D_MODEL, cols), lambda l: (0, 0, 0)),
                  pl.BlockSpec((1, cols), lambda l: (0, 0))],
        out_specs=[pl.BlockSpec((1, D_MODEL, cols), lambda l: (l, 0, 0)),
                   pl.BlockSpec((D_MODEL, 1), lambda l: (0, 0))],
        out_shape=[jax.ShapeDtypeStruct((n_l, D_MODEL, cols), F32), jax.ShapeDtypeStruct((D_MODEL, 1), F32)],
        compiler_params=_cp("arbitrary"),
    )(c16, d16, w_mod, dmmc_k)


def _cctx_grad(cparts, c_ctx2, name):
    def body(p_ref, c_ref, o_ref):
        tot = ((p_ref[0] + p_ref[2]) + p_ref[4]) + p_ref[6]
        cv = c_ref[...]
        sg = _sigmoid(cv)
        o_ref[...] = tot * (sg * (1.0 + cv * (1.0 - sg)))

    return pl.pallas_call(
        body, name=name, out_shape=jax.ShapeDtypeStruct((8, 128), F32),
        in_specs=[pl.BlockSpec(memory_space=pltpu.VMEM), pl.BlockSpec(memory_space=pltpu.VMEM)],
        out_specs=pl.BlockSpec(memory_space=pltpu.VMEM),
    )(cparts, c_ctx2)


def _sum_devices(ga, name):
    def body(g_ref, o_ref):
        acc = g_ref[0]
        for d in range(1, N_DEV):
            acc = acc + g_ref[d]
        o_ref[...] = acc

    return pl.pallas_call(
        body, name=name, out_shape=jax.ShapeDtypeStruct(ga.shape[1:], F32),
        in_specs=[pl.BlockSpec(memory_space=pltpu.VMEM)], out_specs=pl.BlockSpec(memory_space=pltpu.VMEM),
    )(ga)


def _place():
    return lax.axis_index("x"), lax.axis_index("y"), lax.axis_index("c")


def _flip(a, d):
    return 1 - a if d else a


_CHIP_FLIPS = ((1, 0), (0, 1), (1, 1))


def _allgather_small(v, name, after=()):
    r, cc = v.shape

    def body(v_ref, *rest):
        out_ref, send_sems, recv_sems, local_sem = rest[-4:]
        x, y, c = _place()
        me = 4 * x + 2 * y + c
        mine = pltpu.make_async_copy(v_ref, out_ref.at[me], local_sem)
        mine.start()
        sends = []
        for k in range(1, N_DEV):
            peer = (_flip(x, (k >> 2) & 1), _flip(y, (k >> 1) & 1), _flip(c, k & 1))
            cp = pltpu.make_async_remote_copy(src_ref=v_ref, dst_ref=out_ref.at[me], send_sem=send_sems.at[k - 1],
                                              recv_sem=recv_sems.at[k - 1], device_id=peer, device_id_type=MESH)
            cp.start()
            sends.append(cp)
        for k in range(1, N_DEV):
            px, py, pc = _flip(x, (k >> 2) & 1), _flip(y, (k >> 1) & 1), _flip(c, k & 1)
            pltpu.make_async_remote_copy(src_ref=v_ref, dst_ref=out_ref.at[4 * px + 2 * py + pc],
                                         send_sem=send_sems.at[k - 1], recv_sem=recv_sems.at[k - 1],
                                         device_id=(px, py, pc), device_id_type=MESH).wait_recv()
        for cp in sends:
            cp.wait_send()
        mine.wait()

    return pl.pallas_call(
        body, name=name, out_shape=jax.ShapeDtypeStruct((N_DEV, r, cc), F32),
        in_specs=[pl.BlockSpec(memory_space=pltpu.VMEM)] + [pl.BlockSpec(memory_space=pl.ANY)] * len(after),
        out_specs=pl.BlockSpec(memory_space=pltpu.VMEM),
        scratch_shapes=[pltpu.SemaphoreType.DMA((N_DEV - 1,)), pltpu.SemaphoreType.DMA((N_DEV - 1,)),
                        pltpu.SemaphoreType.DMA],
        compiler_params=pltpu.CompilerParams(vmem_limit_bytes=VMEM_LIMIT_BYTES),
    )(v, *after)


_HBM_SPEC = pl.BlockSpec(memory_space=pltpu.HBM)
_SEM_SPEC = pl.BlockSpec(memory_space=pltpu.SEMAPHORE)
_EFFECT = pltpu.SideEffectType.DATAFLOW_SIDE_EFFECTING


def _in_hbm(a):
    return pltpu.with_memory_space_constraint(a, pltpu.HBM)


def _my_half(ref, c):
    h = ref.shape[0] // 2
    return ref.at[pl.ds(pl.multiple_of(c * h, 16), h)]


def _gather_start(arrs, groups, after, name, halves=False):
    n, n_g = len(arrs), len(groups)

    def body(*refs):
        ins, zones = refs[:n], refs[n:2 * n]
        sems = refs[2 * n + 1:2 * n + 1 + 2 * n_g]
        token = refs[2 * n + 1 + 2 * n_g + 2 * n]
        x, y, c = _place()
        k_me = 2 * x + y
        for g, members in enumerate(groups):
            for t, a in enumerate(members):
                for j, (dx, dy) in enumerate(_CHIP_FLIPS):
                    src, dst = ins[a], zones[a].at[k_me]
                    if halves:
                        src, dst = _my_half(src, c), _my_half(dst, c)
                    pltpu.make_async_remote_copy(
                        src_ref=src, dst_ref=dst, send_sem=sems[2 * g].at[3 * t + j],
                        recv_sem=sems[2 * g + 1].at[3 * t + j], device_id=(_flip(x, dx), _flip(y, dy), c),
                        device_id_type=MESH).start()
        token[...] = jnp.zeros_like(token)

    k_own = 2 * lax.axis_index("x") + lax.axis_index("y")
    zones = [lax.dynamic_update_slice(lax.empty((N_CHIPS,) + a.shape, a.dtype), a[None], (k_own,) + (0,) * a.ndim)
             for a in arrs]
    sem_shapes = []
    for members in groups:
        sem_shapes += [pltpu.SemaphoreType.DMA((3 * len(members),))] * 2
    outs = pl.pallas_call(
        body, name=name,
        out_shape=sem_shapes + [pltpu.HBM(a.shape, a.dtype) for a in arrs]
        + [pltpu.HBM(z.shape, z.dtype) for z in zones] + [jax.ShapeDtypeStruct((8, 128), F32)],
        in_specs=[_HBM_SPEC] * (2 * n) + [pl.BlockSpec(memory_space=pl.ANY)],
        out_specs=[_SEM_SPEC] * (2 * n_g) + [_HBM_SPEC] * (2 * n) + [pl.BlockSpec(memory_space=pltpu.VMEM)],
        input_output_aliases={i: 2 * n_g + i for i in range(2 * n)},
        compiler_params=pltpu.CompilerParams(has_side_effects=_EFFECT),
    )(*[_in_hbm(a) for a in arrs], *[_in_hbm(z) for z in zones], after)
    sems = outs[:2 * n_g]
    thru = outs[2 * n_g:2 * n_g + n]
    zones = outs[2 * n_g + n:2 * n_g + 2 * n]
    return [(sems[2 * g], sems[2 * g + 1]) for g in range(n_g)], thru, zones, outs[-1]


def _gather_wait(shards, zones, send_sems, recv_sems, after, name, halves=False):
    m = len(shards)

    def body(*refs):
        ins, zs = refs[:m], refs[m:2 * m]
        ssem, rsem = refs[2 * m], refs[2 * m + 1]
        x, y, c = _place()
        for t in range(m):
            for j, (dx, dy) in enumerate(_CHIP_FLIPS):
                px, py = _flip(x, dx), _flip(y, dy)
                src, dst = ins[t], zs[t].at[2 * px + py]
                if halves:
                    src, dst = _my_half(src, c), _my_half(dst, c)
                cp = pltpu.make_async_remote_copy(
                    src_ref=src, dst_ref=dst, send_sem=ssem.at[3 * t + j],
                    recv_sem=rsem.at[3 * t + j], device_id=(px, py, c), device_id_type=MESH)
                cp.wait_send()
                cp.wait_recv()

    after = list(after) if isinstance(after, (list, tuple)) else [after]
    outs = pl.pallas_call(
        body, name=name,
        out_shape=[pltpu.HBM(a.shape, a.dtype) for a in list(shards) + list(zones)],
        in_specs=[_HBM_SPEC] * (2 * m) + [_SEM_SPEC, _SEM_SPEC] + [pl.BlockSpec(memory_space=pl.ANY)] * len(after),
        out_specs=[_HBM_SPEC] * (2 * m),
        input_output_aliases={i: i for i in range(2 * m)},
        compiler_params=pltpu.CompilerParams(has_side_effects=_EFFECT),
    )(*shards, *zones, send_sems, recv_sems, *after)
    return outs[m:]


def _relay_halves(zones, name):
    n = len(zones)

    def slot_half(ref, k, half):
        h = ref.shape[1] // 2
        return ref.at[k, pl.ds(pl.multiple_of(half * h, 16), h)]

    def body(*refs):
        ins, outs = refs[:n], refs[n:2 * n]
        send_sems, recv_sems = refs[2 * n:]
        x, y, c = _place()
        sends = []
        for a in range(n):
            for j, (dx, dy) in enumerate(_CHIP_FLIPS):
                kp = 2 * _flip(x, dx) + _flip(y, dy)
                cp = pltpu.make_async_remote_copy(
                    src_ref=slot_half(ins[a], kp, c), dst_ref=slot_half(outs[a], kp, c),
                    send_sem=send_sems.at[3 * a + j], recv_sem=recv_sems.at[3 * a + j],
                    device_id=(x, y, 1 - c), device_id_type=MESH)
                cp.start()
                sends.append(cp)
        for a in range(n):
            for j, (dx, dy) in enumerate(_CHIP_FLIPS):
                kp = 2 * _flip(x, dx) + _flip(y, dy)
                pltpu.make_async_remote_copy(
                    src_ref=slot_half(ins[a], kp, c), dst_ref=slot_half(outs[a], kp, 1 - c),
                    send_sem=send_sems.at[3 * a + j], recv_sem=recv_sems.at[3 * a + j],
                    device_id=(x, y, 1 - c), device_id_type=MESH).wait_recv()
        for cp in sends:
            cp.wait_send()

    any_spec = pl.BlockSpec(memory_space=pl.ANY)
    return pl.pallas_call(
        body, name=name,
        out_shape=[jax.ShapeDtypeStruct(z.shape, z.dtype) for z in zones],
        in_specs=[any_spec] * n, out_specs=[any_spec] * n,
        input_output_aliases={a: a for a in range(n)},
        scratch_shapes=[pltpu.SemaphoreType.DMA((3 * n,)), pltpu.SemaphoreType.DMA((3 * n,))],
    )(*zones)


def _scatter_start(arrs, name):
    n = len(arrs)

    def body(*refs):
        ins, lands = refs[:n], refs[n:2 * n]
        ssem, rsem = refs[2 * n], refs[2 * n + 1]
        token = refs[2 * n + 2 + 2 * n]
        x, y, c = _place()
        for a in range(n):
            for j, (dx, dy) in enumerate(_CHIP_FLIPS):
                px, py = _flip(x, dx), _flip(y, dy)
                pltpu.make_async_remote_copy(
                    src_ref=ins[a].at[2 * px + py], dst_ref=lands[a].at[j], send_sem=ssem.at[3 * a + j],
                    recv_sem=rsem.at[3 * a + j], device_id=(px, py, c), device_id_type=MESH).start()
        token[...] = jnp.zeros_like(token)

    lands = [lax.empty((3,) + a.shape[1:], a.dtype) for a in arrs]
    outs = pl.pallas_call(
        body, name=name,
        out_shape=[pltpu.SemaphoreType.DMA((3 * n,))] * 2 + [pltpu.HBM(a.shape, a.dtype) for a in arrs]
        + [pltpu.HBM(z.shape, z.dtype) for z in lands] + [jax.ShapeDtypeStruct((8, 128), F32)],
        in_specs=[_HBM_SPEC] * (2 * n),
        out_specs=[_SEM_SPEC] * 2 + [_HBM_SPEC] * (2 * n) + [pl.BlockSpec(memory_space=pltpu.VMEM)],
        input_output_aliases={i: 2 + i for i in range(2 * n)},
        compiler_params=pltpu.CompilerParams(has_side_effects=_EFFECT),
    )(*[_in_hbm(a) for a in arrs], *[_in_hbm(z) for z in lands])
    return outs[0], outs[1], outs[2:2 + n], outs[2 + n:2 + 2 * n], outs[-1]


def _scatter_wait(arrs, lands, send_sems, recv_sems, after, name):
    n = len(arrs)

    def body(*refs):
        ins, lz = refs[:n], refs[n:2 * n]
        ssem, rsem = refs[2 * n], refs[2 * n + 1]
        x, y, c = _place()
        for a in range(n):
            for j, (dx, dy) in enumerate(_CHIP_FLIPS):
                px, py = _flip(x, dx), _flip(y, dy)
                cp = pltpu.make_async_remote_copy(
                    src_ref=ins[a].at[2 * px + py], dst_ref=lz[a].at[j], send_sem=ssem.at[3 * a + j],
                    recv_sem=rsem.at[3 * a + j], device_id=(px, py, c), device_id_type=MESH)
                cp.wait_send()
                cp.wait_recv()

    outs = pl.pallas_call(
        body, name=name,
        out_shape=[pltpu.HBM(a.shape, a.dtype) for a in list(arrs) + list(lands)],
        in_specs=[_HBM_SPEC] * (2 * n) + [_SEM_SPEC, _SEM_SPEC, pl.BlockSpec(memory_space=pl.ANY)],
        out_specs=[_HBM_SPEC] * (2 * n),
        input_output_aliases={i: i for i in range(2 * n)},
        compiler_params=pltpu.CompilerParams(has_side_effects=_EFFECT),
    )(*arrs, *lands, send_sems, recv_sems, after)
    return outs[:n], outs[n:]


def _swap_start(arrs, name):
    n = len(arrs)

    def body(*refs):
        ins, lands = refs[:n], refs[n:2 * n]
        ssem, rsem = refs[2 * n], refs[2 * n + 1]
        token = refs[2 * n + 2 + 2 * n]
        x, y, c = _place()
        for a in range(n):
            pltpu.make_async_remote_copy(src_ref=ins[a], dst_ref=lands[a], send_sem=ssem.at[a], recv_sem=rsem.at[a],
                                         device_id=(x, y, 1 - c), device_id_type=MESH).start()
        token[...] = jnp.zeros_like(token)

    lands = [lax.empty(a.shape, a.dtype) for a in arrs]
    outs = pl.pallas_call(
        body, name=name,
        out_shape=[pltpu.SemaphoreType.DMA((n,))] * 2 + [pltpu.HBM(a.shape, a.dtype) for a in arrs]
        + [pltpu.HBM(z.shape, z.dtype) for z in lands] + [jax.ShapeDtypeStruct((8, 128), F32)],
        in_specs=[_HBM_SPEC] * (2 * n),
        out_specs=[_SEM_SPEC] * 2 + [_HBM_SPEC] * (2 * n) + [pl.BlockSpec(memory_space=pltpu.VMEM)],
        input_output_aliases={i: 2 + i for i in range(2 * n)},
        compiler_params=pltpu.CompilerParams(has_side_effects=_EFFECT),
    )(*[_in_hbm(a) for a in arrs], *[_in_hbm(z) for z in lands])
    return outs[0], outs[1], outs[2:2 + n], outs[2 + n:2 + 2 * n], outs[-1]


def _swap_wait(arrs, lands, send_sems, recv_sems, after, name):
    n = len(arrs)

    def body(*refs):
        ins, lz = refs[:n], refs[n:2 * n]
        ssem, rsem = refs[2 * n], refs[2 * n + 1]
        x, y, c = _place()
        for a in range(n):
            cp = pltpu.make_async_remote_copy(src_ref=ins[a], dst_ref=lz[a], send_sem=ssem.at[a], recv_sem=rsem.at[a],
                                              device_id=(x, y, 1 - c), device_id_type=MESH)
            cp.wait_send()
            cp.wait_recv()

    outs = pl.pallas_call(
        body, name=name,
        out_shape=[pltpu.HBM(a.shape, a.dtype) for a in list(arrs) + list(lands)],
        in_specs=[_HBM_SPEC] * (2 * n) + [_SEM_SPEC, _SEM_SPEC, pl.BlockSpec(memory_space=pl.ANY)],
        out_specs=[_HBM_SPEC] * (2 * n),
        input_output_aliases={i: i for i in range(2 * n)},
        compiler_params=pltpu.CompilerParams(has_side_effects=_EFFECT),
    )(*arrs, *lands, send_sems, recv_sems, after)
    return outs[:n], outs[n:]


def _row_tile(rows, cols):
    for tr in (1024, 512, 256, 128, 64, 32, 16, 8):
        if rows % tr == 0 and tr * cols * 4 <= (1 << 20):
            return tr
    return rows


def _partial_sum(g_full, recv, k_idx, name):
    _, r, c = g_full.shape
    tr = _row_tile(r, c)

    def body(k_ref, g_ref, r_ref, o_ref):
        del k_ref
        acc = g_ref[0].astype(F32)
        for j in range(3):
            acc = acc + r_ref[j].astype(F32)
        o_ref[...] = acc

    return pl.pallas_call(
        body, name=name,
        grid_spec=pltpu.PrefetchScalarGridSpec(
            num_scalar_prefetch=1, grid=(r // tr,),
            in_specs=[pl.BlockSpec((1, tr, c), lambda i, k: (k[0], i, 0)),
                      pl.BlockSpec((3, tr, c), lambda i, k: (0, i, 0))],
            out_specs=pl.BlockSpec((tr, c), lambda i, k: (i, 0))),
        out_shape=jax.ShapeDtypeStruct((r, c), F32),
        compiler_params=_cp("parallel"),
    )(k_idx, g_full, recv)


def _adamw(w3, parts, m3, v3, layer, prev, name):
    n_l, r, c = w3.shape
    tr = _row_tile(r, c)
    n_i = r // tr
    n_p = len(parts)
    c1 = 1.0 - ADAM_B1 ** ADAM_STEP
    c2 = 1.0 - ADAM_B2 ** ADAM_STEP
    stacked = [isinstance(p, tuple) for p in parts]

    def body(*refs):
        w_ref, m_ref, v_ref = refs[0:3]
        g_refs = refs[3:3 + n_p]
        go_ref, d_ref, mo_ref, vo_ref = refs[-4:]
        g = None
        for p in range(n_p):
            term = g_refs[p][0] if stacked[p] else g_refs[p][...]
            g = term if g is None else g + term
        w = w_ref[0]
        m = ADAM_B1 * m_ref[0] + (1.0 - ADAM_B1) * g
        v = ADAM_B2 * v_ref[0] + (1.0 - ADAM_B2) * (g * g)
        m_hat = m / c1
        v_hat = v / c2
        go_ref[0] = g
        d_ref[0] = -ADAM_LR * (m_hat / (jnp.sqrt(v_hat) + ADAM_EPS) + ADAM_WD * w)
        mo_ref[0] = m
        vo_ref[0] = v

    blk = pl.BlockSpec((1, tr, c), lambda i: (layer, i, 0))
    in_specs = [blk, blk, blk]
    args = [w3, m3, v3]
    for part in parts:
        if isinstance(part, tuple):
            in_specs.append(pl.BlockSpec((1, tr, c), functools.partial(lambda idx, i: (idx, i, 0), part[1])))
            args.append(part[0])
        else:
            in_specs.append(pl.BlockSpec((tr, c), lambda i: (i, 0)))
            args.append(part)
    aliases = {}
    if prev is not None:
        in_specs += [pl.BlockSpec(memory_space=pl.ANY)] * 4
        aliases = {len(args) + q: q for q in range(4)}
        args += list(prev)
    shp = jax.ShapeDtypeStruct((n_l, r, c), F32)
    return pl.pallas_call(
        body, name=name, grid=(n_i,), in_specs=in_specs, out_specs=[blk] * 4, out_shape=[shp] * 4,
        input_output_aliases=aliases, compiler_params=_cp("parallel"),
    )(*args)


_SMALL_W = 4096
_PACK_ROWS = 352
_N9 = N_MOD * D_MODEL


def _flat_pad(parts, total):
    flat = jnp.concatenate([p.reshape(-1) for p in parts])
    return jnp.concatenate([flat, jnp.zeros((total - flat.shape[0],), F32)])


def kernel(x, c, ctx, c_ctx, w_mod, b_mod, norm_g, ffn1_wi, ffn1_wo, ffn2_wi, ffn2_wo, w_in, w_a2_f, b_a_f, w_a2_b, b_a_b, sink, gla_g, w_out, w_pool, pool_scale, final_g, loss_target, m_c_ctx, m_w_mod, m_b_mod, m_norm_g, m_ffn1_wi, m_ffn1_wo, m_ffn2_wi, m_ffn2_wo, m_w_in, m_w_a2_f, m_b_a_f, m_w_a2_b, m_b_a_b, m_sink, m_gla_g, m_w_out, m_w_pool, m_pool_scale, m_final_g, v_c_ctx, v_w_mod, v_b_mod, v_norm_g, v_ffn1_wi, v_ffn1_wo, v_ffn2_wi, v_ffn2_wo, v_w_in, v_w_a2_f, v_b_a_f, v_w_a2_b, v_b_a_b, v_sink, v_gla_g, v_w_out, v_w_pool, v_pool_scale, v_final_g):
    t_len, l_ctx = x.shape[1], ctx.shape[1]
    tm = ROW_TILE
    pad = (-(t_len + l_ctx)) % tm
    rows0 = t_len + l_ctx + pad
    n_x = t_len // tm
    xi, yi, ci = _place()
    k_me = 2 * xi + yi
    me = 4 * xi + 2 * yi + ci
    mod_cols = w_mod.shape[2]
    n_grp = len(POOL_WINDOWS)

    small_w = _flat_pad([norm_g, w_a2_f, w_a2_b, pool_scale], _SMALL_W).reshape(_SMALL_W // 128, 128)
    shards = [ffn1_wi[0], ffn1_wi[1], ffn1_wo[0], ffn1_wo[1], ffn2_wi[0], ffn2_wi[1], ffn2_wo[0], ffn2_wo[1],
              w_in[0], w_out[0], w_pool[0].reshape(n_grp * w_pool.shape[2], POOL_GROUP)]

    send_src = [s.astype(BF16) for s in shards] + [small_w]
    groups = ([11, 0], [2], [8, 9], [4], [6], [1], [3], [10, 5], [7])
    started = {}

    def gather_start(gs, after):
        members, index, pos = [], [], 0
        for g in gs:
            members += groups[g]
            index.append(tuple(range(pos, pos + len(groups[g]))))
            pos += len(groups[g])
        sems, thru, zones, token = _gather_start([send_src[a] for a in members], tuple(index), after,
                                                 "gather_start_%d" % gs[0], halves=gs == two_level)
        for k, (g, idx) in enumerate(zip(gs, index)):
            started[g] = (sems[k], [thru[i] for i in idx], [zones[i] for i in idx])
        return token

    def gather_wait(g, after):
        (ssem, rsem), thru, zones = started[g]
        got = _gather_wait(thru, zones, ssem, rsem, after, "gather_wait_%d" % g, halves=(g,) == two_level)
        if (g,) == two_level:
            got = _relay_halves(got, "gather_relay_%d" % g)
        return dict(zip(groups[g], got))

    two_level = (0,)

    c_all = _allgather_small(c.reshape(8, 128), "gather_cond").reshape(N_DEV, D_MODEL)
    tok = gather_start((0,), c_all)
    c16 = jnp.concatenate([c_all, c_ctx[None], jnp.zeros((_CROWS - N_DEV - 1, D_MODEL), F32)], axis=0) + tok[0:1, 0:1]
    bias_k = lax.dynamic_slice(b_mod, (0, k_me * mod_cols), (2, mod_cols)).reshape(2, 1, mod_cols)
    mm_k = _adaln_fwd(c16, w_mod, bias_k, "adaln_fwd")
    cs = _rope_tables(t_len, rows0)
    xcat = jnp.concatenate([x[0], ctx[0], jnp.zeros((pad, D_MODEL), F32)], axis=0)
    mm_all = _allgather_small(mm_k.reshape(-1, 128), "gather_mod", (cs, xcat)).reshape(N_DEV, 2, _CROWS, mod_cols)
    mm_full = jnp.concatenate([mm_all[2 * k] for k in range(N_CHIPS)], axis=-1)
    mm_x = lax.dynamic_index_in_dim(mm_full, me, axis=1, keepdims=False)
    mm_c = mm_full[:, N_DEV]
    mods = [jnp.stack([mm_x[l].reshape(N_MOD, D_MODEL), mm_c[l].reshape(N_MOD, D_MODEL)]) for l in range(2)]
    gathered = gather_wait(0, mods[0])
    sw = gathered[11].reshape(N_CHIPS, _SMALL_W)
    ng_n = norm_g.size
    a2_n = w_a2_f.size
    norm_g_full = jnp.concatenate([sw[k, :ng_n].reshape(norm_g.shape) for k in range(N_CHIPS)], axis=-1)
    w_a2_f_full = jnp.concatenate([sw[k, ng_n:ng_n + a2_n].reshape(w_a2_f.shape[1:]) for k in range(N_CHIPS)], axis=-1)
    w_a2_b_full = jnp.concatenate(
        [sw[k, ng_n + a2_n:ng_n + 2 * a2_n].reshape(w_a2_b.shape[1:]) for k in range(N_CHIPS)], axis=-1)
    pscale_full = jnp.concatenate(
        [sw[k, ng_n + 2 * a2_n:ng_n + 2 * a2_n + pool_scale.size] for k in range(N_CHIPS)]).reshape(1, D_MODEL)
    wg2, bias2 = _gate_weights(w_a2_f_full, b_a_f[0], w_a2_b_full, b_a_b[0])
    gla_g2 = gla_g.reshape(1, B_DV)
    final_g2 = final_g.reshape(1, D_MODEL)

    g3 = [norm_g_full[0], norm_g_full[1]]

    w1i, w1o, w2i, w2o = [None, None], [None, None], [None, None], [None, None]
    w1i[0] = gathered[0]
    mods_a = mods[0] + gather_start((1, 2), w1i[0])[0:1, 0:1]
    x1, sv_a1, w1o[0] = _ffn_forward(xcat, g3[0], mods_a, 0, w1i[0],
                                     lambda s: (gather_wait(1, s)[2], gather_start((3, 4), s)), n_x, "l0_ffn1")
    gathered = gather_wait(2, x1)
    w_in_full = jnp.concatenate([gathered[8][k] for k in range(N_CHIPS)], axis=1)
    wcat = _w_in_to_cat(w_in_full)
    w_out_full = gathered[9].reshape(D_MODEL, D_MODEL)
    pace_groups = {"proj": (5, 6), "attn": (7, 8)}
    no_dep = jnp.zeros((8, 128), F32)
    x2, sv_am = _mixer_ab_forward(
        x1, g3[0], mods[0], wcat, wg2, bias2, sink[0], gla_g2, w_out_full, cs, t_len, l_ctx, n_x,
        lambda tag, res_: gather_start(pace_groups[tag], res_) if tag in pace_groups else no_dep)
    mods_a = mods[0]
    w2i[0], w2o[0] = gather_wait(3, x2)[4], gather_wait(4, x2)[6]
    x3, sv_a2, _ = _ffn_forward(x2, g3[0], mods_a, 2, w2i[0], lambda s: (w2o[0], None), n_x, "l0_ffn2")
    w1i[1], w1o[1] = gather_wait(5, x3)[1], gather_wait(6, x3)[3]
    x4, sv_b1, _ = _ffn_forward(x3, g3[1], mods[1], 0, w1i[1], lambda s: (w1o[1], None), n_x, "l1_ffn1")
    gathered = gather_wait(7, x4)
    w2i[1] = gathered[5]
    wp_full = gathered[10].reshape(N_CHIPS, n_grp, -1, POOL_GROUP).transpose(1, 0, 2, 3).reshape(
        n_grp, POOL_GROUP, POOL_GROUP)
    x5, sv_bm = _mixer_pool_forward(x4, g3[1], mods[1], wp_full, pscale_full, t_len)
    x6, sv_b2, w2o[1] = _ffn_forward(x5, g3[1], mods[1], 2, w2i[1], lambda s: (gather_wait(8, s)[7], None), n_x,
                                     "l1_ffn2")
    dx6, loss_part, d_final_g = _final_loss(x6, final_g2, loss_target[0], "final_loss")
    loss = lax.psum(loss_part[0, 0], ("x", "y", "c"))

    sent = []

    def sender(weight, layer):
        def send(grads):
            nm = "%s_%d" % (weight or "_".join(grads), layer)
            ssem, rsem, thru, lands, token = _scatter_start(list(grads.values()), "scatter_start_" + nm)
            targets = [((weight + "_" + tag) if weight else tag, layer) for tag in grads]
            sent.append((nm, targets, thru, lands, ssem, rsem))
            return token[0:1, 0:1]
        return send

    dx5, st_b2, dg_b2 = _ffn_backward(dx6, sv_b2, g3[1], mods[1], 2, w2i[1], w2o[1], n_x, sender("ffn2", 1),
                                      "l1_ffn2_b")
    dx4, st_bm, dg_bm, d_pscale, d_wp = _mixer_pool_backward(dx5, sv_bm, g3[1], mods[1], wp_full, pscale_full, t_len)
    d_wp4 = d_wp.reshape(n_grp, N_CHIPS, -1, POOL_GROUP).transpose(1, 0, 2, 3).reshape(N_CHIPS, -1, POOL_GROUP)
    mods1 = mods[1] + sender("", 0)({"w_pool": d_wp4})
    dx3, st_b1, dg_b1 = _ffn_backward(dx4, sv_b1, g3[1], mods1, 0, w1i[1], w1o[1], n_x, sender("ffn1", 1),
                                      "l1_ffn1_b")
    dx2, st_a2, dg_a2 = _ffn_backward(dx3, sv_a2, g3[0], mods[0], 2, w2i[0], w2o[0], n_x, sender("ffn2", 0),
                                      "l0_ffn2_b")
    dx1, st_am, dg_am, d_wcat, d_wg2, d_bias2, d_sink, d_glag, d_wout = _mixer_ab_backward(
        dx2, sv_am, g3[0], mods[0], wcat, wg2, bias2, sink[0], gla_g2, w_out_full, cs, t_len, l_ctx, n_x)
    d_w_in4 = _cat_to_w_in(d_wcat).reshape(D_MODEL, N_CHIPS, -1).transpose(1, 0, 2)
    mods0 = mods[0] + sender("", 0)({"w_in": d_w_in4, "w_out": d_wout.reshape(N_CHIPS, -1, D_MODEL)})
    dx0, st_a1, dg_a1 = _ffn_backward(dx1, sv_a1, g3[0], mods0, 0, w1i[0], w1o[0], n_x, sender("ffn1", 0),
                                      "l0_ffn1_b", out_tiles=n_x)
    grad_x = dx0[None]

    def as3(a):
        n_l = a.shape[0] if a.ndim == 3 else 1
        return a.reshape(n_l, -1, a.shape[-1])

    res = {}
    big_w = {"ffn1_wi": (ffn1_wi, m_ffn1_wi, v_ffn1_wi), "ffn1_wo": (ffn1_wo, m_ffn1_wo, v_ffn1_wo),
             "ffn2_wi": (ffn2_wi, m_ffn2_wi, v_ffn2_wi), "ffn2_wo": (ffn2_wo, m_ffn2_wo, v_ffn2_wo),
             "w_in": (w_in, m_w_in, v_w_in), "w_out": (w_out, m_w_out, v_w_out), "w_pool": (w_pool, m_w_pool, v_w_pool)}
    k_idx = k_me.reshape(1).astype(jnp.int32)
    chain = dx0
    def finish(swap, after):
        lo, targets, s_sem, r_sem, s_thru, s_lands = swap
        mine, other = _swap_wait(s_thru, s_lands, s_sem, r_sem, after, "swap_wait_%d" % lo)
        last = after
        for (wname, layer), p, q in zip(targets, mine, other):
            w, m, v = big_w[wname]
            res[wname] = _adamw(as3(w), [p, q], as3(m), as3(v), layer, res.get(wname),
                                "adamw_%s_%d" % (wname, layer))
            last = res[wname][3]
        return last

    swap = None
    for lo, hi in ((0, 1), (1, 3), (3, 4), (4, 5), (5, 6)):
        partial, targets = [], []
        for nm, sent_targets, thru, lands, ssem, rsem in sent[lo:hi]:
            mine, recv = _scatter_wait(thru, lands, ssem, rsem, chain, "scatter_wait_" + nm)
            for k, (wname, layer) in enumerate(sent_targets):
                partial.append(_partial_sum(mine[k], recv[k], k_idx, "partial_sum_%s_%d" % (wname, layer)))
                targets.append((wname, layer))
        s_sem, r_sem, s_thru, s_lands, token = _swap_start(partial, "swap_start_%d" % lo)
        if swap is not None:
            chain = finish(swap, token)
        swap = (lo, targets, s_sem, r_sem, s_thru, s_lands)

    def mod_row(st1, dg1, stm, dgm, st2, dg2, s):
        return jnp.concatenate([st1[s, 0], st1[s, 1], dg1[s, 0], stm[s, 0], stm[s, 1], dgm[s, 0],
                                st2[s, 0], st2[s, 1], dg2[s, 0]])

    dg_bm2 = jnp.concatenate([dg_bm, jnp.zeros_like(dg_bm)], axis=0)[:, None, :]
    d_mm_x0 = mod_row(st_a1, dg_a1, st_am, dg_am, st_a2, dg_a2, 0)
    d_mm_x1 = mod_row(st_b1, dg_b1, st_bm, dg_bm2, st_b2, dg_b2, 0)
    d_mm_c0 = mod_row(st_a1, dg_a1, st_am, dg_am, st_a2, dg_a2, 1)
    d_norm_g = jnp.stack([jnp.stack([st[0, 2] + st[1, 2] for st in (st_a1, st_am, st_a2)]),
                          jnp.stack([st[0, 2] + st[1, 2] for st in (st_b1, st_bm, st_b2)])])
    rk = B_GATE_RANK
    pack = _flat_pad([d_mm_x0, d_mm_x1, d_mm_c0, d_norm_g, d_bias2, d_wg2[0:rk, 0:256], d_wg2[rk:2 * rk, 256:512],
                      d_sink[:, 0], jnp.zeros((120,), F32), d_glag, d_pscale, d_final_g],
                     _PACK_ROWS * 128).reshape(_PACK_ROWS, 128)
    pack = pack + 0.0 * chain[0, 0:1, 0:1]
    pack_all = _allgather_small(pack, "gather_small_grads")
    tot = _sum_devices(pack_all, "sum_small_grads").reshape(-1)
    rows_all = pack_all.reshape(N_DEV, -1)
    o = 3 * _N9
    g_norm_g_full = tot[o:o + 6 * D_MODEL].reshape(2, 3, D_MODEL)
    o += 6 * D_MODEL
    g_bias2 = tot[o:o + 512]
    o += 512
    g_w_a2_f_full = tot[o:o + rk * 256].reshape(rk, 256)
    o += rk * 256
    g_w_a2_b_full = tot[o:o + rk * 256].reshape(rk, 256)
    o += rk * 256
    g_sink = tot[o:o + A_HEADS]
    o += 128
    g_gla_g = tot[o:o + B_DV]
    o += B_DV
    g_pscale_full = tot[o:o + D_MODEL]
    o += D_MODEL
    g_final_g = tot[o:o + D_MODEL]
    d_mmc_tot = tot[2 * _N9:3 * _N9]
    g_b_mod = jnp.stack([tot[0:_N9] + d_mmc_tot, tot[_N9:2 * _N9]])

    zrows = jnp.zeros((_CROWS - N_DEV - 1, _N9), F32)
    d16 = jnp.stack([jnp.concatenate([rows_all[:, 0:_N9], d_mmc_tot[None], zrows], axis=0),
                     jnp.concatenate([rows_all[:, _N9:2 * _N9], jnp.zeros((1, _N9), F32), zrows], axis=0)])
    d16_k = lax.dynamic_slice(d16, (0, 0, k_me * mod_cols), (2, _CROWS, mod_cols))
    dmmc_k = lax.dynamic_slice(d_mmc_tot, (k_me * mod_cols,), (mod_cols,)).reshape(1, mod_cols)
    g_w_mod, c_part = _adaln_bwd(c16, d16_k, w_mod, dmmc_k, "adaln_bwd")
    c_parts = _allgather_small(c_part.reshape(8, 128), "gather_cctx")
    g_c_ctx = _cctx_grad(c_parts, c_ctx.reshape(8, 128), "cctx_grad").reshape(D_MODEL)

    def small(w, g, m, v, shape3, nm):
        return [o_.reshape(w.shape) for o_ in _adamw(w.reshape(shape3), [g.reshape(shape3[1:])],
                                                    m.reshape(shape3), v.reshape(shape3), 0, None, "adamw_" + nm)]

    def own(a, axis, size):
        return lax.dynamic_slice_in_dim(a, k_me * size, size, axis=axis)

    res["c_ctx"] = small(c_ctx, g_c_ctx, m_c_ctx, v_c_ctx, (1, 8, 128), "c_ctx")
    upd = _adamw(w_mod, [(g_w_mod, 1)], m_w_mod, v_w_mod, 1, None, "adamw_w_mod_1")
    res["w_mod"] = _adamw(w_mod, [(g_w_mod, 0)], m_w_mod, v_w_mod, 0, upd, "adamw_w_mod_0")
    res["b_mod"] = small(b_mod, g_b_mod, m_b_mod, v_b_mod, (1, 2, _N9), "b_mod")
    res["norm_g"] = small(norm_g, own(g_norm_g_full, 2, norm_g.shape[2]), m_norm_g, v_norm_g,
                          (1, 6, norm_g.shape[2]), "norm_g")
    res["w_a2_f"] = small(w_a2_f, own(g_w_a2_f_full, 1, w_a2_f.shape[2]), m_w_a2_f, v_w_a2_f,
                          (1, rk, w_a2_f.shape[2]), "w_a2_f")
    res["b_a_f"] = small(b_a_f, g_bias2[0:256], m_b_a_f, v_b_a_f, (1, 1, 256), "b_a_f")
    res["w_a2_b"] = small(w_a2_b, own(g_w_a2_b_full, 1, w_a2_b.shape[2]), m_w_a2_b, v_w_a2_b,
                          (1, rk, w_a2_b.shape[2]), "w_a2_b")
    res["b_a_b"] = small(b_a_b, g_bias2[256:512], m_b_a_b, v_b_a_b, (1, 1, 256), "b_a_b")
    res["sink"] = small(sink, g_sink, m_sink, v_sink, (1, 1, A_HEADS), "sink")
    res["gla_g"] = small(gla_g, g_gla_g, m_gla_g, v_gla_g, (1, 1, B_DV), "gla_g")
    res["pool_scale"] = small(pool_scale, own(g_pscale_full, 0, pool_scale.shape[1]), m_pool_scale, v_pool_scale,
                              (1, 1, pool_scale.shape[1]), "pool_scale")
    res["final_g"] = small(final_g, g_final_g, m_final_g, v_final_g, (1, 8, 128), "final_g")
    finish(swap, res["final_g"][0])
    for wname, (w, _, _) in big_w.items():
        res[wname] = [o_.reshape(w.shape) for o_ in res[wname]]

    names = ["c_ctx", "w_mod", "b_mod", "norm_g", "ffn1_wi", "ffn1_wo", "ffn2_wi", "ffn2_wo", "w_in", "w_a2_f",
             "b_a_f", "w_a2_b", "b_a_b", "sink", "gla_g", "w_out", "w_pool", "pool_scale", "final_g"]
    outs = [loss, grad_x]
    for field in range(4):
        outs += [res[nm][field] for nm in names]
    return tuple(outs)
```

```python
import functools

import jax
import jax.numpy as jnp
import numpy as np
from jax import lax
from jax.experimental import pallas as pl
from jax.experimental.pallas import tpu as pltpu

F32 = jnp.float32
BF16 = jnp.bfloat16

D_MODEL = 1024
N_MOD = 9
D_FF = 2816
RMS_EPS = 1e-6
A_HEADS = 8
A_KV_HEADS = 2
A_HEAD_DIM = 64
WINDOW = 128
ROPE_BASE = 10000.0
GRID_W = 64
B_HEADS = 4
B_DK = 64
B_DV = 128
B_GATE_RANK = 16
B_GATE_NORM = 16.0
B_CHUNK = 64
POOL_WINDOWS = (2, 4, 8, 16)
POOL_GROUP = D_MODEL // len(POOL_WINDOWS)
PROJ_DIM = 2336

ADAM_LR = 0.001
ADAM_B1 = 0.9
ADAM_B2 = 0.999
ADAM_EPS = 1e-08
ADAM_WD = 0.01
ADAM_STEP = 10

N_CHIPS = 4
N_DEV = 8
ROW_TILE = 512
VMEM_LIMIT_BYTES = 56 * 1024 * 1024
MESH = pl.DeviceIdType.MESH

ZC_Q, ZC_QK, ZC_V, ZC_R, ZC_KV, ZC_G, ZC_W = 0, 512, 1024, 1536, 2048, 2304, 2432


def _cp(*sem):
    return pltpu.CompilerParams(dimension_semantics=sem if sem else None, vmem_limit_bytes=VMEM_LIMIT_BYTES)


def _dot(a, b):
    return jnp.dot(a, b, preferred_element_type=F32)


def _dot_nt(a, b):
    return lax.dot_general(a, b, (((1,), (1,)), ((), ())), preferred_element_type=F32)


def _dot_tn(a, b):
    return lax.dot_general(a, b, (((0,), (0,)), ((), ())), preferred_element_type=F32)


def _dot_tn_hi(a, b):
    return lax.dot_general(a, b, (((0,), (0,)), ((), ())), preferred_element_type=F32,
                           precision=lax.Precision.HIGHEST)


def _sigmoid(x):
    return 1.0 / (1.0 + jnp.exp(-x))


MXU_COLS = 256


def _col_chunks(n):
    return [(c0, min(MXU_COLS, n - c0)) for c0 in range(0, n, MXU_COLS)]


WIDE_ROW_TILE = 1024


def _matmul_row_tile(rows, n_x):
    if rows % WIDE_ROW_TILE == 0 and n_x * ROW_TILE >= rows:
        return WIDE_ROW_TILE
    return ROW_TILE


def _resident(block_shape, index_map):
    return pl.BlockSpec(block_shape, index_map, pipeline_mode=pl.Buffered(1))


def _stream_of(i, n_x):
    return jnp.where(i >= n_x, 1, 0)


def _rms_mod_fwd(x, g3, mods, j, n_x, out_dtype, name):
    rows = x.shape[0]
    tm = ROW_TILE
    n_i = rows // tm

    def body(x_ref, g_ref, m_ref, o_ref):
        xv = x_ref[...]
        r = lax.rsqrt(jnp.mean(xv * xv, axis=-1, keepdims=True) + RMS_EPS)
        g = g_ref[j:j + 1, :]
        shift = m_ref[0, 3 * j:3 * j + 1, :]
        scale = m_ref[0, 3 * j + 1:3 * j + 2, :]
        o_ref[...] = (((xv * r) * g) * (1.0 + scale) + shift).astype(out_dtype)

    return pl.pallas_call(
        body, name=name, grid=(n_i,),
        in_specs=[pl.BlockSpec((tm, D_MODEL), lambda i: (i, 0)),
                  pl.BlockSpec((3, D_MODEL), lambda i: (0, 0)),
                  pl.BlockSpec((1, N_MOD, D_MODEL), lambda i: (_stream_of(i, n_x), 0, 0))],
        out_specs=pl.BlockSpec((tm, D_MODEL), lambda i: (i, 0)),
        out_shape=jax.ShapeDtypeStruct((rows, D_MODEL), out_dtype),
        compiler_params=_cp("parallel"),
    )(x, g3, mods)


def _rms_mod_bwd_tail(dh, xv, g, scale, stream, acc_ref, first):
    r = lax.rsqrt(jnp.mean(xv * xv, axis=-1, keepdims=True) + RMS_EPS)
    xhat = xv * r
    t1 = jnp.sum(dh, axis=0, keepdims=True)
    t2 = jnp.sum(dh * xhat, axis=0, keepdims=True)
    stats = jnp.concatenate([t1, t2 * g, t2 * (1.0 + scale)], axis=0)

    @pl.when(first)
    def _():
        acc_ref[...] = jnp.zeros_like(acc_ref)

    acc_ref[pl.ds(stream, 1)] += stats[None]
    dxh = dh * (g * (1.0 + scale))
    return r * (dxh - xhat * jnp.mean(dxh * xhat, axis=-1, keepdims=True))


def _ffn_up(x, g3, mods, jmod, n_x, w4, name):
    rows = x.shape[0]
    h = w4.shape[2]
    tm = _matmul_row_tile(rows, n_x)
    n_i = rows // tm

    def body(x_ref, g_ref, m_ref, wa_ref, wu_ref, hn_ref, au_ref, s_ref):
        xv = x_ref[...]
        r = lax.rsqrt(jnp.mean(xv * xv, axis=-1, keepdims=True) + RMS_EPS)
        g = g_ref[jmod:jmod + 1, :]
        shift = m_ref[0, 3 * jmod:3 * jmod + 1, :]
        scale = m_ref[0, 3 * jmod + 1:3 * jmod + 2, :]
        hv = (((xv * r) * g) * (1.0 + scale) + shift).astype(BF16)

        @pl.when(pl.program_id(0) == 0)
        def _():
            hn_ref[...] = hv

        for c0, cw in _col_chunks(h):
            cols = slice(c0, c0 + cw)
            a = _dot(hv, wa_ref[0, :, cols])
            u = _dot(hv, wu_ref[0, :, cols])
            sg = _sigmoid(a)
            silu = a * sg
            au_ref[0, :, cols] = (u * (sg * (1.0 + a * (1.0 - sg)))).astype(BF16)
            au_ref[1, :, cols] = silu.astype(BF16)
            s_ref[:, cols] = (silu * u).astype(BF16)

    return pl.pallas_call(
        body, name=name, grid=(2, n_i),
        in_specs=[pl.BlockSpec((tm, D_MODEL), lambda j, i: (i, 0)),
                  pl.BlockSpec((3, D_MODEL), lambda j, i: (0, 0)),
                  pl.BlockSpec((1, N_MOD, D_MODEL), lambda j, i: (_stream_of(i, n_x), 0, 0)),
                  pl.BlockSpec((1, D_MODEL, h), lambda j, i: (j, 0, 0)),
                  pl.BlockSpec((1, D_MODEL, h), lambda j, i: (j + 2, 0, 0))],
        out_specs=[pl.BlockSpec((tm, D_MODEL), lambda j, i: (jnp.where(j == 0, i, n_i - 1), 0)),
                   pl.BlockSpec((2, tm, h), lambda j, i: (0, i, j)),
                   pl.BlockSpec((tm, h), lambda j, i: (i, j))],
        out_shape=[jax.ShapeDtypeStruct((rows, D_MODEL), BF16),
                   jax.ShapeDtypeStruct((2, rows, 2 * h), BF16),
                   jax.ShapeDtypeStruct((rows, 2 * h), BF16)],
        compiler_params=_cp("arbitrary", "arbitrary"),
    )(x, g3, mods, w4, w4)


def _matmul_resid(a, w, xres, mods, gate_idx, coef, n_x, rows, name):
    k = a.shape[1]
    tm = _matmul_row_tile(rows, n_x)
    n_i = rows // tm

    def body(a_ref, w_ref, x_ref, m_ref, o_ref, f_ref):
        av = a_ref[...]
        for c0, cw in _col_chunks(D_MODEL):
            cols = slice(c0, c0 + cw)
            f = _dot(av, w_ref[:, cols])
            f_ref[:, cols] = f.astype(BF16)
            o_ref[:, cols] = x_ref[:, cols] + (coef * m_ref[0, gate_idx:gate_idx + 1, cols]) * f

    return pl.pallas_call(
        body, name=name, grid=(n_i,),
        in_specs=[pl.BlockSpec((tm, k), lambda i: (i, 0)),
                  _resident((k, D_MODEL), lambda i: (0, 0)),
                  pl.BlockSpec((tm, D_MODEL), lambda i: (i, 0)),
                  pl.BlockSpec((1, N_MOD, D_MODEL), lambda i: (_stream_of(i, n_x), 0, 0))],
        out_specs=[pl.BlockSpec((tm, D_MODEL), lambda i: (i, 0)),
                   pl.BlockSpec((tm, D_MODEL), lambda i: (i, 0))],
        out_shape=[jax.ShapeDtypeStruct((rows, D_MODEL), F32),
                   jax.ShapeDtypeStruct((rows, D_MODEL), BF16)],
        compiler_params=_cp("parallel"),
    )(a, w, xres, mods)


def _gate_dy(dout, f, mods, gate_idx, coef, n_x, rows, w, name):
    tm = ROW_TILE
    n_i = rows // tm
    n_out = w.shape[0]

    def body(d_ref, f_ref, m_ref, w_ref, dy_ref, da_ref, acc_ref):
        i = pl.program_id(0)
        dv = d_ref[...]
        gate = m_ref[0, gate_idx:gate_idx + 1, :]
        dyb = (dv * (coef * gate)).astype(BF16)
        dy_ref[...] = dyb
        da_ref[...] = _dot_nt(dyb, w_ref[...])

        @pl.when(i == 0)
        def _():
            acc_ref[...] = jnp.zeros_like(acc_ref)

        part = coef * jnp.sum(dv * f_ref[...].astype(F32), axis=0, keepdims=True)
        acc_ref[pl.ds(_stream_of(i, n_x), 1)] += part[None]

    return pl.pallas_call(
        body, name=name, grid=(n_i,),
        in_specs=[pl.BlockSpec((tm, D_MODEL), lambda i: (i, 0)),
                  pl.BlockSpec((tm, D_MODEL), lambda i: (i, 0)),
                  pl.BlockSpec((1, N_MOD, D_MODEL), lambda i: (_stream_of(i, n_x), 0, 0)),
                  pl.BlockSpec((n_out, D_MODEL), lambda i: (0, 0))],
        out_specs=[pl.BlockSpec((tm, D_MODEL), lambda i: (i, 0)),
                   pl.BlockSpec((tm, n_out), lambda i: (i, 0)),
                   pl.BlockSpec((2, 1, D_MODEL), lambda i: (0, 0, 0))],
        out_shape=[jax.ShapeDtypeStruct((rows, D_MODEL), BF16),
                   jax.ShapeDtypeStruct((rows, n_out), F32),
                   jax.ShapeDtypeStruct((2, 1, D_MODEL), F32)],
        compiler_params=_cp("arbitrary"),
    )(dout, f, mods, w)


def _ffn_bwd_dz(dout, f, mods, gate_idx, coef, n_x, wo2, au, name):
    rows = dout.shape[0]
    h = wo2.shape[1]
    tm = ROW_TILE
    n_i = rows // tm

    def body(d_ref, f_ref, m_ref, wo_ref, au_ref, dy_ref, dz_ref, acc_ref):
        j, i = pl.program_id(0), pl.program_id(1)
        dv = d_ref[...]
        gate = m_ref[0, gate_idx:gate_idx + 1, :]
        dyb = (dv * (coef * gate)).astype(BF16)

        @pl.when((j == 0) & (i == 0))
        def _():
            acc_ref[...] = jnp.zeros_like(acc_ref)

        @pl.when(j == 0)
        def _():
            dy_ref[...] = dyb
            part = coef * jnp.sum(dv * f_ref[...].astype(F32), axis=0, keepdims=True)
            acc_ref[pl.ds(_stream_of(i, n_x), 1)] += part[None]

        for c0, cw in _col_chunks(h):
            cols = slice(c0, c0 + cw)
            ds = _dot_nt(dyb, wo_ref[0, cols, :])
            dz_ref[0, :, cols] = (ds * au_ref[0, :, cols].astype(F32)).astype(BF16)
            dz_ref[1, :, cols] = (ds * au_ref[1, :, cols].astype(F32)).astype(BF16)

    return pl.pallas_call(
        body, name=name, grid=(2, n_i),
        in_specs=[pl.BlockSpec((tm, D_MODEL), lambda j, i: (i, 0)),
                  pl.BlockSpec((tm, D_MODEL), lambda j, i: (jnp.where(j == 0, i, n_i - 1), 0)),
                  pl.BlockSpec((1, N_MOD, D_MODEL), lambda j, i: (_stream_of(i, n_x), 0, 0)),
                  pl.BlockSpec((1, h, D_MODEL), lambda j, i: (j, 0, 0)),
                  pl.BlockSpec((2, tm, h), lambda j, i: (0, i, j))],
        out_specs=[pl.BlockSpec((tm, D_MODEL), lambda j, i: (jnp.where(j == 0, i, n_i - 1), 0)),
                   pl.BlockSpec((2, tm, h), lambda j, i: (0, i, j)),
                   pl.BlockSpec((2, 1, D_MODEL), lambda j, i: (0, 0, 0))],
        out_shape=[jax.ShapeDtypeStruct((rows, D_MODEL), BF16),
                   jax.ShapeDtypeStruct((2, rows, 2 * h), BF16),
                   jax.ShapeDtypeStruct((2, 1, D_MODEL), F32)],
        compiler_params=_cp("arbitrary", "arbitrary"),
    )(dout, f, mods, wo2, au)


def _token_tile(rows):
    for tk in (2048, 1536, 1024):
        if rows % tk == 0:
            return tk
    return ROW_TILE


def _matmul_tn(a, b, a_spec, b_spec, out_shape, out_spec, grid, name):
    nd_a = len(a_spec.block_shape)
    nd_b = len(b_spec.block_shape)
    nd_o = len(out_spec.block_shape)
    k_axis = len(grid) - 1
    n_k = grid[k_axis]

    def body(a_ref, b_ref, o_ref, acc_ref):
        av = a_ref[(0,) * (nd_a - 2)]
        bv = b_ref[(0,) * (nd_b - 2)]
        part = _dot_tn(av, bv)
        k = pl.program_id(k_axis)

        @pl.when(k == 0)
        def _():
            acc_ref[...] = part

        @pl.when(k > 0)
        def _():
            acc_ref[...] += part

        @pl.when(k == n_k - 1)
        def _():
            o_ref[(0,) * (nd_o - 2)] = acc_ref[...].astype(BF16)

    return pl.pallas_call(
        body, name=name, grid=grid, in_specs=[a_spec, b_spec], out_specs=out_spec,
        out_shape=jax.ShapeDtypeStruct(out_shape, BF16),
        scratch_shapes=[pltpu.VMEM(tuple(out_spec.block_shape[-2:]), F32)],
        compiler_params=_cp(*(("arbitrary",) * len(grid))),
    )(a, b)


def _bwd_dx(pairs, x, dres, dres_tiles, g3, mods, j, n_x, name, out_tiles=None):
    rows = x.shape[0]
    tm = ROW_TILE
    n_i = rows // tm
    n_o = n_i if out_tiles is None else out_tiles
    n_p = len(pairs)
    nds = [(len(p[1].block_shape), len(p[3].block_shape)) for p in pairs]

    def body(*refs):
        dz_refs = refs[0:2 * n_p:2]
        w_refs = refs[1:2 * n_p:2]
        x_ref, dres_ref, g_ref, m_ref, dx_ref, acc_ref = refs[2 * n_p:]
        i = pl.program_id(0)
        dzs = [dz_refs[p][(0,) * (nds[p][0] - 2)] for p in range(n_p)]
        pieces = []
        for c0, cw in _col_chunks(D_MODEL):
            acc = None
            for p in range(n_p):
                lead = (0,) * (nds[p][1] - 2)
                part = _dot_nt(dzs[p], w_refs[p][lead + (slice(c0, c0 + cw), slice(None))])
                acc = part if acc is None else acc + part
            pieces.append(acc)
        dh = jnp.concatenate(pieces, axis=1)
        g = g_ref[j:j + 1, :]
        scale = m_ref[0, 3 * j + 1:3 * j + 2, :]
        dx = _rms_mod_bwd_tail(dh, x_ref[...], g, scale, _stream_of(i, n_x), acc_ref, i == 0)
        dres_v = jnp.where(i < dres_tiles, dres_ref[...], 0.0)

        @pl.when(i < n_o)
        def _():
            dx_ref[...] = dres_v + dx

    in_specs, args = [], []
    for dz, dz_spec, w, w_spec in pairs:
        in_specs += [dz_spec, w_spec]
        args += [dz, w]
    in_specs += [pl.BlockSpec((tm, D_MODEL), lambda i: (i, 0)),
                 pl.BlockSpec((tm, D_MODEL), lambda i: (jnp.minimum(i, dres_tiles - 1), 0)),
                 pl.BlockSpec((3, D_MODEL), lambda i: (0, 0)),
                 pl.BlockSpec((1, N_MOD, D_MODEL), lambda i: (_stream_of(i, n_x), 0, 0))]
    args += [x, dres, g3, mods]
    return pl.pallas_call(
        body, name=name, grid=(n_i,), in_specs=in_specs,
        out_specs=[pl.BlockSpec((tm, D_MODEL), lambda i: (jnp.minimum(i, n_o - 1), 0)),
                   pl.BlockSpec((2, 3, D_MODEL), lambda i: (0, 0, 0))],
        out_shape=[jax.ShapeDtypeStruct((n_o * tm, D_MODEL), F32),
                   jax.ShapeDtypeStruct((2, 3, D_MODEL), F32)],
        compiler_params=_cp("arbitrary"),
    )(*args)


def _ffn_forward(x, g3, mods, j, w4_in, w4_out_of, n_x, name):
    rows = x.shape[0]
    hn, au, s = _ffn_up(x, g3, mods, j, n_x, w4_in, name + "_up")
    w4_out, dep = w4_out_of(s)
    if dep is not None:
        mods = mods + dep[0:1, 0:1]
    wo = w4_out.reshape(D_FF, D_MODEL)
    out, f = _matmul_resid(s, wo, x, mods, 3 * j + 2, 0.5, n_x, rows, name + "_down")
    return out, (x, hn, au, s, f), w4_out


def _ffn_backward(dout, saved, g3, mods, j, w4_in, w4_out, n_x, send, name, out_tiles=None):
    x, hn, au, s, f = saved
    rows = x.shape[0]
    tm = ROW_TILE
    n_i = rows // tm
    h = w4_in.shape[2]
    wo2 = w4_out.reshape(2, h, D_MODEL)
    dy, dz, dgate = _ffn_bwd_dz(dout, f, mods, 3 * j + 2, 0.5, n_x, wo2, au, name + "_dz")
    tk = _token_tile(rows)
    n_k = rows // tk
    d_wi = _matmul_tn(
        hn, dz, pl.BlockSpec((tk, D_MODEL), lambda q, k: (k, 0)),
        pl.BlockSpec((1, tk, h), lambda q, k: (q // 2, k, q % 2)),
        (4, D_MODEL, h), pl.BlockSpec((1, D_MODEL, h), lambda q, k: (q, 0, 0)), (4, n_k), name + "_dwi")
    d_wo = _matmul_tn(
        s, dy, pl.BlockSpec((tk, h), lambda n, k: (k, n)), pl.BlockSpec((tk, D_MODEL), lambda n, k: (k, 0)),
        (D_FF, D_MODEL), pl.BlockSpec((h, D_MODEL), lambda n, k: (n, 0)), (2, n_k), name + "_dwo")
    mods = mods + send({"wi": d_wi, "wo": d_wo.reshape(w4_out.shape)})
    pairs = [(dz, pl.BlockSpec((1, tm, h), functools.partial(lambda q, i: (q // 2, i, q % 2), q)),
              w4_in, pl.BlockSpec((1, D_MODEL, h), functools.partial(lambda q, i: (q, 0, 0), q)))
             for q in range(4)]
    dx, stats = _bwd_dx(pairs, x, dout, n_i, g3, mods, j, n_x, name + "_dx", out_tiles)
    return dx, stats, dgate


def _rope_tables(t_len, rows):
    n = A_HEAD_DIM // 4
    freqs = np.float32(ROPE_BASE) ** (-np.arange(n, dtype=np.float32) / np.float32(n))
    t = np.arange(t_len)
    ang_r = (t // GRID_W).astype(np.float32)[:, None] * freqs
    ang_c = (t % GRID_W).astype(np.float32)[:, None] * freqs
    cos = np.concatenate([np.cos(ang_r), np.cos(ang_r), np.cos(ang_c), np.cos(ang_c)], axis=1)
    sin = np.concatenate([-np.sin(ang_r), np.sin(ang_r), -np.sin(ang_c), np.sin(ang_c)], axis=1)
    cos = np.concatenate([cos, np.ones((rows - t_len, A_HEAD_DIM), np.float32)], axis=0)
    sin = np.concatenate([sin, np.zeros((rows - t_len, A_HEAD_DIM), np.float32)], axis=0)
    return jnp.asarray(np.concatenate([cos, cos, sin, sin], axis=1).astype(np.float32))


def _swap16(x):
    n = x.shape[1]
    lane = lax.broadcasted_iota(jnp.int32, x.shape, 1)
    first = jnp.bitwise_and(lane, 16) == 0
    return jnp.where(first, pltpu.roll(x, n - 16, 1), pltpu.roll(x, 16, 1))


def _log_sigmoid(x):
    return jnp.minimum(x, 0.0) - jnp.log(1.0 + jnp.exp(-jnp.abs(x)))


def _proj_fwd(x, g3, mods, n_x, wcat, wg2, bias2, cs, name):
    rows = x.shape[0]
    tm = ROW_TILE

    def body(x_ref, g_ref, m_ref, w_ref, wg_ref, b_ref, cs_ref, h_ref, zc_ref, la_ref):
        xv = x_ref[...]
        r = lax.rsqrt(jnp.mean(xv * xv, axis=-1, keepdims=True) + RMS_EPS)
        hv = (((xv * r) * g_ref[1:2, :]) * (1.0 + m_ref[0, 4:5, :]) + m_ref[0, 3:4, :]).astype(BF16)
        h_ref[...] = hv
        z = _dot(hv, w_ref[...])
        cos = cs_ref[:, 0:128]
        sin = cs_ref[:, 128:256]
        cosq = jnp.concatenate([cos] * 4, axis=1)
        sinq = jnp.concatenate([sin] * 4, axis=1)
        q = z[:, ZC_Q:ZC_QK]
        zc_ref[:, ZC_Q:ZC_QK] = q * cosq + _swap16(q) * sinq
        zc_ref[:, ZC_QK:ZC_KV] = z[:, ZC_QK:ZC_KV]
        kk = z[:, ZC_KV:ZC_KV + 128]
        zc_ref[:, ZC_KV:ZC_KV + 128] = kk * cos + _swap16(kk) * sin
        zc_ref[:, ZC_KV + 128:ZC_W] = z[:, ZC_KV + 128:ZC_W]
        zg = z[:, ZC_G:ZC_W]
        pre = _dot(zg.astype(BF16), wg_ref[...]) + b_ref[...]
        la_ref[...] = _log_sigmoid(pre) / B_GATE_NORM

    return pl.pallas_call(
        body, name=name, grid=(rows // tm,),
        in_specs=[pl.BlockSpec((tm, D_MODEL), lambda i: (i, 0)),
                  pl.BlockSpec((3, D_MODEL), lambda i: (0, 0)),
                  pl.BlockSpec((1, N_MOD, D_MODEL), lambda i: (_stream_of(i, n_x), 0, 0)),
                  pl.BlockSpec((D_MODEL, ZC_W), lambda i: (0, 0)),
                  pl.BlockSpec((128, 512), lambda i: (0, 0)),
                  pl.BlockSpec((1, 512), lambda i: (0, 0)),
                  pl.BlockSpec((tm, 256), lambda i: (i, 0))],
        out_specs=[pl.BlockSpec((tm, D_MODEL), lambda i: (i, 0)),
                   pl.BlockSpec((tm, ZC_W), lambda i: (i, 0)),
                   pl.BlockSpec((tm, 512), lambda i: (i, 0))],
        out_shape=[jax.ShapeDtypeStruct((rows, D_MODEL), BF16),
                   jax.ShapeDtypeStruct((rows, ZC_W), F32),
                   jax.ShapeDtypeStruct((rows, 512), F32)],
        compiler_params=_cp("parallel"),
    )(x, g3, mods, wcat, wg2, bias2, cs)


_QB = WINDOW


def _attn_specs(t_len, l_ctx):
    nb = t_len // _QB
    kvb = ZC_KV // 256
    return [pl.BlockSpec(memory_space=pltpu.SMEM),
            pl.BlockSpec((_QB, 512), lambda n: (n, 0)),
            pl.BlockSpec((_QB, 256), lambda n: (jnp.maximum(n - 1, 0), kvb)),
            pl.BlockSpec((_QB, 256), lambda n: (n, kvb)),
            pl.BlockSpec((_QB, 256), lambda n: (n + 1, kvb)),
            pl.BlockSpec((l_ctx, 256), lambda n: (t_len // l_ctx, kvb))], nb


_HEAD_PAIRS = ((0, 1), (2, 3))


def _attn_keys(kp, kc, kn, kx, g):
    hd = A_HEAD_DIM
    ks = slice(g * hd, (g + 1) * hd)
    vs = slice(128 + g * hd, 128 + (g + 1) * hd)
    kb = jnp.concatenate([kp[:, ks], kc[:, ks], kn[:, ks]], axis=0).astype(BF16)
    vb = jnp.concatenate([kp[:, vs], kc[:, vs], kn[:, vs]], axis=0).astype(BF16)
    return kb, vb, kx[:, ks].astype(BF16), kx[:, vs].astype(BF16)


def _attn_probs(n, t_len, sink_ref, qv, kb, kxb, g, rs):
    hd = A_HEAD_DIM
    qg = jnp.concatenate([qv[:, (4 * g + r) * hd:(4 * g + r + 1) * hd] for r in rs], axis=0).astype(BF16)
    qi = lax.broadcasted_iota(jnp.int32, (_QB, 3 * _QB), 0)
    kj = lax.broadcasted_iota(jnp.int32, (_QB, 3 * _QB), 1)
    kpos = n * _QB - _QB + kj
    valid = (kpos >= 0) & (kpos < t_len) & (jnp.abs(kj - _QB - qi) <= WINDOW)
    valid = jnp.concatenate([valid] * len(rs), axis=0)
    scale = hd ** -0.5
    s = jnp.where(valid, _dot_nt(qg, kb) * scale, -jnp.inf)
    sc = _dot_nt(qg, kxb) * scale
    sk = jnp.concatenate([jnp.full((_QB, 1), sink_ref[4 * g + r], F32) for r in rs], axis=0)
    m = jnp.maximum(jnp.maximum(jnp.max(s, axis=-1, keepdims=True), jnp.max(sc, axis=-1, keepdims=True)), sk)
    p = jnp.exp(s - m)
    pc = jnp.exp(sc - m)
    ps = jnp.exp(sk - m)
    inv = 1.0 / (jnp.sum(p, axis=-1, keepdims=True) + jnp.sum(pc, axis=-1, keepdims=True) + ps)
    return p, pc, ps, inv, qg


def _attn_fwd(zc, sink, t_len, l_ctx, name):
    in_specs, nb = _attn_specs(t_len, l_ctx)

    def body(sink_ref, q_ref, kp_ref, kc_ref, kn_ref, kx_ref, o_ref):
        n = pl.program_id(0)
        qv = q_ref[...]
        outs = []
        for g in range(A_KV_HEADS):
            kb, vb, kxb, vxb = _attn_keys(kp_ref[...], kc_ref[...], kn_ref[...], kx_ref[...], g)
            for rs in _HEAD_PAIRS:
                p, pc, _, inv, _ = _attn_probs(n, t_len, sink_ref, qv, kb, kxb, g, rs)
                o = (_dot(p.astype(BF16), vb) + _dot(pc.astype(BF16), vxb)) * inv
                outs += [o[i * _QB:(i + 1) * _QB] for i in range(len(rs))]
        o_ref[...] = jnp.concatenate(outs, axis=1)

    return pl.pallas_call(
        body, name=name, grid=(nb,), in_specs=in_specs,
        out_specs=pl.BlockSpec((_QB, 512), lambda n: (n, 0)),
        out_shape=jax.ShapeDtypeStruct((t_len, 512), F32),
        compiler_params=_cp("parallel"),
    )(sink, zc, zc, zc, zc, zc)


def _attn_bwd(zc, sink, o, dcat, t_len, l_ctx, name):
    rows = zc.shape[0]
    in_specs, nb = _attn_specs(t_len, l_ctx)
    in_specs = in_specs + [pl.BlockSpec((_QB, 512), lambda n: (n, 0)), pl.BlockSpec((_QB, 512), lambda n: (n, 0))]
    hd = A_HEAD_DIM
    scale = hd ** -0.5

    def body(sink_ref, q_ref, kp_ref, kc_ref, kn_ref, kx_ref, o_ref, do_ref, dq_ref, dkv_ref, dsink_ref):
        n = pl.program_id(0)

        @pl.when(n == 0)
        def _():
            dkv_ref[...] = jnp.zeros_like(dkv_ref)
            dsink_ref[...] = jnp.zeros_like(dsink_ref)

        qv = q_ref[...]
        ov = o_ref[...]
        dov = do_ref[...]
        dqs, dkbs, dvbs, dkxs, dvxs, dsinks = [], [], [], [], [], []
        for g in range(A_KV_HEADS):
            kb, vb, kxb, vxb = _attn_keys(kp_ref[...], kc_ref[...], kn_ref[...], kx_ref[...], g)
            parts = []
            for rs in _HEAD_PAIRS:
                p, pc, ps, inv, qg = _attn_probs(n, t_len, sink_ref, qv, kb, kxb, g, rs)
                og = jnp.concatenate([ov[:, (4 * g + r) * hd:(4 * g + r + 1) * hd] for r in rs], axis=0)
                dog = jnp.concatenate([dov[:, (4 * g + r) * hd:(4 * g + r + 1) * hd] for r in rs], axis=0)
                delta = jnp.sum(og * dog, axis=-1, keepdims=True)
                dogb = dog.astype(BF16)
                pn = p * inv
                pcn = pc * inv
                ds = (pn * (_dot_nt(dogb, vb) - delta) * scale).astype(BF16)
                dsc = (pcn * (_dot_nt(dogb, vxb) - delta) * scale).astype(BF16)
                dsk = (ps * inv) * (0.0 - delta)
                dqg = _dot(ds, kb) + _dot(dsc, kxb)
                dqs += [dqg[i * _QB:(i + 1) * _QB] for i in range(len(rs))]
                parts.append((_dot_tn(ds, qg), _dot_tn(pn.astype(BF16), dogb),
                              _dot_tn(dsc, qg), _dot_tn(pcn.astype(BF16), dogb)))
                for i in range(len(rs)):
                    tot = jnp.sum(dsk[i * _QB:(i + 1) * _QB], axis=0, keepdims=True)
                    dsinks.append(jnp.broadcast_to(tot, (1, 128)))
            dkbs.append(parts[0][0] + parts[1][0])
            dvbs.append(parts[0][1] + parts[1][1])
            dkxs.append(parts[0][2] + parts[1][2])
            dvxs.append(parts[0][3] + parts[1][3])
        dsink_ref[...] += jnp.concatenate(dsinks, axis=0)
        dq_ref[...] = jnp.concatenate(dqs, axis=1)
        band = jnp.concatenate(dkbs + dvbs, axis=1)
        ctxc = jnp.concatenate(dkxs + dvxs, axis=1)
        r_prev = pl.multiple_of(jnp.maximum(n - 1, 0) * _QB, _QB)
        r_cur = pl.multiple_of(n * _QB, _QB)
        r_next = pl.multiple_of((n + 1) * _QB, _QB)
        dkv_ref[pl.ds(r_prev, _QB), :] += band[0:_QB]
        dkv_ref[pl.ds(r_cur, _QB), :] += band[_QB:2 * _QB]
        dkv_ref[pl.ds(r_next, _QB), :] += band[2 * _QB:3 * _QB]
        dkv_ref[t_len:t_len + l_ctx, :] += ctxc

    return pl.pallas_call(
        body, name=name, grid=(nb,), in_specs=in_specs,
        out_specs=[pl.BlockSpec((_QB, 512), lambda n: (n, 0)),
                   pl.BlockSpec((rows, 256), lambda n: (0, 0)),
                   pl.BlockSpec((8, 128), lambda n: (0, 0))],
        out_shape=[jax.ShapeDtypeStruct((t_len, 512), F32),
                   jax.ShapeDtypeStruct((rows, 256), F32),
                   jax.ShapeDtypeStruct((8, 128), F32)],
        compiler_params=_cp("arbitrary"),
    )(sink, zc, zc, zc, zc, zc, o, dcat)


_GC = B_CHUNK


def _split_bf16(a):
    hi = a.astype(BF16)
    return hi, (a - hi.astype(F32)).astype(BF16)


def _gla_chunk_terms(qk, la, reverse):
    q = qk[:, 0:256]
    k = qk[:, 256:512]
    off = 256 if reverse else 0
    lad = la[:, off:off + 256]
    ii = lax.broadcasted_iota(jnp.int32, (_GC, _GC), 0)
    jj = lax.broadcasted_iota(jnp.int32, (_GC, _GC), 1)
    mask = (jj >= ii) if reverse else (jj <= ii)
    tri = jnp.where(mask, 1.0, 0.0).astype(BF16)
    la_hi, la_lo = _split_bf16(lad)
    g = _dot(tri, la_hi) + _dot(tri, la_lo)
    gl = jnp.sum(lad, axis=0, keepdims=True)
    eg = jnp.exp(g)
    eng = jnp.exp(-g)
    eend = jnp.exp(gl - g)
    sc = B_DK ** -0.5
    qt = q * (sc * eg)
    kt = k * eng
    ke = k * eend
    return mask, tri, gl, eg, eng, eend, qt, kt, ke


def _same_head(rows, cols, row_shift, col_shift):
    r = jnp.right_shift(lax.broadcasted_iota(jnp.int32, (rows, cols), 0), row_shift)
    c = jnp.right_shift(lax.broadcasted_iota(jnp.int32, (rows, cols), 1), col_shift)
    return r == c


def _block_diag_rows(x, col_shift):
    tiled = jnp.concatenate([x] * B_HEADS, axis=0)
    return jnp.where(_same_head(tiled.shape[0], tiled.shape[1], 6, col_shift), tiled, jnp.zeros_like(tiled))


def _fold_heads(x):
    c = x.shape[0] // B_HEADS
    return (x[0:c] + x[c:2 * c]) + (x[2 * c:3 * c] + x[3 * c:4 * c])


def _chunk_mask4(reverse):
    ii = lax.broadcasted_iota(jnp.int32, (_GC, B_HEADS * _GC), 0)
    jj = jnp.bitwise_and(lax.broadcasted_iota(jnp.int32, (_GC, B_HEADS * _GC), 1), _GC - 1)
    return (jj >= ii) if reverse else (jj <= ii)


_ST_SHAPE = (B_HEADS * B_DV, B_HEADS * B_DK)


def _state_blocks(t):
    return [t[hh * B_DV:(hh + 1) * B_DV, hh * B_DK:(hh + 1) * B_DK] for hh in range(B_HEADS)]


def _state_from_blocks(blocks):
    full = jnp.concatenate([jnp.concatenate([b] * B_HEADS, axis=1) for b in blocks], axis=0)
    return jnp.where(_same_head(_ST_SHAPE[0], _ST_SHAPE[1], 7, 6), full, 0.0)


def _gla_fwd(zc, la, dep, t_len, l_ctx, name):
    rows = zc.shape[0]
    n_x = t_len // _GC
    n_c = n_x + l_ctx // _GC
    qkb, vb = ZC_QK // 512, ZC_V // 512

    def ch_f(c):
        return lax.rem(c + n_x, n_c)

    def ch_r(c):
        return n_c - 1 - c

    def body(qkf_ref, vf_ref, laf_ref, qkr_ref, vr_ref, lar_ref, dep_ref, of_ref, or_ref, spf_ref, spr_ref, stf, strv):
        del dep_ref
        c = pl.program_id(0)

        @pl.when(c == 0)
        def _():
            stf[...] = jnp.zeros_like(stf)
            strv[...] = jnp.zeros_like(strv)

        results = []
        for qk_ref, v_ref, la_ref, st, reverse in ((qkf_ref, vf_ref, laf_ref, stf, False),
                                                   (qkr_ref, vr_ref, lar_ref, strv, True)):
            mask, _, gl, _, _, _, qt, kt, ke = _gla_chunk_terms(qk_ref[...], la_ref[...], reverse)
            vbf = v_ref[...].astype(BF16)
            qtb, keb = qt.astype(BF16), ke.astype(BF16)
            kbd = _block_diag_rows(kt.astype(BF16), 6)
            vbd = _block_diag_rows(vbf, 7)
            mask4 = _chunk_mask4(reverse)
            t_prev = st[...]
            att = jnp.where(mask4, _dot_nt(qtb, kbd), 0.0).astype(BF16)
            o_all = _dot(att, vbd) + _dot_nt(qtb, t_prev.astype(BF16))
            t_new = t_prev * jnp.exp(gl) + jnp.where(_same_head(_ST_SHAPE[0], _ST_SHAPE[1], 7, 6),
                                                     _dot_tn(vbf, keb), 0.0)
            results.append((o_all, t_prev, t_new))
        for (o_all, t_prev, t_new), o_ref, sp_ref, st in zip(results, (of_ref, or_ref), (spf_ref, spr_ref), (stf, strv)):
            o_ref[...] = o_all
            for hh, blk in enumerate(_state_blocks(t_prev)):
                sp_ref[0, hh] = blk
            st[...] = t_new

    st_shape = (B_HEADS, B_DV, B_DK)
    return pl.pallas_call(
        body, name=name, grid=(n_c,),
        in_specs=[pl.BlockSpec((_GC, 512), lambda c: (ch_f(c), qkb)),
                  pl.BlockSpec((_GC, 512), lambda c: (ch_f(c), vb)),
                  pl.BlockSpec((_GC, 512), lambda c: (ch_f(c), 0)),
                  pl.BlockSpec((_GC, 512), lambda c: (ch_r(c), qkb)),
                  pl.BlockSpec((_GC, 512), lambda c: (ch_r(c), vb)),
                  pl.BlockSpec((_GC, 512), lambda c: (ch_r(c), 0)),
                  pl.BlockSpec((8, 128), lambda c: (0, 0))],
        out_specs=[pl.BlockSpec((_GC, 512), lambda c: (ch_f(c), 0)),
                   pl.BlockSpec((_GC, 512), lambda c: (ch_r(c), 0)),
                   pl.BlockSpec((1,) + st_shape, lambda c: (c, 0, 0, 0)),
                   pl.BlockSpec((1,) + st_shape, lambda c: (c, 0, 0, 0))],
        out_shape=[jax.ShapeDtypeStruct((rows, 512), F32), jax.ShapeDtypeStruct((rows, 512), F32),
                   jax.ShapeDtypeStruct((n_c,) + st_shape, F32), jax.ShapeDtypeStruct((n_c,) + st_shape, F32)],
        scratch_shapes=[pltpu.VMEM(_ST_SHAPE, F32), pltpu.VMEM(_ST_SHAPE, F32)],
        compiler_params=_cp("arbitrary"),
    )(zc, zc, la, zc, zc, la, dep)


def _gla_bwd(zc, la, spf, spr, dosum, t_len, l_ctx, name):
    rows = zc.shape[0]
    n_x = t_len // _GC
    n_c = n_x + l_ctx // _GC
    n_all = rows // _GC
    qkb, vb = ZC_QK // 512, ZC_V // 512

    def scan_of(c):
        return jnp.maximum(n_c - 1 - c, 0)

    def ch_f(c):
        return jnp.where(c < n_c, lax.rem(scan_of(c) + n_x, n_c), c)

    def ch_r(c):
        return c

    def do_of(ch):
        return jnp.minimum(ch, n_x - 1)

    def body(qkf_ref, vf_ref, laf_ref, spf_ref, dof_ref, qkr_ref, vr_ref, lar_ref, spr_ref, dor_ref,
             dqkf_ref, dvf_ref, dlaf_ref, dqkr_ref, dvr_ref, dlar_ref, dsf, dsr):
        c = pl.program_id(0)

        @pl.when(c == 0)
        def _():
            dsf[...] = jnp.zeros_like(dsf)
            dsr[...] = jnp.zeros_like(dsr)

        @pl.when(c >= n_c)
        def _():
            for r in (dqkf_ref, dvf_ref, dlaf_ref, dqkr_ref, dvr_ref, dlar_ref):
                r[...] = jnp.zeros_like(r)

        @pl.when(c < n_c)
        def _():
            sc = B_DK ** -0.5
            results = []
            for qk_ref, v_ref, la_ref, sp_ref, do_ref, dst, reverse, ch in (
                    (qkf_ref, vf_ref, laf_ref, spf_ref, dof_ref, dsf, False, ch_f(c)),
                    (qkr_ref, vr_ref, lar_ref, spr_ref, dor_ref, dsr, True, ch_r(c))):
                mask, tri, gl, eg, eng, eend, qt, kt, ke = _gla_chunk_terms(qk_ref[...], la_ref[...], reverse)
                vbf = v_ref[...].astype(BF16)
                dob = jnp.where(ch < n_x, do_ref[...], 0.0).astype(BF16)
                qtb, keb = qt.astype(BF16), ke.astype(BF16)
                kbd = _block_diag_rows(kt.astype(BF16), 6)
                vbd = _block_diag_rows(vbf, 7)
                mask4 = _chunk_mask4(reverse)
                egl = jnp.exp(gl)
                t_prev = _state_from_blocks([sp_ref[0, hh] for hh in range(B_HEADS)])
                dt_new = dst[...]
                tpb, dtb = t_prev.astype(BF16), dt_new.astype(BF16)
                att = jnp.where(mask4, _dot_nt(qtb, kbd), 0.0).astype(BF16)
                datt = jnp.where(mask4, _dot_nt(dob, vbd), 0.0).astype(BF16)
                dqt = _dot(datt, kbd) + _dot(dob, tpb)
                dkt = _fold_heads(jnp.where(_same_head(256, 256, 6, 6), _dot_tn(datt, qtb), 0.0))
                dv = _fold_heads(jnp.where(_same_head(256, 512, 6, 7), _dot_tn(att, dob), 0.0)) + _dot_nt(keb, dtb)
                dke = _dot(vbf, dtb)
                dt_prev = dt_new * egl + jnp.where(_same_head(_ST_SHAPE[0], _ST_SHAPE[1], 7, 6),
                                                   _dot_tn(dob, qtb), 0.0)
                dgl = (jnp.sum(dke * ke, axis=0, keepdims=True)
                       + jnp.sum(dt_new * t_prev, axis=0, keepdims=True) * egl)
                dg_hi, dg_lo = _split_bf16(dqt * qt - dkt * kt - dke * ke)
                dla = _dot_tn(tri, dg_hi) + _dot_tn(tri, dg_lo) + dgl
                dqk = jnp.concatenate([dqt * (sc * eg), dkt * eng + dke * eend], axis=1)
                results.append((dqk, dv, dla, dt_prev))
            for (dqk, dv, dla, dt_prev), dqk_ref, dv_ref, dla_ref, dst in zip(
                    results, (dqkf_ref, dqkr_ref), (dvf_ref, dvr_ref), (dlaf_ref, dlar_ref), (dsf, dsr)):
                dqk_ref[...] = dqk
                dv_ref[...] = dv
                dla_ref[...] = dla
                dst[...] = dt_prev

    st_shape = (B_HEADS, B_DV, B_DK)

    def side(chf):
        return [pl.BlockSpec((_GC, 512), lambda c: (chf(c), qkb)),
                pl.BlockSpec((_GC, 512), lambda c: (chf(c), vb)),
                pl.BlockSpec((_GC, 512), lambda c: (chf(c), 0)),
                pl.BlockSpec((1,) + st_shape, lambda c: (scan_of(c), 0, 0, 0)),
                pl.BlockSpec((_GC, 512), lambda c: (do_of(chf(c)), 0))]

    def out_side(chf):
        return [pl.BlockSpec((_GC, 512), lambda c: (chf(c), 0)),
                pl.BlockSpec((_GC, 512), lambda c: (chf(c), 0)),
                pl.BlockSpec((_GC, 256), lambda c: (chf(c), 0))]

    shp = [jax.ShapeDtypeStruct((rows, 512), F32), jax.ShapeDtypeStruct((rows, 512), F32),
           jax.ShapeDtypeStruct((rows, 256), F32)]
    return pl.pallas_call(
        body, name=name, grid=(n_all,),
        in_specs=side(ch_f) + side(ch_r),
        out_specs=out_side(ch_f) + out_side(ch_r),
        out_shape=shp + shp,
        scratch_shapes=[pltpu.VMEM(_ST_SHAPE, F32), pltpu.VMEM(_ST_SHAPE, F32)],
        compiler_params=_cp("arbitrary"),
    )(zc, zc, la, spf, dosum, zc, zc, la, spr, dosum)


def _gla_out_fwd(o_a, o_f, o_r, zc, gla_g, t_len, name):
    tm = ROW_TILE
    rb = ZC_R // 512

    def body(oa_ref, of_ref, or_ref, r_ref, g_ref, cat_ref):
        osum = of_ref[...] + or_ref[...]
        g = g_ref[...]
        pieces = []
        for hh in range(B_HEADS):
            oh = osum[:, hh * B_DV:(hh + 1) * B_DV]
            rs = lax.rsqrt(jnp.mean(oh * oh, axis=-1, keepdims=True) + RMS_EPS)
            pieces.append((oh * rs) * g)
        r = r_ref[...]
        cat_ref[:, 0:512] = oa_ref[...].astype(BF16)
        cat_ref[:, 512:1024] = (jnp.concatenate(pieces, axis=1) * (r * _sigmoid(r))).astype(BF16)

    return pl.pallas_call(
        body, name=name, grid=(t_len // tm,),
        in_specs=[pl.BlockSpec((tm, 512), lambda i: (i, 0)),
                  pl.BlockSpec((tm, 512), lambda i: (i, 0)),
                  pl.BlockSpec((tm, 512), lambda i: (i, 0)),
                  pl.BlockSpec((tm, 512), lambda i: (i, rb)),
                  pl.BlockSpec((1, B_DV), lambda i: (0, 0))],
        out_specs=pl.BlockSpec((tm, D_MODEL), lambda i: (i, 0)),
        out_shape=jax.ShapeDtypeStruct((t_len, D_MODEL), BF16),
        compiler_params=_cp("parallel"),
    )(o_a, o_f, o_r, zc, gla_g)


def _gla_out_bwd(dcat, o_f, o_r, zc, gla_g, t_len, name):
    tm = ROW_TILE
    rb = ZC_R // 512

    def body(d_ref, of_ref, or_ref, r_ref, g_ref, dos_ref, dr_ref, dg_ref):
        i = pl.program_id(0)
        osum = of_ref[...] + or_ref[...]
        g = g_ref[...]
        r = r_ref[...]
        dgo = d_ref[...]
        sg = _sigmoid(r)
        dnrmg = dgo * (r * sg)
        nrms, dos = [], []
        dg_acc = jnp.zeros((1, B_DV), F32)
        for hh in range(B_HEADS):
            oh = osum[:, hh * B_DV:(hh + 1) * B_DV]
            rs = lax.rsqrt(jnp.mean(oh * oh, axis=-1, keepdims=True) + RMS_EPS)
            nrm = oh * rs
            dn = dnrmg[:, hh * B_DV:(hh + 1) * B_DV]
            dg_acc = dg_acc + jnp.sum(dn * nrm, axis=0, keepdims=True)
            dnn = dn * g
            dos.append(rs * (dnn - nrm * jnp.mean(dnn * nrm, axis=-1, keepdims=True)))
            nrms.append(nrm * g)
        dos_ref[...] = jnp.concatenate(dos, axis=1)
        dr_ref[...] = dgo * jnp.concatenate(nrms, axis=1) * (sg * (1.0 + r * (1.0 - sg)))

        @pl.when(i == 0)
        def _():
            dg_ref[...] = jnp.zeros_like(dg_ref)

        dg_ref[...] += dg_acc

    return pl.pallas_call(
        body, name=name, grid=(t_len // tm,),
        in_specs=[pl.BlockSpec((tm, 512), lambda i: (i, 1)),
                  pl.BlockSpec((tm, 512), lambda i: (i, 0)),
                  pl.BlockSpec((tm, 512), lambda i: (i, 0)),
                  pl.BlockSpec((tm, 512), lambda i: (i, rb)),
                  pl.BlockSpec((1, B_DV), lambda i: (0, 0))],
        out_specs=[pl.BlockSpec((tm, 512), lambda i: (i, 0)),
                   pl.BlockSpec((tm, 512), lambda i: (i, 0)),
                   pl.BlockSpec((1, B_DV), lambda i: (0, 0))],
        out_shape=[jax.ShapeDtypeStruct((t_len, 512), F32), jax.ShapeDtypeStruct((t_len, 512), F32),
                   jax.ShapeDtypeStruct((1, B_DV), F32)],
        compiler_params=_cp("arbitrary"),
    )(dcat, o_f, o_r, zc, gla_g)


def _mix_prep(dq, dkv, dqk_f, dqk_r, dv_f, dv_r, d_r, dla_f, dla_r, zc, wg2, bias2, cs, t_len, name):
    rows = zc.shape[0]
    tm = ROW_TILE
    n_x = t_len // tm
    gb = ZC_G // 128

    def xrow(i):
        return jnp.minimum(i, n_x - 1)

    def body(dq_ref, dkv_ref, dqkf_ref, dqkr_ref, dvf_ref, dvr_ref, dr_ref, dlaf_ref, dlar_ref, zg_ref, wg_ref,
             b_ref, cs_ref, dz_ref, dwg_ref, db_ref):
        i = pl.program_id(0)
        is_x = i < n_x
        cos = cs_ref[:, 0:128]
        sin = cs_ref[:, 128:256]
        cosq = jnp.concatenate([cos] * 4, axis=1)
        sinq = jnp.concatenate([sin] * 4, axis=1)
        dqv = jnp.where(is_x, dq_ref[...], 0.0)
        dz_ref[:, ZC_Q:ZC_QK] = (dqv * cosq + _swap16(dqv * sinq)).astype(BF16)
        dz_ref[:, ZC_QK:ZC_V] = (dqkf_ref[...] + dqkr_ref[...]).astype(BF16)
        dz_ref[:, ZC_V:ZC_R] = (dvf_ref[...] + dvr_ref[...]).astype(BF16)
        dz_ref[:, ZC_R:ZC_KV] = jnp.where(is_x, dr_ref[...], 0.0).astype(BF16)
        dk = dkv_ref[:, 0:128]
        dz_ref[:, ZC_KV:ZC_KV + 128] = (dk * cos + _swap16(dk * sin)).astype(BF16)
        dz_ref[:, ZC_KV + 128:ZC_G] = dkv_ref[:, 128:256].astype(BF16)
        zgb = zg_ref[...].astype(BF16)
        wg = wg_ref[...]
        pre = _dot(zgb, wg) + b_ref[...]
        dla = jnp.concatenate([dlaf_ref[...], dlar_ref[...]], axis=1)
        dpre = dla * (_sigmoid(-pre) / B_GATE_NORM)
        dpb = dpre.astype(BF16)
        dz_ref[:, ZC_G:ZC_W] = _dot_nt(dpb, wg).astype(BF16)

        @pl.when(i == 0)
        def _():
            dwg_ref[...] = jnp.zeros_like(dwg_ref)
            db_ref[...] = jnp.zeros_like(db_ref)

        dwg_ref[...] += _dot_tn(zgb, dpb)
        db_ref[...] += jnp.sum(dpre, axis=0, keepdims=True)

    return pl.pallas_call(
        body, name=name, grid=(rows // tm,),
        in_specs=[pl.BlockSpec((tm, 512), lambda i: (xrow(i), 0)),
                  pl.BlockSpec((tm, 256), lambda i: (i, 0)),
                  pl.BlockSpec((tm, 512), lambda i: (i, 0)),
                  pl.BlockSpec((tm, 512), lambda i: (i, 0)),
                  pl.BlockSpec((tm, 512), lambda i: (i, 0)),
                  pl.BlockSpec((tm, 512), lambda i: (i, 0)),
                  pl.BlockSpec((tm, 512), lambda i: (xrow(i), 0)),
                  pl.BlockSpec((tm, 256), lambda i: (i, 0)),
                  pl.BlockSpec((tm, 256), lambda i: (i, 0)),
                  pl.BlockSpec((tm, 128), lambda i: (i, gb)),
                  pl.BlockSpec((128, 512), lambda i: (0, 0)),
                  pl.BlockSpec((1, 512), lambda i: (0, 0)),
                  pl.BlockSpec((tm, 256), lambda i: (i, 0))],
        out_specs=[pl.BlockSpec((tm, ZC_W), lambda i: (i, 0)),
                   pl.BlockSpec((128, 512), lambda i: (0, 0)),
                   pl.BlockSpec((1, 512), lambda i: (0, 0))],
        out_shape=[jax.ShapeDtypeStruct((rows, ZC_W), BF16),
                   jax.ShapeDtypeStruct((128, 512), F32),
                   jax.ShapeDtypeStruct((1, 512), F32)],
        compiler_params=_cp("arbitrary"),
    )(dq, dkv, dqk_f, dqk_r, dv_f, dv_r, d_r, dla_f, dla_r, zc, wg2, bias2, cs)


def _gate_weights(w_a2_f, b_a_f, w_a2_b, b_a_b):
    wg2 = jnp.zeros((128, 512), F32)
    wg2 = wg2.at[0:B_GATE_RANK, 0:256].set(w_a2_f).at[B_GATE_RANK:2 * B_GATE_RANK, 256:512].set(w_a2_b)
    bias2 = jnp.concatenate([b_a_f, b_a_b]).reshape(1, 512)
    return wg2.astype(BF16), bias2


_WIN_PERM = ((0, 512), (768, 1280), (1280, 1792), (1792, 2304), (512, 768), (2304, 2336))


def _w_in_to_cat(w_in_full):
    parts = [w_in_full[:, a:b] for a, b in _WIN_PERM]
    parts.append(jnp.zeros((w_in_full.shape[0], ZC_W - PROJ_DIM), w_in_full.dtype))
    return jnp.concatenate(parts, axis=1)


def _cat_to_w_in(d_wcat):
    return jnp.concatenate([d_wcat[:, ZC_Q:ZC_QK], d_wcat[:, ZC_KV:ZC_G], d_wcat[:, ZC_QK:ZC_KV],
                            d_wcat[:, ZC_G:ZC_G + 2 * B_GATE_RANK]], axis=1)


def _mixer_ab_forward(x1, g3, mods, wcat, wg2, bias2, sink, gla_g, w_out, cs, t_len, l_ctx, n_x, pace):
    h, zc, la = _proj_fwd(x1, g3, mods, n_x, wcat, wg2, bias2, cs, "mix0_proj")
    dep = pace("proj", zc)
    o_a = _attn_fwd(zc, sink + dep[0, 0], t_len, l_ctx, "mix0_attn")
    dep = pace("attn", o_a)
    o_f, o_r, spf, spr = _gla_fwd(zc, la, dep, t_len, l_ctx, "mix0_gla")
    dep = pace("gla", o_f)
    cat = _gla_out_fwd(o_a, o_f, o_r, zc, gla_g + dep[0:1, 0:1], t_len, "mix0_glaout")
    x2, y = _matmul_resid(cat, w_out, x1, mods, 5, 1.0, n_x, t_len, "mix0_out")
    return x2, (x1, h, zc, la, o_a, o_f, o_r, spf, spr, cat, y)


def _mixer_ab_backward(dx2, saved, g3, mods, wcat, wg2, bias2, sink, gla_g, w_out, cs, t_len, l_ctx, n_x):
    x1, h, zc, la, o_a, o_f, o_r, spf, spr, cat, y = saved
    rows = x1.shape[0]
    tm = ROW_TILE
    dy, dcat, dgate = _gate_dy(dx2, y, mods, 5, 1.0, n_x, t_len, w_out, "mix0_dy")
    tk = _token_tile(t_len)
    d_wout = _matmul_tn(
        cat, dy, pl.BlockSpec((tk, D_MODEL), lambda n, k: (k, 0)), pl.BlockSpec((tk, D_MODEL), lambda n, k: (k, 0)),
        (D_MODEL, D_MODEL), pl.BlockSpec((D_MODEL, D_MODEL), lambda n, k: (0, 0)), (1, t_len // tk), "mix0_dwout")
    dos, d_r, d_glag = _gla_out_bwd(dcat, o_f, o_r, zc, gla_g, t_len, "mix0_dglaout")
    dqk_f, dv_f, dla_f, dqk_r, dv_r, dla_r = _gla_bwd(zc, la, spf, spr, dos, t_len, l_ctx, "mix0_dgla")
    dq, dkv, dsink = _attn_bwd(zc, sink, o_a, dcat, t_len, l_ctx, "mix0_dattn")
    dzc, dwg2, dbias2 = _mix_prep(dq, dkv, dqk_f, dqk_r, dv_f, dv_r, d_r, dla_f, dla_r, zc, wg2, bias2, cs, t_len,
                                  "mix0_prep")
    tk = _token_tile(rows)
    d_wcat = _matmul_tn(
        h, dzc, pl.BlockSpec((tk, D_MODEL), lambda n, k: (k, 0)), pl.BlockSpec((tk, ZC_W), lambda n, k: (k, 0)),
        (D_MODEL, ZC_W), pl.BlockSpec((D_MODEL, ZC_W), lambda n, k: (0, 0)), (1, rows // tk), "mix0_dwin")
    pairs = [(dzc, pl.BlockSpec((tm, ZC_W), lambda i: (i, 0)), wcat, pl.BlockSpec((D_MODEL, ZC_W), lambda i: (0, 0)))]
    dx1, stats = _bwd_dx(pairs, x1, dx2, t_len // tm, g3, mods, 1, n_x, "mix0_dx")
    return dx1, stats, dgate, d_wcat, dwg2, dbias2, dsink, d_glag, d_wout


_PT = 256
_PH = 16


def _pool_window(n, t_len, w, transpose):
    shape = (_PT, _PT + 2 * _PH)
    a = n * _PT + lax.broadcasted_iota(jnp.int32, shape, 0)
    b = n * _PT - _PH + lax.broadcasted_iota(jnp.int32, shape, 1)
    t, s = (b, a) if transpose else (a, b)
    lo = jnp.maximum(t - w // 2, 0)
    hi = jnp.minimum(t + (w - w // 2), t_len)
    inside = (s >= lo) & (s < hi) & (t >= 0) & (t < t_len)
    return jnp.where(inside, 1.0, 0.0).astype(BF16)


def _pool_inv_count(first, count, t_len, w):
    t = first + lax.broadcasted_iota(jnp.int32, (count, 1), 0)
    lo = jnp.maximum(t - w // 2, 0)
    hi = jnp.minimum(t + (w - w // 2), t_len)
    return jnp.where((t >= 0) & (t < t_len), 1.0 / jnp.maximum(hi - lo, 1).astype(F32), 0.0)


def _window_sum(win, vals):
    hi, lo = _split_bf16(vals)
    return _dot(win, hi) + _dot(win, lo)


def _pool_halo(p_ref, c_ref, n_ref):
    return jnp.concatenate([p_ref[_PT - _PH:_PT, :], c_ref[...], n_ref[0:_PH, :]], axis=0)


def _pool_specs(t_len):
    nb = t_len // _PT
    return [pl.BlockSpec((_PT, D_MODEL), lambda n: (jnp.maximum(n - 1, 0), 0)),
            pl.BlockSpec((_PT, D_MODEL), lambda n: (n, 0)),
            pl.BlockSpec((_PT, D_MODEL), lambda n: (jnp.minimum(n + 1, nb - 1), 0))], nb


def _pool_fwd(h, wp, pscale, x1, mods, t_len, name):
    halo_specs, nb = _pool_specs(t_len)

    def body(hp_ref, hc_ref, hn_ref, w_ref, ps_ref, x_ref, m_ref, x2_ref, pooled_ref, ypre_ref):
        n = pl.program_id(0)
        hcat = _pool_halo(hp_ref, hc_ref, hn_ref)
        ys = []
        for gi, w in enumerate(POOL_WINDOWS):
            cols = slice(gi * POOL_GROUP, (gi + 1) * POOL_GROUP)
            hg = hcat[:, cols]
            mean = _window_sum(_pool_window(n, t_len, w, False), hg) * _pool_inv_count(n * _PT, _PT, t_len, w)
            pooled = (mean - hg[_PH:_PH + _PT]).astype(BF16)
            pooled_ref[:, cols] = pooled
            ys.append(_dot(pooled, w_ref[gi]))
        ypre = jnp.concatenate(ys, axis=1)
        ypre_ref[...] = ypre
        x2_ref[...] = x_ref[...] + m_ref[0, 5:6, :] * (ypre * ps_ref[...])

    return pl.pallas_call(
        body, name=name, grid=(nb,),
        in_specs=halo_specs + [pl.BlockSpec((4, POOL_GROUP, POOL_GROUP), lambda n: (0, 0, 0)),
                               pl.BlockSpec((1, D_MODEL), lambda n: (0, 0)),
                               pl.BlockSpec((_PT, D_MODEL), lambda n: (n, 0)),
                               pl.BlockSpec((1, N_MOD, D_MODEL), lambda n: (0, 0, 0))],
        out_specs=[pl.BlockSpec((_PT, D_MODEL), lambda n: (n, 0))] * 3,
        out_shape=[jax.ShapeDtypeStruct((t_len, D_MODEL), F32), jax.ShapeDtypeStruct((t_len, D_MODEL), BF16),
                   jax.ShapeDtypeStruct((t_len, D_MODEL), F32)],
        compiler_params=_cp("parallel"),
    )(h, h, h, wp, pscale, x1, mods)


def _pool_bwd_a(dx2, ypre, wp, pscale, mods, t_len, name):
    nb = t_len // _PT

    def body(d_ref, y_ref, w_ref, ps_ref, m_ref, dyp_ref, dpl_ref, dgate_ref, dps_ref):
        n = pl.program_id(0)
        dv = d_ref[...]
        ypre = y_ref[...]
        ps = ps_ref[...]
        dy = dv * m_ref[0, 5:6, :]
        dyp = (dy * ps).astype(BF16)
        dyp_ref[...] = dyp
        for gi in range(len(POOL_WINDOWS)):
            cols = slice(gi * POOL_GROUP, (gi + 1) * POOL_GROUP)
            dpl_ref[:, cols] = _dot_nt(dyp[:, cols], w_ref[gi])

        @pl.when(n == 0)
        def _():
            dgate_ref[...] = jnp.zeros_like(dgate_ref)
            dps_ref[...] = jnp.zeros_like(dps_ref)

        dgate_ref[...] += jnp.sum(dv * (ypre * ps), axis=0, keepdims=True)
        dps_ref[...] += jnp.sum(dy * ypre, axis=0, keepdims=True)

    return pl.pallas_call(
        body, name=name, grid=(nb,),
        in_specs=[pl.BlockSpec((_PT, D_MODEL), lambda n: (n, 0)),
                  pl.BlockSpec((_PT, D_MODEL), lambda n: (n, 0)),
                  pl.BlockSpec((4, POOL_GROUP, POOL_GROUP), lambda n: (0, 0, 0)),
                  pl.BlockSpec((1, D_MODEL), lambda n: (0, 0)),
                  pl.BlockSpec((1, N_MOD, D_MODEL), lambda n: (0, 0, 0))],
        out_specs=[pl.BlockSpec((_PT, D_MODEL), lambda n: (n, 0)),
                   pl.BlockSpec((_PT, D_MODEL), lambda n: (n, 0)),
                   pl.BlockSpec((1, D_MODEL), lambda n: (0, 0)),
                   pl.BlockSpec((1, D_MODEL), lambda n: (0, 0))],
        out_shape=[jax.ShapeDtypeStruct((t_len, D_MODEL), BF16), jax.ShapeDtypeStruct((t_len, D_MODEL), F32),
                   jax.ShapeDtypeStruct((1, D_MODEL), F32), jax.ShapeDtypeStruct((1, D_MODEL), F32)],
        compiler_params=_cp("arbitrary"),
    )(dx2, ypre, wp, pscale, mods)


def _pool_bwd_dx(dpl, x1, dx2, g3, mods, t_len, name):
    halo_specs, nb = _pool_specs(t_len)

    def body(dp_ref, dc_ref, dn_ref, x_ref, d_ref, g_ref, m_ref, dx_ref, acc_ref):
        n = pl.program_id(0)
        dcat = _pool_halo(dp_ref, dc_ref, dn_ref)
        dhs = []
        for gi, w in enumerate(POOL_WINDOWS):
            cols = slice(gi * POOL_GROUP, (gi + 1) * POOL_GROUP)
            dg = dcat[:, cols]
            scaled = dg * _pool_inv_count(n * _PT - _PH, _PT + 2 * _PH, t_len, w)
            dhs.append(_window_sum(_pool_window(n, t_len, w, True), scaled) - dg[_PH:_PH + _PT])
        dh = jnp.concatenate(dhs, axis=1)
        g = g_ref[1:2, :]
        scale = m_ref[0, 4:5, :]
        dx = _rms_mod_bwd_tail(dh, x_ref[...], g, scale, 0, acc_ref, n == 0)
        dx_ref[...] = d_ref[...] + dx

    return pl.pallas_call(
        body, name=name, grid=(nb,),
        in_specs=halo_specs + [pl.BlockSpec((_PT, D_MODEL), lambda n: (n, 0)),
                               pl.BlockSpec((_PT, D_MODEL), lambda n: (n, 0)),
                               pl.BlockSpec((3, D_MODEL), lambda n: (0, 0)),
                               pl.BlockSpec((1, N_MOD, D_MODEL), lambda n: (0, 0, 0))],
        out_specs=[pl.BlockSpec((_PT, D_MODEL), lambda n: (n, 0)),
                   pl.BlockSpec((2, 3, D_MODEL), lambda n: (0, 0, 0))],
        out_shape=[jax.ShapeDtypeStruct((t_len, D_MODEL), F32), jax.ShapeDtypeStruct((2, 3, D_MODEL), F32)],
        compiler_params=_cp("arbitrary"),
    )(dpl, dpl, dpl, x1, dx2, g3, mods)


def _mixer_pool_forward(x1, g3, mods, wp, pscale, t_len):
    h = _rms_mod_fwd(x1, g3, mods, 1, t_len // ROW_TILE, F32, "mix1_mod")
    x2, pooled, ypre = _pool_fwd(h, wp, pscale, x1, mods, t_len, "mix1_pool")
    return x2, (x1, pooled, ypre)


def _mixer_pool_backward(dx2, saved, g3, mods, wp, pscale, t_len):
    x1, pooled, ypre = saved
    tm = ROW_TILE
    dyp, dpl, dgate, dps = _pool_bwd_a(dx2, ypre, wp, pscale, mods, t_len, "mix1_da")
    d_wp = _matmul_tn(
        pooled, dyp, pl.BlockSpec((tm, POOL_GROUP), lambda g, k: (k, g)),
        pl.BlockSpec((tm, POOL_GROUP), lambda g, k: (k, g)),
        (4, POOL_GROUP, POOL_GROUP), pl.BlockSpec((1, POOL_GROUP, POOL_GROUP), lambda g, k: (g, 0, 0)),
        (4, t_len // tm), "mix1_dwp")
    dx1, stats = _pool_bwd_dx(dpl, x1, dx2, g3, mods, t_len, "mix1_dx")
    return dx1, stats, dgate, dps, d_wp


def _final_loss(x3, final_g, target, name):
    t_len = x3.shape[0]
    tm = ROW_TILE

    def body(x_ref, g_ref, t_ref, dx_ref, loss_ref, dg_ref):
        i = pl.program_id(0)
        xv = x_ref[...]
        g = g_ref[...]
        r = lax.rsqrt(jnp.mean(xv * xv, axis=-1, keepdims=True) + RMS_EPS)
        xhat = xv * r
        err = xhat * g - t_ref[...]
        part = 0.5 * jnp.sum(jnp.mean(err * err, axis=-1, keepdims=True), axis=0, keepdims=True)
        dy = err * (1.0 / D_MODEL)

        @pl.when(i == 0)
        def _():
            loss_ref[...] = jnp.zeros_like(loss_ref)
            dg_ref[...] = jnp.zeros_like(dg_ref)

        loss_ref[...] += jnp.broadcast_to(part, (1, 128))
        dg_ref[...] += jnp.sum(dy * xhat, axis=0, keepdims=True)
        dxh = dy * g
        dx_ref[...] = r * (dxh - xhat * jnp.mean(dxh * xhat, axis=-1, keepdims=True))

    return pl.pallas_call(
        body, name=name, grid=(t_len // tm,),
        in_specs=[pl.BlockSpec((tm, D_MODEL), lambda i: (i, 0)),
                  pl.BlockSpec((1, D_MODEL), lambda i: (0, 0)),
                  pl.BlockSpec((tm, D_MODEL), lambda i: (i, 0))],
        out_specs=[pl.BlockSpec((tm, D_MODEL), lambda i: (i, 0)),
                   pl.BlockSpec((1, 128), lambda i: (0, 0)),
                   pl.BlockSpec((1, D_MODEL), lambda i: (0, 0))],
        out_shape=[jax.ShapeDtypeStruct((t_len, D_MODEL), F32), jax.ShapeDtypeStruct((1, 128), F32),
                   jax.ShapeDtypeStruct((1, D_MODEL), F32)],
        compiler_params=_cp("arbitrary"),
    )(x3, final_g, target)


_CROWS = 16


def _adaln_fwd(c16, w_mod, bias_k, name):
    n_l, _, cols = w_mod.shape

    def body(c_ref, w_ref, b_ref, o_ref):
        cv = c_ref[...]
        sc = (cv * _sigmoid(cv)).astype(BF16)
        o_ref[0] = _dot(sc, w_ref[0].astype(BF16)) + b_ref[0]

    return pl.pallas_call(
        body, name=name, grid=(n_l,),
        in_specs=[pl.BlockSpec((_CROWS, D_MODEL), lambda l: (0, 0)),
                  pl.BlockSpec((1, D_MODEL, cols), lambda l: (l, 0, 0)),
                  pl.BlockSpec((1, 1, cols), lambda l: (l, 0, 0))],
        out_specs=pl.BlockSpec((1, _CROWS, cols), lambda l: (l, 0, 0)),
        out_shape=jax.ShapeDtypeStruct((n_l, _CROWS, cols), F32),
        compiler_params=_cp("parallel"),
    )(c16, w_mod, bias_k)


def _adaln_bwd(c16, d16, w_mod, dmmc_k, name):
    n_l, _, cols = w_mod.shape

    def body(c_ref, d_ref, w_ref, dm_ref, gw_ref, cp_ref):
        layer = pl.program_id(0)
        cv = c_ref[...]
        gw_ref[0] = _dot_tn_hi(cv * _sigmoid(cv), d_ref[0])

        @pl.when(layer == 0)
        def _():
            cp_ref[...] = jnp.sum(w_ref[0] * dm_ref[...], axis=1, keepdims=True)

    return pl.pallas_call(
        body, name=name, grid=(n_l,),
        in_specs=[pl.BlockSpec((_CROWS, D_MODEL), lambda l: (0, 0)),
                  pl.BlockSpec((1, _CROWS, cols), lambda l: (l, 0, 0)),
                  pl.BlockSpec((1, D_MODEL, cols), lambda l: (0, 0, 0)),
                  pl.BlockSpec((1, cols), lambda l: (0, 0))],
        out_specs=[pl.BlockSpec((1, D_MODEL, cols), lambda l: (l, 0, 0)),
                   pl.BlockSpec((D_MODEL, 1), lambda l: (0, 0))],
        out_shape=[jax.ShapeDtypeStruct((n_l, D_MODEL, cols), F32), jax.ShapeDtypeStruct((D_MODEL, 1), F32)],
        compiler_params=_cp("arbitrary"),
    )(c16, d16, w_mod, dmmc_k)


def _cctx_grad(cparts, c_ctx2, name):
    def body(p_ref, c_ref, o_ref):
        tot = ((p_ref[0] + p_ref[2]) + p_ref[4]) + p_ref[6]
        cv = c_ref[...]
        sg = _sigmoid(cv)
        o_ref[...] = tot * (sg * (1.0 + cv * (1.0 - sg)))

    return pl.pallas_call(
        body, name=name, out_shape=jax.ShapeDtypeStruct((8, 128), F32),
        in_specs=[pl.BlockSpec(memory_space=pltpu.VMEM), pl.BlockSpec(memory_space=pltpu.VMEM)],
        out_specs=pl.BlockSpec(memory_space=pltpu.VMEM),
    )(cparts, c_ctx2)


def _sum_devices(ga, name):
    def body(g_ref, o_ref):
        acc = g_ref[0]
        for d in range(1, N_DEV):
            acc = acc + g_ref[d]
        o_ref[...] = acc

    return pl.pallas_call(
        body, name=name, out_shape=jax.ShapeDtypeStruct(ga.shape[1:], F32),
        in_specs=[pl.BlockSpec(memory_space=pltpu.VMEM)], out_specs=pl.BlockSpec(memory_space=pltpu.VMEM),
    )(ga)


def _place():
    return lax.axis_index("x"), lax.axis_index("y"), lax.axis_index("c")


def _flip(a, d):
    return 1 - a if d else a


_CHIP_FLIPS = ((1, 0), (0, 1), (1, 1))


def _allgather_small(v, name, after=()):
    r, cc = v.shape

    def body(v_ref, *rest):
        out_ref, send_sems, recv_sems, local_sem = rest[-4:]
        x, y, c = _place()
        me = 4 * x + 2 * y + c
        mine = pltpu.make_async_copy(v_ref, out_ref.at[me], local_sem)
        mine.start()
        sends = []
        for k in range(1, N_DEV):
            peer = (_flip(x, (k >> 2) & 1), _flip(y, (k >> 1) & 1), _flip(c, k & 1))
            cp = pltpu.make_async_remote_copy(src_ref=v_ref, dst_ref=out_ref.at[me], send_sem=send_sems.at[k - 1],
                                              recv_sem=recv_sems.at[k - 1], device_id=peer, device_id_type=MESH)
            cp.start()
            sends.append(cp)
        for k in range(1, N_DEV):
            px, py, pc = _flip(x, (k >> 2) & 1), _flip(y, (k >> 1) & 1), _flip(c, k & 1)
            pltpu.make_async_remote_copy(src_ref=v_ref, dst_ref=out_ref.at[4 * px + 2 * py + pc],
                                         send_sem=send_sems.at[k - 1], recv_sem=recv_sems.at[k - 1],
                                         device_id=(px, py, pc), device_id_type=MESH).wait_recv()
        for cp in sends:
            cp.wait_send()
        mine.wait()

    return pl.pallas_call(
        body, name=name, out_shape=jax.ShapeDtypeStruct((N_DEV, r, cc), F32),
        in_specs=[pl.BlockSpec(memory_space=pltpu.VMEM)] + [pl.BlockSpec(memory_space=pl.ANY)] * len(after),
        out_specs=pl.BlockSpec(memory_space=pltpu.VMEM),
        scratch_shapes=[pltpu.SemaphoreType.DMA((N_DEV - 1,)), pltpu.SemaphoreType.DMA((N_DEV - 1,)),
                        pltpu.SemaphoreType.DMA],
        compiler_params=pltpu.CompilerParams(vmem_limit_bytes=VMEM_LIMIT_BYTES),
    )(v, *after)


_HBM_SPEC = pl.BlockSpec(memory_space=pltpu.HBM)
_SEM_SPEC = pl.BlockSpec(memory_space=pltpu.SEMAPHORE)
_EFFECT = pltpu.SideEffectType.DATAFLOW_SIDE_EFFECTING


def _in_hbm(a):
    return pltpu.with_memory_space_constraint(a, pltpu.HBM)


def _my_half(ref, c):
    h = ref.shape[0] // 2
    return ref.at[pl.ds(pl.multiple_of(c * h, 16), h)]


def _gather_start(arrs, groups, after, name, halves=False):
    n, n_g = len(arrs), len(groups)

    def body(*refs):
        ins, zones = refs[:n], refs[n:2 * n]
        sems = refs[2 * n + 1:2 * n + 1 + 2 * n_g]
        token = refs[2 * n + 1 + 2 * n_g + 2 * n]
        x, y, c = _place()
        k_me = 2 * x + y
        for g, members in enumerate(groups):
            for t, a in enumerate(members):
                for j, (dx, dy) in enumerate(_CHIP_FLIPS):
                    src, dst = ins[a], zones[a].at[k_me]
                    if halves:
                        src, dst = _my_half(src, c), _my_half(dst, c)
                    pltpu.make_async_remote_copy(
                        src_ref=src, dst_ref=dst, send_sem=sems[2 * g].at[3 * t + j],
                        recv_sem=sems[2 * g + 1].at[3 * t + j], device_id=(_flip(x, dx), _flip(y, dy), c),
                        device_id_type=MESH).start()
        token[...] = jnp.zeros_like(token)

    k_own = 2 * lax.axis_index("x") + lax.axis_index("y")
    zones = [lax.dynamic_update_slice(lax.empty((N_CHIPS,) + a.shape, a.dtype), a[None], (k_own,) + (0,) * a.ndim)
             for a in arrs]
    sem_shapes = []
    for members in groups:
        sem_shapes += [pltpu.SemaphoreType.DMA((3 * len(members),))] * 2
    outs = pl.pallas_call(
        body, name=name,
        out_shape=sem_shapes + [pltpu.HBM(a.shape, a.dtype) for a in arrs]
        + [pltpu.HBM(z.shape, z.dtype) for z in zones] + [jax.ShapeDtypeStruct((8, 128), F32)],
        in_specs=[_HBM_SPEC] * (2 * n) + [pl.BlockSpec(memory_space=pl.ANY)],
        out_specs=[_SEM_SPEC] * (2 * n_g) + [_HBM_SPEC] * (2 * n) + [pl.BlockSpec(memory_space=pltpu.VMEM)],
        input_output_aliases={i: 2 * n_g + i for i in range(2 * n)},
        compiler_params=pltpu.CompilerParams(has_side_effects=_EFFECT),
    )(*[_in_hbm(a) for a in arrs], *[_in_hbm(z) for z in zones], after)
    sems = outs[:2 * n_g]
    thru = outs[2 * n_g:2 * n_g + n]
    zones = outs[2 * n_g + n:2 * n_g + 2 * n]
    return [(sems[2 * g], sems[2 * g + 1]) for g in range(n_g)], thru, zones, outs[-1]


def _gather_wait(shards, zones, send_sems, recv_sems, after, name, halves=False):
    m = len(shards)

    def body(*refs):
        ins, zs = refs[:m], refs[m:2 * m]
        ssem, rsem = refs[2 * m], refs[2 * m + 1]
        x, y, c = _place()
        for t in range(m):
            for j, (dx, dy) in enumerate(_CHIP_FLIPS):
                px, py = _flip(x, dx), _flip(y, dy)
                src, dst = ins[t], zs[t].at[2 * px + py]
                if halves:
                    src, dst = _my_half(src, c), _my_half(dst, c)
                cp = pltpu.make_async_remote_copy(
                    src_ref=src, dst_ref=dst, send_sem=ssem.at[3 * t + j],
                    recv_sem=rsem.at[3 * t + j], device_id=(px, py, c), device_id_type=MESH)
                cp.wait_send()
                cp.wait_recv()

    after = list(after) if isinstance(after, (list, tuple)) else [after]
    outs = pl.pallas_call(
        body, name=name,
        out_shape=[pltpu.HBM(a.shape, a.dtype) for a in list(shards) + list(zones)],
        in_specs=[_HBM_SPEC] * (2 * m) + [_SEM_SPEC, _SEM_SPEC] + [pl.BlockSpec(memory_space=pl.ANY)] * len(after),
        out_specs=[_HBM_SPEC] * (2 * m),
        input_output_aliases={i: i for i in range(2 * m)},
        compiler_params=pltpu.CompilerParams(has_side_effects=_EFFECT),
    )(*shards, *zones, send_sems, recv_sems, *after)
    return outs[m:]


def _relay_halves(zones, name):
    n = len(zones)

    def slot_half(ref, k, half):
        h = ref.shape[1] // 2
        return ref.at[k, pl.ds(pl.multiple_of(half * h, 16), h)]

    def body(*refs):
        ins, outs = refs[:n], refs[n:2 * n]
        send_sems, recv_sems = refs[2 * n:]
        x, y, c = _place()
        sends = []
        for a in range(n):
            for j, (dx, dy) in enumerate(_CHIP_FLIPS):
                kp = 2 * _flip(x, dx) + _flip(y, dy)
                cp = pltpu.make_async_remote_copy(
                    src_ref=slot_half(ins[a], kp, c), dst_ref=slot_half(outs[a], kp, c),
                    send_sem=send_sems.at[3 * a + j], recv_sem=recv_sems.at[3 * a + j],
                    device_id=(x, y, 1 - c), device_id_type=MESH)
                cp.start()
                sends.append(cp)
        for a in range(n):
            for j, (dx, dy) in enumerate(_CHIP_FLIPS):
                kp = 2 * _flip(x, dx) + _flip(y, dy)
                pltpu.make_async_remote_copy(
                    src_ref=slot_half(ins[a], kp, c), dst_ref=slot_half(outs[a], kp, 1 - c),
                    send_sem=send_sems.at[3 * a + j], recv_sem=recv_sems.at[3 * a + j],
                    device_id=(x, y, 1 - c), device_id_type=MESH).wait_recv()
        for cp in sends:
            cp.wait_send()

    any_spec = pl.BlockSpec(memory_space=pl.ANY)
    return pl.pallas_call(
        body, name=name,
        out_shape=[jax.ShapeDtypeStruct(z.shape, z.dtype) for z in zones],
        in_specs=[any_spec] * n, out_specs=[any_spec] * n,
        input_output_aliases={a: a for a in range(n)},
        scratch_shapes=[pltpu.SemaphoreType.DMA((3 * n,)), pltpu.SemaphoreType.DMA((3 * n,))],
    )(*zones)


def _scatter_start(arrs, name):
    n = len(arrs)

    def body(*refs):
        ins, lands = refs[:n], refs[n:2 * n]
        ssem, rsem = refs[2 * n], refs[2 * n + 1]
        token = refs[2 * n + 2 + 2 * n]
        x, y, c = _place()
        for a in range(n):
            for j, (dx, dy) in enumerate(_CHIP_FLIPS):
                px, py = _flip(x, dx), _flip(y, dy)
                pltpu.make_async_remote_copy(
                    src_ref=ins[a].at[2 * px + py], dst_ref=lands[a].at[j], send_sem=ssem.at[3 * a + j],
                    recv_sem=rsem.at[3 * a + j], device_id=(px, py, c), device_id_type=MESH).start()
        token[...] = jnp.zeros_like(token)

    lands = [lax.empty((3,) + a.shape[1:], a.dtype) for a in arrs]
    outs = pl.pallas_call(
        body, name=name,
        out_shape=[pltpu.SemaphoreType.DMA((3 * n,))] * 2 + [pltpu.HBM(a.shape, a.dtype) for a in arrs]
        + [pltpu.HBM(z.shape, z.dtype) for z in lands] + [jax.ShapeDtypeStruct((8, 128), F32)],
        in_specs=[_HBM_SPEC] * (2 * n),
        out_specs=[_SEM_SPEC] * 2 + [_HBM_SPEC] * (2 * n) + [pl.BlockSpec(memory_space=pltpu.VMEM)],
        input_output_aliases={i: 2 + i for i in range(2 * n)},
        compiler_params=pltpu.CompilerParams(has_side_effects=_EFFECT),
    )(*[_in_hbm(a) for a in arrs], *[_in_hbm(z) for z in lands])
    return outs[0], outs[1], outs[2:2 + n], outs[2 + n:2 + 2 * n], outs[-1]


def _scatter_wait(arrs, lands, send_sems, recv_sems, after, name):
    n = len(arrs)

    def body(*refs):
        ins, lz = refs[:n], refs[n:2 * n]
        ssem, rsem = refs[2 * n], refs[2 * n + 1]
        x, y, c = _place()
        for a in range(n):
            for j, (dx, dy) in enumerate(_CHIP_FLIPS):
                px, py = _flip(x, dx), _flip(y, dy)
                cp = pltpu.make_async_remote_copy(
                    src_ref=ins[a].at[2 * px + py], dst_ref=lz[a].at[j], send_sem=ssem.at[3 * a + j],
                    recv_sem=rsem.at[3 * a + j], device_id=(px, py, c), device_id_type=MESH)
                cp.wait_send()
                cp.wait_recv()

    outs = pl.pallas_call(
        body, name=name,
        out_shape=[pltpu.HBM(a.shape, a.dtype) for a in list(arrs) + list(lands)],
        in_specs=[_HBM_SPEC] * (2 * n) + [_SEM_SPEC, _SEM_SPEC, pl.BlockSpec(memory_space=pl.ANY)],
        out_specs=[_HBM_SPEC] * (2 * n),
        input_output_aliases={i: i for i in range(2 * n)},
        compiler_params=pltpu.CompilerParams(has_side_effects=_EFFECT),
    )(*arrs, *lands, send_sems, recv_sems, after)
    return outs[:n], outs[n:]


def _swap_start(arrs, name):
    n = len(arrs)

    def body(*refs):
        ins, lands = refs[:n], refs[n:2 * n]
        ssem, rsem = refs[2 * n], refs[2 * n + 1]
        token = refs[2 * n + 2 + 2 * n]
        x, y, c = _place()
        for a in range(n):
            pltpu.make_async_remote_copy(src_ref=ins[a], dst_ref=lands[a], send_sem=ssem.at[a], recv_sem=rsem.at[a],
                                         device_id=(x, y, 1 - c), device_id_type=MESH).start()
        token[...] = jnp.zeros_like(token)

    lands = [lax.empty(a.shape, a.dtype) for a in arrs]
    outs = pl.pallas_call(
        body, name=name,
        out_shape=[pltpu.SemaphoreType.DMA((n,))] * 2 + [pltpu.HBM(a.shape, a.dtype) for a in arrs]
        + [pltpu.HBM(z.shape, z.dtype) for z in lands] + [jax.ShapeDtypeStruct((8, 128), F32)],
        in_specs=[_HBM_SPEC] * (2 * n),
        out_specs=[_SEM_SPEC] * 2 + [_HBM_SPEC] * (2 * n) + [pl.BlockSpec(memory_space=pltpu.VMEM)],
        input_output_aliases={i: 2 + i for i in range(2 * n)},
        compiler_params=pltpu.CompilerParams(has_side_effects=_EFFECT),
    )(*[_in_hbm(a) for a in arrs], *[_in_hbm(z) for z in lands])
    return outs[0], outs[1], outs[2:2 + n], outs[2 + n:2 + 2 * n], outs[-1]


def _swap_wait(arrs, lands, send_sems, recv_sems, after, name):
    n = len(arrs)

    def body(*refs):
        ins, lz = refs[:n], refs[n:2 * n]
        ssem, rsem = refs[2 * n], refs[2 * n + 1]
        x, y, c = _place()
        for a in range(n):
            cp = pltpu.make_async_remote_copy(src_ref=ins[a], dst_ref=lz[a], send_sem=ssem.at[a], recv_sem=rsem.at[a],
                                              device_id=(x, y, 1 - c), device_id_type=MESH)
            cp.wait_send()
            cp.wait_recv()

    outs = pl.pallas_call(
        body, name=name,
        out_shape=[pltpu.HBM(a.shape, a.dtype) for a in list(arrs) + list(lands)],
        in_specs=[_HBM_SPEC] * (2 * n) + [_SEM_SPEC, _SEM_SPEC, pl.BlockSpec(memory_space=pl.ANY)],
        out_specs=[_HBM_SPEC] * (2 * n),
        input_output_aliases={i: i for i in range(2 * n)},
        compiler_params=pltpu.CompilerParams(has_side_effects=_EFFECT),
    )(*arrs, *lands, send_sems, recv_sems, after)
    return outs[:n], outs[n:]


def _row_tile(rows, cols):
    for tr in (1024, 512, 256, 128, 64, 32, 16, 8):
        if rows % tr == 0 and tr * cols * 4 <= (1 << 20):
            return tr
    return rows


def _partial_sum(g_full, recv, k_idx, name):
    _, r, c = g_full.shape
    tr = _row_tile(r, c)

    def body(k_ref, g_ref, r_ref, o_ref):
        del k_ref
        acc = g_ref[0].astype(F32)
        for j in range(3):
            acc = acc + r_ref[j].astype(F32)
        o_ref[...] = acc

    return pl.pallas_call(
        body, name=name,
        grid_spec=pltpu.PrefetchScalarGridSpec(
            num_scalar_prefetch=1, grid=(r // tr,),
            in_specs=[pl.BlockSpec((1, tr, c), lambda i, k: (k[0], i, 0)),
                      pl.BlockSpec((3, tr, c), lambda i, k: (0, i, 0))],
            out_specs=pl.BlockSpec((tr, c), lambda i, k: (i, 0))),
        out_shape=jax.ShapeDtypeStruct((r, c), F32),
        compiler_params=_cp("parallel"),
    )(k_idx, g_full, recv)


def _adamw(w3, parts, m3, v3, layer, prev, name):
    n_l, r, c = w3.shape
    tr = _row_tile(r, c)
    n_i = r // tr
    n_p = len(parts)
    c1 = 1.0 - ADAM_B1 ** ADAM_STEP
    c2 = 1.0 - ADAM_B2 ** ADAM_STEP
    stacked = [isinstance(p, tuple) for p in parts]

    def body(*refs):
        w_ref, m_ref, v_ref = refs[0:3]
        g_refs = refs[3:3 + n_p]
        go_ref, d_ref, mo_ref, vo_ref = refs[-4:]
        g = None
        for p in range(n_p):
            term = g_refs[p][0] if stacked[p] else g_refs[p][...]
            g = term if g is None else g + term
        w = w_ref[0]
        m = ADAM_B1 * m_ref[0] + (1.0 - ADAM_B1) * g
        v = ADAM_B2 * v_ref[0] + (1.0 - ADAM_B2) * (g * g)
        m_hat = m / c1
        v_hat = v / c2
        go_ref[0] = g
        d_ref[0] = -ADAM_LR * (m_hat / (jnp.sqrt(v_hat) + ADAM_EPS) + ADAM_WD * w)
        mo_ref[0] = m
        vo_ref[0] = v

    blk = pl.BlockSpec((1, tr, c), lambda i: (layer, i, 0))
    in_specs = [blk, blk, blk]
    args = [w3, m3, v3]
    for part in parts:
        if isinstance(part, tuple):
            in_specs.append(pl.BlockSpec((1, tr, c), functools.partial(lambda idx, i: (idx, i, 0), part[1])))
            args.append(part[0])
        else:
            in_specs.append(pl.BlockSpec((tr, c), lambda i: (i, 0)))
            args.append(part)
    aliases = {}
    if prev is not None:
        in_specs += [pl.BlockSpec(memory_space=pl.ANY)] * 4
        aliases = {len(args) + q: q for q in range(4)}
        args += list(prev)
    shp = jax.ShapeDtypeStruct((n_l, r, c), F32)
    return pl.pallas_call(
        body, name=name, grid=(n_i,), in_specs=in_specs, out_specs=[blk] * 4, out_shape=[shp] * 4,
        input_output_aliases=aliases, compiler_params=_cp("parallel"),
    )(*args)


_SMALL_W = 4096
_PACK_ROWS = 352
_N9 = N_MOD * D_MODEL


def _flat_pad(parts, total):
    flat = jnp.concatenate([p.reshape(-1) for p in parts])
    return jnp.concatenate([flat, jnp.zeros((total - flat.shape[0],), F32)])


def kernel(x, c, ctx, c_ctx, w_mod, b_mod, norm_g, ffn1_wi, ffn1_wo, ffn2_wi, ffn2_wo, w_in, w_a2_f, b_a_f, w_a2_b, b_a_b, sink, gla_g, w_out, w_pool, pool_scale, final_g, loss_target, m_c_ctx, m_w_mod, m_b_mod, m_norm_g, m_ffn1_wi, m_ffn1_wo, m_ffn2_wi, m_ffn2_wo, m_w_in, m_w_a2_f, m_b_a_f, m_w_a2_b, m_b_a_b, m_sink, m_gla_g, m_w_out, m_w_pool, m_pool_scale, m_final_g, v_c_ctx, v_w_mod, v_b_mod, v_norm_g, v_ffn1_wi, v_ffn1_wo, v_ffn2_wi, v_ffn2_wo, v_w_in, v_w_a2_f, v_b_a_f, v_w_a2_b, v_b_a_b, v_sink, v_gla_g, v_w_out, v_w_pool, v_pool_scale, v_final_g):
    t_len, l_ctx = x.shape[1], ctx.shape[1]
    tm = ROW_TILE
    pad = (-(t_len + l_ctx)) % tm
    rows0 = t_len + l_ctx + pad
    n_x = t_len // tm
    xi, yi, ci = _place()
    k_me = 2 * xi + yi
    me = 4 * xi + 2 * yi + ci
    mod_cols = w_mod.shape[2]
    n_grp = len(POOL_WINDOWS)

    small_w = _flat_pad([norm_g, w_a2_f, w_a2_b, pool_scale], _SMALL_W).reshape(_SMALL_W // 128, 128)
    shards = [ffn1_wi[0], ffn1_wi[1], ffn1_wo[0], ffn1_wo[1], ffn2_wi[0], ffn2_wi[1], ffn2_wo[0], ffn2_wo[1],
              w_in[0], w_out[0], w_pool[0].reshape(n_grp * w_pool.shape[2], POOL_GROUP)]

    send_src = [s.astype(BF16) for s in shards] + [small_w]
    groups = ([11, 0], [2], [8, 9], [4], [6], [1], [3], [10, 5], [7])
    started = {}

    def gather_start(gs, after):
        members, index, pos = [], [], 0
        for g in gs:
            members += groups[g]
            index.append(tuple(range(pos, pos + len(groups[g]))))
            pos += len(groups[g])
        sems, thru, zones, token = _gather_start([send_src[a] for a in members], tuple(index), after,
                                                 "gather_start_%d" % gs[0], halves=gs == two_level)
        for k, (g, idx) in enumerate(zip(gs, index)):
            started[g] = (sems[k], [thru[i] for i in idx], [zones[i] for i in idx])
        return token

    def gather_wait(g, after):
        (ssem, rsem), thru, zones = started[g]
        got = _gather_wait(thru, zones, ssem, rsem, after, "gather_wait_%d" % g, halves=(g,) == two_level)
        if (g,) == two_level:
            got = _relay_halves(got, "gather_relay_%d" % g)
        return dict(zip(groups[g], got))

    two_level = (0,)

    c_all = _allgather_small(c.reshape(8, 128), "gather_cond").reshape(N_DEV, D_MODEL)
    tok = gather_start((0,), c_all)
    c16 = jnp.concatenate([c_all, c_ctx[None], jnp.zeros((_CROWS - N_DEV - 1, D_MODEL), F32)], axis=0) + tok[0:1, 0:1]
    bias_k = lax.dynamic_slice(b_mod, (0, k_me * mod_cols), (2, mod_cols)).reshape(2, 1, mod_cols)
    mm_k = _adaln_fwd(c16, w_mod, bias_k, "adaln_fwd")
    cs = _rope_tables(t_len, rows0)
    xcat = jnp.concatenate([x[0], ctx[0], jnp.zeros((pad, D_MODEL), F32)], axis=0)
    mm_all = _allgather_small(mm_k.reshape(-1, 128), "gather_mod", (cs, xcat)).reshape(N_DEV, 2, _CROWS, mod_cols)
    mm_full = jnp.concatenate([mm_all[2 * k] for k in range(N_CHIPS)], axis=-1)
    mm_x = lax.dynamic_index_in_dim(mm_full, me, axis=1, keepdims=False)
    mm_c = mm_full[:, N_DEV]
    mods = [jnp.stack([mm_x[l].reshape(N_MOD, D_MODEL), mm_c[l].reshape(N_MOD, D_MODEL)]) for l in range(2)]
    gathered = gather_wait(0, mods[0])
    sw = gathered[11].reshape(N_CHIPS, _SMALL_W)
    ng_n = norm_g.size
    a2_n = w_a2_f.size
    norm_g_full = jnp.concatenate([sw[k, :ng_n].reshape(norm_g.shape) for k in range(N_CHIPS)], axis=-1)
    w_a2_f_full = jnp.concatenate([sw[k, ng_n:ng_n + a2_n].reshape(w_a2_f.shape[1:]) for k in range(N_CHIPS)], axis=-1)
    w_a2_b_full = jnp.concatenate(
        [sw[k, ng_n + a2_n:ng_n + 2 * a2_n].reshape(w_a2_b.shape[1:]) for k in range(N_CHIPS)], axis=-1)
    pscale_full = jnp.concatenate(
        [sw[k, ng_n + 2 * a2_n:ng_n + 2 * a2_n + pool_scale.size] for k in range(N_CHIPS)]).reshape(1, D_MODEL)
    wg2, bias2 = _gate_weights(w_a2_f_full, b_a_f[0], w_a2_b_full, b_a_b[0])
    gla_g2 = gla_g.reshape(1, B_DV)
    final_g2 = final_g.reshape(1, D_MODEL)

    g3 = [norm_g_full[0], norm_g_full[1]]

    w1i, w1o, w2i, w2o = [None, None], [None, None], [None, None], [None, None]
    w1i[0] = gathered[0]
    mods_a = mods[0] + gather_start((1, 2), w1i[0])[0:1, 0:1]
    x1, sv_a1, w1o[0] = _ffn_forward(xcat, g3[0], mods_a, 0, w1i[0],
                                     lambda s: (gather_wait(1, s)[2], gather_start((3, 4), s)), n_x, "l0_ffn1")
    gathered = gather_wait(2, x1)
    w_in_full = jnp.concatenate([gathered[8][k] for k in range(N_CHIPS)], axis=1)
    wcat = _w_in_to_cat(w_in_full)
    w_out_full = gathered[9].reshape(D_MODEL, D_MODEL)
    pace_groups = {"proj": (5, 6), "attn": (7, 8)}
    no_dep = jnp.zeros((8, 128), F32)
    x2, sv_am = _mixer_ab_forward(
        x1, g3[0], mods[0], wcat, wg2, bias2, sink[0], gla_g2, w_out_full, cs, t_len, l_ctx, n_x,
        lambda tag, res_: gather_start(pace_groups[tag], res_) if tag in pace_groups else no_dep)
    mods_a = mods[0]
    w2i[0], w2o[0] = gather_wait(3, x2)[4], gather_wait(4, x2)[6]
    x3, sv_a2, _ = _ffn_forward(x2, g3[0], mods_a, 2, w2i[0], lambda s: (w2o[0], None), n_x, "l0_ffn2")
    w1i[1], w1o[1] = gather_wait(5, x3)[1], gather_wait(6, x3)[3]
    x4, sv_b1, _ = _ffn_forward(x3, g3[1], mods[1], 0, w1i[1], lambda s: (w1o[1], None), n_x, "l1_ffn1")
    gathered = gather_wait(7, x4)
    w2i[1] = gathered[5]
    wp_full = gathered[10].reshape(N_CHIPS, n_grp, -1, POOL_GROUP).transpose(1, 0, 2, 3).reshape(
        n_grp, POOL_GROUP, POOL_GROUP)
    x5, sv_bm = _mixer_pool_forward(x4, g3[1], mods[1], wp_full, pscale_full, t_len)
    x6, sv_b2, w2o[1] = _ffn_forward(x5, g3[1], mods[1], 2, w2i[1], lambda s: (gather_wait(8, s)[7], None), n_x,
                                     "l1_ffn2")
    dx6, loss_part, d_final_g = _final_loss(x6, final_g2, loss_target[0], "final_loss")
    loss = lax.psum(loss_part[0, 0], ("x", "y", "c"))

    sent = []

    def sender(weight, layer):
        def send(grads):
            nm = "%s_%d" % (weight or "_".join(grads), layer)
            ssem, rsem, thru, lands, token = _scatter_start(list(grads.values()), "scatter_start_" + nm)
            targets = [((weight + "_" + tag) if weight else tag, layer) for tag in grads]
            sent.append((nm, targets, thru, lands, ssem, rsem))
            return token[0:1, 0:1]
        return send

    dx5, st_b2, dg_b2 = _ffn_backward(dx6, sv_b2, g3[1], mods[1], 2, w2i[1], w2o[1], n_x, sender("ffn2", 1),
                                      "l1_ffn2_b")
    dx4, st_bm, dg_bm, d_pscale, d_wp = _mixer_pool_backward(dx5, sv_bm, g3[1], mods[1], wp_full, pscale_full, t_len)
    d_wp4 = d_wp.reshape(n_grp, N_CHIPS, -1, POOL_GROUP).transpose(1, 0, 2, 3).reshape(N_CHIPS, -1, POOL_GROUP)
    mods1 = mods[1] + sender("", 0)({"w_pool": d_wp4})
    dx3, st_b1, dg_b1 = _ffn_backward(dx4, sv_b1, g3[1], mods1, 0, w1i[1], w1o[1], n_x, sender("ffn1", 1),
                                      "l1_ffn1_b")
    dx2, st_a2, dg_a2 = _ffn_backward(dx3, sv_a2, g3[0], mods[0], 2, w2i[0], w2o[0], n_x, sender("ffn2", 0),
                                      "l0_ffn2_b")
    dx1, st_am, dg_am, d_wcat, d_wg2, d_bias2, d_sink, d_glag, d_wout = _mixer_ab_backward(
        dx2, sv_am, g3[0], mods[0], wcat, wg2, bias2, sink[0], gla_g2, w_out_full, cs, t_len, l_ctx, n_x)
    d_w_in4 = _cat_to_w_in(d_wcat).reshape(D_MODEL, N_CHIPS, -1).transpose(1, 0, 2)
    mods0 = mods[0] + sender("", 0)({"w_in": d_w_in4, "w_out": d_wout.reshape(N_CHIPS, -1, D_MODEL)})
    dx0, st_a1, dg_a1 = _ffn_backward(dx1, sv_a1, g3[0], mods0, 0, w1i[0], w1o[0], n_x, sender("ffn1", 0),
                                      "l0_ffn1_b", out_tiles=n_x)
    grad_x = dx0[None]

    def as3(a):
        n_l = a.shape[0] if a.ndim == 3 else 1
        return a.reshape(n_l, -1, a.shape[-1])

    res = {}
    big_w = {"ffn1_wi": (ffn1_wi, m_ffn1_wi, v_ffn1_wi), "ffn1_wo": (ffn1_wo, m_ffn1_wo, v_ffn1_wo),
             "ffn2_wi": (ffn2_wi, m_ffn2_wi, v_ffn2_wi), "ffn2_wo": (ffn2_wo, m_ffn2_wo, v_ffn2_wo),
             "w_in": (w_in, m_w_in, v_w_in), "w_out": (w_out, m_w_out, v_w_out), "w_pool": (w_pool, m_w_pool, v_w_pool)}
    k_idx = k_me.reshape(1).astype(jnp.int32)
    chain = dx0
    def finish(swap, after):
        lo, targets, s_sem, r_sem, s_thru, s_lands = swap
        mine, other = _swap_wait(s_thru, s_lands, s_sem, r_sem, after, "swap_wait_%d" % lo)
        last = after
        for (wname, layer), p, q in zip(targets, mine, other):
            w, m, v = big_w[wname]
            res[wname] = _adamw(as3(w), [p, q], as3(m), as3(v), layer, res.get(wname),
                                "adamw_%s_%d" % (wname, layer))
            last = res[wname][3]
        return last

    swap = None
    for lo, hi in ((0, 1), (1, 3), (3, 4), (4, 5), (5, 6)):
        partial, targets = [], []
        for nm, sent_targets, thru, lands, ssem, rsem in sent[lo:hi]:
            mine, recv = _scatter_wait(thru, lands, ssem, rsem, chain, "scatter_wait_" + nm)
            for k, (wname, layer) in enumerate(sent_targets):
                partial.append(_partial_sum(mine[k], recv[k], k_idx, "partial_sum_%s_%d" % (wname, layer)))
                targets.append((wname, layer))
        s_sem, r_sem, s_thru, s_lands, token = _swap_start(partial, "swap_start_%d" % lo)
        if swap is not None:
            chain = finish(swap, token)
        swap = (lo, targets, s_sem, r_sem, s_thru, s_lands)

    def mod_row(st1, dg1, stm, dgm, st2, dg2, s):
        return jnp.concatenate([st1[s, 0], st1[s, 1], dg1[s, 0], stm[s, 0], stm[s, 1], dgm[s, 0],
                                st2[s, 0], st2[s, 1], dg2[s, 0]])

    dg_bm2 = jnp.concatenate([dg_bm, jnp.zeros_like(dg_bm)], axis=0)[:, None, :]
    d_mm_x0 = mod_row(st_a1, dg_a1, st_am, dg_am, st_a2, dg_a2, 0)
    d_mm_x1 = mod_row(st_b1, dg_b1, st_bm, dg_bm2, st_b2, dg_b2, 0)
    d_mm_c0 = mod_row(st_a1, dg_a1, st_am, dg_am, st_a2, dg_a2, 1)
    d_norm_g = jnp.stack([jnp.stack([st[0, 2] + st[1, 2] for st in (st_a1, st_am, st_a2)]),
                          jnp.stack([st[0, 2] + st[1, 2] for st in (st_b1, st_bm, st_b2)])])
    rk = B_GATE_RANK
    pack = _flat_pad([d_mm_x0, d_mm_x1, d_mm_c0, d_norm_g, d_bias2, d_wg2[0:rk, 0:256], d_wg2[rk:2 * rk, 256:512],
                      d_sink[:, 0], jnp.zeros((120,), F32), d_glag, d_pscale, d_final_g],
                     _PACK_ROWS * 128).reshape(_PACK_ROWS, 128)
    pack = pack + 0.0 * chain[0, 0:1, 0:1]
    pack_all = _allgather_small(pack, "gather_small_grads")
    tot = _sum_devices(pack_all, "sum_small_grads").reshape(-1)
    rows_all = pack_all.reshape(N_DEV, -1)
    o = 3 * _N9
    g_norm_g_full = tot[o:o + 6 * D_MODEL].reshape(2, 3, D_MODEL)
    o += 6 * D_MODEL
    g_bias2 = tot[o:o + 512]
    o += 512
    g_w_a2_f_full = tot[o:o + rk * 256].reshape(rk, 256)
    o += rk * 256
    g_w_a2_b_full = tot[o:o + rk * 256].reshape(rk, 256)
    o += rk * 256
    g_sink = tot[o:o + A_HEADS]
    o += 128
    g_gla_g = tot[o:o + B_DV]
    o += B_DV
    g_pscale_full = tot[o:o + D_MODEL]
    o += D_MODEL
    g_final_g = tot[o:o + D_MODEL]
    d_mmc_tot = tot[2 * _N9:3 * _N9]
    g_b_mod = jnp.stack([tot[0:_N9] + d_mmc_tot, tot[_N9:2 * _N9]])

    zrows = jnp.zeros((_CROWS - N_DEV - 1, _N9), F32)
    d16 = jnp.stack([jnp.concatenate([rows_all[:, 0:_N9], d_mmc_tot[None], zrows], axis=0),
                     jnp.concatenate([rows_all[:, _N9:2 * _N9], jnp.zeros((1, _N9), F32), zrows], axis=0)])
    d16_k = lax.dynamic_slice(d16, (0, 0, k_me * mod_cols), (2, _CROWS, mod_cols))
    dmmc_k = lax.dynamic_slice(d_mmc_tot, (k_me * mod_cols,), (mod_cols,)).reshape(1, mod_cols)
    g_w_mod, c_part = _adaln_bwd(c16, d16_k, w_mod, dmmc_k, "adaln_bwd")
    c_parts = _allgather_small(c_part.reshape(8, 128), "gather_cctx")
    g_c_ctx = _cctx_grad(c_parts, c_ctx.reshape(8, 128), "cctx_grad").reshape(D_MODEL)

    def small(w, g, m, v, shape3, nm):
        return [o_.reshape(w.shape) for o_ in _adamw(w.reshape(shape3), [g.reshape(shape3[1:])],
                                                    m.reshape(shape3), v.reshape(shape3), 0, None, "adamw_" + nm)]

    def own(a, axis, size):
        return lax.dynamic_slice_in_dim(a, k_me * size, size, axis=axis)

    res["c_ctx"] = small(c_ctx, g_c_ctx, m_c_ctx, v_c_ctx, (1, 8, 128), "c_ctx")
    upd = _adamw(w_mod, [(g_w_mod, 1)], m_w_mod, v_w_mod, 1, None, "adamw_w_mod_1")
    res["w_mod"] = _adamw(w_mod, [(g_w_mod, 0)], m_w_mod, v_w_mod, 0, upd, "adamw_w_mod_0")
    res["b_mod"] = small(b_mod, g_b_mod, m_b_mod, v_b_mod, (1, 2, _N9), "b_mod")
    res["norm_g"] = small(norm_g, own(g_norm_g_full, 2, norm_g.shape[2]), m_norm_g, v_norm_g,
                          (1, 6, norm_g.shape[2]), "norm_g")
    res["w_a2_f"] = small(w_a2_f, own(g_w_a2_f_full, 1, w_a2_f.shape[2]), m_w_a2_f, v_w_a2_f,
                          (1, rk, w_a2_f.shape[2]), "w_a2_f")
    res["b_a_f"] = small(b_a_f, g_bias2[0:256], m_b_a_f, v_b_a_f, (1, 1, 256), "b_a_f")
    res["w_a2_b"] = small(w_a2_b, own(g_w_a2_b_full, 1, w_a2_b.shape[2]), m_w_a2_b, v_w_a2_b,
                          (1, rk, w_a2_b.shape[2]), "w_a2_b")
    res["b_a_b"] = small(b_a_b, g_bias2[256:512], m_b_a_b, v_b_a_b, (1, 1, 256), "b_a_b")
    res["sink"] = small(sink, g_sink, m_sink, v_sink, (1, 1, A_HEADS), "sink")
    res["gla_g"] = small(gla_g, g_gla_g, m_gla_g, v_gla_g, (1, 1, B_DV), "gla_g")
    res["pool_scale"] = small(pool_scale, own(g_pscale_full, 0, pool_scale.shape[1]), m_pool_scale, v_pool_scale,
                              (1, 1, pool_scale.shape[1]), "pool_scale")
    res["final_g"] = small(final_g, g_final_g, m_final_g, v_final_g, (1, 8, 128), "final_g")
    finish(swap, res["final_g"][0])
    for wname, (w, _, _) in big_w.items():
        res[wname] = [o_.reshape(w.shape) for o_ in res[wname]]

    names = ["c_ctx", "w_mod", "b_mod", "norm_g", "ffn1_wi", "ffn1_wo", "ffn2_wi", "ffn2_wo", "w_in", "w_a2_f",
             "b_a_f", "w_a2_b", "b_a_b", "sink", "gla_g", "w_out", "w_pool", "pool_scale", "final_g"]
    outs = [loss, grad_x]
    for field in range(4):
        outs += [res[nm][field] for nm in names]
    return tuple(outs)
```
